```python
import math
import jax, jax.numpy as jnp
from jax import lax
import numpy as np

D_MODEL = 1024
BATCH = 16
SEQ = 256
DEPTH = 2
DEC_BATCH = 4
DEC_SEQ = 1024
PAST_LEN = 512

GRID_W = 64
EPS = 1e-6
ROPE_THETA = 10000.0
ROPE_DIM = 32
H_A = 4
DK_A = 32
DV_A = 64
GLA_LR = 16
GLA_TAU = 16.0
GLA_CHUNK = 64
H_B = 8
DN_B = 64
DR_B = ROPE_DIM
DV_B = 64
Q_LORA = 256
KV_LORA = 128
H_C = 4
DC = ROPE_DIM
W_A = H_A * DV_A
W_B = H_B * DV_B
W_C = H_C * 2 * DC
D_MIX = W_A + W_B + W_C
IN_SIZES = (H_A * DK_A, H_A * DK_A, W_A, 2 * GLA_LR, W_A,
            Q_LORA, KV_LORA, DR_B, W_B,
            H_C * 2 * DC, H_C * 2 * DC, W_C, W_C)
N_IN = sum(IN_SIZES)
Q_BLOCK = 128
DENSE_KEY_LIMIT = 2048

kernel_name = "hybrid_gla_mla_diff_diffusion_step"


def rms_norm(x, g):
    x32 = x.astype(jnp.float32)
    y = x32 * lax.rsqrt(jnp.mean(x32 * x32, axis=-1, keepdims=True) + EPS)
    return (y * g.astype(jnp.float32)).astype(x.dtype)


def split_cols(t, sizes):
    idx, acc = [], 0
    for s in sizes[:-1]:
        acc += s
        idx.append(acc)
    return jnp.split(t, idx, axis=-1)


def to_heads(t, h):
    b, n, w = t.shape
    return t.reshape(b, n, h, w // h).transpose(0, 2, 1, 3)


def from_heads(t):
    b, h, n, d = t.shape
    return t.transpose(0, 2, 1, 3).reshape(b, n, h * d)


def axial_rope_tables(n, rot_dim):
    n_rows = n // GRID_W
    row = jnp.repeat(jnp.arange(n_rows), GRID_W).astype(jnp.float32)
    col = jnp.tile(jnp.arange(GRID_W), n_rows).astype(jnp.float32)
    half = rot_dim // 2
    inv = 1.0 / (ROPE_THETA ** (jnp.arange(0, half, 2, dtype=jnp.float32) / half))
    ar = row[:, None] * inv
    ac = col[:, None] * inv
    ang = jnp.concatenate([ar, ar, ac, ac], axis=-1)
    return jnp.cos(ang), jnp.sin(ang)


def _rot_half(z):
    h = z.shape[-1] // 2
    return jnp.concatenate([-z[..., h:], z[..., :h]], axis=-1)


def apply_axial_rope(x, rope):
    cos, sin = rope
    x32 = x.astype(jnp.float32)
    half = x.shape[-1] // 2
    rot = jnp.concatenate([_rot_half(x32[..., :half]), _rot_half(x32[..., half:])], axis=-1)
    return (x32 * cos + rot * sin).astype(x.dtype)


def attend(q, k, v, scale):
    b, h, sq, d = q.shape
    sk, dv = k.shape[2], v.shape[-1]
    k32, v32 = k.astype(jnp.float32), v.astype(jnp.float32)

    def block(qb):
        s = jnp.einsum('bhqd,bhkd->bhqk', qb.astype(jnp.float32), k32) * scale
        p = jax.nn.softmax(s, axis=-1)
        return jnp.einsum('bhqk,bhkv->bhqv', p, v32)

    if sk >= DENSE_KEY_LIMIT and sq % Q_BLOCK == 0:
        qb = q.reshape(b, h, sq // Q_BLOCK, Q_BLOCK, d).transpose(2, 0, 1, 3, 4)
        out = lax.map(block, qb).transpose(1, 2, 0, 3, 4).reshape(b, h, sq, dv)
    else:
        out = block(q)
    return out.astype(q.dtype)


def gla_chunked(q, k, v, log_a, s0):
    b, h, t, dk = q.shape
    dv = v.shape[-1]
    c = GLA_CHUNK
    n = t // c
    f32 = jnp.float32
    q = q.astype(f32).reshape(b, h, n, c, dk)
    k = k.astype(f32).reshape(b, h, n, c, dk)
    v32 = v.astype(f32).reshape(b, h, n, c, dv)
    bcum = jnp.cumsum(log_a.astype(f32).reshape(b, h, n, c, dk), axis=3)
    diff = bcum[..., :, None, :] - bcum[..., None, :, :]
    mask = jnp.tril(jnp.ones((c, c), dtype=bool))[:, :, None]
    decay = jnp.where(mask, jnp.exp(jnp.minimum(diff, 0.0)), 0.0)
    attn = jnp.einsum('bhnid,bhnjd,bhnijd->bhnij', q, k, decay)
    o_intra = jnp.einsum('bhnij,bhnjv->bhniv', attn, v32)
    b_last = bcum[..., -1:, :]
    k_dec = k * jnp.exp(b_last - bcum)
    chunk_state = jnp.einsum('bhncd,bhncv->bhndv', k_dec, v32)
    chunk_decay = jnp.exp(b_last[..., 0, :])

    def step(s, inp):
        d, u = inp
        return d[..., None] * s + u, s

    s_fin, s_prev = lax.scan(step, s0.astype(f32),
                             (jnp.moveaxis(chunk_decay, 2, 0), jnp.moveaxis(chunk_state, 2, 0)))
    s_prev = jnp.moveaxis(s_prev, 0, 2)
    o_inter = jnp.einsum('bhncd,bhndv->bhncv', q * jnp.exp(bcum), s_prev)
    o = (o_intra + o_inter).reshape(b, h, t, dv)
    return o.astype(v.dtype), s_fin.astype(v.dtype)


def gla_bidir(q, k, v, la_f, la_b, s0_f, s0_b):
    o_f, s_f = gla_chunked(q, k, v, la_f, s0_f)
    fl = lambda z: jnp.flip(z, axis=2)
    o_b, s_b = gla_chunked(fl(q), fl(k), fl(v), fl(la_b), s0_b)
    return o_f + fl(o_b), s_f, s_b


def mixer_sublayer(x, mod, P, layer, ctx, rope):
    bsz, t, _ = x.shape
    shift, scale, gate = jnp.split(mod.astype(x.dtype), 3, axis=-1)
    h = rms_norm(x, P['g_pre']) * (1 + scale) + shift
    proj = h @ P['w_in']
    (gq, gk, gv, ga, gg, cq, ckv, kpe, mg, dq, dk, dvv, dg) = split_cols(proj, IN_SIZES)

    qa = to_heads(gq, H_A) * (DK_A ** -0.5)
    ka = to_heads(gk, H_A)
    va = to_heads(gv, H_A)
    ga_f, ga_b = jnp.split(ga, 2, axis=-1)
    la_f = to_heads(jax.nn.log_sigmoid((ga_f @ P['w_gla_af'] + P['b_gla_af']).astype(jnp.float32)) / GLA_TAU, H_A)
    la_b = to_heads(jax.nn.log_sigmoid((ga_b @ P['w_gla_ab'] + P['b_gla_ab']).astype(jnp.float32)) / GLA_TAU, H_A)
    if ctx is None:
        s0_f = jnp.zeros((bsz, H_A, DK_A, DV_A), x.dtype)
        s0_b = s0_f
    else:
        s0_f, s0_b = ctx['gla'][:, 0], ctx['gla'][:, 1]
    oa, s_f, s_b = gla_bidir(qa, ka, va, la_f, la_b, s0_f, s0_b)
    oa = from_heads(rms_norm(oa, P['g_gla']))

    qb = to_heads(rms_norm(cq, P['g_mla_q']) @ P['w_mla_uq'], H_B)
    q_nope, q_pe = qb[..., :DN_B], qb[..., DN_B:]
    ckv_n = rms_norm(ckv, P['g_mla_kv'])
    kvb = to_heads(ckv_n @ P['w_mla_ukv'], H_B)
    k_nope, v_b = kvb[..., :DN_B], kvb[..., DN_B:]
    k_pe = kpe[:, None]
    if rope is not None:
        q_pe = apply_axial_rope(q_pe, rope)
        k_pe = apply_axial_rope(k_pe, rope)
    q_full = jnp.concatenate([q_nope, q_pe], axis=-1)
    k_full = jnp.concatenate([k_nope, jnp.broadcast_to(k_pe, (bsz, H_B, t, DR_B))], axis=-1)
    if ctx is not None:
        sc = ctx['mla_ckv'].shape[1]
        kvc = to_heads(ctx['mla_ckv'] @ P['w_mla_ukv'], H_B)
        kpe_c = jnp.broadcast_to(ctx['mla_kpe'][:, None], (bsz, H_B, sc, DR_B))
        k_full = jnp.concatenate([k_full, jnp.concatenate([kvc[..., :DN_B], kpe_c], axis=-1)], axis=2)
        v_b = jnp.concatenate([v_b, kvc[..., DN_B:]], axis=2)
    ob = from_heads(attend(q_full, k_full, v_b, (DN_B + DR_B) ** -0.5))

    qc = to_heads(dq, H_C)
    kc = to_heads(dk, H_C)
    vc = to_heads(dvv, H_C)
    q1, q2 = qc[..., :DC], qc[..., DC:]
    if rope is not None:
        q1 = apply_axial_rope(q1, rope)
        q2 = apply_axial_rope(q2, rope)
        kc = jnp.concatenate([apply_axial_rope(kc[..., :DC], rope),
                              apply_axial_rope(kc[..., DC:], rope)], axis=-1)
    k_att, v_att = kc, vc
    if ctx is not None:
        k_att = jnp.concatenate([kc, ctx['diff_k']], axis=2)
        v_att = jnp.concatenate([vc, ctx['diff_v']], axis=2)
    lam_init = 0.8 - 0.6 * math.exp(-0.3 * layer)
    lam = (jnp.exp(jnp.sum((P['lam_q1'] * P['lam_k1']).astype(jnp.float32)))
           - jnp.exp(jnp.sum((P['lam_q2'] * P['lam_k2']).astype(jnp.float32))) + lam_init)
    o1 = attend(q1, k_att[..., :DC], v_att, DC ** -0.5)
    o2 = attend(q2, k_att[..., DC:], v_att, DC ** -0.5)
    oc = o1 - lam.astype(o1.dtype) * o2
    oc = from_heads(rms_norm(oc, P['g_diff']) * (1.0 - lam_init))

    mix = jnp.concatenate([oa * jax.nn.silu(gg), ob * jax.nn.silu(mg), oc * jax.nn.silu(dg)], axis=-1)
    out = rms_norm(mix @ P['w_out'], P['g_post'])
    x_new = x + gate * out
    if ctx is None:
        return x_new, (ckv_n, kpe, kc, vc, jnp.stack([s_f, s_b], axis=1))
    return x_new, None


def setup_inputs(seed: int = 0) -> dict:
    key = jax.random.key(seed)
    ks = jax.random.split(key, 32)
    nrm = lambda k, shape, s=1.0: jax.random.normal(k, shape, jnp.float32) * s
    gain = lambda k, shape: 1.0 + 0.01 * jax.random.normal(k, shape, jnp.float32)
    return {
        "x_prompt": nrm(ks[0], (BATCH, SEQ, D_MODEL)),
        "x_sample": nrm(ks[1], (DEC_BATCH, DEC_SEQ, D_MODEL)),
        "c": nrm(ks[2], (DEC_BATCH, D_MODEL)),
        "cache_mla_ckv": nrm(ks[3], (DEC_BATCH, DEPTH, PAST_LEN, KV_LORA)),
        "cache_mla_kpe": nrm(ks[4], (DEC_BATCH, DEPTH, PAST_LEN, ROPE_DIM)),
        "cache_diff_k": nrm(ks[5], (DEC_BATCH, DEPTH, H_C, PAST_LEN, 2 * DC)),
        "cache_diff_v": nrm(ks[6], (DEC_BATCH, DEPTH, H_C, PAST_LEN, 2 * DC)),
        "state_gla": nrm(ks[7], (DEC_BATCH, DEPTH, 2, H_A, DK_A, DV_A)),
        "c_ctx": nrm(ks[8], (D_MODEL,)),
        "w_ada": nrm(ks[9], (DEPTH, D_MODEL, 3 * D_MODEL), 0.5 * D_MODEL ** -0.5),
        "b_ada": nrm(ks[10], (DEPTH, 3 * D_MODEL), 0.01),
        "g_pre": gain(ks[11], (DEPTH, D_MODEL)),
        "g_post": gain(ks[12], (DEPTH, D_MODEL)),
        "w_in": nrm(ks[13], (DEPTH, D_MODEL, N_IN), D_MODEL ** -0.5),
        "w_gla_af": nrm(ks[14], (DEPTH, GLA_LR, H_A * DK_A), GLA_LR ** -0.5),
        "b_gla_af": nrm(ks[15], (DEPTH, H_A * DK_A), 0.1),
        "w_gla_ab": nrm(ks[16], (DEPTH, GLA_LR, H_A * DK_A), GLA_LR ** -0.5),
        "b_gla_ab": nrm(ks[17], (DEPTH, H_A * DK_A), 0.1),
        "g_gla": gain(ks[18], (DEPTH, DV_A)),
        "g_mla_q": gain(ks[19], (DEPTH, Q_LORA)),
        "w_mla_uq": nrm(ks[20], (DEPTH, Q_LORA, H_B * (DN_B + DR_B)), Q_LORA ** -0.5),
        "g_mla_kv": gain(ks[21], (DEPTH, KV_LORA)),
        "w_mla_ukv": nrm(ks[22], (DEPTH, KV_LORA, H_B * (DN_B + DV_B)), KV_LORA ** -0.5),
        "lam_q1": nrm(ks[23], (DEPTH, DC), 0.1),
        "lam_k1": nrm(ks[24], (DEPTH, DC), 0.1),
        "lam_q2": nrm(ks[25], (DEPTH, DC), 0.1),
        "lam_k2": nrm(ks[26], (DEPTH, DC), 0.1),
        "g_diff": gain(ks[27], (DEPTH, 2 * DC)),
        "w_out": nrm(ks[28], (DEPTH, D_MIX, D_MODEL), D_MIX ** -0.5),
    }


def reference(x_prompt, x_sample, c, cache_mla_ckv, cache_mla_kpe, cache_diff_k, cache_diff_v,
              state_gla, c_ctx, w_ada, b_ada, g_pre, g_post, w_in, w_gla_af, b_gla_af,
              w_gla_ab, b_gla_ab, g_gla, g_mla_q, w_mla_uq, g_mla_kv, w_mla_ukv,
              lam_q1, lam_k1, lam_q2, lam_k2, g_diff, w_out):
    rope = axial_rope_tables(x_sample.shape[1], ROPE_DIM)
    y_p, y_s = x_prompt, x_sample
    ckv_l, kpe_l, dk_l, dv_l, gla_l = [], [], [], [], []
    for l in range(DEPTH):
        P = dict(g_pre=g_pre[l], g_post=g_post[l], w_in=w_in[l],
                 w_gla_af=w_gla_af[l], b_gla_af=b_gla_af[l], w_gla_ab=w_gla_ab[l], b_gla_ab=b_gla_ab[l],
                 g_gla=g_gla[l], g_mla_q=g_mla_q[l], w_mla_uq=w_mla_uq[l], g_mla_kv=g_mla_kv[l],
                 w_mla_ukv=w_mla_ukv[l], lam_q1=lam_q1[l], lam_k1=lam_k1[l], lam_q2=lam_q2[l],
                 lam_k2=lam_k2[l], g_diff=g_diff[l], w_out=w_out[l])
        mod_ctx = (jax.nn.silu(c_ctx) @ w_ada[l] + b_ada[l])[None, None, :]
        y_p, (ckv_n, kpe, kc, vc, gst) = mixer_sublayer(y_p, mod_ctx, P, l, None, None)
        ckv_l.append(ckv_n); kpe_l.append(kpe); dk_l.append(kc); dv_l.append(vc); gla_l.append(gst)
        mod_lat = (jax.nn.silu(c) @ w_ada[l] + b_ada[l])[:, None, :]
        ctx = dict(mla_ckv=cache_mla_ckv[:, l], mla_kpe=cache_mla_kpe[:, l],
                   diff_k=cache_diff_k[:, l], diff_v=cache_diff_v[:, l], gla=state_gla[:, l])
        y_s, _ = mixer_sublayer(y_s, mod_lat, P, l, ctx, rope)
    new_mla_ckv = jnp.stack(ckv_l, axis=1)
    new_mla_kpe = jnp.stack(kpe_l, axis=1)
    new_diff_k = jnp.stack(dk_l, axis=1)
    new_diff_v = jnp.stack(dv_l, axis=1)
    new_state_gla = jnp.stack(gla_l, axis=1)
    return (y_p, y_s, new_mla_ckv, new_mla_kpe, new_diff_k, new_diff_v, new_state_gla)
```

```python
import functools
import math

import numpy as np
import jax
import jax.numpy as jnp
from jax import lax
from jax.experimental import pallas as pl
from jax.experimental.pallas import tpu as pltpu

F32 = jnp.float32
BF16 = jnp.bfloat16

D_MODEL = 1024
DEPTH = 2
GRID_W = 64
EPS = 1e-6
ROPE_THETA = 10000.0
ROPE_DIM = 32
H_A, DK_A, DV_A = 4, 32, 64
GLA_LR = 16
GLA_TAU = 16.0
GLA_CHUNK = 64
H_B, DN_B, DR_B, DV_B = 8, 64, 32, 64
Q_LORA, KV_LORA = 256, 128
H_C, DC = 4, 32
W_A, W_B, W_C = H_A * DV_A, H_B * DV_B, H_C * 2 * DC

_IN = dict(gq=0, gk=128, gv=256, ga=512, gg=544, cq=800, ckv=1056, kpe=1184, mg=1216,
           dq=1728, dk=1984, dv=2240, dg=2496)

A_GLA = 0
A_MLA = 896
A_MG = 1536
A_DIFF = 2048
N_AUG = 3584

V7X_VMEM_LIMIT_BYTES = 56 * 1024 * 1024
TOKEN_BLOCK = 512
ATT_QBLOCK = 256


def _cparams():
    return pltpu.CompilerParams(dimension_semantics=("arbitrary",),
                                vmem_limit_bytes=V7X_VMEM_LIMIT_BYTES)


def _rms(x, g):
    return x * lax.rsqrt(jnp.mean(x * x, axis=-1, keepdims=True) + EPS) * g


def _silu(x):
    return x * jax.nn.sigmoid(x)


def _log_sigmoid(x):
    return jnp.minimum(x, 0.0) - jnp.log1p(jnp.exp(-jnp.abs(x)))


def _lane_mask(width, lo, size, dtype):
    lane = lax.broadcasted_iota(jnp.int32, (1, width), 1)
    return jnp.where((lane >= lo) & (lane < lo + size), 1.0, 0.0).astype(dtype)


def _dot(a, b):
    return jnp.dot(a, b, preferred_element_type=F32)


def _dot_nt(a, b):
    return lax.dot_general(a, b, (((1,), (1,)), ((), ())), preferred_element_type=F32)


def _dot_tn(a, b):
    return lax.dot_general(a, b, (((0,), (0,)), ((), ())), preferred_element_type=F32)


def _mod_kernel(c_ref, w_ref, b_ref, o_ref):
    s = _silu(c_ref[...]).astype(BF16)
    o_ref[0] = _dot(s, w_ref[0].astype(BF16)) + b_ref[0]


def _mod_call(cvecs, w_ada, b_ada):
    nb = 1024
    return pl.pallas_call(
        _mod_kernel,
        grid=(DEPTH, 3 * D_MODEL // nb),
        in_specs=[pl.BlockSpec((8, D_MODEL), lambda l, j: (0, 0)),
                  pl.BlockSpec((1, D_MODEL, nb), lambda l, j: (l, 0, j)),
                  pl.BlockSpec((1, 1, nb), lambda l, j: (l, 0, j))],
        out_specs=pl.BlockSpec((1, 8, nb), lambda l, j: (l, 0, j)),
        out_shape=jax.ShapeDtypeStruct((DEPTH, 8, 3 * D_MODEL), F32),
        compiler_params=pltpu.CompilerParams(dimension_semantics=("arbitrary", "arbitrary"),
                                             vmem_limit_bytes=V7X_VMEM_LIMIT_BYTES),
        name="adaln_mod",
    )(cvecs, w_ada, b_ada.reshape(DEPTH, 1, 3 * D_MODEL))


def _ctxkv_kernel(ckv_ref, kpe_ref, w_ref, o_ref):
    kv = _dot(ckv_ref[0, 0].astype(BF16), w_ref[0])
    kpe = kpe_ref[0, 0]
    o_ref[0, 0, :, 0:512] = kv[:, 0:512].astype(BF16)
    o_ref[0, 0, :, 512:640] = jnp.concatenate([kpe] * 4, axis=-1).astype(BF16)
    o_ref[0, 0, :, 640:1152] = kv[:, 512:1024].astype(BF16)


def _ctxkv_call(cache_ckv, cache_kpe, wukv):
    nb, _, tc, _ = cache_ckv.shape
    return pl.pallas_call(
        _ctxkv_kernel,
        grid=(DEPTH, nb),
        in_specs=[pl.BlockSpec((1, 1, tc, KV_LORA), lambda l, b: (b, l, 0, 0)),
                  pl.BlockSpec((1, 1, tc, DR_B), lambda l, b: (b, l, 0, 0)),
                  pl.BlockSpec((1, KV_LORA, 1024), lambda l, b: (l, 0, 0))],
        out_specs=pl.BlockSpec((1, 1, tc, 1152), lambda l, b: (l, b, 0, 0)),
        out_shape=jax.ShapeDtypeStruct((DEPTH, nb, tc, 1152), BF16),
        compiler_params=pltpu.CompilerParams(dimension_semantics=("arbitrary", "arbitrary"),
                                             vmem_limit_bytes=V7X_VMEM_LIMIT_BYTES),
        name="mla_ctx_kv",
    )(cache_ckv, cache_kpe, wukv)


def _pre_kernel(*refs, rope, ctx_out, bpb, seq):
    it = iter(refs)
    (x_ref, mod_ref, gpre_ref, w_ref, wg_ref, bg_ref, gq_ref, gkv_ref, wuq_ref,
     wukv_ref) = (next(it) for _ in range(10))
    if rope:
        cos_ref, sin_ref = next(it), next(it)
    qk_ref, v_ref, la_ref, gates_ref, mq_ref, mkv_ref, dqkv_ref = (next(it) for _ in range(7))
    if ctx_out:
        ckvn_ref, kpe_ref, kc_ref, vc_ref = (next(it) for _ in range(4))

    d = D_MODEL
    shift = mod_ref[0, :, 0:d]
    scale = mod_ref[0, :, d:2 * d]
    h = (_rms(x_ref[...], gpre_ref[...]) * (1.0 + scale) + shift).astype(BF16)
    if rope:
        cos = cos_ref[...]
        sin = sin_ref[...]

    pg = _dot(h, w_ref[:, A_GLA:A_GLA + 896])
    qk_ref[:, 0:128] = pg[:, 0:128] * (DK_A ** -0.5)
    qk_ref[:, 128:256] = pg[:, 128:256]
    v_ref[...] = pg[:, 256:512]
    xg = _dot(pg[:, 512:640].astype(BF16), wg_ref[...]) + bg_ref[...]
    la_ref[...] = _log_sigmoid(xg) * (1.0 / GLA_TAU)
    gates_ref[:, 0:W_A] = _silu(pg[:, 640:896])

    nm = 640 if rope else 512
    pm = _dot(h, w_ref[:, A_MLA:A_MLA + nm])
    qall = _dot(_rms(pm[:, 0:256], gq_ref[...]).astype(BF16), wuq_ref[...])
    q_pe = qall[:, 512:768]
    if rope:
        q_pe = q_pe * cos + qall[:, 768:1024] * sin
    sb = (DN_B + DR_B) ** -0.5
    mq_ref[:, 0:512] = (qall[:, 0:512] * sb).astype(BF16)
    mq_ref[:, 512:768] = (q_pe * sb).astype(BF16)
    ckvn = _rms(pm[:, 256:384], gkv_ref[...])
    kvall = _dot(ckvn.astype(BF16), wukv_ref[...])
    kpe4 = pm[:, 384:512]
    if rope:
        kpe4 = kpe4 * cos[:, 0:128] + pm[:, 512:640] * sin[:, 0:128]
    mkv_ref[:, 0:512] = kvall[:, 0:512].astype(BF16)
    mkv_ref[:, 512:640] = kpe4.astype(BF16)
    mkv_ref[:, 640:1152] = kvall[:, 512:1024].astype(BF16)
    if ctx_out:
        ckvn_ref[...] = ckvn
        kpe_ref[...] = pm[:, 384:384 + DR_B]
    gates_ref[:, W_A:W_A + W_B] = _silu(_dot(h, w_ref[:, A_MG:A_MG + 512]))

    if rope:
        pd = _dot(h, w_ref[:, A_DIFF:A_DIFF + 1024])
        dq = pd[:, 0:256] * cos + pd[:, 256:512] * sin
        dk = pd[:, 512:768] * cos + pd[:, 768:1024] * sin
    else:
        dq = _dot(h, w_ref[:, A_DIFF:A_DIFF + 256])
        dk = _dot(h, w_ref[:, A_DIFF + 512:A_DIFF + 768])
    pv = _dot(h, w_ref[:, A_DIFF + 1024:A_DIFF + 1536])
    dv = pv[:, 0:256]
    dqkv_ref[:, 0:256] = (dq * (DC ** -0.5)).astype(BF16)
    dqkv_ref[:, 256:512] = dk.astype(BF16)
    dqkv_ref[:, 512:768] = dv.astype(BF16)
    gates_ref[:, W_A + W_B:W_A + W_B + W_C] = _silu(pv[:, 256:512])
    if ctx_out:
        for bb in range(bpb):
            for hh in range(H_C):
                kc_ref[bb, hh] = dk[bb * seq:(bb + 1) * seq, 64 * hh:64 * hh + 64]
                vc_ref[bb, hh] = dv[bb * seq:(bb + 1) * seq, 64 * hh:64 * hh + 64]


def _pre_call(x2d, modl, lw, *, seq, rope_tabs, ctx_out):
    n = x2d.shape[0]
    tm = min(TOKEN_BLOCK, n)
    bpb = max(tm // seq, 1)
    rope = rope_tabs is not None
    sample = rope
    steps_per_seq = max(seq // tm, 1)

    def mod_idx(i):
        return ((i * tm) // seq + 1 if sample else 0, 0, 0)

    const = lambda i: (0, 0)
    in_specs = [pl.BlockSpec((tm, D_MODEL), lambda i: (i, 0)),
                pl.BlockSpec((1, 1, 3 * D_MODEL), mod_idx),
                pl.BlockSpec((1, D_MODEL), const),
                pl.BlockSpec((D_MODEL, N_AUG), const),
                pl.BlockSpec((128, 256), const),
                pl.BlockSpec((1, 256), const),
                pl.BlockSpec((1, Q_LORA), const),
                pl.BlockSpec((1, KV_LORA), const),
                pl.BlockSpec((Q_LORA, 1024), const),
                pl.BlockSpec((KV_LORA, 1024), const)]
    args = [x2d, modl, lw["g_pre"], lw["w_aug"], lw["w_gate"], lw["b_gate"], lw["g_mla_q"],
            lw["g_mla_kv"], lw["w_uq"], lw["w_ukv"]]
    if rope:
        in_specs += [pl.BlockSpec((tm, 256), lambda i: (i % steps_per_seq, 0))] * 2
        args += list(rope_tabs)
    widths = [(256, F32), (256, F32), (256, F32), (1024, F32), (768, BF16), (1152, BF16), (768, BF16)]
    out_specs = [pl.BlockSpec((tm, w), lambda i: (i, 0)) for w, _ in widths]
    out_shape = [jax.ShapeDtypeStruct((n, w), dt) for w, dt in widths]
    if ctx_out:
        nbt = n // seq
        out_specs += [pl.BlockSpec((tm, KV_LORA), lambda i: (i, 0)),
                      pl.BlockSpec((tm, DR_B), lambda i: (i, 0)),
                      pl.BlockSpec((bpb, H_C, seq, 2 * DC), lambda i: (i, 0, 0, 0)),
                      pl.BlockSpec((bpb, H_C, seq, 2 * DC), lambda i: (i, 0, 0, 0))]
        out_shape += [jax.ShapeDtypeStruct((n, KV_LORA), F32),
                      jax.ShapeDtypeStruct((n, DR_B), F32),
                      jax.ShapeDtypeStruct((nbt, H_C, seq, 2 * DC), F32),
                      jax.ShapeDtypeStruct((nbt, H_C, seq, 2 * DC), F32)]
    return pl.pallas_call(
        functools.partial(_pre_kernel, rope=rope, ctx_out=ctx_out, bpb=bpb, seq=seq),
        grid=(n // tm,), in_specs=in_specs, out_specs=out_specs, out_shape=out_shape,
        compiler_params=_cparams(), name="pre_rope" if rope else "pre_ctx",
    )(*args)


_GLA_LEVELS = (1, 2, 4, 8, 16, 32)


def _gla_consts(rev):
    c = GLA_CHUNK
    row = lax.broadcasted_iota(jnp.int32, (c, 128), 0)
    pos = (c - 1 - row) if rev else row
    ri = lax.broadcasted_iota(jnp.int32, (c, H_A * c), 0)
    cj = lax.broadcasted_iota(jnp.int32, (c, H_A * c), 1) & (c - 1)
    pi = (c - 1 - ri) if rev else ri
    pj = (c - 1 - cj) if rev else cj
    x = pi ^ pj
    lvl = jnp.where(pi == pj, 0, -1)
    for kbit in range(6):
        lvl = jnp.where((pj < pi) & ((x >> kbit) == 1), kbit + 1, lvl)
    return pos, lvl


def _gla_chunk(q, k, v, la, b, st_prev, pos, lvl, hm_bf, hm_f32, vm_bf, rev):
    c = GLA_CHUNK
    prv = pltpu.roll(la, c - 1 if rev else 1, 0)
    nxt = pltpu.roll(la, 1 if rev else c - 1, 0)
    s_tot = jnp.where(lvl == 0, _dot_nt(q.astype(BF16), jnp.concatenate([k.astype(BF16)] * H_A, 0) * hm_bf), 0.0)
    for kbit, m in enumerate(_GLA_LEVELS):
        up = ((pos >> kbit) & 1) == 1
        if m == 1:
            e = jnp.where(up, la, 0.0)
        elif m == 2:
            c4 = pos & 3
            e = jnp.where(c4 == 0, nxt, jnp.where(c4 == 1, 0.0, jnp.where(c4 == 2, la, la + prv)))
        else:
            nblk = c // (2 * m)
            loc = m if rev else m - 1
            b3 = b.reshape(nblk, 2 * m, 128)
            ref = jnp.broadcast_to(b3[:, loc:loc + 1, :], (nblk, 2 * m, 128)).reshape(c, 128)
            dlt = b - ref
            e = jnp.where(up, dlt, -dlt)
        xm = (jnp.where(up, q, k) * jnp.exp(e)).astype(BF16)
        sm = _dot_nt(xm, jnp.concatenate([xm] * H_A, 0) * hm_bf)
        s_tot = jnp.where(lvl == kbit + 1, sm, s_tot)
    vbd = jnp.concatenate([v] * H_A, 0) * vm_bf
    blast = b[0:1, :] if rev else b[c - 1:c, :]
    qbar = (q * jnp.exp(b)).astype(BF16)
    kdec = (k * jnp.exp(blast - b)).astype(BF16)
    o = _dot(s_tot.astype(BF16), vbd) + _dot_nt(qbar, st_prev.astype(BF16))
    st_new = st_prev * jnp.exp(blast) + _dot_tn(v, kdec) * hm_f32
    return o, st_new


def _gla_kernel(*refs, seq, has_s0):
    it = iter(refs)
    qk_ref, v_ref, la_ref, g_ref = (next(it) for _ in range(4))
    s0_ref = next(it) if has_s0 else None
    oa_ref = next(it)
    sfin_ref = None if has_s0 else next(it)
    b_sc, acc_sc, st_sc = next(it), next(it), next(it)

    c = GLA_CHUNK
    nc = seq // c
    rowc = lax.broadcasted_iota(jnp.int32, (seq, 128), 0) & (c - 1)
    bf = la_ref[:, 0:128]
    bb = la_ref[:, 128:256]
    s = 1
    while s < c:
        bf = bf + jnp.where(rowc >= s, pltpu.roll(bf, s, 0), 0.0)
        bb = bb + jnp.where(rowc < c - s, pltpu.roll(bb, seq - s, 0), 0.0)
        s *= 2
    b_sc[0] = bf
    b_sc[1] = bb
    acc_sc[...] = jnp.zeros_like(acc_sc)
    for d in range(2):
        st_sc[d] = s0_ref[0, d] if has_s0 else jnp.zeros((H_A * DV_A, H_A * DK_A), F32)

    hrow = lax.broadcasted_iota(jnp.int32, (H_A * c, 128), 0) // c
    hm_f32 = jnp.where(hrow == lax.broadcasted_iota(jnp.int32, (H_A * c, 128), 1) // DK_A, 1.0, 0.0)
    hm_bf = hm_f32.astype(BF16)
    vrow = lax.broadcasted_iota(jnp.int32, (H_A * c, H_A * DV_A), 0) // c
    vm_bf = jnp.where(vrow == lax.broadcasted_iota(jnp.int32, (H_A * c, H_A * DV_A), 1) // DV_A,
                      1.0, 0.0).astype(BF16)
    consts = (_gla_consts(False), _gla_consts(True))

    def body(n, carry):
        for d, rev in ((0, False), (1, True)):
            cn = (nc - 1 - n) if rev else n
            rows = pl.ds(pl.multiple_of(cn * c, c), c)
            pos, lvl = consts[d]
            o, st_new = _gla_chunk(qk_ref[rows, 0:128], qk_ref[rows, 128:256],
                                   v_ref[rows, :].astype(BF16), la_ref[rows, 128 * d:128 * d + 128],
                                   b_sc[d, rows, :], st_sc[d], pos, lvl, hm_bf, hm_f32, vm_bf, rev)
            acc_sc[rows, :] = acc_sc[rows, :] + o
            st_sc[d] = st_new
        return carry

    lax.fori_loop(0, nc, body, 0)

    oa = acc_sc[...]
    sq = oa * oa
    inv = jnp.zeros_like(oa)
    for hh in range(H_A):
        m = _lane_mask(W_A, DV_A * hh, DV_A, F32)
        ms = jnp.sum(sq * m, axis=-1, keepdims=True) * (1.0 / DV_A)
        inv = inv + lax.rsqrt(ms + EPS) * m
    oa_ref[...] = oa * inv * g_ref[...]
    if not has_s0:
        sfin_ref[0, 0] = st_sc[0]
        sfin_ref[0, 1] = st_sc[1]


def _gla_call(qk, v, la, g4, s0t, *, seq):
    n = qk.shape[0]
    nb = n // seq
    has_s0 = s0t is not None
    blk = lambda w: pl.BlockSpec((seq, w), lambda i: (i, 0))
    st_spec = pl.BlockSpec((1, 2, H_A * DV_A, H_A * DK_A), lambda i: (i, 0, 0, 0))
    in_specs = [blk(256), blk(256), blk(256), pl.BlockSpec((1, W_A), lambda i: (0, 0))]
    args = [qk, v, la, g4]
    out_specs = [blk(W_A)]
    out_shape = [jax.ShapeDtypeStruct((n, W_A), F32)]
    if has_s0:
        in_specs.append(st_spec)
        args.append(s0t)
    else:
        out_specs.append(st_spec)
        out_shape.append(jax.ShapeDtypeStruct((nb, 2, H_A * DV_A, H_A * DK_A), F32))
    return pl.pallas_call(
        functools.partial(_gla_kernel, seq=seq, has_s0=has_s0),
        grid=(nb,), in_specs=in_specs, out_specs=out_specs, out_shape=out_shape,
        scratch_shapes=[pltpu.VMEM((2, seq, 128), F32), pltpu.VMEM((seq, W_A), F32),
                        pltpu.VMEM((2, H_A * DV_A, H_A * DK_A), F32)],
        compiler_params=_cparams(), name="gla_state" if has_s0 else "gla_ctx",
    )(*args)


def _softmax_pv(s, vmat):
    m = jnp.max(s, axis=-1, keepdims=True)
    e = jnp.exp(s - m)
    l = jnp.sum(e, axis=-1, keepdims=True)
    return _dot(e.astype(BF16), vmat) * (1.0 / l)


def _mla_kernel(*refs, seq, ctx_len):
    it = iter(refs)
    q_ref, kv_ref = next(it), next(it)
    ckv_ref = next(it) if ctx_len else None
    ob_ref = next(it)
    kk_sc, vm_sc = next(it), next(it)

    for p in range(H_B // 2):
        kk_sc[p, 0:seq, 0:128] = kv_ref[:, 128 * p:128 * p + 128]
        kk_sc[p, 0:seq, 128:256] = kv_ref[:, 512:640]
        if ctx_len:
            kk_sc[p, seq:seq + ctx_len, 0:128] = ckv_ref[0, :, 128 * p:128 * p + 128]
            kk_sc[p, seq:seq + ctx_len, 128:256] = ckv_ref[0, :, 512:640]
        for hh in range(2):
            m = _lane_mask(128, DV_B * hh, DV_B, BF16)
            vm_sc[2 * p + hh, 0:seq, :] = kv_ref[:, 640 + 128 * p:640 + 128 * p + 128] * m
            if ctx_len:
                vm_sc[2 * p + hh, seq:seq + ctx_len, :] = ckv_ref[0, :, 640 + 128 * p:640 + 128 * p + 128] * m

    qb = min(ATT_QBLOCK, seq)

    def body(i, carry):
        rows = pl.ds(pl.multiple_of(i * qb, qb), qb)
        for p in range(H_B // 2):
            slab = jnp.zeros((qb, 128), F32)
            for hh in range(2):
                h = 2 * p + hh
                qn = q_ref[rows, 128 * p:128 * p + 128] * _lane_mask(128, DN_B * hh, DN_B, BF16)
                qp = (q_ref[rows, 512 + 128 * (h // 4):512 + 128 * (h // 4) + 128]
                      * _lane_mask(128, DR_B * (h % 4), DR_B, BF16))
                s = _dot_nt(jnp.concatenate([qn, qp], axis=-1), kk_sc[p])
                slab = slab + _softmax_pv(s, vm_sc[h])
            ob_ref[rows, 128 * p:128 * p + 128] = slab
        return carry

    lax.fori_loop(0, seq // qb, body, 0)


def _mla_call(mq, mkv, ckv, *, seq):
    n = mq.shape[0]
    ctx_len = 0 if ckv is None else ckv.shape[1]
    in_specs = [pl.BlockSpec((seq, 768), lambda i: (i, 0)), pl.BlockSpec((seq, 1152), lambda i: (i, 0))]
    args = [mq, mkv]
    if ctx_len:
        in_specs.append(pl.BlockSpec((1, ctx_len, 1152), lambda i: (i, 0, 0)))
        args.append(ckv)
    tk = seq + ctx_len
    return pl.pallas_call(
        functools.partial(_mla_kernel, seq=seq, ctx_len=ctx_len),
        grid=(n // seq,), in_specs=in_specs,
        out_specs=pl.BlockSpec((seq, W_B), lambda i: (i, 0)),
        out_shape=jax.ShapeDtypeStruct((n, W_B), F32),
        scratch_shapes=[pltpu.VMEM((H_B // 2, tk, 256), BF16), pltpu.VMEM((H_B, tk, 128), BF16)],
        compiler_params=_cparams(), name="mla_ctx" if ctx_len else "mla_self",
    )(*args)


def _diff_kernel(*refs, seq, ctx_len, lam_init):
    it = iter(refs)
    qkv_ref, lam_ref, g_ref = next(it), next(it), next(it)
    ck_ref, cv_ref = (next(it), next(it)) if ctx_len else (None, None)
    oc_ref = next(it)
    k_sc, vm_sc = next(it), next(it)

    lam = (jnp.exp(jnp.sum(lam_ref[0:1, :] * lam_ref[1:2, :], axis=-1, keepdims=True))
           - jnp.exp(jnp.sum(lam_ref[2:3, :] * lam_ref[3:4, :], axis=-1, keepdims=True)) + lam_init)
    for p in range(H_C // 2):
        k_sc[p, 0:seq, :] = qkv_ref[:, 256 + 128 * p:256 + 128 * p + 128]
        if ctx_len:
            k_sc[p, seq:seq + ctx_len, :] = ck_ref[0, :, 128 * p:128 * p + 128]
        for hh in range(2):
            m = _lane_mask(128, 2 * DC * hh, 2 * DC, BF16)
            vm_sc[2 * p + hh, 0:seq, :] = qkv_ref[:, 512 + 128 * p:512 + 128 * p + 128] * m
            if ctx_len:
                vm_sc[2 * p + hh, seq:seq + ctx_len, :] = cv_ref[0, :, 128 * p:128 * p + 128] * m

    qb = min(ATT_QBLOCK, seq)

    def body(i, carry):
        rows = pl.ds(pl.multiple_of(i * qb, qb), qb)
        for p in range(H_C // 2):
            slab = jnp.zeros((qb, 128), F32)
            qs = qkv_ref[rows, 128 * p:128 * p + 128]
            for hh in range(2):
                h = 2 * p + hh
                o1 = _softmax_pv(_dot_nt(qs * _lane_mask(128, 2 * DC * hh, DC, BF16), k_sc[p]), vm_sc[h])
                o2 = _softmax_pv(_dot_nt(qs * _lane_mask(128, 2 * DC * hh + DC, DC, BF16), k_sc[p]), vm_sc[h])
                slab = slab + (o1 - lam * o2)
            sq = slab * slab
            inv = jnp.zeros_like(slab)
            for hh in range(2):
                m = _lane_mask(128, 2 * DC * hh, 2 * DC, F32)
                ms = jnp.sum(sq * m, axis=-1, keepdims=True) * (1.0 / (2 * DC))
                inv = inv + lax.rsqrt(ms + EPS) * m
            oc_ref[rows, 128 * p:128 * p + 128] = (slab * inv * g_ref[:, 128 * p:128 * p + 128]
                                                    * (1.0 - lam_init))
        return carry

    lax.fori_loop(0, seq // qb, body, 0)


def _diff_call(dqkv, lamp, g4, ck, cv, *, seq, lam_init):
    n = dqkv.shape[0]
    ctx_len = 0 if ck is None else ck.shape[1]
    in_specs = [pl.BlockSpec((seq, 768), lambda i: (i, 0)),
                pl.BlockSpec((4, DC), lambda i: (0, 0)),
                pl.BlockSpec((1, W_C), lambda i: (0, 0))]
    args = [dqkv, lamp, g4]
    if ctx_len:
        in_specs += [pl.BlockSpec((1, ctx_len, W_C), lambda i: (i, 0, 0))] * 2
        args += [ck, cv]
    tk = seq + ctx_len
    return pl.pallas_call(
        functools.partial(_diff_kernel, seq=seq, ctx_len=ctx_len, lam_init=lam_init),
        grid=(n // seq,), in_specs=in_specs,
        out_specs=pl.BlockSpec((seq, W_C), lambda i: (i, 0)),
        out_shape=jax.ShapeDtypeStruct((n, W_C), F32),
        scratch_shapes=[pltpu.VMEM((H_C // 2, tk, 128), BF16), pltpu.VMEM((H_C, tk, 128), BF16)],
        compiler_params=_cparams(), name="diff_ctx" if ctx_len else "diff_self",
    )(*args)


def _post_kernel(oa_ref, ob_ref, oc_ref, gates_ref, x_ref, mod_ref, w_ref, g_ref, y_ref):
    mix = jnp.concatenate([oa_ref[...], ob_ref[...], oc_ref[...]], axis=-1) * gates_ref[...]
    out = _rms(_dot(mix.astype(BF16), w_ref[...]), g_ref[...])
    y_ref[...] = x_ref[...] + mod_ref[0, :, 2 * D_MODEL:3 * D_MODEL] * out


def _post_call(oa, ob, oc, gates, x2d, modl, w_out, g_post, *, seq, sample):
    n = x2d.shape[0]
    tm = min(TOKEN_BLOCK, n)
    blk = lambda w: pl.BlockSpec((tm, w), lambda i: (i, 0))

    def mod_idx(i):
        return ((i * tm) // seq + 1 if sample else 0, 0, 0)

    return pl.pallas_call(
        _post_kernel,
        grid=(n // tm,),
        in_specs=[blk(W_A), blk(W_B), blk(W_C), blk(D_MODEL), blk(D_MODEL),
                  pl.BlockSpec((1, 1, 3 * D_MODEL), mod_idx),
                  pl.BlockSpec((D_MODEL, D_MODEL), lambda i: (0, 0)),
                  pl.BlockSpec((1, D_MODEL), lambda i: (0, 0))],
        out_specs=blk(D_MODEL),
        out_shape=jax.ShapeDtypeStruct((n, D_MODEL), F32),
        compiler_params=_cparams(), name="post",
    )(oa, ob, oc, gates, x2d, modl, w_out, g_post)


def _rot_cols(w):
    ncol = w.shape[-1]
    col = np.arange(ncol)
    first = (col % 16) < 8
    src = np.where(first, col + 8, col - 8)
    sign = np.where(first, -1.0, 1.0).astype(np.float32)
    return w[..., src] * sign


def _pack_weights(w_in, w_gla_af, b_gla_af, w_gla_ab, b_gla_ab, w_mla_uq, w_mla_ukv):
    sl = lambda name, width: w_in[..., _IN[name]:_IN[name] + width]
    kpe = sl("kpe", DR_B)
    ga = jnp.pad(sl("ga", 2 * GLA_LR), ((0, 0), (0, 0), (0, 128 - 2 * GLA_LR)))
    w_aug = jnp.concatenate(
        [sl("gq", 128), sl("gk", 128), sl("gv", 256), ga, sl("gg", 256),
         sl("cq", 256), sl("ckv", 128), jnp.tile(kpe, (1, 1, 4)), jnp.tile(_rot_cols(kpe), (1, 1, 4)),
         sl("mg", 512),
         sl("dq", 256), _rot_cols(sl("dq", 256)), sl("dk", 256), _rot_cols(sl("dk", 256)),
         sl("dv", 256), sl("dg", 256)], axis=-1).astype(BF16)
    w_gate = jnp.zeros((DEPTH, 128, 256), F32)
    w_gate = w_gate.at[:, 0:GLA_LR, 0:128].set(w_gla_af).at[:, GLA_LR:2 * GLA_LR, 128:256].set(w_gla_ab)
    b_gate = jnp.concatenate([b_gla_af, b_gla_ab], axis=-1).reshape(DEPTH, 1, 256)
    hq = np.arange(H_B)[:, None] * (DN_B + DR_B)
    nope_idx = (hq + np.arange(DN_B)[None, :]).reshape(-1)
    pe_idx = (hq + DN_B + np.arange(DR_B)[None, :]).reshape(-1)
    w_pe = w_mla_uq[..., pe_idx]
    w_uq = jnp.concatenate([w_mla_uq[..., nope_idx], w_pe, _rot_cols(w_pe)], axis=-1).astype(BF16)
    hk = np.arange(H_B)[:, None] * (DN_B + DV_B)
    k_idx = (hk + np.arange(DN_B)[None, :]).reshape(-1)
    v_idx = (hk + DN_B + np.arange(DV_B)[None, :]).reshape(-1)
    w_ukv = jnp.concatenate([w_mla_ukv[..., k_idx], w_mla_ukv[..., v_idx]], axis=-1).astype(BF16)
    return w_aug, w_gate.astype(BF16), b_gate, w_uq, w_ukv


def _rope_tables(n):
    n_rows = n // GRID_W
    row = jnp.repeat(jnp.arange(n_rows), GRID_W).astype(F32)
    col = jnp.tile(jnp.arange(GRID_W), n_rows).astype(F32)
    half = ROPE_DIM // 2
    inv = 1.0 / (ROPE_THETA ** (jnp.arange(0, half, 2, dtype=F32) / half))
    ar = row[:, None] * inv
    ac = col[:, None] * inv
    ang = jnp.concatenate([ar, ar, ac, ac], axis=-1)
    return jnp.tile(jnp.cos(ang), (1, 8)), jnp.tile(jnp.sin(ang), (1, 8))


def _sublayer(x2d, modl, lw, layer, *, seq, rope_tabs, ctx):
    sample = ctx is not None
    pre = _pre_call(x2d, modl, lw, seq=seq, rope_tabs=rope_tabs, ctx_out=not sample)
    qk, v, la, gates, mq, mkv, dqkv = pre[:7]
    lam_init = 0.8 - 0.6 * math.exp(-0.3 * layer)
    if sample:
        (oa,) = _gla_call(qk, v, la, lw["g_gla4"], ctx["s0t"], seq=seq)
        ob = _mla_call(mq, mkv, ctx["mla_kv"], seq=seq)
        oc = _diff_call(dqkv, lw["lam"], lw["g_diff4"], ctx["diff_k"], ctx["diff_v"], seq=seq,
                        lam_init=lam_init)
        extras = None
    else:
        oa, sfin = _gla_call(qk, v, la, lw["g_gla4"], None, seq=seq)
        ob = _mla_call(mq, mkv, None, seq=seq)
        oc = _diff_call(dqkv, lw["lam"], lw["g_diff4"], None, None, seq=seq, lam_init=lam_init)
        extras = list(pre[7:]) + [sfin]
    y = _post_call(oa, ob, oc, gates, x2d, modl, lw["w_out"], lw["g_post"], seq=seq, sample=sample)
    return y, extras


def _state_to_t(state):
    b = state.shape[0]
    st = jnp.zeros((b, 2, H_A, DV_A, H_A, DK_A), F32)
    for hh in range(H_A):
        st = st.at[:, :, hh, :, hh, :].set(jnp.swapaxes(state[:, :, hh], -1, -2))
    return st.reshape(b, 2, H_A * DV_A, H_A * DK_A)


def _state_from_t(st):
    b = st.shape[0]
    s6 = st.reshape(b, 2, H_A, DV_A, H_A, DK_A)
    return jnp.stack([jnp.swapaxes(s6[:, :, hh, :, hh, :], -1, -2) for hh in range(H_A)], axis=2)


def kernel(x_prompt, x_sample, c, cache_mla_ckv, cache_mla_kpe, cache_diff_k, cache_diff_v, state_gla,
           c_ctx, w_ada, b_ada, g_pre, g_post, w_in, w_gla_af, b_gla_af, w_gla_ab, b_gla_ab, g_gla,
           g_mla_q, w_mla_uq, g_mla_kv, w_mla_ukv, lam_q1, lam_k1, lam_q2, lam_k2, g_diff, w_out):
    bp, tp, d = x_prompt.shape
    bs, ts, _ = x_sample.shape

    w_aug, w_gate, b_gate, w_uq, w_ukv = _pack_weights(w_in, w_gla_af, b_gla_af, w_gla_ab, b_gla_ab,
                                                       w_mla_uq, w_mla_ukv)
    w_out_bf = w_out.astype(BF16)
    cvecs = jnp.zeros((8, d), F32).at[0].set(c_ctx).at[1:1 + bs].set(c)
    mod = _mod_call(cvecs, w_ada, b_ada)
    rope_tabs = _rope_tables(ts)
    ctx_kv = _ctxkv_call(cache_mla_ckv, cache_mla_kpe, w_ukv)
    tc = cache_diff_k.shape[3]
    ck_all = jnp.transpose(cache_diff_k, (1, 0, 3, 2, 4)).reshape(DEPTH, bs, tc, W_C).astype(BF16)
    cv_all = jnp.transpose(cache_diff_v, (1, 0, 3, 2, 4)).reshape(DEPTH, bs, tc, W_C).astype(BF16)

    y_p = x_prompt.reshape(bp * tp, d)
    y_s = x_sample.reshape(bs * ts, d)
    ckv_l, kpe_l, dk_l, dv_l, gla_l = [], [], [], [], []
    for l in range(DEPTH):
        lw = dict(g_pre=g_pre[l][None], g_post=g_post[l][None], w_aug=w_aug[l], w_gate=w_gate[l],
                  b_gate=b_gate[l], g_mla_q=g_mla_q[l][None], g_mla_kv=g_mla_kv[l][None],
                  w_uq=w_uq[l], w_ukv=w_ukv[l], g_gla4=jnp.tile(g_gla[l], H_A)[None],
                  g_diff4=jnp.tile(g_diff[l], H_C)[None], w_out=w_out_bf[l],
                  lam=jnp.stack([lam_q1[l], lam_k1[l], lam_q2[l], lam_k2[l]]))
        modl = mod[l].reshape(8, 1, 3 * d)
        y_p, (ckvn, kpe, kc, vc, sfin) = _sublayer(y_p, modl, lw, l, seq=tp, rope_tabs=None, ctx=None)
        ckv_l.append(ckvn.reshape(bp, tp, KV_LORA))
        kpe_l.append(kpe.reshape(bp, tp, DR_B))
        dk_l.append(kc)
        dv_l.append(vc)
        gla_l.append(_state_from_t(sfin))
        ctx = dict(s0t=_state_to_t(state_gla[:, l]), mla_kv=ctx_kv[l], diff_k=ck_all[l], diff_v=cv_all[l])
        y_s, _ = _sublayer(y_s, modl, lw, l, seq=ts, rope_tabs=rope_tabs, ctx=ctx)
    return (y_p.reshape(bp, tp, d), y_s.reshape(bs, ts, d),
            jnp.stack(ckv_l, axis=1), jnp.stack(kpe_l, axis=1), jnp.stack(dk_l, axis=1),
            jnp.stack(dv_l, axis=1), jnp.stack(gla_l, axis=1))
```

```python
import functools
import math

import numpy as np
import jax
import jax.numpy as jnp
from jax import lax
from jax.experimental import pallas as pl
from jax.experimental.pallas import tpu as pltpu

F32 = jnp.float32
BF16 = jnp.bfloat16

D_MODEL = 1024
DEPTH = 2
GRID_W = 64
EPS = 1e-6
ROPE_THETA = 10000.0
ROPE_DIM = 32
H_A, DK_A, DV_A = 4, 32, 64
GLA_LR = 16
GLA_TAU = 16.0
GLA_CHUNK = 64
H_B, DN_B, DR_B, DV_B = 8, 64, 32, 64
Q_LORA, KV_LORA = 256, 128
H_C, DC = 4, 32
W_A, W_B, W_C = H_A * DV_A, H_B * DV_B, H_C * 2 * DC
ST_R, ST_C = H_A * DV_A, H_A * DK_A

_IN = dict(gq=0, gk=128, gv=256, ga=512, gg=544, cq=800, ckv=1056, kpe=1184, mg=1216,
           dq=1728, dk=1984, dv=2240, dg=2496)

A_GLA = 0
A_MLA = 896
A_MG = 1536
A_DIFF = 2048
N_AUG = 3584
MOD_ROWS = 8

V7X_VMEM_LIMIT_BYTES = 56 * 1024 * 1024
TOKEN_BLOCK = 512
ATT_QBLOCK = 256


def _cparams(n_axes=1):
    return pltpu.CompilerParams(dimension_semantics=("arbitrary",) * n_axes,
                                vmem_limit_bytes=V7X_VMEM_LIMIT_BYTES)


def _rms(x, g):
    return x * lax.rsqrt(jnp.mean(x * x, axis=-1, keepdims=True) + EPS) * g


def _silu(x):
    return x * jax.nn.sigmoid(x)


def _log_sigmoid(x):
    return jnp.minimum(x, 0.0) - jnp.log1p(jnp.exp(-jnp.abs(x)))


def _lane_mask(width, lo, size, dtype):
    lane = lax.broadcasted_iota(jnp.int32, (1, width), 1)
    return jnp.where((lane >= lo) & (lane < lo + size), 1.0, 0.0).astype(dtype)


def _dot(a, b):
    return jnp.dot(a, b, preferred_element_type=F32)


def _dot_nt(a, b):
    return lax.dot_general(a, b, (((1,), (1,)), ((), ())), preferred_element_type=F32)


def _dot_tn(a, b):
    return lax.dot_general(a, b, (((0,), (0,)), ((), ())), preferred_element_type=F32)


def _layer_spec(shape, layer):
    nd = len(shape)
    return pl.BlockSpec((1,) + tuple(shape[1:]), lambda *_: (layer,) + (0,) * (nd - 1))


_ANY = pl.BlockSpec(memory_space=pl.ANY)


def _mod_kernel(c_ref, w_ref, b_ref, o_ref):
    s = _silu(c_ref[...]).astype(BF16)
    o_ref[0] = _dot(s, w_ref[0].astype(BF16)) + b_ref[0]


def _mod_call(cvecs, w_ada, b_ada):
    nb = 1024
    return pl.pallas_call(
        _mod_kernel,
        grid=(DEPTH, 3 * D_MODEL // nb),
        in_specs=[pl.BlockSpec((MOD_ROWS, D_MODEL), lambda l, j: (0, 0)),
                  pl.BlockSpec((1, D_MODEL, nb), lambda l, j: (l, 0, j)),
                  pl.BlockSpec((1, 1, nb), lambda l, j: (l, 0, j))],
        out_specs=pl.BlockSpec((1, MOD_ROWS, nb), lambda l, j: (l, 0, j)),
        out_shape=jax.ShapeDtypeStruct((DEPTH, MOD_ROWS, 3 * D_MODEL), F32),
        compiler_params=_cparams(2), name="adaln_mod",
    )(cvecs, w_ada, b_ada.reshape(DEPTH, 1, 3 * D_MODEL))


def _ctxkv_kernel(ckv_ref, kpe_ref, w_ref, o_ref):
    kv = _dot(ckv_ref[0, 0].astype(BF16), w_ref[0])
    kpe = kpe_ref[0, 0]
    o_ref[0, 0, :, 0:512] = kv[:, 0:512].astype(BF16)
    o_ref[0, 0, :, 512:640] = jnp.concatenate([kpe] * 4, axis=-1).astype(BF16)
    o_ref[0, 0, :, 640:1152] = kv[:, 512:1024].astype(BF16)


def _ctxkv_call(cache_ckv, cache_kpe, wukv):
    nb, _, tc, _ = cache_ckv.shape
    return pl.pallas_call(
        _ctxkv_kernel,
        grid=(DEPTH, nb),
        in_specs=[pl.BlockSpec((1, 1, tc, KV_LORA), lambda l, b: (b, l, 0, 0)),
                  pl.BlockSpec((1, 1, tc, DR_B), lambda l, b: (b, l, 0, 0)),
                  pl.BlockSpec((1, KV_LORA, 1024), lambda l, b: (l, 0, 0))],
        out_specs=pl.BlockSpec((1, 1, tc, 1152), lambda l, b: (l, b, 0, 0)),
        out_shape=jax.ShapeDtypeStruct((DEPTH, nb, tc, 1152), BF16),
        compiler_params=_cparams(2), name="mla_ctx_kv",
    )(cache_ckv, cache_kpe, wukv)


def _pre_kernel(*refs, rope, ctx_out, alias_in, bpb, seq):
    it = iter(refs)
    (x_ref, mod_ref, gpre_ref, w_ref, wg_ref, bg_ref, gq_ref, gkv_ref, wuq_ref,
     wukv_ref) = (next(it) for _ in range(10))
    if rope:
        cos_ref, sin_ref = next(it), next(it)
    for _ in range(alias_in):
        next(it)
    qk_ref, v_ref, la_ref, gates_ref, mq_ref, mkv_ref, dqkv_ref = (next(it) for _ in range(7))
    if ctx_out:
        ckvn_ref, kpe_ref, kc_ref, vc_ref = (next(it) for _ in range(4))

    d = D_MODEL
    shift = mod_ref[0, :, 0:d]
    scale = mod_ref[0, :, d:2 * d]
    h = (_rms(x_ref[...], gpre_ref[0]) * (1.0 + scale) + shift).astype(BF16)
    if rope:
        cos = cos_ref[...]
        sin = sin_ref[...]

    pg = _dot(h, w_ref[0, :, A_GLA:A_GLA + 896])
    qk_ref[:, 0:128] = pg[:, 0:128] * (DK_A ** -0.5)
    qk_ref[:, 128:256] = pg[:, 128:256]
    v_ref[...] = pg[:, 256:512]
    xg = _dot(pg[:, 512:640].astype(BF16), wg_ref[0]) + bg_ref[0]
    la_ref[...] = _log_sigmoid(xg) * (1.0 / GLA_TAU)
    gates_ref[:, 0:W_A] = _silu(pg[:, 640:896])

    nm = 640 if rope else 512
    pm = _dot(h, w_ref[0, :, A_MLA:A_MLA + nm])
    qall = _dot(_rms(pm[:, 0:256], gq_ref[0]).astype(BF16), wuq_ref[0])
    q_pe = qall[:, 512:768]
    if rope:
        q_pe = q_pe * cos + qall[:, 768:1024] * sin
    sb = (DN_B + DR_B) ** -0.5
    mq_ref[:, 0:512] = (qall[:, 0:512] * sb).astype(BF16)
    mq_ref[:, 512:768] = (q_pe * sb).astype(BF16)
    ckvn = _rms(pm[:, 256:384], gkv_ref[0])
    kvall = _dot(ckvn.astype(BF16), wukv_ref[0])
    kpe4 = pm[:, 384:512]
    if rope:
        kpe4 = kpe4 * cos[:, 0:128] + pm[:, 512:640] * sin[:, 0:128]
    mkv_ref[:, 0:512] = kvall[:, 0:512].astype(BF16)
    mkv_ref[:, 512:640] = kpe4.astype(BF16)
    mkv_ref[:, 640:1152] = kvall[:, 512:1024].astype(BF16)
    gates_ref[:, W_A:W_A + W_B] = _silu(_dot(h, w_ref[0, :, A_MG:A_MG + 512]))

    if rope:
        pd = _dot(h, w_ref[0, :, A_DIFF:A_DIFF + 1024])
        dq = pd[:, 0:256] * cos + pd[:, 256:512] * sin
        dk = pd[:, 512:768] * cos + pd[:, 768:1024] * sin
    else:
        dq = _dot(h, w_ref[0, :, A_DIFF:A_DIFF + 256])
        dk = _dot(h, w_ref[0, :, A_DIFF + 512:A_DIFF + 768])
    pv = _dot(h, w_ref[0, :, A_DIFF + 1024:A_DIFF + 1536])
    dv = pv[:, 0:256]
    dqkv_ref[:, 0:256] = (dq * (DC ** -0.5)).astype(BF16)
    dqkv_ref[:, 256:512] = dk.astype(BF16)
    dqkv_ref[:, 512:768] = dv.astype(BF16)
    gates_ref[:, W_A + W_B:W_A + W_B + W_C] = _silu(pv[:, 256:512])
    if ctx_out:
        for bb in range(bpb):
            rs = slice(bb * seq, (bb + 1) * seq)
            ckvn_ref[bb, 0] = ckvn[rs]
            kpe_ref[bb, 0] = pm[rs, 384:384 + DR_B]
            for hh in range(H_C):
                kc_ref[bb, 0, hh] = dk[rs, 64 * hh:64 * hh + 64]
                vc_ref[bb, 0, hh] = dv[rs, 64 * hh:64 * hh + 64]


def _pre_call(x2d, mod, pw, layer, *, seq, rope_tabs, caches):
    n = x2d.shape[0]
    tm = min(TOKEN_BLOCK, n)
    bpb = max(tm // seq, 1)
    rope = rope_tabs is not None
    ctx_out = caches is not None
    steps_per_seq = max(seq // tm, 1)
    nbt = n // seq

    def mod_idx(i):
        return (layer * MOD_ROWS + ((i * tm) // seq + 1 if rope else 0), 0, 0)

    names = ["g_pre", "w_aug", "w_gate", "b_gate", "g_mla_q", "g_mla_kv", "w_uq", "w_ukv"]
    in_specs = [pl.BlockSpec((tm, D_MODEL), lambda i: (i, 0)), pl.BlockSpec((1, 1, 3 * D_MODEL), mod_idx)]
    in_specs += [_layer_spec(pw[k].shape, layer) for k in names]
    args = [x2d, mod] + [pw[k] for k in names]
    if rope:
        in_specs += [pl.BlockSpec((tm, 256), lambda i: (i % steps_per_seq, 0))] * 2
        args += list(rope_tabs)
    widths = [(256, F32), (256, F32), (256, F32), (1024, F32), (768, BF16), (1152, BF16), (768, BF16)]
    out_specs = [pl.BlockSpec((tm, w), lambda i: (i, 0)) for w, _ in widths]
    out_shape = [jax.ShapeDtypeStruct((n, w), dt) for w, dt in widths]
    aliases = {}
    if ctx_out:
        out_specs += [pl.BlockSpec((bpb, 1, seq, KV_LORA), lambda i: (i, layer, 0, 0)),
                      pl.BlockSpec((bpb, 1, seq, DR_B), lambda i: (i, layer, 0, 0)),
                      pl.BlockSpec((bpb, 1, H_C, seq, 2 * DC), lambda i: (i, layer, 0, 0, 0)),
                      pl.BlockSpec((bpb, 1, H_C, seq, 2 * DC), lambda i: (i, layer, 0, 0, 0))]
        out_shape += [jax.ShapeDtypeStruct((nbt, DEPTH, seq, KV_LORA), F32),
                      jax.ShapeDtypeStruct((nbt, DEPTH, seq, DR_B), F32),
                      jax.ShapeDtypeStruct((nbt, DEPTH, H_C, seq, 2 * DC), F32),
                      jax.ShapeDtypeStruct((nbt, DEPTH, H_C, seq, 2 * DC), F32)]
        for j, arr in enumerate(caches):
            aliases[len(args)] = len(widths) + j
            in_specs.append(_ANY)
            args.append(arr)
    return pl.pallas_call(
        functools.partial(_pre_kernel, rope=rope, ctx_out=ctx_out, alias_in=len(aliases), bpb=bpb, seq=seq),
        grid=(n // tm,), in_specs=in_specs, out_specs=out_specs, out_shape=out_shape,
        input_output_aliases=aliases,
        compiler_params=_cparams(), name="pre_rope" if rope else "pre_ctx",
    )(*args)


_GLA_LEVELS = (1, 2, 4, 8, 16, 32)


def _gla_consts(rev):
    c = GLA_CHUNK
    row = lax.broadcasted_iota(jnp.int32, (c, 128), 0)
    pos = (c - 1 - row) if rev else row
    ri = lax.broadcasted_iota(jnp.int32, (c, H_A * c), 0)
    cj = lax.broadcasted_iota(jnp.int32, (c, H_A * c), 1) & (c - 1)
    pi = (c - 1 - ri) if rev else ri
    pj = (c - 1 - cj) if rev else cj
    x = pi ^ pj
    lvl = jnp.where(pi == pj, 0, -1)
    for kbit in range(6):
        lvl = jnp.where((pj < pi) & ((x >> kbit) == 1), kbit + 1, lvl)
    return pos, lvl


def _gla_chunk(q, k, v, la, b, st_prev, pos, lvl, hm_bf, hm_f32, vm_bf, rev):
    c = GLA_CHUNK
    prv = pltpu.roll(la, c - 1 if rev else 1, 0)
    nxt = pltpu.roll(la, 1 if rev else c - 1, 0)
    s_tot = jnp.where(lvl == 0, _dot_nt(q.astype(BF16), jnp.concatenate([k.astype(BF16)] * H_A, 0) * hm_bf), 0.0)
    for kbit, m in enumerate(_GLA_LEVELS):
        up = ((pos >> kbit) & 1) == 1
        if m == 1:
            e = jnp.where(up, la, 0.0)
        elif m == 2:
            c4 = pos & 3
            e = jnp.where(c4 == 0, nxt, jnp.where(c4 == 1, 0.0, jnp.where(c4 == 2, la, la + prv)))
        else:
            nblk = c // (2 * m)
            loc = m if rev else m - 1
            b3 = b.reshape(nblk, 2 * m, 128)
            ref = jnp.broadcast_to(b3[:, loc:loc + 1, :], (nblk, 2 * m, 128)).reshape(c, 128)
            dlt = b - ref
            e = jnp.where(up, dlt, -dlt)
        xm = (jnp.where(up, q, k) * jnp.exp(e)).astype(BF16)
        sm = _dot_nt(xm, jnp.concatenate([xm] * H_A, 0) * hm_bf)
        s_tot = jnp.where(lvl == kbit + 1, sm, s_tot)
    vbd = jnp.concatenate([v] * H_A, 0) * vm_bf
    blast = b[0:1, :] if rev else b[c - 1:c, :]
    qbar = (q * jnp.exp(b)).astype(BF16)
    kdec = (k * jnp.exp(blast - b)).astype(BF16)
    o = _dot(s_tot.astype(BF16), vbd) + _dot_nt(qbar, st_prev.astype(BF16))
    st_new = st_prev * jnp.exp(blast) + _dot_tn(v, kdec) * hm_f32
    return o, st_new


def _gla_kernel(*refs, seq, has_s0):
    it = iter(refs)
    qk_ref, v_ref, la_ref, g_ref = (next(it) for _ in range(4))
    s0_ref = next(it) if has_s0 else None
    oa_ref = next(it)
    sfin_ref = None if has_s0 else next(it)
    b_sc, acc_sc, st_sc = next(it), next(it), next(it)

    c = GLA_CHUNK
    nc = seq // c
    rowc = lax.broadcasted_iota(jnp.int32, (seq, 128), 0) & (c - 1)
    bf = la_ref[:, 0:128]
    bb = la_ref[:, 128:256]
    s = 1
    while s < c:
        bf = bf + jnp.where(rowc >= s, pltpu.roll(bf, s, 0), 0.0)
        bb = bb + jnp.where(rowc < c - s, pltpu.roll(bb, seq - s, 0), 0.0)
        s *= 2
    b_sc[0] = bf
    b_sc[1] = bb
    acc_sc[...] = jnp.zeros_like(acc_sc)
    for d in range(2):
        st_sc[d] = s0_ref[0, d] if has_s0 else jnp.zeros((ST_R, ST_C), F32)

    hrow = lax.broadcasted_iota(jnp.int32, (H_A * c, 128), 0) // c
    hm_f32 = jnp.where(hrow == lax.broadcasted_iota(jnp.int32, (H_A * c, 128), 1) // DK_A, 1.0, 0.0)
    hm_bf = hm_f32.astype(BF16)
    vrow = lax.broadcasted_iota(jnp.int32, (H_A * c, H_A * DV_A), 0) // c
    vm_bf = jnp.where(vrow == lax.broadcasted_iota(jnp.int32, (H_A * c, H_A * DV_A), 1) // DV_A,
                      1.0, 0.0).astype(BF16)
    consts = (_gla_consts(False), _gla_consts(True))

    def body(n, carry):
        for d, rev in ((0, False), (1, True)):
            cn = (nc - 1 - n) if rev else n
            rows = pl.ds(pl.multiple_of(cn * c, c), c)
            pos, lvl = consts[d]
            o, st_new = _gla_chunk(qk_ref[rows, 0:128], qk_ref[rows, 128:256],
                                   v_ref[rows, :].astype(BF16), la_ref[rows, 128 * d:128 * d + 128],
                                   b_sc[d, rows, :], st_sc[d], pos, lvl, hm_bf, hm_f32, vm_bf, rev)
            acc_sc[rows, :] = acc_sc[rows, :] + o
            st_sc[d] = st_new
        return carry

    lax.fori_loop(0, nc, body, 0)

    oa = acc_sc[...]
    sq = oa * oa
    inv = jnp.zeros_like(oa)
    for hh in range(H_A):
        m = _lane_mask(W_A, DV_A * hh, DV_A, F32)
        ms = jnp.sum(sq * m, axis=-1, keepdims=True) * (1.0 / DV_A)
        inv = inv + lax.rsqrt(ms + EPS) * m
    oa_ref[...] = oa * inv * g_ref[0]
    if not has_s0:
        sfin_ref[0, 0] = st_sc[0]
        sfin_ref[0, 1] = st_sc[1]


def _gla_call(qk, v, la, g4, s0t, layer, *, seq):
    n = qk.shape[0]
    nb = n // seq
    has_s0 = s0t is not None
    blk = lambda w: pl.BlockSpec((seq, w), lambda i: (i, 0))
    in_specs = [blk(256), blk(256), blk(256), _layer_spec(g4.shape, layer)]
    args = [qk, v, la, g4]
    out_specs = [blk(W_A)]
    out_shape = [jax.ShapeDtypeStruct((n, W_A), F32)]
    if has_s0:
        in_specs.append(pl.BlockSpec((1, 2, ST_R, ST_C), lambda i: (i * DEPTH + layer, 0, 0, 0)))
        args.append(s0t)
    else:
        out_specs.append(pl.BlockSpec((1, 2, ST_R, ST_C), lambda i: (i, 0, 0, 0)))
        out_shape.append(jax.ShapeDtypeStruct((nb, 2, ST_R, ST_C), F32))
    return pl.pallas_call(
        functools.partial(_gla_kernel, seq=seq, has_s0=has_s0),
        grid=(nb,), in_specs=in_specs, out_specs=out_specs, out_shape=out_shape,
        scratch_shapes=[pltpu.VMEM((2, seq, 128), F32), pltpu.VMEM((seq, W_A), F32),
                        pltpu.VMEM((2, ST_R, ST_C), F32)],
        compiler_params=_cparams(), name="gla_state" if has_s0 else "gla_ctx",
    )(*args)


def _softmax_pv(s, vmat):
    m = jnp.max(s, axis=-1, keepdims=True)
    e = jnp.exp(s - m)
    l = jnp.sum(e, axis=-1, keepdims=True)
    return _dot(e.astype(BF16), vmat) * (1.0 / l)


def _mla_kernel(*refs, seq, ctx_len):
    it = iter(refs)
    q_ref, kv_ref = next(it), next(it)
    ckv_ref = next(it) if ctx_len else None
    ob_ref = next(it)
    kk_sc, vm_sc = next(it), next(it)

    for p in range(H_B // 2):
        kk_sc[p, 0:seq, 0:128] = kv_ref[:, 128 * p:128 * p + 128]
        kk_sc[p, 0:seq, 128:256] = kv_ref[:, 512:640]
        if ctx_len:
            kk_sc[p, seq:seq + ctx_len, 0:128] = ckv_ref[0, 0, :, 128 * p:128 * p + 128]
            kk_sc[p, seq:seq + ctx_len, 128:256] = ckv_ref[0, 0, :, 512:640]
        for hh in range(2):
            m = _lane_mask(128, DV_B * hh, DV_B, BF16)
            vm_sc[2 * p + hh, 0:seq, :] = kv_ref[:, 640 + 128 * p:640 + 128 * p + 128] * m
            if ctx_len:
                vm_sc[2 * p + hh, seq:seq + ctx_len, :] = (
                    ckv_ref[0, 0, :, 640 + 128 * p:640 + 128 * p + 128] * m)

    qb = min(ATT_QBLOCK, seq)

    def body(i, carry):
        rows = pl.ds(pl.multiple_of(i * qb, qb), qb)
        for p in range(H_B // 2):
            slab = jnp.zeros((qb, 128), F32)
            for hh in range(2):
                h = 2 * p + hh
                qn = q_ref[rows, 128 * p:128 * p + 128] * _lane_mask(128, DN_B * hh, DN_B, BF16)
                qp = (q_ref[rows, 512 + 128 * (h // 4):512 + 128 * (h // 4) + 128]
                      * _lane_mask(128, DR_B * (h % 4), DR_B, BF16))
                s = _dot_nt(jnp.concatenate([qn, qp], axis=-1), kk_sc[p])
                slab = slab + _softmax_pv(s, vm_sc[h])
            ob_ref[rows, 128 * p:128 * p + 128] = slab
        return carry

    lax.fori_loop(0, seq // qb, body, 0)


def _mla_call(mq, mkv, ckv, layer, *, seq):
    n = mq.shape[0]
    ctx_len = 0 if ckv is None else ckv.shape[2]
    in_specs = [pl.BlockSpec((seq, 768), lambda i: (i, 0)), pl.BlockSpec((seq, 1152), lambda i: (i, 0))]
    args = [mq, mkv]
    if ctx_len:
        in_specs.append(pl.BlockSpec((1, 1, ctx_len, 1152), lambda i: (layer, i, 0, 0)))
        args.append(ckv)
    tk = seq + ctx_len
    return pl.pallas_call(
        functools.partial(_mla_kernel, seq=seq, ctx_len=ctx_len),
        grid=(n // seq,), in_specs=in_specs,
        out_specs=pl.BlockSpec((seq, W_B), lambda i: (i, 0)),
        out_shape=jax.ShapeDtypeStruct((n, W_B), F32),
        scratch_shapes=[pltpu.VMEM((H_B // 2, tk, 256), BF16), pltpu.VMEM((H_B, tk, 128), BF16)],
        compiler_params=_cparams(), name="mla_ctx" if ctx_len else "mla_self",
    )(*args)


def _diff_kernel(*refs, seq, ctx_len, lam_init):
    it = iter(refs)
    qkv_ref, lam_ref, g_ref = next(it), next(it), next(it)
    ck_ref, cv_ref = (next(it), next(it)) if ctx_len else (None, None)
    oc_ref = next(it)
    k_sc, vm_sc = next(it), next(it)

    lam = (jnp.exp(jnp.sum(lam_ref[0, 0:1, :] * lam_ref[0, 1:2, :], axis=-1, keepdims=True))
           - jnp.exp(jnp.sum(lam_ref[0, 2:3, :] * lam_ref[0, 3:4, :], axis=-1, keepdims=True)) + lam_init)
    for p in range(H_C // 2):
        k_sc[p, 0:seq, :] = qkv_ref[:, 256 + 128 * p:256 + 128 * p + 128]
        for hh in range(2):
            h = 2 * p + hh
            m = _lane_mask(128, 2 * DC * hh, 2 * DC, BF16)
            vm_sc[h, 0:seq, :] = qkv_ref[:, 512 + 128 * p:512 + 128 * p + 128] * m
            if ctx_len:
                lo = 2 * DC * hh
                k_sc[p, seq:seq + ctx_len, lo:lo + 2 * DC] = ck_ref[0, 0, h].astype(BF16)
                vm_sc[h, seq:seq + ctx_len, lo:lo + 2 * DC] = cv_ref[0, 0, h].astype(BF16)
                vm_sc[h, seq:seq + ctx_len, 2 * DC - lo:4 * DC - lo] = jnp.zeros((ctx_len, 2 * DC), BF16)

    qb = min(ATT_QBLOCK, seq)

    def body(i, carry):
        rows = pl.ds(pl.multiple_of(i * qb, qb), qb)
        for p in range(H_C // 2):
            slab = jnp.zeros((qb, 128), F32)
            qs = qkv_ref[rows, 128 * p:128 * p + 128]
            for hh in range(2):
                h = 2 * p + hh
                o1 = _softmax_pv(_dot_nt(qs * _lane_mask(128, 2 * DC * hh, DC, BF16), k_sc[p]), vm_sc[h])
                o2 = _softmax_pv(_dot_nt(qs * _lane_mask(128, 2 * DC * hh + DC, DC, BF16), k_sc[p]), vm_sc[h])
                slab = slab + (o1 - lam * o2)
            sq = slab * slab
            inv = jnp.zeros_like(slab)
            for hh in range(2):
                m = _lane_mask(128, 2 * DC * hh, 2 * DC, F32)
                ms = jnp.sum(sq * m, axis=-1, keepdims=True) * (1.0 / (2 * DC))
                inv = inv + lax.rsqrt(ms + EPS) * m
            oc_ref[rows, 128 * p:128 * p + 128] = (slab * inv * g_ref[0, :, 128 * p:128 * p + 128]
                                                    * (1.0 - lam_init))
        return carry

    lax.fori_loop(0, seq // qb, body, 0)


def _diff_call(dqkv, lamp, g4, ck, cv, layer, *, seq):
    n = dqkv.shape[0]
    ctx_len = 0 if ck is None else ck.shape[3]
    lam_init = 0.8 - 0.6 * math.exp(-0.3 * layer)
    in_specs = [pl.BlockSpec((seq, 768), lambda i: (i, 0)),
                _layer_spec(lamp.shape, layer), _layer_spec(g4.shape, layer)]
    args = [dqkv, lamp, g4]
    if ctx_len:
        in_specs += [pl.BlockSpec((1, 1, H_C, ctx_len, 2 * DC), lambda i: (i, layer, 0, 0, 0))] * 2
        args += [ck, cv]
    tk = seq + ctx_len
    return pl.pallas_call(
        functools.partial(_diff_kernel, seq=seq, ctx_len=ctx_len, lam_init=lam_init),
        grid=(n // seq,), in_specs=in_specs,
        out_specs=pl.BlockSpec((seq, W_C), lambda i: (i, 0)),
        out_shape=jax.ShapeDtypeStruct((n, W_C), F32),
        scratch_shapes=[pltpu.VMEM((H_C // 2, tk, 128), BF16), pltpu.VMEM((H_C, tk, 128), BF16)],
        compiler_params=_cparams(), name="diff_ctx" if ctx_len else "diff_self",
    )(*args)


def _post_kernel(oa_ref, ob_ref, oc_ref, gates_ref, x_ref, mod_ref, w_ref, g_ref, y_ref):
    mix = jnp.concatenate([oa_ref[...], ob_ref[...], oc_ref[...]], axis=-1) * gates_ref[...]
    out = _rms(_dot(mix.astype(BF16), w_ref[0]), g_ref[0])
    y_ref[...] = x_ref[...] + mod_ref[0, :, 2 * D_MODEL:3 * D_MODEL] * out


def _post_call(oa, ob, oc, gates, x2d, mod, pw, layer, *, seq, sample):
    n = x2d.shape[0]
    tm = min(TOKEN_BLOCK, n)
    blk = lambda w: pl.BlockSpec((tm, w), lambda i: (i, 0))

    def mod_idx(i):
        return (layer * MOD_ROWS + ((i * tm) // seq + 1 if sample else 0), 0, 0)

    return pl.pallas_call(
        _post_kernel,
        grid=(n // tm,),
        in_specs=[blk(W_A), blk(W_B), blk(W_C), blk(D_MODEL), blk(D_MODEL),
                  pl.BlockSpec((1, 1, 3 * D_MODEL), mod_idx),
                  _layer_spec(pw["w_out"].shape, layer), _layer_spec(pw["g_post"].shape, layer)],
        out_specs=blk(D_MODEL),
        out_shape=jax.ShapeDtypeStruct((n, D_MODEL), F32),
        compiler_params=_cparams(), name="post",
    )(oa, ob, oc, gates, x2d, mod, pw["w_out"], pw["g_post"])


def _rot_cols(w):
    s = w.shape
    w4 = w.reshape(s[:-1] + (s[-1] // 16, 2, 8))
    return jnp.concatenate([-w4[..., 1:2, :], w4[..., 0:1, :]], axis=-2).reshape(s)


def _pack_params(g_pre, g_post, w_in, w_gla_af, b_gla_af, w_gla_ab, b_gla_ab, g_gla, g_mla_q, w_mla_uq,
                 g_mla_kv, w_mla_ukv, lam_q1, lam_k1, lam_q2, lam_k2, g_diff, w_out):
    sl = lambda name, width: w_in[..., _IN[name]:_IN[name] + width]
    kpe = sl("kpe", DR_B)
    zpad = jnp.zeros((DEPTH, D_MODEL, 128 - 2 * GLA_LR), F32)
    w_aug = jnp.concatenate(
        [w_in[..., 0:_IN["gg"]], zpad, sl("gg", 256),
         sl("cq", 256), sl("ckv", 128), kpe, kpe, kpe, kpe, *([_rot_cols(kpe)] * 4),
         sl("mg", 512),
         sl("dq", 256), _rot_cols(sl("dq", 256)), sl("dk", 256), _rot_cols(sl("dk", 256)),
         w_in[..., _IN["dv"]:_IN["dv"] + 512]], axis=-1).astype(BF16)
    zg = jnp.zeros((DEPTH, GLA_LR, 128), F32)
    w_gate = jnp.concatenate(
        [jnp.concatenate([w_gla_af, zg], axis=-1), jnp.concatenate([zg, w_gla_ab], axis=-1),
         jnp.zeros((DEPTH, 128 - 2 * GLA_LR, 256), F32)], axis=1).astype(BF16)
    uq = w_mla_uq.reshape(DEPTH, Q_LORA, H_B, DN_B + DR_B)
    w_pe = uq[..., DN_B:].reshape(DEPTH, Q_LORA, H_B * DR_B)
    w_uq = jnp.concatenate([uq[..., :DN_B].reshape(DEPTH, Q_LORA, H_B * DN_B), w_pe, _rot_cols(w_pe)],
                           axis=-1).astype(BF16)
    ukv = w_mla_ukv.reshape(DEPTH, KV_LORA, H_B, DN_B + DV_B)
    w_ukv = jnp.concatenate([ukv[..., :DN_B].reshape(DEPTH, KV_LORA, H_B * DN_B),
                             ukv[..., DN_B:].reshape(DEPTH, KV_LORA, H_B * DV_B)], axis=-1).astype(BF16)
    row = lambda a: a.reshape(DEPTH, 1, a.shape[-1])
    return dict(
        w_aug=w_aug, w_gate=w_gate, b_gate=row(jnp.concatenate([b_gla_af, b_gla_ab], axis=-1)),
        w_uq=w_uq, w_ukv=w_ukv, w_out=w_out.astype(BF16),
        g_pre=row(g_pre), g_post=row(g_post), g_mla_q=row(g_mla_q), g_mla_kv=row(g_mla_kv),
        g_gla4=row(jnp.tile(g_gla, (1, H_A))), g_diff4=row(jnp.tile(g_diff, (1, H_C))),
        lam=jnp.stack([lam_q1, lam_k1, lam_q2, lam_k2], axis=1))


def _rope_tables(n):
    t = np.arange(n)
    row = (t // GRID_W).astype(np.float32)
    col = (t % GRID_W).astype(np.float32)
    half = ROPE_DIM // 2
    inv = (1.0 / (np.float32(ROPE_THETA) ** (np.arange(0, half, 2, dtype=np.float32) / np.float32(half)))
           ).astype(np.float32)
    ar = row[:, None] * inv
    ac = col[:, None] * inv
    ang = np.concatenate([ar, ar, ac, ac], axis=-1).astype(np.float32)
    return (jnp.asarray(np.tile(np.cos(ang), (1, 8)).astype(np.float32)),
            jnp.asarray(np.tile(np.sin(ang), (1, 8)).astype(np.float32)))


def _state_to_t(state):
    b = state.shape[0]
    eye = jnp.eye(H_A, dtype=F32)
    st = jnp.einsum("bldhkv,hg->bldhvgk", state, eye)
    return st.reshape(b * DEPTH, 2, ST_R, ST_C)


def _state_from_t(st):
    b = st.shape[0]
    s6 = st.reshape(b, 2, H_A, DV_A, H_A, DK_A)
    return jnp.einsum("bdhvhk->bdhkv", s6)


def _sublayer(x2d, mod, pw, layer, *, seq, rope_tabs, ctx, caches):
    pre = _pre_call(x2d, mod, pw, layer, seq=seq, rope_tabs=rope_tabs, caches=caches)
    qk, v, la, gates, mq, mkv, dqkv = pre[:7]
    if ctx is not None:
        (oa,) = _gla_call(qk, v, la, pw["g_gla4"], ctx["s0t"], layer, seq=seq)
        ob = _mla_call(mq, mkv, ctx["mla_kv"], layer, seq=seq)
        oc = _diff_call(dqkv, pw["lam"], pw["g_diff4"], ctx["diff_k"], ctx["diff_v"], layer, seq=seq)
        extras = None
    else:
        oa, sfin = _gla_call(qk, v, la, pw["g_gla4"], None, layer, seq=seq)
        ob = _mla_call(mq, mkv, None, layer, seq=seq)
        oc = _diff_call(dqkv, pw["lam"], pw["g_diff4"], None, None, layer, seq=seq)
        extras = (tuple(pre[7:]), sfin)
    y = _post_call(oa, ob, oc, gates, x2d, mod, pw, layer, seq=seq, sample=ctx is not None)
    return y, extras


def kernel(x_prompt, x_sample, c, cache_mla_ckv, cache_mla_kpe, cache_diff_k, cache_diff_v, state_gla,
           c_ctx, w_ada, b_ada, g_pre, g_post, w_in, w_gla_af, b_gla_af, w_gla_ab, b_gla_ab, g_gla,
           g_mla_q, w_mla_uq, g_mla_kv, w_mla_ukv, lam_q1, lam_k1, lam_q2, lam_k2, g_diff, w_out):
    bp, tp, d = x_prompt.shape
    bs, ts, _ = x_sample.shape

    pw = _pack_params(g_pre, g_post, w_in, w_gla_af, b_gla_af, w_gla_ab, b_gla_ab, g_gla, g_mla_q,
                      w_mla_uq, g_mla_kv, w_mla_ukv, lam_q1, lam_k1, lam_q2, lam_k2, g_diff, w_out)
    cvecs = jnp.concatenate([c_ctx[None], c, jnp.zeros((MOD_ROWS - 1 - bs, d), F32)], axis=0)
    mod = _mod_call(cvecs, w_ada, b_ada).reshape(DEPTH * MOD_ROWS, 1, 3 * d)
    rope_tabs = _rope_tables(ts)
    ctx = dict(s0t=_state_to_t(state_gla),
               mla_kv=_ctxkv_call(cache_mla_ckv, cache_mla_kpe, pw["w_ukv"]),
               diff_k=cache_diff_k, diff_v=cache_diff_v)

    y_p = x_prompt.reshape(bp * tp, d)
    y_s = x_sample.reshape(bs * ts, d)
    caches = ()
    gla_l = []
    for l in range(DEPTH):
        y_p, (caches, sfin) = _sublayer(y_p, mod, pw, l, seq=tp, rope_tabs=None, ctx=None, caches=caches)
        gla_l.append(sfin)
        y_s, _ = _sublayer(y_s, mod, pw, l, seq=ts, rope_tabs=rope_tabs, ctx=ctx, caches=None)
    new_state = _state_from_t(jnp.stack(gla_l, axis=1).reshape(bp * DEPTH, 2, ST_R, ST_C))
    return (y_p.reshape(bp, tp, d), y_s.reshape(bs, ts, d), *caches,
            new_state.reshape(bp, DEPTH, 2, H_A, DK_A, DV_A))
```

```python
import functools
import math

import numpy as np
import jax
import jax.numpy as jnp
from jax import lax
from jax.experimental import pallas as pl
from jax.experimental.pallas import tpu as pltpu

F32 = jnp.float32
BF16 = jnp.bfloat16

D_MODEL = 1024
DEPTH = 2
GRID_W = 64
EPS = 1e-6
ROPE_THETA = 10000.0
ROPE_DIM = 32
H_A, DK_A, DV_A = 4, 32, 64
GLA_LR = 16
GLA_TAU = 16.0
GLA_CHUNK = 64
H_B, DN_B, DR_B, DV_B = 8, 64, 32, 64
Q_LORA, KV_LORA = 256, 128
H_C, DC = 4, 32
W_A, W_B, W_C = H_A * DV_A, H_B * DV_B, H_C * 2 * DC
ST_R, ST_C = H_A * DV_A, H_A * DK_A

R_GLA = (0, 544)
R_GG = (544, 800)
R_MLA = (800, 1216)
R_MG = (1216, 1728)
R_DIFF = (1728, 2752)
N_IN = 2752
X_KPE4, X_KPE4_ROT, X_DQ_ROT, X_DK_ROT, N_EXTRA = 0, 128, 256, 512, 768
MOD_ROWS = 8

V7X_VMEM_LIMIT_BYTES = 56 * 1024 * 1024
TOKEN_BLOCK = 512
ATT_QBLOCK = 256


def _cparams(n_axes=1):
    return pltpu.CompilerParams(dimension_semantics=("arbitrary",) * n_axes,
                                vmem_limit_bytes=V7X_VMEM_LIMIT_BYTES)


def _rms(x, g):
    return x * lax.rsqrt(jnp.mean(x * x, axis=-1, keepdims=True) + EPS) * g


def _silu(x):
    return x * jax.nn.sigmoid(x)


def _log_sigmoid(x):
    return jnp.minimum(x, 0.0) - jnp.log1p(jnp.exp(-jnp.abs(x)))


def _lane_mask(width, lo, size, dtype):
    lane = lax.broadcasted_iota(jnp.int32, (1, width), 1)
    return jnp.where((lane >= lo) & (lane < lo + size), 1.0, 0.0).astype(dtype)


def _dot(a, b):
    return jnp.dot(a, b, preferred_element_type=F32)


def _dot_nt(a, b):
    return lax.dot_general(a, b, (((1,), (1,)), ((), ())), preferred_element_type=F32)


def _dot_tn(a, b):
    return lax.dot_general(a, b, (((0,), (0,)), ((), ())), preferred_element_type=F32)


def _layer_spec(shape, layer):
    nd = len(shape)
    return pl.BlockSpec((1,) + tuple(shape[1:]), lambda *_: (layer,) + (0,) * (nd - 1))


_ANY = pl.BlockSpec(memory_space=pl.ANY)


def _mod_kernel(c_ref, w_ref, b_ref, o_ref):
    s = _silu(c_ref[...]).astype(BF16)
    o_ref[0] = _dot(s, w_ref[0].astype(BF16)) + b_ref[0]


def _mod_call(cvecs, w_ada, b_ada):
    nb = 1024
    return pl.pallas_call(
        _mod_kernel,
        grid=(DEPTH, 3 * D_MODEL // nb),
        in_specs=[pl.BlockSpec((MOD_ROWS, D_MODEL), lambda l, j: (0, 0)),
                  pl.BlockSpec((1, D_MODEL, nb), lambda l, j: (l, 0, j)),
                  pl.BlockSpec((1, 1, nb), lambda l, j: (l, 0, j))],
        out_specs=pl.BlockSpec((1, MOD_ROWS, nb), lambda l, j: (l, 0, j)),
        out_shape=jax.ShapeDtypeStruct((DEPTH, MOD_ROWS, 3 * D_MODEL), F32),
        compiler_params=_cparams(2), name="adaln_mod",
    )(cvecs, w_ada, b_ada.reshape(DEPTH, 1, 3 * D_MODEL))


def _ctxkv_kernel(ckv_ref, kpe_ref, w_ref, o_ref):
    kv = _dot(ckv_ref[0, 0].astype(BF16), w_ref[0])
    kpe4 = jnp.concatenate([kpe_ref[0, 0]] * 4, axis=0).T
    o_ref[0, 0, :, 0:512] = kv[:, 0:512].astype(BF16)
    o_ref[0, 0, :, 512:640] = kpe4.astype(BF16)
    o_ref[0, 0, :, 640:1152] = kv[:, 512:1024].astype(BF16)


def _ctxkv_call(cache_ckv, cache_kpe_t, wukv):
    nb, _, tc, _ = cache_ckv.shape
    return pl.pallas_call(
        _ctxkv_kernel,
        grid=(DEPTH, nb),
        in_specs=[pl.BlockSpec((1, 1, tc, KV_LORA), lambda l, b: (b, l, 0, 0)),
                  pl.BlockSpec((1, 1, DR_B, tc), lambda l, b: (b, l, 0, 0)),
                  pl.BlockSpec((1, KV_LORA, 1024), lambda l, b: (l, 0, 0))],
        out_specs=pl.BlockSpec((1, 1, tc, 1152), lambda l, b: (l, b, 0, 0)),
        out_shape=jax.ShapeDtypeStruct((DEPTH, nb, tc, 1152), BF16),
        compiler_params=_cparams(2), name="mla_ctx_kv",
    )(cache_ckv, cache_kpe_t, wukv)


def _pre_kernel(*refs, rope, ctx_out, alias_in, bpb, seq):
    it = iter(refs)
    (x_ref, mod_ref, gpre_ref, w_ref, wx_ref, wg_ref, bg_ref, gq_ref, gkv_ref, wuq_ref,
     wukv_ref) = (next(it) for _ in range(11))
    if rope:
        cos_ref, sin_ref = next(it), next(it)
    for _ in range(alias_in):
        next(it)
    qk_ref, v_ref, la_ref, gates_ref, mq_ref, mkv_ref, dqkv_ref = (next(it) for _ in range(7))
    if ctx_out:
        ckvn_ref, kpe_ref, kc_ref, vc_ref = (next(it) for _ in range(4))

    d = D_MODEL
    shift = mod_ref[0, :, 0:d]
    scale = mod_ref[0, :, d:2 * d]
    h = (_rms(x_ref[...], gpre_ref[0]) * (1.0 + scale) + shift).astype(BF16)
    proj = lambda r: _dot_nt(h, w_ref[0, r[0]:r[1], :])
    projx = lambda lo, hi: _dot_nt(h, wx_ref[0, lo:hi, :])
    if rope:
        cos = cos_ref[...]
        sin = sin_ref[...]

    pg = proj(R_GLA)
    qk_ref[:, 0:128] = pg[:, 0:128] * (DK_A ** -0.5)
    qk_ref[:, 128:256] = pg[:, 128:256]
    v_ref[...] = pg[:, 256:512]
    xg = _dot(pg[:, 512:512 + 2 * GLA_LR].astype(BF16), wg_ref[0]) + bg_ref[0]
    la_ref[...] = _log_sigmoid(xg) * (1.0 / GLA_TAU)
    gates_ref[:, 0:W_A] = _silu(proj(R_GG))

    pm = proj(R_MLA)
    qall = _dot(_rms(pm[:, 0:256], gq_ref[0]).astype(BF16), wuq_ref[0])
    q_pe = qall[:, 512:768]
    if rope:
        q_pe = q_pe * cos + qall[:, 768:1024] * sin
    sb = (DN_B + DR_B) ** -0.5
    mq_ref[:, 0:512] = (qall[:, 0:512] * sb).astype(BF16)
    mq_ref[:, 512:768] = (q_pe * sb).astype(BF16)
    ckvn = _rms(pm[:, 256:384], gkv_ref[0])
    kvall = _dot(ckvn.astype(BF16), wukv_ref[0])
    if rope:
        px = projx(X_KPE4, X_KPE4_ROT + 128)
        kpe4 = px[:, 0:128] * cos[:, 0:128] + px[:, 128:256] * sin[:, 0:128]
    else:
        kpe4 = projx(X_KPE4, X_KPE4 + 128)
    mkv_ref[:, 0:512] = kvall[:, 0:512].astype(BF16)
    mkv_ref[:, 512:640] = kpe4.astype(BF16)
    mkv_ref[:, 640:1152] = kvall[:, 512:1024].astype(BF16)
    gates_ref[:, W_A:W_A + W_B] = _silu(proj(R_MG))

    pd = proj(R_DIFF)
    dq, dk, dv = pd[:, 0:256], pd[:, 256:512], pd[:, 512:768]
    if rope:
        pr = projx(X_DQ_ROT, X_DK_ROT + 256)
        dq = dq * cos + pr[:, 0:256] * sin
        dk = dk * cos + pr[:, 256:512] * sin
    dqkv_ref[:, 0:256] = (dq * (DC ** -0.5)).astype(BF16)
    dqkv_ref[:, 256:512] = dk.astype(BF16)
    dqkv_ref[:, 512:768] = dv.astype(BF16)
    gates_ref[:, W_A + W_B:W_A + W_B + W_C] = _silu(pd[:, 768:1024])
    if ctx_out:
        kpe_t = kpe4.T
        dk_t = dk.T
        dv_t = dv.T
        for bb in range(bpb):
            rs = slice(bb * seq, (bb + 1) * seq)
            ckvn_ref[bb, 0] = ckvn[rs]
            kpe_ref[bb, 0] = kpe_t[0:DR_B, rs]
            kc_ref[bb, 0] = dk_t[:, rs].reshape(H_C, 2 * DC, seq)
            vc_ref[bb, 0] = dv_t[:, rs].reshape(H_C, 2 * DC, seq)


def _pre_call(x2d, mod, pw, layer, *, seq, rope_tabs, caches):
    n = x2d.shape[0]
    tm = min(TOKEN_BLOCK, n)
    bpb = max(tm // seq, 1)
    rope = rope_tabs is not None
    ctx_out = caches is not None
    steps_per_seq = max(seq // tm, 1)
    nbt = n // seq

    def mod_idx(i):
        return (layer * MOD_ROWS + ((i * tm) // seq + 1 if rope else 0), 0, 0)

    names = ["g_pre", "w_t", "w_x", "w_gate", "b_gate", "g_mla_q", "g_mla_kv", "w_uq", "w_ukv"]
    in_specs = [pl.BlockSpec((tm, D_MODEL), lambda i: (i, 0)), pl.BlockSpec((1, 1, 3 * D_MODEL), mod_idx)]
    in_specs += [_layer_spec(pw[k].shape, layer) for k in names]
    args = [x2d, mod] + [pw[k] for k in names]
    if rope:
        in_specs += [pl.BlockSpec((tm, 256), lambda i: (i % steps_per_seq, 0))] * 2
        args += list(rope_tabs)
    widths = [(256, F32), (256, F32), (256, F32), (1024, F32), (768, BF16), (1152, BF16), (768, BF16)]
    out_specs = [pl.BlockSpec((tm, w), lambda i: (i, 0)) for w, _ in widths]
    out_shape = [jax.ShapeDtypeStruct((n, w), dt) for w, dt in widths]
    aliases = {}
    if ctx_out:
        out_specs += [pl.BlockSpec((bpb, 1, seq, KV_LORA), lambda i: (i, layer, 0, 0)),
                      pl.BlockSpec((bpb, 1, DR_B, seq), lambda i: (i, layer, 0, 0)),
                      pl.BlockSpec((bpb, 1, H_C, 2 * DC, seq), lambda i: (i, layer, 0, 0, 0)),
                      pl.BlockSpec((bpb, 1, H_C, 2 * DC, seq), lambda i: (i, layer, 0, 0, 0))]
        out_shape += [jax.ShapeDtypeStruct((nbt, DEPTH, seq, KV_LORA), F32),
                      jax.ShapeDtypeStruct((nbt, DEPTH, DR_B, seq), F32),
                      jax.ShapeDtypeStruct((nbt, DEPTH, H_C, 2 * DC, seq), F32),
                      jax.ShapeDtypeStruct((nbt, DEPTH, H_C, 2 * DC, seq), F32)]
        for j, arr in enumerate(caches):
            aliases[len(args)] = len(widths) + j
            in_specs.append(_ANY)
            args.append(arr)
    return pl.pallas_call(
        functools.partial(_pre_kernel, rope=rope, ctx_out=ctx_out, alias_in=len(aliases), bpb=bpb, seq=seq),
        grid=(n // tm,), in_specs=in_specs, out_specs=out_specs, out_shape=out_shape,
        input_output_aliases=aliases,
        compiler_params=_cparams(), name="pre_rope" if rope else "pre_ctx",
    )(*args)


_GLA_LEVELS = (1, 2, 4, 8, 16, 32)


def _gla_consts(rev):
    c = GLA_CHUNK
    row = lax.broadcasted_iota(jnp.int32, (c, 128), 0)
    pos = (c - 1 - row) if rev else row
    ri = lax.broadcasted_iota(jnp.int32, (c, H_A * c), 0)
    cj = lax.broadcasted_iota(jnp.int32, (c, H_A * c), 1) & (c - 1)
    pi = (c - 1 - ri) if rev else ri
    pj = (c - 1 - cj) if rev else cj
    x = pi ^ pj
    lvl = jnp.where(pi == pj, 0, -1)
    for kbit in range(6):
        lvl = jnp.where((pj < pi) & ((x >> kbit) == 1), kbit + 1, lvl)
    return pos, lvl


def _gla_chunk(q, k, v, la, b, st_prev, pos, lvl, hm_bf, hm_f32, vm_bf, rev):
    c = GLA_CHUNK
    prv = pltpu.roll(la, c - 1 if rev else 1, 0)
    nxt = pltpu.roll(la, 1 if rev else c - 1, 0)
    s_tot = jnp.where(lvl == 0, _dot_nt(q.astype(BF16), jnp.concatenate([k.astype(BF16)] * H_A, 0) * hm_bf), 0.0)
    for kbit, m in enumerate(_GLA_LEVELS):
        up = ((pos >> kbit) & 1) == 1
        if m == 1:
            e = jnp.where(up, la, 0.0)
        elif m == 2:
            c4 = pos & 3
            e = jnp.where(c4 == 0, nxt, jnp.where(c4 == 1, 0.0, jnp.where(c4 == 2, la, la + prv)))
        else:
            nblk = c // (2 * m)
            loc = m if rev else m - 1
            b3 = b.reshape(nblk, 2 * m, 128)
            ref = jnp.broadcast_to(b3[:, loc:loc + 1, :], (nblk, 2 * m, 128)).reshape(c, 128)
            dlt = b - ref
            e = jnp.where(up, dlt, -dlt)
        xm = (jnp.where(up, q, k) * jnp.exp(e)).astype(BF16)
        sm = _dot_nt(xm, jnp.concatenate([xm] * H_A, 0) * hm_bf)
        s_tot = jnp.where(lvl == kbit + 1, sm, s_tot)
    vbd = jnp.concatenate([v] * H_A, 0) * vm_bf
    blast = b[0:1, :] if rev else b[c - 1:c, :]
    qbar = (q * jnp.exp(b)).astype(BF16)
    kdec = (k * jnp.exp(blast - b)).astype(BF16)
    o = _dot(s_tot.astype(BF16), vbd) + _dot_nt(qbar, st_prev.astype(BF16))
    st_new = st_prev * jnp.exp(blast) + _dot_tn(v, kdec) * hm_f32
    return o, st_new


def _gla_kernel(*refs, seq, has_s0, alias_in):
    it = iter(refs)
    qk_ref, v_ref, la_ref, g_ref = (next(it) for _ in range(4))
    s0_ref = next(it) if has_s0 else None
    for _ in range(alias_in):
        next(it)
    oa_ref = next(it)
    sfin_ref = None if has_s0 else next(it)
    b_sc, acc_sc, st_sc, blk_sc = next(it), next(it), next(it), next(it)

    c = GLA_CHUNK
    nc = seq // c
    rowc = lax.broadcasted_iota(jnp.int32, (seq, 128), 0) & (c - 1)
    bf = la_ref[:, 0:128]
    bb = la_ref[:, 128:256]
    s = 1
    while s < c:
        bf = bf + jnp.where(rowc >= s, pltpu.roll(bf, s, 0), 0.0)
        bb = bb + jnp.where(rowc < c - s, pltpu.roll(bb, seq - s, 0), 0.0)
        s *= 2
    b_sc[0] = bf
    b_sc[1] = bb
    acc_sc[...] = jnp.zeros_like(acc_sc)
    for d in range(2):
        if has_s0:
            blk_sc[...] = jnp.zeros_like(blk_sc)
            for hh in range(H_A):
                blk_sc[DK_A * hh:DK_A * (hh + 1), DV_A * hh:DV_A * (hh + 1)] = s0_ref[0, 0, d, hh]
            st_sc[d] = blk_sc[...].T
        else:
            st_sc[d] = jnp.zeros((ST_R, ST_C), F32)

    hrow = lax.broadcasted_iota(jnp.int32, (H_A * c, 128), 0) // c
    hm_f32 = jnp.where(hrow == lax.broadcasted_iota(jnp.int32, (H_A * c, 128), 1) // DK_A, 1.0, 0.0)
    hm_bf = hm_f32.astype(BF16)
    vrow = lax.broadcasted_iota(jnp.int32, (H_A * c, H_A * DV_A), 0) // c
    vm_bf = jnp.where(vrow == lax.broadcasted_iota(jnp.int32, (H_A * c, H_A * DV_A), 1) // DV_A,
                      1.0, 0.0).astype(BF16)
    consts = (_gla_consts(False), _gla_consts(True))

    def body(n, carry):
        for d, rev in ((0, False), (1, True)):
            cn = (nc - 1 - n) if rev else n
            rows = pl.ds(pl.multiple_of(cn * c, c), c)
            pos, lvl = consts[d]
            o, st_new = _gla_chunk(qk_ref[rows, 0:128], qk_ref[rows, 128:256],
                                   v_ref[rows, :].astype(BF16), la_ref[rows, 128 * d:128 * d + 128],
                                   b_sc[d, rows, :], st_sc[d], pos, lvl, hm_bf, hm_f32, vm_bf, rev)
            acc_sc[rows, :] = acc_sc[rows, :] + o
            st_sc[d] = st_new
        return carry

    lax.fori_loop(0, nc, body, 0)

    oa = acc_sc[...]
    sq = oa * oa
    inv = jnp.zeros_like(oa)
    for hh in range(H_A):
        m = _lane_mask(W_A, DV_A * hh, DV_A, F32)
        ms = jnp.sum(sq * m, axis=-1, keepdims=True) * (1.0 / DV_A)
        inv = inv + lax.rsqrt(ms + EPS) * m
    oa_ref[...] = oa * inv * g_ref[0]
    if not has_s0:
        for d in range(2):
            blk_sc[...] = st_sc[d].T
            for hh in range(H_A):
                sfin_ref[0, 0, d, hh] = blk_sc[DK_A * hh:DK_A * (hh + 1), DV_A * hh:DV_A * (hh + 1)]


def _gla_call(qk, v, la, g4, layer, *, seq, state_in=None, state_out=None):
    n = qk.shape[0]
    nb = n // seq
    has_s0 = state_in is not None
    blk = lambda w: pl.BlockSpec((seq, w), lambda i: (i, 0))
    st_spec = pl.BlockSpec((1, 1, 2, H_A, DK_A, DV_A), lambda i: (i, layer, 0, 0, 0, 0))
    in_specs = [blk(256), blk(256), blk(256), _layer_spec(g4.shape, layer)]
    args = [qk, v, la, g4]
    out_specs = [blk(W_A)]
    out_shape = [jax.ShapeDtypeStruct((n, W_A), F32)]
    aliases = {}
    if has_s0:
        in_specs.append(st_spec)
        args.append(state_in)
    else:
        out_specs.append(st_spec)
        out_shape.append(jax.ShapeDtypeStruct((nb, DEPTH, 2, H_A, DK_A, DV_A), F32))
        if state_out is not None:
            aliases[len(args)] = 1
            in_specs.append(_ANY)
            args.append(state_out)
    return pl.pallas_call(
        functools.partial(_gla_kernel, seq=seq, has_s0=has_s0, alias_in=len(aliases)),
        grid=(nb,), in_specs=in_specs, out_specs=out_specs, out_shape=out_shape,
        input_output_aliases=aliases,
        scratch_shapes=[pltpu.VMEM((2, seq, 128), F32), pltpu.VMEM((seq, W_A), F32),
                        pltpu.VMEM((2, ST_R, ST_C), F32), pltpu.VMEM((ST_C, ST_R), F32)],
        compiler_params=_cparams(), name="gla_state" if has_s0 else "gla_ctx",
    )(*args)


def _softmax_pv(s, vmat):
    m = jnp.max(s, axis=-1, keepdims=True)
    e = jnp.exp(s - m)
    l = jnp.sum(e, axis=-1, keepdims=True)
    return _dot(e.astype(BF16), vmat) * (1.0 / l)


def _mla_kernel(*refs, seq, ctx_len):
    it = iter(refs)
    q_ref, kv_ref = next(it), next(it)
    ckv_ref = next(it) if ctx_len else None
    ob_ref = next(it)
    kk_sc, vm_sc = next(it), next(it)

    for p in range(H_B // 2):
        kk_sc[p, 0:seq, 0:128] = kv_ref[:, 128 * p:128 * p + 128]
        kk_sc[p, 0:seq, 128:256] = kv_ref[:, 512:640]
        if ctx_len:
            kk_sc[p, seq:seq + ctx_len, 0:128] = ckv_ref[0, 0, :, 128 * p:128 * p + 128]
            kk_sc[p, seq:seq + ctx_len, 128:256] = ckv_ref[0, 0, :, 512:640]
        for hh in range(2):
            m = _lane_mask(128, DV_B * hh, DV_B, BF16)
            vm_sc[2 * p + hh, 0:seq, :] = kv_ref[:, 640 + 128 * p:640 + 128 * p + 128] * m
            if ctx_len:
                vm_sc[2 * p + hh, seq:seq + ctx_len, :] = (
                    ckv_ref[0, 0, :, 640 + 128 * p:640 + 128 * p + 128] * m)

    qb = min(ATT_QBLOCK, seq)

    def body(i, carry):
        rows = pl.ds(pl.multiple_of(i * qb, qb), qb)
        for p in range(H_B // 2):
            slab = jnp.zeros((qb, 128), F32)
            for hh in range(2):
                h = 2 * p + hh
                qn = q_ref[rows, 128 * p:128 * p + 128] * _lane_mask(128, DN_B * hh, DN_B, BF16)
                qp = (q_ref[rows, 512 + 128 * (h // 4):512 + 128 * (h // 4) + 128]
                      * _lane_mask(128, DR_B * (h % 4), DR_B, BF16))
                s = _dot_nt(jnp.concatenate([qn, qp], axis=-1), kk_sc[p])
                slab = slab + _softmax_pv(s, vm_sc[h])
            ob_ref[rows, 128 * p:128 * p + 128] = slab
        return carry

    lax.fori_loop(0, seq // qb, body, 0)


def _mla_call(mq, mkv, ckv, layer, *, seq):
    n = mq.shape[0]
    ctx_len = 0 if ckv is None else ckv.shape[2]
    in_specs = [pl.BlockSpec((seq, 768), lambda i: (i, 0)), pl.BlockSpec((seq, 1152), lambda i: (i, 0))]
    args = [mq, mkv]
    if ctx_len:
        in_specs.append(pl.BlockSpec((1, 1, ctx_len, 1152), lambda i: (layer, i, 0, 0)))
        args.append(ckv)
    tk = seq + ctx_len
    return pl.pallas_call(
        functools.partial(_mla_kernel, seq=seq, ctx_len=ctx_len),
        grid=(n // seq,), in_specs=in_specs,
        out_specs=pl.BlockSpec((seq, W_B), lambda i: (i, 0)),
        out_shape=jax.ShapeDtypeStruct((n, W_B), F32),
        scratch_shapes=[pltpu.VMEM((H_B // 2, tk, 256), BF16), pltpu.VMEM((H_B, tk, 128), BF16)],
        compiler_params=_cparams(), name="mla_ctx" if ctx_len else "mla_self",
    )(*args)


def _diff_kernel(*refs, seq, ctx_len, lam_init):
    it = iter(refs)
    qkv_ref, lam_ref, g_ref = next(it), next(it), next(it)
    ck_ref, cv_ref = (next(it), next(it)) if ctx_len else (None, None)
    oc_ref = next(it)
    k_sc, vm_sc = next(it), next(it)
    kt_sc, vt_sc = (next(it), next(it)) if ctx_len else (None, None)

    lam = (jnp.exp(jnp.sum(lam_ref[0, 0:1, :] * lam_ref[0, 1:2, :], axis=-1, keepdims=True))
           - jnp.exp(jnp.sum(lam_ref[0, 2:3, :] * lam_ref[0, 3:4, :], axis=-1, keepdims=True)) + lam_init)
    dh = 2 * DC
    for p in range(H_C // 2):
        k_sc[p] = qkv_ref[:, 256 + 128 * p:256 + 128 * p + 128]
        for hh in range(2):
            h = 2 * p + hh
            vm_sc[h] = qkv_ref[:, 512 + 128 * p:512 + 128 * p + 128] * _lane_mask(128, dh * hh, dh, BF16)
            if ctx_len:
                kt_sc[p, dh * hh:dh * (hh + 1), :] = ck_ref[0, 0, h].astype(BF16)
                vt_sc[h, dh * hh:dh * (hh + 1), :] = cv_ref[0, 0, h].astype(BF16)
                vt_sc[h, dh * (1 - hh):dh * (2 - hh), :] = jnp.zeros((dh, ctx_len), BF16)

    def attend(qm, p, h):
        s = _dot_nt(qm, k_sc[p])
        if not ctx_len:
            return _softmax_pv(s, vm_sc[h])
        sc = _dot(qm, kt_sc[p])
        m = jnp.maximum(jnp.max(s, axis=-1, keepdims=True), jnp.max(sc, axis=-1, keepdims=True))
        e = jnp.exp(s - m)
        ec = jnp.exp(sc - m)
        l = jnp.sum(e, axis=-1, keepdims=True) + jnp.sum(ec, axis=-1, keepdims=True)
        return (_dot(e.astype(BF16), vm_sc[h]) + _dot_nt(ec.astype(BF16), vt_sc[h])) * (1.0 / l)

    qb = min(ATT_QBLOCK, seq)

    def body(i, carry):
        rows = pl.ds(pl.multiple_of(i * qb, qb), qb)
        for p in range(H_C // 2):
            slab = jnp.zeros((qb, 128), F32)
            qs = qkv_ref[rows, 128 * p:128 * p + 128]
            for hh in range(2):
                h = 2 * p + hh
                o1 = attend(qs * _lane_mask(128, dh * hh, DC, BF16), p, h)
                o2 = attend(qs * _lane_mask(128, dh * hh + DC, DC, BF16), p, h)
                slab = slab + (o1 - lam * o2)
            sq = slab * slab
            inv = jnp.zeros_like(slab)
            for hh in range(2):
                m = _lane_mask(128, dh * hh, dh, F32)
                ms = jnp.sum(sq * m, axis=-1, keepdims=True) * (1.0 / dh)
                inv = inv + lax.rsqrt(ms + EPS) * m
            oc_ref[rows, 128 * p:128 * p + 128] = (slab * inv * g_ref[0, :, 128 * p:128 * p + 128]
                                                    * (1.0 - lam_init))
        return carry

    lax.fori_loop(0, seq // qb, body, 0)


def _diff_call(dqkv, lamp, g4, ck_t, cv_t, layer, *, seq):
    n = dqkv.shape[0]
    ctx_len = 0 if ck_t is None else ck_t.shape[4]
    lam_init = 0.8 - 0.6 * math.exp(-0.3 * layer)
    in_specs = [pl.BlockSpec((seq, 768), lambda i: (i, 0)),
                _layer_spec(lamp.shape, layer), _layer_spec(g4.shape, layer)]
    args = [dqkv, lamp, g4]
    scratch = [pltpu.VMEM((H_C // 2, seq, 128), BF16), pltpu.VMEM((H_C, seq, 128), BF16)]
    if ctx_len:
        in_specs += [pl.BlockSpec((1, 1, H_C, 2 * DC, ctx_len), lambda i: (i, layer, 0, 0, 0))] * 2
        args += [ck_t, cv_t]
        scratch += [pltpu.VMEM((H_C // 2, 128, ctx_len), BF16), pltpu.VMEM((H_C, 128, ctx_len), BF16)]
    return pl.pallas_call(
        functools.partial(_diff_kernel, seq=seq, ctx_len=ctx_len, lam_init=lam_init),
        grid=(n // seq,), in_specs=in_specs,
        out_specs=pl.BlockSpec((seq, W_C), lambda i: (i, 0)),
        out_shape=jax.ShapeDtypeStruct((n, W_C), F32),
        scratch_shapes=scratch,
        compiler_params=_cparams(), name="diff_ctx" if ctx_len else "diff_self",
    )(*args)


def _post_kernel(oa_ref, ob_ref, oc_ref, gates_ref, x_ref, mod_ref, w_ref, g_ref, y_ref):
    mix = jnp.concatenate([oa_ref[...], ob_ref[...], oc_ref[...]], axis=-1) * gates_ref[...]
    out = _rms(_dot(mix.astype(BF16), w_ref[0]), g_ref[0])
    y_ref[...] = x_ref[...] + mod_ref[0, :, 2 * D_MODEL:3 * D_MODEL] * out


def _post_call(oa, ob, oc, gates, x2d, mod, pw, layer, *, seq, sample):
    n = x2d.shape[0]
    tm = min(TOKEN_BLOCK, n)
    blk = lambda w: pl.BlockSpec((tm, w), lambda i: (i, 0))

    def mod_idx(i):
        return (layer * MOD_ROWS + ((i * tm) // seq + 1 if sample else 0), 0, 0)

    return pl.pallas_call(
        _post_kernel,
        grid=(n // tm,),
        in_specs=[blk(W_A), blk(W_B), blk(W_C), blk(D_MODEL), blk(D_MODEL),
                  pl.BlockSpec((1, 1, 3 * D_MODEL), mod_idx),
                  _layer_spec(pw["w_out"].shape, layer), _layer_spec(pw["g_post"].shape, layer)],
        out_specs=blk(D_MODEL),
        out_shape=jax.ShapeDtypeStruct((n, D_MODEL), F32),
        compiler_params=_cparams(), name="post",
    )(oa, ob, oc, gates, x2d, mod, pw["w_out"], pw["g_post"])


def _rot_rows(w):
    dep, n, dm = w.shape
    w4 = w.reshape(dep, n // 16, 2, 8, dm)
    return jnp.concatenate([-w4[:, :, 1:2], w4[:, :, 0:1]], axis=2).reshape(dep, n, dm)


def _rot_cols(w):
    s = w.shape
    w4 = w.reshape(s[:-1] + (s[-1] // 16, 2, 8))
    return jnp.concatenate([-w4[..., 1:2, :], w4[..., 0:1, :]], axis=-2).reshape(s)


def _pack_params(g_pre, g_post, w_in, w_gla_af, b_gla_af, w_gla_ab, b_gla_ab, g_gla, g_mla_q, w_mla_uq,
                 g_mla_kv, w_mla_ukv, lam_q1, lam_k1, lam_q2, lam_k2, g_diff, w_out):
    w_t = jnp.swapaxes(w_in, 1, 2)
    kpe_t = w_t[:, R_MLA[1] - DR_B:R_MLA[1]]
    dq_t = w_t[:, R_DIFF[0]:R_DIFF[0] + 256]
    dk_t = w_t[:, R_DIFF[0] + 256:R_DIFF[0] + 512]
    w_x = jnp.concatenate([kpe_t] * 4 + [_rot_rows(kpe_t)] * 4 + [_rot_rows(dq_t), _rot_rows(dk_t)],
                          axis=1).astype(BF16)
    zg = jnp.zeros((DEPTH, GLA_LR, 128), F32)
    w_gate = jnp.concatenate([jnp.concatenate([w_gla_af, zg], axis=-1),
                              jnp.concatenate([zg, w_gla_ab], axis=-1)], axis=1).astype(BF16)
    uq = w_mla_uq.reshape(DEPTH, Q_LORA, H_B, DN_B + DR_B)
    w_pe = uq[..., DN_B:].reshape(DEPTH, Q_LORA, H_B * DR_B)
    w_uq = jnp.concatenate([uq[..., :DN_B].reshape(DEPTH, Q_LORA, H_B * DN_B), w_pe, _rot_cols(w_pe)],
                           axis=-1).astype(BF16)
    ukv = w_mla_ukv.reshape(DEPTH, KV_LORA, H_B, DN_B + DV_B)
    w_ukv = jnp.concatenate([ukv[..., :DN_B].reshape(DEPTH, KV_LORA, H_B * DN_B),
                             ukv[..., DN_B:].reshape(DEPTH, KV_LORA, H_B * DV_B)], axis=-1).astype(BF16)
    row = lambda a: a.reshape(DEPTH, 1, a.shape[-1])
    return dict(
        w_t=w_t.astype(BF16), w_x=w_x, w_gate=w_gate,
        b_gate=row(jnp.concatenate([b_gla_af, b_gla_ab], axis=-1)),
        w_uq=w_uq, w_ukv=w_ukv, w_out=w_out.astype(BF16),
        g_pre=row(g_pre), g_post=row(g_post), g_mla_q=row(g_mla_q), g_mla_kv=row(g_mla_kv),
        g_gla4=row(jnp.tile(g_gla, (1, H_A))), g_diff4=row(jnp.tile(g_diff, (1, H_C))),
        lam=jnp.stack([lam_q1, lam_k1, lam_q2, lam_k2], axis=1))


def _rope_tables(n):
    t = np.arange(n)
    row = (t // GRID_W).astype(np.float32)
    col = (t % GRID_W).astype(np.float32)
    half = ROPE_DIM // 2
    inv = (1.0 / (np.float32(ROPE_THETA) ** (np.arange(0, half, 2, dtype=np.float32) / np.float32(half)))
           ).astype(np.float32)
    ar = row[:, None] * inv
    ac = col[:, None] * inv
    ang = np.concatenate([ar, ar, ac, ac], axis=-1).astype(np.float32)
    return (jnp.asarray(np.tile(np.cos(ang), (1, 8)).astype(np.float32)),
            jnp.asarray(np.tile(np.sin(ang), (1, 8)).astype(np.float32)))


def _sublayer(x2d, mod, pw, layer, *, seq, rope_tabs, ctx, caches):
    sample = ctx is not None
    pre = _pre_call(x2d, mod, pw, layer, seq=seq, rope_tabs=rope_tabs,
                    caches=None if sample else caches[:4])
    qk, v, la, gates, mq, mkv, dqkv = pre[:7]
    if sample:
        (oa,) = _gla_call(qk, v, la, pw["g_gla4"], layer, seq=seq, state_in=ctx["state"])
        ob = _mla_call(mq, mkv, ctx["mla_kv"], layer, seq=seq)
        oc = _diff_call(dqkv, pw["lam"], pw["g_diff4"], ctx["diff_k_t"], ctx["diff_v_t"], layer, seq=seq)
        new_caches = None
    else:
        oa, sfin = _gla_call(qk, v, la, pw["g_gla4"], layer, seq=seq,
                             state_out=caches[4] if caches else None)
        ob = _mla_call(mq, mkv, None, layer, seq=seq)
        oc = _diff_call(dqkv, pw["lam"], pw["g_diff4"], None, None, layer, seq=seq)
        new_caches = tuple(pre[7:]) + (sfin,)
    y = _post_call(oa, ob, oc, gates, x2d, mod, pw, layer, seq=seq, sample=sample)
    return y, new_caches


def kernel(x_prompt, x_sample, c, cache_mla_ckv, cache_mla_kpe, cache_diff_k, cache_diff_v, state_gla,
           c_ctx, w_ada, b_ada, g_pre, g_post, w_in, w_gla_af, b_gla_af, w_gla_ab, b_gla_ab, g_gla,
           g_mla_q, w_mla_uq, g_mla_kv, w_mla_ukv, lam_q1, lam_k1, lam_q2, lam_k2, g_diff, w_out):
    bp, tp, d = x_prompt.shape
    bs, ts, _ = x_sample.shape

    pw = _pack_params(g_pre, g_post, w_in, w_gla_af, b_gla_af, w_gla_ab, b_gla_ab, g_gla, g_mla_q,
                      w_mla_uq, g_mla_kv, w_mla_ukv, lam_q1, lam_k1, lam_q2, lam_k2, g_diff, w_out)
    cvecs = jnp.concatenate([c_ctx[None], c, jnp.zeros((MOD_ROWS - 1 - bs, d), F32)], axis=0)
    mod = _mod_call(cvecs, w_ada, b_ada).reshape(DEPTH * MOD_ROWS, 1, 3 * d)
    rope_tabs = _rope_tables(ts)
    ctx = dict(state=state_gla,
               mla_kv=_ctxkv_call(cache_mla_ckv, jnp.swapaxes(cache_mla_kpe, -1, -2), pw["w_ukv"]),
               diff_k_t=jnp.swapaxes(cache_diff_k, -1, -2), diff_v_t=jnp.swapaxes(cache_diff_v, -1, -2))

    y_p = x_prompt.reshape(bp * tp, d)
    y_s = x_sample.reshape(bs * ts, d)
    caches = ()
    for l in range(DEPTH):
        y_p, caches = _sublayer(y_p, mod, pw, l, seq=tp, rope_tabs=None, ctx=None, caches=caches)
        y_s, _ = _sublayer(y_s, mod, pw, l, seq=ts, rope_tabs=rope_tabs, ctx=ctx, caches=None)
    ckvn, kpe_t, kc_t, vc_t, new_state = caches
    return (y_p.reshape(bp, tp, d), y_s.reshape(bs, ts, d), ckvn, jnp.swapaxes(kpe_t, -1, -2),
            jnp.swapaxes(kc_t, -1, -2), jnp.swapaxes(vc_t, -1, -2), new_state)
```

```python
import functools
import math

import numpy as np
import jax
import jax.numpy as jnp
from jax import lax
from jax.experimental import pallas as pl
from jax.experimental.pallas import tpu as pltpu

F32 = jnp.float32
BF16 = jnp.bfloat16

D_MODEL = 1024
DEPTH = 2
GRID_W = 64
EPS = 1e-6
ROPE_THETA = 10000.0
ROPE_DIM = 32
H_A, DK_A, DV_A = 4, 32, 64
GLA_LR = 16
GLA_TAU = 16.0
GLA_CHUNK = 64
H_B, DN_B, DR_B, DV_B = 8, 64, 32, 64
Q_LORA, KV_LORA = 256, 128
H_C, DC = 4, 32
W_A, W_B, W_C = H_A * DV_A, H_B * DV_B, H_C * 2 * DC
ST_R, ST_C = H_A * DV_A, H_A * DK_A
LOG2E = math.log2(math.e)

R_GLA = (0, 544)
R_GG = (544, 800)
R_MLA = (800, 1216)
R_MG = (1216, 1728)
R_DIFF = (1728, 2752)
N_IN = 2752
X_KPE4, X_KPE4_ROT, X_DQ_ROT, X_DK_ROT, N_EXTRA = 0, 128, 256, 512, 768
MOD_ROWS = 8

V7X_VMEM_LIMIT_BYTES = 56 * 1024 * 1024
TOKEN_BLOCK = 512
ATT_QBLOCK = 512


def _cparams(n_axes=1):
    return pltpu.CompilerParams(dimension_semantics=("arbitrary",) * n_axes,
                                vmem_limit_bytes=V7X_VMEM_LIMIT_BYTES)


def _rms(x, g):
    return x * lax.rsqrt(jnp.mean(x * x, axis=-1, keepdims=True) + EPS) * g


def _silu(x):
    return x * jax.nn.sigmoid(x)


def _log_sigmoid(x):
    return jnp.minimum(x, 0.0) - jnp.log1p(jnp.exp(-jnp.abs(x)))


def _lane_mask(width, lo, size, dtype):
    lane = lax.broadcasted_iota(jnp.int32, (1, width), 1)
    return jnp.where((lane >= lo) & (lane < lo + size), 1.0, 0.0).astype(dtype)


def _dot(a, b):
    return jnp.dot(a, b, preferred_element_type=F32)


def _dot_nt(a, b):
    return lax.dot_general(a, b, (((1,), (1,)), ((), ())), preferred_element_type=F32)


def _dot_tn(a, b):
    return lax.dot_general(a, b, (((0,), (0,)), ((), ())), preferred_element_type=F32)


def _layer_spec(shape, layer):
    nd = len(shape)
    return pl.BlockSpec((1,) + tuple(shape[1:]), lambda *_: (layer,) + (0,) * (nd - 1))


_ANY = pl.BlockSpec(memory_space=pl.ANY)


def _mod_kernel(c_ref, w_ref, b_ref, o_ref):
    s = _silu(c_ref[...]).astype(BF16)
    o_ref[0] = _dot(s, w_ref[0].astype(BF16)) + b_ref[0]


def _mod_call(cvecs, w_ada, b_ada):
    nb = 1024
    return pl.pallas_call(
        _mod_kernel,
        grid=(DEPTH, 3 * D_MODEL // nb),
        in_specs=[pl.BlockSpec((MOD_ROWS, D_MODEL), lambda l, j: (0, 0)),
                  pl.BlockSpec((1, D_MODEL, nb), lambda l, j: (l, 0, j)),
                  pl.BlockSpec((1, 1, nb), lambda l, j: (l, 0, j))],
        out_specs=pl.BlockSpec((1, MOD_ROWS, nb), lambda l, j: (l, 0, j)),
        out_shape=jax.ShapeDtypeStruct((DEPTH, MOD_ROWS, 3 * D_MODEL), F32),
        compiler_params=_cparams(2), name="adaln_mod",
    )(cvecs, w_ada, b_ada.reshape(DEPTH, 1, 3 * D_MODEL))


def _ctxkv_kernel(ckv_ref, kpe_ref, w_ref, k_ref, vt_ref):
    kv = _dot(ckv_ref[0, 0].astype(BF16), w_ref[0])
    kpe4 = jnp.concatenate([kpe_ref[0, 0]] * 4, axis=0).T
    k_ref[0, 0, :, 0:512] = kv[:, 0:512].astype(BF16)
    k_ref[0, 0, :, 512:640] = kpe4.astype(BF16)
    vt_ref[0, 0] = kv[:, 512:1024].T.astype(BF16)


def _ctxkv_call(cache_ckv, cache_kpe_t, wukv):
    nb, _, tc, _ = cache_ckv.shape
    return pl.pallas_call(
        _ctxkv_kernel,
        grid=(DEPTH, nb),
        in_specs=[pl.BlockSpec((1, 1, tc, KV_LORA), lambda l, b: (b, l, 0, 0)),
                  pl.BlockSpec((1, 1, DR_B, tc), lambda l, b: (b, l, 0, 0)),
                  pl.BlockSpec((1, KV_LORA, 1024), lambda l, b: (l, 0, 0))],
        out_specs=[pl.BlockSpec((1, 1, tc, 640), lambda l, b: (l, b, 0, 0)),
                   pl.BlockSpec((1, 1, W_B, tc), lambda l, b: (l, b, 0, 0))],
        out_shape=[jax.ShapeDtypeStruct((DEPTH, nb, tc, 640), BF16),
                   jax.ShapeDtypeStruct((DEPTH, nb, W_B, tc), BF16)],
        compiler_params=_cparams(2), name="mla_ctx_kv",
    )(cache_ckv, cache_kpe_t, wukv)


def _pre_kernel(*refs, rope, ctx_out, alias_in, bpb, seq):
    it = iter(refs)
    (x_ref, mod_ref, gpre_ref, w_ref, wx_ref, wg_ref, bg_ref, gq_ref, gkv_ref, wuq_ref,
     wukv_ref) = (next(it) for _ in range(11))
    if rope:
        cos_ref, sin_ref = next(it), next(it)
    for _ in range(alias_in):
        next(it)
    (qk_ref, v_ref, la_ref, ga_ref, gb_ref, gc_ref, mq_ref, mk_ref, mvt_ref, dqk_ref,
     dvt_ref) = (next(it) for _ in range(11))
    if ctx_out:
        ckvn_ref, kpe_ref, kc_ref, vc_ref = (next(it) for _ in range(4))

    d = D_MODEL
    shift = mod_ref[0, :, 0:d]
    scale = mod_ref[0, :, d:2 * d]
    h = (_rms(x_ref[...], gpre_ref[0]) * (1.0 + scale) + shift).astype(BF16)
    proj = lambda r: _dot_nt(h, w_ref[0, r[0]:r[1], :])
    projx = lambda lo, hi: _dot_nt(h, wx_ref[0, lo:hi, :])
    if rope:
        cos = cos_ref[...]
        sin = sin_ref[...]

    pg = proj(R_GLA)
    qk_ref[:, 0:128] = pg[:, 0:128] * (DK_A ** -0.5)
    qk_ref[:, 128:256] = pg[:, 128:256]
    v_ref[...] = pg[:, 256:512]
    xg = _dot(pg[:, 512:512 + 2 * GLA_LR].astype(BF16), wg_ref[0]) + bg_ref[0]
    la_ref[...] = _log_sigmoid(xg) * (1.0 / GLA_TAU)
    ga_ref[...] = _silu(proj(R_GG)).astype(BF16)

    pm = proj(R_MLA)
    qall = _dot(_rms(pm[:, 0:256], gq_ref[0]).astype(BF16), wuq_ref[0])
    q_pe = qall[:, 512:768]
    if rope:
        q_pe = q_pe * cos + qall[:, 768:1024] * sin
    sb = (DN_B + DR_B) ** -0.5 * LOG2E
    mq_ref[:, 0:512] = (qall[:, 0:512] * sb).astype(BF16)
    mq_ref[:, 512:768] = (q_pe * sb).astype(BF16)
    ckvn = _rms(pm[:, 256:384], gkv_ref[0])
    kvall = _dot(ckvn.astype(BF16), wukv_ref[0])
    if rope:
        px = projx(X_KPE4, X_KPE4_ROT + 128)
        kpe4 = px[:, 0:128] * cos[:, 0:128] + px[:, 128:256] * sin[:, 0:128]
    else:
        kpe4 = projx(X_KPE4, X_KPE4 + 128)
    mk_ref[:, 0:512] = kvall[:, 0:512].astype(BF16)
    mk_ref[:, 512:640] = kpe4.astype(BF16)
    mvt_ref[...] = kvall[:, 512:1024].T.astype(BF16)
    gb_ref[...] = _silu(proj(R_MG)).astype(BF16)

    pd = proj(R_DIFF)
    dq, dk, dv = pd[:, 0:256], pd[:, 256:512], pd[:, 512:768]
    if rope:
        pr = projx(X_DQ_ROT, X_DK_ROT + 256)
        dq = dq * cos + pr[:, 0:256] * sin
        dk = dk * cos + pr[:, 256:512] * sin
    dqk_ref[:, 0:256] = (dq * (DC ** -0.5 * LOG2E)).astype(BF16)
    dqk_ref[:, 256:512] = dk.astype(BF16)
    dv_t = dv.T
    dvt_ref[...] = dv_t.astype(BF16)
    gc_ref[...] = _silu(pd[:, 768:1024]).astype(BF16)
    if ctx_out:
        kpe_t = kpe4.T
        dk_t = dk.T
        for bb in range(bpb):
            rs = slice(bb * seq, (bb + 1) * seq)
            ckvn_ref[bb, 0] = ckvn[rs]
            kpe_ref[bb, 0] = kpe_t[0:DR_B, rs]
            kc_ref[bb, 0] = dk_t[:, rs].reshape(H_C, 2 * DC, seq)
            vc_ref[bb, 0] = dv_t[:, rs].reshape(H_C, 2 * DC, seq)


def _pre_call(x2d, mod, pw, layer, *, seq, rope_tabs, caches):
    n = x2d.shape[0]
    tm = min(TOKEN_BLOCK, n)
    bpb = max(tm // seq, 1)
    rope = rope_tabs is not None
    ctx_out = caches is not None
    steps_per_seq = max(seq // tm, 1)
    nbt = n // seq

    def mod_idx(i):
        return (layer * MOD_ROWS + ((i * tm) // seq + 1 if rope else 0), 0, 0)

    names = ["g_pre", "w_t", "w_x", "w_gate", "b_gate", "g_mla_q", "g_mla_kv", "w_uq", "w_ukv"]
    in_specs = [pl.BlockSpec((tm, D_MODEL), lambda i: (i, 0)), pl.BlockSpec((1, 1, 3 * D_MODEL), mod_idx)]
    in_specs += [_layer_spec(pw[k].shape, layer) for k in names]
    args = [x2d, mod] + [pw[k] for k in names]
    if rope:
        in_specs += [pl.BlockSpec((tm, 256), lambda i: (i % steps_per_seq, 0))] * 2
        args += list(rope_tabs)
    outs = [(256, F32, False), (256, F32, False), (256, F32, False),
            (W_A, BF16, False), (W_B, BF16, False), (W_C, BF16, False),
            (768, BF16, False), (640, BF16, False), (W_B, BF16, True),
            (512, BF16, False), (W_C, BF16, True)]
    out_specs = [pl.BlockSpec((w, tm), lambda i: (0, i)) if tr else pl.BlockSpec((tm, w), lambda i: (i, 0))
                 for w, _, tr in outs]
    out_shape = [jax.ShapeDtypeStruct((w, n) if tr else (n, w), dt) for w, dt, tr in outs]
    widths = outs
    aliases = {}
    if ctx_out:
        out_specs += [pl.BlockSpec((bpb, 1, seq, KV_LORA), lambda i: (i, layer, 0, 0)),
                      pl.BlockSpec((bpb, 1, DR_B, seq), lambda i: (i, layer, 0, 0)),
                      pl.BlockSpec((bpb, 1, H_C, 2 * DC, seq), lambda i: (i, layer, 0, 0, 0)),
                      pl.BlockSpec((bpb, 1, H_C, 2 * DC, seq), lambda i: (i, layer, 0, 0, 0))]
        out_shape += [jax.ShapeDtypeStruct((nbt, DEPTH, seq, KV_LORA), F32),
                      jax.ShapeDtypeStruct((nbt, DEPTH, DR_B, seq), F32),
                      jax.ShapeDtypeStruct((nbt, DEPTH, H_C, 2 * DC, seq), F32),
                      jax.ShapeDtypeStruct((nbt, DEPTH, H_C, 2 * DC, seq), F32)]
        for j, arr in enumerate(caches):
            aliases[len(args)] = len(widths) + j
            in_specs.append(_ANY)
            args.append(arr)
    return pl.pallas_call(
        functools.partial(_pre_kernel, rope=rope, ctx_out=ctx_out, alias_in=len(aliases), bpb=bpb, seq=seq),
        grid=(n // tm,), in_specs=in_specs, out_specs=out_specs, out_shape=out_shape,
        input_output_aliases=aliases,
        compiler_params=_cparams(), name="pre_rope" if rope else "pre_ctx",
    )(*args)


_GLA_LEVELS = (1, 2, 4, 8, 16, 32)


def _gla_consts(rev):
    c = GLA_CHUNK
    row = lax.broadcasted_iota(jnp.int32, (c, 128), 0)
    pos = (c - 1 - row) if rev else row
    ri = lax.broadcasted_iota(jnp.int32, (c, H_A * c), 0)
    cj = lax.broadcasted_iota(jnp.int32, (c, H_A * c), 1) & (c - 1)
    pi = (c - 1 - ri) if rev else ri
    pj = (c - 1 - cj) if rev else cj
    x = pi ^ pj
    lvl = jnp.where(pi == pj, 0, -1)
    for kbit in range(6):
        lvl = jnp.where((pj < pi) & ((x >> kbit) == 1), kbit + 1, lvl)
    return pos, lvl


def _gla_chunk(q, k, v, la, b, st_prev, pos, lvl, hm_bf, hm_f32, vm_bf, rev):
    c = GLA_CHUNK
    prv = pltpu.roll(la, c - 1 if rev else 1, 0)
    nxt = pltpu.roll(la, 1 if rev else c - 1, 0)
    s_tot = jnp.where(lvl == 0, _dot_nt(q.astype(BF16), jnp.concatenate([k.astype(BF16)] * H_A, 0) * hm_bf), 0.0)
    for kbit, m in enumerate(_GLA_LEVELS):
        up = ((pos >> kbit) & 1) == 1
        if m == 1:
            e = jnp.where(up, la, 0.0)
        elif m == 2:
            c4 = pos & 3
            e = jnp.where(c4 == 0, nxt, jnp.where(c4 == 1, 0.0, jnp.where(c4 == 2, la, la + prv)))
        else:
            nblk = c // (2 * m)
            loc = m if rev else m - 1
            b3 = b.reshape(nblk, 2 * m, 128)
            ref = jnp.broadcast_to(b3[:, loc:loc + 1, :], (nblk, 2 * m, 128)).reshape(c, 128)
            dlt = b - ref
            e = jnp.where(up, dlt, -dlt)
        xm = (jnp.where(up, q, k) * jnp.exp(e)).astype(BF16)
        sm = _dot_nt(xm, jnp.concatenate([xm] * H_A, 0) * hm_bf)
        s_tot = jnp.where(lvl == kbit + 1, sm, s_tot)
    vbd = jnp.concatenate([v] * H_A, 0) * vm_bf
    blast = b[0:1, :] if rev else b[c - 1:c, :]
    qbar = (q * jnp.exp(b)).astype(BF16)
    kdec = (k * jnp.exp(blast - b)).astype(BF16)
    o = _dot(s_tot.astype(BF16), vbd) + _dot_nt(qbar, st_prev.astype(BF16))
    st_new = st_prev * jnp.exp(blast) + _dot_tn(v, kdec) * hm_f32
    return o, st_new


def _gla_kernel(*refs, seq, has_s0, alias_in):
    it = iter(refs)
    qk_ref, v_ref, la_ref, gate_ref, g_ref = (next(it) for _ in range(5))
    s0_ref = next(it) if has_s0 else None
    for _ in range(alias_in):
        next(it)
    oa_ref = next(it)
    sfin_ref = None if has_s0 else next(it)
    b_sc, acc_sc, st_sc, blk_sc = next(it), next(it), next(it), next(it)

    c = GLA_CHUNK
    nc = seq // c
    rowc = lax.broadcasted_iota(jnp.int32, (seq, 128), 0) & (c - 1)
    bf = la_ref[:, 0:128]
    bb = la_ref[:, 128:256]
    s = 1
    while s < c:
        bf = bf + jnp.where(rowc >= s, pltpu.roll(bf, s, 0), 0.0)
        bb = bb + jnp.where(rowc < c - s, pltpu.roll(bb, seq - s, 0), 0.0)
        s *= 2
    b_sc[0] = bf
    b_sc[1] = bb
    acc_sc[...] = jnp.zeros_like(acc_sc)
    for d in range(2):
        if has_s0:
            blk_sc[...] = jnp.zeros_like(blk_sc)
            for hh in range(H_A):
                blk_sc[DK_A * hh:DK_A * (hh + 1), DV_A * hh:DV_A * (hh + 1)] = s0_ref[0, 0, d, hh]
            st_sc[d] = blk_sc[...].T
        else:
            st_sc[d] = jnp.zeros((ST_R, ST_C), F32)

    hrow = lax.broadcasted_iota(jnp.int32, (H_A * c, 128), 0) // c
    hm_f32 = jnp.where(hrow == lax.broadcasted_iota(jnp.int32, (H_A * c, 128), 1) // DK_A, 1.0, 0.0)
    hm_bf = hm_f32.astype(BF16)
    vrow = lax.broadcasted_iota(jnp.int32, (H_A * c, H_A * DV_A), 0) // c
    vm_bf = jnp.where(vrow == lax.broadcasted_iota(jnp.int32, (H_A * c, H_A * DV_A), 1) // DV_A,
                      1.0, 0.0).astype(BF16)
    consts = (_gla_consts(False), _gla_consts(True))

    def body(n, carry):
        for d, rev in ((0, False), (1, True)):
            cn = (nc - 1 - n) if rev else n
            rows = pl.ds(pl.multiple_of(cn * c, c), c)
            pos, lvl = consts[d]
            o, st_new = _gla_chunk(qk_ref[rows, 0:128], qk_ref[rows, 128:256],
                                   v_ref[rows, :].astype(BF16), la_ref[rows, 128 * d:128 * d + 128],
                                   b_sc[d, rows, :], st_sc[d], pos, lvl, hm_bf, hm_f32, vm_bf, rev)
            acc_sc[rows, :] = acc_sc[rows, :] + o
            st_sc[d] = st_new
        return carry

    lax.fori_loop(0, nc, body, 0)

    oa = acc_sc[...]
    sq = oa * oa
    inv = jnp.zeros_like(oa)
    for hh in range(H_A):
        m = _lane_mask(W_A, DV_A * hh, DV_A, F32)
        ms = jnp.sum(sq * m, axis=-1, keepdims=True) * (1.0 / DV_A)
        inv = inv + lax.rsqrt(ms + EPS) * m
    oa_ref[...] = (oa * inv * g_ref[0] * gate_ref[...]).astype(BF16)
    if not has_s0:
        for d in range(2):
            blk_sc[...] = st_sc[d].T
            for hh in range(H_A):
                sfin_ref[0, 0, d, hh] = blk_sc[DK_A * hh:DK_A * (hh + 1), DV_A * hh:DV_A * (hh + 1)]


def _gla_call(qk, v, la, gate, g4, layer, *, seq, state_in=None, state_out=None):
    n = qk.shape[0]
    nb = n // seq
    has_s0 = state_in is not None
    blk = lambda w: pl.BlockSpec((seq, w), lambda i: (i, 0))
    st_spec = pl.BlockSpec((1, 1, 2, H_A, DK_A, DV_A), lambda i: (i, layer, 0, 0, 0, 0))
    in_specs = [blk(256), blk(256), blk(256), blk(W_A), _layer_spec(g4.shape, layer)]
    args = [qk, v, la, gate, g4]
    out_specs = [blk(W_A)]
    out_shape = [jax.ShapeDtypeStruct((n, W_A), BF16)]
    aliases = {}
    if has_s0:
        in_specs.append(st_spec)
        args.append(state_in)
    else:
        out_specs.append(st_spec)
        out_shape.append(jax.ShapeDtypeStruct((nb, DEPTH, 2, H_A, DK_A, DV_A), F32))
        if state_out is not None:
            aliases[len(args)] = 1
            in_specs.append(_ANY)
            args.append(state_out)
    return pl.pallas_call(
        functools.partial(_gla_kernel, seq=seq, has_s0=has_s0, alias_in=len(aliases)),
        grid=(nb,), in_specs=in_specs, out_specs=out_specs, out_shape=out_shape,
        input_output_aliases=aliases,
        scratch_shapes=[pltpu.VMEM((2, seq, 128), F32), pltpu.VMEM((seq, W_A), F32),
                        pltpu.VMEM((2, ST_R, ST_C), F32), pltpu.VMEM((ST_C, ST_R), F32)],
        compiler_params=_cparams(), name="gla_state" if has_s0 else "gla_ctx",
    )(*args)


def _softmax_t_pv(st, vt):
    m = _col_reduce(st, jnp.max)
    e = jnp.exp2(st - m)
    l = _col_reduce(e, jnp.sum)
    return _dot(vt, e.astype(BF16)) * (1.0 / l)


def _pipelined_attention(score_fns, value_fns):
    outs = []
    st_next = score_fns[0]()
    for j, vfn in enumerate(value_fns):
        st = st_next
        if j + 1 < len(score_fns):
            st_next = score_fns[j + 1]()
        outs.append(_softmax_t_pv(st, vfn()))
    return outs


def _col_reduce(x, op):
    rows, cols = x.shape
    part = 128 if rows % 128 == 0 and rows > 128 else rows
    if part != rows:
        x = op(x.reshape(rows // part, part, cols), axis=0)
    return op(x, axis=0, keepdims=True)


def _mla_kernel(*refs, seq, ctx_len):
    it = iter(refs)
    q_ref, k_ref, vt_ref, gate_ref = (next(it) for _ in range(4))
    ck_ref, cvt_ref = (next(it), next(it)) if ctx_len else (None, None)
    ob_ref = next(it)
    kk_sc, vt_sc = next(it), next(it)

    for p in range(H_B // 2):
        kk_sc[p, 0:seq, 0:128] = k_ref[:, 128 * p:128 * p + 128]
        kk_sc[p, 0:seq, 128:256] = k_ref[:, 512:640]
        if ctx_len:
            kk_sc[p, seq:seq + ctx_len, 0:128] = ck_ref[0, 0, :, 128 * p:128 * p + 128]
            kk_sc[p, seq:seq + ctx_len, 128:256] = ck_ref[0, 0, :, 512:640]
    vt_sc[:, 0:seq] = vt_ref[...]
    if ctx_len:
        vt_sc[:, seq:seq + ctx_len] = cvt_ref[0, 0]

    qb = min(ATT_QBLOCK, seq)

    def body(i, carry):
        rows = pl.ds(pl.multiple_of(i * qb, qb), qb)
        def scores(h):
            p, hh = divmod(h, 2)
            qn = q_ref[rows, 128 * p:128 * p + 128] * _lane_mask(128, DN_B * hh, DN_B, BF16)
            qp = (q_ref[rows, 512 + 128 * (h // 4):512 + 128 * (h // 4) + 128]
                  * _lane_mask(128, DR_B * (h % 4), DR_B, BF16))
            return _dot_nt(kk_sc[p], jnp.concatenate([qn, qp], axis=-1))

        outs = _pipelined_attention(
            [functools.partial(scores, h) for h in range(H_B)],
            [functools.partial(lambda h: vt_sc[DV_B * h:DV_B * (h + 1), :], h) for h in range(H_B)])
        ob = jnp.concatenate(outs, axis=0).T
        ob_ref[rows, :] = (ob * gate_ref[rows, :]).astype(BF16)
        return carry

    lax.fori_loop(0, seq // qb, body, 0)


def _mla_call(mq, mk, mvt, gate, ctx_k, ctx_vt, layer, *, seq):
    n = mq.shape[0]
    ctx_len = 0 if ctx_k is None else ctx_k.shape[2]
    in_specs = [pl.BlockSpec((seq, 768), lambda i: (i, 0)), pl.BlockSpec((seq, 640), lambda i: (i, 0)),
                pl.BlockSpec((W_B, seq), lambda i: (0, i)), pl.BlockSpec((seq, W_B), lambda i: (i, 0))]
    args = [mq, mk, mvt, gate]
    if ctx_len:
        in_specs += [pl.BlockSpec((1, 1, ctx_len, 640), lambda i: (layer, i, 0, 0)),
                     pl.BlockSpec((1, 1, W_B, ctx_len), lambda i: (layer, i, 0, 0))]
        args += [ctx_k, ctx_vt]
    tk = seq + ctx_len
    return pl.pallas_call(
        functools.partial(_mla_kernel, seq=seq, ctx_len=ctx_len),
        grid=(n // seq,), in_specs=in_specs,
        out_specs=pl.BlockSpec((seq, W_B), lambda i: (i, 0)),
        out_shape=jax.ShapeDtypeStruct((n, W_B), BF16),
        scratch_shapes=[pltpu.VMEM((H_B // 2, tk, 256), BF16), pltpu.VMEM((W_B, tk), BF16)],
        compiler_params=_cparams(), name="mla_ctx" if ctx_len else "mla_self",
    )(*args)


def _diff_kernel(*refs, seq, ctx_len, lam_init):
    it = iter(refs)
    qk_ref, vt_ref, gate_ref, lam_ref, g_ref = (next(it) for _ in range(5))
    ck_ref, cv_ref = (next(it), next(it)) if ctx_len else (None, None)
    oc_ref = next(it)
    k_sc, vt_sc = next(it), next(it)

    lam = (jnp.exp(jnp.sum(lam_ref[0, 0:1, :] * lam_ref[0, 1:2, :], axis=-1, keepdims=True))
           - jnp.exp(jnp.sum(lam_ref[0, 2:3, :] * lam_ref[0, 3:4, :], axis=-1, keepdims=True)) + lam_init)
    dh = 2 * DC
    for p in range(H_C // 2):
        k_sc[p, 0:seq, :] = qk_ref[:, 256 + 128 * p:256 + 128 * p + 128]
        if ctx_len:
            pair_t = jnp.concatenate([ck_ref[0, 0, 2 * p], ck_ref[0, 0, 2 * p + 1]], axis=0)
            k_sc[p, seq:seq + ctx_len, :] = pair_t.T.astype(BF16)
    vt_sc[:, 0:seq] = vt_ref[...]
    if ctx_len:
        for h in range(H_C):
            vt_sc[dh * h:dh * (h + 1), seq:seq + ctx_len] = cv_ref[0, 0, h].astype(BF16)

    qb = min(ATT_QBLOCK, seq)

    def body(i, carry):
        rows = pl.ds(pl.multiple_of(i * qb, qb), qb)
        def scores(h, comp):
            p, hh = divmod(h, 2)
            qm = qk_ref[rows, 128 * p:128 * p + 128] * _lane_mask(128, dh * hh + DC * comp, DC, BF16)
            return _dot_nt(k_sc[p], qm)

        items = [(h, comp) for h in range(H_C) for comp in range(2)]
        o12 = _pipelined_attention(
            [functools.partial(scores, h, comp) for h, comp in items],
            [functools.partial(lambda h: vt_sc[dh * h:dh * (h + 1), :], h) for h, _ in items])
        outs = []
        for h in range(H_C):
            ot = o12[2 * h] - lam * o12[2 * h + 1]
            outs.append(ot * lax.rsqrt(jnp.mean(ot * ot, axis=0, keepdims=True) + EPS))
        oc = jnp.concatenate(outs, axis=0).T
        oc_ref[rows, :] = (oc * g_ref[0] * (1.0 - lam_init) * gate_ref[rows, :]).astype(BF16)
        return carry

    lax.fori_loop(0, seq // qb, body, 0)


def _diff_call(dqk, dvt, gate, lamp, g4, ck_t, cv_t, layer, *, seq):
    n = dqk.shape[0]
    ctx_len = 0 if ck_t is None else ck_t.shape[4]
    lam_init = 0.8 - 0.6 * math.exp(-0.3 * layer)
    in_specs = [pl.BlockSpec((seq, 512), lambda i: (i, 0)), pl.BlockSpec((W_C, seq), lambda i: (0, i)),
                pl.BlockSpec((seq, W_C), lambda i: (i, 0)),
                _layer_spec(lamp.shape, layer), _layer_spec(g4.shape, layer)]
    args = [dqk, dvt, gate, lamp, g4]
    if ctx_len:
        in_specs += [pl.BlockSpec((1, 1, H_C, 2 * DC, ctx_len), lambda i: (i, layer, 0, 0, 0))] * 2
        args += [ck_t, cv_t]
    tk = seq + ctx_len
    return pl.pallas_call(
        functools.partial(_diff_kernel, seq=seq, ctx_len=ctx_len, lam_init=lam_init),
        grid=(n // seq,), in_specs=in_specs,
        out_specs=pl.BlockSpec((seq, W_C), lambda i: (i, 0)),
        out_shape=jax.ShapeDtypeStruct((n, W_C), BF16),
        scratch_shapes=[pltpu.VMEM((H_C // 2, tk, 128), BF16), pltpu.VMEM((W_C, tk), BF16)],
        compiler_params=_cparams(), name="diff_ctx" if ctx_len else "diff_self",
    )(*args)


def _post_kernel(oa_ref, ob_ref, oc_ref, x_ref, mod_ref, w_ref, g_ref, y_ref):
    mix = jnp.concatenate([oa_ref[...], ob_ref[...], oc_ref[...]], axis=-1)
    out = _rms(_dot(mix, w_ref[0]), g_ref[0])
    y_ref[...] = x_ref[...] + mod_ref[0, :, 2 * D_MODEL:3 * D_MODEL] * out


def _post_call(oa, ob, oc, x2d, mod, pw, layer, *, seq, sample):
    n = x2d.shape[0]
    tm = min(TOKEN_BLOCK, n)
    blk = lambda w: pl.BlockSpec((tm, w), lambda i: (i, 0))

    def mod_idx(i):
        return (layer * MOD_ROWS + ((i * tm) // seq + 1 if sample else 0), 0, 0)

    return pl.pallas_call(
        _post_kernel,
        grid=(n // tm,),
        in_specs=[blk(W_A), blk(W_B), blk(W_C), blk(D_MODEL),
                  pl.BlockSpec((1, 1, 3 * D_MODEL), mod_idx),
                  _layer_spec(pw["w_out"].shape, layer), _layer_spec(pw["g_post"].shape, layer)],
        out_specs=blk(D_MODEL),
        out_shape=jax.ShapeDtypeStruct((n, D_MODEL), F32),
        compiler_params=_cparams(), name="post",
    )(oa, ob, oc, x2d, mod, pw["w_out"], pw["g_post"])


def _rot_rows(w):
    dep, n, dm = w.shape
    w4 = w.reshape(dep, n // 16, 2, 8, dm)
    return jnp.concatenate([-w4[:, :, 1:2], w4[:, :, 0:1]], axis=2).reshape(dep, n, dm)


def _rot_cols(w):
    s = w.shape
    w4 = w.reshape(s[:-1] + (s[-1] // 16, 2, 8))
    return jnp.concatenate([-w4[..., 1:2, :], w4[..., 0:1, :]], axis=-2).reshape(s)


def _pack_params(g_pre, g_post, w_in, w_gla_af, b_gla_af, w_gla_ab, b_gla_ab, g_gla, g_mla_q, w_mla_uq,
                 g_mla_kv, w_mla_ukv, lam_q1, lam_k1, lam_q2, lam_k2, g_diff, w_out):
    w_t = jnp.swapaxes(w_in, 1, 2)
    kpe_t = w_t[:, R_MLA[1] - DR_B:R_MLA[1]]
    dq_t = w_t[:, R_DIFF[0]:R_DIFF[0] + 256]
    dk_t = w_t[:, R_DIFF[0] + 256:R_DIFF[0] + 512]
    w_x = jnp.concatenate([kpe_t] * 4 + [_rot_rows(kpe_t)] * 4 + [_rot_rows(dq_t), _rot_rows(dk_t)],
                          axis=1).astype(BF16)
    zg = jnp.zeros((DEPTH, GLA_LR, 128), F32)
    w_gate = jnp.concatenate([jnp.concatenate([w_gla_af, zg], axis=-1),
                              jnp.concatenate([zg, w_gla_ab], axis=-1)], axis=1).astype(BF16)
    uq = w_mla_uq.reshape(DEPTH, Q_LORA, H_B, DN_B + DR_B)
    w_pe = uq[..., DN_B:].reshape(DEPTH, Q_LORA, H_B * DR_B)
    w_uq = jnp.concatenate([uq[..., :DN_B].reshape(DEPTH, Q_LORA, H_B * DN_B), w_pe, _rot_cols(w_pe)],
                           axis=-1).astype(BF16)
    ukv = w_mla_ukv.reshape(DEPTH, KV_LORA, H_B, DN_B + DV_B)
    w_ukv = jnp.concatenate([ukv[..., :DN_B].reshape(DEPTH, KV_LORA, H_B * DN_B),
                             ukv[..., DN_B:].reshape(DEPTH, KV_LORA, H_B * DV_B)], axis=-1).astype(BF16)
    row = lambda a: a.reshape(DEPTH, 1, a.shape[-1])
    return dict(
        w_t=w_t.astype(BF16), w_x=w_x, w_gate=w_gate,
        b_gate=row(jnp.concatenate([b_gla_af, b_gla_ab], axis=-1)),
        w_uq=w_uq, w_ukv=w_ukv, w_out=w_out.astype(BF16),
        g_pre=row(g_pre), g_post=row(g_post), g_mla_q=row(g_mla_q), g_mla_kv=row(g_mla_kv),
        g_gla4=row(jnp.tile(g_gla, (1, H_A))), g_diff4=row(jnp.tile(g_diff, (1, H_C))),
        lam=jnp.stack([lam_q1, lam_k1, lam_q2, lam_k2], axis=1))


def _rope_tables(n):
    t = np.arange(n)
    row = (t // GRID_W).astype(np.float32)
    col = (t % GRID_W).astype(np.float32)
    half = ROPE_DIM // 2
    inv = (1.0 / (np.float32(ROPE_THETA) ** (np.arange(0, half, 2, dtype=np.float32) / np.float32(half)))
           ).astype(np.float32)
    ar = row[:, None] * inv
    ac = col[:, None] * inv
    ang = np.concatenate([ar, ar, ac, ac], axis=-1).astype(np.float32)
    return (jnp.asarray(np.tile(np.cos(ang), (1, 8)).astype(np.float32)),
            jnp.asarray(np.tile(np.sin(ang), (1, 8)).astype(np.float32)))


def _sublayer(x2d, mod, pw, layer, *, seq, rope_tabs, ctx, caches):
    sample = ctx is not None
    pre = _pre_call(x2d, mod, pw, layer, seq=seq, rope_tabs=rope_tabs,
                    caches=None if sample else caches[:4])
    qk, v, la, ga, gb, gc, mq, mk, mvt, dqk, dvt = pre[:11]
    if sample:
        (oa,) = _gla_call(qk, v, la, ga, pw["g_gla4"], layer, seq=seq, state_in=ctx["state"])
        ob = _mla_call(mq, mk, mvt, gb, ctx["mla_k"], ctx["mla_vt"], layer, seq=seq)
        oc = _diff_call(dqk, dvt, gc, pw["lam"], pw["g_diff4"], ctx["diff_k_t"], ctx["diff_v_t"], layer,
                        seq=seq)
        new_caches = None
    else:
        oa, sfin = _gla_call(qk, v, la, ga, pw["g_gla4"], layer, seq=seq,
                             state_out=caches[4] if caches else None)
        ob = _mla_call(mq, mk, mvt, gb, None, None, layer, seq=seq)
        oc = _diff_call(dqk, dvt, gc, pw["lam"], pw["g_diff4"], None, None, layer, seq=seq)
        new_caches = tuple(pre[11:]) + (sfin,)
    y = _post_call(oa, ob, oc, x2d, mod, pw, layer, seq=seq, sample=sample)
    return y, new_caches


def kernel(x_prompt, x_sample, c, cache_mla_ckv, cache_mla_kpe, cache_diff_k, cache_diff_v, state_gla,
           c_ctx, w_ada, b_ada, g_pre, g_post, w_in, w_gla_af, b_gla_af, w_gla_ab, b_gla_ab, g_gla,
           g_mla_q, w_mla_uq, g_mla_kv, w_mla_ukv, lam_q1, lam_k1, lam_q2, lam_k2, g_diff, w_out):
    bp, tp, d = x_prompt.shape
    bs, ts, _ = x_sample.shape

    pw = _pack_params(g_pre, g_post, w_in, w_gla_af, b_gla_af, w_gla_ab, b_gla_ab, g_gla, g_mla_q,
                      w_mla_uq, g_mla_kv, w_mla_ukv, lam_q1, lam_k1, lam_q2, lam_k2, g_diff, w_out)
    cvecs = jnp.concatenate([c_ctx[None], c, jnp.zeros((MOD_ROWS - 1 - bs, d), F32)], axis=0)
    mod = _mod_call(cvecs, w_ada, b_ada).reshape(DEPTH * MOD_ROWS, 1, 3 * d)
    rope_tabs = _rope_tables(ts)
    ctx_k, ctx_vt = _ctxkv_call(cache_mla_ckv, jnp.swapaxes(cache_mla_kpe, -1, -2), pw["w_ukv"])
    ctx = dict(state=state_gla, mla_k=ctx_k, mla_vt=ctx_vt,
               diff_k_t=jnp.swapaxes(cache_diff_k, -1, -2), diff_v_t=jnp.swapaxes(cache_diff_v, -1, -2))

    y_p = x_prompt.reshape(bp * tp, d)
    y_s = x_sample.reshape(bs * ts, d)
    caches = ()
    for l in range(DEPTH):
        y_p, caches = _sublayer(y_p, mod, pw, l, seq=tp, rope_tabs=None, ctx=None, caches=caches)
        y_s, _ = _sublayer(y_s, mod, pw, l, seq=ts, rope_tabs=rope_tabs, ctx=ctx, caches=None)
    ckvn, kpe_t, kc_t, vc_t, new_state = caches
    return (y_p.reshape(bp, tp, d), y_s.reshape(bs, ts, d), ckvn, jnp.swapaxes(kpe_t, -1, -2),
            jnp.swapaxes(kc_t, -1, -2), jnp.swapaxes(vc_t, -1, -2), new_state)
```

```python
import functools
import math

import numpy as np
import jax
import jax.numpy as jnp
from jax import lax
from jax.experimental import pallas as pl
from jax.experimental.pallas import tpu as pltpu

F32 = jnp.float32
BF16 = jnp.bfloat16

D_MODEL = 1024
DEPTH = 2
GRID_W = 64
EPS = 1e-6
ROPE_THETA = 10000.0
ROPE_DIM = 32
H_A, DK_A, DV_A = 4, 32, 64
GLA_LR = 16
GLA_TAU = 16.0
GLA_CHUNK = 64
H_B, DN_B, DR_B, DV_B = 8, 64, 32, 64
Q_LORA, KV_LORA = 256, 128
H_C, DC = 4, 32
W_A, W_B, W_C = H_A * DV_A, H_B * DV_B, H_C * 2 * DC
ST_R, ST_C = H_A * DV_A, H_A * DK_A
LOG2E = math.log2(math.e)

R_GLA = (0, 544)
R_GG = (544, 800)
R_MLA = (800, 1216)
R_MG = (1216, 1728)
R_DIFF = (1728, 2752)
N_IN = 2752
MOD_ROWS = 8

V7X_VMEM_LIMIT_BYTES = 56 * 1024 * 1024
TOKEN_BLOCK = 512
ATT_QBLOCK = 512


def _cparams(n_axes=1):
    return pltpu.CompilerParams(dimension_semantics=("arbitrary",) * n_axes,
                                vmem_limit_bytes=V7X_VMEM_LIMIT_BYTES)


def _rms(x, g):
    return x * lax.rsqrt(jnp.mean(x * x, axis=-1, keepdims=True) + EPS) * g


def _silu(x):
    return x * jax.nn.sigmoid(x)


def _log_sigmoid(x):
    return jnp.minimum(x, 0.0) - jnp.log1p(jnp.exp(-jnp.abs(x)))


def _rope(z, cos, sin):
    w = z.shape[-1]
    lane = lax.broadcasted_iota(jnp.int32, z.shape, 1)
    rot = jnp.where((lane & 15) < 8, -pltpu.roll(z, w - 8, 1), pltpu.roll(z, 8, 1))
    return z * cos + rot * sin


def _lane_mask(width, lo, size, dtype):
    lane = lax.broadcasted_iota(jnp.int32, (1, width), 1)
    return jnp.where((lane >= lo) & (lane < lo + size), 1.0, 0.0).astype(dtype)


def _dot(a, b):
    return jnp.dot(a, b, preferred_element_type=F32)


def _dot_nt(a, b):
    return lax.dot_general(a, b, (((1,), (1,)), ((), ())), preferred_element_type=F32)


def _dot_tn(a, b):
    return lax.dot_general(a, b, (((0,), (0,)), ((), ())), preferred_element_type=F32)


def _layer_spec(shape, layer):
    nd = len(shape)
    return pl.BlockSpec((1,) + tuple(shape[1:]), lambda *_: (layer,) + (0,) * (nd - 1))


_ANY = pl.BlockSpec(memory_space=pl.ANY)


def _mod_kernel(c_ref, w_ref, b_ref, o_ref):
    s = _silu(c_ref[...]).astype(BF16)
    o_ref[0] = _dot(s, w_ref[0].astype(BF16)) + b_ref[0]


def _mod_call(cvecs, w_ada, b_ada):
    nb = 1024
    return pl.pallas_call(
        _mod_kernel,
        grid=(DEPTH, 3 * D_MODEL // nb),
        in_specs=[pl.BlockSpec((MOD_ROWS, D_MODEL), lambda l, j: (0, 0)),
                  pl.BlockSpec((1, D_MODEL, nb), lambda l, j: (l, 0, j)),
                  pl.BlockSpec((1, 1, nb), lambda l, j: (l, 0, j))],
        out_specs=pl.BlockSpec((1, MOD_ROWS, nb), lambda l, j: (l, 0, j)),
        out_shape=jax.ShapeDtypeStruct((DEPTH, MOD_ROWS, 3 * D_MODEL), F32),
        compiler_params=_cparams(2), name="adaln_mod",
    )(cvecs, w_ada, b_ada.reshape(DEPTH, 1, 3 * D_MODEL))


def _ctxkv_kernel(ckv_ref, kpe_ref, w_ref, k_ref, vt_ref):
    kv = _dot(ckv_ref[0, 0].astype(BF16), w_ref[0])
    kpe4 = jnp.concatenate([kpe_ref[0, 0]] * 4, axis=0).T
    k_ref[0, 0, :, 0:512] = kv[:, 0:512].astype(BF16)
    k_ref[0, 0, :, 512:640] = kpe4.astype(BF16)
    vt_ref[0, 0] = kv[:, 512:1024].T.astype(BF16)


def _ctxkv_call(cache_ckv, cache_kpe_t, wukv):
    nb, _, tc, _ = cache_ckv.shape
    return pl.pallas_call(
        _ctxkv_kernel,
        grid=(DEPTH, nb),
        in_specs=[pl.BlockSpec((1, 1, tc, KV_LORA), lambda l, b: (b, l, 0, 0)),
                  pl.BlockSpec((1, 1, DR_B, tc), lambda l, b: (b, l, 0, 0)),
                  pl.BlockSpec((1, KV_LORA, 1024), lambda l, b: (l, 0, 0))],
        out_specs=[pl.BlockSpec((1, 1, tc, 640), lambda l, b: (l, b, 0, 0)),
                   pl.BlockSpec((1, 1, W_B, tc), lambda l, b: (l, b, 0, 0))],
        out_shape=[jax.ShapeDtypeStruct((DEPTH, nb, tc, 640), BF16),
                   jax.ShapeDtypeStruct((DEPTH, nb, W_B, tc), BF16)],
        compiler_params=_cparams(2), name="mla_ctx_kv",
    )(cache_ckv, cache_kpe_t, wukv)


def _pre_kernel(*refs, rope, ctx_out, alias_in, bpb, seq):
    it = iter(refs)
    (x_ref, mod_ref, gpre_ref, w_ref, wx_ref, wg_ref, bg_ref, gq_ref, gkv_ref, wuq_ref,
     wukv_ref) = (next(it) for _ in range(11))
    if rope:
        cos_ref, sin_ref = next(it), next(it)
    for _ in range(alias_in):
        next(it)
    (qk_ref, v_ref, la_ref, ga_ref, gb_ref, gc_ref, mq_ref, mk_ref, mvt_ref, dqk_ref,
     dvt_ref) = (next(it) for _ in range(11))
    if ctx_out:
        ckvn_ref, kpe_ref, kc_ref, vc_ref = (next(it) for _ in range(4))

    d = D_MODEL
    shift = mod_ref[0, :, 0:d]
    scale = mod_ref[0, :, d:2 * d]
    h = (_rms(x_ref[...], gpre_ref[0]) * (1.0 + scale) + shift).astype(BF16)
    proj = lambda r: _dot_nt(h, w_ref[0, r[0]:r[1], :])
    if rope:
        cos = cos_ref[...]
        sin = sin_ref[...]

    pg = proj(R_GLA)
    qk_ref[:, 0:128] = pg[:, 0:128] * (DK_A ** -0.5)
    qk_ref[:, 128:256] = pg[:, 128:256]
    v_ref[...] = pg[:, 256:512]
    xg = _dot(pg[:, 512:512 + 2 * GLA_LR].astype(BF16), wg_ref[0]) + bg_ref[0]
    la_ref[...] = _log_sigmoid(xg) * (1.0 / GLA_TAU)
    ga_ref[...] = _silu(proj(R_GG)).astype(BF16)

    pm = proj(R_MLA)
    qall = _dot(_rms(pm[:, 0:256], gq_ref[0]).astype(BF16), wuq_ref[0])
    q_pe = qall[:, 512:768]
    if rope:
        q_pe = _rope(q_pe, cos, sin)
    sb = (DN_B + DR_B) ** -0.5 * LOG2E
    mq_ref[:, 0:512] = (qall[:, 0:512] * sb).astype(BF16)
    mq_ref[:, 512:768] = (q_pe * sb).astype(BF16)
    ckvn = _rms(pm[:, 256:384], gkv_ref[0])
    kvall = _dot(ckvn.astype(BF16), wukv_ref[0])
    kpe4 = _dot_nt(h, wx_ref[0])
    if rope:
        kpe4 = _rope(kpe4, cos[:, 0:128], sin[:, 0:128])
    mk_ref[:, 0:512] = kvall[:, 0:512].astype(BF16)
    mk_ref[:, 512:640] = kpe4.astype(BF16)
    mvt_ref[...] = kvall[:, 512:1024].T.astype(BF16)
    gb_ref[...] = _silu(proj(R_MG)).astype(BF16)

    pd = proj(R_DIFF)
    dq, dk, dv = pd[:, 0:256], pd[:, 256:512], pd[:, 512:768]
    if rope:
        dq = _rope(dq, cos, sin)
        dk = _rope(dk, cos, sin)
    dqk_ref[:, 0:256] = (dq * (DC ** -0.5 * LOG2E)).astype(BF16)
    dqk_ref[:, 256:512] = dk.astype(BF16)
    dv_t = dv.T
    dvt_ref[...] = dv_t.astype(BF16)
    gc_ref[...] = _silu(pd[:, 768:1024]).astype(BF16)
    if ctx_out:
        kpe_t = kpe4.T
        dk_t = dk.T
        for bb in range(bpb):
            rs = slice(bb * seq, (bb + 1) * seq)
            ckvn_ref[bb, 0] = ckvn[rs]
            kpe_ref[bb, 0] = kpe_t[0:DR_B, rs]
            kc_ref[bb, 0] = dk_t[:, rs].reshape(H_C, 2 * DC, seq)
            vc_ref[bb, 0] = dv_t[:, rs].reshape(H_C, 2 * DC, seq)


def _pre_call(x2d, mod, pw, layer, *, seq, rope_tabs, caches):
    n = x2d.shape[0]
    tm = min(TOKEN_BLOCK, n)
    bpb = max(tm // seq, 1)
    rope = rope_tabs is not None
    ctx_out = caches is not None
    steps_per_seq = max(seq // tm, 1)
    nbt = n // seq

    def mod_idx(i):
        return (layer * MOD_ROWS + ((i * tm) // seq + 1 if rope else 0), 0, 0)

    names = ["g_pre", "w_t", "w_x", "w_gate", "b_gate", "g_mla_q", "g_mla_kv", "w_uq", "w_ukv"]
    in_specs = [pl.BlockSpec((tm, D_MODEL), lambda i: (i, 0)), pl.BlockSpec((1, 1, 3 * D_MODEL), mod_idx)]
    in_specs += [_layer_spec(pw[k].shape, layer) for k in names]
    args = [x2d, mod] + [pw[k] for k in names]
    if rope:
        in_specs += [pl.BlockSpec((tm, 256), lambda i: (i % steps_per_seq, 0))] * 2
        args += list(rope_tabs)
    outs = [(256, F32, False), (256, F32, False), (256, F32, False),
            (W_A, BF16, False), (W_B, BF16, False), (W_C, BF16, False),
            (768, BF16, False), (640, BF16, False), (W_B, BF16, True),
            (512, BF16, False), (W_C, BF16, True)]
    out_specs = [pl.BlockSpec((w, tm), lambda i: (0, i)) if tr else pl.BlockSpec((tm, w), lambda i: (i, 0))
                 for w, _, tr in outs]
    out_shape = [jax.ShapeDtypeStruct((w, n) if tr else (n, w), dt) for w, dt, tr in outs]
    widths = outs
    aliases = {}
    if ctx_out:
        out_specs += [pl.BlockSpec((bpb, 1, seq, KV_LORA), lambda i: (i, layer, 0, 0)),
                      pl.BlockSpec((bpb, 1, DR_B, seq), lambda i: (i, layer, 0, 0)),
                      pl.BlockSpec((bpb, 1, H_C, 2 * DC, seq), lambda i: (i, layer, 0, 0, 0)),
                      pl.BlockSpec((bpb, 1, H_C, 2 * DC, seq), lambda i: (i, layer, 0, 0, 0))]
        out_shape += [jax.ShapeDtypeStruct((nbt, DEPTH, seq, KV_LORA), F32),
                      jax.ShapeDtypeStruct((nbt, DEPTH, DR_B, seq), F32),
                      jax.ShapeDtypeStruct((nbt, DEPTH, H_C, 2 * DC, seq), F32),
                      jax.ShapeDtypeStruct((nbt, DEPTH, H_C, 2 * DC, seq), F32)]
        for j, arr in enumerate(caches):
            aliases[len(args)] = len(widths) + j
            in_specs.append(_ANY)
            args.append(arr)
    return pl.pallas_call(
        functools.partial(_pre_kernel, rope=rope, ctx_out=ctx_out, alias_in=len(aliases), bpb=bpb, seq=seq),
        grid=(n // tm,), in_specs=in_specs, out_specs=out_specs, out_shape=out_shape,
        input_output_aliases=aliases,
        compiler_params=_cparams(), name="pre_rope" if rope else "pre_ctx",
    )(*args)


_GLA_LEVELS = (1, 2, 4, 8, 16, 32)


def _gla_consts(rev):
    c = GLA_CHUNK
    row = lax.broadcasted_iota(jnp.int32, (c, 128), 0)
    pos = (c - 1 - row) if rev else row
    ri = lax.broadcasted_iota(jnp.int32, (c, H_A * c), 0)
    cj = lax.broadcasted_iota(jnp.int32, (c, H_A * c), 1) & (c - 1)
    pi = (c - 1 - ri) if rev else ri
    pj = (c - 1 - cj) if rev else cj
    x = pi ^ pj
    lvl = jnp.where(pi == pj, 0, -1)
    for kbit in range(6):
        lvl = jnp.where((pj < pi) & ((x >> kbit) == 1), kbit + 1, lvl)
    return pos, lvl


def _gla_scores(q, k, la, b, pos, lvl, hm_bf, rev):
    c = GLA_CHUNK
    prv = pltpu.roll(la, c - 1 if rev else 1, 0)
    nxt = pltpu.roll(la, 1 if rev else c - 1, 0)
    s_tot = jnp.where(lvl == 0, _dot_nt(q.astype(BF16), jnp.concatenate([k.astype(BF16)] * H_A, 0) * hm_bf), 0.0)
    for kbit, m in enumerate(_GLA_LEVELS):
        up = ((pos >> kbit) & 1) == 1
        if m == 1:
            e = jnp.where(up, la, 0.0)
        elif m == 2:
            c4 = pos & 3
            e = jnp.where(c4 == 0, nxt, jnp.where(c4 == 1, 0.0, jnp.where(c4 == 2, la, la + prv)))
        else:
            nblk = c // (2 * m)
            loc = m if rev else m - 1
            b3 = b.reshape(nblk, 2 * m, 128)
            ref = jnp.broadcast_to(b3[:, loc:loc + 1, :], (nblk, 2 * m, 128)).reshape(c, 128)
            dlt = b - ref
            e = jnp.where(up, dlt, -dlt)
        xm = (jnp.where(up, q, k) * jnp.exp(e)).astype(BF16)
        sm = _dot_nt(xm, jnp.concatenate([xm] * H_A, 0) * hm_bf)
        s_tot = jnp.where(lvl == kbit + 1, sm, s_tot)
    return s_tot


def _gla_apply(s_tot, q, k, v, b, st_prev, hm_f32, vm_bf, rev):
    c = GLA_CHUNK
    vbd = jnp.concatenate([v] * H_A, 0) * vm_bf
    blast = b[0:1, :] if rev else b[c - 1:c, :]
    qbar = (q * jnp.exp(b)).astype(BF16)
    kdec = (k * jnp.exp(blast - b)).astype(BF16)
    o = _dot(s_tot.astype(BF16), vbd) + _dot_nt(qbar, st_prev.astype(BF16))
    st_new = st_prev * jnp.exp(blast) + _dot_tn(v, kdec) * hm_f32
    return o, st_new


def _gla_kernel(*refs, seq, has_s0, alias_in):
    it = iter(refs)
    qk_ref, v_ref, la_ref, gate_ref, g_ref = (next(it) for _ in range(5))
    s0_ref = next(it) if has_s0 else None
    for _ in range(alias_in):
        next(it)
    oa_ref = next(it)
    sfin_ref = None if has_s0 else next(it)
    b_sc, acc_sc, st_sc, blk_sc = next(it), next(it), next(it), next(it)

    c = GLA_CHUNK
    nc = seq // c
    rowc = lax.broadcasted_iota(jnp.int32, (seq, 128), 0) & (c - 1)
    bf = la_ref[:, 0:128]
    bb = la_ref[:, 128:256]
    s = 1
    while s < c:
        bf = bf + jnp.where(rowc >= s, pltpu.roll(bf, s, 0), 0.0)
        bb = bb + jnp.where(rowc < c - s, pltpu.roll(bb, seq - s, 0), 0.0)
        s *= 2
    b_sc[0] = bf
    b_sc[1] = bb
    acc_sc[...] = jnp.zeros_like(acc_sc)
    for d in range(2):
        if has_s0:
            blk_sc[...] = jnp.zeros_like(blk_sc)
            for hh in range(H_A):
                blk_sc[DK_A * hh:DK_A * (hh + 1), DV_A * hh:DV_A * (hh + 1)] = s0_ref[0, 0, d, hh]
            st_sc[d] = blk_sc[...].T
        else:
            st_sc[d] = jnp.zeros((ST_R, ST_C), F32)

    hrow = lax.broadcasted_iota(jnp.int32, (H_A * c, 128), 0) // c
    hm_f32 = jnp.where(hrow == lax.broadcasted_iota(jnp.int32, (H_A * c, 128), 1) // DK_A, 1.0, 0.0)
    hm_bf = hm_f32.astype(BF16)
    vrow = lax.broadcasted_iota(jnp.int32, (H_A * c, H_A * DV_A), 0) // c
    vm_bf = jnp.where(vrow == lax.broadcasted_iota(jnp.int32, (H_A * c, H_A * DV_A), 1) // DV_A,
                      1.0, 0.0).astype(BF16)
    consts = (_gla_consts(False), _gla_consts(True))

    def chunk_rows(n, d):
        cn = (nc - 1 - n) if d else n
        start = cn * c
        return pl.ds(start if isinstance(start, int) else pl.multiple_of(start, c), c)

    def scores(n):
        out = []
        for d in range(2):
            rows = chunk_rows(n, d)
            pos, lvl = consts[d]
            out.append(_gla_scores(qk_ref[rows, 0:128], qk_ref[rows, 128:256],
                                   la_ref[rows, 128 * d:128 * d + 128], b_sc[d, rows, :],
                                   pos, lvl, hm_bf, bool(d)))
        return tuple(out)

    def apply(n, s_both):
        for d in range(2):
            rows = chunk_rows(n, d)
            o, st_new = _gla_apply(s_both[d], qk_ref[rows, 0:128], qk_ref[rows, 128:256],
                                   v_ref[rows, :].astype(BF16), b_sc[d, rows, :], st_sc[d],
                                   hm_f32, vm_bf, bool(d))
            acc_sc[rows, :] = acc_sc[rows, :] + o
            st_sc[d] = st_new

    def body(n, s_cur):
        s_next = scores(n + 1)
        apply(n, s_cur)
        return s_next

    s_last = lax.fori_loop(0, nc - 1, body, scores(0))
    apply(nc - 1, s_last)

    oa = acc_sc[...]
    sq = oa * oa
    inv = jnp.zeros_like(oa)
    for hh in range(H_A):
        m = _lane_mask(W_A, DV_A * hh, DV_A, F32)
        ms = jnp.sum(sq * m, axis=-1, keepdims=True) * (1.0 / DV_A)
        inv = inv + lax.rsqrt(ms + EPS) * m
    oa_ref[...] = (oa * inv * g_ref[0] * gate_ref[...]).astype(BF16)
    if not has_s0:
        for d in range(2):
            blk_sc[...] = st_sc[d].T
            for hh in range(H_A):
                sfin_ref[0, 0, d, hh] = blk_sc[DK_A * hh:DK_A * (hh + 1), DV_A * hh:DV_A * (hh + 1)]


def _gla_call(qk, v, la, gate, g4, layer, *, seq, state_in=None, state_out=None):
    n = qk.shape[0]
    nb = n // seq
    has_s0 = state_in is not None
    blk = lambda w: pl.BlockSpec((seq, w), lambda i: (i, 0))
    st_spec = pl.BlockSpec((1, 1, 2, H_A, DK_A, DV_A), lambda i: (i, layer, 0, 0, 0, 0))
    in_specs = [blk(256), blk(256), blk(256), blk(W_A), _layer_spec(g4.shape, layer)]
    args = [qk, v, la, gate, g4]
    out_specs = [blk(W_A)]
    out_shape = [jax.ShapeDtypeStruct((n, W_A), BF16)]
    aliases = {}
    if has_s0:
        in_specs.append(st_spec)
        args.append(state_in)
    else:
        out_specs.append(st_spec)
        out_shape.append(jax.ShapeDtypeStruct((nb, DEPTH, 2, H_A, DK_A, DV_A), F32))
        if state_out is not None:
            aliases[len(args)] = 1
            in_specs.append(_ANY)
            args.append(state_out)
    return pl.pallas_call(
        functools.partial(_gla_kernel, seq=seq, has_s0=has_s0, alias_in=len(aliases)),
        grid=(nb,), in_specs=in_specs, out_specs=out_specs, out_shape=out_shape,
        input_output_aliases=aliases,
        scratch_shapes=[pltpu.VMEM((2, seq, 128), F32), pltpu.VMEM((seq, W_A), F32),
                        pltpu.VMEM((2, ST_R, ST_C), F32), pltpu.VMEM((ST_C, ST_R), F32)],
        compiler_params=_cparams(), name="gla_state" if has_s0 else "gla_ctx",
    )(*args)


def _softmax_t_pv(st, vt):
    m = _col_reduce(st, jnp.max)
    e = jnp.exp2(st - m)
    l = _col_reduce(e, jnp.sum)
    return _dot(vt, e.astype(BF16)) * (1.0 / l)


def _pipelined_attention(score_fns, value_fns):
    outs = []
    st_next = score_fns[0]()
    for j, vfn in enumerate(value_fns):
        st = st_next
        if j + 1 < len(score_fns):
            st_next = score_fns[j + 1]()
        outs.append(_softmax_t_pv(st, vfn()))
    return outs


def _col_reduce(x, op):
    rows, cols = x.shape
    part = 128 if rows % 128 == 0 and rows > 128 else rows
    if part != rows:
        x = op(x.reshape(rows // part, part, cols), axis=0)
    return op(x, axis=0, keepdims=True)


def _mla_kernel(*refs, seq, ctx_len):
    it = iter(refs)
    q_ref, k_ref, vt_ref, gate_ref = (next(it) for _ in range(4))
    ck_ref, cvt_ref = (next(it), next(it)) if ctx_len else (None, None)
    ob_ref = next(it)
    kk_sc, vt_sc = next(it), next(it)

    for p in range(H_B // 2):
        kk_sc[p, 0:seq, 0:128] = k_ref[:, 128 * p:128 * p + 128]
        kk_sc[p, 0:seq, 128:256] = k_ref[:, 512:640]
        if ctx_len:
            kk_sc[p, seq:seq + ctx_len, 0:128] = ck_ref[0, 0, :, 128 * p:128 * p + 128]
            kk_sc[p, seq:seq + ctx_len, 128:256] = ck_ref[0, 0, :, 512:640]
    vt_sc[:, 0:seq] = vt_ref[...]
    if ctx_len:
        vt_sc[:, seq:seq + ctx_len] = cvt_ref[0, 0]

    qb = min(ATT_QBLOCK, seq)

    def body(i, carry):
        rows = pl.ds(pl.multiple_of(i * qb, qb), qb)
        def scores(h):
            p, hh = divmod(h, 2)
            qn = q_ref[rows, 128 * p:128 * p + 128] * _lane_mask(128, DN_B * hh, DN_B, BF16)
            qp = (q_ref[rows, 512 + 128 * (h // 4):512 + 128 * (h // 4) + 128]
                  * _lane_mask(128, DR_B * (h % 4), DR_B, BF16))
            return _dot_nt(kk_sc[p], jnp.concatenate([qn, qp], axis=-1))

        outs = _pipelined_attention(
            [functools.partial(scores, h) for h in range(H_B)],
            [functools.partial(lambda h: vt_sc[DV_B * h:DV_B * (h + 1), :], h) for h in range(H_B)])
        ob = jnp.concatenate(outs, axis=0).T
        ob_ref[rows, :] = (ob * gate_ref[rows, :]).astype(BF16)
        return carry

    lax.fori_loop(0, seq // qb, body, 0)


def _mla_call(mq, mk, mvt, gate, ctx_k, ctx_vt, layer, *, seq):
    n = mq.shape[0]
    ctx_len = 0 if ctx_k is None else ctx_k.shape[2]
    in_specs = [pl.BlockSpec((seq, 768), lambda i: (i, 0)), pl.BlockSpec((seq, 640), lambda i: (i, 0)),
                pl.BlockSpec((W_B, seq), lambda i: (0, i)), pl.BlockSpec((seq, W_B), lambda i: (i, 0))]
    args = [mq, mk, mvt, gate]
    if ctx_len:
        in_specs += [pl.BlockSpec((1, 1, ctx_len, 640), lambda i: (layer, i, 0, 0)),
                     pl.BlockSpec((1, 1, W_B, ctx_len), lambda i: (layer, i, 0, 0))]
        args += [ctx_k, ctx_vt]
    tk = seq + ctx_len
    return pl.pallas_call(
        functools.partial(_mla_kernel, seq=seq, ctx_len=ctx_len),
        grid=(n // seq,), in_specs=in_specs,
        out_specs=pl.BlockSpec((seq, W_B), lambda i: (i, 0)),
        out_shape=jax.ShapeDtypeStruct((n, W_B), BF16),
        scratch_shapes=[pltpu.VMEM((H_B // 2, tk, 256), BF16), pltpu.VMEM((W_B, tk), BF16)],
        compiler_params=_cparams(), name="mla_ctx" if ctx_len else "mla_self",
    )(*args)


def _diff_kernel(*refs, seq, ctx_len, lam_init):
    it = iter(refs)
    qk_ref, vt_ref, gate_ref, lam_ref, g_ref = (next(it) for _ in range(5))
    ck_ref, cv_ref = (next(it), next(it)) if ctx_len else (None, None)
    oc_ref = next(it)
    k_sc, vt_sc = next(it), next(it)

    lam = (jnp.exp(jnp.sum(lam_ref[0, 0:1, :] * lam_ref[0, 1:2, :], axis=-1, keepdims=True))
           - jnp.exp(jnp.sum(lam_ref[0, 2:3, :] * lam_ref[0, 3:4, :], axis=-1, keepdims=True)) + lam_init)
    dh = 2 * DC
    for p in range(H_C // 2):
        k_sc[p, 0:seq, :] = qk_ref[:, 256 + 128 * p:256 + 128 * p + 128]
        if ctx_len:
            pair_t = jnp.concatenate([ck_ref[0, 0, 2 * p], ck_ref[0, 0, 2 * p + 1]], axis=0)
            k_sc[p, seq:seq + ctx_len, :] = pair_t.T.astype(BF16)
    vt_sc[:, 0:seq] = vt_ref[...]
    if ctx_len:
        for h in range(H_C):
            vt_sc[dh * h:dh * (h + 1), seq:seq + ctx_len] = cv_ref[0, 0, h].astype(BF16)

    qb = min(ATT_QBLOCK, seq)

    def body(i, carry):
        rows = pl.ds(pl.multiple_of(i * qb, qb), qb)
        def scores(h, comp):
            p, hh = divmod(h, 2)
            qm = qk_ref[rows, 128 * p:128 * p + 128] * _lane_mask(128, dh * hh + DC * comp, DC, BF16)
            return _dot_nt(k_sc[p], qm)

        items = [(h, comp) for h in range(H_C) for comp in range(2)]
        o12 = _pipelined_attention(
            [functools.partial(scores, h, comp) for h, comp in items],
            [functools.partial(lambda h: vt_sc[dh * h:dh * (h + 1), :], h) for h, _ in items])
        outs = []
        for h in range(H_C):
            ot = o12[2 * h] - lam * o12[2 * h + 1]
            outs.append(ot * lax.rsqrt(jnp.mean(ot * ot, axis=0, keepdims=True) + EPS))
        oc = jnp.concatenate(outs, axis=0).T
        oc_ref[rows, :] = (oc * g_ref[0] * (1.0 - lam_init) * gate_ref[rows, :]).astype(BF16)
        return carry

    lax.fori_loop(0, seq // qb, body, 0)


def _diff_call(dqk, dvt, gate, lamp, g4, ck_t, cv_t, layer, *, seq):
    n = dqk.shape[0]
    ctx_len = 0 if ck_t is None else ck_t.shape[4]
    lam_init = 0.8 - 0.6 * math.exp(-0.3 * layer)
    in_specs = [pl.BlockSpec((seq, 512), lambda i: (i, 0)), pl.BlockSpec((W_C, seq), lambda i: (0, i)),
                pl.BlockSpec((seq, W_C), lambda i: (i, 0)),
                _layer_spec(lamp.shape, layer), _layer_spec(g4.shape, layer)]
    args = [dqk, dvt, gate, lamp, g4]
    if ctx_len:
        in_specs += [pl.BlockSpec((1, 1, H_C, 2 * DC, ctx_len), lambda i: (i, layer, 0, 0, 0))] * 2
        args += [ck_t, cv_t]
    tk = seq + ctx_len
    return pl.pallas_call(
        functools.partial(_diff_kernel, seq=seq, ctx_len=ctx_len, lam_init=lam_init),
        grid=(n // seq,), in_specs=in_specs,
        out_specs=pl.BlockSpec((seq, W_C), lambda i: (i, 0)),
        out_shape=jax.ShapeDtypeStruct((n, W_C), BF16),
        scratch_shapes=[pltpu.VMEM((H_C // 2, tk, 128), BF16), pltpu.VMEM((W_C, tk), BF16)],
        compiler_params=_cparams(), name="diff_ctx" if ctx_len else "diff_self",
    )(*args)


def _post_kernel(oa_ref, ob_ref, oc_ref, x_ref, mod_ref, w_ref, g_ref, y_ref):
    mix = jnp.concatenate([oa_ref[...], ob_ref[...], oc_ref[...]], axis=-1)
    out = _rms(_dot(mix, w_ref[0]), g_ref[0])
    y_ref[...] = x_ref[...] + mod_ref[0, :, 2 * D_MODEL:3 * D_MODEL] * out


def _post_call(oa, ob, oc, x2d, mod, pw, layer, *, seq, sample):
    n = x2d.shape[0]
    tm = min(TOKEN_BLOCK, n)
    blk = lambda w: pl.BlockSpec((tm, w), lambda i: (i, 0))

    def mod_idx(i):
        return (layer * MOD_ROWS + ((i * tm) // seq + 1 if sample else 0), 0, 0)

    return pl.pallas_call(
        _post_kernel,
        grid=(n // tm,),
        in_specs=[blk(W_A), blk(W_B), blk(W_C), blk(D_MODEL),
                  pl.BlockSpec((1, 1, 3 * D_MODEL), mod_idx),
                  _layer_spec(pw["w_out"].shape, layer), _layer_spec(pw["g_post"].shape, layer)],
        out_specs=blk(D_MODEL),
        out_shape=jax.ShapeDtypeStruct((n, D_MODEL), F32),
        compiler_params=_cparams(), name="post",
    )(oa, ob, oc, x2d, mod, pw["w_out"], pw["g_post"])


def _pack_params(g_pre, g_post, w_in, w_gla_af, b_gla_af, w_gla_ab, b_gla_ab, g_gla, g_mla_q, w_mla_uq,
                 g_mla_kv, w_mla_ukv, lam_q1, lam_k1, lam_q2, lam_k2, g_diff, w_out):
    w_t = jnp.swapaxes(w_in, 1, 2)
    kpe_t = w_t[:, R_MLA[1] - DR_B:R_MLA[1]]
    w_x = jnp.concatenate([kpe_t] * 4, axis=1).astype(BF16)
    zg = jnp.zeros((DEPTH, GLA_LR, 128), F32)
    w_gate = jnp.concatenate([jnp.concatenate([w_gla_af, zg], axis=-1),
                              jnp.concatenate([zg, w_gla_ab], axis=-1)], axis=1).astype(BF16)
    uq = w_mla_uq.reshape(DEPTH, Q_LORA, H_B, DN_B + DR_B)
    w_pe = uq[..., DN_B:].reshape(DEPTH, Q_LORA, H_B * DR_B)
    w_uq = jnp.concatenate([uq[..., :DN_B].reshape(DEPTH, Q_LORA, H_B * DN_B), w_pe], axis=-1).astype(BF16)
    ukv = w_mla_ukv.reshape(DEPTH, KV_LORA, H_B, DN_B + DV_B)
    w_ukv = jnp.concatenate([ukv[..., :DN_B].reshape(DEPTH, KV_LORA, H_B * DN_B),
                             ukv[..., DN_B:].reshape(DEPTH, KV_LORA, H_B * DV_B)], axis=-1).astype(BF16)
    row = lambda a: a.reshape(DEPTH, 1, a.shape[-1])
    return dict(
        w_t=w_t.astype(BF16), w_x=w_x, w_gate=w_gate,
        b_gate=row(jnp.concatenate([b_gla_af, b_gla_ab], axis=-1)),
        w_uq=w_uq, w_ukv=w_ukv, w_out=w_out.astype(BF16),
        g_pre=row(g_pre), g_post=row(g_post), g_mla_q=row(g_mla_q), g_mla_kv=row(g_mla_kv),
        g_gla4=row(jnp.tile(g_gla, (1, H_A))), g_diff4=row(jnp.tile(g_diff, (1, H_C))),
        lam=jnp.stack([lam_q1, lam_k1, lam_q2, lam_k2], axis=1))


def _rope_tables(n):
    t = np.arange(n)
    row = (t // GRID_W).astype(np.float32)
    col = (t % GRID_W).astype(np.float32)
    half = ROPE_DIM // 2
    inv = (1.0 / (np.float32(ROPE_THETA) ** (np.arange(0, half, 2, dtype=np.float32) / np.float32(half)))
           ).astype(np.float32)
    ar = row[:, None] * inv
    ac = col[:, None] * inv
    ang = np.concatenate([ar, ar, ac, ac], axis=-1).astype(np.float32)
    return (jnp.asarray(np.tile(np.cos(ang), (1, 8)).astype(np.float32)),
            jnp.asarray(np.tile(np.sin(ang), (1, 8)).astype(np.float32)))


def _sublayer(x2d, mod, pw, layer, *, seq, rope_tabs, ctx, caches):
    sample = ctx is not None
    pre = _pre_call(x2d, mod, pw, layer, seq=seq, rope_tabs=rope_tabs,
                    caches=None if sample else caches[:4])
    qk, v, la, ga, gb, gc, mq, mk, mvt, dqk, dvt = pre[:11]
    if sample:
        (oa,) = _gla_call(qk, v, la, ga, pw["g_gla4"], layer, seq=seq, state_in=ctx["state"])
        ob = _mla_call(mq, mk, mvt, gb, ctx["mla_k"], ctx["mla_vt"], layer, seq=seq)
        oc = _diff_call(dqk, dvt, gc, pw["lam"], pw["g_diff4"], ctx["diff_k_t"], ctx["diff_v_t"], layer,
                        seq=seq)
        new_caches = None
    else:
        oa, sfin = _gla_call(qk, v, la, ga, pw["g_gla4"], layer, seq=seq,
                             state_out=caches[4] if caches else None)
        ob = _mla_call(mq, mk, mvt, gb, None, None, layer, seq=seq)
        oc = _diff_call(dqk, dvt, gc, pw["lam"], pw["g_diff4"], None, None, layer, seq=seq)
        new_caches = tuple(pre[11:]) + (sfin,)
    y = _post_call(oa, ob, oc, x2d, mod, pw, layer, seq=seq, sample=sample)
    return y, new_caches


def kernel(x_prompt, x_sample, c, cache_mla_ckv, cache_mla_kpe, cache_diff_k, cache_diff_v, state_gla,
           c_ctx, w_ada, b_ada, g_pre, g_post, w_in, w_gla_af, b_gla_af, w_gla_ab, b_gla_ab, g_gla,
           g_mla_q, w_mla_uq, g_mla_kv, w_mla_ukv, lam_q1, lam_k1, lam_q2, lam_k2, g_diff, w_out):
    bp, tp, d = x_prompt.shape
    bs, ts, _ = x_sample.shape

    pw = _pack_params(g_pre, g_post, w_in, w_gla_af, b_gla_af, w_gla_ab, b_gla_ab, g_gla, g_mla_q,
                      w_mla_uq, g_mla_kv, w_mla_ukv, lam_q1, lam_k1, lam_q2, lam_k2, g_diff, w_out)
    cvecs = jnp.concatenate([c_ctx[None], c, jnp.zeros((MOD_ROWS - 1 - bs, d), F32)], axis=0)
    mod = _mod_call(cvecs, w_ada, b_ada).reshape(DEPTH * MOD_ROWS, 1, 3 * d)
    rope_tabs = _rope_tables(ts)
    ctx_k, ctx_vt = _ctxkv_call(cache_mla_ckv, jnp.swapaxes(cache_mla_kpe, -1, -2), pw["w_ukv"])
    ctx = dict(state=state_gla, mla_k=ctx_k, mla_vt=ctx_vt,
               diff_k_t=jnp.swapaxes(cache_diff_k, -1, -2), diff_v_t=jnp.swapaxes(cache_diff_v, -1, -2))

    y_p = x_prompt.reshape(bp * tp, d)
    y_s = x_sample.reshape(bs * ts, d)
    caches = ()
    for l in range(DEPTH):
        y_p, caches = _sublayer(y_p, mod, pw, l, seq=tp, rope_tabs=None, ctx=None, caches=caches)
        y_s, _ = _sublayer(y_s, mod, pw, l, seq=ts, rope_tabs=rope_tabs, ctx=ctx, caches=None)
    ckvn, kpe_t, kc_t, vc_t, new_state = caches
    return (y_p.reshape(bp, tp, d), y_s.reshape(bs, ts, d), ckvn, jnp.swapaxes(kpe_t, -1, -2),
            jnp.swapaxes(kc_t, -1, -2), jnp.swapaxes(vc_t, -1, -2), new_state)
```

```python
import functools
import math

import numpy as np
import jax
import jax.numpy as jnp
from jax import lax
from jax.experimental import pallas as pl
from jax.experimental.pallas import tpu as pltpu

F32 = jnp.float32
BF16 = jnp.bfloat16

D_MODEL = 1024
DEPTH = 2
GRID_W = 64
EPS = 1e-6
ROPE_THETA = 10000.0
ROPE_DIM = 32
H_A, DK_A, DV_A = 4, 32, 64
GLA_LR = 16
GLA_TAU = 16.0
GLA_CHUNK = 64
H_B, DN_B, DR_B, DV_B = 8, 64, 32, 64
Q_LORA, KV_LORA = 256, 128
H_C, DC = 4, 32
W_A, W_B, W_C = H_A * DV_A, H_B * DV_B, H_C * 2 * DC
ST_R, ST_C = H_A * DV_A, H_A * DK_A
LOG2E = math.log2(math.e)
SUBLANES = 8

R_GLA = (0, 544)
R_GG = (544, 800)
R_MLA = (800, 1216)
R_MG = (1216, 1728)
R_DIFF = (1728, 2752)
N_IN = 2752
MOD_ROWS = 8

V7X_VMEM_LIMIT_BYTES = 56 * 1024 * 1024
TOKEN_BLOCK = 512
POST_BLOCK = 1024
ATT_QBLOCK = 512


def _cparams(n_axes=1):
    return pltpu.CompilerParams(dimension_semantics=("arbitrary",) * n_axes,
                                vmem_limit_bytes=V7X_VMEM_LIMIT_BYTES)


def _rms(x, g):
    return x * lax.rsqrt(jnp.mean(x * x, axis=-1, keepdims=True) + EPS) * g


def _silu(x):
    return x * jax.nn.sigmoid(x)


def _log_sigmoid(x):
    return jnp.minimum(x, 0.0) - jnp.log1p(jnp.exp(-jnp.abs(x)))


def _rope(z, cos, sin):
    w = z.shape[-1]
    lane = lax.broadcasted_iota(jnp.int32, z.shape, 1)
    rot = jnp.where((lane & 15) < 8, -pltpu.roll(z, w - 8, 1), pltpu.roll(z, 8, 1))
    return z * cos + rot * sin


def _lane_mask(width, lo, size, dtype):
    lane = lax.broadcasted_iota(jnp.int32, (1, width), 1)
    return jnp.where((lane >= lo) & (lane < lo + size), 1.0, 0.0).astype(dtype)


def _dot(a, b):
    return jnp.dot(a, b, preferred_element_type=F32)


def _dot_nt(a, b):
    return lax.dot_general(a, b, (((1,), (1,)), ((), ())), preferred_element_type=F32)


def _dot_tn(a, b):
    return lax.dot_general(a, b, (((0,), (0,)), ((), ())), preferred_element_type=F32)


def _layer_spec(shape, layer):
    nd = len(shape)
    return pl.BlockSpec((1,) + tuple(shape[1:]), lambda *_: (layer,) + (0,) * (nd - 1))


_ANY = pl.BlockSpec(memory_space=pl.ANY)


def _mod_kernel(c_ref, w_ref, b_ref, o_ref):
    s = _silu(c_ref[...]).astype(BF16)
    o_ref[0] = _dot(s, w_ref[0].astype(BF16)) + b_ref[0]


def _mod_call(cvecs, w_ada, b_ada):
    nb = 1024
    return pl.pallas_call(
        _mod_kernel,
        grid=(DEPTH, 3 * D_MODEL // nb),
        in_specs=[pl.BlockSpec((MOD_ROWS, D_MODEL), lambda l, j: (0, 0)),
                  pl.BlockSpec((1, D_MODEL, nb), lambda l, j: (l, 0, j)),
                  pl.BlockSpec((1, 1, nb), lambda l, j: (l, 0, j))],
        out_specs=pl.BlockSpec((1, MOD_ROWS, nb), lambda l, j: (l, 0, j)),
        out_shape=jax.ShapeDtypeStruct((DEPTH, MOD_ROWS, 3 * D_MODEL), F32),
        compiler_params=_cparams(2), name="adaln_mod",
    )(cvecs, w_ada, b_ada.reshape(DEPTH, 1, 3 * D_MODEL))


def _ctxkv_kernel(ckv_ref, kpe_ref, w_ref, k_ref, vt_ref):
    kv = _dot(ckv_ref[0, 0].astype(BF16), w_ref[0])
    kpe4 = jnp.concatenate([kpe_ref[0, 0]] * 4, axis=0).T
    k_ref[0, 0, :, 0:512] = kv[:, 0:512].astype(BF16)
    k_ref[0, 0, :, 512:640] = kpe4.astype(BF16)
    vt_ref[0, 0] = kv[:, 512:1024].T.astype(BF16)


def _ctxkv_call(cache_ckv, cache_kpe_t, wukv):
    nb, _, tc, _ = cache_ckv.shape
    return pl.pallas_call(
        _ctxkv_kernel,
        grid=(DEPTH, nb),
        in_specs=[pl.BlockSpec((1, 1, tc, KV_LORA), lambda l, b: (b, l, 0, 0)),
                  pl.BlockSpec((1, 1, DR_B, tc), lambda l, b: (b, l, 0, 0)),
                  pl.BlockSpec((1, KV_LORA, 1024), lambda l, b: (l, 0, 0))],
        out_specs=[pl.BlockSpec((1, 1, tc, 640), lambda l, b: (l, b, 0, 0)),
                   pl.BlockSpec((1, 1, W_B, tc), lambda l, b: (l, b, 0, 0))],
        out_shape=[jax.ShapeDtypeStruct((DEPTH, nb, tc, 640), BF16),
                   jax.ShapeDtypeStruct((DEPTH, nb, W_B, tc), BF16)],
        compiler_params=_cparams(2), name="mla_ctx_kv",
    )(cache_ckv, cache_kpe_t, wukv)


def _pre_kernel(*refs, rope, ctx_out, alias_in, bpb, seq):
    it = iter(refs)
    (x_ref, mod_ref, gpre_ref, w_ref, wx_ref, wg_ref, bg_ref, gq_ref, gkv_ref, wuq_ref,
     wukv_ref) = (next(it) for _ in range(11))
    if rope:
        cos_ref, sin_ref = next(it), next(it)
    for _ in range(alias_in):
        next(it)
    (qk_ref, v_ref, la_ref, bs_ref, ga_ref, gb_ref, gc_ref, mq_ref, mk_ref, mvt_ref, dqk_ref,
     dvt_ref) = (next(it) for _ in range(12))
    if ctx_out:
        ckvn_ref, kpe_ref, kc_ref, vc_ref = (next(it) for _ in range(4))

    d = D_MODEL
    shift = mod_ref[0, :, 0:d]
    scale = mod_ref[0, :, d:2 * d]
    h = (_rms(x_ref[...], gpre_ref[0]) * (1.0 + scale) + shift).astype(BF16)
    proj = lambda r: _dot_nt(h, w_ref[0, r[0]:r[1], :])
    if rope:
        cos = cos_ref[...]
        sin = sin_ref[...]

    pg = proj(R_GLA)
    qk_ref[:, 0:128] = pg[:, 0:128] * (DK_A ** -0.5)
    qk_ref[:, 128:256] = pg[:, 128:256]
    v_ref[...] = pg[:, 256:512]
    xg = _dot(pg[:, 512:512 + 2 * GLA_LR].astype(BF16), wg_ref[0]) + bg_ref[0]
    la = _log_sigmoid(xg) * (1.0 / GLA_TAU)
    la_ref[...] = la
    bs_ref[:, 0:128] = _chunk_scan(la[:, 0:128], False)
    bs_ref[:, 128:256] = _chunk_scan(la[:, 128:256], True)
    ga_ref[...] = _silu(proj(R_GG)).astype(BF16)

    pm = proj(R_MLA)
    qall = _dot(_rms(pm[:, 0:256], gq_ref[0]).astype(BF16), wuq_ref[0])
    q_pe = qall[:, 512:768]
    if rope:
        q_pe = _rope(q_pe, cos, sin)
    sb = (DN_B + DR_B) ** -0.5 * LOG2E
    mq_ref[:, 0:512] = (qall[:, 0:512] * sb).astype(BF16)
    mq_ref[:, 512:768] = (q_pe * sb).astype(BF16)
    ckvn = _rms(pm[:, 256:384], gkv_ref[0])
    kvall = _dot(ckvn.astype(BF16), wukv_ref[0])
    kpe4 = _dot_nt(h, wx_ref[0])
    if rope:
        kpe4 = _rope(kpe4, cos[:, 0:128], sin[:, 0:128])
    mk_ref[:, 0:512] = kvall[:, 0:512].astype(BF16)
    mk_ref[:, 512:640] = kpe4.astype(BF16)
    mvt_ref[...] = kvall[:, 512:1024].T.astype(BF16)
    gb_ref[...] = _silu(proj(R_MG)).astype(BF16)

    pd = proj(R_DIFF)
    dq, dk, dv = pd[:, 0:256], pd[:, 256:512], pd[:, 512:768]
    if rope:
        dq = _rope(dq, cos, sin)
        dk = _rope(dk, cos, sin)
    dqk_ref[:, 0:256] = (dq * (DC ** -0.5 * LOG2E)).astype(BF16)
    dqk_ref[:, 256:512] = dk.astype(BF16)
    dv_t = dv.T
    dvt_ref[...] = dv_t.astype(BF16)
    gc_ref[...] = _silu(pd[:, 768:1024]).astype(BF16)
    if ctx_out:
        kpe_t = kpe4.T
        dk_t = dk.T
        for bb in range(bpb):
            rs = slice(bb * seq, (bb + 1) * seq)
            ckvn_ref[bb, 0] = ckvn[rs]
            kpe_ref[bb, 0] = kpe_t[0:DR_B, rs]
            kc_ref[bb, 0] = dk_t[:, rs].reshape(H_C, 2 * DC, seq)
            vc_ref[bb, 0] = dv_t[:, rs].reshape(H_C, 2 * DC, seq)


def _pre_call(x2d, mod, pw, layer, *, seq, rope_tabs, caches):
    n = x2d.shape[0]
    tm = min(TOKEN_BLOCK, n)
    bpb = max(tm // seq, 1)
    rope = rope_tabs is not None
    ctx_out = caches is not None
    steps_per_seq = max(seq // tm, 1)
    nbt = n // seq

    def mod_idx(i):
        return (layer * MOD_ROWS + ((i * tm) // seq + 1 if rope else 0), 0, 0)

    names = ["g_pre", "w_t", "w_x", "w_gate", "b_gate", "g_mla_q", "g_mla_kv", "w_uq", "w_ukv"]
    in_specs = [pl.BlockSpec((tm, D_MODEL), lambda i: (i, 0)), pl.BlockSpec((1, 1, 3 * D_MODEL), mod_idx)]
    in_specs += [_layer_spec(pw[k].shape, layer) for k in names]
    args = [x2d, mod] + [pw[k] for k in names]
    if rope:
        in_specs += [pl.BlockSpec((tm, 256), lambda i: (i % steps_per_seq, 0))] * 2
        args += list(rope_tabs)
    outs = [(256, F32, False), (256, F32, False), (256, F32, False), (256, F32, False),
            (W_A, BF16, False), (W_B, BF16, False), (W_C, BF16, False),
            (768, BF16, False), (640, BF16, False), (W_B, BF16, True),
            (512, BF16, False), (W_C, BF16, True)]
    out_specs = [pl.BlockSpec((w, tm), lambda i: (0, i)) if tr else pl.BlockSpec((tm, w), lambda i: (i, 0))
                 for w, _, tr in outs]
    out_shape = [jax.ShapeDtypeStruct((w, n) if tr else (n, w), dt) for w, dt, tr in outs]
    widths = outs
    aliases = {}
    if ctx_out:
        out_specs += [pl.BlockSpec((bpb, 1, seq, KV_LORA), lambda i: (i, layer, 0, 0)),
                      pl.BlockSpec((bpb, 1, DR_B, seq), lambda i: (i, layer, 0, 0)),
                      pl.BlockSpec((bpb, 1, H_C, 2 * DC, seq), lambda i: (i, layer, 0, 0, 0)),
                      pl.BlockSpec((bpb, 1, H_C, 2 * DC, seq), lambda i: (i, layer, 0, 0, 0))]
        out_shape += [jax.ShapeDtypeStruct((nbt, DEPTH, seq, KV_LORA), F32),
                      jax.ShapeDtypeStruct((nbt, DEPTH, DR_B, seq), F32),
                      jax.ShapeDtypeStruct((nbt, DEPTH, H_C, 2 * DC, seq), F32),
                      jax.ShapeDtypeStruct((nbt, DEPTH, H_C, 2 * DC, seq), F32)]
        for j, arr in enumerate(caches):
            aliases[len(args)] = len(widths) + j
            in_specs.append(_ANY)
            args.append(arr)
    return pl.pallas_call(
        functools.partial(_pre_kernel, rope=rope, ctx_out=ctx_out, alias_in=len(aliases), bpb=bpb, seq=seq),
        grid=(n // tm,), in_specs=in_specs, out_specs=out_specs, out_shape=out_shape,
        input_output_aliases=aliases,
        compiler_params=_cparams(), name="pre_rope" if rope else "pre_ctx",
    )(*args)


_GLA_LEVELS = (1, 2, 4, 8, 16, 32)


def _gla_consts(rev):
    c = GLA_CHUNK
    row = lax.broadcasted_iota(jnp.int32, (c, 128), 0)
    pos = (c - 1 - row) if rev else row
    ri = lax.broadcasted_iota(jnp.int32, (c, H_A * c), 0)
    cj = lax.broadcasted_iota(jnp.int32, (c, H_A * c), 1) & (c - 1)
    pi = (c - 1 - ri) if rev else ri
    pj = (c - 1 - cj) if rev else cj
    x = pi ^ pj
    lvl = jnp.where(pi == pj, 0, -1)
    for kbit in range(6):
        lvl = jnp.where((pj < pi) & ((x >> kbit) == 1), kbit + 1, lvl)
    return pos, lvl


def _chunk_scan(x, rev):
    rows = x.shape[0]
    nt = rows // SUBLANES
    tiles_per_chunk = GLA_CHUNK // SUBLANES
    x3 = x.reshape(nt, SUBLANES, 128)
    sub = lax.broadcasted_iota(jnp.int32, x3.shape, 1)
    tile = lax.broadcasted_iota(jnp.int32, x3.shape, 0) & (tiles_per_chunk - 1)
    edge = 0 if rev else SUBLANES - 1
    s = 1
    while s < SUBLANES:
        if rev:
            x3 = x3 + jnp.where(sub < SUBLANES - s, pltpu.roll(x3, SUBLANES - s, 1), 0.0)
        else:
            x3 = x3 + jnp.where(sub >= s, pltpu.roll(x3, s, 1), 0.0)
        s *= 2
    s = 1
    while s < tiles_per_chunk:
        tot = jnp.broadcast_to(x3[:, edge:edge + 1, :], x3.shape)
        if rev:
            shifted = jnp.concatenate([tot[s:], tot[:s]], axis=0)
            x3 = x3 + jnp.where(tile < tiles_per_chunk - s, shifted, 0.0)
        else:
            shifted = jnp.concatenate([tot[nt - s:], tot[:nt - s]], axis=0)
            x3 = x3 + jnp.where(tile >= s, shifted, 0.0)
        s *= 2
    return x3.reshape(rows, 128)


def _gla_scores(q, k, la, b, pos, lvl, hm_bf, rev):
    c = GLA_CHUNK
    prv = pltpu.roll(la, c - 1 if rev else 1, 0)
    nxt = pltpu.roll(la, 1 if rev else c - 1, 0)
    s_tot = jnp.where(lvl == 0, _dot_nt(q.astype(BF16), jnp.concatenate([k.astype(BF16)] * H_A, 0) * hm_bf), 0.0)
    for kbit, m in enumerate(_GLA_LEVELS):
        up = ((pos >> kbit) & 1) == 1
        if m == 1:
            e = jnp.where(up, la, 0.0)
        elif m == 2:
            c4 = pos & 3
            e = jnp.where(c4 == 0, nxt, jnp.where(c4 == 1, 0.0, jnp.where(c4 == 2, la, la + prv)))
        else:
            nblk = c // (2 * m)
            loc = m if rev else m - 1
            b3 = b.reshape(nblk, 2 * m, 128)
            ref = jnp.broadcast_to(b3[:, loc:loc + 1, :], (nblk, 2 * m, 128)).reshape(c, 128)
            dlt = b - ref
            e = jnp.where(up, dlt, -dlt)
        xm = (jnp.where(up, q, k) * jnp.exp(e)).astype(BF16)
        sm = _dot_nt(xm, jnp.concatenate([xm] * H_A, 0) * hm_bf)
        s_tot = jnp.where(lvl == kbit + 1, sm, s_tot)
    return s_tot


def _gla_apply(s_tot, q, k, v, b, st_prev, hm_f32, vm_bf, rev):
    c = GLA_CHUNK
    vbd = jnp.concatenate([v] * H_A, 0) * vm_bf
    blast = b[0:1, :] if rev else b[c - 1:c, :]
    qbar = (q * jnp.exp(b)).astype(BF16)
    kdec = (k * jnp.exp(blast - b)).astype(BF16)
    o = _dot(s_tot.astype(BF16), vbd) + _dot_nt(qbar, st_prev.astype(BF16))
    st_new = st_prev * jnp.exp(blast) + _dot_tn(v, kdec) * hm_f32
    return o, st_new


def _gla_kernel(*refs, seq, has_s0, alias_in):
    it = iter(refs)
    qk_ref, v_ref, la_ref, b_ref, gate_ref, g_ref = (next(it) for _ in range(6))
    s0_ref = next(it) if has_s0 else None
    for _ in range(alias_in):
        next(it)
    oa_ref = next(it)
    sfin_ref = None if has_s0 else next(it)
    acc_sc, st_sc, blk_sc = next(it), next(it), next(it)

    c = GLA_CHUNK
    nc = seq // c
    acc_sc[...] = jnp.zeros_like(acc_sc)
    for d in range(2):
        if has_s0:
            blk_sc[...] = jnp.zeros_like(blk_sc)
            for hh in range(H_A):
                blk_sc[DK_A * hh:DK_A * (hh + 1), DV_A * hh:DV_A * (hh + 1)] = s0_ref[0, 0, d, hh]
            st_sc[d] = blk_sc[...].T
        else:
            st_sc[d] = jnp.zeros((ST_R, ST_C), F32)

    hrow = lax.broadcasted_iota(jnp.int32, (H_A * c, 128), 0) // c
    hm_f32 = jnp.where(hrow == lax.broadcasted_iota(jnp.int32, (H_A * c, 128), 1) // DK_A, 1.0, 0.0)
    hm_bf = hm_f32.astype(BF16)
    vrow = lax.broadcasted_iota(jnp.int32, (H_A * c, H_A * DV_A), 0) // c
    vm_bf = jnp.where(vrow == lax.broadcasted_iota(jnp.int32, (H_A * c, H_A * DV_A), 1) // DV_A,
                      1.0, 0.0).astype(BF16)
    consts = (_gla_consts(False), _gla_consts(True))

    def chunk_rows(n, d):
        cn = (nc - 1 - n) if d else n
        start = cn * c
        return pl.ds(start if isinstance(start, int) else pl.multiple_of(start, c), c)

    def scores(n):
        out = []
        for d in range(2):
            rows = chunk_rows(n, d)
            pos, lvl = consts[d]
            out.append(_gla_scores(qk_ref[rows, 0:128], qk_ref[rows, 128:256],
                                   la_ref[rows, 128 * d:128 * d + 128], b_ref[rows, 128 * d:128 * d + 128],
                                   pos, lvl, hm_bf, bool(d)))
        return tuple(out)

    def apply(n, s_both):
        for d in range(2):
            rows = chunk_rows(n, d)
            o, st_new = _gla_apply(s_both[d], qk_ref[rows, 0:128], qk_ref[rows, 128:256],
                                   v_ref[rows, :].astype(BF16), b_ref[rows, 128 * d:128 * d + 128],
                                   st_sc[d], hm_f32, vm_bf, bool(d))
            acc_sc[rows, :] = acc_sc[rows, :] + o
            st_sc[d] = st_new

    def body(n, s_cur):
        s_next = scores(n + 1)
        apply(n, s_cur)
        return s_next

    s_last = lax.fori_loop(0, nc - 1, body, scores(0))
    apply(nc - 1, s_last)

    first = lax.broadcasted_iota(jnp.int32, (seq, 128), 1) < DV_A
    for p in range(H_A // 2):
        cols = slice(128 * p, 128 * (p + 1))
        oa = acc_sc[:, cols]
        sq = oa * oa
        s0 = jnp.sum(jnp.where(first, sq, 0.0), axis=-1, keepdims=True)
        s1 = jnp.sum(jnp.where(first, 0.0, sq), axis=-1, keepdims=True)
        inv = jnp.where(first, lax.rsqrt(s0 * (1.0 / DV_A) + EPS), lax.rsqrt(s1 * (1.0 / DV_A) + EPS))
        oa_ref[:, cols] = (oa * inv * g_ref[0, :, cols] * gate_ref[:, cols]).astype(BF16)
    if not has_s0:
        for d in range(2):
            blk_sc[...] = st_sc[d].T
            for hh in range(H_A):
                sfin_ref[0, 0, d, hh] = blk_sc[DK_A * hh:DK_A * (hh + 1), DV_A * hh:DV_A * (hh + 1)]


def _gla_call(qk, v, la, bsum, gate, g4, layer, *, seq, state_in=None, state_out=None):
    n = qk.shape[0]
    nb = n // seq
    has_s0 = state_in is not None
    blk = lambda w: pl.BlockSpec((seq, w), lambda i: (i, 0))
    st_spec = pl.BlockSpec((1, 1, 2, H_A, DK_A, DV_A), lambda i: (i, layer, 0, 0, 0, 0))
    in_specs = [blk(256), blk(256), blk(256), blk(256), blk(W_A), _layer_spec(g4.shape, layer)]
    args = [qk, v, la, bsum, gate, g4]
    out_specs = [blk(W_A)]
    out_shape = [jax.ShapeDtypeStruct((n, W_A), BF16)]
    aliases = {}
    if has_s0:
        in_specs.append(st_spec)
        args.append(state_in)
    else:
        out_specs.append(st_spec)
        out_shape.append(jax.ShapeDtypeStruct((nb, DEPTH, 2, H_A, DK_A, DV_A), F32))
        if state_out is not None:
            aliases[len(args)] = 1
            in_specs.append(_ANY)
            args.append(state_out)
    return pl.pallas_call(
        functools.partial(_gla_kernel, seq=seq, has_s0=has_s0, alias_in=len(aliases)),
        grid=(nb,), in_specs=in_specs, out_specs=out_specs, out_shape=out_shape,
        input_output_aliases=aliases,
        scratch_shapes=[pltpu.VMEM((seq, W_A), F32), pltpu.VMEM((2, ST_R, ST_C), F32),
                        pltpu.VMEM((ST_C, ST_R), F32)],
        compiler_params=_cparams(), name="gla_state" if has_s0 else "gla_ctx",
    )(*args)


def _softmax_t_pv(st, vt):
    m = _col_reduce(st, jnp.max)
    e = jnp.exp2(st - m)
    l = _col_reduce(e, jnp.sum)
    return _dot(vt, e.astype(BF16)) * (1.0 / l)


def _pipelined_attention(score_fns, value_fns, depth):
    outs = []
    pending = [fn() for fn in score_fns[:depth]]
    for j, vfn in enumerate(value_fns):
        st = pending.pop(0)
        if j + depth < len(score_fns):
            pending.append(score_fns[j + depth]())
        outs.append(_softmax_t_pv(st, vfn()))
    return outs


def _lookahead(n_keys):
    return 2 if n_keys >= 1024 else 3


def _col_reduce(x, op):
    rows, cols = x.shape
    part = 128 if rows % 128 == 0 and rows > 128 else rows
    if part != rows:
        x = op(x.reshape(rows // part, part, cols), axis=0)
    return op(x, axis=0, keepdims=True)


def _mla_kernel(*refs, seq, ctx_len):
    it = iter(refs)
    q_ref, k_ref, vt_ref, gate_ref = (next(it) for _ in range(4))
    ck_ref, cvt_ref = (next(it), next(it)) if ctx_len else (None, None)
    ob_ref = next(it)
    kk_sc, vt_sc = next(it), next(it)

    for p in range(H_B // 2):
        kk_sc[p, 0:seq, 0:128] = k_ref[:, 128 * p:128 * p + 128]
        kk_sc[p, 0:seq, 128:256] = k_ref[:, 512:640]
        if ctx_len:
            kk_sc[p, seq:seq + ctx_len, 0:128] = ck_ref[0, 0, :, 128 * p:128 * p + 128]
            kk_sc[p, seq:seq + ctx_len, 128:256] = ck_ref[0, 0, :, 512:640]
    vt_sc[:, 0:seq] = vt_ref[...]
    if ctx_len:
        vt_sc[:, seq:seq + ctx_len] = cvt_ref[0, 0]

    qb = min(ATT_QBLOCK, seq)

    def body(i, carry):
        rows = pl.ds(pl.multiple_of(i * qb, qb), qb)
        def scores(h):
            p, hh = divmod(h, 2)
            qn = q_ref[rows, 128 * p:128 * p + 128] * _lane_mask(128, DN_B * hh, DN_B, BF16)
            qp = (q_ref[rows, 512 + 128 * (h // 4):512 + 128 * (h // 4) + 128]
                  * _lane_mask(128, DR_B * (h % 4), DR_B, BF16))
            return _dot_nt(kk_sc[p], jnp.concatenate([qn, qp], axis=-1))

        outs = _pipelined_attention(
            [functools.partial(scores, h) for h in range(H_B)],
            [functools.partial(lambda h: vt_sc[DV_B * h:DV_B * (h + 1), :], h) for h in range(H_B)],
            _lookahead(seq + ctx_len))
        ob = jnp.concatenate(outs, axis=0).T
        ob_ref[rows, :] = (ob * gate_ref[rows, :]).astype(BF16)
        return carry

    lax.fori_loop(0, seq // qb, body, 0)


def _mla_call(mq, mk, mvt, gate, ctx_k, ctx_vt, layer, *, seq):
    n = mq.shape[0]
    ctx_len = 0 if ctx_k is None else ctx_k.shape[2]
    in_specs = [pl.BlockSpec((seq, 768), lambda i: (i, 0)), pl.BlockSpec((seq, 640), lambda i: (i, 0)),
                pl.BlockSpec((W_B, seq), lambda i: (0, i)), pl.BlockSpec((seq, W_B), lambda i: (i, 0))]
    args = [mq, mk, mvt, gate]
    if ctx_len:
        in_specs += [pl.BlockSpec((1, 1, ctx_len, 640), lambda i: (layer, i, 0, 0)),
                     pl.BlockSpec((1, 1, W_B, ctx_len), lambda i: (layer, i, 0, 0))]
        args += [ctx_k, ctx_vt]
    tk = seq + ctx_len
    return pl.pallas_call(
        functools.partial(_mla_kernel, seq=seq, ctx_len=ctx_len),
        grid=(n // seq,), in_specs=in_specs,
        out_specs=pl.BlockSpec((seq, W_B), lambda i: (i, 0)),
        out_shape=jax.ShapeDtypeStruct((n, W_B), BF16),
        scratch_shapes=[pltpu.VMEM((H_B // 2, tk, 256), BF16), pltpu.VMEM((W_B, tk), BF16)],
        compiler_params=_cparams(), name="mla_ctx" if ctx_len else "mla_self",
    )(*args)


def _diff_kernel(*refs, seq, ctx_len, lam_init):
    it = iter(refs)
    qk_ref, vt_ref, gate_ref, lam_ref, g_ref = (next(it) for _ in range(5))
    ck_ref, cv_ref = (next(it), next(it)) if ctx_len else (None, None)
    oc_ref = next(it)
    k_sc, vt_sc = next(it), next(it)

    lam = (jnp.exp(jnp.sum(lam_ref[0, 0:1, :] * lam_ref[0, 1:2, :], axis=-1, keepdims=True))
           - jnp.exp(jnp.sum(lam_ref[0, 2:3, :] * lam_ref[0, 3:4, :], axis=-1, keepdims=True)) + lam_init)
    dh = 2 * DC
    for p in range(H_C // 2):
        k_sc[p, 0:seq, :] = qk_ref[:, 256 + 128 * p:256 + 128 * p + 128]
        if ctx_len:
            pair_t = jnp.concatenate([ck_ref[0, 0, 2 * p], ck_ref[0, 0, 2 * p + 1]], axis=0)
            k_sc[p, seq:seq + ctx_len, :] = pair_t.T.astype(BF16)
    vt_sc[:, 0:seq] = vt_ref[...]
    if ctx_len:
        for h in range(H_C):
            vt_sc[dh * h:dh * (h + 1), seq:seq + ctx_len] = cv_ref[0, 0, h].astype(BF16)

    qb = min(ATT_QBLOCK, seq)

    def body(i, carry):
        rows = pl.ds(pl.multiple_of(i * qb, qb), qb)
        def scores(h, comp):
            p, hh = divmod(h, 2)
            qm = qk_ref[rows, 128 * p:128 * p + 128] * _lane_mask(128, dh * hh + DC * comp, DC, BF16)
            return _dot_nt(k_sc[p], qm)

        items = [(h, comp) for h in range(H_C) for comp in range(2)]
        o12 = _pipelined_attention(
            [functools.partial(scores, h, comp) for h, comp in items],
            [functools.partial(lambda h: vt_sc[dh * h:dh * (h + 1), :], h) for h, _ in items],
            _lookahead(seq + ctx_len))
        outs = []
        for h in range(H_C):
            ot = o12[2 * h] - lam * o12[2 * h + 1]
            outs.append(ot * lax.rsqrt(jnp.mean(ot * ot, axis=0, keepdims=True) + EPS))
        oc = jnp.concatenate(outs, axis=0).T
        oc_ref[rows, :] = (oc * g_ref[0] * (1.0 - lam_init) * gate_ref[rows, :]).astype(BF16)
        return carry

    lax.fori_loop(0, seq // qb, body, 0)


def _diff_call(dqk, dvt, gate, lamp, g4, ck_t, cv_t, layer, *, seq):
    n = dqk.shape[0]
    ctx_len = 0 if ck_t is None else ck_t.shape[4]
    lam_init = 0.8 - 0.6 * math.exp(-0.3 * layer)
    in_specs = [pl.BlockSpec((seq, 512), lambda i: (i, 0)), pl.BlockSpec((W_C, seq), lambda i: (0, i)),
                pl.BlockSpec((seq, W_C), lambda i: (i, 0)),
                _layer_spec(lamp.shape, layer), _layer_spec(g4.shape, layer)]
    args = [dqk, dvt, gate, lamp, g4]
    if ctx_len:
        in_specs += [pl.BlockSpec((1, 1, H_C, 2 * DC, ctx_len), lambda i: (i, layer, 0, 0, 0))] * 2
        args += [ck_t, cv_t]
    tk = seq + ctx_len
    return pl.pallas_call(
        functools.partial(_diff_kernel, seq=seq, ctx_len=ctx_len, lam_init=lam_init),
        grid=(n // seq,), in_specs=in_specs,
        out_specs=pl.BlockSpec((seq, W_C), lambda i: (i, 0)),
        out_shape=jax.ShapeDtypeStruct((n, W_C), BF16),
        scratch_shapes=[pltpu.VMEM((H_C // 2, tk, 128), BF16), pltpu.VMEM((W_C, tk), BF16)],
        compiler_params=_cparams(), name="diff_ctx" if ctx_len else "diff_self",
    )(*args)


def _post_kernel(oa_ref, ob_ref, oc_ref, x_ref, mod_ref, w_ref, g_ref, y_ref):
    mix = jnp.concatenate([oa_ref[...], ob_ref[...], oc_ref[...]], axis=-1)
    out = _rms(_dot(mix, w_ref[0]), g_ref[0])
    y_ref[...] = x_ref[...] + mod_ref[0, :, 2 * D_MODEL:3 * D_MODEL] * out


def _post_call(oa, ob, oc, x2d, mod, pw, layer, *, seq, sample):
    n = x2d.shape[0]
    tm = min(POST_BLOCK, seq if sample else n)
    blk = lambda w: pl.BlockSpec((tm, w), lambda i: (i, 0))

    def mod_idx(i):
        return (layer * MOD_ROWS + ((i * tm) // seq + 1 if sample else 0), 0, 0)

    return pl.pallas_call(
        _post_kernel,
        grid=(n // tm,),
        in_specs=[blk(W_A), blk(W_B), blk(W_C), blk(D_MODEL),
                  pl.BlockSpec((1, 1, 3 * D_MODEL), mod_idx),
                  _layer_spec(pw["w_out"].shape, layer), _layer_spec(pw["g_post"].shape, layer)],
        out_specs=blk(D_MODEL),
        out_shape=jax.ShapeDtypeStruct((n, D_MODEL), F32),
        compiler_params=_cparams(), name="post",
    )(oa, ob, oc, x2d, mod, pw["w_out"], pw["g_post"])


def _pack_params(g_pre, g_post, w_in, w_gla_af, b_gla_af, w_gla_ab, b_gla_ab, g_gla, g_mla_q, w_mla_uq,
                 g_mla_kv, w_mla_ukv, lam_q1, lam_k1, lam_q2, lam_k2, g_diff, w_out):
    w_t = jnp.swapaxes(w_in, 1, 2)
    kpe_t = w_t[:, R_MLA[1] - DR_B:R_MLA[1]]
    w_x = jnp.concatenate([kpe_t] * 4, axis=1).astype(BF16)
    zg = jnp.zeros((DEPTH, GLA_LR, 128), F32)
    w_gate = jnp.concatenate([jnp.concatenate([w_gla_af, zg], axis=-1),
                              jnp.concatenate([zg, w_gla_ab], axis=-1)], axis=1).astype(BF16)
    uq = w_mla_uq.reshape(DEPTH, Q_LORA, H_B, DN_B + DR_B)
    w_pe = uq[..., DN_B:].reshape(DEPTH, Q_LORA, H_B * DR_B)
    w_uq = jnp.concatenate([uq[..., :DN_B].reshape(DEPTH, Q_LORA, H_B * DN_B), w_pe], axis=-1).astype(BF16)
    ukv = w_mla_ukv.reshape(DEPTH, KV_LORA, H_B, DN_B + DV_B)
    w_ukv = jnp.concatenate([ukv[..., :DN_B].reshape(DEPTH, KV_LORA, H_B * DN_B),
                             ukv[..., DN_B:].reshape(DEPTH, KV_LORA, H_B * DV_B)], axis=-1).astype(BF16)
    row = lambda a: a.reshape(DEPTH, 1, a.shape[-1])
    return dict(
        w_t=w_t.astype(BF16), w_x=w_x, w_gate=w_gate,
        b_gate=row(jnp.concatenate([b_gla_af, b_gla_ab], axis=-1)),
        w_uq=w_uq, w_ukv=w_ukv, w_out=w_out.astype(BF16),
        g_pre=row(g_pre), g_post=row(g_post), g_mla_q=row(g_mla_q), g_mla_kv=row(g_mla_kv),
        g_gla4=row(jnp.tile(g_gla, (1, H_A))), g_diff4=row(jnp.tile(g_diff, (1, H_C))),
        lam=jnp.stack([lam_q1, lam_k1, lam_q2, lam_k2], axis=1))


def _rope_tables(n):
    t = np.arange(n)
    row = (t // GRID_W).astype(np.float32)
    col = (t % GRID_W).astype(np.float32)
    half = ROPE_DIM // 2
    inv = (1.0 / (np.float32(ROPE_THETA) ** (np.arange(0, half, 2, dtype=np.float32) / np.float32(half)))
           ).astype(np.float32)
    ar = row[:, None] * inv
    ac = col[:, None] * inv
    ang = np.concatenate([ar, ar, ac, ac], axis=-1).astype(np.float32)
    return (jnp.asarray(np.tile(np.cos(ang), (1, 8)).astype(np.float32)),
            jnp.asarray(np.tile(np.sin(ang), (1, 8)).astype(np.float32)))


def _sublayer(x2d, mod, pw, layer, *, seq, rope_tabs, ctx, caches):
    sample = ctx is not None
    pre = _pre_call(x2d, mod, pw, layer, seq=seq, rope_tabs=rope_tabs,
                    caches=None if sample else caches[:4])
    qk, v, la, bsum, ga, gb, gc, mq, mk, mvt, dqk, dvt = pre[:12]
    if sample:
        (oa,) = _gla_call(qk, v, la, bsum, ga, pw["g_gla4"], layer, seq=seq, state_in=ctx["state"])
        ob = _mla_call(mq, mk, mvt, gb, ctx["mla_k"], ctx["mla_vt"], layer, seq=seq)
        oc = _diff_call(dqk, dvt, gc, pw["lam"], pw["g_diff4"], ctx["diff_k_t"], ctx["diff_v_t"], layer,
                        seq=seq)
        new_caches = None
    else:
        oa, sfin = _gla_call(qk, v, la, bsum, ga, pw["g_gla4"], layer, seq=seq,
                             state_out=caches[4])
        ob = _mla_call(mq, mk, mvt, gb, None, None, layer, seq=seq)
        oc = _diff_call(dqk, dvt, gc, pw["lam"], pw["g_diff4"], None, None, layer, seq=seq)
        new_caches = tuple(pre[12:]) + (sfin,)
    y = _post_call(oa, ob, oc, x2d, mod, pw, layer, seq=seq, sample=sample)
    return y, new_caches


def kernel(x_prompt, x_sample, c, cache_mla_ckv, cache_mla_kpe, cache_diff_k, cache_diff_v, state_gla,
           c_ctx, w_ada, b_ada, g_pre, g_post, w_in, w_gla_af, b_gla_af, w_gla_ab, b_gla_ab, g_gla,
           g_mla_q, w_mla_uq, g_mla_kv, w_mla_ukv, lam_q1, lam_k1, lam_q2, lam_k2, g_diff, w_out):
    bp, tp, d = x_prompt.shape
    bs, ts, _ = x_sample.shape

    pw = _pack_params(g_pre, g_post, w_in, w_gla_af, b_gla_af, w_gla_ab, b_gla_ab, g_gla, g_mla_q,
                      w_mla_uq, g_mla_kv, w_mla_ukv, lam_q1, lam_k1, lam_q2, lam_k2, g_diff, w_out)
    cvecs = jnp.concatenate([c_ctx[None], c, jnp.zeros((MOD_ROWS - 1 - bs, d), F32)], axis=0)
    mod = _mod_call(cvecs, w_ada, b_ada).reshape(DEPTH * MOD_ROWS, 1, 3 * d)
    rope_tabs = _rope_tables(ts)
    ctx_k, ctx_vt = _ctxkv_call(cache_mla_ckv, jnp.swapaxes(cache_mla_kpe, -1, -2), pw["w_ukv"])
    ctx = dict(state=state_gla, mla_k=ctx_k, mla_vt=ctx_vt,
               diff_k_t=jnp.swapaxes(cache_diff_k, -1, -2), diff_v_t=jnp.swapaxes(cache_diff_v, -1, -2))

    y_p = x_prompt.reshape(bp * tp, d)
    y_s = x_sample.reshape(bs * ts, d)
    caches = tuple(jnp.zeros(s, F32) for s in (
        (bp, DEPTH, tp, KV_LORA), (bp, DEPTH, DR_B, tp), (bp, DEPTH, H_C, 2 * DC, tp),
        (bp, DEPTH, H_C, 2 * DC, tp), (bp, DEPTH, 2, H_A, DK_A, DV_A)))
    for l in range(DEPTH):
        y_p, caches = _sublayer(y_p, mod, pw, l, seq=tp, rope_tabs=None, ctx=None, caches=caches)
        y_s, _ = _sublayer(y_s, mod, pw, l, seq=ts, rope_tabs=rope_tabs, ctx=ctx, caches=None)
    ckvn, kpe_t, kc_t, vc_t, new_state = caches
    return (y_p.reshape(bp, tp, d), y_s.reshape(bs, ts, d), ckvn, jnp.swapaxes(kpe_t, -1, -2),
            jnp.swapaxes(kc_t, -1, -2), jnp.swapaxes(vc_t, -1, -2), new_state)
```

```python
import functools
import math

import numpy as np
import jax
import jax.numpy as jnp
from jax import lax
from jax.experimental import pallas as pl
from jax.experimental.pallas import tpu as pltpu

F32 = jnp.float32
BF16 = jnp.bfloat16

D_MODEL = 1024
DEPTH = 2
GRID_W = 64
EPS = 1e-6
ROPE_THETA = 10000.0
ROPE_DIM = 32
H_A, DK_A, DV_A = 4, 32, 64
GLA_LR = 16
GLA_TAU = 16.0
GLA_CHUNK = 64
H_B, DN_B, DR_B, DV_B = 8, 64, 32, 64
Q_LORA, KV_LORA = 256, 128
H_C, DC = 4, 32
W_A, W_B, W_C = H_A * DV_A, H_B * DV_B, H_C * 2 * DC
ST_R, ST_C = H_A * DV_A, H_A * DK_A
LOG2E = math.log2(math.e)
SUBLANES = 8

R_GLA = (0, 544)
R_GG = (544, 800)
R_MLA = (800, 1216)
R_MG = (1216, 1728)
R_DIFF = (1728, 2752)
N_IN = 2752
MOD_ROWS = 8

V7X_VMEM_LIMIT_BYTES = 56 * 1024 * 1024
TOKEN_BLOCK = 1024
POST_BLOCK = 1024
ATT_QBLOCK = 512


def _cparams(n_axes=1):
    return pltpu.CompilerParams(dimension_semantics=("arbitrary",) * n_axes,
                                vmem_limit_bytes=V7X_VMEM_LIMIT_BYTES)


def _rms(x, g):
    return x * lax.rsqrt(jnp.mean(x * x, axis=-1, keepdims=True) + EPS) * g


def _silu(x):
    return x * jax.nn.sigmoid(x)


def _log_sigmoid(x):
    return jnp.minimum(x, 0.0) - jnp.log1p(jnp.exp(-jnp.abs(x)))


def _rope(z, cos, sin):
    w = z.shape[-1]
    lane = lax.broadcasted_iota(jnp.int32, z.shape, 1)
    rot = jnp.where((lane & 15) < 8, -pltpu.roll(z, w - 8, 1), pltpu.roll(z, 8, 1))
    return z * cos + rot * sin


def _lane_mask(width, lo, size, dtype):
    lane = lax.broadcasted_iota(jnp.int32, (1, width), 1)
    return jnp.where((lane >= lo) & (lane < lo + size), 1.0, 0.0).astype(dtype)


def _dot(a, b):
    return jnp.dot(a, b, preferred_element_type=F32)


def _dot_nt(a, b):
    return lax.dot_general(a, b, (((1,), (1,)), ((), ())), preferred_element_type=F32)


def _dot_tn(a, b):
    return lax.dot_general(a, b, (((0,), (0,)), ((), ())), preferred_element_type=F32)


def _layer_spec(shape, layer):
    nd = len(shape)
    return pl.BlockSpec((1,) + tuple(shape[1:]), lambda *_: (layer,) + (0,) * (nd - 1))


_ANY = pl.BlockSpec(memory_space=pl.ANY)


def _mod_kernel(c_ref, w_ref, b_ref, o_ref):
    s = _silu(c_ref[...]).astype(BF16)
    o_ref[0] = _dot(s, w_ref[0].astype(BF16)) + b_ref[0]


def _mod_call(cvecs, w_ada, b_ada):
    nb = 1024
    return pl.pallas_call(
        _mod_kernel,
        grid=(DEPTH, 3 * D_MODEL // nb),
        in_specs=[pl.BlockSpec((MOD_ROWS, D_MODEL), lambda l, j: (0, 0)),
                  pl.BlockSpec((1, D_MODEL, nb), lambda l, j: (l, 0, j)),
                  pl.BlockSpec((1, 1, nb), lambda l, j: (l, 0, j))],
        out_specs=pl.BlockSpec((1, MOD_ROWS, nb), lambda l, j: (l, 0, j)),
        out_shape=jax.ShapeDtypeStruct((DEPTH, MOD_ROWS, 3 * D_MODEL), F32),
        compiler_params=_cparams(2), name="adaln_mod",
    )(cvecs, w_ada, b_ada.reshape(DEPTH, 1, 3 * D_MODEL))


def _ctxkv_kernel(ckv_ref, kpe_ref, w_ref, k_ref, vt_ref):
    kv = _dot(ckv_ref[0, 0].astype(BF16), w_ref[0])
    kpe4 = jnp.concatenate([kpe_ref[0, 0]] * 4, axis=0).T
    k_ref[0, 0, :, 0:512] = kv[:, 0:512].astype(BF16)
    k_ref[0, 0, :, 512:640] = kpe4.astype(BF16)
    vt_ref[0, 0] = kv[:, 512:1024].T.astype(BF16)


def _ctxkv_call(cache_ckv, cache_kpe_t, wukv):
    nb, _, tc, _ = cache_ckv.shape
    return pl.pallas_call(
        _ctxkv_kernel,
        grid=(DEPTH, nb),
        in_specs=[pl.BlockSpec((1, 1, tc, KV_LORA), lambda l, b: (b, l, 0, 0)),
                  pl.BlockSpec((1, 1, DR_B, tc), lambda l, b: (b, l, 0, 0)),
                  pl.BlockSpec((1, KV_LORA, 1024), lambda l, b: (l, 0, 0))],
        out_specs=[pl.BlockSpec((1, 1, tc, 640), lambda l, b: (l, b, 0, 0)),
                   pl.BlockSpec((1, 1, W_B, tc), lambda l, b: (l, b, 0, 0))],
        out_shape=[jax.ShapeDtypeStruct((DEPTH, nb, tc, 640), BF16),
                   jax.ShapeDtypeStruct((DEPTH, nb, W_B, tc), BF16)],
        compiler_params=_cparams(2), name="mla_ctx_kv",
    )(cache_ckv, cache_kpe_t, wukv)


def _pre_kernel(*refs, rope, ctx_out, alias_in, bpb, seq):
    it = iter(refs)
    (x_ref, mod_ref, gpre_ref, w_ref, wx_ref, wg_ref, bg_ref, gq_ref, gkv_ref, wuq_ref,
     wukv_ref) = (next(it) for _ in range(11))
    if rope:
        cos_ref, sin_ref = next(it), next(it)
    for _ in range(alias_in):
        next(it)
    (qk_ref, v_ref, la_ref, bs_ref, ga_ref, gb_ref, gc_ref, mq_ref, mk_ref, mvt_ref, dqk_ref,
     dvt_ref) = (next(it) for _ in range(12))
    if ctx_out:
        ckvn_ref, kpe_ref, kc_ref, vc_ref = (next(it) for _ in range(4))

    d = D_MODEL
    shift = mod_ref[0, :, 0:d]
    scale = mod_ref[0, :, d:2 * d]
    h = (_rms(x_ref[...], gpre_ref[0]) * (1.0 + scale) + shift).astype(BF16)
    proj = lambda r: _dot_nt(h, w_ref[0, r[0]:r[1], :])
    if rope:
        cos = cos_ref[...]
        sin = sin_ref[...]

    pg = proj(R_GLA)
    qk_ref[:, 0:128] = pg[:, 0:128] * (DK_A ** -0.5)
    qk_ref[:, 128:256] = pg[:, 128:256]
    v_ref[...] = pg[:, 256:512]
    xg = _dot(pg[:, 512:512 + 2 * GLA_LR].astype(BF16), wg_ref[0]) + bg_ref[0]
    la = _log_sigmoid(xg) * (1.0 / GLA_TAU)
    la_ref[...] = la
    bs_ref[:, 0:128] = _chunk_scan(la[:, 0:128], False)
    bs_ref[:, 128:256] = _chunk_scan(la[:, 128:256], True)
    ga_ref[...] = _silu(proj(R_GG)).astype(BF16)

    pm = proj(R_MLA)
    qall = _dot(_rms(pm[:, 0:256], gq_ref[0]).astype(BF16), wuq_ref[0])
    q_pe = qall[:, 512:768]
    if rope:
        q_pe = _rope(q_pe, cos, sin)
    sb = (DN_B + DR_B) ** -0.5 * LOG2E
    mq_ref[:, 0:512] = (qall[:, 0:512] * sb).astype(BF16)
    mq_ref[:, 512:768] = (q_pe * sb).astype(BF16)
    ckvn = _rms(pm[:, 256:384], gkv_ref[0])
    kvall = _dot(ckvn.astype(BF16), wukv_ref[0])
    kpe4 = _dot_nt(h, wx_ref[0])
    if rope:
        kpe4 = _rope(kpe4, cos[:, 0:128], sin[:, 0:128])
    mk_ref[:, 0:512] = kvall[:, 0:512].astype(BF16)
    mk_ref[:, 512:640] = kpe4.astype(BF16)
    mvt_ref[...] = kvall[:, 512:1024].T.astype(BF16)
    gb_ref[...] = _silu(proj(R_MG)).astype(BF16)

    pd = proj(R_DIFF)
    dq, dk, dv = pd[:, 0:256], pd[:, 256:512], pd[:, 512:768]
    if rope:
        dq = _rope(dq, cos, sin)
        dk = _rope(dk, cos, sin)
    dqk_ref[:, 0:256] = (dq * (DC ** -0.5 * LOG2E)).astype(BF16)
    dqk_ref[:, 256:512] = dk.astype(BF16)
    dv_t = dv.T
    dvt_ref[...] = dv_t.astype(BF16)
    gc_ref[...] = _silu(pd[:, 768:1024]).astype(BF16)
    if ctx_out:
        kpe_t = kpe4.T
        dk_t = dk.T
        for ref in () if alias_in else (ckvn_ref, kpe_ref, kc_ref, vc_ref):
            ref[:, 1:] = jnp.zeros(ref[:, 1:].shape, F32)
        for bb in range(bpb):
            rs = slice(bb * seq, (bb + 1) * seq)
            ckvn_ref[bb, 0] = ckvn[rs]
            kpe_ref[bb, 0] = kpe_t[0:DR_B, rs]
            kc_ref[bb, 0] = dk_t[:, rs].reshape(H_C, 2 * DC, seq)
            vc_ref[bb, 0] = dv_t[:, rs].reshape(H_C, 2 * DC, seq)


def _pre_call(x2d, mod, pw, layer, *, seq, rope_tabs, caches):
    n = x2d.shape[0]
    tm = min(TOKEN_BLOCK, n)
    bpb = max(tm // seq, 1)
    rope = rope_tabs is not None
    ctx_out = caches is not None
    steps_per_seq = max(seq // tm, 1)
    nbt = n // seq

    def mod_idx(i):
        return (layer * MOD_ROWS + ((i * tm) // seq + 1 if rope else 0), 0, 0)

    names = ["g_pre", "w_t", "w_x", "w_gate", "b_gate", "g_mla_q", "g_mla_kv", "w_uq", "w_ukv"]
    in_specs = [pl.BlockSpec((tm, D_MODEL), lambda i: (i, 0)), pl.BlockSpec((1, 1, 3 * D_MODEL), mod_idx)]
    in_specs += [_layer_spec(pw[k].shape, layer) for k in names]
    args = [x2d, mod] + [pw[k] for k in names]
    if rope:
        in_specs += [pl.BlockSpec((tm, 256), lambda i: (i % steps_per_seq, 0))] * 2
        args += list(rope_tabs)
    outs = [(256, F32, False), (256, F32, False), (256, F32, False), (256, F32, False),
            (W_A, BF16, False), (W_B, BF16, False), (W_C, BF16, False),
            (768, BF16, False), (640, BF16, False), (W_B, BF16, True),
            (512, BF16, False), (W_C, BF16, True)]
    out_specs = [pl.BlockSpec((w, tm), lambda i: (0, i)) if tr else pl.BlockSpec((tm, w), lambda i: (i, 0))
                 for w, _, tr in outs]
    out_shape = [jax.ShapeDtypeStruct((w, n) if tr else (n, w), dt) for w, dt, tr in outs]
    widths = outs
    aliases = {}
    if ctx_out:
        assert caches or layer == 0
        nl = 1 if caches else DEPTH
        out_specs += [pl.BlockSpec((bpb, nl, seq, KV_LORA), lambda i: (i, layer, 0, 0)),
                      pl.BlockSpec((bpb, nl, DR_B, seq), lambda i: (i, layer, 0, 0)),
                      pl.BlockSpec((bpb, nl, H_C, 2 * DC, seq), lambda i: (i, layer, 0, 0, 0)),
                      pl.BlockSpec((bpb, nl, H_C, 2 * DC, seq), lambda i: (i, layer, 0, 0, 0))]
        out_shape += [jax.ShapeDtypeStruct((nbt, DEPTH, seq, KV_LORA), F32),
                      jax.ShapeDtypeStruct((nbt, DEPTH, DR_B, seq), F32),
                      jax.ShapeDtypeStruct((nbt, DEPTH, H_C, 2 * DC, seq), F32),
                      jax.ShapeDtypeStruct((nbt, DEPTH, H_C, 2 * DC, seq), F32)]
        for j, arr in enumerate(caches):
            aliases[len(args)] = len(widths) + j
            in_specs.append(_ANY)
            args.append(arr)
    return pl.pallas_call(
        functools.partial(_pre_kernel, rope=rope, ctx_out=ctx_out, alias_in=len(aliases), bpb=bpb, seq=seq),
        grid=(n // tm,), in_specs=in_specs, out_specs=out_specs, out_shape=out_shape,
        input_output_aliases=aliases,
        compiler_params=_cparams(), name="pre_rope" if rope else "pre_ctx",
    )(*args)


_GLA_LEVELS = (1, 2, 4, 8, 16, 32)


def _gla_consts(rev):
    c = GLA_CHUNK
    row = lax.broadcasted_iota(jnp.int32, (c, 128), 0)
    pos = (c - 1 - row) if rev else row
    ri = lax.broadcasted_iota(jnp.int32, (c, H_A * c), 0)
    cj = lax.broadcasted_iota(jnp.int32, (c, H_A * c), 1) & (c - 1)
    pi = (c - 1 - ri) if rev else ri
    pj = (c - 1 - cj) if rev else cj
    x = pi ^ pj
    lvl = jnp.where(pi == pj, 0, -1)
    for kbit in range(6):
        lvl = jnp.where((pj < pi) & ((x >> kbit) == 1), kbit + 1, lvl)
    return pos, lvl


def _chunk_scan(x, rev):
    rows = x.shape[0]
    nt = rows // SUBLANES
    tiles_per_chunk = GLA_CHUNK // SUBLANES
    x3 = x.reshape(nt, SUBLANES, 128)
    sub = lax.broadcasted_iota(jnp.int32, x3.shape, 1)
    tile = lax.broadcasted_iota(jnp.int32, x3.shape, 0) & (tiles_per_chunk - 1)
    edge = 0 if rev else SUBLANES - 1
    s = 1
    while s < SUBLANES:
        if rev:
            x3 = x3 + jnp.where(sub < SUBLANES - s, pltpu.roll(x3, SUBLANES - s, 1), 0.0)
        else:
            x3 = x3 + jnp.where(sub >= s, pltpu.roll(x3, s, 1), 0.0)
        s *= 2
    s = 1
    while s < tiles_per_chunk:
        tot = jnp.broadcast_to(x3[:, edge:edge + 1, :], x3.shape)
        if rev:
            shifted = jnp.concatenate([tot[s:], tot[:s]], axis=0)
            x3 = x3 + jnp.where(tile < tiles_per_chunk - s, shifted, 0.0)
        else:
            shifted = jnp.concatenate([tot[nt - s:], tot[:nt - s]], axis=0)
            x3 = x3 + jnp.where(tile >= s, shifted, 0.0)
        s *= 2
    return x3.reshape(rows, 128)


def _gla_scores(q, k, la, b, pos, lvl, hm_bf, rev):
    c = GLA_CHUNK
    prv = pltpu.roll(la, c - 1 if rev else 1, 0)
    nxt = pltpu.roll(la, 1 if rev else c - 1, 0)
    s_tot = jnp.where(lvl == 0, _dot_nt(q.astype(BF16), jnp.concatenate([k.astype(BF16)] * H_A, 0) * hm_bf), 0.0)
    for kbit, m in enumerate(_GLA_LEVELS):
        up = ((pos >> kbit) & 1) == 1
        if m == 1:
            e = jnp.where(up, la, 0.0)
        elif m == 2:
            c4 = pos & 3
            e = jnp.where(c4 == 0, nxt, jnp.where(c4 == 1, 0.0, jnp.where(c4 == 2, la, la + prv)))
        else:
            nblk = c // (2 * m)
            loc = m if rev else m - 1
            b3 = b.reshape(nblk, 2 * m, 128)
            ref = jnp.broadcast_to(b3[:, loc:loc + 1, :], (nblk, 2 * m, 128)).reshape(c, 128)
            dlt = b - ref
            e = jnp.where(up, dlt, -dlt)
        xm = (jnp.where(up, q, k) * jnp.exp(e)).astype(BF16)
        sm = _dot_nt(xm, jnp.concatenate([xm] * H_A, 0) * hm_bf)
        s_tot = jnp.where(lvl == kbit + 1, sm, s_tot)
    return s_tot


def _gla_apply(s_tot, q, k, v, b, st_prev, hm_f32, vm_bf, rev):
    c = GLA_CHUNK
    vbd = jnp.concatenate([v] * H_A, 0) * vm_bf
    blast = b[0:1, :] if rev else b[c - 1:c, :]
    qbar = (q * jnp.exp(b)).astype(BF16)
    kdec = (k * jnp.exp(blast - b)).astype(BF16)
    o = _dot(s_tot.astype(BF16), vbd) + _dot_nt(qbar, st_prev.astype(BF16))
    st_new = st_prev * jnp.exp(blast) + _dot_tn(v, kdec) * hm_f32
    return o, st_new


def _gla_kernel(*refs, seq, has_s0, alias_in):
    it = iter(refs)
    qk_ref, v_ref, la_ref, b_ref, gate_ref, g_ref = (next(it) for _ in range(6))
    s0_ref = next(it) if has_s0 else None
    for _ in range(alias_in):
        next(it)
    oa_ref = next(it)
    sfin_ref = None if has_s0 else next(it)
    acc_sc, st_sc, blk_sc = next(it), next(it), next(it)

    c = GLA_CHUNK
    nc = seq // c
    acc_sc[...] = jnp.zeros_like(acc_sc)
    for d in range(2):
        if has_s0:
            blk_sc[...] = jnp.zeros_like(blk_sc)
            for hh in range(H_A):
                blk_sc[DK_A * hh:DK_A * (hh + 1), DV_A * hh:DV_A * (hh + 1)] = s0_ref[0, 0, d, hh]
            st_sc[d] = blk_sc[...].T
        else:
            st_sc[d] = jnp.zeros((ST_R, ST_C), F32)

    hrow = lax.broadcasted_iota(jnp.int32, (H_A * c, 128), 0) // c
    hm_f32 = jnp.where(hrow == lax.broadcasted_iota(jnp.int32, (H_A * c, 128), 1) // DK_A, 1.0, 0.0)
    hm_bf = hm_f32.astype(BF16)
    vrow = lax.broadcasted_iota(jnp.int32, (H_A * c, H_A * DV_A), 0) // c
    vm_bf = jnp.where(vrow == lax.broadcasted_iota(jnp.int32, (H_A * c, H_A * DV_A), 1) // DV_A,
                      1.0, 0.0).astype(BF16)
    consts = (_gla_consts(False), _gla_consts(True))

    def chunk_rows(n, d):
        cn = (nc - 1 - n) if d else n
        start = cn * c
        return pl.ds(start if isinstance(start, int) else pl.multiple_of(start, c), c)

    def scores(n):
        out = []
        for d in range(2):
            rows = chunk_rows(n, d)
            pos, lvl = consts[d]
            out.append(_gla_scores(qk_ref[rows, 0:128], qk_ref[rows, 128:256],
                                   la_ref[rows, 128 * d:128 * d + 128], b_ref[rows, 128 * d:128 * d + 128],
                                   pos, lvl, hm_bf, bool(d)))
        return tuple(out)

    def apply(n, s_both):
        for d in range(2):
            rows = chunk_rows(n, d)
            o, st_new = _gla_apply(s_both[d], qk_ref[rows, 0:128], qk_ref[rows, 128:256],
                                   v_ref[rows, :].astype(BF16), b_ref[rows, 128 * d:128 * d + 128],
                                   st_sc[d], hm_f32, vm_bf, bool(d))
            acc_sc[rows, :] = acc_sc[rows, :] + o
            st_sc[d] = st_new

    def body(n, s_cur):
        s_next = scores(n + 1)
        apply(n, s_cur)
        return s_next

    s_last = lax.fori_loop(0, nc - 1, body, scores(0), unroll=3 if (nc - 1) % 3 == 0 else 1)
    apply(nc - 1, s_last)

    first = lax.broadcasted_iota(jnp.int32, (seq, 128), 1) < DV_A
    for p in range(H_A // 2):
        cols = slice(128 * p, 128 * (p + 1))
        oa = acc_sc[:, cols]
        sq = oa * oa
        s0 = jnp.sum(jnp.where(first, sq, 0.0), axis=-1, keepdims=True)
        s1 = jnp.sum(jnp.where(first, 0.0, sq), axis=-1, keepdims=True)
        inv = jnp.where(first, lax.rsqrt(s0 * (1.0 / DV_A) + EPS), lax.rsqrt(s1 * (1.0 / DV_A) + EPS))
        oa_ref[:, cols] = (oa * inv * g_ref[0, :, cols] * gate_ref[:, cols]).astype(BF16)
    if not has_s0:
        if not alias_in:
            sfin_ref[:, 1:] = jnp.zeros(sfin_ref[:, 1:].shape, F32)
        for d in range(2):
            blk_sc[...] = st_sc[d].T
            for hh in range(H_A):
                sfin_ref[0, 0, d, hh] = blk_sc[DK_A * hh:DK_A * (hh + 1), DV_A * hh:DV_A * (hh + 1)]


def _gla_call(qk, v, la, bsum, gate, g4, layer, *, seq, state_in=None, state_out=None):
    n = qk.shape[0]
    nb = n // seq
    has_s0 = state_in is not None
    assert has_s0 or state_out is not None or layer == 0
    blk = lambda w: pl.BlockSpec((seq, w), lambda i: (i, 0))
    nl = DEPTH if (not has_s0 and state_out is None) else 1
    st_spec = pl.BlockSpec((1, nl, 2, H_A, DK_A, DV_A), lambda i: (i, layer, 0, 0, 0, 0))
    in_specs = [blk(256), blk(256), blk(256), blk(256), blk(W_A), _layer_spec(g4.shape, layer)]
    args = [qk, v, la, bsum, gate, g4]
    out_specs = [blk(W_A)]
    out_shape = [jax.ShapeDtypeStruct((n, W_A), BF16)]
    aliases = {}
    if has_s0:
        in_specs.append(st_spec)
        args.append(state_in)
    else:
        out_specs.append(st_spec)
        out_shape.append(jax.ShapeDtypeStruct((nb, DEPTH, 2, H_A, DK_A, DV_A), F32))
        if state_out is not None:
            aliases[len(args)] = 1
            in_specs.append(_ANY)
            args.append(state_out)
    return pl.pallas_call(
        functools.partial(_gla_kernel, seq=seq, has_s0=has_s0, alias_in=len(aliases)),
        grid=(nb,), in_specs=in_specs, out_specs=out_specs, out_shape=out_shape,
        input_output_aliases=aliases,
        scratch_shapes=[pltpu.VMEM((seq, W_A), F32), pltpu.VMEM((2, ST_R, ST_C), F32),
                        pltpu.VMEM((ST_C, ST_R), F32)],
        compiler_params=_cparams(), name="gla_state" if has_s0 else "gla_ctx",
    )(*args)


def _softmax_t_pv(st, vt):
    m = _col_reduce(st, jnp.max)
    e = jnp.exp2(st - m)
    l = _col_reduce(e, jnp.sum)
    return _dot(vt, e.astype(BF16)) * (1.0 / l)


def _pipelined_attention(score_fns, value_fns, depth):
    outs = []
    pending = [fn() for fn in score_fns[:depth]]
    for j, vfn in enumerate(value_fns):
        st = pending.pop(0)
        if j + depth < len(score_fns):
            pending.append(score_fns[j + depth]())
        outs.append(_softmax_t_pv(st, vfn()))
    return outs


def _lookahead(n_keys):
    return 2 if n_keys >= 1024 else 3


def _col_reduce(x, op):
    rows, cols = x.shape
    part = 128 if rows % 128 == 0 and rows > 128 else rows
    if part != rows:
        x = op(x.reshape(rows // part, part, cols), axis=0)
    return op(x, axis=0, keepdims=True)


def _mla_kernel(*refs, seq, ctx_len):
    it = iter(refs)
    q_ref, k_ref, vt_ref, gate_ref = (next(it) for _ in range(4))
    ck_ref, cvt_ref = (next(it), next(it)) if ctx_len else (None, None)
    ob_ref = next(it)
    kk_sc, vt_sc = next(it), next(it)

    for p in range(H_B // 2):
        kk_sc[p, 0:seq, 0:128] = k_ref[:, 128 * p:128 * p + 128]
        kk_sc[p, 0:seq, 128:256] = k_ref[:, 512:640]
        if ctx_len:
            kk_sc[p, seq:seq + ctx_len, 0:128] = ck_ref[0, 0, :, 128 * p:128 * p + 128]
            kk_sc[p, seq:seq + ctx_len, 128:256] = ck_ref[0, 0, :, 512:640]
    vt_sc[:, 0:seq] = vt_ref[...]
    if ctx_len:
        vt_sc[:, seq:seq + ctx_len] = cvt_ref[0, 0]

    qb = min(ATT_QBLOCK, seq)

    def body(i, carry):
        rows = pl.ds(pl.multiple_of(i * qb, qb), qb)
        def scores(h):
            p, hh = divmod(h, 2)
            qn = q_ref[rows, 128 * p:128 * p + 128] * _lane_mask(128, DN_B * hh, DN_B, BF16)
            qp = (q_ref[rows, 512 + 128 * (h // 4):512 + 128 * (h // 4) + 128]
                  * _lane_mask(128, DR_B * (h % 4), DR_B, BF16))
            return _dot_nt(kk_sc[p], jnp.concatenate([qn, qp], axis=-1))

        outs = _pipelined_attention(
            [functools.partial(scores, h) for h in range(H_B)],
            [functools.partial(lambda h: vt_sc[DV_B * h:DV_B * (h + 1), :], h) for h in range(H_B)],
            _lookahead(seq + ctx_len))
        ob = jnp.concatenate(outs, axis=0).T
        ob_ref[rows, :] = (ob * gate_ref[rows, :]).astype(BF16)
        return carry

    lax.fori_loop(0, seq // qb, body, 0)


def _mla_call(mq, mk, mvt, gate, ctx_k, ctx_vt, layer, *, seq):
    n = mq.shape[0]
    ctx_len = 0 if ctx_k is None else ctx_k.shape[2]
    in_specs = [pl.BlockSpec((seq, 768), lambda i: (i, 0)), pl.BlockSpec((seq, 640), lambda i: (i, 0)),
                pl.BlockSpec((W_B, seq), lambda i: (0, i)), pl.BlockSpec((seq, W_B), lambda i: (i, 0))]
    args = [mq, mk, mvt, gate]
    if ctx_len:
        in_specs += [pl.BlockSpec((1, 1, ctx_len, 640), lambda i: (layer, i, 0, 0)),
                     pl.BlockSpec((1, 1, W_B, ctx_len), lambda i: (layer, i, 0, 0))]
        args += [ctx_k, ctx_vt]
    tk = seq + ctx_len
    return pl.pallas_call(
        functools.partial(_mla_kernel, seq=seq, ctx_len=ctx_len),
        grid=(n // seq,), in_specs=in_specs,
        out_specs=pl.BlockSpec((seq, W_B), lambda i: (i, 0)),
        out_shape=jax.ShapeDtypeStruct((n, W_B), BF16),
        scratch_shapes=[pltpu.VMEM((H_B // 2, tk, 256), BF16), pltpu.VMEM((W_B, tk), BF16)],
        compiler_params=_cparams(), name="mla_ctx" if ctx_len else "mla_self",
    )(*args)


def _diff_kernel(*refs, seq, ctx_len, lam_init):
    it = iter(refs)
    qk_ref, vt_ref, gate_ref, lam_ref, g_ref = (next(it) for _ in range(5))
    ck_ref, cv_ref = (next(it), next(it)) if ctx_len else (None, None)
    oc_ref = next(it)
    k_sc, vt_sc = next(it), next(it)

    lam = (jnp.exp(jnp.sum(lam_ref[0, 0:1, :] * lam_ref[0, 1:2, :], axis=-1, keepdims=True))
           - jnp.exp(jnp.sum(lam_ref[0, 2:3, :] * lam_ref[0, 3:4, :], axis=-1, keepdims=True)) + lam_init)
    dh = 2 * DC
    for p in range(H_C // 2):
        k_sc[p, 0:seq, :] = qk_ref[:, 256 + 128 * p:256 + 128 * p + 128]
        if ctx_len:
            pair_t = jnp.concatenate([ck_ref[0, 0, 2 * p], ck_ref[0, 0, 2 * p + 1]], axis=0)
            k_sc[p, seq:seq + ctx_len, :] = pair_t.T.astype(BF16)
    vt_sc[:, 0:seq] = vt_ref[...]
    if ctx_len:
        for h in range(H_C):
            vt_sc[dh * h:dh * (h + 1), seq:seq + ctx_len] = cv_ref[0, 0, h].astype(BF16)

    qb = min(ATT_QBLOCK, seq)

    def body(i, carry):
        rows = pl.ds(pl.multiple_of(i * qb, qb), qb)
        def scores(h, comp):
            p, hh = divmod(h, 2)
            qm = qk_ref[rows, 128 * p:128 * p + 128] * _lane_mask(128, dh * hh + DC * comp, DC, BF16)
            return _dot_nt(k_sc[p], qm)

        items = [(h, comp) for h in range(H_C) for comp in range(2)]
        o12 = _pipelined_attention(
            [functools.partial(scores, h, comp) for h, comp in items],
            [functools.partial(lambda h: vt_sc[dh * h:dh * (h + 1), :], h) for h, _ in items],
            _lookahead(seq + ctx_len))
        outs = []
        for h in range(H_C):
            ot = o12[2 * h] - lam * o12[2 * h + 1]
            outs.append(ot * lax.rsqrt(jnp.mean(ot * ot, axis=0, keepdims=True) + EPS))
        oc = jnp.concatenate(outs, axis=0).T
        oc_ref[rows, :] = (oc * g_ref[0] * (1.0 - lam_init) * gate_ref[rows, :]).astype(BF16)
        return carry

    lax.fori_loop(0, seq // qb, body, 0)


def _diff_call(dqk, dvt, gate, lamp, g4, ck_t, cv_t, layer, *, seq):
    n = dqk.shape[0]
    ctx_len = 0 if ck_t is None else ck_t.shape[4]
    lam_init = 0.8 - 0.6 * math.exp(-0.3 * layer)
    in_specs = [pl.BlockSpec((seq, 512), lambda i: (i, 0)), pl.BlockSpec((W_C, seq), lambda i: (0, i)),
                pl.BlockSpec((seq, W_C), lambda i: (i, 0)),
                _layer_spec(lamp.shape, layer), _layer_spec(g4.shape, layer)]
    args = [dqk, dvt, gate, lamp, g4]
    if ctx_len:
        in_specs += [pl.BlockSpec((1, 1, H_C, 2 * DC, ctx_len), lambda i: (i, layer, 0, 0, 0))] * 2
        args += [ck_t, cv_t]
    tk = seq + ctx_len
    return pl.pallas_call(
        functools.partial(_diff_kernel, seq=seq, ctx_len=ctx_len, lam_init=lam_init),
        grid=(n // seq,), in_specs=in_specs,
        out_specs=pl.BlockSpec((seq, W_C), lambda i: (i, 0)),
        out_shape=jax.ShapeDtypeStruct((n, W_C), BF16),
        scratch_shapes=[pltpu.VMEM((H_C // 2, tk, 128), BF16), pltpu.VMEM((W_C, tk), BF16)],
        compiler_params=_cparams(), name="diff_ctx" if ctx_len else "diff_self",
    )(*args)


def _post_kernel(oa_ref, ob_ref, oc_ref, x_ref, mod_ref, w_ref, g_ref, y_ref):
    mix = jnp.concatenate([oa_ref[...], ob_ref[...], oc_ref[...]], axis=-1)
    out = _rms(_dot(mix, w_ref[0]), g_ref[0])
    y_ref[...] = x_ref[...] + mod_ref[0, :, 2 * D_MODEL:3 * D_MODEL] * out


def _post_call(oa, ob, oc, x2d, mod, pw, layer, *, seq, sample):
    n = x2d.shape[0]
    tm = min(POST_BLOCK, seq if sample else n)
    blk = lambda w: pl.BlockSpec((tm, w), lambda i: (i, 0))

    def mod_idx(i):
        return (layer * MOD_ROWS + ((i * tm) // seq + 1 if sample else 0), 0, 0)

    return pl.pallas_call(
        _post_kernel,
        grid=(n // tm,),
        in_specs=[blk(W_A), blk(W_B), blk(W_C), blk(D_MODEL),
                  pl.BlockSpec((1, 1, 3 * D_MODEL), mod_idx),
                  _layer_spec(pw["w_out"].shape, layer), _layer_spec(pw["g_post"].shape, layer)],
        out_specs=blk(D_MODEL),
        out_shape=jax.ShapeDtypeStruct((n, D_MODEL), F32),
        compiler_params=_cparams(), name="post",
    )(oa, ob, oc, x2d, mod, pw["w_out"], pw["g_post"])


def _pack_params(g_pre, g_post, w_in, w_gla_af, b_gla_af, w_gla_ab, b_gla_ab, g_gla, g_mla_q, w_mla_uq,
                 g_mla_kv, w_mla_ukv, lam_q1, lam_k1, lam_q2, lam_k2, g_diff, w_out):
    w_t = jnp.swapaxes(w_in, 1, 2)
    kpe_t = w_t[:, R_MLA[1] - DR_B:R_MLA[1]]
    w_x = jnp.concatenate([kpe_t] * 4, axis=1).astype(BF16)
    zg = jnp.zeros((DEPTH, GLA_LR, 128), F32)
    w_gate = jnp.concatenate([jnp.concatenate([w_gla_af, zg], axis=-1),
                              jnp.concatenate([zg, w_gla_ab], axis=-1)], axis=1).astype(BF16)
    uq = w_mla_uq.reshape(DEPTH, Q_LORA, H_B, DN_B + DR_B)
    w_pe = uq[..., DN_B:].reshape(DEPTH, Q_LORA, H_B * DR_B)
    w_uq = jnp.concatenate([uq[..., :DN_B].reshape(DEPTH, Q_LORA, H_B * DN_B), w_pe], axis=-1).astype(BF16)
    ukv = w_mla_ukv.reshape(DEPTH, KV_LORA, H_B, DN_B + DV_B)
    w_ukv = jnp.concatenate([ukv[..., :DN_B].reshape(DEPTH, KV_LORA, H_B * DN_B),
                             ukv[..., DN_B:].reshape(DEPTH, KV_LORA, H_B * DV_B)], axis=-1).astype(BF16)
    row = lambda a: a.reshape(DEPTH, 1, a.shape[-1])
    return dict(
        w_t=w_t.astype(BF16), w_x=w_x, w_gate=w_gate,
        b_gate=row(jnp.concatenate([b_gla_af, b_gla_ab], axis=-1)),
        w_uq=w_uq, w_ukv=w_ukv, w_out=w_out.astype(BF16),
        g_pre=row(g_pre), g_post=row(g_post), g_mla_q=row(g_mla_q), g_mla_kv=row(g_mla_kv),
        g_gla4=row(jnp.tile(g_gla, (1, H_A))), g_diff4=row(jnp.tile(g_diff, (1, H_C))),
        lam=jnp.stack([lam_q1, lam_k1, lam_q2, lam_k2], axis=1))


def _rope_tables(n):
    t = np.arange(n)
    row = (t // GRID_W).astype(np.float32)
    col = (t % GRID_W).astype(np.float32)
    half = ROPE_DIM // 2
    inv = (1.0 / (np.float32(ROPE_THETA) ** (np.arange(0, half, 2, dtype=np.float32) / np.float32(half)))
           ).astype(np.float32)
    ar = row[:, None] * inv
    ac = col[:, None] * inv
    ang = np.concatenate([ar, ar, ac, ac], axis=-1).astype(np.float32)
    return (jnp.asarray(np.tile(np.cos(ang), (1, 8)).astype(np.float32)),
            jnp.asarray(np.tile(np.sin(ang), (1, 8)).astype(np.float32)))


def _sublayer(x2d, mod, pw, layer, *, seq, rope_tabs, ctx, caches):
    sample = ctx is not None
    pre = _pre_call(x2d, mod, pw, layer, seq=seq, rope_tabs=rope_tabs,
                    caches=None if sample else caches[:4])
    qk, v, la, bsum, ga, gb, gc, mq, mk, mvt, dqk, dvt = pre[:12]
    if sample:
        (oa,) = _gla_call(qk, v, la, bsum, ga, pw["g_gla4"], layer, seq=seq, state_in=ctx["state"])
        ob = _mla_call(mq, mk, mvt, gb, ctx["mla_k"], ctx["mla_vt"], layer, seq=seq)
        oc = _diff_call(dqk, dvt, gc, pw["lam"], pw["g_diff4"], ctx["diff_k_t"], ctx["diff_v_t"], layer,
                        seq=seq)
        new_caches = None
    else:
        oa, sfin = _gla_call(qk, v, la, bsum, ga, pw["g_gla4"], layer, seq=seq,
                             state_out=caches[4] if caches else None)
        ob = _mla_call(mq, mk, mvt, gb, None, None, layer, seq=seq)
        oc = _diff_call(dqk, dvt, gc, pw["lam"], pw["g_diff4"], None, None, layer, seq=seq)
        new_caches = tuple(pre[12:]) + (sfin,)
    y = _post_call(oa, ob, oc, x2d, mod, pw, layer, seq=seq, sample=sample)
    return y, new_caches


def kernel(x_prompt, x_sample, c, cache_mla_ckv, cache_mla_kpe, cache_diff_k, cache_diff_v, state_gla,
           c_ctx, w_ada, b_ada, g_pre, g_post, w_in, w_gla_af, b_gla_af, w_gla_ab, b_gla_ab, g_gla,
           g_mla_q, w_mla_uq, g_mla_kv, w_mla_ukv, lam_q1, lam_k1, lam_q2, lam_k2, g_diff, w_out):
    bp, tp, d = x_prompt.shape
    bs, ts, _ = x_sample.shape

    pw = _pack_params(g_pre, g_post, w_in, w_gla_af, b_gla_af, w_gla_ab, b_gla_ab, g_gla, g_mla_q,
                      w_mla_uq, g_mla_kv, w_mla_ukv, lam_q1, lam_k1, lam_q2, lam_k2, g_diff, w_out)
    cvecs = jnp.concatenate([c_ctx[None], c, jnp.zeros((MOD_ROWS - 1 - bs, d), F32)], axis=0)
    mod = _mod_call(cvecs, w_ada, b_ada).reshape(DEPTH * MOD_ROWS, 1, 3 * d)
    rope_tabs = _rope_tables(ts)
    ctx_k, ctx_vt = _ctxkv_call(cache_mla_ckv, jnp.swapaxes(cache_mla_kpe, -1, -2), pw["w_ukv"])
    ctx = dict(state=state_gla, mla_k=ctx_k, mla_vt=ctx_vt,
               diff_k_t=jnp.swapaxes(cache_diff_k, -1, -2), diff_v_t=jnp.swapaxes(cache_diff_v, -1, -2))

    y_p = x_prompt.reshape(bp * tp, d)
    y_s = x_sample.reshape(bs * ts, d)
    caches = ()
    for l in range(DEPTH):
        y_p, caches = _sublayer(y_p, mod, pw, l, seq=tp, rope_tabs=None, ctx=None, caches=caches)
        y_s, _ = _sublayer(y_s, mod, pw, l, seq=ts, rope_tabs=rope_tabs, ctx=ctx, caches=None)
    ckvn, kpe_t, kc_t, vc_t, new_state = caches
    return (y_p.reshape(bp, tp, d), y_s.reshape(bs, ts, d), ckvn, jnp.swapaxes(kpe_t, -1, -2),
            jnp.swapaxes(kc_t, -1, -2), jnp.swapaxes(vc_t, -1, -2), new_state)
```

```python
import functools
import math

import numpy as np
import jax
import jax.numpy as jnp
from jax import lax
from jax.experimental import pallas as pl
from jax.experimental.pallas import tpu as pltpu

F32 = jnp.float32
BF16 = jnp.bfloat16

D_MODEL = 1024
DEPTH = 2
GRID_W = 64
EPS = 1e-6
ROPE_THETA = 10000.0
ROPE_DIM = 32
H_A, DK_A, DV_A = 4, 32, 64
GLA_LR = 16
GLA_TAU = 16.0
GLA_CHUNK = 64
H_B, DN_B, DR_B, DV_B = 8, 64, 32, 64
Q_LORA, KV_LORA = 256, 128
H_C, DC = 4, 32
W_A, W_B, W_C = H_A * DV_A, H_B * DV_B, H_C * 2 * DC
ST_R, ST_C = H_A * DV_A, H_A * DK_A
LOG2E = math.log2(math.e)
SUBLANES = 8

R_GLA = (0, 544)
R_GG = (544, 800)
R_MLA = (800, 1216)
R_MG = (1216, 1728)
R_DIFF = (1728, 2752)
N_IN = 2752
MOD_ROWS = 8

V7X_VMEM_LIMIT_BYTES = 56 * 1024 * 1024
TOKEN_BLOCK = 1024
POST_BLOCK = 1024
ATT_QBLOCK = 512
ATT_GROUP_ROWS = 1024
GLA_GROUP_ROWS = 1024


def _cparams(n_axes=1):
    return pltpu.CompilerParams(dimension_semantics=("arbitrary",) * n_axes,
                                vmem_limit_bytes=V7X_VMEM_LIMIT_BYTES)


def _rms(x, g):
    return x * lax.rsqrt(jnp.mean(x * x, axis=-1, keepdims=True) + EPS) * g


def _silu(x):
    return x * jax.nn.sigmoid(x)


def _log_sigmoid(x):
    return jnp.minimum(x, 0.0) - jnp.log1p(jnp.exp(-jnp.abs(x)))


def _rope(z, cos, sin):
    w = z.shape[-1]
    lane = lax.broadcasted_iota(jnp.int32, z.shape, 1)
    rot = jnp.where((lane & 15) < 8, -pltpu.roll(z, w - 8, 1), pltpu.roll(z, 8, 1))
    return z * cos + rot * sin


def _lane_mask(width, lo, size, dtype):
    lane = lax.broadcasted_iota(jnp.int32, (1, width), 1)
    return jnp.where((lane >= lo) & (lane < lo + size), 1.0, 0.0).astype(dtype)


def _dot(a, b):
    return jnp.dot(a, b, preferred_element_type=F32)


def _dot_nt(a, b):
    return lax.dot_general(a, b, (((1,), (1,)), ((), ())), preferred_element_type=F32)


def _dot_tn(a, b):
    return lax.dot_general(a, b, (((0,), (0,)), ((), ())), preferred_element_type=F32)


def _layer_spec(shape, layer):
    nd = len(shape)
    return pl.BlockSpec((1,) + tuple(shape[1:]), lambda *_: (layer,) + (0,) * (nd - 1))


_ANY = pl.BlockSpec(memory_space=pl.ANY)


def _mod_kernel(c_ref, w_ref, b_ref, o_ref):
    s = _silu(c_ref[...]).astype(BF16)
    o_ref[0] = _dot(s, w_ref[0].astype(BF16)) + b_ref[0]


def _mod_call(cvecs, w_ada, b_ada):
    nb = 1024
    return pl.pallas_call(
        _mod_kernel,
        grid=(DEPTH, 3 * D_MODEL // nb),
        in_specs=[pl.BlockSpec((MOD_ROWS, D_MODEL), lambda l, j: (0, 0)),
                  pl.BlockSpec((1, D_MODEL, nb), lambda l, j: (l, 0, j)),
                  pl.BlockSpec((1, 1, nb), lambda l, j: (l, 0, j))],
        out_specs=pl.BlockSpec((1, MOD_ROWS, nb), lambda l, j: (l, 0, j)),
        out_shape=jax.ShapeDtypeStruct((DEPTH, MOD_ROWS, 3 * D_MODEL), F32),
        compiler_params=_cparams(2), name="adaln_mod",
    )(cvecs, w_ada, b_ada.reshape(DEPTH, 1, 3 * D_MODEL))


def _ctxkv_kernel(ckv_ref, kpe_ref, w_ref, k_ref, vt_ref):
    kv = _dot(ckv_ref[0, 0].astype(BF16), w_ref[0])
    kpe4 = jnp.concatenate([kpe_ref[0, 0]] * 4, axis=0).T
    k_ref[0, 0, :, 0:512] = kv[:, 0:512].astype(BF16)
    k_ref[0, 0, :, 512:640] = kpe4.astype(BF16)
    vt_ref[0, 0] = kv[:, 512:1024].T.astype(BF16)


def _ctxkv_call(cache_ckv, cache_kpe_t, wukv):
    nb, _, tc, _ = cache_ckv.shape
    return pl.pallas_call(
        _ctxkv_kernel,
        grid=(DEPTH, nb),
        in_specs=[pl.BlockSpec((1, 1, tc, KV_LORA), lambda l, b: (b, l, 0, 0)),
                  pl.BlockSpec((1, 1, DR_B, tc), lambda l, b: (b, l, 0, 0)),
                  pl.BlockSpec((1, KV_LORA, 1024), lambda l, b: (l, 0, 0))],
        out_specs=[pl.BlockSpec((1, 1, tc, 640), lambda l, b: (l, b, 0, 0)),
                   pl.BlockSpec((1, 1, W_B, tc), lambda l, b: (l, b, 0, 0))],
        out_shape=[jax.ShapeDtypeStruct((DEPTH, nb, tc, 640), BF16),
                   jax.ShapeDtypeStruct((DEPTH, nb, W_B, tc), BF16)],
        compiler_params=_cparams(2), name="mla_ctx_kv",
    )(cache_ckv, cache_kpe_t, wukv)


def _pre_kernel(*refs, rope, ctx_out, alias_in, bpb, seq):
    it = iter(refs)
    (x_ref, mod_ref, gpre_ref, w_ref, wx_ref, wg_ref, bg_ref, gq_ref, gkv_ref, wuq_ref,
     wukv_ref) = (next(it) for _ in range(11))
    if rope:
        cos_ref, sin_ref = next(it), next(it)
    for _ in range(alias_in):
        next(it)
    (qk_ref, v_ref, la_ref, bs_ref, ga_ref, gb_ref, gc_ref, mq_ref, mk_ref, mvt_ref, dqk_ref,
     dvt_ref) = (next(it) for _ in range(12))
    if ctx_out:
        ckvn_ref, kpe_ref, kc_ref, vc_ref = (next(it) for _ in range(4))

    d = D_MODEL
    shift = mod_ref[0, :, 0:d]
    scale = mod_ref[0, :, d:2 * d]
    h = (_rms(x_ref[...], gpre_ref[0]) * (1.0 + scale) + shift).astype(BF16)
    proj = lambda r: _dot_nt(h, w_ref[0, r[0]:r[1], :])
    if rope:
        cos = cos_ref[...]
        sin = sin_ref[...]

    pg = proj(R_GLA)
    qk_ref[:, 0:128] = pg[:, 0:128] * (DK_A ** -0.5)
    qk_ref[:, 128:256] = pg[:, 128:256]
    v_ref[...] = pg[:, 256:512]
    xg = _dot(pg[:, 512:512 + 2 * GLA_LR].astype(BF16), wg_ref[0]) + bg_ref[0]
    la = _log_sigmoid(xg) * (1.0 / GLA_TAU)
    la_ref[...] = la
    bs_ref[:, 0:128] = _chunk_scan(la[:, 0:128], False)
    bs_ref[:, 128:256] = _chunk_scan(la[:, 128:256], True)
    ga_ref[...] = _silu(proj(R_GG)).astype(BF16)

    pm = proj(R_MLA)
    qall = _dot(_rms(pm[:, 0:256], gq_ref[0]).astype(BF16), wuq_ref[0])
    q_pe = qall[:, 512:768]
    if rope:
        q_pe = _rope(q_pe, cos, sin)
    sb = (DN_B + DR_B) ** -0.5 * LOG2E
    mq_ref[:, 0:512] = (qall[:, 0:512] * sb).astype(BF16)
    mq_ref[:, 512:768] = (q_pe * sb).astype(BF16)
    ckvn = _rms(pm[:, 256:384], gkv_ref[0])
    kvall = _dot(ckvn.astype(BF16), wukv_ref[0])
    kpe4 = _dot_nt(h, wx_ref[0])
    if rope:
        kpe4 = _rope(kpe4, cos[:, 0:128], sin[:, 0:128])
    mk_ref[:, 0:512] = kvall[:, 0:512].astype(BF16)
    mk_ref[:, 512:640] = kpe4.astype(BF16)
    mvt_ref[...] = kvall[:, 512:1024].T.astype(BF16)
    gb_ref[...] = _silu(proj(R_MG)).astype(BF16)

    pd = proj(R_DIFF)
    dq, dk, dv = pd[:, 0:256], pd[:, 256:512], pd[:, 512:768]
    if rope:
        dq = _rope(dq, cos, sin)
        dk = _rope(dk, cos, sin)
    dqk_ref[:, 0:256] = (dq * (DC ** -0.5 * LOG2E)).astype(BF16)
    dqk_ref[:, 256:512] = dk.astype(BF16)
    dv_t = dv.T
    dvt_ref[...] = dv_t.astype(BF16)
    gc_ref[...] = _silu(pd[:, 768:1024]).astype(BF16)
    if ctx_out:
        kpe_t = kpe4.T
        dk_t = dk.T
        for ref in () if alias_in else (ckvn_ref, kpe_ref, kc_ref, vc_ref):
            ref[:, 1:] = jnp.zeros(ref[:, 1:].shape, F32)
        for bb in range(bpb):
            rs = slice(bb * seq, (bb + 1) * seq)
            ckvn_ref[bb, 0] = ckvn[rs]
            kpe_ref[bb, 0] = kpe_t[0:DR_B, rs]
            kc_ref[bb, 0] = dk_t[:, rs].reshape(H_C, 2 * DC, seq)
            vc_ref[bb, 0] = dv_t[:, rs].reshape(H_C, 2 * DC, seq)


def _pre_call(x2d, mod, pw, layer, *, seq, rope_tabs, caches):
    n = x2d.shape[0]
    tm = min(TOKEN_BLOCK, n)
    bpb = max(tm // seq, 1)
    rope = rope_tabs is not None
    ctx_out = caches is not None
    steps_per_seq = max(seq // tm, 1)
    nbt = n // seq

    def mod_idx(i):
        return (layer * MOD_ROWS + ((i * tm) // seq + 1 if rope else 0), 0, 0)

    names = ["g_pre", "w_t", "w_x", "w_gate", "b_gate", "g_mla_q", "g_mla_kv", "w_uq", "w_ukv"]
    in_specs = [pl.BlockSpec((tm, D_MODEL), lambda i: (i, 0)), pl.BlockSpec((1, 1, 3 * D_MODEL), mod_idx)]
    in_specs += [_layer_spec(pw[k].shape, layer) for k in names]
    args = [x2d, mod] + [pw[k] for k in names]
    if rope:
        in_specs += [pl.BlockSpec((tm, 256), lambda i: (i % steps_per_seq, 0))] * 2
        args += list(rope_tabs)
    outs = [(256, F32, False), (256, F32, False), (256, F32, False), (256, F32, False),
            (W_A, BF16, False), (W_B, BF16, False), (W_C, BF16, False),
            (768, BF16, False), (640, BF16, False), (W_B, BF16, True),
            (512, BF16, False), (W_C, BF16, True)]
    out_specs = [pl.BlockSpec((w, tm), lambda i: (0, i)) if tr else pl.BlockSpec((tm, w), lambda i: (i, 0))
                 for w, _, tr in outs]
    out_shape = [jax.ShapeDtypeStruct((w, n) if tr else (n, w), dt) for w, dt, tr in outs]
    widths = outs
    aliases = {}
    if ctx_out:
        assert caches or layer == 0
        nl = 1 if caches else DEPTH
        out_specs += [pl.BlockSpec((bpb, nl, seq, KV_LORA), lambda i: (i, layer, 0, 0)),
                      pl.BlockSpec((bpb, nl, DR_B, seq), lambda i: (i, layer, 0, 0)),
                      pl.BlockSpec((bpb, nl, H_C, 2 * DC, seq), lambda i: (i, layer, 0, 0, 0)),
                      pl.BlockSpec((bpb, nl, H_C, 2 * DC, seq), lambda i: (i, layer, 0, 0, 0))]
        out_shape += [jax.ShapeDtypeStruct((nbt, DEPTH, seq, KV_LORA), F32),
                      jax.ShapeDtypeStruct((nbt, DEPTH, DR_B, seq), F32),
                      jax.ShapeDtypeStruct((nbt, DEPTH, H_C, 2 * DC, seq), F32),
                      jax.ShapeDtypeStruct((nbt, DEPTH, H_C, 2 * DC, seq), F32)]
        for j, arr in enumerate(caches):
            aliases[len(args)] = len(widths) + j
            in_specs.append(_ANY)
            args.append(arr)
    return pl.pallas_call(
        functools.partial(_pre_kernel, rope=rope, ctx_out=ctx_out, alias_in=len(aliases), bpb=bpb, seq=seq),
        grid=(n // tm,), in_specs=in_specs, out_specs=out_specs, out_shape=out_shape,
        input_output_aliases=aliases,
        compiler_params=_cparams(), name="pre_rope" if rope else "pre_ctx",
    )(*args)


_GLA_LEVELS = (1, 2, 4, 8, 16, 32)


def _gla_consts(rev):
    c = GLA_CHUNK
    row = lax.broadcasted_iota(jnp.int32, (c, 128), 0)
    pos = (c - 1 - row) if rev else row
    ri = lax.broadcasted_iota(jnp.int32, (c, H_A * c), 0)
    cj = lax.broadcasted_iota(jnp.int32, (c, H_A * c), 1) & (c - 1)
    pi = (c - 1 - ri) if rev else ri
    pj = (c - 1 - cj) if rev else cj
    x = pi ^ pj
    lvl = jnp.where(pi == pj, 0, -1)
    for kbit in range(6):
        lvl = jnp.where((pj < pi) & ((x >> kbit) == 1), kbit + 1, lvl)
    return pos, lvl


def _chunk_scan(x, rev):
    rows = x.shape[0]
    nt = rows // SUBLANES
    tiles_per_chunk = GLA_CHUNK // SUBLANES
    x3 = x.reshape(nt, SUBLANES, 128)
    sub = lax.broadcasted_iota(jnp.int32, x3.shape, 1)
    tile = lax.broadcasted_iota(jnp.int32, x3.shape, 0) & (tiles_per_chunk - 1)
    edge = 0 if rev else SUBLANES - 1
    s = 1
    while s < SUBLANES:
        if rev:
            x3 = x3 + jnp.where(sub < SUBLANES - s, pltpu.roll(x3, SUBLANES - s, 1), 0.0)
        else:
            x3 = x3 + jnp.where(sub >= s, pltpu.roll(x3, s, 1), 0.0)
        s *= 2
    s = 1
    while s < tiles_per_chunk:
        tot = jnp.broadcast_to(x3[:, edge:edge + 1, :], x3.shape)
        if rev:
            shifted = jnp.concatenate([tot[s:], tot[:s]], axis=0)
            x3 = x3 + jnp.where(tile < tiles_per_chunk - s, shifted, 0.0)
        else:
            shifted = jnp.concatenate([tot[nt - s:], tot[:nt - s]], axis=0)
            x3 = x3 + jnp.where(tile >= s, shifted, 0.0)
        s *= 2
    return x3.reshape(rows, 128)


def _gla_scores(q, k, la, b, pos, lvl, hm_bf, rev):
    c = GLA_CHUNK
    prv = pltpu.roll(la, c - 1 if rev else 1, 0)
    nxt = pltpu.roll(la, 1 if rev else c - 1, 0)
    s_tot = jnp.where(lvl == 0, _dot_nt(q.astype(BF16), jnp.concatenate([k.astype(BF16)] * H_A, 0) * hm_bf), 0.0)
    for kbit, m in enumerate(_GLA_LEVELS):
        up = ((pos >> kbit) & 1) == 1
        if m == 1:
            e = jnp.where(up, la, 0.0)
        elif m == 2:
            c4 = pos & 3
            e = jnp.where(c4 == 0, nxt, jnp.where(c4 == 1, 0.0, jnp.where(c4 == 2, la, la + prv)))
        else:
            nblk = c // (2 * m)
            loc = m if rev else m - 1
            b3 = b.reshape(nblk, 2 * m, 128)
            ref = jnp.broadcast_to(b3[:, loc:loc + 1, :], (nblk, 2 * m, 128)).reshape(c, 128)
            dlt = b - ref
            e = jnp.where(up, dlt, -dlt)
        xm = (jnp.where(up, q, k) * jnp.exp(e)).astype(BF16)
        sm = _dot_nt(xm, jnp.concatenate([xm] * H_A, 0) * hm_bf)
        s_tot = jnp.where(lvl == kbit + 1, sm, s_tot)
    return s_tot


def _gla_apply(s_tot, q, k, v, b, st_prev, hm_f32, vm_bf, rev):
    c = GLA_CHUNK
    vbd = jnp.concatenate([v] * H_A, 0) * vm_bf
    blast = b[0:1, :] if rev else b[c - 1:c, :]
    qbar = (q * jnp.exp(b)).astype(BF16)
    kdec = (k * jnp.exp(blast - b)).astype(BF16)
    o = _dot(s_tot.astype(BF16), vbd) + _dot_nt(qbar, st_prev.astype(BF16))
    st_new = st_prev * jnp.exp(blast) + _dot_tn(v, kdec) * hm_f32
    return o, st_new


def _gla_kernel(*refs, seq, has_s0, alias_in, spb):
    it = iter(refs)
    qk_ref, v_ref, la_ref, b_ref, gate_ref, g_ref = (next(it) for _ in range(6))
    s0_ref = next(it) if has_s0 else None
    for _ in range(alias_in):
        next(it)
    oa_ref = next(it)
    sfin_ref = None if has_s0 else next(it)
    acc_sc, st_sc, blk_sc = next(it), next(it), next(it)

    c = GLA_CHUNK
    nc = seq // c
    acc_sc[...] = jnp.zeros_like(acc_sc)
    if not has_s0 and not alias_in:
        sfin_ref[:, 1:] = jnp.zeros(sfin_ref[:, 1:].shape, F32)

    hrow = lax.broadcasted_iota(jnp.int32, (H_A * c, 128), 0) // c
    hm_f32 = jnp.where(hrow == lax.broadcasted_iota(jnp.int32, (H_A * c, 128), 1) // DK_A, 1.0, 0.0)
    hm_bf = hm_f32.astype(BF16)
    vrow = lax.broadcasted_iota(jnp.int32, (H_A * c, H_A * DV_A), 0) // c
    vm_bf = jnp.where(vrow == lax.broadcasted_iota(jnp.int32, (H_A * c, H_A * DV_A), 1) // DV_A,
                      1.0, 0.0).astype(BF16)
    consts = (_gla_consts(False), _gla_consts(True))

    def run_sequence(bb):
        base = bb * seq
        for d in range(2):
            if has_s0:
                blk_sc[...] = jnp.zeros_like(blk_sc)
                for hh in range(H_A):
                    blk_sc[DK_A * hh:DK_A * (hh + 1), DV_A * hh:DV_A * (hh + 1)] = s0_ref[bb, 0, d, hh]
                st_sc[d] = blk_sc[...].T
            else:
                st_sc[d] = jnp.zeros((ST_R, ST_C), F32)

        def chunk_rows(n, d):
            cn = (nc - 1 - n) if d else n
            start = base + cn * c
            return pl.ds(start if isinstance(start, int) else pl.multiple_of(start, c), c)

        def scores(n):
            out = []
            for d in range(2):
                rows = chunk_rows(n, d)
                pos, lvl = consts[d]
                out.append(_gla_scores(qk_ref[rows, 0:128], qk_ref[rows, 128:256],
                                       la_ref[rows, 128 * d:128 * d + 128],
                                       b_ref[rows, 128 * d:128 * d + 128], pos, lvl, hm_bf, bool(d)))
            return tuple(out)

        def apply(n, s_both):
            for d in range(2):
                rows = chunk_rows(n, d)
                o, st_new = _gla_apply(s_both[d], qk_ref[rows, 0:128], qk_ref[rows, 128:256],
                                       v_ref[rows, :].astype(BF16), b_ref[rows, 128 * d:128 * d + 128],
                                       st_sc[d], hm_f32, vm_bf, bool(d))
                acc_sc[rows, :] = acc_sc[rows, :] + o
                st_sc[d] = st_new

        def body(n, s_cur):
            s_next = scores(n + 1)
            apply(n, s_cur)
            return s_next

        s_last = lax.fori_loop(0, nc - 1, body, scores(0), unroll=3 if (nc - 1) % 3 == 0 else 1)
        apply(nc - 1, s_last)

        rs = slice(base, base + seq)
        first = lax.broadcasted_iota(jnp.int32, (seq, 128), 1) < DV_A
        for p in range(H_A // 2):
            cols = slice(128 * p, 128 * (p + 1))
            oa = acc_sc[rs, cols]
            sq = oa * oa
            s0 = jnp.sum(jnp.where(first, sq, 0.0), axis=-1, keepdims=True)
            s1 = jnp.sum(jnp.where(first, 0.0, sq), axis=-1, keepdims=True)
            inv = jnp.where(first, lax.rsqrt(s0 * (1.0 / DV_A) + EPS), lax.rsqrt(s1 * (1.0 / DV_A) + EPS))
            oa_ref[rs, cols] = (oa * inv * g_ref[0, :, cols] * gate_ref[rs, cols]).astype(BF16)
        if not has_s0:
            for d in range(2):
                blk_sc[...] = st_sc[d].T
                for hh in range(H_A):
                    sfin_ref[bb, 0, d, hh] = blk_sc[DK_A * hh:DK_A * (hh + 1), DV_A * hh:DV_A * (hh + 1)]

    for bb in range(spb):
        run_sequence(bb)


def _gla_call(qk, v, la, bsum, gate, g4, layer, *, seq, state_in=None, state_out=None):
    n = qk.shape[0]
    nb = n // seq
    has_s0 = state_in is not None
    assert has_s0 or state_out is not None or layer == 0
    spb = max(1, min(nb, GLA_GROUP_ROWS // seq))
    blk = lambda w: pl.BlockSpec((spb * seq, w), lambda i: (i, 0))
    nl = DEPTH if (not has_s0 and state_out is None) else 1
    st_spec = pl.BlockSpec((spb, nl, 2, H_A, DK_A, DV_A), lambda i: (i, layer, 0, 0, 0, 0))
    in_specs = [blk(256), blk(256), blk(256), blk(256), blk(W_A), _layer_spec(g4.shape, layer)]
    args = [qk, v, la, bsum, gate, g4]
    out_specs = [blk(W_A)]
    out_shape = [jax.ShapeDtypeStruct((n, W_A), BF16)]
    aliases = {}
    if has_s0:
        in_specs.append(st_spec)
        args.append(state_in)
    else:
        out_specs.append(st_spec)
        out_shape.append(jax.ShapeDtypeStruct((nb, DEPTH, 2, H_A, DK_A, DV_A), F32))
        if state_out is not None:
            aliases[len(args)] = 1
            in_specs.append(_ANY)
            args.append(state_out)
    return pl.pallas_call(
        functools.partial(_gla_kernel, seq=seq, has_s0=has_s0, alias_in=len(aliases), spb=spb),
        grid=(nb // spb,), in_specs=in_specs, out_specs=out_specs, out_shape=out_shape,
        input_output_aliases=aliases,
        scratch_shapes=[pltpu.VMEM((spb * seq, W_A), F32), pltpu.VMEM((2, ST_R, ST_C), F32),
                        pltpu.VMEM((ST_C, ST_R), F32)],
        compiler_params=_cparams(), name="gla_state" if has_s0 else "gla_ctx",
    )(*args)


def _softmax_t_pv(st, vt):
    m = _col_reduce(st, jnp.max)
    e = jnp.exp2(st - m)
    l = _col_reduce(e, jnp.sum)
    return _dot(vt, e.astype(BF16)) * (1.0 / l)


def _pipelined_attention(score_fns, value_fns, depth):
    outs = []
    pending = [fn() for fn in score_fns[:depth]]
    for j, vfn in enumerate(value_fns):
        st = pending.pop(0)
        if j + depth < len(score_fns):
            pending.append(score_fns[j + depth]())
        outs.append(_softmax_t_pv(st, vfn()))
    return outs


def _lookahead(n_keys):
    return 2 if n_keys >= 1024 else 3


def _col_reduce(x, op):
    rows, cols = x.shape
    part = 128 if rows % 128 == 0 and rows > 128 else rows
    if part != rows:
        x = op(x.reshape(rows // part, part, cols), axis=0)
    return op(x, axis=0, keepdims=True)


def _seqs_per_step(n, seq, ctx_len):
    if ctx_len or seq >= ATT_QBLOCK:
        return 1
    return max(1, min(n // seq, ATT_GROUP_ROWS // seq))


def _mla_kernel(*refs, seq, ctx_len, spb):
    it = iter(refs)
    q_ref, k_ref, vt_ref, gate_ref = (next(it) for _ in range(4))
    ck_ref, cvt_ref = (next(it), next(it)) if ctx_len else (None, None)
    ob_ref = next(it)
    kk_sc, vt_sc = next(it), next(it)

    for bb in range(spb):
        rs = slice(bb * seq, (bb + 1) * seq)
        for p in range(H_B // 2):
            kk_sc[bb, p, 0:seq, 0:128] = k_ref[rs, 128 * p:128 * p + 128]
            kk_sc[bb, p, 0:seq, 128:256] = k_ref[rs, 512:640]
            if ctx_len:
                kk_sc[bb, p, seq:seq + ctx_len, 0:128] = ck_ref[0, 0, :, 128 * p:128 * p + 128]
                kk_sc[bb, p, seq:seq + ctx_len, 128:256] = ck_ref[0, 0, :, 512:640]
        vt_sc[bb, :, 0:seq] = vt_ref[:, rs]
        if ctx_len:
            vt_sc[bb, :, seq:seq + ctx_len] = cvt_ref[0, 0]

    def scores(bb, rows, h):
        p, hh = divmod(h, 2)
        qn = q_ref[rows, 128 * p:128 * p + 128] * _lane_mask(128, DN_B * hh, DN_B, BF16)
        qp = (q_ref[rows, 512 + 128 * (h // 4):512 + 128 * (h // 4) + 128]
              * _lane_mask(128, DR_B * (h % 4), DR_B, BF16))
        return _dot_nt(kk_sc[bb, p], jnp.concatenate([qn, qp], axis=-1))

    def run(blocks):
        items = [(bb, rows, h) for bb, rows in blocks for h in range(H_B)]
        outs = _pipelined_attention(
            [functools.partial(scores, bb, rows, h) for bb, rows, h in items],
            [functools.partial(lambda bb, h: vt_sc[bb, DV_B * h:DV_B * (h + 1), :], bb, h) for bb, _, h in items],
            _lookahead(seq + ctx_len))
        for j, (_, rows) in enumerate(blocks):
            ob = jnp.concatenate(outs[H_B * j:H_B * (j + 1)], axis=0).T
            ob_ref[rows, :] = (ob * gate_ref[rows, :]).astype(BF16)

    qb = min(ATT_QBLOCK, seq)
    if seq == qb:
        run([(bb, slice(bb * seq, (bb + 1) * seq)) for bb in range(spb)])
    else:
        def body(i, carry):
            run([(0, pl.ds(pl.multiple_of(i * qb, qb), qb))])
            return carry
        lax.fori_loop(0, seq // qb, body, 0)


def _mla_call(mq, mk, mvt, gate, ctx_k, ctx_vt, layer, *, seq):
    n = mq.shape[0]
    ctx_len = 0 if ctx_k is None else ctx_k.shape[2]
    spb = _seqs_per_step(n, seq, ctx_len)
    rows = spb * seq
    in_specs = [pl.BlockSpec((rows, 768), lambda i: (i, 0)), pl.BlockSpec((rows, 640), lambda i: (i, 0)),
                pl.BlockSpec((W_B, rows), lambda i: (0, i)), pl.BlockSpec((rows, W_B), lambda i: (i, 0))]
    args = [mq, mk, mvt, gate]
    if ctx_len:
        in_specs += [pl.BlockSpec((1, 1, ctx_len, 640), lambda i: (layer, i, 0, 0)),
                     pl.BlockSpec((1, 1, W_B, ctx_len), lambda i: (layer, i, 0, 0))]
        args += [ctx_k, ctx_vt]
    tk = seq + ctx_len
    return pl.pallas_call(
        functools.partial(_mla_kernel, seq=seq, ctx_len=ctx_len, spb=spb),
        grid=(n // rows,), in_specs=in_specs,
        out_specs=pl.BlockSpec((rows, W_B), lambda i: (i, 0)),
        out_shape=jax.ShapeDtypeStruct((n, W_B), BF16),
        scratch_shapes=[pltpu.VMEM((spb, H_B // 2, tk, 256), BF16), pltpu.VMEM((spb, W_B, tk), BF16)],
        compiler_params=_cparams(), name="mla_ctx" if ctx_len else "mla_self",
    )(*args)


def _diff_kernel(*refs, seq, ctx_len, lam_init, spb):
    it = iter(refs)
    qk_ref, vt_ref, gate_ref, lam_ref, g_ref = (next(it) for _ in range(5))
    ck_ref, cv_ref = (next(it), next(it)) if ctx_len else (None, None)
    oc_ref = next(it)
    k_sc, vt_sc = next(it), next(it)

    lam = (jnp.exp(jnp.sum(lam_ref[0, 0:1, :] * lam_ref[0, 1:2, :], axis=-1, keepdims=True))
           - jnp.exp(jnp.sum(lam_ref[0, 2:3, :] * lam_ref[0, 3:4, :], axis=-1, keepdims=True)) + lam_init)
    dh = 2 * DC
    for bb in range(spb):
        rs = slice(bb * seq, (bb + 1) * seq)
        for p in range(H_C // 2):
            k_sc[bb, p, 0:seq, :] = qk_ref[rs, 256 + 128 * p:256 + 128 * p + 128]
            if ctx_len:
                pair_t = jnp.concatenate([ck_ref[0, 0, 2 * p], ck_ref[0, 0, 2 * p + 1]], axis=0)
                k_sc[bb, p, seq:seq + ctx_len, :] = pair_t.T.astype(BF16)
        vt_sc[bb, :, 0:seq] = vt_ref[:, rs]
        if ctx_len:
            for h in range(H_C):
                vt_sc[bb, dh * h:dh * (h + 1), seq:seq + ctx_len] = cv_ref[0, 0, h].astype(BF16)

    def scores(bb, rows, h, comp):
        p, hh = divmod(h, 2)
        qm = qk_ref[rows, 128 * p:128 * p + 128] * _lane_mask(128, dh * hh + DC * comp, DC, BF16)
        return _dot_nt(k_sc[bb, p], qm)

    def run(blocks):
        items = [(bb, rows, h, comp) for bb, rows in blocks for h in range(H_C) for comp in range(2)]
        o12 = _pipelined_attention(
            [functools.partial(scores, *item) for item in items],
            [functools.partial(lambda bb, h: vt_sc[bb, dh * h:dh * (h + 1), :], bb, h) for bb, _, h, _ in items],
            _lookahead(seq + ctx_len))
        for j, (_, rows) in enumerate(blocks):
            outs = []
            for h in range(H_C):
                o1, o2 = o12[2 * (H_C * j + h)], o12[2 * (H_C * j + h) + 1]
                ot = o1 - lam * o2
                outs.append(ot * lax.rsqrt(jnp.mean(ot * ot, axis=0, keepdims=True) + EPS))
            oc = jnp.concatenate(outs, axis=0).T
            oc_ref[rows, :] = (oc * g_ref[0] * (1.0 - lam_init) * gate_ref[rows, :]).astype(BF16)

    qb = min(ATT_QBLOCK, seq)
    if seq == qb:
        run([(bb, slice(bb * seq, (bb + 1) * seq)) for bb in range(spb)])
    else:
        def body(i, carry):
            run([(0, pl.ds(pl.multiple_of(i * qb, qb), qb))])
            return carry
        lax.fori_loop(0, seq // qb, body, 0)


def _diff_call(dqk, dvt, gate, lamp, g4, ck_t, cv_t, layer, *, seq):
    n = dqk.shape[0]
    ctx_len = 0 if ck_t is None else ck_t.shape[4]
    lam_init = 0.8 - 0.6 * math.exp(-0.3 * layer)
    spb = _seqs_per_step(n, seq, ctx_len)
    rows = spb * seq
    in_specs = [pl.BlockSpec((rows, 512), lambda i: (i, 0)), pl.BlockSpec((W_C, rows), lambda i: (0, i)),
                pl.BlockSpec((rows, W_C), lambda i: (i, 0)),
                _layer_spec(lamp.shape, layer), _layer_spec(g4.shape, layer)]
    args = [dqk, dvt, gate, lamp, g4]
    if ctx_len:
        in_specs += [pl.BlockSpec((1, 1, H_C, 2 * DC, ctx_len), lambda i: (i, layer, 0, 0, 0))] * 2
        args += [ck_t, cv_t]
    tk = seq + ctx_len
    return pl.pallas_call(
        functools.partial(_diff_kernel, seq=seq, ctx_len=ctx_len, lam_init=lam_init, spb=spb),
        grid=(n // rows,), in_specs=in_specs,
        out_specs=pl.BlockSpec((rows, W_C), lambda i: (i, 0)),
        out_shape=jax.ShapeDtypeStruct((n, W_C), BF16),
        scratch_shapes=[pltpu.VMEM((spb, H_C // 2, tk, 128), BF16), pltpu.VMEM((spb, W_C, tk), BF16)],
        compiler_params=_cparams(), name="diff_ctx" if ctx_len else "diff_self",
    )(*args)


def _post_kernel(oa_ref, ob_ref, oc_ref, x_ref, mod_ref, w_ref, g_ref, y_ref):
    mix = jnp.concatenate([oa_ref[...], ob_ref[...], oc_ref[...]], axis=-1)
    out = _rms(_dot(mix, w_ref[0]), g_ref[0])
    y_ref[...] = x_ref[...] + mod_ref[0, :, 2 * D_MODEL:3 * D_MODEL] * out


def _post_call(oa, ob, oc, x2d, mod, pw, layer, *, seq, sample):
    n = x2d.shape[0]
    tm = min(POST_BLOCK, seq if sample else n)
    blk = lambda w: pl.BlockSpec((tm, w), lambda i: (i, 0))

    def mod_idx(i):
        return (layer * MOD_ROWS + ((i * tm) // seq + 1 if sample else 0), 0, 0)

    return pl.pallas_call(
        _post_kernel,
        grid=(n // tm,),
        in_specs=[blk(W_A), blk(W_B), blk(W_C), blk(D_MODEL),
                  pl.BlockSpec((1, 1, 3 * D_MODEL), mod_idx),
                  _layer_spec(pw["w_out"].shape, layer), _layer_spec(pw["g_post"].shape, layer)],
        out_specs=blk(D_MODEL),
        out_shape=jax.ShapeDtypeStruct((n, D_MODEL), F32),
        compiler_params=_cparams(), name="post",
    )(oa, ob, oc, x2d, mod, pw["w_out"], pw["g_post"])


def _pack_params(g_pre, g_post, w_in, w_gla_af, b_gla_af, w_gla_ab, b_gla_ab, g_gla, g_mla_q, w_mla_uq,
                 g_mla_kv, w_mla_ukv, lam_q1, lam_k1, lam_q2, lam_k2, g_diff, w_out):
    w_t = jnp.swapaxes(w_in, 1, 2)
    kpe_t = w_t[:, R_MLA[1] - DR_B:R_MLA[1]]
    w_x = jnp.concatenate([kpe_t] * 4, axis=1).astype(BF16)
    zg = jnp.zeros((DEPTH, GLA_LR, 128), F32)
    w_gate = jnp.concatenate([jnp.concatenate([w_gla_af, zg], axis=-1),
                              jnp.concatenate([zg, w_gla_ab], axis=-1)], axis=1).astype(BF16)
    uq = w_mla_uq.reshape(DEPTH, Q_LORA, H_B, DN_B + DR_B)
    w_pe = uq[..., DN_B:].reshape(DEPTH, Q_LORA, H_B * DR_B)
    w_uq = jnp.concatenate([uq[..., :DN_B].reshape(DEPTH, Q_LORA, H_B * DN_B), w_pe], axis=-1).astype(BF16)
    ukv = w_mla_ukv.reshape(DEPTH, KV_LORA, H_B, DN_B + DV_B)
    w_ukv = jnp.concatenate([ukv[..., :DN_B].reshape(DEPTH, KV_LORA, H_B * DN_B),
                             ukv[..., DN_B:].reshape(DEPTH, KV_LORA, H_B * DV_B)], axis=-1).astype(BF16)
    row = lambda a: a.reshape(DEPTH, 1, a.shape[-1])
    return dict(
        w_t=w_t.astype(BF16), w_x=w_x, w_gate=w_gate,
        b_gate=row(jnp.concatenate([b_gla_af, b_gla_ab], axis=-1)),
        w_uq=w_uq, w_ukv=w_ukv, w_out=w_out.astype(BF16),
        g_pre=row(g_pre), g_post=row(g_post), g_mla_q=row(g_mla_q), g_mla_kv=row(g_mla_kv),
        g_gla4=row(jnp.tile(g_gla, (1, H_A))), g_diff4=row(jnp.tile(g_diff, (1, H_C))),
        lam=jnp.stack([lam_q1, lam_k1, lam_q2, lam_k2], axis=1))


def _rope_tables(n):
    t = np.arange(n)
    row = (t // GRID_W).astype(np.float32)
    col = (t % GRID_W).astype(np.float32)
    half = ROPE_DIM // 2
    inv = (1.0 / (np.float32(ROPE_THETA) ** (np.arange(0, half, 2, dtype=np.float32) / np.float32(half)))
           ).astype(np.float32)
    ar = row[:, None] * inv
    ac = col[:, None] * inv
    ang = np.concatenate([ar, ar, ac, ac], axis=-1).astype(np.float32)
    return (jnp.asarray(np.tile(np.cos(ang), (1, 8)).astype(np.float32)),
            jnp.asarray(np.tile(np.sin(ang), (1, 8)).astype(np.float32)))


def _sublayer(x2d, mod, pw, layer, *, seq, rope_tabs, ctx, caches):
    sample = ctx is not None
    pre = _pre_call(x2d, mod, pw, layer, seq=seq, rope_tabs=rope_tabs,
                    caches=None if sample else caches[:4])
    qk, v, la, bsum, ga, gb, gc, mq, mk, mvt, dqk, dvt = pre[:12]
    if sample:
        (oa,) = _gla_call(qk, v, la, bsum, ga, pw["g_gla4"], layer, seq=seq, state_in=ctx["state"])
        ob = _mla_call(mq, mk, mvt, gb, ctx["mla_k"], ctx["mla_vt"], layer, seq=seq)
        oc = _diff_call(dqk, dvt, gc, pw["lam"], pw["g_diff4"], ctx["diff_k_t"], ctx["diff_v_t"], layer,
                        seq=seq)
        new_caches = None
    else:
        oa, sfin = _gla_call(qk, v, la, bsum, ga, pw["g_gla4"], layer, seq=seq,
                             state_out=caches[4] if caches else None)
        ob = _mla_call(mq, mk, mvt, gb, None, None, layer, seq=seq)
        oc = _diff_call(dqk, dvt, gc, pw["lam"], pw["g_diff4"], None, None, layer, seq=seq)
        new_caches = tuple(pre[12:]) + (sfin,)
    y = _post_call(oa, ob, oc, x2d, mod, pw, layer, seq=seq, sample=sample)
    return y, new_caches


def kernel(x_prompt, x_sample, c, cache_mla_ckv, cache_mla_kpe, cache_diff_k, cache_diff_v, state_gla,
           c_ctx, w_ada, b_ada, g_pre, g_post, w_in, w_gla_af, b_gla_af, w_gla_ab, b_gla_ab, g_gla,
           g_mla_q, w_mla_uq, g_mla_kv, w_mla_ukv, lam_q1, lam_k1, lam_q2, lam_k2, g_diff, w_out):
    bp, tp, d = x_prompt.shape
    bs, ts, _ = x_sample.shape

    pw = _pack_params(g_pre, g_post, w_in, w_gla_af, b_gla_af, w_gla_ab, b_gla_ab, g_gla, g_mla_q,
                      w_mla_uq, g_mla_kv, w_mla_ukv, lam_q1, lam_k1, lam_q2, lam_k2, g_diff, w_out)
    cvecs = jnp.concatenate([c_ctx[None], c, jnp.zeros((MOD_ROWS - 1 - bs, d), F32)], axis=0)
    mod = _mod_call(cvecs, w_ada, b_ada).reshape(DEPTH * MOD_ROWS, 1, 3 * d)
    rope_tabs = _rope_tables(ts)
    ctx_k, ctx_vt = _ctxkv_call(cache_mla_ckv, jnp.swapaxes(cache_mla_kpe, -1, -2), pw["w_ukv"])
    ctx = dict(state=state_gla, mla_k=ctx_k, mla_vt=ctx_vt,
               diff_k_t=jnp.swapaxes(cache_diff_k, -1, -2), diff_v_t=jnp.swapaxes(cache_diff_v, -1, -2))

    y_p = x_prompt.reshape(bp * tp, d)
    y_s = x_sample.reshape(bs * ts, d)
    caches = ()
    for l in range(DEPTH):
        y_p, caches = _sublayer(y_p, mod, pw, l, seq=tp, rope_tabs=None, ctx=None, caches=caches)
        y_s, _ = _sublayer(y_s, mod, pw, l, seq=ts, rope_tabs=rope_tabs, ctx=ctx, caches=None)
    ckvn, kpe_t, kc_t, vc_t, new_state = caches
    return (y_p.reshape(bp, tp, d), y_s.reshape(bs, ts, d), ckvn, jnp.swapaxes(kpe_t, -1, -2),
            jnp.swapaxes(kc_t, -1, -2), jnp.swapaxes(vc_t, -1, -2), new_state)
```

```python
import functools
import math

import numpy as np
import jax
import jax.numpy as jnp
from jax import lax
from jax.experimental import pallas as pl
from jax.experimental.pallas import tpu as pltpu

F32 = jnp.float32
BF16 = jnp.bfloat16

D_MODEL = 1024
DEPTH = 2
GRID_W = 64
EPS = 1e-6
ROPE_THETA = 10000.0
ROPE_DIM = 32
H_A, DK_A, DV_A = 4, 32, 64
GLA_LR = 16
GLA_TAU = 16.0
GLA_CHUNK = 64
H_B, DN_B, DR_B, DV_B = 8, 64, 32, 64
Q_LORA, KV_LORA = 256, 128
H_C, DC = 4, 32
W_A, W_B, W_C = H_A * DV_A, H_B * DV_B, H_C * 2 * DC
ST_R, ST_C = H_A * DV_A, H_A * DK_A
LOG2E = math.log2(math.e)
SUBLANES = 8

_IN = dict(ga=512, gg=544, mg=1216)
R_QKV = (0, 512)
R_GG = (512, 768)
R_MLA = (768, 1280)
R_MG = (1280, 1792)
R_DIFF = (1792, 2816)
MOD_ROWS = 8

V7X_VMEM_LIMIT_BYTES = 56 * 1024 * 1024
TOKEN_BLOCK = 1024
POST_BLOCK = 1024
ATT_QBLOCK = 512
ATT_GROUP_ROWS = 1024
GLA_GROUP_ROWS = 1024
ATT_STATIC_BLOCKS = 4


def _cparams(n_axes=1):
    return pltpu.CompilerParams(dimension_semantics=("arbitrary",) * n_axes,
                                vmem_limit_bytes=V7X_VMEM_LIMIT_BYTES)


def _rms(x, g):
    return x * lax.rsqrt(jnp.mean(x * x, axis=-1, keepdims=True) + EPS) * g


def _silu(x):
    return x * jax.nn.sigmoid(x)


def _log_sigmoid(x):
    return jnp.minimum(x, 0.0) - jnp.log1p(jnp.exp(-jnp.abs(x)))


def _rope(z, cos, sin):
    w = z.shape[-1]
    lane = lax.broadcasted_iota(jnp.int32, z.shape, 1)
    rot = jnp.where((lane & 15) < 8, -pltpu.roll(z, w - 8, 1), pltpu.roll(z, 8, 1))
    return z * cos + rot * sin


def _lane_mask(width, lo, size, dtype):
    lane = lax.broadcasted_iota(jnp.int32, (1, width), 1)
    return jnp.where((lane >= lo) & (lane < lo + size), 1.0, 0.0).astype(dtype)


def _dot(a, b):
    return jnp.dot(a, b, preferred_element_type=F32)


def _dot_nt(a, b):
    return lax.dot_general(a, b, (((1,), (1,)), ((), ())), preferred_element_type=F32)


def _dot_tn(a, b):
    return lax.dot_general(a, b, (((0,), (0,)), ((), ())), preferred_element_type=F32)


def _layer_spec(shape, layer):
    nd = len(shape)
    return pl.BlockSpec((1,) + tuple(shape[1:]), lambda *_: (layer,) + (0,) * (nd - 1))


_ANY = pl.BlockSpec(memory_space=pl.ANY)


def _mod_kernel(c_ref, w_ref, b_ref, o_ref):
    s = _silu(c_ref[...]).astype(BF16)
    o_ref[0] = _dot(s, w_ref[0].astype(BF16)) + b_ref[0]


def _mod_call(cvecs, w_ada, b_ada):
    nb = 1024
    return pl.pallas_call(
        _mod_kernel,
        grid=(DEPTH, 3 * D_MODEL // nb),
        in_specs=[pl.BlockSpec((MOD_ROWS, D_MODEL), lambda l, j: (0, 0)),
                  pl.BlockSpec((1, D_MODEL, nb), lambda l, j: (l, 0, j)),
                  pl.BlockSpec((1, 1, nb), lambda l, j: (l, 0, j))],
        out_specs=pl.BlockSpec((1, MOD_ROWS, nb), lambda l, j: (l, 0, j)),
        out_shape=jax.ShapeDtypeStruct((DEPTH, MOD_ROWS, 3 * D_MODEL), F32),
        compiler_params=_cparams(2), name="adaln_mod",
    )(cvecs, w_ada, b_ada.reshape(DEPTH, 1, 3 * D_MODEL))


def _ctxkv_kernel(ckv_ref, kpe_ref, w_ref, k_ref, vt_ref):
    kv = _dot(ckv_ref[0, 0].astype(BF16), w_ref[0])
    kpe4 = jnp.concatenate([kpe_ref[0, 0]] * 4, axis=0).T
    k_ref[0, 0, :, 0:512] = kv[:, 0:512].astype(BF16)
    k_ref[0, 0, :, 512:640] = kpe4.astype(BF16)
    vt_ref[0, 0] = kv[:, 512:1024].T.astype(BF16)


def _ctxkv_call(cache_ckv, cache_kpe_t, wukv):
    nb, _, tc, _ = cache_ckv.shape
    return pl.pallas_call(
        _ctxkv_kernel,
        grid=(DEPTH, nb),
        in_specs=[pl.BlockSpec((1, 1, tc, KV_LORA), lambda l, b: (b, l, 0, 0)),
                  pl.BlockSpec((1, 1, DR_B, tc), lambda l, b: (b, l, 0, 0)),
                  pl.BlockSpec((1, KV_LORA, 1024), lambda l, b: (l, 0, 0))],
        out_specs=[pl.BlockSpec((1, 1, tc, 640), lambda l, b: (l, b, 0, 0)),
                   pl.BlockSpec((1, 1, W_B, tc), lambda l, b: (l, b, 0, 0))],
        out_shape=[jax.ShapeDtypeStruct((DEPTH, nb, tc, 640), BF16),
                   jax.ShapeDtypeStruct((DEPTH, nb, W_B, tc), BF16)],
        compiler_params=_cparams(2), name="mla_ctx_kv",
    )(cache_ckv, cache_kpe_t, wukv)


def _pre_kernel(*refs, rope, ctx_out, alias_in, bpb, seq):
    it = iter(refs)
    (x_ref, mod_ref, gpre_ref, w_ref, wg_ref, bg_ref, gq_ref, gkv_ref, wuq_ref,
     wukv_ref) = (next(it) for _ in range(10))
    if rope:
        cos_ref, sin_ref = next(it), next(it)
    for _ in range(alias_in):
        next(it)
    (qk_ref, v_ref, la_ref, bs_ref, ga_ref, gb_ref, gc_ref, mq_ref, mk_ref, mvt_ref, dqk_ref,
     dvt_ref) = (next(it) for _ in range(12))
    if ctx_out:
        ckvn_ref, kpe_ref, kc_ref, vc_ref = (next(it) for _ in range(4))

    d = D_MODEL
    shift = mod_ref[0, :, 0:d]
    scale = mod_ref[0, :, d:2 * d]
    h = (_rms(x_ref[...], gpre_ref[0]) * (1.0 + scale) + shift).astype(BF16)
    proj = lambda r: _dot_nt(h, w_ref[0, r[0]:r[1], :])
    if rope:
        cos = cos_ref[...]
        sin = sin_ref[...]

    pg = proj(R_QKV)
    qk_ref[:, 0:128] = pg[:, 0:128] * (DK_A ** -0.5)
    qk_ref[:, 128:256] = pg[:, 128:256]
    v_ref[...] = pg[:, 256:512]
    pm = proj(R_MLA)
    tail = pm[:, 384:512]
    xg = _dot(tail.astype(BF16), wg_ref[0]) + bg_ref[0]
    la = _log_sigmoid(xg) * (1.0 / GLA_TAU)
    la_ref[...] = la
    bs_ref[:, 0:128] = _chunk_scan(la[:, 0:128], False)
    bs_ref[:, 128:256] = _chunk_scan(la[:, 128:256], True)
    ga_ref[...] = _silu(proj(R_GG)).astype(BF16)

    qall = _dot(_rms(pm[:, 0:256], gq_ref[0]).astype(BF16), wuq_ref[0])
    q_pe = qall[:, 512:768]
    if rope:
        q_pe = _rope(q_pe, cos, sin)
    sb = (DN_B + DR_B) ** -0.5 * LOG2E
    mq_ref[:, 0:512] = (qall[:, 0:512] * sb).astype(BF16)
    mq_ref[:, 512:768] = (q_pe * sb).astype(BF16)
    ckvn = _rms(pm[:, 256:384], gkv_ref[0])
    kvall = _dot(ckvn.astype(BF16), wukv_ref[0])
    lane = lax.broadcasted_iota(jnp.int32, tail.shape, 1)
    kpe4 = jnp.where(lane < DR_B, tail, 0.0)
    kpe4 = kpe4 + pltpu.roll(kpe4, DR_B, 1)
    kpe4 = kpe4 + pltpu.roll(kpe4, 2 * DR_B, 1)
    if rope:
        kpe4 = _rope(kpe4, cos[:, 0:128], sin[:, 0:128])
    mk_ref[:, 0:512] = kvall[:, 0:512].astype(BF16)
    mk_ref[:, 512:640] = kpe4.astype(BF16)
    mvt_ref[...] = kvall[:, 512:1024].T.astype(BF16)
    gb_ref[...] = _silu(proj(R_MG)).astype(BF16)

    pd = proj(R_DIFF)
    dq, dk, dv = pd[:, 0:256], pd[:, 256:512], pd[:, 512:768]
    if rope:
        dq = _rope(dq, cos, sin)
        dk = _rope(dk, cos, sin)
    dqk_ref[:, 0:256] = (dq * (DC ** -0.5 * LOG2E)).astype(BF16)
    dqk_ref[:, 256:512] = dk.astype(BF16)
    dv_t = dv.T
    dvt_ref[...] = dv_t.astype(BF16)
    gc_ref[...] = _silu(pd[:, 768:1024]).astype(BF16)
    if ctx_out:
        kpe_t = kpe4.T
        dk_t = dk.T
        for ref in () if alias_in else (ckvn_ref, kpe_ref, kc_ref, vc_ref):
            ref[:, 1:] = jnp.zeros(ref[:, 1:].shape, F32)
        for bb in range(bpb):
            rs = slice(bb * seq, (bb + 1) * seq)
            ckvn_ref[bb, 0] = ckvn[rs]
            kpe_ref[bb, 0] = kpe_t[0:DR_B, rs]
            kc_ref[bb, 0] = dk_t[:, rs].reshape(H_C, 2 * DC, seq)
            vc_ref[bb, 0] = dv_t[:, rs].reshape(H_C, 2 * DC, seq)


def _pre_call(x2d, mod, pw, layer, *, seq, rope_tabs, caches):
    n = x2d.shape[0]
    tm = min(TOKEN_BLOCK, n)
    bpb = max(tm // seq, 1)
    rope = rope_tabs is not None
    ctx_out = caches is not None
    steps_per_seq = max(seq // tm, 1)
    nbt = n // seq

    def mod_idx(i):
        return (layer * MOD_ROWS + ((i * tm) // seq + 1 if rope else 0), 0, 0)

    names = ["g_pre", "w_t", "w_gate", "b_gate", "g_mla_q", "g_mla_kv", "w_uq", "w_ukv"]
    in_specs = [pl.BlockSpec((tm, D_MODEL), lambda i: (i, 0)), pl.BlockSpec((1, 1, 3 * D_MODEL), mod_idx)]
    in_specs += [_layer_spec(pw[k].shape, layer) for k in names]
    args = [x2d, mod] + [pw[k] for k in names]
    if rope:
        in_specs += [pl.BlockSpec((tm, 256), lambda i: (i % steps_per_seq, 0))] * 2
        args += list(rope_tabs)
    outs = [(256, F32, False), (256, F32, False), (256, F32, False), (256, F32, False),
            (W_A, BF16, False), (W_B, BF16, False), (W_C, BF16, False),
            (768, BF16, False), (640, BF16, False), (W_B, BF16, True),
            (512, BF16, False), (W_C, BF16, True)]
    out_specs = [pl.BlockSpec((w, tm), lambda i: (0, i)) if tr else pl.BlockSpec((tm, w), lambda i: (i, 0))
                 for w, _, tr in outs]
    out_shape = [jax.ShapeDtypeStruct((w, n) if tr else (n, w), dt) for w, dt, tr in outs]
    widths = outs
    aliases = {}
    if ctx_out:
        assert caches or layer == 0
        nl = 1 if caches else DEPTH
        out_specs += [pl.BlockSpec((bpb, nl, seq, KV_LORA), lambda i: (i, layer, 0, 0)),
                      pl.BlockSpec((bpb, nl, DR_B, seq), lambda i: (i, layer, 0, 0)),
                      pl.BlockSpec((bpb, nl, H_C, 2 * DC, seq), lambda i: (i, layer, 0, 0, 0)),
                      pl.BlockSpec((bpb, nl, H_C, 2 * DC, seq), lambda i: (i, layer, 0, 0, 0))]
        out_shape += [jax.ShapeDtypeStruct((nbt, DEPTH, seq, KV_LORA), F32),
                      jax.ShapeDtypeStruct((nbt, DEPTH, DR_B, seq), F32),
                      jax.ShapeDtypeStruct((nbt, DEPTH, H_C, 2 * DC, seq), F32),
                      jax.ShapeDtypeStruct((nbt, DEPTH, H_C, 2 * DC, seq), F32)]
        for j, arr in enumerate(caches):
            aliases[len(args)] = len(widths) + j
            in_specs.append(_ANY)
            args.append(arr)
    return pl.pallas_call(
        functools.partial(_pre_kernel, rope=rope, ctx_out=ctx_out, alias_in=len(aliases), bpb=bpb, seq=seq),
        grid=(n // tm,), in_specs=in_specs, out_specs=out_specs, out_shape=out_shape,
        input_output_aliases=aliases,
        compiler_params=_cparams(), name="pre_rope" if rope else "pre_ctx",
    )(*args)


_GLA_LEVELS = (1, 2, 4, 8, 16, 32)


def _gla_consts(rev):
    c = GLA_CHUNK
    row = lax.broadcasted_iota(jnp.int32, (c, 128), 0)
    pos = (c - 1 - row) if rev else row
    ri = lax.broadcasted_iota(jnp.int32, (c, H_A * c), 0)
    cj = lax.broadcasted_iota(jnp.int32, (c, H_A * c), 1) & (c - 1)
    pi = (c - 1 - ri) if rev else ri
    pj = (c - 1 - cj) if rev else cj
    x = pi ^ pj
    lvl = jnp.where(pi == pj, 0, -1)
    for kbit in range(6):
        lvl = jnp.where((pj < pi) & ((x >> kbit) == 1), kbit + 1, lvl)
    return pos, lvl


def _chunk_scan(x, rev):
    rows = x.shape[0]
    nt = rows // SUBLANES
    tiles_per_chunk = GLA_CHUNK // SUBLANES
    x3 = x.reshape(nt, SUBLANES, 128)
    sub = lax.broadcasted_iota(jnp.int32, x3.shape, 1)
    tile = lax.broadcasted_iota(jnp.int32, x3.shape, 0) & (tiles_per_chunk - 1)
    edge = 0 if rev else SUBLANES - 1
    s = 1
    while s < SUBLANES:
        if rev:
            x3 = x3 + jnp.where(sub < SUBLANES - s, pltpu.roll(x3, SUBLANES - s, 1), 0.0)
        else:
            x3 = x3 + jnp.where(sub >= s, pltpu.roll(x3, s, 1), 0.0)
        s *= 2
    s = 1
    while s < tiles_per_chunk:
        tot = jnp.broadcast_to(x3[:, edge:edge + 1, :], x3.shape)
        if rev:
            shifted = jnp.concatenate([tot[s:], tot[:s]], axis=0)
            x3 = x3 + jnp.where(tile < tiles_per_chunk - s, shifted, 0.0)
        else:
            shifted = jnp.concatenate([tot[nt - s:], tot[:nt - s]], axis=0)
            x3 = x3 + jnp.where(tile >= s, shifted, 0.0)
        s *= 2
    return x3.reshape(rows, 128)


def _gla_scores(q, k, la, b, pos, lvl, hm_bf, rev):
    c = GLA_CHUNK
    prv = pltpu.roll(la, c - 1 if rev else 1, 0)
    nxt = pltpu.roll(la, 1 if rev else c - 1, 0)
    s_tot = jnp.where(lvl == 0, _dot_nt(q.astype(BF16), jnp.concatenate([k.astype(BF16)] * H_A, 0) * hm_bf), 0.0)
    for kbit, m in enumerate(_GLA_LEVELS):
        up = ((pos >> kbit) & 1) == 1
        if m == 1:
            e = jnp.where(up, la, 0.0)
        elif m == 2:
            c4 = pos & 3
            e = jnp.where(c4 == 0, nxt, jnp.where(c4 == 1, 0.0, jnp.where(c4 == 2, la, la + prv)))
        else:
            nblk = c // (2 * m)
            loc = m if rev else m - 1
            b3 = b.reshape(nblk, 2 * m, 128)
            ref = jnp.broadcast_to(b3[:, loc:loc + 1, :], (nblk, 2 * m, 128)).reshape(c, 128)
            dlt = b - ref
            e = jnp.where(up, dlt, -dlt)
        xm = (jnp.where(up, q, k) * jnp.exp(e)).astype(BF16)
        sm = _dot_nt(xm, jnp.concatenate([xm] * H_A, 0) * hm_bf)
        s_tot = jnp.where(lvl == kbit + 1, sm, s_tot)
    return s_tot


def _gla_apply(s_tot, q, k, v, b, st_prev, hm_f32, vm_bf, rev):
    c = GLA_CHUNK
    vbd = jnp.concatenate([v] * H_A, 0) * vm_bf
    blast = b[0:1, :] if rev else b[c - 1:c, :]
    qbar = (q * jnp.exp(b)).astype(BF16)
    kdec = (k * jnp.exp(blast - b)).astype(BF16)
    o = _dot(s_tot.astype(BF16), vbd) + _dot_nt(qbar, st_prev.astype(BF16))
    st_new = st_prev * jnp.exp(blast) + _dot_tn(v, kdec) * hm_f32
    return o, st_new


def _gla_kernel(*refs, seq, has_s0, alias_in, spb):
    it = iter(refs)
    qk_ref, v_ref, la_ref, b_ref, gate_ref, g_ref = (next(it) for _ in range(6))
    s0_ref = next(it) if has_s0 else None
    for _ in range(alias_in):
        next(it)
    oa_ref = next(it)
    sfin_ref = None if has_s0 else next(it)
    acc_sc, st_sc, blk_sc = next(it), next(it), next(it)

    c = GLA_CHUNK
    nc = seq // c
    acc_sc[...] = jnp.zeros_like(acc_sc)
    if not has_s0 and not alias_in:
        sfin_ref[:, 1:] = jnp.zeros(sfin_ref[:, 1:].shape, F32)

    hrow = lax.broadcasted_iota(jnp.int32, (H_A * c, 128), 0) // c
    hm_f32 = jnp.where(hrow == lax.broadcasted_iota(jnp.int32, (H_A * c, 128), 1) // DK_A, 1.0, 0.0)
    hm_bf = hm_f32.astype(BF16)
    vrow = lax.broadcasted_iota(jnp.int32, (H_A * c, H_A * DV_A), 0) // c
    vm_bf = jnp.where(vrow == lax.broadcasted_iota(jnp.int32, (H_A * c, H_A * DV_A), 1) // DV_A,
                      1.0, 0.0).astype(BF16)
    consts = (_gla_consts(False), _gla_consts(True))

    def run_sequence(bb):
        base = bb * seq
        for d in range(2):
            if has_s0:
                blk_sc[...] = jnp.zeros_like(blk_sc)
                for hh in range(H_A):
                    blk_sc[DK_A * hh:DK_A * (hh + 1), DV_A * hh:DV_A * (hh + 1)] = s0_ref[bb, 0, d, hh]
                st_sc[d] = blk_sc[...].T
            else:
                st_sc[d] = jnp.zeros((ST_R, ST_C), F32)

        def chunk_rows(n, d):
            cn = (nc - 1 - n) if d else n
            start = base + cn * c
            return pl.ds(start if isinstance(start, int) else pl.multiple_of(start, c), c)

        def scores(n):
            out = []
            for d in range(2):
                rows = chunk_rows(n, d)
                pos, lvl = consts[d]
                out.append(_gla_scores(qk_ref[rows, 0:128], qk_ref[rows, 128:256],
                                       la_ref[rows, 128 * d:128 * d + 128],
                                       b_ref[rows, 128 * d:128 * d + 128], pos, lvl, hm_bf, bool(d)))
            return tuple(out)

        def apply(n, s_both):
            for d in range(2):
                rows = chunk_rows(n, d)
                o, st_new = _gla_apply(s_both[d], qk_ref[rows, 0:128], qk_ref[rows, 128:256],
                                       v_ref[rows, :].astype(BF16), b_ref[rows, 128 * d:128 * d + 128],
                                       st_sc[d], hm_f32, vm_bf, bool(d))
                acc_sc[rows, :] = acc_sc[rows, :] + o
                st_sc[d] = st_new

        def body(n, s_cur):
            s_next = scores(n + 1)
            apply(n, s_cur)
            return s_next

        s_last = lax.fori_loop(0, nc - 1, body, scores(0), unroll=3 if (nc - 1) % 3 == 0 else 1)
        apply(nc - 1, s_last)

        rs = slice(base, base + seq)
        first = lax.broadcasted_iota(jnp.int32, (seq, 128), 1) < DV_A
        for p in range(H_A // 2):
            cols = slice(128 * p, 128 * (p + 1))
            oa = acc_sc[rs, cols]
            sq = oa * oa
            s0 = jnp.sum(jnp.where(first, sq, 0.0), axis=-1, keepdims=True)
            s1 = jnp.sum(jnp.where(first, 0.0, sq), axis=-1, keepdims=True)
            inv = jnp.where(first, lax.rsqrt(s0 * (1.0 / DV_A) + EPS), lax.rsqrt(s1 * (1.0 / DV_A) + EPS))
            oa_ref[rs, cols] = (oa * inv * g_ref[0, :, cols] * gate_ref[rs, cols]).astype(BF16)
        if not has_s0:
            for d in range(2):
                blk_sc[...] = st_sc[d].T
                for hh in range(H_A):
                    sfin_ref[bb, 0, d, hh] = blk_sc[DK_A * hh:DK_A * (hh + 1), DV_A * hh:DV_A * (hh + 1)]

    for bb in range(spb):
        run_sequence(bb)


def _gla_call(qk, v, la, bsum, gate, g4, layer, *, seq, state_in=None, state_out=None):
    n = qk.shape[0]
    nb = n // seq
    has_s0 = state_in is not None
    assert has_s0 or state_out is not None or layer == 0
    spb = max(1, min(nb, GLA_GROUP_ROWS // seq))
    blk = lambda w: pl.BlockSpec((spb * seq, w), lambda i: (i, 0))
    nl = DEPTH if (not has_s0 and state_out is None) else 1
    st_spec = pl.BlockSpec((spb, nl, 2, H_A, DK_A, DV_A), lambda i: (i, layer, 0, 0, 0, 0))
    in_specs = [blk(256), blk(256), blk(256), blk(256), blk(W_A), _layer_spec(g4.shape, layer)]
    args = [qk, v, la, bsum, gate, g4]
    out_specs = [blk(W_A)]
    out_shape = [jax.ShapeDtypeStruct((n, W_A), BF16)]
    aliases = {}
    if has_s0:
        in_specs.append(st_spec)
        args.append(state_in)
    else:
        out_specs.append(st_spec)
        out_shape.append(jax.ShapeDtypeStruct((nb, DEPTH, 2, H_A, DK_A, DV_A), F32))
        if state_out is not None:
            aliases[len(args)] = 1
            in_specs.append(_ANY)
            args.append(state_out)
    return pl.pallas_call(
        functools.partial(_gla_kernel, seq=seq, has_s0=has_s0, alias_in=len(aliases), spb=spb),
        grid=(nb // spb,), in_specs=in_specs, out_specs=out_specs, out_shape=out_shape,
        input_output_aliases=aliases,
        scratch_shapes=[pltpu.VMEM((spb * seq, W_A), F32), pltpu.VMEM((2, ST_R, ST_C), F32),
                        pltpu.VMEM((ST_C, ST_R), F32)],
        compiler_params=_cparams(), name="gla_state" if has_s0 else "gla_ctx",
    )(*args)


def _softmax_t_pv(st, vt):
    m = _col_reduce(st, jnp.max)
    e = jnp.exp2(st - m)
    l = _col_reduce(e, jnp.sum)
    return _dot(vt, e.astype(BF16)) * (1.0 / l)


def _pipelined_attention(score_fns, value_fns, depth):
    outs = []
    pending = [fn() for fn in score_fns[:depth]]
    for j, vfn in enumerate(value_fns):
        st = pending.pop(0)
        if j + depth < len(score_fns):
            pending.append(score_fns[j + depth]())
        outs.append(_softmax_t_pv(st, vfn()))
    return outs


def _lookahead(n_keys):
    return 2 if n_keys >= 1024 else 3


def _col_reduce(x, op):
    rows, cols = x.shape
    part = 128 if rows % 128 == 0 and rows > 128 else rows
    if part != rows:
        x = op(x.reshape(rows // part, part, cols), axis=0)
    return op(x, axis=0, keepdims=True)


def _seqs_per_step(n, seq, ctx_len):
    if ctx_len or seq >= ATT_QBLOCK:
        return 1
    return max(1, min(n // seq, ATT_GROUP_ROWS // seq))


def _for_query_blocks(run, seq, spb):
    qb = min(ATT_QBLOCK, seq)
    nq = seq // qb
    if spb * nq <= ATT_STATIC_BLOCKS:
        run([(bb, slice(bb * seq + j * qb, bb * seq + (j + 1) * qb)) for bb in range(spb) for j in range(nq)])
    else:
        assert spb == 1

        def body(i, carry):
            run([(0, pl.ds(pl.multiple_of(i * qb, qb), qb))])
            return carry
        lax.fori_loop(0, nq, body, 0)


def _mla_kernel(*refs, seq, ctx_len, spb):
    it = iter(refs)
    q_ref, k_ref, vt_ref, gate_ref = (next(it) for _ in range(4))
    ck_ref, cvt_ref = (next(it), next(it)) if ctx_len else (None, None)
    ob_ref = next(it)
    kk_sc, vt_sc = next(it), next(it)

    for bb in range(spb):
        rs = slice(bb * seq, (bb + 1) * seq)
        for p in range(H_B // 2):
            kk_sc[bb, p, 0:seq, 0:128] = k_ref[rs, 128 * p:128 * p + 128]
            kk_sc[bb, p, 0:seq, 128:256] = k_ref[rs, 512:640]
            if ctx_len:
                kk_sc[bb, p, seq:seq + ctx_len, 0:128] = ck_ref[0, 0, :, 128 * p:128 * p + 128]
                kk_sc[bb, p, seq:seq + ctx_len, 128:256] = ck_ref[0, 0, :, 512:640]
        vt_sc[bb, :, 0:seq] = vt_ref[:, rs]
        if ctx_len:
            vt_sc[bb, :, seq:seq + ctx_len] = cvt_ref[0, 0]

    def scores(bb, rows, h):
        p, hh = divmod(h, 2)
        qn = q_ref[rows, 128 * p:128 * p + 128] * _lane_mask(128, DN_B * hh, DN_B, BF16)
        qp = (q_ref[rows, 512 + 128 * (h // 4):512 + 128 * (h // 4) + 128]
              * _lane_mask(128, DR_B * (h % 4), DR_B, BF16))
        return _dot_nt(kk_sc[bb, p], jnp.concatenate([qn, qp], axis=-1))

    def run(blocks):
        items = [(bb, rows, h) for bb, rows in blocks for h in range(H_B)]
        outs = _pipelined_attention(
            [functools.partial(scores, bb, rows, h) for bb, rows, h in items],
            [functools.partial(lambda bb, h: vt_sc[bb, DV_B * h:DV_B * (h + 1), :], bb, h) for bb, _, h in items],
            _lookahead(seq + ctx_len))
        for j, (_, rows) in enumerate(blocks):
            ob = jnp.concatenate(outs[H_B * j:H_B * (j + 1)], axis=0).T
            ob_ref[rows, :] = (ob * gate_ref[rows, :]).astype(BF16)

    _for_query_blocks(run, seq, spb)


def _mla_call(mq, mk, mvt, gate, ctx_k, ctx_vt, layer, *, seq):
    n = mq.shape[0]
    ctx_len = 0 if ctx_k is None else ctx_k.shape[2]
    spb = _seqs_per_step(n, seq, ctx_len)
    rows = spb * seq
    in_specs = [pl.BlockSpec((rows, 768), lambda i: (i, 0)), pl.BlockSpec((rows, 640), lambda i: (i, 0)),
                pl.BlockSpec((W_B, rows), lambda i: (0, i)), pl.BlockSpec((rows, W_B), lambda i: (i, 0))]
    args = [mq, mk, mvt, gate]
    if ctx_len:
        in_specs += [pl.BlockSpec((1, 1, ctx_len, 640), lambda i: (layer, i, 0, 0)),
                     pl.BlockSpec((1, 1, W_B, ctx_len), lambda i: (layer, i, 0, 0))]
        args += [ctx_k, ctx_vt]
    tk = seq + ctx_len
    return pl.pallas_call(
        functools.partial(_mla_kernel, seq=seq, ctx_len=ctx_len, spb=spb),
        grid=(n // rows,), in_specs=in_specs,
        out_specs=pl.BlockSpec((rows, W_B), lambda i: (i, 0)),
        out_shape=jax.ShapeDtypeStruct((n, W_B), BF16),
        scratch_shapes=[pltpu.VMEM((spb, H_B // 2, tk, 256), BF16), pltpu.VMEM((spb, W_B, tk), BF16)],
        compiler_params=_cparams(), name="mla_ctx" if ctx_len else "mla_self",
    )(*args)


def _diff_kernel(*refs, seq, ctx_len, lam_init, spb):
    it = iter(refs)
    qk_ref, vt_ref, gate_ref, lam_ref, g_ref = (next(it) for _ in range(5))
    ck_ref, cv_ref = (next(it), next(it)) if ctx_len else (None, None)
    oc_ref = next(it)
    k_sc, vt_sc = next(it), next(it)

    lam = (jnp.exp(jnp.sum(lam_ref[0, 0:1, :] * lam_ref[0, 1:2, :], axis=-1, keepdims=True))
           - jnp.exp(jnp.sum(lam_ref[0, 2:3, :] * lam_ref[0, 3:4, :], axis=-1, keepdims=True)) + lam_init)
    dh = 2 * DC
    for bb in range(spb):
        rs = slice(bb * seq, (bb + 1) * seq)
        for p in range(H_C // 2):
            k_sc[bb, p, 0:seq, :] = qk_ref[rs, 256 + 128 * p:256 + 128 * p + 128]
            if ctx_len:
                pair_t = jnp.concatenate([ck_ref[0, 0, 2 * p], ck_ref[0, 0, 2 * p + 1]], axis=0)
                k_sc[bb, p, seq:seq + ctx_len, :] = pair_t.T.astype(BF16)
        vt_sc[bb, :, 0:seq] = vt_ref[:, rs]
        if ctx_len:
            for h in range(H_C):
                vt_sc[bb, dh * h:dh * (h + 1), seq:seq + ctx_len] = cv_ref[0, 0, h].astype(BF16)

    def scores(bb, rows, h, comp):
        p, hh = divmod(h, 2)
        qm = qk_ref[rows, 128 * p:128 * p + 128] * _lane_mask(128, dh * hh + DC * comp, DC, BF16)
        return _dot_nt(k_sc[bb, p], qm)

    def run(blocks):
        items = [(bb, rows, h, comp) for bb, rows in blocks for h in range(H_C) for comp in range(2)]
        o12 = _pipelined_attention(
            [functools.partial(scores, *item) for item in items],
            [functools.partial(lambda bb, h: vt_sc[bb, dh * h:dh * (h + 1), :], bb, h) for bb, _, h, _ in items],
            _lookahead(seq + ctx_len))
        for j, (_, rows) in enumerate(blocks):
            outs = []
            for h in range(H_C):
                o1, o2 = o12[2 * (H_C * j + h)], o12[2 * (H_C * j + h) + 1]
                ot = o1 - lam * o2
                outs.append(ot * lax.rsqrt(jnp.mean(ot * ot, axis=0, keepdims=True) + EPS))
            oc = jnp.concatenate(outs, axis=0).T
            oc_ref[rows, :] = (oc * g_ref[0] * (1.0 - lam_init) * gate_ref[rows, :]).astype(BF16)

    _for_query_blocks(run, seq, spb)


def _diff_call(dqk, dvt, gate, lamp, g4, ck_t, cv_t, layer, *, seq):
    n = dqk.shape[0]
    ctx_len = 0 if ck_t is None else ck_t.shape[4]
    lam_init = 0.8 - 0.6 * math.exp(-0.3 * layer)
    spb = _seqs_per_step(n, seq, ctx_len)
    rows = spb * seq
    in_specs = [pl.BlockSpec((rows, 512), lambda i: (i, 0)), pl.BlockSpec((W_C, rows), lambda i: (0, i)),
                pl.BlockSpec((rows, W_C), lambda i: (i, 0)),
                _layer_spec(lamp.shape, layer), _layer_spec(g4.shape, layer)]
    args = [dqk, dvt, gate, lamp, g4]
    if ctx_len:
        in_specs += [pl.BlockSpec((1, 1, H_C, 2 * DC, ctx_len), lambda i: (i, layer, 0, 0, 0))] * 2
        args += [ck_t, cv_t]
    tk = seq + ctx_len
    return pl.pallas_call(
        functools.partial(_diff_kernel, seq=seq, ctx_len=ctx_len, lam_init=lam_init, spb=spb),
        grid=(n // rows,), in_specs=in_specs,
        out_specs=pl.BlockSpec((rows, W_C), lambda i: (i, 0)),
        out_shape=jax.ShapeDtypeStruct((n, W_C), BF16),
        scratch_shapes=[pltpu.VMEM((spb, H_C // 2, tk, 128), BF16), pltpu.VMEM((spb, W_C, tk), BF16)],
        compiler_params=_cparams(), name="diff_ctx" if ctx_len else "diff_self",
    )(*args)


def _post_kernel(oa_ref, ob_ref, oc_ref, x_ref, mod_ref, w_ref, g_ref, y_ref):
    mix = jnp.concatenate([oa_ref[...], ob_ref[...], oc_ref[...]], axis=-1)
    out = _rms(_dot(mix, w_ref[0]), g_ref[0])
    y_ref[...] = x_ref[...] + mod_ref[0, :, 2 * D_MODEL:3 * D_MODEL] * out


def _post_call(oa, ob, oc, x2d, mod, pw, layer, *, seq, sample):
    n = x2d.shape[0]
    tm = min(POST_BLOCK, seq if sample else n)
    blk = lambda w: pl.BlockSpec((tm, w), lambda i: (i, 0))

    def mod_idx(i):
        return (layer * MOD_ROWS + ((i * tm) // seq + 1 if sample else 0), 0, 0)

    return pl.pallas_call(
        _post_kernel,
        grid=(n // tm,),
        in_specs=[blk(W_A), blk(W_B), blk(W_C), blk(D_MODEL),
                  pl.BlockSpec((1, 1, 3 * D_MODEL), mod_idx),
                  _layer_spec(pw["w_out"].shape, layer), _layer_spec(pw["g_post"].shape, layer)],
        out_specs=blk(D_MODEL),
        out_shape=jax.ShapeDtypeStruct((n, D_MODEL), F32),
        compiler_params=_cparams(), name="post",
    )(oa, ob, oc, x2d, mod, pw["w_out"], pw["g_post"])


def _pack_params(g_pre, g_post, w_in, w_gla_af, b_gla_af, w_gla_ab, b_gla_ab, g_gla, g_mla_q, w_mla_uq,
                 g_mla_kv, w_mla_ukv, lam_q1, lam_k1, lam_q2, lam_k2, g_diff, w_out):
    w_t = jnp.swapaxes(w_in, 1, 2)
    ga0, ga1 = _IN["ga"], _IN["gg"]
    w_t = jnp.concatenate([w_t[:, 0:ga0], w_t[:, ga1:_IN["mg"]], w_t[:, ga0:ga1],
                           jnp.zeros((DEPTH, 64, D_MODEL), F32), w_t[:, _IN["mg"]:]], axis=1)
    assert w_t.shape[1] == R_DIFF[1]
    zg = jnp.zeros((DEPTH, GLA_LR, 128), F32)
    w_gate = jnp.concatenate([jnp.zeros((DEPTH, DR_B, 256), F32),
                              jnp.concatenate([w_gla_af, zg], axis=-1),
                              jnp.concatenate([zg, w_gla_ab], axis=-1),
                              jnp.zeros((DEPTH, 128 - DR_B - 2 * GLA_LR, 256), F32)], axis=1).astype(BF16)
    uq = w_mla_uq.reshape(DEPTH, Q_LORA, H_B, DN_B + DR_B)
    w_pe = uq[..., DN_B:].reshape(DEPTH, Q_LORA, H_B * DR_B)
    w_uq = jnp.concatenate([uq[..., :DN_B].reshape(DEPTH, Q_LORA, H_B * DN_B), w_pe], axis=-1).astype(BF16)
    ukv = w_mla_ukv.reshape(DEPTH, KV_LORA, H_B, DN_B + DV_B)
    w_ukv = jnp.concatenate([ukv[..., :DN_B].reshape(DEPTH, KV_LORA, H_B * DN_B),
                             ukv[..., DN_B:].reshape(DEPTH, KV_LORA, H_B * DV_B)], axis=-1).astype(BF16)
    row = lambda a: a.reshape(DEPTH, 1, a.shape[-1])
    return dict(
        w_t=w_t.astype(BF16), w_gate=w_gate,
        b_gate=row(jnp.concatenate([b_gla_af, b_gla_ab], axis=-1)),
        w_uq=w_uq, w_ukv=w_ukv, w_out=w_out.astype(BF16),
        g_pre=row(g_pre), g_post=row(g_post), g_mla_q=row(g_mla_q), g_mla_kv=row(g_mla_kv),
        g_gla4=row(jnp.tile(g_gla, (1, H_A))), g_diff4=row(jnp.tile(g_diff, (1, H_C))),
        lam=jnp.stack([lam_q1, lam_k1, lam_q2, lam_k2], axis=1))


def _rope_tables(n):
    t = np.arange(n)
    row = (t // GRID_W).astype(np.float32)
    col = (t % GRID_W).astype(np.float32)
    half = ROPE_DIM // 2
    inv = (1.0 / (np.float32(ROPE_THETA) ** (np.arange(0, half, 2, dtype=np.float32) / np.float32(half)))
           ).astype(np.float32)
    ar = row[:, None] * inv
    ac = col[:, None] * inv
    ang = np.concatenate([ar, ar, ac, ac], axis=-1).astype(np.float32)
    return (jnp.asarray(np.tile(np.cos(ang), (1, 8)).astype(np.float32)),
            jnp.asarray(np.tile(np.sin(ang), (1, 8)).astype(np.float32)))


def _sublayer(x2d, mod, pw, layer, *, seq, rope_tabs, ctx, caches):
    sample = ctx is not None
    pre = _pre_call(x2d, mod, pw, layer, seq=seq, rope_tabs=rope_tabs,
                    caches=None if sample else caches[:4])
    qk, v, la, bsum, ga, gb, gc, mq, mk, mvt, dqk, dvt = pre[:12]
    if sample:
        (oa,) = _gla_call(qk, v, la, bsum, ga, pw["g_gla4"], layer, seq=seq, state_in=ctx["state"])
        ob = _mla_call(mq, mk, mvt, gb, ctx["mla_k"], ctx["mla_vt"], layer, seq=seq)
        oc = _diff_call(dqk, dvt, gc, pw["lam"], pw["g_diff4"], ctx["diff_k_t"], ctx["diff_v_t"], layer,
                        seq=seq)
        new_caches = None
    else:
        oa, sfin = _gla_call(qk, v, la, bsum, ga, pw["g_gla4"], layer, seq=seq,
                             state_out=caches[4] if caches else None)
        ob = _mla_call(mq, mk, mvt, gb, None, None, layer, seq=seq)
        oc = _diff_call(dqk, dvt, gc, pw["lam"], pw["g_diff4"], None, None, layer, seq=seq)
        new_caches = tuple(pre[12:]) + (sfin,)
    y = _post_call(oa, ob, oc, x2d, mod, pw, layer, seq=seq, sample=sample)
    return y, new_caches


def kernel(x_prompt, x_sample, c, cache_mla_ckv, cache_mla_kpe, cache_diff_k, cache_diff_v, state_gla,
           c_ctx, w_ada, b_ada, g_pre, g_post, w_in, w_gla_af, b_gla_af, w_gla_ab, b_gla_ab, g_gla,
           g_mla_q, w_mla_uq, g_mla_kv, w_mla_ukv, lam_q1, lam_k1, lam_q2, lam_k2, g_diff, w_out):
    bp, tp, d = x_prompt.shape
    bs, ts, _ = x_sample.shape

    pw = _pack_params(g_pre, g_post, w_in, w_gla_af, b_gla_af, w_gla_ab, b_gla_ab, g_gla, g_mla_q,
                      w_mla_uq, g_mla_kv, w_mla_ukv, lam_q1, lam_k1, lam_q2, lam_k2, g_diff, w_out)
    cvecs = jnp.concatenate([c_ctx[None], c, jnp.zeros((MOD_ROWS - 1 - bs, d), F32)], axis=0)
    mod = _mod_call(cvecs, w_ada, b_ada).reshape(DEPTH * MOD_ROWS, 1, 3 * d)
    rope_tabs = _rope_tables(ts)
    ctx_k, ctx_vt = _ctxkv_call(cache_mla_ckv, jnp.swapaxes(cache_mla_kpe, -1, -2), pw["w_ukv"])
    ctx = dict(state=state_gla, mla_k=ctx_k, mla_vt=ctx_vt,
               diff_k_t=jnp.swapaxes(cache_diff_k, -1, -2), diff_v_t=jnp.swapaxes(cache_diff_v, -1, -2))

    y_p = x_prompt.reshape(bp * tp, d)
    y_s = x_sample.reshape(bs * ts, d)
    caches = ()
    for l in range(DEPTH):
        y_p, caches = _sublayer(y_p, mod, pw, l, seq=tp, rope_tabs=None, ctx=None, caches=caches)
        y_s, _ = _sublayer(y_s, mod, pw, l, seq=ts, rope_tabs=rope_tabs, ctx=ctx, caches=None)
    ckvn, kpe_t, kc_t, vc_t, new_state = caches
    return (y_p.reshape(bp, tp, d), y_s.reshape(bs, ts, d), ckvn, jnp.swapaxes(kpe_t, -1, -2),
            jnp.swapaxes(kc_t, -1, -2), jnp.swapaxes(vc_t, -1, -2), new_state)
```

```python
import functools
import math

import numpy as np
import jax
import jax.numpy as jnp
from jax import lax
from jax.experimental import pallas as pl
from jax.experimental.pallas import tpu as pltpu

F32 = jnp.float32
BF16 = jnp.bfloat16

D_MODEL = 1024
DEPTH = 2
GRID_W = 64
EPS = 1e-6
ROPE_THETA = 10000.0
ROPE_DIM = 32
H_A, DK_A, DV_A = 4, 32, 64
GLA_LR = 16
GLA_TAU = 16.0
GLA_CHUNK = 64
H_B, DN_B, DR_B, DV_B = 8, 64, 32, 64
Q_LORA, KV_LORA = 256, 128
H_C, DC = 4, 32
W_A, W_B, W_C = H_A * DV_A, H_B * DV_B, H_C * 2 * DC
ST_R, ST_C = H_A * DV_A, H_A * DK_A
LOG2E = math.log2(math.e)
SUBLANES = 8

_IN = dict(ga=512, gg=544, mg=1216)
R_QKV = (0, 512)
R_GG = (512, 768)
R_MLA = (768, 1280)
R_MG = (1280, 1792)
R_DIFF = (1792, 2816)
MOD_ROWS = 8

V7X_VMEM_LIMIT_BYTES = 56 * 1024 * 1024
TOKEN_BLOCK = 512
POST_BLOCK = 512
ATT_QBLOCK = 512
ATT_GROUP_ROWS = 1024
GLA_GROUP_ROWS = 1024
ATT_STATIC_BLOCKS = 4


def _cparams(n_axes=1):
    return pltpu.CompilerParams(dimension_semantics=("arbitrary",) * n_axes,
                                vmem_limit_bytes=V7X_VMEM_LIMIT_BYTES)


def _rms(x, g):
    return x * lax.rsqrt(jnp.mean(x * x, axis=-1, keepdims=True) + EPS) * g


def _silu(x):
    return x * jax.nn.sigmoid(x)


def _log_sigmoid(x):
    return jnp.minimum(x, 0.0) - jnp.log1p(jnp.exp(-jnp.abs(x)))


def _rope(z, cos, sin):
    w = z.shape[-1]
    lane = lax.broadcasted_iota(jnp.int32, z.shape, 1)
    rot = jnp.where((lane & 15) < 8, -pltpu.roll(z, w - 8, 1), pltpu.roll(z, 8, 1))
    return z * cos + rot * sin


def _lane_mask(width, lo, size, dtype):
    lane = lax.broadcasted_iota(jnp.int32, (1, width), 1)
    return jnp.where((lane >= lo) & (lane < lo + size), 1.0, 0.0).astype(dtype)


def _dot(a, b):
    return jnp.dot(a, b, preferred_element_type=F32)


def _dot_nt(a, b):
    return lax.dot_general(a, b, (((1,), (1,)), ((), ())), preferred_element_type=F32)


def _dot_tn(a, b):
    return lax.dot_general(a, b, (((0,), (0,)), ((), ())), preferred_element_type=F32)


def _layer_spec(shape, layer):
    nd = len(shape)
    return pl.BlockSpec((1,) + tuple(shape[1:]), lambda *_: (layer,) + (0,) * (nd - 1))


_ANY = pl.BlockSpec(memory_space=pl.ANY)


def _mod_kernel(c_ref, w_ref, b_ref, o_ref):
    s = _silu(c_ref[...]).astype(BF16)
    o_ref[0] = _dot(s, w_ref[0].astype(BF16)) + b_ref[0]


def _mod_call(cvecs, w_ada, b_ada):
    nb = 1024
    return pl.pallas_call(
        _mod_kernel,
        grid=(DEPTH, 3 * D_MODEL // nb),
        in_specs=[pl.BlockSpec((MOD_ROWS, D_MODEL), lambda l, j: (0, 0)),
                  pl.BlockSpec((1, D_MODEL, nb), lambda l, j: (l, 0, j)),
                  pl.BlockSpec((1, 1, nb), lambda l, j: (l, 0, j))],
        out_specs=pl.BlockSpec((1, MOD_ROWS, nb), lambda l, j: (l, 0, j)),
        out_shape=jax.ShapeDtypeStruct((DEPTH, MOD_ROWS, 3 * D_MODEL), F32),
        compiler_params=_cparams(2), name="adaln_mod",
    )(cvecs, w_ada, b_ada.reshape(DEPTH, 1, 3 * D_MODEL))


def _ctxkv_kernel(ckv_ref, kpe_ref, w_ref, k_ref, vt_ref):
    kv = _dot(ckv_ref[0, 0].astype(BF16), w_ref[0])
    kpe4 = jnp.concatenate([kpe_ref[0, 0]] * 4, axis=0).T
    k_ref[0, 0, :, 0:512] = kv[:, 0:512].astype(BF16)
    k_ref[0, 0, :, 512:640] = kpe4.astype(BF16)
    vt_ref[0, 0] = kv[:, 512:1024].T.astype(BF16)


def _ctxkv_call(cache_ckv, cache_kpe_t, wukv):
    nb, _, tc, _ = cache_ckv.shape
    return pl.pallas_call(
        _ctxkv_kernel,
        grid=(DEPTH, nb),
        in_specs=[pl.BlockSpec((1, 1, tc, KV_LORA), lambda l, b: (b, l, 0, 0)),
                  pl.BlockSpec((1, 1, DR_B, tc), lambda l, b: (b, l, 0, 0)),
                  pl.BlockSpec((1, KV_LORA, 1024), lambda l, b: (l, 0, 0))],
        out_specs=[pl.BlockSpec((1, 1, tc, 640), lambda l, b: (l, b, 0, 0)),
                   pl.BlockSpec((1, 1, W_B, tc), lambda l, b: (l, b, 0, 0))],
        out_shape=[jax.ShapeDtypeStruct((DEPTH, nb, tc, 640), BF16),
                   jax.ShapeDtypeStruct((DEPTH, nb, W_B, tc), BF16)],
        compiler_params=_cparams(2), name="mla_ctx_kv",
    )(cache_ckv, cache_kpe_t, wukv)


def _pre_kernel(*refs, rope, ctx_out, alias_in, bpb, seq):
    it = iter(refs)
    (x_ref, mod_ref, gpre_ref, w_ref, wg_ref, bg_ref, gq_ref, gkv_ref, wuq_ref,
     wukv_ref) = (next(it) for _ in range(10))
    if rope:
        cos_ref, sin_ref = next(it), next(it)
    for _ in range(alias_in):
        next(it)
    (qk_ref, v_ref, la_ref, bs_ref, ga_ref, gb_ref, gc_ref, mq_ref, mk_ref, mvt_ref, dqk_ref,
     dvt_ref) = (next(it) for _ in range(12))
    if ctx_out:
        ckvn_ref, kpe_ref, kc_ref, vc_ref = (next(it) for _ in range(4))

    d = D_MODEL
    shift = mod_ref[0, :, 0:d]
    scale = mod_ref[0, :, d:2 * d]
    h = (_rms(x_ref[...], gpre_ref[0]) * (1.0 + scale) + shift).astype(BF16)
    proj = lambda r: _dot_nt(h, w_ref[0, r[0]:r[1], :])
    if rope:
        cos = cos_ref[...]
        sin = sin_ref[...]

    pg = proj(R_QKV)
    qk_ref[:, 0:128] = pg[:, 0:128] * (DK_A ** -0.5)
    qk_ref[:, 128:256] = pg[:, 128:256]
    v_ref[...] = pg[:, 256:512]
    pm = proj(R_MLA)
    tail = pm[:, 384:512]
    xg = _dot(tail.astype(BF16), wg_ref[0]) + bg_ref[0]
    la = _log_sigmoid(xg) * (1.0 / GLA_TAU)
    la_ref[...] = la
    bs_ref[:, 0:128] = _chunk_scan(la[:, 0:128], False)
    bs_ref[:, 128:256] = _chunk_scan(la[:, 128:256], True)
    ga_ref[...] = _silu(proj(R_GG)).astype(BF16)

    qall = _dot(_rms(pm[:, 0:256], gq_ref[0]).astype(BF16), wuq_ref[0])
    q_pe = qall[:, 512:768]
    if rope:
        q_pe = _rope(q_pe, cos, sin)
    sb = (DN_B + DR_B) ** -0.5 * LOG2E
    mq_ref[:, 0:512] = (qall[:, 0:512] * sb).astype(BF16)
    mq_ref[:, 512:768] = (q_pe * sb).astype(BF16)
    ckvn = _rms(pm[:, 256:384], gkv_ref[0])
    kvall = _dot(ckvn.astype(BF16), wukv_ref[0])
    lane = lax.broadcasted_iota(jnp.int32, tail.shape, 1)
    kpe4 = jnp.where(lane < DR_B, tail, 0.0)
    kpe4 = kpe4 + pltpu.roll(kpe4, DR_B, 1)
    kpe4 = kpe4 + pltpu.roll(kpe4, 2 * DR_B, 1)
    if rope:
        kpe4 = _rope(kpe4, cos[:, 0:128], sin[:, 0:128])
    mk_ref[:, 0:512] = kvall[:, 0:512].astype(BF16)
    mk_ref[:, 512:640] = kpe4.astype(BF16)
    mvt_ref[...] = kvall[:, 512:1024].T.astype(BF16)
    gb_ref[...] = _silu(proj(R_MG)).astype(BF16)

    pd = proj(R_DIFF)
    dq, dk, dv = pd[:, 0:256], pd[:, 256:512], pd[:, 512:768]
    if rope:
        dq = _rope(dq, cos, sin)
        dk = _rope(dk, cos, sin)
    dqk_ref[:, 0:256] = (dq * (DC ** -0.5 * LOG2E)).astype(BF16)
    dqk_ref[:, 256:512] = dk.astype(BF16)
    dv_t = dv.T
    dvt_ref[...] = dv_t.astype(BF16)
    gc_ref[...] = _silu(pd[:, 768:1024]).astype(BF16)
    if ctx_out:
        kpe_t = kpe4.T
        dk_t = dk.T
        for ref in () if alias_in else (ckvn_ref, kpe_ref, kc_ref, vc_ref):
            ref[:, 1:] = jnp.zeros(ref[:, 1:].shape, F32)
        for bb in range(bpb):
            rs = slice(bb * seq, (bb + 1) * seq)
            ckvn_ref[bb, 0] = ckvn[rs]
            kpe_ref[bb, 0] = kpe_t[0:DR_B, rs]
            kc_ref[bb, 0] = dk_t[:, rs].reshape(H_C, 2 * DC, seq)
            vc_ref[bb, 0] = dv_t[:, rs].reshape(H_C, 2 * DC, seq)


def _pre_call(x2d, mod, pw, layer, *, seq, rope_tabs, caches):
    n = x2d.shape[0]
    tm = min(TOKEN_BLOCK, n)
    bpb = max(tm // seq, 1)
    rope = rope_tabs is not None
    ctx_out = caches is not None
    steps_per_seq = max(seq // tm, 1)
    nbt = n // seq

    def mod_idx(i):
        return (layer * MOD_ROWS + ((i * tm) // seq + 1 if rope else 0), 0, 0)

    names = ["g_pre", "w_t", "w_gate", "b_gate", "g_mla_q", "g_mla_kv", "w_uq", "w_ukv"]
    in_specs = [pl.BlockSpec((tm, D_MODEL), lambda i: (i, 0)), pl.BlockSpec((1, 1, 3 * D_MODEL), mod_idx)]
    in_specs += [_layer_spec(pw[k].shape, layer) for k in names]
    args = [x2d, mod] + [pw[k] for k in names]
    if rope:
        in_specs += [pl.BlockSpec((tm, 256), lambda i: (i % steps_per_seq, 0))] * 2
        args += list(rope_tabs)
    outs = [(256, F32, False), (256, F32, False), (256, F32, False), (256, F32, False),
            (W_A, BF16, False), (W_B, BF16, False), (W_C, BF16, False),
            (768, BF16, False), (640, BF16, False), (W_B, BF16, True),
            (512, BF16, False), (W_C, BF16, True)]
    out_specs = [pl.BlockSpec((w, tm), lambda i: (0, i)) if tr else pl.BlockSpec((tm, w), lambda i: (i, 0))
                 for w, _, tr in outs]
    out_shape = [jax.ShapeDtypeStruct((w, n) if tr else (n, w), dt) for w, dt, tr in outs]
    widths = outs
    aliases = {}
    if ctx_out:
        assert caches or layer == 0
        nl = 1 if caches else DEPTH
        out_specs += [pl.BlockSpec((bpb, nl, seq, KV_LORA), lambda i: (i, layer, 0, 0)),
                      pl.BlockSpec((bpb, nl, DR_B, seq), lambda i: (i, layer, 0, 0)),
                      pl.BlockSpec((bpb, nl, H_C, 2 * DC, seq), lambda i: (i, layer, 0, 0, 0)),
                      pl.BlockSpec((bpb, nl, H_C, 2 * DC, seq), lambda i: (i, layer, 0, 0, 0))]
        out_shape += [jax.ShapeDtypeStruct((nbt, DEPTH, seq, KV_LORA), F32),
                      jax.ShapeDtypeStruct((nbt, DEPTH, DR_B, seq), F32),
                      jax.ShapeDtypeStruct((nbt, DEPTH, H_C, 2 * DC, seq), F32),
                      jax.ShapeDtypeStruct((nbt, DEPTH, H_C, 2 * DC, seq), F32)]
        for j, arr in enumerate(caches):
            aliases[len(args)] = len(widths) + j
            in_specs.append(_ANY)
            args.append(arr)
    return pl.pallas_call(
        functools.partial(_pre_kernel, rope=rope, ctx_out=ctx_out, alias_in=len(aliases), bpb=bpb, seq=seq),
        grid=(n // tm,), in_specs=in_specs, out_specs=out_specs, out_shape=out_shape,
        input_output_aliases=aliases,
        compiler_params=_cparams(), name="pre_rope" if rope else "pre_ctx",
    )(*args)


_GLA_LEVELS = (1, 2, 4, 8, 16, 32)


def _gla_consts(rev):
    c = GLA_CHUNK
    row = lax.broadcasted_iota(jnp.int32, (c, 128), 0)
    pos = (c - 1 - row) if rev else row
    ri = lax.broadcasted_iota(jnp.int32, (c, H_A * c), 0)
    cj = lax.broadcasted_iota(jnp.int32, (c, H_A * c), 1) & (c - 1)
    pi = (c - 1 - ri) if rev else ri
    pj = (c - 1 - cj) if rev else cj
    x = pi ^ pj
    lvl = jnp.where(pi == pj, 0, -1)
    for kbit in range(6):
        lvl = jnp.where((pj < pi) & ((x >> kbit) == 1), kbit + 1, lvl)
    return pos, lvl


def _chunk_scan(x, rev):
    rows = x.shape[0]
    nt = rows // SUBLANES
    tiles_per_chunk = GLA_CHUNK // SUBLANES
    x3 = x.reshape(nt, SUBLANES, 128)
    sub = lax.broadcasted_iota(jnp.int32, x3.shape, 1)
    tile = lax.broadcasted_iota(jnp.int32, x3.shape, 0) & (tiles_per_chunk - 1)
    edge = 0 if rev else SUBLANES - 1
    s = 1
    while s < SUBLANES:
        if rev:
            x3 = x3 + jnp.where(sub < SUBLANES - s, pltpu.roll(x3, SUBLANES - s, 1), 0.0)
        else:
            x3 = x3 + jnp.where(sub >= s, pltpu.roll(x3, s, 1), 0.0)
        s *= 2
    s = 1
    while s < tiles_per_chunk:
        tot = jnp.broadcast_to(x3[:, edge:edge + 1, :], x3.shape)
        if rev:
            shifted = jnp.concatenate([tot[s:], tot[:s]], axis=0)
            x3 = x3 + jnp.where(tile < tiles_per_chunk - s, shifted, 0.0)
        else:
            shifted = jnp.concatenate([tot[nt - s:], tot[:nt - s]], axis=0)
            x3 = x3 + jnp.where(tile >= s, shifted, 0.0)
        s *= 2
    return x3.reshape(rows, 128)


def _gla_scores(q, k, la, b, pos, lvl, hm_bf, rev):
    c = GLA_CHUNK
    prv = pltpu.roll(la, c - 1 if rev else 1, 0)
    nxt = pltpu.roll(la, 1 if rev else c - 1, 0)
    s_tot = jnp.where(lvl == 0, _dot_nt(q.astype(BF16), jnp.concatenate([k.astype(BF16)] * H_A, 0) * hm_bf), 0.0)
    for kbit, m in enumerate(_GLA_LEVELS):
        up = ((pos >> kbit) & 1) == 1
        if m == 1:
            e = jnp.where(up, la, 0.0)
        elif m == 2:
            c4 = pos & 3
            e = jnp.where(c4 == 0, nxt, jnp.where(c4 == 1, 0.0, jnp.where(c4 == 2, la, la + prv)))
        else:
            nblk = c // (2 * m)
            loc = m if rev else m - 1
            b3 = b.reshape(nblk, 2 * m, 128)
            ref = jnp.broadcast_to(b3[:, loc:loc + 1, :], (nblk, 2 * m, 128)).reshape(c, 128)
            dlt = b - ref
            e = jnp.where(up, dlt, -dlt)
        xm = (jnp.where(up, q, k) * jnp.exp(e)).astype(BF16)
        sm = _dot_nt(xm, jnp.concatenate([xm] * H_A, 0) * hm_bf)
        s_tot = jnp.where(lvl == kbit + 1, sm, s_tot)
    return s_tot


def _gla_apply(s_tot, q, k, v, b, st_prev, hm_f32, vm_bf, rev):
    c = GLA_CHUNK
    vbd = jnp.concatenate([v] * H_A, 0) * vm_bf
    blast = b[0:1, :] if rev else b[c - 1:c, :]
    qbar = (q * jnp.exp(b)).astype(BF16)
    kdec = (k * jnp.exp(blast - b)).astype(BF16)
    o = _dot(s_tot.astype(BF16), vbd) + _dot_nt(qbar, st_prev.astype(BF16))
    st_new = st_prev * jnp.exp(blast) + _dot_tn(v, kdec) * hm_f32
    return o, st_new


def _gla_kernel(*refs, seq, has_s0, alias_in, spb):
    it = iter(refs)
    qk_ref, v_ref, la_ref, b_ref, gate_ref, g_ref = (next(it) for _ in range(6))
    s0_ref = next(it) if has_s0 else None
    for _ in range(alias_in):
        next(it)
    oa_ref = next(it)
    sfin_ref = None if has_s0 else next(it)
    acc_sc, st_sc, blk_sc = next(it), next(it), next(it)

    c = GLA_CHUNK
    nc = seq // c
    acc_sc[...] = jnp.zeros_like(acc_sc)
    if not has_s0 and not alias_in:
        sfin_ref[:, 1:] = jnp.zeros(sfin_ref[:, 1:].shape, F32)

    hrow = lax.broadcasted_iota(jnp.int32, (H_A * c, 128), 0) // c
    hm_f32 = jnp.where(hrow == lax.broadcasted_iota(jnp.int32, (H_A * c, 128), 1) // DK_A, 1.0, 0.0)
    hm_bf = hm_f32.astype(BF16)
    vrow = lax.broadcasted_iota(jnp.int32, (H_A * c, H_A * DV_A), 0) // c
    vm_bf = jnp.where(vrow == lax.broadcasted_iota(jnp.int32, (H_A * c, H_A * DV_A), 1) // DV_A,
                      1.0, 0.0).astype(BF16)
    consts = (_gla_consts(False), _gla_consts(True))

    def run_sequence(bb):
        base = bb * seq
        for d in range(2):
            if has_s0:
                blk_sc[...] = jnp.zeros_like(blk_sc)
                for hh in range(H_A):
                    blk_sc[DK_A * hh:DK_A * (hh + 1), DV_A * hh:DV_A * (hh + 1)] = s0_ref[bb, 0, d, hh]
                st_sc[d] = blk_sc[...].T
            else:
                st_sc[d] = jnp.zeros((ST_R, ST_C), F32)

        def chunk_rows(n, d):
            cn = (nc - 1 - n) if d else n
            start = base + cn * c
            return pl.ds(start if isinstance(start, int) else pl.multiple_of(start, c), c)

        def scores(n):
            out = []
            for d in range(2):
                rows = chunk_rows(n, d)
                pos, lvl = consts[d]
                out.append(_gla_scores(qk_ref[rows, 0:128], qk_ref[rows, 128:256],
                                       la_ref[rows, 128 * d:128 * d + 128],
                                       b_ref[rows, 128 * d:128 * d + 128], pos, lvl, hm_bf, bool(d)))
            return tuple(out)

        def apply(n, s_both):
            for d in range(2):
                rows = chunk_rows(n, d)
                o, st_new = _gla_apply(s_both[d], qk_ref[rows, 0:128], qk_ref[rows, 128:256],
                                       v_ref[rows, :].astype(BF16), b_ref[rows, 128 * d:128 * d + 128],
                                       st_sc[d], hm_f32, vm_bf, bool(d))
                acc_sc[rows, :] = acc_sc[rows, :] + o
                st_sc[d] = st_new

        def body(n, s_cur):
            s_next = scores(n + 1)
            apply(n, s_cur)
            return s_next

        s_last = lax.fori_loop(0, nc - 1, body, scores(0), unroll=3 if (nc - 1) % 3 == 0 else 1)
        apply(nc - 1, s_last)

        rs = slice(base, base + seq)
        first = lax.broadcasted_iota(jnp.int32, (seq, 128), 1) < DV_A
        for p in range(H_A // 2):
            cols = slice(128 * p, 128 * (p + 1))
            oa = acc_sc[rs, cols]
            sq = oa * oa
            s0 = jnp.sum(jnp.where(first, sq, 0.0), axis=-1, keepdims=True)
            s1 = jnp.sum(jnp.where(first, 0.0, sq), axis=-1, keepdims=True)
            inv = jnp.where(first, lax.rsqrt(s0 * (1.0 / DV_A) + EPS), lax.rsqrt(s1 * (1.0 / DV_A) + EPS))
            oa_ref[rs, cols] = (oa * inv * g_ref[0, :, cols] * gate_ref[rs, cols]).astype(BF16)
        if not has_s0:
            for d in range(2):
                blk_sc[...] = st_sc[d].T
                for hh in range(H_A):
                    sfin_ref[bb, 0, d, hh] = blk_sc[DK_A * hh:DK_A * (hh + 1), DV_A * hh:DV_A * (hh + 1)]

    for bb in range(spb):
        run_sequence(bb)


def _gla_call(qk, v, la, bsum, gate, g4, layer, *, seq, state_in=None, state_out=None):
    n = qk.shape[0]
    nb = n // seq
    has_s0 = state_in is not None
    assert has_s0 or state_out is not None or layer == 0
    spb = max(1, min(nb, GLA_GROUP_ROWS // seq))
    blk = lambda w: pl.BlockSpec((spb * seq, w), lambda i: (i, 0))
    nl = DEPTH if (not has_s0 and state_out is None) else 1
    st_spec = pl.BlockSpec((spb, nl, 2, H_A, DK_A, DV_A), lambda i: (i, layer, 0, 0, 0, 0))
    in_specs = [blk(256), blk(256), blk(256), blk(256), blk(W_A), _layer_spec(g4.shape, layer)]
    args = [qk, v, la, bsum, gate, g4]
    out_specs = [blk(W_A)]
    out_shape = [jax.ShapeDtypeStruct((n, W_A), BF16)]
    aliases = {}
    if has_s0:
        in_specs.append(st_spec)
        args.append(state_in)
    else:
        out_specs.append(st_spec)
        out_shape.append(jax.ShapeDtypeStruct((nb, DEPTH, 2, H_A, DK_A, DV_A), F32))
        if state_out is not None:
            aliases[len(args)] = 1
            in_specs.append(_ANY)
            args.append(state_out)
    return pl.pallas_call(
        functools.partial(_gla_kernel, seq=seq, has_s0=has_s0, alias_in=len(aliases), spb=spb),
        grid=(nb // spb,), in_specs=in_specs, out_specs=out_specs, out_shape=out_shape,
        input_output_aliases=aliases,
        scratch_shapes=[pltpu.VMEM((spb * seq, W_A), F32), pltpu.VMEM((2, ST_R, ST_C), F32),
                        pltpu.VMEM((ST_C, ST_R), F32)],
        compiler_params=_cparams(), name="gla_state" if has_s0 else "gla_ctx",
    )(*args)


def _softmax_t_pv(st, vt):
    m = _col_reduce(st, jnp.max)
    e = jnp.exp2(st - m)
    l = _col_reduce(e, jnp.sum)
    return _dot(vt, e.astype(BF16)) * (1.0 / l)


def _pipelined_attention(score_fns, value_fns, depth):
    outs = []
    pending = [fn() for fn in score_fns[:depth]]
    for j, vfn in enumerate(value_fns):
        st = pending.pop(0)
        if j + depth < len(score_fns):
            pending.append(score_fns[j + depth]())
        outs.append(_softmax_t_pv(st, vfn()))
    return outs


def _lookahead(n_keys):
    return 2 if n_keys >= 1024 else 3


def _col_reduce(x, op):
    rows, cols = x.shape
    part = 128 if rows % 128 == 0 and rows > 128 else rows
    if part != rows:
        x = op(x.reshape(rows // part, part, cols), axis=0)
    return op(x, axis=0, keepdims=True)


def _seqs_per_step(n, seq, ctx_len):
    if ctx_len or seq >= ATT_QBLOCK:
        return 1
    return max(1, min(n // seq, ATT_GROUP_ROWS // seq))


def _for_query_blocks(run, seq, spb):
    qb = min(ATT_QBLOCK, seq)
    nq = seq // qb
    if spb * nq <= ATT_STATIC_BLOCKS:
        run([(bb, slice(bb * seq + j * qb, bb * seq + (j + 1) * qb)) for bb in range(spb) for j in range(nq)])
    else:
        assert spb == 1

        def body(i, carry):
            run([(0, pl.ds(pl.multiple_of(i * qb, qb), qb))])
            return carry
        lax.fori_loop(0, nq, body, 0)


def _mla_kernel(*refs, seq, ctx_len, spb):
    it = iter(refs)
    q_ref, k_ref, vt_ref, gate_ref = (next(it) for _ in range(4))
    ck_ref, cvt_ref = (next(it), next(it)) if ctx_len else (None, None)
    ob_ref = next(it)
    kk_sc, vt_sc = next(it), next(it)

    for bb in range(spb):
        rs = slice(bb * seq, (bb + 1) * seq)
        for p in range(H_B // 2):
            kk_sc[bb, p, 0:seq, 0:128] = k_ref[rs, 128 * p:128 * p + 128]
            kk_sc[bb, p, 0:seq, 128:256] = k_ref[rs, 512:640]
            if ctx_len:
                kk_sc[bb, p, seq:seq + ctx_len, 0:128] = ck_ref[0, 0, :, 128 * p:128 * p + 128]
                kk_sc[bb, p, seq:seq + ctx_len, 128:256] = ck_ref[0, 0, :, 512:640]
        vt_sc[bb, :, 0:seq] = vt_ref[:, rs]
        if ctx_len:
            vt_sc[bb, :, seq:seq + ctx_len] = cvt_ref[0, 0]

    def scores(bb, rows, h):
        p, hh = divmod(h, 2)
        qn = q_ref[rows, 128 * p:128 * p + 128] * _lane_mask(128, DN_B * hh, DN_B, BF16)
        qp = (q_ref[rows, 512 + 128 * (h // 4):512 + 128 * (h // 4) + 128]
              * _lane_mask(128, DR_B * (h % 4), DR_B, BF16))
        return _dot_nt(kk_sc[bb, p], jnp.concatenate([qn, qp], axis=-1))

    def run(blocks):
        items = [(bb, rows, h) for bb, rows in blocks for h in range(H_B)]
        outs = _pipelined_attention(
            [functools.partial(scores, bb, rows, h) for bb, rows, h in items],
            [functools.partial(lambda bb, h: vt_sc[bb, DV_B * h:DV_B * (h + 1), :], bb, h) for bb, _, h in items],
            _lookahead(seq + ctx_len))
        for j, (_, rows) in enumerate(blocks):
            ob = jnp.concatenate(outs[H_B * j:H_B * (j + 1)], axis=0).T
            ob_ref[rows, :] = (ob * gate_ref[rows, :]).astype(BF16)

    _for_query_blocks(run, seq, spb)


def _mla_call(mq, mk, mvt, gate, ctx_k, ctx_vt, layer, *, seq):
    n = mq.shape[0]
    ctx_len = 0 if ctx_k is None else ctx_k.shape[2]
    spb = _seqs_per_step(n, seq, ctx_len)
    rows = spb * seq
    in_specs = [pl.BlockSpec((rows, 768), lambda i: (i, 0)), pl.BlockSpec((rows, 640), lambda i: (i, 0)),
                pl.BlockSpec((W_B, rows), lambda i: (0, i)), pl.BlockSpec((rows, W_B), lambda i: (i, 0))]
    args = [mq, mk, mvt, gate]
    if ctx_len:
        in_specs += [pl.BlockSpec((1, 1, ctx_len, 640), lambda i: (layer, i, 0, 0)),
                     pl.BlockSpec((1, 1, W_B, ctx_len), lambda i: (layer, i, 0, 0))]
        args += [ctx_k, ctx_vt]
    tk = seq + ctx_len
    return pl.pallas_call(
        functools.partial(_mla_kernel, seq=seq, ctx_len=ctx_len, spb=spb),
        grid=(n // rows,), in_specs=in_specs,
        out_specs=pl.BlockSpec((rows, W_B), lambda i: (i, 0)),
        out_shape=jax.ShapeDtypeStruct((n, W_B), BF16),
        scratch_shapes=[pltpu.VMEM((spb, H_B // 2, tk, 256), BF16), pltpu.VMEM((spb, W_B, tk), BF16)],
        compiler_params=_cparams(), name="mla_ctx" if ctx_len else "mla_self",
    )(*args)


def _diff_kernel(*refs, seq, ctx_len, lam_init, spb):
    it = iter(refs)
    qk_ref, vt_ref, gate_ref, lam_ref, g_ref = (next(it) for _ in range(5))
    ck_ref, cv_ref = (next(it), next(it)) if ctx_len else (None, None)
    oc_ref = next(it)
    k_sc, vt_sc = next(it), next(it)

    lam = (jnp.exp(jnp.sum(lam_ref[0, 0:1, :] * lam_ref[0, 1:2, :], axis=-1, keepdims=True))
           - jnp.exp(jnp.sum(lam_ref[0, 2:3, :] * lam_ref[0, 3:4, :], axis=-1, keepdims=True)) + lam_init)
    dh = 2 * DC
    for bb in range(spb):
        rs = slice(bb * seq, (bb + 1) * seq)
        for p in range(H_C // 2):
            k_sc[bb, p, 0:seq, :] = qk_ref[rs, 256 + 128 * p:256 + 128 * p + 128]
            if ctx_len:
                pair_t = jnp.concatenate([ck_ref[0, 0, 2 * p], ck_ref[0, 0, 2 * p + 1]], axis=0)
                k_sc[bb, p, seq:seq + ctx_len, :] = pair_t.T.astype(BF16)
        vt_sc[bb, :, 0:seq] = vt_ref[:, rs]
        if ctx_len:
            for h in range(H_C):
                vt_sc[bb, dh * h:dh * (h + 1), seq:seq + ctx_len] = cv_ref[0, 0, h].astype(BF16)

    def scores(bb, rows, h, comp):
        p, hh = divmod(h, 2)
        qm = qk_ref[rows, 128 * p:128 * p + 128] * _lane_mask(128, dh * hh + DC * comp, DC, BF16)
        return _dot_nt(k_sc[bb, p], qm)

    def run(blocks):
        items = [(bb, rows, h, comp) for bb, rows in blocks for h in range(H_C) for comp in range(2)]
        o12 = _pipelined_attention(
            [functools.partial(scores, *item) for item in items],
            [functools.partial(lambda bb, h: vt_sc[bb, dh * h:dh * (h + 1), :], bb, h) for bb, _, h, _ in items],
            _lookahead(seq + ctx_len))
        for j, (_, rows) in enumerate(blocks):
            outs = []
            for h in range(H_C):
                o1, o2 = o12[2 * (H_C * j + h)], o12[2 * (H_C * j + h) + 1]
                ot = o1 - lam * o2
                outs.append(ot * lax.rsqrt(jnp.mean(ot * ot, axis=0, keepdims=True) + EPS))
            oc = jnp.concatenate(outs, axis=0).T
            oc_ref[rows, :] = (oc * g_ref[0] * (1.0 - lam_init) * gate_ref[rows, :]).astype(BF16)

    _for_query_blocks(run, seq, spb)


def _diff_call(dqk, dvt, gate, lamp, g4, ck_t, cv_t, layer, *, seq):
    n = dqk.shape[0]
    ctx_len = 0 if ck_t is None else ck_t.shape[4]
    lam_init = 0.8 - 0.6 * math.exp(-0.3 * layer)
    spb = _seqs_per_step(n, seq, ctx_len)
    rows = spb * seq
    in_specs = [pl.BlockSpec((rows, 512), lambda i: (i, 0)), pl.BlockSpec((W_C, rows), lambda i: (0, i)),
                pl.BlockSpec((rows, W_C), lambda i: (i, 0)),
                _layer_spec(lamp.shape, layer), _layer_spec(g4.shape, layer)]
    args = [dqk, dvt, gate, lamp, g4]
    if ctx_len:
        in_specs += [pl.BlockSpec((1, 1, H_C, 2 * DC, ctx_len), lambda i: (i, layer, 0, 0, 0))] * 2
        args += [ck_t, cv_t]
    tk = seq + ctx_len
    return pl.pallas_call(
        functools.partial(_diff_kernel, seq=seq, ctx_len=ctx_len, lam_init=lam_init, spb=spb),
        grid=(n // rows,), in_specs=in_specs,
        out_specs=pl.BlockSpec((rows, W_C), lambda i: (i, 0)),
        out_shape=jax.ShapeDtypeStruct((n, W_C), BF16),
        scratch_shapes=[pltpu.VMEM((spb, H_C // 2, tk, 128), BF16), pltpu.VMEM((spb, W_C, tk), BF16)],
        compiler_params=_cparams(), name="diff_ctx" if ctx_len else "diff_self",
    )(*args)


def _post_kernel(oa_ref, ob_ref, oc_ref, x_ref, mod_ref, w_ref, g_ref, y_ref):
    mix = jnp.concatenate([oa_ref[...], ob_ref[...], oc_ref[...]], axis=-1)
    out = _rms(_dot(mix, w_ref[0]), g_ref[0])
    y_ref[...] = x_ref[...] + mod_ref[0, :, 2 * D_MODEL:3 * D_MODEL] * out


def _post_call(oa, ob, oc, x2d, mod, pw, layer, *, seq, sample):
    n = x2d.shape[0]
    tm = min(POST_BLOCK, seq if sample else n)
    blk = lambda w: pl.BlockSpec((tm, w), lambda i: (i, 0))

    def mod_idx(i):
        return (layer * MOD_ROWS + ((i * tm) // seq + 1 if sample else 0), 0, 0)

    return pl.pallas_call(
        _post_kernel,
        grid=(n // tm,),
        in_specs=[blk(W_A), blk(W_B), blk(W_C), blk(D_MODEL),
                  pl.BlockSpec((1, 1, 3 * D_MODEL), mod_idx),
                  _layer_spec(pw["w_out"].shape, layer), _layer_spec(pw["g_post"].shape, layer)],
        out_specs=blk(D_MODEL),
        out_shape=jax.ShapeDtypeStruct((n, D_MODEL), F32),
        compiler_params=_cparams(), name="post",
    )(oa, ob, oc, x2d, mod, pw["w_out"], pw["g_post"])


def _pack_params(g_pre, g_post, w_in, w_gla_af, b_gla_af, w_gla_ab, b_gla_ab, g_gla, g_mla_q, w_mla_uq,
                 g_mla_kv, w_mla_ukv, lam_q1, lam_k1, lam_q2, lam_k2, g_diff, w_out):
    w_t = jnp.swapaxes(w_in, 1, 2)
    ga0, ga1 = _IN["ga"], _IN["gg"]
    w_t = jnp.concatenate([w_t[:, 0:ga0], w_t[:, ga1:_IN["mg"]], w_t[:, ga0:ga1],
                           jnp.zeros((DEPTH, 64, D_MODEL), F32), w_t[:, _IN["mg"]:]], axis=1)
    assert w_t.shape[1] == R_DIFF[1]
    zg = jnp.zeros((DEPTH, GLA_LR, 128), F32)
    w_gate = jnp.concatenate([jnp.zeros((DEPTH, DR_B, 256), F32),
                              jnp.concatenate([w_gla_af, zg], axis=-1),
                              jnp.concatenate([zg, w_gla_ab], axis=-1),
                              jnp.zeros((DEPTH, 128 - DR_B - 2 * GLA_LR, 256), F32)], axis=1).astype(BF16)
    uq = w_mla_uq.reshape(DEPTH, Q_LORA, H_B, DN_B + DR_B)
    w_pe = uq[..., DN_B:].reshape(DEPTH, Q_LORA, H_B * DR_B)
    w_uq = jnp.concatenate([uq[..., :DN_B].reshape(DEPTH, Q_LORA, H_B * DN_B), w_pe], axis=-1).astype(BF16)
    ukv = w_mla_ukv.reshape(DEPTH, KV_LORA, H_B, DN_B + DV_B)
    w_ukv = jnp.concatenate([ukv[..., :DN_B].reshape(DEPTH, KV_LORA, H_B * DN_B),
                             ukv[..., DN_B:].reshape(DEPTH, KV_LORA, H_B * DV_B)], axis=-1).astype(BF16)
    row = lambda a: a.reshape(DEPTH, 1, a.shape[-1])
    return dict(
        w_t=w_t.astype(BF16), w_gate=w_gate,
        b_gate=row(jnp.concatenate([b_gla_af, b_gla_ab], axis=-1)),
        w_uq=w_uq, w_ukv=w_ukv, w_out=w_out.astype(BF16),
        g_pre=row(g_pre), g_post=row(g_post), g_mla_q=row(g_mla_q), g_mla_kv=row(g_mla_kv),
        g_gla4=row(jnp.tile(g_gla, (1, H_A))), g_diff4=row(jnp.tile(g_diff, (1, H_C))),
        lam=jnp.stack([lam_q1, lam_k1, lam_q2, lam_k2], axis=1))


def _rope_tables(n):
    t = np.arange(n)
    row = (t // GRID_W).astype(np.float32)
    col = (t % GRID_W).astype(np.float32)
    half = ROPE_DIM // 2
    inv = (1.0 / (np.float32(ROPE_THETA) ** (np.arange(0, half, 2, dtype=np.float32) / np.float32(half)))
           ).astype(np.float32)
    ar = row[:, None] * inv
    ac = col[:, None] * inv
    ang = np.concatenate([ar, ar, ac, ac], axis=-1).astype(np.float32)
    return (jnp.asarray(np.tile(np.cos(ang), (1, 8)).astype(np.float32)),
            jnp.asarray(np.tile(np.sin(ang), (1, 8)).astype(np.float32)))


def _sublayer(x2d, mod, pw, layer, *, seq, rope_tabs, ctx, caches):
    sample = ctx is not None
    pre = _pre_call(x2d, mod, pw, layer, seq=seq, rope_tabs=rope_tabs,
                    caches=None if sample else caches[:4])
    qk, v, la, bsum, ga, gb, gc, mq, mk, mvt, dqk, dvt = pre[:12]
    if sample:
        (oa,) = _gla_call(qk, v, la, bsum, ga, pw["g_gla4"], layer, seq=seq, state_in=ctx["state"])
        ob = _mla_call(mq, mk, mvt, gb, ctx["mla_k"], ctx["mla_vt"], layer, seq=seq)
        oc = _diff_call(dqk, dvt, gc, pw["lam"], pw["g_diff4"], ctx["diff_k_t"], ctx["diff_v_t"], layer,
                        seq=seq)
        new_caches = None
    else:
        oa, sfin = _gla_call(qk, v, la, bsum, ga, pw["g_gla4"], layer, seq=seq,
                             state_out=caches[4] if caches else None)
        ob = _mla_call(mq, mk, mvt, gb, None, None, layer, seq=seq)
        oc = _diff_call(dqk, dvt, gc, pw["lam"], pw["g_diff4"], None, None, layer, seq=seq)
        new_caches = tuple(pre[12:]) + (sfin,)
    y = _post_call(oa, ob, oc, x2d, mod, pw, layer, seq=seq, sample=sample)
    return y, new_caches


def kernel(x_prompt, x_sample, c, cache_mla_ckv, cache_mla_kpe, cache_diff_k, cache_diff_v, state_gla,
           c_ctx, w_ada, b_ada, g_pre, g_post, w_in, w_gla_af, b_gla_af, w_gla_ab, b_gla_ab, g_gla,
           g_mla_q, w_mla_uq, g_mla_kv, w_mla_ukv, lam_q1, lam_k1, lam_q2, lam_k2, g_diff, w_out):
    bp, tp, d = x_prompt.shape
    bs, ts, _ = x_sample.shape

    pw = _pack_params(g_pre, g_post, w_in, w_gla_af, b_gla_af, w_gla_ab, b_gla_ab, g_gla, g_mla_q,
                      w_mla_uq, g_mla_kv, w_mla_ukv, lam_q1, lam_k1, lam_q2, lam_k2, g_diff, w_out)
    cvecs = jnp.concatenate([c_ctx[None], c, jnp.zeros((MOD_ROWS - 1 - bs, d), F32)], axis=0)
    mod = _mod_call(cvecs, w_ada, b_ada).reshape(DEPTH * MOD_ROWS, 1, 3 * d)
    rope_tabs = _rope_tables(ts)
    ctx_k, ctx_vt = _ctxkv_call(cache_mla_ckv, jnp.swapaxes(cache_mla_kpe, -1, -2), pw["w_ukv"])
    ctx = dict(state=state_gla, mla_k=ctx_k, mla_vt=ctx_vt,
               diff_k_t=jnp.swapaxes(cache_diff_k, -1, -2), diff_v_t=jnp.swapaxes(cache_diff_v, -1, -2))

    y_p = x_prompt.reshape(bp * tp, d)
    y_s = x_sample.reshape(bs * ts, d)
    caches = ()
    for l in range(DEPTH):
        y_p, caches = _sublayer(y_p, mod, pw, l, seq=tp, rope_tabs=None, ctx=None, caches=caches)
        y_s, _ = _sublayer(y_s, mod, pw, l, seq=ts, rope_tabs=rope_tabs, ctx=ctx, caches=None)
    ckvn, kpe_t, kc_t, vc_t, new_state = caches
    return (y_p.reshape(bp, tp, d), y_s.reshape(bs, ts, d), ckvn, jnp.swapaxes(kpe_t, -1, -2),
            jnp.swapaxes(kc_t, -1, -2), jnp.swapaxes(vc_t, -1, -2), new_state)
```

```python
import functools
import math

import numpy as np
import jax
import jax.numpy as jnp
from jax import lax
from jax.experimental import pallas as pl
from jax.experimental.pallas import tpu as pltpu

F32 = jnp.float32
BF16 = jnp.bfloat16

D_MODEL = 1024
DEPTH = 2
GRID_W = 64
EPS = 1e-6
ROPE_THETA = 10000.0
ROPE_DIM = 32
H_A, DK_A, DV_A = 4, 32, 64
GLA_LR = 16
GLA_TAU = 16.0
GLA_CHUNK = 64
H_B, DN_B, DR_B, DV_B = 8, 64, 32, 64
Q_LORA, KV_LORA = 256, 128
H_C, DC = 4, 32
W_A, W_B, W_C = H_A * DV_A, H_B * DV_B, H_C * 2 * DC
ST_R, ST_C = H_A * DV_A, H_A * DK_A
LOG2E = math.log2(math.e)
SUBLANES = 8

_IN = dict(ga=512, gg=544, mg=1216)
R_QKV = (0, 512)
R_GG = (512, 768)
R_MLA = (768, 1280)
R_MG = (1280, 1792)
R_DIFF = (1792, 2816)
MOD_ROWS = 8

V7X_VMEM_LIMIT_BYTES = 56 * 1024 * 1024
TOKEN_BLOCK = 1024
POST_BLOCK = 1024
ATT_QBLOCK = 512
ATT_GROUP_ROWS = 1024
GLA_GROUP_ROWS = 1024
ATT_STATIC_BLOCKS = 4


def _cparams(n_axes=1):
    return pltpu.CompilerParams(dimension_semantics=("arbitrary",) * n_axes,
                                vmem_limit_bytes=V7X_VMEM_LIMIT_BYTES)


def _rms(x, g):
    return x * lax.rsqrt(jnp.mean(x * x, axis=-1, keepdims=True) + EPS) * g


def _silu(x):
    return x * jax.nn.sigmoid(x)


def _log_sigmoid(x):
    return jnp.minimum(x, 0.0) - jnp.log1p(jnp.exp(-jnp.abs(x)))


def _rope(z, cos, sin):
    w = z.shape[-1]
    lane = lax.broadcasted_iota(jnp.int32, z.shape, 1)
    rot = jnp.where((lane & 15) < 8, -pltpu.roll(z, w - 8, 1), pltpu.roll(z, 8, 1))
    return z * cos + rot * sin


def _lane_mask(width, lo, size, dtype):
    lane = lax.broadcasted_iota(jnp.int32, (1, width), 1)
    return jnp.where((lane >= lo) & (lane < lo + size), 1.0, 0.0).astype(dtype)


def _dot(a, b):
    return jnp.dot(a, b, preferred_element_type=F32)


def _dot_nt(a, b):
    return lax.dot_general(a, b, (((1,), (1,)), ((), ())), preferred_element_type=F32)


def _dot_tn(a, b):
    return lax.dot_general(a, b, (((0,), (0,)), ((), ())), preferred_element_type=F32)


def _layer_spec(shape, layer):
    nd = len(shape)
    return pl.BlockSpec((1,) + tuple(shape[1:]), lambda *_: (layer,) + (0,) * (nd - 1))


_ANY = pl.BlockSpec(memory_space=pl.ANY)


def _mod_kernel(c_ref, w_ref, b_ref, o_ref):
    s = _silu(c_ref[...]).astype(BF16)
    o_ref[0] = _dot(s, w_ref[0].astype(BF16)) + b_ref[0]


def _mod_call(cvecs, w_ada, b_ada):
    nb = 1024
    return pl.pallas_call(
        _mod_kernel,
        grid=(DEPTH, 3 * D_MODEL // nb),
        in_specs=[pl.BlockSpec((MOD_ROWS, D_MODEL), lambda l, j: (0, 0)),
                  pl.BlockSpec((1, D_MODEL, nb), lambda l, j: (l, 0, j)),
                  pl.BlockSpec((1, 1, nb), lambda l, j: (l, 0, j))],
        out_specs=pl.BlockSpec((1, MOD_ROWS, nb), lambda l, j: (l, 0, j)),
        out_shape=jax.ShapeDtypeStruct((DEPTH, MOD_ROWS, 3 * D_MODEL), F32),
        compiler_params=_cparams(2), name="adaln_mod",
    )(cvecs, w_ada, b_ada.reshape(DEPTH, 1, 3 * D_MODEL))


def _ctxkv_kernel(ckv_ref, kpe_ref, w_ref, k_ref, vt_ref):
    kv = _dot(ckv_ref[0, 0].astype(BF16), w_ref[0])
    kpe4 = jnp.concatenate([kpe_ref[0, 0]] * 4, axis=0).T
    k_ref[0, 0, :, 0:512] = kv[:, 0:512].astype(BF16)
    k_ref[0, 0, :, 512:640] = kpe4.astype(BF16)
    vt_ref[0, 0] = kv[:, 512:1024].T.astype(BF16)


def _ctxkv_call(cache_ckv, cache_kpe_t, wukv):
    nb, _, tc, _ = cache_ckv.shape
    return pl.pallas_call(
        _ctxkv_kernel,
        grid=(DEPTH, nb),
        in_specs=[pl.BlockSpec((1, 1, tc, KV_LORA), lambda l, b: (b, l, 0, 0)),
                  pl.BlockSpec((1, 1, DR_B, tc), lambda l, b: (b, l, 0, 0)),
                  pl.BlockSpec((1, KV_LORA, 1024), lambda l, b: (l, 0, 0))],
        out_specs=[pl.BlockSpec((1, 1, tc, 640), lambda l, b: (l, b, 0, 0)),
                   pl.BlockSpec((1, 1, W_B, tc), lambda l, b: (l, b, 0, 0))],
        out_shape=[jax.ShapeDtypeStruct((DEPTH, nb, tc, 640), BF16),
                   jax.ShapeDtypeStruct((DEPTH, nb, W_B, tc), BF16)],
        compiler_params=_cparams(2), name="mla_ctx_kv",
    )(cache_ckv, cache_kpe_t, wukv)


def _pre_kernel(*refs, rope, ctx_out, alias_in, bpb, seq):
    it = iter(refs)
    (x_ref, mod_ref, gpre_ref, w_ref, wg_ref, bg_ref, gq_ref, gkv_ref, wuq_ref,
     wukv_ref) = (next(it) for _ in range(10))
    if rope:
        cos_ref, sin_ref = next(it), next(it)
    for _ in range(alias_in):
        next(it)
    (qk_ref, v_ref, la_ref, bs_ref, ga_ref, gb_ref, gc_ref, mq_ref, mk_ref, mvt_ref, dqk_ref,
     dvt_ref) = (next(it) for _ in range(12))
    if ctx_out:
        ckvn_ref, kpe_ref, kc_ref, vc_ref = (next(it) for _ in range(4))

    d = D_MODEL
    shift = mod_ref[0, :, 0:d]
    scale = mod_ref[0, :, d:2 * d]
    h = (_rms(x_ref[...], gpre_ref[0]) * (1.0 + scale) + shift).astype(BF16)
    proj = lambda r: _dot_nt(h, w_ref[0, r[0]:r[1], :])
    if rope:
        cos = cos_ref[...]
        sin = sin_ref[...]

    pg = proj(R_QKV)
    qk_ref[:, 0:128] = pg[:, 0:128] * (DK_A ** -0.5)
    qk_ref[:, 128:256] = pg[:, 128:256]
    v_ref[...] = pg[:, 256:512]
    pm = proj(R_MLA)
    tail = pm[:, 384:512]
    xg = _dot(tail.astype(BF16), wg_ref[0]) + bg_ref[0]
    la = _log_sigmoid(xg) * (1.0 / GLA_TAU)
    la_ref[...] = la
    bs_ref[:, 0:128] = _chunk_scan(la[:, 0:128], False)
    bs_ref[:, 128:256] = _chunk_scan(la[:, 128:256], True)
    ga_ref[...] = _silu(proj(R_GG)).astype(BF16)

    qall = _dot(_rms(pm[:, 0:256], gq_ref[0]).astype(BF16), wuq_ref[0])
    q_pe = qall[:, 512:768]
    if rope:
        q_pe = _rope(q_pe, cos, sin)
    sb = (DN_B + DR_B) ** -0.5 * LOG2E
    mq_ref[:, 0:512] = (qall[:, 0:512] * sb).astype(BF16)
    mq_ref[:, 512:768] = (q_pe * sb).astype(BF16)
    ckvn = _rms(pm[:, 256:384], gkv_ref[0])
    kvall = _dot(ckvn.astype(BF16), wukv_ref[0])
    lane = lax.broadcasted_iota(jnp.int32, tail.shape, 1)
    kpe4 = jnp.where(lane < DR_B, tail, 0.0)
    kpe4 = kpe4 + pltpu.roll(kpe4, DR_B, 1)
    kpe4 = kpe4 + pltpu.roll(kpe4, 2 * DR_B, 1)
    if rope:
        kpe4 = _rope(kpe4, cos[:, 0:128], sin[:, 0:128])
    mk_ref[:, 0:512] = kvall[:, 0:512].astype(BF16)
    mk_ref[:, 512:640] = kpe4.astype(BF16)
    mvt_ref[...] = kvall[:, 512:1024].T.astype(BF16)
    gb_ref[...] = _silu(proj(R_MG)).astype(BF16)

    pd = proj(R_DIFF)
    dq, dk, dv = pd[:, 0:256], pd[:, 256:512], pd[:, 512:768]
    if rope:
        dq = _rope(dq, cos, sin)
        dk = _rope(dk, cos, sin)
    dqk_ref[:, 0:256] = (dq * (DC ** -0.5 * LOG2E)).astype(BF16)
    dqk_ref[:, 256:512] = dk.astype(BF16)
    dv_t = dv.T
    dvt_ref[...] = dv_t.astype(BF16)
    gc_ref[...] = _silu(pd[:, 768:1024]).astype(BF16)
    if ctx_out:
        kpe_t = kpe4.T
        dk_t = dk.T
        for ref in () if alias_in else (ckvn_ref, kpe_ref, kc_ref, vc_ref):
            ref[:, 1:] = jnp.zeros(ref[:, 1:].shape, F32)
        for bb in range(bpb):
            rs = slice(bb * seq, (bb + 1) * seq)
            ckvn_ref[bb, 0] = ckvn[rs]
            kpe_ref[bb, 0] = kpe_t[0:DR_B, rs]
            kc_ref[bb, 0] = dk_t[:, rs].reshape(H_C, 2 * DC, seq)
            vc_ref[bb, 0] = dv_t[:, rs].reshape(H_C, 2 * DC, seq)


def _pre_call(x2d, mod, pw, layer, *, seq, rope_tabs, caches):
    n = x2d.shape[0]
    tm = min(TOKEN_BLOCK, n)
    bpb = max(tm // seq, 1)
    rope = rope_tabs is not None
    ctx_out = caches is not None
    steps_per_seq = max(seq // tm, 1)
    nbt = n // seq

    def mod_idx(i):
        return (layer * MOD_ROWS + ((i * tm) // seq + 1 if rope else 0), 0, 0)

    names = ["g_pre", "w_t", "w_gate", "b_gate", "g_mla_q", "g_mla_kv", "w_uq", "w_ukv"]
    in_specs = [pl.BlockSpec((tm, D_MODEL), lambda i: (i, 0)), pl.BlockSpec((1, 1, 3 * D_MODEL), mod_idx)]
    in_specs += [_layer_spec(pw[k].shape, layer) for k in names]
    args = [x2d, mod] + [pw[k] for k in names]
    if rope:
        in_specs += [pl.BlockSpec((tm, 256), lambda i: (i % steps_per_seq, 0))] * 2
        args += list(rope_tabs)
    outs = [(256, F32, False), (256, F32, False), (256, F32, False), (256, F32, False),
            (W_A, BF16, False), (W_B, BF16, False), (W_C, BF16, False),
            (768, BF16, False), (640, BF16, False), (W_B, BF16, True),
            (512, BF16, False), (W_C, BF16, True)]
    out_specs = [pl.BlockSpec((w, tm), lambda i: (0, i)) if tr else pl.BlockSpec((tm, w), lambda i: (i, 0))
                 for w, _, tr in outs]
    out_shape = [jax.ShapeDtypeStruct((w, n) if tr else (n, w), dt) for w, dt, tr in outs]
    widths = outs
    aliases = {}
    if ctx_out:
        assert caches or layer == 0
        nl = 1 if caches else DEPTH
        out_specs += [pl.BlockSpec((bpb, nl, seq, KV_LORA), lambda i: (i, layer, 0, 0)),
                      pl.BlockSpec((bpb, nl, DR_B, seq), lambda i: (i, layer, 0, 0)),
                      pl.BlockSpec((bpb, nl, H_C, 2 * DC, seq), lambda i: (i, layer, 0, 0, 0)),
                      pl.BlockSpec((bpb, nl, H_C, 2 * DC, seq), lambda i: (i, layer, 0, 0, 0))]
        out_shape += [jax.ShapeDtypeStruct((nbt, DEPTH, seq, KV_LORA), F32),
                      jax.ShapeDtypeStruct((nbt, DEPTH, DR_B, seq), F32),
                      jax.ShapeDtypeStruct((nbt, DEPTH, H_C, 2 * DC, seq), F32),
                      jax.ShapeDtypeStruct((nbt, DEPTH, H_C, 2 * DC, seq), F32)]
        for j, arr in enumerate(caches):
            aliases[len(args)] = len(widths) + j
            in_specs.append(_ANY)
            args.append(arr)
    return pl.pallas_call(
        functools.partial(_pre_kernel, rope=rope, ctx_out=ctx_out, alias_in=len(aliases), bpb=bpb, seq=seq),
        grid=(n // tm,), in_specs=in_specs, out_specs=out_specs, out_shape=out_shape,
        input_output_aliases=aliases,
        compiler_params=_cparams(), name="pre_rope" if rope else "pre_ctx",
    )(*args)


_GLA_LEVELS = (1, 2, 4, 8, 16, 32)


def _gla_consts(rev):
    c = GLA_CHUNK
    row = lax.broadcasted_iota(jnp.int32, (c, 128), 0)
    pos = (c - 1 - row) if rev else row
    ri = lax.broadcasted_iota(jnp.int32, (c, H_A * c), 0)
    cj = lax.broadcasted_iota(jnp.int32, (c, H_A * c), 1) & (c - 1)
    pi = (c - 1 - ri) if rev else ri
    pj = (c - 1 - cj) if rev else cj
    x = pi ^ pj
    lvl = jnp.where(pi == pj, 0, -1)
    for kbit in range(6):
        lvl = jnp.where((pj < pi) & ((x >> kbit) == 1), kbit + 1, lvl)
    return pos, lvl


def _chunk_scan(x, rev):
    rows = x.shape[0]
    nt = rows // SUBLANES
    tiles_per_chunk = GLA_CHUNK // SUBLANES
    x3 = x.reshape(nt, SUBLANES, 128)
    sub = lax.broadcasted_iota(jnp.int32, x3.shape, 1)
    tile = lax.broadcasted_iota(jnp.int32, x3.shape, 0) & (tiles_per_chunk - 1)
    edge = 0 if rev else SUBLANES - 1
    s = 1
    while s < SUBLANES:
        if rev:
            x3 = x3 + jnp.where(sub < SUBLANES - s, pltpu.roll(x3, SUBLANES - s, 1), 0.0)
        else:
            x3 = x3 + jnp.where(sub >= s, pltpu.roll(x3, s, 1), 0.0)
        s *= 2
    s = 1
    while s < tiles_per_chunk:
        tot = jnp.broadcast_to(x3[:, edge:edge + 1, :], x3.shape)
        if rev:
            shifted = jnp.concatenate([tot[s:], tot[:s]], axis=0)
            x3 = x3 + jnp.where(tile < tiles_per_chunk - s, shifted, 0.0)
        else:
            shifted = jnp.concatenate([tot[nt - s:], tot[:nt - s]], axis=0)
            x3 = x3 + jnp.where(tile >= s, shifted, 0.0)
        s *= 2
    return x3.reshape(rows, 128)


def _gla_scores(q, k, la, b, pos, lvl, hm_bf, rev):
    c = GLA_CHUNK
    prv = pltpu.roll(la, c - 1 if rev else 1, 0)
    nxt = pltpu.roll(la, 1 if rev else c - 1, 0)
    s_tot = jnp.where(lvl == 0, _dot_nt(q.astype(BF16), jnp.concatenate([k.astype(BF16)] * H_A, 0) * hm_bf), 0.0)
    for kbit, m in enumerate(_GLA_LEVELS):
        up = ((pos >> kbit) & 1) == 1
        if m == 1:
            e = jnp.where(up, la, 0.0)
        elif m == 2:
            c4 = pos & 3
            e = jnp.where(c4 == 0, nxt, jnp.where(c4 == 1, 0.0, jnp.where(c4 == 2, la, la + prv)))
        else:
            nblk = c // (2 * m)
            loc = m if rev else m - 1
            b3 = b.reshape(nblk, 2 * m, 128)
            ref = jnp.broadcast_to(b3[:, loc:loc + 1, :], (nblk, 2 * m, 128)).reshape(c, 128)
            dlt = b - ref
            e = jnp.where(up, dlt, -dlt)
        xm = (jnp.where(up, q, k) * jnp.exp(e)).astype(BF16)
        sm = _dot_nt(xm, jnp.concatenate([xm] * H_A, 0) * hm_bf)
        s_tot = jnp.where(lvl == kbit + 1, sm, s_tot)
    return s_tot


def _gla_apply(s_tot, q, k, v, b, st_prev, hm_f32, vm_bf, rev):
    c = GLA_CHUNK
    vbd = jnp.concatenate([v] * H_A, 0) * vm_bf
    blast = b[0:1, :] if rev else b[c - 1:c, :]
    qbar = (q * jnp.exp(b)).astype(BF16)
    kdec = (k * jnp.exp(blast - b)).astype(BF16)
    o = _dot(s_tot.astype(BF16), vbd) + _dot_nt(qbar, st_prev.astype(BF16))
    st_new = st_prev * jnp.exp(blast) + _dot_tn(v, kdec) * hm_f32
    return o, st_new


def _gla_kernel(*refs, seq, has_s0, alias_in, spb):
    it = iter(refs)
    qk_ref, v_ref, la_ref, b_ref, gate_ref, g_ref = (next(it) for _ in range(6))
    s0_ref = next(it) if has_s0 else None
    for _ in range(alias_in):
        next(it)
    oa_ref = next(it)
    sfin_ref = None if has_s0 else next(it)
    acc_sc, st_sc, blk_sc = next(it), next(it), next(it)

    c = GLA_CHUNK
    nc = seq // c
    acc_sc[...] = jnp.zeros_like(acc_sc)
    if not has_s0 and not alias_in:
        sfin_ref[:, 1:] = jnp.zeros(sfin_ref[:, 1:].shape, F32)

    hrow = lax.broadcasted_iota(jnp.int32, (H_A * c, 128), 0) // c
    hm_f32 = jnp.where(hrow == lax.broadcasted_iota(jnp.int32, (H_A * c, 128), 1) // DK_A, 1.0, 0.0)
    hm_bf = hm_f32.astype(BF16)
    vrow = lax.broadcasted_iota(jnp.int32, (H_A * c, H_A * DV_A), 0) // c
    vm_bf = jnp.where(vrow == lax.broadcasted_iota(jnp.int32, (H_A * c, H_A * DV_A), 1) // DV_A,
                      1.0, 0.0).astype(BF16)
    consts = (_gla_consts(False), _gla_consts(True))

    def run_sequence(bb):
        base = bb * seq
        for d in range(2):
            if has_s0:
                blk_sc[...] = jnp.zeros_like(blk_sc)
                for hh in range(H_A):
                    blk_sc[DK_A * hh:DK_A * (hh + 1), DV_A * hh:DV_A * (hh + 1)] = s0_ref[bb, 0, d, hh]
                st_sc[d] = blk_sc[...].T
            else:
                st_sc[d] = jnp.zeros((ST_R, ST_C), F32)

        def chunk_rows(n, d):
            cn = (nc - 1 - n) if d else n
            start = base + cn * c
            return pl.ds(start if isinstance(start, int) else pl.multiple_of(start, c), c)

        def scores(n):
            out = []
            for d in range(2):
                rows = chunk_rows(n, d)
                pos, lvl = consts[d]
                out.append(_gla_scores(qk_ref[rows, 0:128], qk_ref[rows, 128:256],
                                       la_ref[rows, 128 * d:128 * d + 128],
                                       b_ref[rows, 128 * d:128 * d + 128], pos, lvl, hm_bf, bool(d)))
            return tuple(out)

        def apply(n, s_both):
            for d in range(2):
                rows = chunk_rows(n, d)
                o, st_new = _gla_apply(s_both[d], qk_ref[rows, 0:128], qk_ref[rows, 128:256],
                                       v_ref[rows, :].astype(BF16), b_ref[rows, 128 * d:128 * d + 128],
                                       st_sc[d], hm_f32, vm_bf, bool(d))
                acc_sc[rows, :] = acc_sc[rows, :] + o
                st_sc[d] = st_new

        def body(n, s_cur):
            s_next = scores(n + 1)
            apply(n, s_cur)
            return s_next

        trips = nc - 1
        s_last = lax.fori_loop(0, trips, body, scores(0),
                               unroll=next(u for u in (5, 3, 1) if trips % u == 0))
        apply(nc - 1, s_last)

        rs = slice(base, base + seq)
        first = lax.broadcasted_iota(jnp.int32, (seq, 128), 1) < DV_A
        for p in range(H_A // 2):
            cols = slice(128 * p, 128 * (p + 1))
            oa = acc_sc[rs, cols]
            sq = oa * oa
            s0 = jnp.sum(jnp.where(first, sq, 0.0), axis=-1, keepdims=True)
            s1 = jnp.sum(jnp.where(first, 0.0, sq), axis=-1, keepdims=True)
            inv = jnp.where(first, lax.rsqrt(s0 * (1.0 / DV_A) + EPS), lax.rsqrt(s1 * (1.0 / DV_A) + EPS))
            oa_ref[rs, cols] = (oa * inv * g_ref[0, :, cols] * gate_ref[rs, cols]).astype(BF16)
        if not has_s0:
            for d in range(2):
                blk_sc[...] = st_sc[d].T
                for hh in range(H_A):
                    sfin_ref[bb, 0, d, hh] = blk_sc[DK_A * hh:DK_A * (hh + 1), DV_A * hh:DV_A * (hh + 1)]

    for bb in range(spb):
        run_sequence(bb)


def _gla_call(qk, v, la, bsum, gate, g4, layer, *, seq, state_in=None, state_out=None):
    n = qk.shape[0]
    nb = n // seq
    has_s0 = state_in is not None
    assert has_s0 or state_out is not None or layer == 0
    spb = max(1, min(nb, GLA_GROUP_ROWS // seq))
    blk = lambda w: pl.BlockSpec((spb * seq, w), lambda i: (i, 0))
    nl = DEPTH if (not has_s0 and state_out is None) else 1
    st_spec = pl.BlockSpec((spb, nl, 2, H_A, DK_A, DV_A), lambda i: (i, layer, 0, 0, 0, 0))
    in_specs = [blk(256), blk(256), blk(256), blk(256), blk(W_A), _layer_spec(g4.shape, layer)]
    args = [qk, v, la, bsum, gate, g4]
    out_specs = [blk(W_A)]
    out_shape = [jax.ShapeDtypeStruct((n, W_A), BF16)]
    aliases = {}
    if has_s0:
        in_specs.append(st_spec)
        args.append(state_in)
    else:
        out_specs.append(st_spec)
        out_shape.append(jax.ShapeDtypeStruct((nb, DEPTH, 2, H_A, DK_A, DV_A), F32))
        if state_out is not None:
            aliases[len(args)] = 1
            in_specs.append(_ANY)
            args.append(state_out)
    return pl.pallas_call(
        functools.partial(_gla_kernel, seq=seq, has_s0=has_s0, alias_in=len(aliases), spb=spb),
        grid=(nb // spb,), in_specs=in_specs, out_specs=out_specs, out_shape=out_shape,
        input_output_aliases=aliases,
        scratch_shapes=[pltpu.VMEM((spb * seq, W_A), F32), pltpu.VMEM((2, ST_R, ST_C), F32),
                        pltpu.VMEM((ST_C, ST_R), F32)],
        compiler_params=_cparams(), name="gla_state" if has_s0 else "gla_ctx",
    )(*args)


def _softmax_t_pv(st, vt):
    m = _col_reduce(st, jnp.max)
    e = jnp.exp2(st - m)
    l = _col_reduce(e, jnp.sum)
    return _dot(vt, e.astype(BF16)) * (1.0 / l)


def _pipelined_attention(score_fns, value_fns, depth):
    outs = []
    pending = [fn() for fn in score_fns[:depth]]
    for j, vfn in enumerate(value_fns):
        st = pending.pop(0)
        if j + depth < len(score_fns):
            pending.append(score_fns[j + depth]())
        outs.append(_softmax_t_pv(st, vfn()))
    return outs


def _lookahead(n_keys):
    return 2 if n_keys >= 1024 else 3


def _col_reduce(x, op):
    rows, cols = x.shape
    part = 128 if rows % 128 == 0 and rows > 128 else rows
    if part != rows:
        x = op(x.reshape(rows // part, part, cols), axis=0)
    return op(x, axis=0, keepdims=True)


def _seqs_per_step(n, seq, ctx_len):
    if ctx_len or seq >= ATT_QBLOCK:
        return 1
    return max(1, min(n // seq, ATT_GROUP_ROWS // seq))


def _for_query_blocks(run, seq, spb):
    qb = min(ATT_QBLOCK, seq)
    nq = seq // qb
    if spb * nq <= ATT_STATIC_BLOCKS:
        run([(bb, slice(bb * seq + j * qb, bb * seq + (j + 1) * qb)) for bb in range(spb) for j in range(nq)])
    else:
        assert spb == 1

        def body(i, carry):
            run([(0, pl.ds(pl.multiple_of(i * qb, qb), qb))])
            return carry
        lax.fori_loop(0, nq, body, 0)


def _mla_part(q_ref, k_ref, vt_ref, gate_ref, ck_ref, cvt_ref, ob_ref, kk_sc, vt_sc, *, seq, ctx_len, spb):
    for bb in range(spb):
        rs = slice(bb * seq, (bb + 1) * seq)
        for p in range(H_B // 2):
            kk_sc[bb, p, 0:seq, 0:128] = k_ref[rs, 128 * p:128 * p + 128]
            kk_sc[bb, p, 0:seq, 128:256] = k_ref[rs, 512:640]
            if ctx_len:
                kk_sc[bb, p, seq:seq + ctx_len, 0:128] = ck_ref[0, 0, :, 128 * p:128 * p + 128]
                kk_sc[bb, p, seq:seq + ctx_len, 128:256] = ck_ref[0, 0, :, 512:640]
        vt_sc[bb, :, 0:seq] = vt_ref[:, rs]
        if ctx_len:
            vt_sc[bb, :, seq:seq + ctx_len] = cvt_ref[0, 0]

    def scores(bb, rows, h):
        p, hh = divmod(h, 2)
        qn = q_ref[rows, 128 * p:128 * p + 128] * _lane_mask(128, DN_B * hh, DN_B, BF16)
        qp = (q_ref[rows, 512 + 128 * (h // 4):512 + 128 * (h // 4) + 128]
              * _lane_mask(128, DR_B * (h % 4), DR_B, BF16))
        return _dot_nt(kk_sc[bb, p], jnp.concatenate([qn, qp], axis=-1))

    def items(blocks):
        its = [(bb, rows, h) for bb, rows in blocks for h in range(H_B)]

        def finish(outs):
            for j, (_, rows) in enumerate(blocks):
                ob = jnp.concatenate(outs[H_B * j:H_B * (j + 1)], axis=0).T
                ob_ref[rows, :] = (ob * gate_ref[rows, :]).astype(BF16)

        return ([functools.partial(scores, bb, rows, h) for bb, rows, h in its],
                [functools.partial(lambda bb, h: vt_sc[bb, DV_B * h:DV_B * (h + 1), :], bb, h) for bb, _, h in its],
                finish)

    return items


def _diff_part(qk_ref, vt_ref, gate_ref, lam_ref, g_ref, ck_ref, cv_ref, oc_ref, k_sc, vt_sc, *,
               seq, ctx_len, lam_init, spb):
    lam = (jnp.exp(jnp.sum(lam_ref[0, 0:1, :] * lam_ref[0, 1:2, :], axis=-1, keepdims=True))
           - jnp.exp(jnp.sum(lam_ref[0, 2:3, :] * lam_ref[0, 3:4, :], axis=-1, keepdims=True)) + lam_init)
    dh = 2 * DC
    for bb in range(spb):
        rs = slice(bb * seq, (bb + 1) * seq)
        for p in range(H_C // 2):
            k_sc[bb, p, 0:seq, :] = qk_ref[rs, 256 + 128 * p:256 + 128 * p + 128]
            if ctx_len:
                pair_t = jnp.concatenate([ck_ref[0, 0, 2 * p], ck_ref[0, 0, 2 * p + 1]], axis=0)
                k_sc[bb, p, seq:seq + ctx_len, :] = pair_t.T.astype(BF16)
        vt_sc[bb, :, 0:seq] = vt_ref[:, rs]
        if ctx_len:
            for h in range(H_C):
                vt_sc[bb, dh * h:dh * (h + 1), seq:seq + ctx_len] = cv_ref[0, 0, h].astype(BF16)

    def scores(bb, rows, h, comp):
        p, hh = divmod(h, 2)
        qm = qk_ref[rows, 128 * p:128 * p + 128] * _lane_mask(128, dh * hh + DC * comp, DC, BF16)
        return _dot_nt(k_sc[bb, p], qm)

    def items(blocks):
        its = [(bb, rows, h, comp) for bb, rows in blocks for h in range(H_C) for comp in range(2)]

        def finish(o12):
            for j, (_, rows) in enumerate(blocks):
                outs = []
                for h in range(H_C):
                    o1, o2 = o12[2 * (H_C * j + h)], o12[2 * (H_C * j + h) + 1]
                    ot = o1 - lam * o2
                    outs.append(ot * lax.rsqrt(jnp.mean(ot * ot, axis=0, keepdims=True) + EPS))
                oc = jnp.concatenate(outs, axis=0).T
                oc_ref[rows, :] = (oc * g_ref[0] * (1.0 - lam_init) * gate_ref[rows, :]).astype(BF16)

        return ([functools.partial(scores, *item) for item in its],
                [functools.partial(lambda bb, h: vt_sc[bb, dh * h:dh * (h + 1), :], bb, h) for bb, _, h, _ in its],
                finish)

    return items


def _attn_kernel(*refs, seq, ctx_len, lam_init, spb, mla, diff):
    it = iter(refs)
    n_ctx = 2 if ctx_len else 0
    mla_in = [next(it) for _ in range(4 + n_ctx)] + [None] * (2 - n_ctx) if mla else None
    diff_in = [next(it) for _ in range(5 + n_ctx)] + [None] * (2 - n_ctx) if diff else None
    ob_ref = next(it) if mla else None
    oc_ref = next(it) if diff else None
    parts = []
    if mla:
        parts.append(_mla_part(*mla_in, ob_ref, next(it), next(it), seq=seq, ctx_len=ctx_len, spb=spb))
    if diff:
        parts.append(_diff_part(*diff_in, oc_ref, next(it), next(it), seq=seq, ctx_len=ctx_len,
                                lam_init=lam_init, spb=spb))

    def run(blocks):
        built = [part(blocks) for part in parts]
        outs = _pipelined_attention([f for b in built for f in b[0]], [f for b in built for f in b[1]],
                                    _lookahead(seq + ctx_len))
        lo = 0
        for score_fns, _, finish in built:
            finish(outs[lo:lo + len(score_fns)])
            lo += len(score_fns)

    _for_query_blocks(run, seq, spb)


def _attn_call(mq, mk, mvt, gate_b, dqk, dvt, gate_c, lamp, g4, ctx, layer, *, seq):
    n = mq.shape[0]
    ctx_len = 0 if ctx is None else ctx["mla_k"].shape[2]
    lam_init = 0.8 - 0.6 * math.exp(-0.3 * layer)
    spb = _seqs_per_step(n, seq, ctx_len)
    rows = spb * seq
    tk = seq + ctx_len
    rblk = lambda w: pl.BlockSpec((rows, w), lambda i: (i, 0))
    tblk = lambda w: pl.BlockSpec((w, rows), lambda i: (0, i))
    mla_specs, mla_args = [rblk(768), rblk(640), tblk(W_B), rblk(W_B)], [mq, mk, mvt, gate_b]
    diff_specs = [rblk(512), tblk(W_C), rblk(W_C), _layer_spec(lamp.shape, layer), _layer_spec(g4.shape, layer)]
    diff_args = [dqk, dvt, gate_c, lamp, g4]
    if ctx_len:
        mla_specs += [pl.BlockSpec((1, 1, ctx_len, 640), lambda i: (layer, i, 0, 0)),
                      pl.BlockSpec((1, 1, W_B, ctx_len), lambda i: (layer, i, 0, 0))]
        mla_args += [ctx["mla_k"], ctx["mla_vt"]]
        diff_specs += [pl.BlockSpec((1, 1, H_C, 2 * DC, ctx_len), lambda i: (i, layer, 0, 0, 0))] * 2
        diff_args += [ctx["diff_k_t"], ctx["diff_v_t"]]
    mla_out = (rblk(W_B), jax.ShapeDtypeStruct((n, W_B), BF16))
    diff_out = (rblk(W_C), jax.ShapeDtypeStruct((n, W_C), BF16))
    mla_scratch = [pltpu.VMEM((spb, H_B // 2, tk, 256), BF16), pltpu.VMEM((spb, W_B, tk), BF16)]
    diff_scratch = [pltpu.VMEM((spb, H_C // 2, tk, 128), BF16), pltpu.VMEM((spb, W_C, tk), BF16)]

    def call(mla, diff, name):
        outs = ([mla_out] if mla else []) + ([diff_out] if diff else [])
        return pl.pallas_call(
            functools.partial(_attn_kernel, seq=seq, ctx_len=ctx_len, lam_init=lam_init, spb=spb,
                              mla=mla, diff=diff),
            grid=(n // rows,),
            in_specs=(mla_specs if mla else []) + (diff_specs if diff else []),
            out_specs=[o[0] for o in outs], out_shape=[o[1] for o in outs],
            scratch_shapes=(mla_scratch if mla else []) + (diff_scratch if diff else []),
            compiler_params=_cparams(), name=name,
        )(*(mla_args if mla else []), *(diff_args if diff else []))

    if ctx_len:
        (ob,) = call(True, False, "mla_ctx")
        (oc,) = call(False, True, "diff_ctx")
        return ob, oc
    return call(True, True, "attn_self")


def _post_kernel(oa_ref, ob_ref, oc_ref, x_ref, mod_ref, w_ref, g_ref, y_ref):
    mix = jnp.concatenate([oa_ref[...], ob_ref[...], oc_ref[...]], axis=-1)
    out = _rms(_dot(mix, w_ref[0]), g_ref[0])
    y_ref[...] = x_ref[...] + mod_ref[0, :, 2 * D_MODEL:3 * D_MODEL] * out


def _post_call(oa, ob, oc, x2d, mod, pw, layer, *, seq, sample):
    n = x2d.shape[0]
    tm = min(POST_BLOCK, seq if sample else n)
    blk = lambda w: pl.BlockSpec((tm, w), lambda i: (i, 0))

    def mod_idx(i):
        return (layer * MOD_ROWS + ((i * tm) // seq + 1 if sample else 0), 0, 0)

    return pl.pallas_call(
        _post_kernel,
        grid=(n // tm,),
        in_specs=[blk(W_A), blk(W_B), blk(W_C), blk(D_MODEL),
                  pl.BlockSpec((1, 1, 3 * D_MODEL), mod_idx),
                  _layer_spec(pw["w_out"].shape, layer), _layer_spec(pw["g_post"].shape, layer)],
        out_specs=blk(D_MODEL),
        out_shape=jax.ShapeDtypeStruct((n, D_MODEL), F32),
        compiler_params=_cparams(), name="post",
    )(oa, ob, oc, x2d, mod, pw["w_out"], pw["g_post"])


def _pack_params(g_pre, g_post, w_in, w_gla_af, b_gla_af, w_gla_ab, b_gla_ab, g_gla, g_mla_q, w_mla_uq,
                 g_mla_kv, w_mla_ukv, lam_q1, lam_k1, lam_q2, lam_k2, g_diff, w_out):
    w_t = jnp.swapaxes(w_in, 1, 2)
    ga0, ga1 = _IN["ga"], _IN["gg"]
    w_t = jnp.concatenate([w_t[:, 0:ga0], w_t[:, ga1:_IN["mg"]], w_t[:, ga0:ga1],
                           jnp.zeros((DEPTH, 64, D_MODEL), F32), w_t[:, _IN["mg"]:]], axis=1)
    assert w_t.shape[1] == R_DIFF[1]
    zg = jnp.zeros((DEPTH, GLA_LR, 128), F32)
    w_gate = jnp.concatenate([jnp.zeros((DEPTH, DR_B, 256), F32),
                              jnp.concatenate([w_gla_af, zg], axis=-1),
                              jnp.concatenate([zg, w_gla_ab], axis=-1),
                              jnp.zeros((DEPTH, 128 - DR_B - 2 * GLA_LR, 256), F32)], axis=1).astype(BF16)
    uq = w_mla_uq.reshape(DEPTH, Q_LORA, H_B, DN_B + DR_B)
    w_pe = uq[..., DN_B:].reshape(DEPTH, Q_LORA, H_B * DR_B)
    w_uq = jnp.concatenate([uq[..., :DN_B].reshape(DEPTH, Q_LORA, H_B * DN_B), w_pe], axis=-1).astype(BF16)
    ukv = w_mla_ukv.reshape(DEPTH, KV_LORA, H_B, DN_B + DV_B)
    w_ukv = jnp.concatenate([ukv[..., :DN_B].reshape(DEPTH, KV_LORA, H_B * DN_B),
                             ukv[..., DN_B:].reshape(DEPTH, KV_LORA, H_B * DV_B)], axis=-1).astype(BF16)
    row = lambda a: a.reshape(DEPTH, 1, a.shape[-1])
    return dict(
        w_t=w_t.astype(BF16), w_gate=w_gate,
        b_gate=row(jnp.concatenate([b_gla_af, b_gla_ab], axis=-1)),
        w_uq=w_uq, w_ukv=w_ukv, w_out=w_out.astype(BF16),
        g_pre=row(g_pre), g_post=row(g_post), g_mla_q=row(g_mla_q), g_mla_kv=row(g_mla_kv),
        g_gla4=row(jnp.tile(g_gla, (1, H_A))), g_diff4=row(jnp.tile(g_diff, (1, H_C))),
        lam=jnp.stack([lam_q1, lam_k1, lam_q2, lam_k2], axis=1))


def _rope_tables(n):
    t = np.arange(n)
    row = (t // GRID_W).astype(np.float32)
    col = (t % GRID_W).astype(np.float32)
    half = ROPE_DIM // 2
    inv = (1.0 / (np.float32(ROPE_THETA) ** (np.arange(0, half, 2, dtype=np.float32) / np.float32(half)))
           ).astype(np.float32)
    ar = row[:, None] * inv
    ac = col[:, None] * inv
    ang = np.concatenate([ar, ar, ac, ac], axis=-1).astype(np.float32)
    return (jnp.asarray(np.tile(np.cos(ang), (1, 8)).astype(np.float32)),
            jnp.asarray(np.tile(np.sin(ang), (1, 8)).astype(np.float32)))


def _sublayer(x2d, mod, pw, layer, *, seq, rope_tabs, ctx, caches):
    sample = ctx is not None
    pre = _pre_call(x2d, mod, pw, layer, seq=seq, rope_tabs=rope_tabs,
                    caches=None if sample else caches[:4])
    qk, v, la, bsum, ga, gb, gc, mq, mk, mvt, dqk, dvt = pre[:12]
    if sample:
        (oa,) = _gla_call(qk, v, la, bsum, ga, pw["g_gla4"], layer, seq=seq, state_in=ctx["state"])
        new_caches = None
    else:
        oa, sfin = _gla_call(qk, v, la, bsum, ga, pw["g_gla4"], layer, seq=seq,
                             state_out=caches[4] if caches else None)
        new_caches = tuple(pre[12:]) + (sfin,)
    ob, oc = _attn_call(mq, mk, mvt, gb, dqk, dvt, gc, pw["lam"], pw["g_diff4"], ctx, layer, seq=seq)
    y = _post_call(oa, ob, oc, x2d, mod, pw, layer, seq=seq, sample=sample)
    return y, new_caches


def kernel(x_prompt, x_sample, c, cache_mla_ckv, cache_mla_kpe, cache_diff_k, cache_diff_v, state_gla,
           c_ctx, w_ada, b_ada, g_pre, g_post, w_in, w_gla_af, b_gla_af, w_gla_ab, b_gla_ab, g_gla,
           g_mla_q, w_mla_uq, g_mla_kv, w_mla_ukv, lam_q1, lam_k1, lam_q2, lam_k2, g_diff, w_out):
    bp, tp, d = x_prompt.shape
    bs, ts, _ = x_sample.shape

    pw = _pack_params(g_pre, g_post, w_in, w_gla_af, b_gla_af, w_gla_ab, b_gla_ab, g_gla, g_mla_q,
                      w_mla_uq, g_mla_kv, w_mla_ukv, lam_q1, lam_k1, lam_q2, lam_k2, g_diff, w_out)
    cvecs = jnp.concatenate([c_ctx[None], c, jnp.zeros((MOD_ROWS - 1 - bs, d), F32)], axis=0)
    mod = _mod_call(cvecs, w_ada, b_ada).reshape(DEPTH * MOD_ROWS, 1, 3 * d)
    rope_tabs = _rope_tables(ts)
    ctx_k, ctx_vt = _ctxkv_call(cache_mla_ckv, jnp.swapaxes(cache_mla_kpe, -1, -2), pw["w_ukv"])
    ctx = dict(state=state_gla, mla_k=ctx_k, mla_vt=ctx_vt,
               diff_k_t=jnp.swapaxes(cache_diff_k, -1, -2), diff_v_t=jnp.swapaxes(cache_diff_v, -1, -2))

    y_p = x_prompt.reshape(bp * tp, d)
    y_s = x_sample.reshape(bs * ts, d)
    caches = ()
    for l in range(DEPTH):
        y_p, caches = _sublayer(y_p, mod, pw, l, seq=tp, rope_tabs=None, ctx=None, caches=caches)
        y_s, _ = _sublayer(y_s, mod, pw, l, seq=ts, rope_tabs=rope_tabs, ctx=ctx, caches=None)
    ckvn, kpe_t, kc_t, vc_t, new_state = caches
    return (y_p.reshape(bp, tp, d), y_s.reshape(bs, ts, d), ckvn, jnp.swapaxes(kpe_t, -1, -2),
            jnp.swapaxes(kc_t, -1, -2), jnp.swapaxes(vc_t, -1, -2), new_state)
```

```python
import functools
import math

import numpy as np
import jax
import jax.numpy as jnp
from jax import lax
from jax.experimental import pallas as pl
from jax.experimental.pallas import tpu as pltpu

F32 = jnp.float32
BF16 = jnp.bfloat16

D_MODEL = 1024
DEPTH = 2
GRID_W = 64
EPS = 1e-6
ROPE_THETA = 10000.0
ROPE_DIM = 32
H_A, DK_A, DV_A = 4, 32, 64
GLA_LR = 16
GLA_TAU = 16.0
GLA_CHUNK = 64
H_B, DN_B, DR_B, DV_B = 8, 64, 32, 64
Q_LORA, KV_LORA = 256, 128
H_C, DC = 4, 32
W_A, W_B, W_C = H_A * DV_A, H_B * DV_B, H_C * 2 * DC
ST_R, ST_C = H_A * DV_A, H_A * DK_A
LOG2E = math.log2(math.e)
SUBLANES = 8

_IN = dict(ga=512, gg=544, mg=1216)
R_QKV = (0, 512)
R_GG = (512, 768)
R_MLA = (768, 1280)
R_MG = (1280, 1792)
R_DIFF = (1792, 2816)
MOD_ROWS = 8

V7X_VMEM_LIMIT_BYTES = 56 * 1024 * 1024
TOKEN_BLOCK = 1024
POST_BLOCK = 1024
ATT_QBLOCK = 512
ATT_GROUP_ROWS = 1024
GLA_GROUP_ROWS = 1024
ATT_STATIC_BLOCKS = 4


def _cparams(n_axes=1):
    return pltpu.CompilerParams(dimension_semantics=("arbitrary",) * n_axes,
                                vmem_limit_bytes=V7X_VMEM_LIMIT_BYTES)


def _rms(x, g):
    return x * lax.rsqrt(jnp.mean(x * x, axis=-1, keepdims=True) + EPS) * g


def _silu(x):
    return x * jax.nn.sigmoid(x)


def _log_sigmoid(x):
    return jnp.minimum(x, 0.0) - jnp.log1p(jnp.exp(-jnp.abs(x)))


def _rope(z, cos, sin):
    w = z.shape[-1]
    lane = lax.broadcasted_iota(jnp.int32, z.shape, 1)
    rot = jnp.where((lane & 15) < 8, -pltpu.roll(z, w - 8, 1), pltpu.roll(z, 8, 1))
    return z * cos + rot * sin


def _lane_mask(width, lo, size, dtype):
    lane = lax.broadcasted_iota(jnp.int32, (1, width), 1)
    return jnp.where((lane >= lo) & (lane < lo + size), 1.0, 0.0).astype(dtype)


def _dot(a, b):
    return jnp.dot(a, b, preferred_element_type=F32)


def _dot_nt(a, b):
    return lax.dot_general(a, b, (((1,), (1,)), ((), ())), preferred_element_type=F32)


def _dot_tn(a, b):
    return lax.dot_general(a, b, (((0,), (0,)), ((), ())), preferred_element_type=F32)


def _layer_spec(shape, layer):
    nd = len(shape)
    return pl.BlockSpec((1,) + tuple(shape[1:]), lambda *_: (layer,) + (0,) * (nd - 1))


_ANY = pl.BlockSpec(memory_space=pl.ANY)


def _mod_kernel(c_ref, w_ref, b_ref, o_ref):
    s = _silu(c_ref[...]).astype(BF16)
    o_ref[0] = _dot(s, w_ref[0].astype(BF16)) + b_ref[0]


def _mod_call(cvecs, w_ada, b_ada):
    nb = 1024
    return pl.pallas_call(
        _mod_kernel,
        grid=(DEPTH, 3 * D_MODEL // nb),
        in_specs=[pl.BlockSpec((MOD_ROWS, D_MODEL), lambda l, j: (0, 0)),
                  pl.BlockSpec((1, D_MODEL, nb), lambda l, j: (l, 0, j)),
                  pl.BlockSpec((1, 1, nb), lambda l, j: (l, 0, j))],
        out_specs=pl.BlockSpec((1, MOD_ROWS, nb), lambda l, j: (l, 0, j)),
        out_shape=jax.ShapeDtypeStruct((DEPTH, MOD_ROWS, 3 * D_MODEL), F32),
        compiler_params=_cparams(2), name="adaln_mod",
    )(cvecs, w_ada, b_ada.reshape(DEPTH, 1, 3 * D_MODEL))


def _ctxkv_kernel(ckv_ref, kpe_ref, w_ref, k_ref, vt_ref):
    kv = _dot(ckv_ref[0, 0].astype(BF16), w_ref[0])
    kpe4 = jnp.concatenate([kpe_ref[0, 0]] * 4, axis=0).T
    k_ref[0, 0, :, 0:512] = kv[:, 0:512].astype(BF16)
    k_ref[0, 0, :, 512:640] = kpe4.astype(BF16)
    vt_ref[0, 0] = kv[:, 512:1024].T.astype(BF16)


def _ctxkv_call(cache_ckv, cache_kpe_t, wukv):
    nb, _, tc, _ = cache_ckv.shape
    return pl.pallas_call(
        _ctxkv_kernel,
        grid=(DEPTH, nb),
        in_specs=[pl.BlockSpec((1, 1, tc, KV_LORA), lambda l, b: (b, l, 0, 0)),
                  pl.BlockSpec((1, 1, DR_B, tc), lambda l, b: (b, l, 0, 0)),
                  pl.BlockSpec((1, KV_LORA, 1024), lambda l, b: (l, 0, 0))],
        out_specs=[pl.BlockSpec((1, 1, tc, 640), lambda l, b: (l, b, 0, 0)),
                   pl.BlockSpec((1, 1, W_B, tc), lambda l, b: (l, b, 0, 0))],
        out_shape=[jax.ShapeDtypeStruct((DEPTH, nb, tc, 640), BF16),
                   jax.ShapeDtypeStruct((DEPTH, nb, W_B, tc), BF16)],
        compiler_params=_cparams(2), name="mla_ctx_kv",
    )(cache_ckv, cache_kpe_t, wukv)


def _pre_kernel(*refs, rope, ctx_out, alias_in, bpb, seq):
    it = iter(refs)
    (x_ref, mod_ref, gpre_ref, w_ref, wg_ref, bg_ref, gq_ref, gkv_ref, wuq_ref,
     wukv_ref) = (next(it) for _ in range(10))
    if rope:
        cos_ref, sin_ref = next(it), next(it)
    for _ in range(alias_in):
        next(it)
    (qk_ref, v_ref, la_ref, bs_ref, ga_ref, gb_ref, gc_ref, mq_ref, mk_ref, mvt_ref, dqk_ref,
     dvt_ref) = (next(it) for _ in range(12))
    if ctx_out:
        ckvn_ref, kpe_ref, kc_ref, vc_ref = (next(it) for _ in range(4))

    d = D_MODEL
    shift = mod_ref[0, :, 0:d]
    scale = mod_ref[0, :, d:2 * d]
    h = (_rms(x_ref[...], gpre_ref[0]) * (1.0 + scale) + shift).astype(BF16)
    proj = lambda r: _dot_nt(h, w_ref[0, r[0]:r[1], :])
    if rope:
        cos = cos_ref[...]
        sin = sin_ref[...]

    pg = proj(R_QKV)
    qk_ref[:, 0:128] = pg[:, 0:128] * (DK_A ** -0.5)
    qk_ref[:, 128:256] = pg[:, 128:256]
    v_ref[...] = pg[:, 256:512]
    pm = proj(R_MLA)
    tail = pm[:, 384:512]
    xg = _dot(tail.astype(BF16), wg_ref[0]) + bg_ref[0]
    la = _log_sigmoid(xg) * (1.0 / GLA_TAU)
    la_ref[...] = la
    bs_ref[:, 0:128] = _chunk_scan(la[:, 0:128], False)
    bs_ref[:, 128:256] = _chunk_scan(la[:, 128:256], True)
    ga_ref[...] = _silu(proj(R_GG)).astype(BF16)

    qall = _dot(_rms(pm[:, 0:256], gq_ref[0]).astype(BF16), wuq_ref[0])
    q_pe = qall[:, 512:768]
    if rope:
        q_pe = _rope(q_pe, cos, sin)
    sb = (DN_B + DR_B) ** -0.5 * LOG2E
    mq_ref[:, 0:512] = (qall[:, 0:512] * sb).astype(BF16)
    mq_ref[:, 512:768] = (q_pe * sb).astype(BF16)
    ckvn = _rms(pm[:, 256:384], gkv_ref[0])
    kvall = _dot(ckvn.astype(BF16), wukv_ref[0])
    lane = lax.broadcasted_iota(jnp.int32, tail.shape, 1)
    kpe4 = jnp.where(lane < DR_B, tail, 0.0)
    kpe4 = kpe4 + pltpu.roll(kpe4, DR_B, 1)
    kpe4 = kpe4 + pltpu.roll(kpe4, 2 * DR_B, 1)
    if rope:
        kpe4 = _rope(kpe4, cos[:, 0:128], sin[:, 0:128])
    mk_ref[:, 0:512] = kvall[:, 0:512].astype(BF16)
    mk_ref[:, 512:640] = kpe4.astype(BF16)
    mvt_ref[...] = kvall[:, 512:1024].T.astype(BF16)
    gb_ref[...] = _silu(proj(R_MG)).astype(BF16)

    pd = proj(R_DIFF)
    dq, dk, dv = pd[:, 0:256], pd[:, 256:512], pd[:, 512:768]
    if rope:
        dq = _rope(dq, cos, sin)
        dk = _rope(dk, cos, sin)
    dqk_ref[:, 0:256] = (dq * (DC ** -0.5 * LOG2E)).astype(BF16)
    dqk_ref[:, 256:512] = dk.astype(BF16)
    dv_t = dv.T
    dvt_ref[...] = dv_t.astype(BF16)
    gc_ref[...] = _silu(pd[:, 768:1024]).astype(BF16)
    if ctx_out:
        kpe_t = kpe4.T
        dk_t = dk.T
        for ref in () if alias_in else (ckvn_ref, kpe_ref, kc_ref, vc_ref):
            ref[:, 1:] = jnp.zeros(ref[:, 1:].shape, F32)
        for bb in range(bpb):
            rs = slice(bb * seq, (bb + 1) * seq)
            ckvn_ref[bb, 0] = ckvn[rs]
            kpe_ref[bb, 0] = kpe_t[0:DR_B, rs]
            kc_ref[bb, 0] = dk_t[:, rs].reshape(H_C, 2 * DC, seq)
            vc_ref[bb, 0] = dv_t[:, rs].reshape(H_C, 2 * DC, seq)


def _pre_call(x2d, mod, pw, layer, *, seq, rope_tabs, caches):
    n = x2d.shape[0]
    tm = min(TOKEN_BLOCK, n)
    bpb = max(tm // seq, 1)
    rope = rope_tabs is not None
    ctx_out = caches is not None
    steps_per_seq = max(seq // tm, 1)
    nbt = n // seq

    def mod_idx(i):
        return (layer * MOD_ROWS + ((i * tm) // seq + 1 if rope else 0), 0, 0)

    names = ["g_pre", "w_t", "w_gate", "b_gate", "g_mla_q", "g_mla_kv", "w_uq", "w_ukv"]
    in_specs = [pl.BlockSpec((tm, D_MODEL), lambda i: (i, 0)), pl.BlockSpec((1, 1, 3 * D_MODEL), mod_idx)]
    in_specs += [_layer_spec(pw[k].shape, layer) for k in names]
    args = [x2d, mod] + [pw[k] for k in names]
    if rope:
        in_specs += [pl.BlockSpec((tm, 256), lambda i: (i % steps_per_seq, 0))] * 2
        args += list(rope_tabs)
    outs = [(256, F32, False), (256, F32, False), (256, F32, False), (256, F32, False),
            (W_A, BF16, False), (W_B, BF16, False), (W_C, BF16, False),
            (768, BF16, False), (640, BF16, False), (W_B, BF16, True),
            (512, BF16, False), (W_C, BF16, True)]
    out_specs = [pl.BlockSpec((w, tm), lambda i: (0, i)) if tr else pl.BlockSpec((tm, w), lambda i: (i, 0))
                 for w, _, tr in outs]
    out_shape = [jax.ShapeDtypeStruct((w, n) if tr else (n, w), dt) for w, dt, tr in outs]
    widths = outs
    aliases = {}
    if ctx_out:
        assert caches or layer == 0
        nl = 1 if caches else DEPTH
        out_specs += [pl.BlockSpec((bpb, nl, seq, KV_LORA), lambda i: (i, layer, 0, 0)),
                      pl.BlockSpec((bpb, nl, DR_B, seq), lambda i: (i, layer, 0, 0)),
                      pl.BlockSpec((bpb, nl, H_C, 2 * DC, seq), lambda i: (i, layer, 0, 0, 0)),
                      pl.BlockSpec((bpb, nl, H_C, 2 * DC, seq), lambda i: (i, layer, 0, 0, 0))]
        out_shape += [jax.ShapeDtypeStruct((nbt, DEPTH, seq, KV_LORA), F32),
                      jax.ShapeDtypeStruct((nbt, DEPTH, DR_B, seq), F32),
                      jax.ShapeDtypeStruct((nbt, DEPTH, H_C, 2 * DC, seq), F32),
                      jax.ShapeDtypeStruct((nbt, DEPTH, H_C, 2 * DC, seq), F32)]
        for j, arr in enumerate(caches):
            aliases[len(args)] = len(widths) + j
            in_specs.append(_ANY)
            args.append(arr)
    return pl.pallas_call(
        functools.partial(_pre_kernel, rope=rope, ctx_out=ctx_out, alias_in=len(aliases), bpb=bpb, seq=seq),
        grid=(n // tm,), in_specs=in_specs, out_specs=out_specs, out_shape=out_shape,
        input_output_aliases=aliases,
        compiler_params=_cparams(), name="pre_rope" if rope else "pre_ctx",
    )(*args)


_GLA_LEVELS = (1, 2, 4, 8, 16, 32)


def _gla_consts(rev):
    c = GLA_CHUNK
    row = lax.broadcasted_iota(jnp.int32, (c, 128), 0)
    pos = (c - 1 - row) if rev else row
    ri = lax.broadcasted_iota(jnp.int32, (c, H_A * c), 0)
    cj = lax.broadcasted_iota(jnp.int32, (c, H_A * c), 1) & (c - 1)
    pi = (c - 1 - ri) if rev else ri
    pj = (c - 1 - cj) if rev else cj
    x = pi ^ pj
    lvl = jnp.where(pi == pj, 0, -1)
    for kbit in range(6):
        lvl = jnp.where((pj < pi) & ((x >> kbit) == 1), kbit + 1, lvl)
    return pos, lvl


def _chunk_scan(x, rev):
    rows = x.shape[0]
    nt = rows // SUBLANES
    tiles_per_chunk = GLA_CHUNK // SUBLANES
    x3 = x.reshape(nt, SUBLANES, 128)
    sub = lax.broadcasted_iota(jnp.int32, x3.shape, 1)
    tile = lax.broadcasted_iota(jnp.int32, x3.shape, 0) & (tiles_per_chunk - 1)
    edge = 0 if rev else SUBLANES - 1
    s = 1
    while s < SUBLANES:
        if rev:
            x3 = x3 + jnp.where(sub < SUBLANES - s, pltpu.roll(x3, SUBLANES - s, 1), 0.0)
        else:
            x3 = x3 + jnp.where(sub >= s, pltpu.roll(x3, s, 1), 0.0)
        s *= 2
    s = 1
    while s < tiles_per_chunk:
        tot = jnp.broadcast_to(x3[:, edge:edge + 1, :], x3.shape)
        if rev:
            shifted = jnp.concatenate([tot[s:], tot[:s]], axis=0)
            x3 = x3 + jnp.where(tile < tiles_per_chunk - s, shifted, 0.0)
        else:
            shifted = jnp.concatenate([tot[nt - s:], tot[:nt - s]], axis=0)
            x3 = x3 + jnp.where(tile >= s, shifted, 0.0)
        s *= 2
    return x3.reshape(rows, 128)


def _gla_scores(q, k, la, b, pos, lvl, hm_bf, rev):
    c = GLA_CHUNK
    prv = pltpu.roll(la, c - 1 if rev else 1, 0)
    nxt = pltpu.roll(la, 1 if rev else c - 1, 0)
    s_tot = jnp.where(lvl == 0, _dot_nt(q.astype(BF16), jnp.concatenate([k.astype(BF16)] * H_A, 0) * hm_bf), 0.0)
    for kbit, m in enumerate(_GLA_LEVELS):
        up = ((pos >> kbit) & 1) == 1
        if m == 1:
            e = jnp.where(up, la, 0.0)
        elif m == 2:
            c4 = pos & 3
            e = jnp.where(c4 == 0, nxt, jnp.where(c4 == 1, 0.0, jnp.where(c4 == 2, la, la + prv)))
        else:
            nblk = c // (2 * m)
            loc = m if rev else m - 1
            b3 = b.reshape(nblk, 2 * m, 128)
            ref = jnp.broadcast_to(b3[:, loc:loc + 1, :], (nblk, 2 * m, 128)).reshape(c, 128)
            dlt = b - ref
            e = jnp.where(up, dlt, -dlt)
        xm = (jnp.where(up, q, k) * jnp.exp(e)).astype(BF16)
        sm = _dot_nt(xm, jnp.concatenate([xm] * H_A, 0) * hm_bf)
        s_tot = jnp.where(lvl == kbit + 1, sm, s_tot)
    return s_tot


def _gla_apply(s_tot, q, k, v, b, st_prev, hm_f32, vm_bf, rev):
    c = GLA_CHUNK
    vbd = jnp.concatenate([v] * H_A, 0) * vm_bf
    blast = b[0:1, :] if rev else b[c - 1:c, :]
    qbar = (q * jnp.exp(b)).astype(BF16)
    kdec = (k * jnp.exp(blast - b)).astype(BF16)
    o = _dot(s_tot.astype(BF16), vbd) + _dot_nt(qbar, st_prev.astype(BF16))
    st_new = st_prev * jnp.exp(blast) + _dot_tn(v, kdec) * hm_f32
    return o, st_new


def _gla_kernel(*refs, seq, has_s0, alias_in, spb):
    it = iter(refs)
    qk_ref, v_ref, la_ref, b_ref, gate_ref, g_ref = (next(it) for _ in range(6))
    s0_ref = next(it) if has_s0 else None
    for _ in range(alias_in):
        next(it)
    oa_ref = next(it)
    sfin_ref = None if has_s0 else next(it)
    acc_sc, st_sc, blk_sc = next(it), next(it), next(it)

    c = GLA_CHUNK
    nc = seq // c
    acc_sc[...] = jnp.zeros_like(acc_sc)
    if not has_s0 and not alias_in:
        sfin_ref[:, 1:] = jnp.zeros(sfin_ref[:, 1:].shape, F32)

    hrow = lax.broadcasted_iota(jnp.int32, (H_A * c, 128), 0) // c
    hm_f32 = jnp.where(hrow == lax.broadcasted_iota(jnp.int32, (H_A * c, 128), 1) // DK_A, 1.0, 0.0)
    hm_bf = hm_f32.astype(BF16)
    vrow = lax.broadcasted_iota(jnp.int32, (H_A * c, H_A * DV_A), 0) // c
    vm_bf = jnp.where(vrow == lax.broadcasted_iota(jnp.int32, (H_A * c, H_A * DV_A), 1) // DV_A,
                      1.0, 0.0).astype(BF16)
    consts = (_gla_consts(False), _gla_consts(True))

    def run_sequence(bb):
        base = bb * seq
        for d in range(2):
            if has_s0:
                blk_sc[...] = jnp.zeros_like(blk_sc)
                for hh in range(H_A):
                    blk_sc[DK_A * hh:DK_A * (hh + 1), DV_A * hh:DV_A * (hh + 1)] = s0_ref[bb, 0, d, hh]
                st_sc[d] = blk_sc[...].T
            else:
                st_sc[d] = jnp.zeros((ST_R, ST_C), F32)

        def chunk_rows(n, d):
            cn = (nc - 1 - n) if d else n
            start = base + cn * c
            return pl.ds(start if isinstance(start, int) else pl.multiple_of(start, c), c)

        def scores(n):
            out = []
            for d in range(2):
                rows = chunk_rows(n, d)
                pos, lvl = consts[d]
                out.append(_gla_scores(qk_ref[rows, 0:128], qk_ref[rows, 128:256],
                                       la_ref[rows, 128 * d:128 * d + 128],
                                       b_ref[rows, 128 * d:128 * d + 128], pos, lvl, hm_bf, bool(d)))
            return tuple(out)

        def apply(n, s_both):
            for d in range(2):
                rows = chunk_rows(n, d)
                o, st_new = _gla_apply(s_both[d], qk_ref[rows, 0:128], qk_ref[rows, 128:256],
                                       v_ref[rows, :].astype(BF16), b_ref[rows, 128 * d:128 * d + 128],
                                       st_sc[d], hm_f32, vm_bf, bool(d))
                acc_sc[rows, :] = acc_sc[rows, :] + o
                st_sc[d] = st_new

        def body(n, s_cur):
            s_next = scores(n + 1)
            apply(n, s_cur)
            return s_next

        trips = nc - 1
        s_last = lax.fori_loop(0, trips, body, scores(0),
                               unroll=next(u for u in (5, 3, 1) if trips % u == 0))
        apply(nc - 1, s_last)

        rs = slice(base, base + seq)
        first = lax.broadcasted_iota(jnp.int32, (seq, 128), 1) < DV_A
        for p in range(H_A // 2):
            cols = slice(128 * p, 128 * (p + 1))
            oa = acc_sc[rs, cols]
            sq = oa * oa
            s0 = jnp.sum(jnp.where(first, sq, 0.0), axis=-1, keepdims=True)
            s1 = jnp.sum(jnp.where(first, 0.0, sq), axis=-1, keepdims=True)
            inv = jnp.where(first, lax.rsqrt(s0 * (1.0 / DV_A) + EPS), lax.rsqrt(s1 * (1.0 / DV_A) + EPS))
            oa_ref[rs, cols] = (oa * inv * g_ref[0, :, cols] * gate_ref[rs, cols]).astype(BF16)
        if not has_s0:
            for d in range(2):
                blk_sc[...] = st_sc[d].T
                for hh in range(H_A):
                    sfin_ref[bb, 0, d, hh] = blk_sc[DK_A * hh:DK_A * (hh + 1), DV_A * hh:DV_A * (hh + 1)]

    for bb in range(spb):
        run_sequence(bb)


def _gla_call(qk, v, la, bsum, gate, g4, layer, *, seq, state_in=None, state_out=None):
    n = qk.shape[0]
    nb = n // seq
    has_s0 = state_in is not None
    assert has_s0 or state_out is not None or layer == 0
    spb = max(1, min(nb, GLA_GROUP_ROWS // seq))
    blk = lambda w: pl.BlockSpec((spb * seq, w), lambda i: (i, 0))
    nl = DEPTH if (not has_s0 and state_out is None) else 1
    st_spec = pl.BlockSpec((spb, nl, 2, H_A, DK_A, DV_A), lambda i: (i, layer, 0, 0, 0, 0))
    in_specs = [blk(256), blk(256), blk(256), blk(256), blk(W_A), _layer_spec(g4.shape, layer)]
    args = [qk, v, la, bsum, gate, g4]
    out_specs = [blk(W_A)]
    out_shape = [jax.ShapeDtypeStruct((n, W_A), BF16)]
    aliases = {}
    if has_s0:
        in_specs.append(st_spec)
        args.append(state_in)
    else:
        out_specs.append(st_spec)
        out_shape.append(jax.ShapeDtypeStruct((nb, DEPTH, 2, H_A, DK_A, DV_A), F32))
        if state_out is not None:
            aliases[len(args)] = 1
            in_specs.append(_ANY)
            args.append(state_out)
    return pl.pallas_call(
        functools.partial(_gla_kernel, seq=seq, has_s0=has_s0, alias_in=len(aliases), spb=spb),
        grid=(nb // spb,), in_specs=in_specs, out_specs=out_specs, out_shape=out_shape,
        input_output_aliases=aliases,
        scratch_shapes=[pltpu.VMEM((spb * seq, W_A), F32), pltpu.VMEM((2, ST_R, ST_C), F32),
                        pltpu.VMEM((ST_C, ST_R), F32)],
        compiler_params=_cparams(), name="gla_state" if has_s0 else "gla_ctx",
    )(*args)


def _softmax_t_pv(st, vt, ones_rows):
    dv, keys = vt.shape
    m = _col_reduce(st, jnp.max)
    e = jnp.exp2(st - m)
    if not ones_rows:
        l = _col_reduce(e, jnp.sum)
        return _dot(vt, e.astype(BF16)) * (1.0 / l)
    o = _dot(jnp.concatenate([vt, jnp.ones((16, keys), BF16)], axis=0), e.astype(BF16))
    return o[:dv] * (1.0 / o[dv:dv + 1])


def _pipelined_attention(score_fns, value_fns, depth):
    outs = []
    pending = [fn() for fn in score_fns[:depth]]
    for j, vfn in enumerate(value_fns):
        st = pending.pop(0)
        if j + depth < len(score_fns):
            pending.append(score_fns[j + depth]())
        outs.append(_softmax_t_pv(st, *vfn()))
    return outs


def _lookahead(n_keys):
    return 2 if n_keys >= 1024 else 3


def _col_reduce(x, op):
    rows, cols = x.shape
    part = 128 if rows % 128 == 0 and rows > 128 else rows
    if part != rows:
        x = op(x.reshape(rows // part, part, cols), axis=0)
    return op(x, axis=0, keepdims=True)


def _seqs_per_step(n, seq, ctx_len):
    if ctx_len or seq >= ATT_QBLOCK:
        return 1
    return max(1, min(n // seq, ATT_GROUP_ROWS // seq))


def _for_query_blocks(run, seq, spb):
    qb = min(ATT_QBLOCK, seq)
    nq = seq // qb
    if spb * nq <= ATT_STATIC_BLOCKS:
        run([(bb, slice(bb * seq + j * qb, bb * seq + (j + 1) * qb)) for bb in range(spb) for j in range(nq)])
    else:
        assert spb == 1

        def body(i, carry):
            run([(0, pl.ds(pl.multiple_of(i * qb, qb), qb))])
            return carry
        lax.fori_loop(0, nq, body, 0)


def _mla_part(q_ref, k_ref, vt_ref, gate_ref, ck_ref, cvt_ref, ob_ref, kk_sc, vt_sc, *, seq, ctx_len, spb):
    for bb in range(spb):
        rs = slice(bb * seq, (bb + 1) * seq)
        for p in range(H_B // 2):
            kk_sc[bb, p, 0:seq, 0:128] = k_ref[rs, 128 * p:128 * p + 128]
            kk_sc[bb, p, 0:seq, 128:256] = k_ref[rs, 512:640]
            if ctx_len:
                kk_sc[bb, p, seq:seq + ctx_len, 0:128] = ck_ref[0, 0, :, 128 * p:128 * p + 128]
                kk_sc[bb, p, seq:seq + ctx_len, 128:256] = ck_ref[0, 0, :, 512:640]
        vt_sc[bb, :, 0:seq] = vt_ref[:, rs]
        if ctx_len:
            vt_sc[bb, :, seq:seq + ctx_len] = cvt_ref[0, 0]

    def scores(bb, rows, h):
        p, hh = divmod(h, 2)
        qn = q_ref[rows, 128 * p:128 * p + 128] * _lane_mask(128, DN_B * hh, DN_B, BF16)
        qp = (q_ref[rows, 512 + 128 * (h // 4):512 + 128 * (h // 4) + 128]
              * _lane_mask(128, DR_B * (h % 4), DR_B, BF16))
        return _dot_nt(kk_sc[bb, p], jnp.concatenate([qn, qp], axis=-1))

    def items(blocks):
        its = [(bb, rows, h) for bb, rows in blocks for h in range(H_B)]

        def finish(outs):
            for j, (_, rows) in enumerate(blocks):
                ob = jnp.concatenate(outs[H_B * j:H_B * (j + 1)], axis=0).T
                ob_ref[rows, :] = (ob * gate_ref[rows, :]).astype(BF16)

        return ([functools.partial(scores, bb, rows, h) for bb, rows, h in its],
                [functools.partial(lambda bb, h: (vt_sc[bb, DV_B * h:DV_B * (h + 1), :], not ctx_len), bb, h)
                 for bb, _, h in its],
                finish)

    return items


def _diff_part(qk_ref, vt_ref, gate_ref, lam_ref, g_ref, ck_ref, cv_ref, oc_ref, k_sc, vt_sc, *,
               seq, ctx_len, lam_init, spb):
    lam = (jnp.exp(jnp.sum(lam_ref[0, 0:1, :] * lam_ref[0, 1:2, :], axis=-1, keepdims=True))
           - jnp.exp(jnp.sum(lam_ref[0, 2:3, :] * lam_ref[0, 3:4, :], axis=-1, keepdims=True)) + lam_init)
    dh = 2 * DC
    for bb in range(spb):
        rs = slice(bb * seq, (bb + 1) * seq)
        for p in range(H_C // 2):
            k_sc[bb, p, 0:seq, :] = qk_ref[rs, 256 + 128 * p:256 + 128 * p + 128]
            if ctx_len:
                pair_t = jnp.concatenate([ck_ref[0, 0, 2 * p], ck_ref[0, 0, 2 * p + 1]], axis=0)
                k_sc[bb, p, seq:seq + ctx_len, :] = pair_t.T.astype(BF16)
        vt_sc[bb, :, 0:seq] = vt_ref[:, rs]
        if ctx_len:
            for h in range(H_C):
                vt_sc[bb, dh * h:dh * (h + 1), seq:seq + ctx_len] = cv_ref[0, 0, h].astype(BF16)

    def scores(bb, rows, h, comp):
        p, hh = divmod(h, 2)
        qm = qk_ref[rows, 128 * p:128 * p + 128] * _lane_mask(128, dh * hh + DC * comp, DC, BF16)
        return _dot_nt(k_sc[bb, p], qm)

    def items(blocks):
        its = [(bb, rows, h, comp) for bb, rows in blocks for h in range(H_C) for comp in range(2)]

        def finish(o12):
            for j, (_, rows) in enumerate(blocks):
                outs = []
                for h in range(H_C):
                    o1, o2 = o12[2 * (H_C * j + h)], o12[2 * (H_C * j + h) + 1]
                    ot = o1 - lam * o2
                    outs.append(ot * lax.rsqrt(jnp.mean(ot * ot, axis=0, keepdims=True) + EPS))
                oc = jnp.concatenate(outs, axis=0).T
                oc_ref[rows, :] = (oc * g_ref[0] * (1.0 - lam_init) * gate_ref[rows, :]).astype(BF16)

        return ([functools.partial(scores, *item) for item in its],
                [functools.partial(lambda bb, h: (vt_sc[bb, dh * h:dh * (h + 1), :], True), bb, h)
                 for bb, _, h, _ in its],
                finish)

    return items


def _attn_kernel(*refs, seq, ctx_len, lam_init, spb, mla, diff):
    it = iter(refs)
    n_ctx = 2 if ctx_len else 0
    mla_in = [next(it) for _ in range(4 + n_ctx)] + [None] * (2 - n_ctx) if mla else None
    diff_in = [next(it) for _ in range(5 + n_ctx)] + [None] * (2 - n_ctx) if diff else None
    ob_ref = next(it) if mla else None
    oc_ref = next(it) if diff else None
    parts = []
    if mla:
        parts.append(_mla_part(*mla_in, ob_ref, next(it), next(it), seq=seq, ctx_len=ctx_len, spb=spb))
    if diff:
        parts.append(_diff_part(*diff_in, oc_ref, next(it), next(it), seq=seq, ctx_len=ctx_len,
                                lam_init=lam_init, spb=spb))

    def run(blocks):
        built = [part(blocks) for part in parts]
        outs = _pipelined_attention([f for b in built for f in b[0]], [f for b in built for f in b[1]],
                                    _lookahead(seq + ctx_len))
        lo = 0
        for score_fns, _, finish in built:
            finish(outs[lo:lo + len(score_fns)])
            lo += len(score_fns)

    _for_query_blocks(run, seq, spb)


def _attn_call(mq, mk, mvt, gate_b, dqk, dvt, gate_c, lamp, g4, ctx, layer, *, seq):
    n = mq.shape[0]
    ctx_len = 0 if ctx is None else ctx["mla_k"].shape[2]
    lam_init = 0.8 - 0.6 * math.exp(-0.3 * layer)
    spb = _seqs_per_step(n, seq, ctx_len)
    rows = spb * seq
    tk = seq + ctx_len
    rblk = lambda w: pl.BlockSpec((rows, w), lambda i: (i, 0))
    tblk = lambda w: pl.BlockSpec((w, rows), lambda i: (0, i))
    mla_specs, mla_args = [rblk(768), rblk(640), tblk(W_B), rblk(W_B)], [mq, mk, mvt, gate_b]
    diff_specs = [rblk(512), tblk(W_C), rblk(W_C), _layer_spec(lamp.shape, layer), _layer_spec(g4.shape, layer)]
    diff_args = [dqk, dvt, gate_c, lamp, g4]
    if ctx_len:
        mla_specs += [pl.BlockSpec((1, 1, ctx_len, 640), lambda i: (layer, i, 0, 0)),
                      pl.BlockSpec((1, 1, W_B, ctx_len), lambda i: (layer, i, 0, 0))]
        mla_args += [ctx["mla_k"], ctx["mla_vt"]]
        diff_specs += [pl.BlockSpec((1, 1, H_C, 2 * DC, ctx_len), lambda i: (i, layer, 0, 0, 0))] * 2
        diff_args += [ctx["diff_k_t"], ctx["diff_v_t"]]
    mla_out = (rblk(W_B), jax.ShapeDtypeStruct((n, W_B), BF16))
    diff_out = (rblk(W_C), jax.ShapeDtypeStruct((n, W_C), BF16))
    mla_scratch = [pltpu.VMEM((spb, H_B // 2, tk, 256), BF16), pltpu.VMEM((spb, W_B, tk), BF16)]
    diff_scratch = [pltpu.VMEM((spb, H_C // 2, tk, 128), BF16), pltpu.VMEM((spb, W_C, tk), BF16)]

    def call(mla, diff, name):
        outs = ([mla_out] if mla else []) + ([diff_out] if diff else [])
        return pl.pallas_call(
            functools.partial(_attn_kernel, seq=seq, ctx_len=ctx_len, lam_init=lam_init, spb=spb,
                              mla=mla, diff=diff),
            grid=(n // rows,),
            in_specs=(mla_specs if mla else []) + (diff_specs if diff else []),
            out_specs=[o[0] for o in outs], out_shape=[o[1] for o in outs],
            scratch_shapes=(mla_scratch if mla else []) + (diff_scratch if diff else []),
            compiler_params=_cparams(), name=name,
        )(*(mla_args if mla else []), *(diff_args if diff else []))

    if ctx_len:
        (ob,) = call(True, False, "mla_ctx")
        (oc,) = call(False, True, "diff_ctx")
        return ob, oc
    return call(True, True, "attn_self")


def _post_kernel(oa_ref, ob_ref, oc_ref, x_ref, mod_ref, w_ref, g_ref, y_ref):
    mix = jnp.concatenate([oa_ref[...], ob_ref[...], oc_ref[...]], axis=-1)
    out = _rms(_dot(mix, w_ref[0]), g_ref[0])
    y_ref[...] = x_ref[...] + mod_ref[0, :, 2 * D_MODEL:3 * D_MODEL] * out


def _post_call(oa, ob, oc, x2d, mod, pw, layer, *, seq, sample):
    n = x2d.shape[0]
    tm = min(POST_BLOCK, seq if sample else n)
    blk = lambda w: pl.BlockSpec((tm, w), lambda i: (i, 0))

    def mod_idx(i):
        return (layer * MOD_ROWS + ((i * tm) // seq + 1 if sample else 0), 0, 0)

    return pl.pallas_call(
        _post_kernel,
        grid=(n // tm,),
        in_specs=[blk(W_A), blk(W_B), blk(W_C), blk(D_MODEL),
                  pl.BlockSpec((1, 1, 3 * D_MODEL), mod_idx),
                  _layer_spec(pw["w_out"].shape, layer), _layer_spec(pw["g_post"].shape, layer)],
        out_specs=blk(D_MODEL),
        out_shape=jax.ShapeDtypeStruct((n, D_MODEL), F32),
        compiler_params=_cparams(), name="post",
    )(oa, ob, oc, x2d, mod, pw["w_out"], pw["g_post"])


def _pack_params(g_pre, g_post, w_in, w_gla_af, b_gla_af, w_gla_ab, b_gla_ab, g_gla, g_mla_q, w_mla_uq,
                 g_mla_kv, w_mla_ukv, lam_q1, lam_k1, lam_q2, lam_k2, g_diff, w_out):
    w_t = jnp.swapaxes(w_in, 1, 2)
    ga0, ga1 = _IN["ga"], _IN["gg"]
    w_t = jnp.concatenate([w_t[:, 0:ga0], w_t[:, ga1:_IN["mg"]], w_t[:, ga0:ga1],
                           jnp.zeros((DEPTH, 64, D_MODEL), F32), w_t[:, _IN["mg"]:]], axis=1)
    assert w_t.shape[1] == R_DIFF[1]
    zg = jnp.zeros((DEPTH, GLA_LR, 128), F32)
    w_gate = jnp.concatenate([jnp.zeros((DEPTH, DR_B, 256), F32),
                              jnp.concatenate([w_gla_af, zg], axis=-1),
                              jnp.concatenate([zg, w_gla_ab], axis=-1),
                              jnp.zeros((DEPTH, 128 - DR_B - 2 * GLA_LR, 256), F32)], axis=1).astype(BF16)
    uq = w_mla_uq.reshape(DEPTH, Q_LORA, H_B, DN_B + DR_B)
    w_pe = uq[..., DN_B:].reshape(DEPTH, Q_LORA, H_B * DR_B)
    w_uq = jnp.concatenate([uq[..., :DN_B].reshape(DEPTH, Q_LORA, H_B * DN_B), w_pe], axis=-1).astype(BF16)
    ukv = w_mla_ukv.reshape(DEPTH, KV_LORA, H_B, DN_B + DV_B)
    w_ukv = jnp.concatenate([ukv[..., :DN_B].reshape(DEPTH, KV_LORA, H_B * DN_B),
                             ukv[..., DN_B:].reshape(DEPTH, KV_LORA, H_B * DV_B)], axis=-1).astype(BF16)
    row = lambda a: a.reshape(DEPTH, 1, a.shape[-1])
    return dict(
        w_t=w_t.astype(BF16), w_gate=w_gate,
        b_gate=row(jnp.concatenate([b_gla_af, b_gla_ab], axis=-1)),
        w_uq=w_uq, w_ukv=w_ukv, w_out=w_out.astype(BF16),
        g_pre=row(g_pre), g_post=row(g_post), g_mla_q=row(g_mla_q), g_mla_kv=row(g_mla_kv),
        g_gla4=row(jnp.tile(g_gla, (1, H_A))), g_diff4=row(jnp.tile(g_diff, (1, H_C))),
        lam=jnp.stack([lam_q1, lam_k1, lam_q2, lam_k2], axis=1))


def _rope_tables(n):
    t = np.arange(n)
    row = (t // GRID_W).astype(np.float32)
    col = (t % GRID_W).astype(np.float32)
    half = ROPE_DIM // 2
    inv = (1.0 / (np.float32(ROPE_THETA) ** (np.arange(0, half, 2, dtype=np.float32) / np.float32(half)))
           ).astype(np.float32)
    ar = row[:, None] * inv
    ac = col[:, None] * inv
    ang = np.concatenate([ar, ar, ac, ac], axis=-1).astype(np.float32)
    return (jnp.asarray(np.tile(np.cos(ang), (1, 8)).astype(np.float32)),
            jnp.asarray(np.tile(np.sin(ang), (1, 8)).astype(np.float32)))


def _sublayer(x2d, mod, pw, layer, *, seq, rope_tabs, ctx, caches):
    sample = ctx is not None
    pre = _pre_call(x2d, mod, pw, layer, seq=seq, rope_tabs=rope_tabs,
                    caches=None if sample else caches[:4])
    qk, v, la, bsum, ga, gb, gc, mq, mk, mvt, dqk, dvt = pre[:12]
    if sample:
        (oa,) = _gla_call(qk, v, la, bsum, ga, pw["g_gla4"], layer, seq=seq, state_in=ctx["state"])
        new_caches = None
    else:
        oa, sfin = _gla_call(qk, v, la, bsum, ga, pw["g_gla4"], layer, seq=seq,
                             state_out=caches[4] if caches else None)
        new_caches = tuple(pre[12:]) + (sfin,)
    ob, oc = _attn_call(mq, mk, mvt, gb, dqk, dvt, gc, pw["lam"], pw["g_diff4"], ctx, layer, seq=seq)
    y = _post_call(oa, ob, oc, x2d, mod, pw, layer, seq=seq, sample=sample)
    return y, new_caches


def kernel(x_prompt, x_sample, c, cache_mla_ckv, cache_mla_kpe, cache_diff_k, cache_diff_v, state_gla,
           c_ctx, w_ada, b_ada, g_pre, g_post, w_in, w_gla_af, b_gla_af, w_gla_ab, b_gla_ab, g_gla,
           g_mla_q, w_mla_uq, g_mla_kv, w_mla_ukv, lam_q1, lam_k1, lam_q2, lam_k2, g_diff, w_out):
    bp, tp, d = x_prompt.shape
    bs, ts, _ = x_sample.shape

    pw = _pack_params(g_pre, g_post, w_in, w_gla_af, b_gla_af, w_gla_ab, b_gla_ab, g_gla, g_mla_q,
                      w_mla_uq, g_mla_kv, w_mla_ukv, lam_q1, lam_k1, lam_q2, lam_k2, g_diff, w_out)
    cvecs = jnp.concatenate([c_ctx[None], c, jnp.zeros((MOD_ROWS - 1 - bs, d), F32)], axis=0)
    mod = _mod_call(cvecs, w_ada, b_ada).reshape(DEPTH * MOD_ROWS, 1, 3 * d)
    rope_tabs = _rope_tables(ts)
    ctx_k, ctx_vt = _ctxkv_call(cache_mla_ckv, jnp.swapaxes(cache_mla_kpe, -1, -2), pw["w_ukv"])
    ctx = dict(state=state_gla, mla_k=ctx_k, mla_vt=ctx_vt,
               diff_k_t=jnp.swapaxes(cache_diff_k, -1, -2), diff_v_t=jnp.swapaxes(cache_diff_v, -1, -2))

    y_p = x_prompt.reshape(bp * tp, d)
    y_s = x_sample.reshape(bs * ts, d)
    caches = ()
    for l in range(DEPTH):
        y_p, caches = _sublayer(y_p, mod, pw, l, seq=tp, rope_tabs=None, ctx=None, caches=caches)
        y_s, _ = _sublayer(y_s, mod, pw, l, seq=ts, rope_tabs=rope_tabs, ctx=ctx, caches=None)
    ckvn, kpe_t, kc_t, vc_t, new_state = caches
    return (y_p.reshape(bp, tp, d), y_s.reshape(bs, ts, d), ckvn, jnp.swapaxes(kpe_t, -1, -2),
            jnp.swapaxes(kc_t, -1, -2), jnp.swapaxes(vc_t, -1, -2), new_state)
```

```python
import functools
import math

import numpy as np
import jax
import jax.numpy as jnp
from jax import lax
from jax.experimental import pallas as pl
from jax.experimental.pallas import tpu as pltpu

F32 = jnp.float32
BF16 = jnp.bfloat16

D_MODEL = 1024
DEPTH = 2
GRID_W = 64
EPS = 1e-6
ROPE_THETA = 10000.0
ROPE_DIM = 32
H_A, DK_A, DV_A = 4, 32, 64
GLA_LR = 16
GLA_TAU = 16.0
GLA_CHUNK = 64
H_B, DN_B, DR_B, DV_B = 8, 64, 32, 64
Q_LORA, KV_LORA = 256, 128
H_C, DC = 4, 32
W_A, W_B, W_C = H_A * DV_A, H_B * DV_B, H_C * 2 * DC
ST_R, ST_C = H_A * DV_A, H_A * DK_A
LOG2E = math.log2(math.e)
SUBLANES = 8

R_QKV = ((0, 512),)
R_GG = ((544, 800),)
R_MLA = ((800, 1216), (512, 544), 64)
R_MG = ((1216, 1728),)
R_DIFF = ((1728, 2752),)
MOD_ROWS = 8

V7X_VMEM_LIMIT_BYTES = 56 * 1024 * 1024
TOKEN_BLOCK = 1024
POST_BLOCK = 1024
ATT_QBLOCK = 512
ATT_GROUP_ROWS = 1024
GLA_GROUP_ROWS = 1024
ATT_STATIC_BLOCKS = 4


def _cparams(n_axes=1):
    return pltpu.CompilerParams(dimension_semantics=("arbitrary",) * n_axes,
                                vmem_limit_bytes=V7X_VMEM_LIMIT_BYTES)


def _rms(x, g):
    return x * lax.rsqrt(jnp.mean(x * x, axis=-1, keepdims=True) + EPS) * g


def _silu(x):
    return x * jax.nn.sigmoid(x)


def _log_sigmoid(x):
    return jnp.minimum(x, 0.0) - jnp.log1p(jnp.exp(-jnp.abs(x)))


def _rope(z, cos, sin):
    w = z.shape[-1]
    lane = lax.broadcasted_iota(jnp.int32, z.shape, 1)
    rot = jnp.where((lane & 15) < 8, -pltpu.roll(z, w - 8, 1), pltpu.roll(z, 8, 1))
    return z * cos + rot * sin


def _lane_mask(width, lo, size, dtype):
    lane = lax.broadcasted_iota(jnp.int32, (1, width), 1)
    return jnp.where((lane >= lo) & (lane < lo + size), 1.0, 0.0).astype(dtype)


def _dot(a, b):
    return jnp.dot(a, b, preferred_element_type=F32)


def _dot_nt(a, b):
    return lax.dot_general(a, b, (((1,), (1,)), ((), ())), preferred_element_type=F32)


def _dot_tn(a, b):
    return lax.dot_general(a, b, (((0,), (0,)), ((), ())), preferred_element_type=F32)


def _layer_spec(shape, layer):
    nd = len(shape)
    return pl.BlockSpec((1,) + tuple(shape[1:]), lambda *_: (layer,) + (0,) * (nd - 1))


_ANY = pl.BlockSpec(memory_space=pl.ANY)


def _mod_kernel(c_ref, w_ref, b_ref, o_ref):
    s = _silu(c_ref[...]).astype(BF16)
    layer = pl.program_id(0)
    bias = b_ref[0:1, :]
    for r in range(1, DEPTH):
        bias = jnp.where(layer == r, b_ref[r:r + 1, :], bias)
    mod = _dot(s, w_ref[0].astype(BF16)) + bias
    for r in range(MOD_ROWS):
        o_ref[r] = mod[r:r + 1, :]


def _mod_call(cvecs, w_ada, b_ada):
    nb = 1024
    return pl.pallas_call(
        _mod_kernel,
        grid=(DEPTH, 3 * D_MODEL // nb),
        in_specs=[pl.BlockSpec((MOD_ROWS, D_MODEL), lambda l, j: (0, 0)),
                  pl.BlockSpec((1, D_MODEL, nb), lambda l, j: (l, 0, j)),
                  pl.BlockSpec((DEPTH, nb), lambda l, j: (0, j))],
        out_specs=pl.BlockSpec((MOD_ROWS, 1, nb), lambda l, j: (l, 0, j)),
        out_shape=jax.ShapeDtypeStruct((DEPTH * MOD_ROWS, 1, 3 * D_MODEL), F32),
        compiler_params=_cparams(2), name="adaln_mod",
    )(cvecs, w_ada, b_ada)


def _ctxkv_kernel(ckv_ref, kpe_ref, w_ref, k_ref, vt_ref):
    kv = _dot(ckv_ref[0, 0].astype(BF16), w_ref[0])
    kpe4 = jnp.concatenate([kpe_ref[0, 0]] * 4, axis=0).T
    k_ref[0, 0, :, 0:512] = kv[:, 0:512].astype(BF16)
    k_ref[0, 0, :, 512:640] = kpe4.astype(BF16)
    vt_ref[0, 0] = kv[:, 512:1024].T.astype(BF16)


def _ctxkv_call(cache_ckv, cache_kpe_t, wukv):
    nb, _, tc, _ = cache_ckv.shape
    return pl.pallas_call(
        _ctxkv_kernel,
        grid=(DEPTH, nb),
        in_specs=[pl.BlockSpec((1, 1, tc, KV_LORA), lambda l, b: (b, l, 0, 0)),
                  pl.BlockSpec((1, 1, DR_B, tc), lambda l, b: (b, l, 0, 0)),
                  pl.BlockSpec((1, KV_LORA, 1024), lambda l, b: (l, 0, 0))],
        out_specs=[pl.BlockSpec((1, 1, tc, 640), lambda l, b: (l, b, 0, 0)),
                   pl.BlockSpec((1, 1, W_B, tc), lambda l, b: (l, b, 0, 0))],
        out_shape=[jax.ShapeDtypeStruct((DEPTH, nb, tc, 640), BF16),
                   jax.ShapeDtypeStruct((DEPTH, nb, W_B, tc), BF16)],
        compiler_params=_cparams(2), name="mla_ctx_kv",
    )(cache_ckv, cache_kpe_t, wukv)


def _pre_kernel(*refs, rope, ctx_out, alias_in, bpb, seq):
    it = iter(refs)
    (x_ref, mod_ref, gpre_ref, w_ref, wg_ref, bg_ref, gq_ref, gkv_ref, wuq_ref,
     wukv_ref) = (next(it) for _ in range(10))
    if rope:
        cos_ref, sin_ref = next(it), next(it)
    for _ in range(alias_in):
        next(it)
    (qk_ref, v_ref, la_ref, bs_ref, ga_ref, gb_ref, gc_ref, mq_ref, mk_ref, mvt_ref, dqk_ref,
     dvt_ref) = (next(it) for _ in range(12))
    if ctx_out:
        ckvn_ref, kpe_ref, kc_ref, vc_ref = (next(it) for _ in range(4))

    d = D_MODEL
    shift = mod_ref[0, :, 0:d]
    scale = mod_ref[0, :, d:2 * d]
    h = (_rms(x_ref[...], gpre_ref[0]) * (1.0 + scale) + shift).astype(BF16)
    def proj(group):
        parts = [jnp.zeros((r, d), BF16) if isinstance(r, int) else w_ref[0, r[0]:r[1], :] for r in group]
        return _dot_nt(h, parts[0] if len(parts) == 1 else jnp.concatenate(parts, axis=0))
    if rope:
        cos = cos_ref[...]
        sin = sin_ref[...]

    pg = proj(R_QKV)
    qk_ref[:, 0:128] = pg[:, 0:128] * (DK_A ** -0.5)
    qk_ref[:, 128:256] = pg[:, 128:256]
    v_ref[...] = pg[:, 256:512]
    pm = proj(R_MLA)
    tail = pm[:, 384:512]
    xg = _dot(tail.astype(BF16), wg_ref[0]) + bg_ref[0]
    la = _log_sigmoid(xg) * (1.0 / GLA_TAU)
    la_ref[...] = la
    bs_ref[:, 0:128] = _chunk_scan(la[:, 0:128], False)
    bs_ref[:, 128:256] = _chunk_scan(la[:, 128:256], True)
    ga_ref[...] = _silu(proj(R_GG)).astype(BF16)

    qall = _dot(_rms(pm[:, 0:256], gq_ref[0]).astype(BF16), wuq_ref[0])
    q_pe = qall[:, 512:768]
    if rope:
        q_pe = _rope(q_pe, cos, sin)
    sb = (DN_B + DR_B) ** -0.5 * LOG2E
    mq_ref[:, 0:512] = (qall[:, 0:512] * sb).astype(BF16)
    mq_ref[:, 512:768] = (q_pe * sb).astype(BF16)
    ckvn = _rms(pm[:, 256:384], gkv_ref[0])
    kvall = _dot(ckvn.astype(BF16), wukv_ref[0])
    lane = lax.broadcasted_iota(jnp.int32, tail.shape, 1)
    kpe4 = jnp.where(lane < DR_B, tail, 0.0)
    kpe4 = kpe4 + pltpu.roll(kpe4, DR_B, 1)
    kpe4 = kpe4 + pltpu.roll(kpe4, 2 * DR_B, 1)
    if rope:
        kpe4 = _rope(kpe4, cos[:, 0:128], sin[:, 0:128])
    mk_ref[:, 0:512] = kvall[:, 0:512].astype(BF16)
    mk_ref[:, 512:640] = kpe4.astype(BF16)
    mvt_ref[...] = kvall[:, 512:1024].T.astype(BF16)
    gb_ref[...] = _silu(proj(R_MG)).astype(BF16)

    pd = proj(R_DIFF)
    dq, dk, dv = pd[:, 0:256], pd[:, 256:512], pd[:, 512:768]
    if rope:
        dq = _rope(dq, cos, sin)
        dk = _rope(dk, cos, sin)
    dqk_ref[:, 0:256] = (dq * (DC ** -0.5 * LOG2E)).astype(BF16)
    dqk_ref[:, 256:512] = dk.astype(BF16)
    dv_t = dv.T
    dvt_ref[...] = dv_t.astype(BF16)
    gc_ref[...] = _silu(pd[:, 768:1024]).astype(BF16)
    if ctx_out:
        kpe_t = kpe4.T
        dk_t = dk.T
        for ref in () if alias_in else (ckvn_ref, kpe_ref, kc_ref, vc_ref):
            ref[:, 1:] = jnp.zeros(ref[:, 1:].shape, F32)
        for bb in range(bpb):
            rs = slice(bb * seq, (bb + 1) * seq)
            ckvn_ref[bb, 0] = ckvn[rs]
            kpe_ref[bb, 0] = kpe_t[0:DR_B, rs]
            kc_ref[bb, 0] = dk_t[:, rs].reshape(H_C, 2 * DC, seq)
            vc_ref[bb, 0] = dv_t[:, rs].reshape(H_C, 2 * DC, seq)


def _pre_call(x2d, mod, pw, layer, *, seq, rope_tabs, caches):
    n = x2d.shape[0]
    tm = min(TOKEN_BLOCK, n)
    bpb = max(tm // seq, 1)
    rope = rope_tabs is not None
    ctx_out = caches is not None
    steps_per_seq = max(seq // tm, 1)
    nbt = n // seq

    def mod_idx(i):
        return (layer * MOD_ROWS + ((i * tm) // seq + 1 if rope else 0), 0, 0)

    names = ["g_pre", "w_t", "w_gate", "b_gate", "g_mla_q", "g_mla_kv", "w_uq", "w_ukv"]
    in_specs = [pl.BlockSpec((tm, D_MODEL), lambda i: (i, 0)), pl.BlockSpec((1, 1, 3 * D_MODEL), mod_idx)]
    in_specs += [_layer_spec(pw[k].shape, layer) for k in names]
    args = [x2d, mod] + [pw[k] for k in names]
    if rope:
        in_specs += [pl.BlockSpec((tm, 256), lambda i: (i % steps_per_seq, 0))] * 2
        args += list(rope_tabs)
    outs = [(256, F32, False), (256, F32, False), (256, F32, False), (256, F32, False),
            (W_A, BF16, False), (W_B, BF16, False), (W_C, BF16, False),
            (768, BF16, False), (640, BF16, False), (W_B, BF16, True),
            (512, BF16, False), (W_C, BF16, True)]
    out_specs = [pl.BlockSpec((w, tm), lambda i: (0, i)) if tr else pl.BlockSpec((tm, w), lambda i: (i, 0))
                 for w, _, tr in outs]
    out_shape = [jax.ShapeDtypeStruct((w, n) if tr else (n, w), dt) for w, dt, tr in outs]
    widths = outs
    aliases = {}
    if ctx_out:
        assert caches or layer == 0
        nl = 1 if caches else DEPTH
        out_specs += [pl.BlockSpec((bpb, nl, seq, KV_LORA), lambda i: (i, layer, 0, 0)),
                      pl.BlockSpec((bpb, nl, DR_B, seq), lambda i: (i, layer, 0, 0)),
                      pl.BlockSpec((bpb, nl, H_C, 2 * DC, seq), lambda i: (i, layer, 0, 0, 0)),
                      pl.BlockSpec((bpb, nl, H_C, 2 * DC, seq), lambda i: (i, layer, 0, 0, 0))]
        out_shape += [jax.ShapeDtypeStruct((nbt, DEPTH, seq, KV_LORA), F32),
                      jax.ShapeDtypeStruct((nbt, DEPTH, DR_B, seq), F32),
                      jax.ShapeDtypeStruct((nbt, DEPTH, H_C, 2 * DC, seq), F32),
                      jax.ShapeDtypeStruct((nbt, DEPTH, H_C, 2 * DC, seq), F32)]
        for j, arr in enumerate(caches):
            aliases[len(args)] = len(widths) + j
            in_specs.append(_ANY)
            args.append(arr)
    return pl.pallas_call(
        functools.partial(_pre_kernel, rope=rope, ctx_out=ctx_out, alias_in=len(aliases), bpb=bpb, seq=seq),
        grid=(n // tm,), in_specs=in_specs, out_specs=out_specs, out_shape=out_shape,
        input_output_aliases=aliases,
        compiler_params=_cparams(), name="pre_rope" if rope else "pre_ctx",
    )(*args)


_GLA_LEVELS = (1, 2, 4, 8, 16, 32)


def _gla_consts(rev):
    c = GLA_CHUNK
    row = lax.broadcasted_iota(jnp.int32, (c, 128), 0)
    pos = (c - 1 - row) if rev else row
    ri = lax.broadcasted_iota(jnp.int32, (c, H_A * c), 0)
    cj = lax.broadcasted_iota(jnp.int32, (c, H_A * c), 1) & (c - 1)
    pi = (c - 1 - ri) if rev else ri
    pj = (c - 1 - cj) if rev else cj
    x = pi ^ pj
    lvl = jnp.where(pi == pj, 0, -1)
    for kbit in range(6):
        lvl = jnp.where((pj < pi) & ((x >> kbit) == 1), kbit + 1, lvl)
    return pos, lvl


def _chunk_scan(x, rev):
    rows = x.shape[0]
    nt = rows // SUBLANES
    tiles_per_chunk = GLA_CHUNK // SUBLANES
    x3 = x.reshape(nt, SUBLANES, 128)
    sub = lax.broadcasted_iota(jnp.int32, x3.shape, 1)
    tile = lax.broadcasted_iota(jnp.int32, x3.shape, 0) & (tiles_per_chunk - 1)
    edge = 0 if rev else SUBLANES - 1
    s = 1
    while s < SUBLANES:
        if rev:
            x3 = x3 + jnp.where(sub < SUBLANES - s, pltpu.roll(x3, SUBLANES - s, 1), 0.0)
        else:
            x3 = x3 + jnp.where(sub >= s, pltpu.roll(x3, s, 1), 0.0)
        s *= 2
    s = 1
    while s < tiles_per_chunk:
        tot = jnp.broadcast_to(x3[:, edge:edge + 1, :], x3.shape)
        if rev:
            shifted = jnp.concatenate([tot[s:], tot[:s]], axis=0)
            x3 = x3 + jnp.where(tile < tiles_per_chunk - s, shifted, 0.0)
        else:
            shifted = jnp.concatenate([tot[nt - s:], tot[:nt - s]], axis=0)
            x3 = x3 + jnp.where(tile >= s, shifted, 0.0)
        s *= 2
    return x3.reshape(rows, 128)


def _gla_scores(q, k, la, b, pos, lvl, hm_bf, rev):
    c = GLA_CHUNK
    prv = pltpu.roll(la, c - 1 if rev else 1, 0)
    nxt = pltpu.roll(la, 1 if rev else c - 1, 0)
    s_tot = jnp.where(lvl == 0, _dot_nt(q.astype(BF16), jnp.concatenate([k.astype(BF16)] * H_A, 0) * hm_bf), 0.0)
    for kbit, m in enumerate(_GLA_LEVELS):
        up = ((pos >> kbit) & 1) == 1
        if m == 1:
            e = jnp.where(up, la, 0.0)
        elif m == 2:
            c4 = pos & 3
            e = jnp.where(c4 == 0, nxt, jnp.where(c4 == 1, 0.0, jnp.where(c4 == 2, la, la + prv)))
        else:
            nblk = c // (2 * m)
            loc = m if rev else m - 1
            b3 = b.reshape(nblk, 2 * m, 128)
            ref = jnp.broadcast_to(b3[:, loc:loc + 1, :], (nblk, 2 * m, 128)).reshape(c, 128)
            dlt = b - ref
            e = jnp.where(up, dlt, -dlt)
        xm = (jnp.where(up, q, k) * jnp.exp(e)).astype(BF16)
        sm = _dot_nt(xm, jnp.concatenate([xm] * H_A, 0) * hm_bf)
        s_tot = jnp.where(lvl == kbit + 1, sm, s_tot)
    return s_tot


def _gla_apply(s_tot, q, k, v, b, st_prev, hm_f32, vm_bf, rev):
    c = GLA_CHUNK
    vbd = jnp.concatenate([v] * H_A, 0) * vm_bf
    blast = b[0:1, :] if rev else b[c - 1:c, :]
    qbar = (q * jnp.exp(b)).astype(BF16)
    kdec = (k * jnp.exp(blast - b)).astype(BF16)
    o = _dot(s_tot.astype(BF16), vbd) + _dot_nt(qbar, st_prev.astype(BF16))
    st_new = st_prev * jnp.exp(blast) + _dot_tn(v, kdec) * hm_f32
    return o, st_new


def _gla_kernel(*refs, seq, has_s0, alias_in, spb):
    it = iter(refs)
    qk_ref, v_ref, la_ref, b_ref, gate_ref, g_ref = (next(it) for _ in range(6))
    s0_ref = next(it) if has_s0 else None
    for _ in range(alias_in):
        next(it)
    oa_ref = next(it)
    sfin_ref = None if has_s0 else next(it)
    acc_sc, st_sc, blk_sc = next(it), next(it), next(it)

    c = GLA_CHUNK
    nc = seq // c
    acc_sc[...] = jnp.zeros_like(acc_sc)
    if not has_s0 and not alias_in:
        sfin_ref[:, 1:] = jnp.zeros(sfin_ref[:, 1:].shape, F32)

    hrow = lax.broadcasted_iota(jnp.int32, (H_A * c, 128), 0) // c
    hm_f32 = jnp.where(hrow == lax.broadcasted_iota(jnp.int32, (H_A * c, 128), 1) // DK_A, 1.0, 0.0)
    hm_bf = hm_f32.astype(BF16)
    vrow = lax.broadcasted_iota(jnp.int32, (H_A * c, H_A * DV_A), 0) // c
    vm_bf = jnp.where(vrow == lax.broadcasted_iota(jnp.int32, (H_A * c, H_A * DV_A), 1) // DV_A,
                      1.0, 0.0).astype(BF16)
    consts = (_gla_consts(False), _gla_consts(True))

    def run_sequence(bb):
        base = bb * seq
        for d in range(2):
            if has_s0:
                blk_sc[...] = jnp.zeros_like(blk_sc)
                for hh in range(H_A):
                    blk_sc[DK_A * hh:DK_A * (hh + 1), DV_A * hh:DV_A * (hh + 1)] = s0_ref[bb, 0, d, hh]
                st_sc[d] = blk_sc[...].T
            else:
                st_sc[d] = jnp.zeros((ST_R, ST_C), F32)

        def chunk_rows(n, d):
            cn = (nc - 1 - n) if d else n
            start = base + cn * c
            return pl.ds(start if isinstance(start, int) else pl.multiple_of(start, c), c)

        def scores(n):
            out = []
            for d in range(2):
                rows = chunk_rows(n, d)
                pos, lvl = consts[d]
                out.append(_gla_scores(qk_ref[rows, 0:128], qk_ref[rows, 128:256],
                                       la_ref[rows, 128 * d:128 * d + 128],
                                       b_ref[rows, 128 * d:128 * d + 128], pos, lvl, hm_bf, bool(d)))
            return tuple(out)

        def apply(n, s_both):
            for d in range(2):
                rows = chunk_rows(n, d)
                o, st_new = _gla_apply(s_both[d], qk_ref[rows, 0:128], qk_ref[rows, 128:256],
                                       v_ref[rows, :].astype(BF16), b_ref[rows, 128 * d:128 * d + 128],
                                       st_sc[d], hm_f32, vm_bf, bool(d))
                acc_sc[rows, :] = acc_sc[rows, :] + o
                st_sc[d] = st_new

        def body(n, s_cur):
            s_next = scores(n + 1)
            apply(n, s_cur)
            return s_next

        trips = nc - 1
        s_last = lax.fori_loop(0, trips, body, scores(0),
                               unroll=next(u for u in (5, 3, 1) if trips % u == 0))
        apply(nc - 1, s_last)

        rs = slice(base, base + seq)
        first = lax.broadcasted_iota(jnp.int32, (seq, 128), 1) < DV_A
        for p in range(H_A // 2):
            cols = slice(128 * p, 128 * (p + 1))
            oa = acc_sc[rs, cols]
            sq = oa * oa
            s0 = jnp.sum(jnp.where(first, sq, 0.0), axis=-1, keepdims=True)
            s1 = jnp.sum(jnp.where(first, 0.0, sq), axis=-1, keepdims=True)
            inv = jnp.where(first, lax.rsqrt(s0 * (1.0 / DV_A) + EPS), lax.rsqrt(s1 * (1.0 / DV_A) + EPS))
            oa_ref[rs, cols] = (oa * inv * g_ref[0, :, cols] * gate_ref[rs, cols]).astype(BF16)
        if not has_s0:
            for d in range(2):
                blk_sc[...] = st_sc[d].T
                for hh in range(H_A):
                    sfin_ref[bb, 0, d, hh] = blk_sc[DK_A * hh:DK_A * (hh + 1), DV_A * hh:DV_A * (hh + 1)]

    for bb in range(spb):
        run_sequence(bb)


def _gla_call(qk, v, la, bsum, gate, g4, layer, *, seq, state_in=None, state_out=None):
    n = qk.shape[0]
    nb = n // seq
    has_s0 = state_in is not None
    assert has_s0 or state_out is not None or layer == 0
    spb = max(1, min(nb, GLA_GROUP_ROWS // seq))
    blk = lambda w: pl.BlockSpec((spb * seq, w), lambda i: (i, 0))
    nl = DEPTH if (not has_s0 and state_out is None) else 1
    st_spec = pl.BlockSpec((spb, nl, 2, H_A, DK_A, DV_A), lambda i: (i, layer, 0, 0, 0, 0))
    in_specs = [blk(256), blk(256), blk(256), blk(256), blk(W_A), _layer_spec(g4.shape, layer)]
    args = [qk, v, la, bsum, gate, g4]
    out_specs = [blk(W_A)]
    out_shape = [jax.ShapeDtypeStruct((n, W_A), BF16)]
    aliases = {}
    if has_s0:
        in_specs.append(st_spec)
        args.append(state_in)
    else:
        out_specs.append(st_spec)
        out_shape.append(jax.ShapeDtypeStruct((nb, DEPTH, 2, H_A, DK_A, DV_A), F32))
        if state_out is not None:
            aliases[len(args)] = 1
            in_specs.append(_ANY)
            args.append(state_out)
    return pl.pallas_call(
        functools.partial(_gla_kernel, seq=seq, has_s0=has_s0, alias_in=len(aliases), spb=spb),
        grid=(nb // spb,), in_specs=in_specs, out_specs=out_specs, out_shape=out_shape,
        input_output_aliases=aliases,
        scratch_shapes=[pltpu.VMEM((spb * seq, W_A), F32), pltpu.VMEM((2, ST_R, ST_C), F32),
                        pltpu.VMEM((ST_C, ST_R), F32)],
        compiler_params=_cparams(), name="gla_state" if has_s0 else "gla_ctx",
    )(*args)


def _softmax_t_pv(st, vt, ones_rows):
    dv, keys = vt.shape
    m = _col_reduce(st, jnp.max)
    e = jnp.exp2(st - m)
    if not ones_rows:
        l = _col_reduce(e, jnp.sum)
        return _dot(vt, e.astype(BF16)) * (1.0 / l)
    o = _dot(jnp.concatenate([vt, jnp.ones((16, keys), BF16)], axis=0), e.astype(BF16))
    return o[:dv] * (1.0 / o[dv:dv + 1])


def _pipelined_attention(score_fns, value_fns, depth):
    outs = []
    pending = [fn() for fn in score_fns[:depth]]
    for j, vfn in enumerate(value_fns):
        st = pending.pop(0)
        if j + depth < len(score_fns):
            pending.append(score_fns[j + depth]())
        outs.append(_softmax_t_pv(st, *vfn()))
    return outs


def _lookahead(n_keys):
    return 2 if n_keys >= 1024 else 3


def _col_reduce(x, op):
    rows, cols = x.shape
    part = 128 if rows % 128 == 0 and rows > 128 else rows
    if part != rows:
        x = op(x.reshape(rows // part, part, cols), axis=0)
    return op(x, axis=0, keepdims=True)


def _seqs_per_step(n, seq, ctx_len):
    if ctx_len or seq >= ATT_QBLOCK:
        return 1
    return max(1, min(n // seq, ATT_GROUP_ROWS // seq))


def _for_query_blocks(run, seq, spb):
    qb = min(ATT_QBLOCK, seq)
    nq = seq // qb
    if spb * nq <= ATT_STATIC_BLOCKS:
        run([(bb, slice(bb * seq + j * qb, bb * seq + (j + 1) * qb)) for bb in range(spb) for j in range(nq)])
    else:
        assert spb == 1

        def body(i, carry):
            run([(0, pl.ds(pl.multiple_of(i * qb, qb), qb))])
            return carry
        lax.fori_loop(0, nq, body, 0)


def _mla_part(q_ref, k_ref, vt_ref, gate_ref, ck_ref, cvt_ref, ob_ref, kk_sc, vt_sc, *, seq, ctx_len, spb):
    for bb in range(spb):
        rs = slice(bb * seq, (bb + 1) * seq)
        for p in range(H_B // 2):
            kk_sc[bb, p, 0:seq, 0:128] = k_ref[rs, 128 * p:128 * p + 128]
            kk_sc[bb, p, 0:seq, 128:256] = k_ref[rs, 512:640]
            if ctx_len:
                kk_sc[bb, p, seq:seq + ctx_len, 0:128] = ck_ref[0, 0, :, 128 * p:128 * p + 128]
                kk_sc[bb, p, seq:seq + ctx_len, 128:256] = ck_ref[0, 0, :, 512:640]
        vt_sc[bb, :, 0:seq] = vt_ref[:, rs]
        if ctx_len:
            vt_sc[bb, :, seq:seq + ctx_len] = cvt_ref[0, 0]

    def scores(bb, rows, h):
        p, hh = divmod(h, 2)
        qn = q_ref[rows, 128 * p:128 * p + 128] * _lane_mask(128, DN_B * hh, DN_B, BF16)
        qp = (q_ref[rows, 512 + 128 * (h // 4):512 + 128 * (h // 4) + 128]
              * _lane_mask(128, DR_B * (h % 4), DR_B, BF16))
        return _dot_nt(kk_sc[bb, p], jnp.concatenate([qn, qp], axis=-1))

    def items(blocks):
        its = [(bb, rows, h) for bb, rows in blocks for h in range(H_B)]

        def finish(outs):
            for j, (_, rows) in enumerate(blocks):
                ob = jnp.concatenate(outs[H_B * j:H_B * (j + 1)], axis=0).T
                ob_ref[rows, :] = (ob * gate_ref[rows, :]).astype(BF16)

        return ([functools.partial(scores, bb, rows, h) for bb, rows, h in its],
                [functools.partial(lambda bb, h: (vt_sc[bb, DV_B * h:DV_B * (h + 1), :], not ctx_len), bb, h)
                 for bb, _, h in its],
                finish)

    return items


def _diff_part(qk_ref, vt_ref, gate_ref, lam_ref, g_ref, ck_ref, cv_ref, oc_ref, k_sc, vt_sc, *,
               seq, ctx_len, lam_init, spb):
    lam = (jnp.exp(jnp.sum(lam_ref[0, 0:1, :] * lam_ref[0, 1:2, :], axis=-1, keepdims=True))
           - jnp.exp(jnp.sum(lam_ref[0, 2:3, :] * lam_ref[0, 3:4, :], axis=-1, keepdims=True)) + lam_init)
    dh = 2 * DC
    for bb in range(spb):
        rs = slice(bb * seq, (bb + 1) * seq)
        for p in range(H_C // 2):
            k_sc[bb, p, 0:seq, :] = qk_ref[rs, 256 + 128 * p:256 + 128 * p + 128]
            if ctx_len:
                pair_t = jnp.concatenate([ck_ref[0, 0, 2 * p], ck_ref[0, 0, 2 * p + 1]], axis=0)
                k_sc[bb, p, seq:seq + ctx_len, :] = pair_t.T.astype(BF16)
        vt_sc[bb, :, 0:seq] = vt_ref[:, rs]
        if ctx_len:
            for h in range(H_C):
                vt_sc[bb, dh * h:dh * (h + 1), seq:seq + ctx_len] = cv_ref[0, 0, h].astype(BF16)

    def scores(bb, rows, h, comp):
        p, hh = divmod(h, 2)
        qm = qk_ref[rows, 128 * p:128 * p + 128] * _lane_mask(128, dh * hh + DC * comp, DC, BF16)
        return _dot_nt(k_sc[bb, p], qm)

    def items(blocks):
        its = [(bb, rows, h, comp) for bb, rows in blocks for h in range(H_C) for comp in range(2)]

        def finish(o12):
            for j, (_, rows) in enumerate(blocks):
                outs = []
                for h in range(H_C):
                    o1, o2 = o12[2 * (H_C * j + h)], o12[2 * (H_C * j + h) + 1]
                    ot = o1 - lam * o2
                    outs.append(ot * lax.rsqrt(jnp.mean(ot * ot, axis=0, keepdims=True) + EPS))
                oc = jnp.concatenate(outs, axis=0).T
                oc_ref[rows, :] = (oc * g_ref[0] * (1.0 - lam_init) * gate_ref[rows, :]).astype(BF16)

        return ([functools.partial(scores, *item) for item in its],
                [functools.partial(lambda bb, h: (vt_sc[bb, dh * h:dh * (h + 1), :], True), bb, h)
                 for bb, _, h, _ in its],
                finish)

    return items


def _attn_kernel(*refs, seq, ctx_len, lam_init, spb, mla, diff):
    it = iter(refs)
    n_ctx = 2 if ctx_len else 0
    mla_in = [next(it) for _ in range(4 + n_ctx)] + [None] * (2 - n_ctx) if mla else None
    diff_in = [next(it) for _ in range(5 + n_ctx)] + [None] * (2 - n_ctx) if diff else None
    ob_ref = next(it) if mla else None
    oc_ref = next(it) if diff else None
    parts = []
    if mla:
        parts.append(_mla_part(*mla_in, ob_ref, next(it), next(it), seq=seq, ctx_len=ctx_len, spb=spb))
    if diff:
        parts.append(_diff_part(*diff_in, oc_ref, next(it), next(it), seq=seq, ctx_len=ctx_len,
                                lam_init=lam_init, spb=spb))

    def run(blocks):
        built = [part(blocks) for part in parts]
        outs = _pipelined_attention([f for b in built for f in b[0]], [f for b in built for f in b[1]],
                                    _lookahead(seq + ctx_len))
        lo = 0
        for score_fns, _, finish in built:
            finish(outs[lo:lo + len(score_fns)])
            lo += len(score_fns)

    _for_query_blocks(run, seq, spb)


def _attn_call(mq, mk, mvt, gate_b, dqk, dvt, gate_c, lamp, g4, ctx, layer, *, seq):
    n = mq.shape[0]
    ctx_len = 0 if ctx is None else ctx["mla_k"].shape[2]
    lam_init = 0.8 - 0.6 * math.exp(-0.3 * layer)
    spb = _seqs_per_step(n, seq, ctx_len)
    rows = spb * seq
    tk = seq + ctx_len
    rblk = lambda w: pl.BlockSpec((rows, w), lambda i: (i, 0))
    tblk = lambda w: pl.BlockSpec((w, rows), lambda i: (0, i))
    mla_specs, mla_args = [rblk(768), rblk(640), tblk(W_B), rblk(W_B)], [mq, mk, mvt, gate_b]
    diff_specs = [rblk(512), tblk(W_C), rblk(W_C), _layer_spec(lamp.shape, layer), _layer_spec(g4.shape, layer)]
    diff_args = [dqk, dvt, gate_c, lamp, g4]
    if ctx_len:
        mla_specs += [pl.BlockSpec((1, 1, ctx_len, 640), lambda i: (layer, i, 0, 0)),
                      pl.BlockSpec((1, 1, W_B, ctx_len), lambda i: (layer, i, 0, 0))]
        mla_args += [ctx["mla_k"], ctx["mla_vt"]]
        diff_specs += [pl.BlockSpec((1, 1, H_C, 2 * DC, ctx_len), lambda i: (i, layer, 0, 0, 0))] * 2
        diff_args += [ctx["diff_k_t"], ctx["diff_v_t"]]
    mla_out = (rblk(W_B), jax.ShapeDtypeStruct((n, W_B), BF16))
    diff_out = (rblk(W_C), jax.ShapeDtypeStruct((n, W_C), BF16))
    mla_scratch = [pltpu.VMEM((spb, H_B // 2, tk, 256), BF16), pltpu.VMEM((spb, W_B, tk), BF16)]
    diff_scratch = [pltpu.VMEM((spb, H_C // 2, tk, 128), BF16), pltpu.VMEM((spb, W_C, tk), BF16)]

    def call(mla, diff, name):
        outs = ([mla_out] if mla else []) + ([diff_out] if diff else [])
        return pl.pallas_call(
            functools.partial(_attn_kernel, seq=seq, ctx_len=ctx_len, lam_init=lam_init, spb=spb,
                              mla=mla, diff=diff),
            grid=(n // rows,),
            in_specs=(mla_specs if mla else []) + (diff_specs if diff else []),
            out_specs=[o[0] for o in outs], out_shape=[o[1] for o in outs],
            scratch_shapes=(mla_scratch if mla else []) + (diff_scratch if diff else []),
            compiler_params=_cparams(), name=name,
        )(*(mla_args if mla else []), *(diff_args if diff else []))

    if ctx_len:
        (ob,) = call(True, False, "mla_ctx")
        (oc,) = call(False, True, "diff_ctx")
        return ob, oc
    return call(True, True, "attn_self")


def _post_kernel(oa_ref, ob_ref, oc_ref, x_ref, mod_ref, w_ref, g_ref, y_ref):
    mix = jnp.concatenate([oa_ref[...], ob_ref[...], oc_ref[...]], axis=-1)
    out = _rms(_dot(mix, w_ref[0]), g_ref[0])
    y_ref[...] = x_ref[...] + mod_ref[0, :, 2 * D_MODEL:3 * D_MODEL] * out


def _post_call(oa, ob, oc, x2d, mod, pw, layer, *, seq, sample):
    n = x2d.shape[0]
    tm = min(POST_BLOCK, seq if sample else n)
    blk = lambda w: pl.BlockSpec((tm, w), lambda i: (i, 0))

    def mod_idx(i):
        return (layer * MOD_ROWS + ((i * tm) // seq + 1 if sample else 0), 0, 0)

    return pl.pallas_call(
        _post_kernel,
        grid=(n // tm,),
        in_specs=[blk(W_A), blk(W_B), blk(W_C), blk(D_MODEL),
                  pl.BlockSpec((1, 1, 3 * D_MODEL), mod_idx),
                  _layer_spec(pw["w_out"].shape, layer), _layer_spec(pw["g_post"].shape, layer)],
        out_specs=blk(D_MODEL),
        out_shape=jax.ShapeDtypeStruct((n, D_MODEL), F32),
        compiler_params=_cparams(), name="post",
    )(oa, ob, oc, x2d, mod, pw["w_out"], pw["g_post"])


def _pack_params(g_pre, g_post, w_in, w_gla_af, b_gla_af, w_gla_ab, b_gla_ab, g_gla, g_mla_q, w_mla_uq,
                 g_mla_kv, w_mla_ukv, lam_q1, lam_k1, lam_q2, lam_k2, g_diff, w_out):
    w_t = jnp.swapaxes(w_in, 1, 2)
    assert w_t.shape[1] == R_DIFF[0][1]
    zg = jnp.zeros((DEPTH, GLA_LR, 128), F32)
    w_gate = jnp.concatenate([jnp.zeros((DEPTH, DR_B, 256), F32),
                              jnp.concatenate([w_gla_af, zg], axis=-1),
                              jnp.concatenate([zg, w_gla_ab], axis=-1),
                              jnp.zeros((DEPTH, 128 - DR_B - 2 * GLA_LR, 256), F32)], axis=1).astype(BF16)
    uq = w_mla_uq.reshape(DEPTH, Q_LORA, H_B, DN_B + DR_B)
    w_pe = uq[..., DN_B:].reshape(DEPTH, Q_LORA, H_B * DR_B)
    w_uq = jnp.concatenate([uq[..., :DN_B].reshape(DEPTH, Q_LORA, H_B * DN_B), w_pe], axis=-1).astype(BF16)
    ukv = w_mla_ukv.reshape(DEPTH, KV_LORA, H_B, DN_B + DV_B)
    w_ukv = jnp.concatenate([ukv[..., :DN_B].reshape(DEPTH, KV_LORA, H_B * DN_B),
                             ukv[..., DN_B:].reshape(DEPTH, KV_LORA, H_B * DV_B)], axis=-1).astype(BF16)
    row = lambda a: a.reshape(DEPTH, 1, a.shape[-1])
    return dict(
        w_t=w_t.astype(BF16), w_gate=w_gate,
        b_gate=row(jnp.concatenate([b_gla_af, b_gla_ab], axis=-1)),
        w_uq=w_uq, w_ukv=w_ukv, w_out=w_out.astype(BF16),
        g_pre=row(g_pre), g_post=row(g_post), g_mla_q=row(g_mla_q), g_mla_kv=row(g_mla_kv),
        g_gla4=row(jnp.tile(g_gla, (1, H_A))), g_diff4=row(jnp.tile(g_diff, (1, H_C))),
        lam=jnp.stack([lam_q1, lam_k1, lam_q2, lam_k2], axis=1))


def _rope_tables(n):
    t = np.arange(n)
    row = (t // GRID_W).astype(np.float32)
    col = (t % GRID_W).astype(np.float32)
    half = ROPE_DIM // 2
    inv = (1.0 / (np.float32(ROPE_THETA) ** (np.arange(0, half, 2, dtype=np.float32) / np.float32(half)))
           ).astype(np.float32)
    ar = row[:, None] * inv
    ac = col[:, None] * inv
    ang = np.concatenate([ar, ar, ac, ac], axis=-1).astype(np.float32)
    return (jnp.asarray(np.tile(np.cos(ang), (1, 8)).astype(np.float32)),
            jnp.asarray(np.tile(np.sin(ang), (1, 8)).astype(np.float32)))


def _sublayer(x2d, mod, pw, layer, *, seq, rope_tabs, ctx, caches):
    sample = ctx is not None
    pre = _pre_call(x2d, mod, pw, layer, seq=seq, rope_tabs=rope_tabs,
                    caches=None if sample else caches[:4])
    qk, v, la, bsum, ga, gb, gc, mq, mk, mvt, dqk, dvt = pre[:12]
    if sample:
        (oa,) = _gla_call(qk, v, la, bsum, ga, pw["g_gla4"], layer, seq=seq, state_in=ctx["state"])
        new_caches = None
    else:
        oa, sfin = _gla_call(qk, v, la, bsum, ga, pw["g_gla4"], layer, seq=seq,
                             state_out=caches[4] if caches else None)
        new_caches = tuple(pre[12:]) + (sfin,)
    ob, oc = _attn_call(mq, mk, mvt, gb, dqk, dvt, gc, pw["lam"], pw["g_diff4"], ctx, layer, seq=seq)
    y = _post_call(oa, ob, oc, x2d, mod, pw, layer, seq=seq, sample=sample)
    return y, new_caches


def kernel(x_prompt, x_sample, c, cache_mla_ckv, cache_mla_kpe, cache_diff_k, cache_diff_v, state_gla,
           c_ctx, w_ada, b_ada, g_pre, g_post, w_in, w_gla_af, b_gla_af, w_gla_ab, b_gla_ab, g_gla,
           g_mla_q, w_mla_uq, g_mla_kv, w_mla_ukv, lam_q1, lam_k1, lam_q2, lam_k2, g_diff, w_out):
    bp, tp, d = x_prompt.shape
    bs, ts, _ = x_sample.shape

    pw = _pack_params(g_pre, g_post, w_in, w_gla_af, b_gla_af, w_gla_ab, b_gla_ab, g_gla, g_mla_q,
                      w_mla_uq, g_mla_kv, w_mla_ukv, lam_q1, lam_k1, lam_q2, lam_k2, g_diff, w_out)
    cvecs = jnp.concatenate([c_ctx[None], c, jnp.zeros((MOD_ROWS - 1 - bs, d), F32)], axis=0)
    mod = _mod_call(cvecs, w_ada, b_ada)
    rope_tabs = _rope_tables(ts)
    ctx_k, ctx_vt = _ctxkv_call(cache_mla_ckv, jnp.swapaxes(cache_mla_kpe, -1, -2), pw["w_ukv"])
    ctx = dict(state=state_gla, mla_k=ctx_k, mla_vt=ctx_vt,
               diff_k_t=jnp.swapaxes(cache_diff_k, -1, -2), diff_v_t=jnp.swapaxes(cache_diff_v, -1, -2))

    y_p = x_prompt.reshape(bp * tp, d)
    y_s = x_sample.reshape(bs * ts, d)
    caches = ()
    for l in range(DEPTH):
        y_p, caches = _sublayer(y_p, mod, pw, l, seq=tp, rope_tabs=None, ctx=None, caches=caches)
        y_s, _ = _sublayer(y_s, mod, pw, l, seq=ts, rope_tabs=rope_tabs, ctx=ctx, caches=None)
    ckvn, kpe_t, kc_t, vc_t, new_state = caches
    return (y_p.reshape(bp, tp, d), y_s.reshape(bs, ts, d), ckvn, jnp.swapaxes(kpe_t, -1, -2),
            jnp.swapaxes(kc_t, -1, -2), jnp.swapaxes(vc_t, -1, -2), new_state)
```

```python
import functools
import math

import numpy as np
import jax
import jax.numpy as jnp
from jax import lax
from jax.experimental import pallas as pl
from jax.experimental.pallas import tpu as pltpu

F32 = jnp.float32
BF16 = jnp.bfloat16

D_MODEL = 1024
DEPTH = 2
GRID_W = 64
EPS = 1e-6
ROPE_THETA = 10000.0
ROPE_DIM = 32
H_A, DK_A, DV_A = 4, 32, 64
GLA_LR = 16
GLA_TAU = 16.0
GLA_CHUNK = 64
H_B, DN_B, DR_B, DV_B = 8, 64, 32, 64
Q_LORA, KV_LORA = 256, 128
H_C, DC = 4, 32
W_A, W_B, W_C = H_A * DV_A, H_B * DV_B, H_C * 2 * DC
ST_R, ST_C = H_A * DV_A, H_A * DK_A
LOG2E = math.log2(math.e)
SUBLANES = 8

R_QKV = ((0, 512),)
R_GG = ((544, 800),)
R_MLA = ((800, 1216), (512, 544), 64)
R_MG = ((1216, 1728),)
R_DIFF = ((1728, 2752),)
MOD_ROWS = 8

V7X_VMEM_LIMIT_BYTES = 56 * 1024 * 1024
TOKEN_BLOCK = 1024
POST_BLOCK = 1024
ATT_QBLOCK = 512
ATT_GROUP_ROWS = 1024
GLA_GROUP_ROWS = 1024
ATT_STATIC_BLOCKS = 4


def _cparams(n_axes=1):
    return pltpu.CompilerParams(dimension_semantics=("arbitrary",) * n_axes,
                                vmem_limit_bytes=V7X_VMEM_LIMIT_BYTES)


def _rms(x, g):
    return x * lax.rsqrt(jnp.mean(x * x, axis=-1, keepdims=True) + EPS) * g


def _silu(x):
    return x * jax.nn.sigmoid(x)


def _log_sigmoid(x):
    return jnp.minimum(x, 0.0) - jnp.log1p(jnp.exp(-jnp.abs(x)))


def _rope(z, cos, sin):
    w = z.shape[-1]
    lane = lax.broadcasted_iota(jnp.int32, z.shape, 1)
    rot = jnp.where((lane & 15) < 8, -pltpu.roll(z, w - 8, 1), pltpu.roll(z, 8, 1))
    return z * cos + rot * sin


def _lane_mask(width, lo, size, dtype):
    lane = lax.broadcasted_iota(jnp.int32, (1, width), 1)
    return jnp.where((lane >= lo) & (lane < lo + size), 1.0, 0.0).astype(dtype)


def _dot(a, b):
    return jnp.dot(a, b, preferred_element_type=F32)


def _dot_nt(a, b):
    return lax.dot_general(a, b, (((1,), (1,)), ((), ())), preferred_element_type=F32)


def _dot_tn(a, b):
    return lax.dot_general(a, b, (((0,), (0,)), ((), ())), preferred_element_type=F32)


def _layer_spec(shape, layer):
    nd = len(shape)
    return pl.BlockSpec((1,) + tuple(shape[1:]), lambda *_: (layer,) + (0,) * (nd - 1))


_ANY = pl.BlockSpec(memory_space=pl.ANY)


def _mod_kernel(c_ref, w_ref, b_ref, o_ref):
    s = _silu(c_ref[...]).astype(BF16)
    layer = pl.program_id(0)
    bias = b_ref[0:1, :]
    for r in range(1, DEPTH):
        bias = jnp.where(layer == r, b_ref[r:r + 1, :], bias)
    mod = _dot(s, w_ref[0].astype(BF16)) + bias
    for r in range(MOD_ROWS):
        o_ref[r] = mod[r:r + 1, :]


def _mod_call(cvecs, w_ada, b_ada):
    nb = 1024
    return pl.pallas_call(
        _mod_kernel,
        grid=(DEPTH, 3 * D_MODEL // nb),
        in_specs=[pl.BlockSpec((MOD_ROWS, D_MODEL), lambda l, j: (0, 0)),
                  pl.BlockSpec((1, D_MODEL, nb), lambda l, j: (l, 0, j)),
                  pl.BlockSpec((DEPTH, nb), lambda l, j: (0, j))],
        out_specs=pl.BlockSpec((MOD_ROWS, 1, nb), lambda l, j: (l, 0, j)),
        out_shape=jax.ShapeDtypeStruct((DEPTH * MOD_ROWS, 1, 3 * D_MODEL), F32),
        compiler_params=_cparams(2), name="adaln_mod",
    )(cvecs, w_ada, b_ada)


def _ctxkv_kernel(ckv_ref, kpe_ref, w_ref, k_ref, vt_ref):
    kv = _dot(ckv_ref[0, 0].astype(BF16), w_ref[0])
    kpe4 = jnp.concatenate([kpe_ref[0, 0]] * 4, axis=0).T
    k_ref[0, 0, :, 0:512] = kv[:, 0:512].astype(BF16)
    k_ref[0, 0, :, 512:640] = kpe4.astype(BF16)
    vt_ref[0, 0] = kv[:, 512:1024].T.astype(BF16)


def _ctxkv_call(cache_ckv, cache_kpe_t, wukv):
    nb, _, tc, _ = cache_ckv.shape
    return pl.pallas_call(
        _ctxkv_kernel,
        grid=(DEPTH, nb),
        in_specs=[pl.BlockSpec((1, 1, tc, KV_LORA), lambda l, b: (b, l, 0, 0)),
                  pl.BlockSpec((1, 1, DR_B, tc), lambda l, b: (b, l, 0, 0)),
                  pl.BlockSpec((1, KV_LORA, 1024), lambda l, b: (l, 0, 0))],
        out_specs=[pl.BlockSpec((1, 1, tc, 640), lambda l, b: (l, b, 0, 0)),
                   pl.BlockSpec((1, 1, W_B, tc), lambda l, b: (l, b, 0, 0))],
        out_shape=[jax.ShapeDtypeStruct((DEPTH, nb, tc, 640), BF16),
                   jax.ShapeDtypeStruct((DEPTH, nb, W_B, tc), BF16)],
        compiler_params=_cparams(2), name="mla_ctx_kv",
    )(cache_ckv, cache_kpe_t, wukv)


def _pre_kernel(*refs, rope, ctx_out, alias_in, bpb, seq):
    it = iter(refs)
    (x_ref, mod_ref, gpre_ref, w_ref, wg_ref, bg_ref, gq_ref, gkv_ref, wuq_ref,
     wukv_ref) = (next(it) for _ in range(10))
    if rope:
        cos_ref, sin_ref = next(it), next(it)
    for _ in range(alias_in):
        next(it)
    (qk_ref, v_ref, la_ref, bs_ref, ga_ref, gb_ref, gc_ref, mq_ref, mk_ref, mvt_ref, dqk_ref,
     dvt_ref) = (next(it) for _ in range(12))
    if ctx_out:
        ckvn_ref, kpe_ref, kc_ref, vc_ref = (next(it) for _ in range(4))

    d = D_MODEL
    shift = mod_ref[0, :, 0:d]
    scale = mod_ref[0, :, d:2 * d]
    h = (_rms(x_ref[...], gpre_ref[0]) * (1.0 + scale) + shift).astype(BF16)
    def proj(group):
        parts = [jnp.zeros((r, d), BF16) if isinstance(r, int) else w_ref[0, r[0]:r[1], :] for r in group]
        return _dot_nt(h, parts[0] if len(parts) == 1 else jnp.concatenate(parts, axis=0))
    if rope:
        cos = cos_ref[...]
        sin = sin_ref[...]

    pg = proj(R_QKV)
    qk_ref[:, 0:128] = pg[:, 0:128] * (DK_A ** -0.5)
    qk_ref[:, 128:256] = pg[:, 128:256]
    v_ref[...] = pg[:, 256:512]
    pm = proj(R_MLA)
    tail = pm[:, 384:512]
    xg = _dot(tail.astype(BF16), wg_ref[0]) + bg_ref[0]
    la = _log_sigmoid(xg) * (1.0 / GLA_TAU)
    la_ref[...] = la
    bs_ref[:, 0:128] = _chunk_scan(la[:, 0:128], False)
    bs_ref[:, 128:256] = _chunk_scan(la[:, 128:256], True)
    ga_ref[...] = _silu(proj(R_GG)).astype(BF16)

    qall = _dot(_rms(pm[:, 0:256], gq_ref[0]).astype(BF16), wuq_ref[0])
    q_pe = qall[:, 512:768]
    if rope:
        q_pe = _rope(q_pe, cos, sin)
    sb = (DN_B + DR_B) ** -0.5 * LOG2E
    mq_ref[:, 0:512] = (qall[:, 0:512] * sb).astype(BF16)
    mq_ref[:, 512:768] = (q_pe * sb).astype(BF16)
    ckvn = _rms(pm[:, 256:384], gkv_ref[0])
    kvall = _dot(ckvn.astype(BF16), wukv_ref[0])
    lane = lax.broadcasted_iota(jnp.int32, tail.shape, 1)
    kpe4 = jnp.where(lane < DR_B, tail, 0.0)
    kpe4 = kpe4 + pltpu.roll(kpe4, DR_B, 1)
    kpe4 = kpe4 + pltpu.roll(kpe4, 2 * DR_B, 1)
    if rope:
        kpe4 = _rope(kpe4, cos[:, 0:128], sin[:, 0:128])
    mk_ref[:, 0:512] = kvall[:, 0:512].astype(BF16)
    mk_ref[:, 512:640] = kpe4.astype(BF16)
    mvt_ref[...] = kvall[:, 512:1024].T.astype(BF16)
    gb_ref[...] = _silu(proj(R_MG)).astype(BF16)

    pd = proj(R_DIFF)
    dq, dk, dv = pd[:, 0:256], pd[:, 256:512], pd[:, 512:768]
    if rope:
        dq = _rope(dq, cos, sin)
        dk = _rope(dk, cos, sin)
    dqk_ref[:, 0:256] = (dq * (DC ** -0.5 * LOG2E)).astype(BF16)
    dqk_ref[:, 256:512] = dk.astype(BF16)
    dv_t = dv.T
    dvt_ref[...] = dv_t.astype(BF16)
    gc_ref[...] = _silu(pd[:, 768:1024]).astype(BF16)
    if ctx_out:
        kpe_t = kpe4.T
        dk_t = dk.T
        for ref in () if alias_in else (ckvn_ref, kpe_ref, kc_ref, vc_ref):
            ref[:, 1:] = jnp.zeros(ref[:, 1:].shape, F32)
        for bb in range(bpb):
            rs = slice(bb * seq, (bb + 1) * seq)
            ckvn_ref[bb, 0] = ckvn[rs]
            kpe_ref[bb, 0] = kpe_t[0:DR_B, rs]
            kc_ref[bb, 0] = dk_t[:, rs].reshape(H_C, 2 * DC, seq)
            vc_ref[bb, 0] = dv_t[:, rs].reshape(H_C, 2 * DC, seq)


def _pre_call(x2d, mod, pw, layer, *, seq, rope_tabs, caches):
    n = x2d.shape[0]
    tm = min(TOKEN_BLOCK, n)
    bpb = max(tm // seq, 1)
    rope = rope_tabs is not None
    ctx_out = caches is not None
    steps_per_seq = max(seq // tm, 1)
    nbt = n // seq

    def mod_idx(i):
        return (layer * MOD_ROWS + ((i * tm) // seq + 1 if rope else 0), 0, 0)

    names = ["g_pre", "w_t", "w_gate", "b_gate", "g_mla_q", "g_mla_kv", "w_uq", "w_ukv"]
    in_specs = [pl.BlockSpec((tm, D_MODEL), lambda i: (i, 0)), pl.BlockSpec((1, 1, 3 * D_MODEL), mod_idx)]
    in_specs += [_layer_spec(pw[k].shape, layer) for k in names]
    args = [x2d, mod] + [pw[k] for k in names]
    if rope:
        in_specs += [pl.BlockSpec((tm, 256), lambda i: (i % steps_per_seq, 0))] * 2
        args += list(rope_tabs)
    outs = [(256, F32, False), (256, F32, False), (256, F32, False), (256, F32, False),
            (W_A, BF16, False), (W_B, BF16, False), (W_C, BF16, False),
            (768, BF16, False), (640, BF16, False), (W_B, BF16, True),
            (512, BF16, False), (W_C, BF16, True)]
    out_specs = [pl.BlockSpec((w, tm), lambda i: (0, i)) if tr else pl.BlockSpec((tm, w), lambda i: (i, 0))
                 for w, _, tr in outs]
    out_shape = [jax.ShapeDtypeStruct((w, n) if tr else (n, w), dt) for w, dt, tr in outs]
    widths = outs
    aliases = {}
    if ctx_out:
        assert caches or layer == 0
        nl = 1 if caches else DEPTH
        out_specs += [pl.BlockSpec((bpb, nl, seq, KV_LORA), lambda i: (i, layer, 0, 0)),
                      pl.BlockSpec((bpb, nl, DR_B, seq), lambda i: (i, layer, 0, 0)),
                      pl.BlockSpec((bpb, nl, H_C, 2 * DC, seq), lambda i: (i, layer, 0, 0, 0)),
                      pl.BlockSpec((bpb, nl, H_C, 2 * DC, seq), lambda i: (i, layer, 0, 0, 0))]
        out_shape += [jax.ShapeDtypeStruct((nbt, DEPTH, seq, KV_LORA), F32),
                      jax.ShapeDtypeStruct((nbt, DEPTH, DR_B, seq), F32),
                      jax.ShapeDtypeStruct((nbt, DEPTH, H_C, 2 * DC, seq), F32),
                      jax.ShapeDtypeStruct((nbt, DEPTH, H_C, 2 * DC, seq), F32)]
        for j, arr in enumerate(caches):
            aliases[len(args)] = len(widths) + j
            in_specs.append(_ANY)
            args.append(arr)
    return pl.pallas_call(
        functools.partial(_pre_kernel, rope=rope, ctx_out=ctx_out, alias_in=len(aliases), bpb=bpb, seq=seq),
        grid=(n // tm,), in_specs=in_specs, out_specs=out_specs, out_shape=out_shape,
        input_output_aliases=aliases,
        compiler_params=_cparams(), name="pre_rope" if rope else "pre_ctx",
    )(*args)


_GLA_LEVELS = (1, 2, 4, 8, 16, 32)


def _gla_consts(rev):
    c = GLA_CHUNK
    row = lax.broadcasted_iota(jnp.int32, (c, 128), 0)
    pos = (c - 1 - row) if rev else row
    ri = lax.broadcasted_iota(jnp.int32, (c, H_A * c), 0)
    cj = lax.broadcasted_iota(jnp.int32, (c, H_A * c), 1) & (c - 1)
    pi = (c - 1 - ri) if rev else ri
    pj = (c - 1 - cj) if rev else cj
    x = pi ^ pj
    lvl = jnp.where(pi == pj, 0, -1)
    for kbit in range(6):
        lvl = jnp.where((pj < pi) & ((x >> kbit) == 1), kbit + 1, lvl)
    return pos, lvl


def _chunk_scan(x, rev):
    rows = x.shape[0]
    nt = rows // SUBLANES
    tiles_per_chunk = GLA_CHUNK // SUBLANES
    x3 = x.reshape(nt, SUBLANES, 128)
    sub = lax.broadcasted_iota(jnp.int32, x3.shape, 1)
    tile = lax.broadcasted_iota(jnp.int32, x3.shape, 0) & (tiles_per_chunk - 1)
    edge = 0 if rev else SUBLANES - 1
    s = 1
    while s < SUBLANES:
        if rev:
            x3 = x3 + jnp.where(sub < SUBLANES - s, pltpu.roll(x3, SUBLANES - s, 1), 0.0)
        else:
            x3 = x3 + jnp.where(sub >= s, pltpu.roll(x3, s, 1), 0.0)
        s *= 2
    s = 1
    while s < tiles_per_chunk:
        tot = jnp.broadcast_to(x3[:, edge:edge + 1, :], x3.shape)
        if rev:
            shifted = jnp.concatenate([tot[s:], tot[:s]], axis=0)
            x3 = x3 + jnp.where(tile < tiles_per_chunk - s, shifted, 0.0)
        else:
            shifted = jnp.concatenate([tot[nt - s:], tot[:nt - s]], axis=0)
            x3 = x3 + jnp.where(tile >= s, shifted, 0.0)
        s *= 2
    return x3.reshape(rows, 128)


def _gla_scores(q, k, la, b, pos, lvl, hm_bf, rev):
    c = GLA_CHUNK
    prv = pltpu.roll(la, c - 1 if rev else 1, 0)
    nxt = pltpu.roll(la, 1 if rev else c - 1, 0)
    s_tot = jnp.where(lvl == 0, _dot_nt(q.astype(BF16), jnp.concatenate([k.astype(BF16)] * H_A, 0) * hm_bf), 0.0)
    for kbit, m in enumerate(_GLA_LEVELS):
        up = ((pos >> kbit) & 1) == 1
        if m == 1:
            e = jnp.where(up, la, 0.0)
        elif m == 2:
            c4 = pos & 3
            e = jnp.where(c4 == 0, nxt, jnp.where(c4 == 1, 0.0, jnp.where(c4 == 2, la, la + prv)))
        else:
            nblk = c // (2 * m)
            loc = m if rev else m - 1
            b3 = b.reshape(nblk, 2 * m, 128)
            ref = jnp.broadcast_to(b3[:, loc:loc + 1, :], (nblk, 2 * m, 128)).reshape(c, 128)
            dlt = b - ref
            e = jnp.where(up, dlt, -dlt)
        xm = (jnp.where(up, q, k) * jnp.exp(e)).astype(BF16)
        sm = _dot_nt(xm, jnp.concatenate([xm] * H_A, 0) * hm_bf)
        s_tot = jnp.where(lvl == kbit + 1, sm, s_tot)
    return s_tot


def _gla_apply(s_tot, q, k, v, b, st_prev, hm_f32, vm_bf, rev):
    c = GLA_CHUNK
    vbd = jnp.concatenate([v] * H_A, 0) * vm_bf
    blast = b[0:1, :] if rev else b[c - 1:c, :]
    qbar = (q * jnp.exp(b)).astype(BF16)
    kdec = (k * jnp.exp(blast - b)).astype(BF16)
    o = _dot(s_tot.astype(BF16), vbd) + _dot_nt(qbar, st_prev.astype(BF16))
    st_new = st_prev * jnp.exp(blast) + _dot_tn(v, kdec) * hm_f32
    return o, st_new


def _gla_kernel(*refs, seq, has_s0, alias_in, spb):
    it = iter(refs)
    qk_ref, v_ref, la_ref, b_ref, gate_ref, g_ref = (next(it) for _ in range(6))
    s0_ref = next(it) if has_s0 else None
    for _ in range(alias_in):
        next(it)
    oa_ref = next(it)
    sfin_ref = None if has_s0 else next(it)
    acc_sc, st_sc, blk_sc = next(it), next(it), next(it)

    c = GLA_CHUNK
    nc = seq // c
    acc_sc[...] = jnp.zeros_like(acc_sc)
    if not has_s0 and not alias_in:
        sfin_ref[:, 1:] = jnp.zeros(sfin_ref[:, 1:].shape, F32)

    hrow = lax.broadcasted_iota(jnp.int32, (H_A * c, 128), 0) // c
    hm_f32 = jnp.where(hrow == lax.broadcasted_iota(jnp.int32, (H_A * c, 128), 1) // DK_A, 1.0, 0.0)
    hm_bf = hm_f32.astype(BF16)
    vrow = lax.broadcasted_iota(jnp.int32, (H_A * c, H_A * DV_A), 0) // c
    vm_bf = jnp.where(vrow == lax.broadcasted_iota(jnp.int32, (H_A * c, H_A * DV_A), 1) // DV_A,
                      1.0, 0.0).astype(BF16)
    consts = (_gla_consts(False), _gla_consts(True))

    def run_sequence(bb):
        base = bb * seq
        for d in range(2):
            if has_s0:
                blk_sc[...] = jnp.zeros_like(blk_sc)
                for hh in range(H_A):
                    blk_sc[DK_A * hh:DK_A * (hh + 1), DV_A * hh:DV_A * (hh + 1)] = s0_ref[bb, 0, d, hh]
                st_sc[d] = blk_sc[...].T
            else:
                st_sc[d] = jnp.zeros((ST_R, ST_C), F32)

        def chunk_rows(n, d):
            cn = (nc - 1 - n) if d else n
            start = base + cn * c
            return pl.ds(start if isinstance(start, int) else pl.multiple_of(start, c), c)

        def scores(n):
            out = []
            for d in range(2):
                rows = chunk_rows(n, d)
                pos, lvl = consts[d]
                out.append(_gla_scores(qk_ref[rows, 0:128], qk_ref[rows, 128:256],
                                       la_ref[rows, 128 * d:128 * d + 128],
                                       b_ref[rows, 128 * d:128 * d + 128], pos, lvl, hm_bf, bool(d)))
            return tuple(out)

        def apply(n, s_both):
            for d in range(2):
                rows = chunk_rows(n, d)
                o, st_new = _gla_apply(s_both[d], qk_ref[rows, 0:128], qk_ref[rows, 128:256],
                                       v_ref[rows, :].astype(BF16), b_ref[rows, 128 * d:128 * d + 128],
                                       st_sc[d], hm_f32, vm_bf, bool(d))
                acc_sc[rows, :] = acc_sc[rows, :] + o
                st_sc[d] = st_new

        def body(n, s_cur):
            s_next = scores(n + 1)
            apply(n, s_cur)
            return s_next

        trips = nc - 1
        s_last = lax.fori_loop(0, trips, body, scores(0),
                               unroll=next(u for u in (5, 3, 1) if trips % u == 0))
        apply(nc - 1, s_last)

        rs = slice(base, base + seq)
        first = lax.broadcasted_iota(jnp.int32, (seq, 128), 1) < DV_A
        for p in range(H_A // 2):
            cols = slice(128 * p, 128 * (p + 1))
            oa = acc_sc[rs, cols]
            sq = oa * oa
            s0 = jnp.sum(jnp.where(first, sq, 0.0), axis=-1, keepdims=True)
            s1 = jnp.sum(jnp.where(first, 0.0, sq), axis=-1, keepdims=True)
            inv = jnp.where(first, lax.rsqrt(s0 * (1.0 / DV_A) + EPS), lax.rsqrt(s1 * (1.0 / DV_A) + EPS))
            oa_ref[rs, cols] = (oa * inv * g_ref[0, :, cols] * gate_ref[rs, cols]).astype(BF16)
        if not has_s0:
            for d in range(2):
                blk_sc[...] = st_sc[d].T
                for hh in range(H_A):
                    sfin_ref[bb, 0, d, hh] = blk_sc[DK_A * hh:DK_A * (hh + 1), DV_A * hh:DV_A * (hh + 1)]

    for bb in range(spb):
        run_sequence(bb)


def _gla_call(qk, v, la, bsum, gate, g4, layer, *, seq, state_in=None, state_out=None):
    n = qk.shape[0]
    nb = n // seq
    has_s0 = state_in is not None
    assert has_s0 or state_out is not None or layer == 0
    spb = max(1, min(nb, GLA_GROUP_ROWS // seq))
    blk = lambda w: pl.BlockSpec((spb * seq, w), lambda i: (i, 0))
    nl = DEPTH if (not has_s0 and state_out is None) else 1
    st_spec = pl.BlockSpec((spb, nl, 2, H_A, DK_A, DV_A), lambda i: (i, layer, 0, 0, 0, 0))
    in_specs = [blk(256), blk(256), blk(256), blk(256), blk(W_A), _layer_spec(g4.shape, layer)]
    args = [qk, v, la, bsum, gate, g4]
    out_specs = [blk(W_A)]
    out_shape = [jax.ShapeDtypeStruct((n, W_A), BF16)]
    aliases = {}
    if has_s0:
        in_specs.append(st_spec)
        args.append(state_in)
    else:
        out_specs.append(st_spec)
        out_shape.append(jax.ShapeDtypeStruct((nb, DEPTH, 2, H_A, DK_A, DV_A), F32))
        if state_out is not None:
            aliases[len(args)] = 1
            in_specs.append(_ANY)
            args.append(state_out)
    return pl.pallas_call(
        functools.partial(_gla_kernel, seq=seq, has_s0=has_s0, alias_in=len(aliases), spb=spb),
        grid=(nb // spb,), in_specs=in_specs, out_specs=out_specs, out_shape=out_shape,
        input_output_aliases=aliases,
        scratch_shapes=[pltpu.VMEM((spb * seq, W_A), F32), pltpu.VMEM((2, ST_R, ST_C), F32),
                        pltpu.VMEM((ST_C, ST_R), F32)],
        compiler_params=_cparams(), name="gla_state" if has_s0 else "gla_ctx",
    )(*args)


def _softmax_t_pv(st, vt, ones_rows):
    dv, keys = vt.shape
    m = _col_reduce(st, jnp.max)
    e = jnp.exp2(st - m)
    if not ones_rows:
        l = _col_reduce(e, jnp.sum)
        return _dot(vt, e.astype(BF16)) * (1.0 / l)
    o = _dot(jnp.concatenate([vt, jnp.ones((16, keys), BF16)], axis=0), e.astype(BF16))
    return o[:dv] * (1.0 / o[dv:dv + 1])


def _pipelined_attention(score_fns, value_fns, depth):
    outs = []
    pending = [fn() for fn in score_fns[:depth]]
    for j, vfn in enumerate(value_fns):
        st = pending.pop(0)
        if j + depth < len(score_fns):
            pending.append(score_fns[j + depth]())
        outs.append(_softmax_t_pv(st, *vfn()))
    return outs


def _lookahead(n_keys):
    return 2 if n_keys >= 1024 else 3


def _col_reduce(x, op):
    rows, cols = x.shape
    part = 128 if rows % 128 == 0 and rows > 128 else rows
    if part != rows:
        x = op(x.reshape(rows // part, part, cols), axis=0)
    return op(x, axis=0, keepdims=True)


def _seqs_per_step(n, seq, ctx_len):
    if ctx_len or seq >= ATT_QBLOCK:
        return 1
    return max(1, min(n // seq, ATT_GROUP_ROWS // seq))


def _for_query_blocks(run, seq, spb):
    qb = min(ATT_QBLOCK, seq)
    nq = seq // qb
    if spb * nq <= ATT_STATIC_BLOCKS:
        run([(bb, slice(bb * seq + j * qb, bb * seq + (j + 1) * qb)) for bb in range(spb) for j in range(nq)])
    else:
        assert spb == 1

        def body(i, carry):
            run([(0, pl.ds(pl.multiple_of(i * qb, qb), qb))])
            return carry
        lax.fori_loop(0, nq, body, 0)


def _mla_part(q_ref, k_ref, vt_ref, gate_ref, ck_ref, cvt_ref, ob_ref, kk_sc, vt_sc, *, seq, ctx_len, spb):
    for bb in range(spb):
        rs = slice(bb * seq, (bb + 1) * seq)
        for p in range(H_B // 2):
            kk_sc[bb, p, 0:seq, 0:128] = k_ref[rs, 128 * p:128 * p + 128]
            kk_sc[bb, p, 0:seq, 128:256] = k_ref[rs, 512:640]
            if ctx_len:
                kk_sc[bb, p, seq:seq + ctx_len, 0:128] = ck_ref[0, 0, :, 128 * p:128 * p + 128]
                kk_sc[bb, p, seq:seq + ctx_len, 128:256] = ck_ref[0, 0, :, 512:640]
        vt_sc[bb, :, 0:seq] = vt_ref[:, rs]
        if ctx_len:
            vt_sc[bb, :, seq:seq + ctx_len] = cvt_ref[0, 0]

    def scores(bb, rows, h):
        p, hh = divmod(h, 2)
        qn = q_ref[rows, 128 * p:128 * p + 128] * _lane_mask(128, DN_B * hh, DN_B, BF16)
        qp = (q_ref[rows, 512 + 128 * (h // 4):512 + 128 * (h // 4) + 128]
              * _lane_mask(128, DR_B * (h % 4), DR_B, BF16))
        return _dot_nt(kk_sc[bb, p], jnp.concatenate([qn, qp], axis=-1))

    def items(blocks):
        its = [(bb, rows, h) for bb, rows in blocks for h in range(H_B)]

        def finish(outs):
            for j, (_, rows) in enumerate(blocks):
                ob = jnp.concatenate(outs[H_B * j:H_B * (j + 1)], axis=0).T
                ob_ref[rows, :] = (ob * gate_ref[rows, :]).astype(BF16)

        return ([functools.partial(scores, bb, rows, h) for bb, rows, h in its],
                [functools.partial(lambda bb, h: (vt_sc[bb, DV_B * h:DV_B * (h + 1), :], not ctx_len), bb, h)
                 for bb, _, h in its],
                finish)

    return items


def _diff_part(qk_ref, vt_ref, gate_ref, lam_ref, g_ref, ck_ref, cv_ref, oc_ref, k_sc, vt_sc, *,
               seq, ctx_len, lam_init, spb):
    lam = (jnp.exp(jnp.sum(lam_ref[0, 0:1, :] * lam_ref[0, 1:2, :], axis=-1, keepdims=True))
           - jnp.exp(jnp.sum(lam_ref[0, 2:3, :] * lam_ref[0, 3:4, :], axis=-1, keepdims=True)) + lam_init)
    dh = 2 * DC
    for bb in range(spb):
        rs = slice(bb * seq, (bb + 1) * seq)
        for p in range(H_C // 2):
            k_sc[bb, p, 0:seq, :] = qk_ref[rs, 256 + 128 * p:256 + 128 * p + 128]
            if ctx_len:
                pair_t = jnp.concatenate([ck_ref[0, 0, 2 * p], ck_ref[0, 0, 2 * p + 1]], axis=0)
                k_sc[bb, p, seq:seq + ctx_len, :] = pair_t.T.astype(BF16)
        vt_sc[bb, :, 0:seq] = vt_ref[:, rs]
        if ctx_len:
            for h in range(H_C):
                vt_sc[bb, dh * h:dh * (h + 1), seq:seq + ctx_len] = cv_ref[0, 0, h].astype(BF16)

    def scores(bb, rows, h, comp):
        p, hh = divmod(h, 2)
        qm = qk_ref[rows, 128 * p:128 * p + 128] * _lane_mask(128, dh * hh + DC * comp, DC, BF16)
        return _dot_nt(k_sc[bb, p], qm)

    def items(blocks):
        its = [(bb, rows, h, comp) for bb, rows in blocks for h in range(H_C) for comp in range(2)]

        def finish(o12):
            for j, (_, rows) in enumerate(blocks):
                outs = []
                for h in range(H_C):
                    o1, o2 = o12[2 * (H_C * j + h)], o12[2 * (H_C * j + h) + 1]
                    ot = o1 - lam * o2
                    outs.append(ot * lax.rsqrt(jnp.mean(ot * ot, axis=0, keepdims=True) + EPS))
                oc = jnp.concatenate(outs, axis=0).T
                oc_ref[rows, :] = (oc * g_ref[0] * (1.0 - lam_init) * gate_ref[rows, :]).astype(BF16)

        return ([functools.partial(scores, *item) for item in its],
                [functools.partial(lambda bb, h: (vt_sc[bb, dh * h:dh * (h + 1), :], True), bb, h)
                 for bb, _, h, _ in its],
                finish)

    return items


def _attn_kernel(*refs, seq, ctx_len, lam_init, spb, mla, diff):
    it = iter(refs)
    n_ctx = 2 if ctx_len else 0
    mla_in = [next(it) for _ in range(4 + n_ctx)] + [None] * (2 - n_ctx) if mla else None
    diff_in = [next(it) for _ in range(5 + n_ctx)] + [None] * (2 - n_ctx) if diff else None
    ob_ref = next(it) if mla else None
    oc_ref = next(it) if diff else None
    parts = []
    if mla:
        parts.append(_mla_part(*mla_in, ob_ref, next(it), next(it), seq=seq, ctx_len=ctx_len, spb=spb))
    if diff:
        parts.append(_diff_part(*diff_in, oc_ref, next(it), next(it), seq=seq, ctx_len=ctx_len,
                                lam_init=lam_init, spb=spb))

    def run(blocks):
        built = [part(blocks) for part in parts]
        outs = _pipelined_attention([f for b in built for f in b[0]], [f for b in built for f in b[1]],
                                    _lookahead(seq + ctx_len))
        lo = 0
        for score_fns, _, finish in built:
            finish(outs[lo:lo + len(score_fns)])
            lo += len(score_fns)

    _for_query_blocks(run, seq, spb)


def _attn_call(mq, mk, mvt, gate_b, dqk, dvt, gate_c, lamp, g4, ctx, layer, *, seq):
    n = mq.shape[0]
    ctx_len = 0 if ctx is None else ctx["mla_k"].shape[2]
    lam_init = 0.8 - 0.6 * math.exp(-0.3 * layer)
    spb = _seqs_per_step(n, seq, ctx_len)
    rows = spb * seq
    tk = seq + ctx_len
    rblk = lambda w: pl.BlockSpec((rows, w), lambda i: (i, 0))
    tblk = lambda w: pl.BlockSpec((w, rows), lambda i: (0, i))
    mla_specs, mla_args = [rblk(768), rblk(640), tblk(W_B), rblk(W_B)], [mq, mk, mvt, gate_b]
    diff_specs = [rblk(512), tblk(W_C), rblk(W_C), _layer_spec(lamp.shape, layer), _layer_spec(g4.shape, layer)]
    diff_args = [dqk, dvt, gate_c, lamp, g4]
    if ctx_len:
        mla_specs += [pl.BlockSpec((1, 1, ctx_len, 640), lambda i: (layer, i, 0, 0)),
                      pl.BlockSpec((1, 1, W_B, ctx_len), lambda i: (layer, i, 0, 0))]
        mla_args += [ctx["mla_k"], ctx["mla_vt"]]
        diff_specs += [pl.BlockSpec((1, 1, H_C, 2 * DC, ctx_len), lambda i: (i, layer, 0, 0, 0))] * 2
        diff_args += [ctx["diff_k_t"], ctx["diff_v_t"]]
    mla_out = (rblk(W_B), jax.ShapeDtypeStruct((n, W_B), BF16))
    diff_out = (rblk(W_C), jax.ShapeDtypeStruct((n, W_C), BF16))
    mla_scratch = [pltpu.VMEM((spb, H_B // 2, tk, 256), BF16), pltpu.VMEM((spb, W_B, tk), BF16)]
    diff_scratch = [pltpu.VMEM((spb, H_C // 2, tk, 128), BF16), pltpu.VMEM((spb, W_C, tk), BF16)]

    def call(mla, diff, name):
        outs = ([mla_out] if mla else []) + ([diff_out] if diff else [])
        return pl.pallas_call(
            functools.partial(_attn_kernel, seq=seq, ctx_len=ctx_len, lam_init=lam_init, spb=spb,
                              mla=mla, diff=diff),
            grid=(n // rows,),
            in_specs=(mla_specs if mla else []) + (diff_specs if diff else []),
            out_specs=[o[0] for o in outs], out_shape=[o[1] for o in outs],
            scratch_shapes=(mla_scratch if mla else []) + (diff_scratch if diff else []),
            compiler_params=_cparams(), name=name,
        )(*(mla_args if mla else []), *(diff_args if diff else []))

    return call(True, True, "attn_ctx" if ctx_len else "attn_self")


def _post_kernel(oa_ref, ob_ref, oc_ref, x_ref, mod_ref, w_ref, g_ref, y_ref):
    mix = jnp.concatenate([oa_ref[...], ob_ref[...], oc_ref[...]], axis=-1)
    out = _rms(_dot(mix, w_ref[0]), g_ref[0])
    y_ref[...] = x_ref[...] + mod_ref[0, :, 2 * D_MODEL:3 * D_MODEL] * out


def _post_call(oa, ob, oc, x2d, mod, pw, layer, *, seq, sample):
    n = x2d.shape[0]
    tm = min(POST_BLOCK, seq if sample else n)
    blk = lambda w: pl.BlockSpec((tm, w), lambda i: (i, 0))

    def mod_idx(i):
        return (layer * MOD_ROWS + ((i * tm) // seq + 1 if sample else 0), 0, 0)

    return pl.pallas_call(
        _post_kernel,
        grid=(n // tm,),
        in_specs=[blk(W_A), blk(W_B), blk(W_C), blk(D_MODEL),
                  pl.BlockSpec((1, 1, 3 * D_MODEL), mod_idx),
                  _layer_spec(pw["w_out"].shape, layer), _layer_spec(pw["g_post"].shape, layer)],
        out_specs=blk(D_MODEL),
        out_shape=jax.ShapeDtypeStruct((n, D_MODEL), F32),
        compiler_params=_cparams(), name="post",
    )(oa, ob, oc, x2d, mod, pw["w_out"], pw["g_post"])


def _pack_params(g_pre, g_post, w_in, w_gla_af, b_gla_af, w_gla_ab, b_gla_ab, g_gla, g_mla_q, w_mla_uq,
                 g_mla_kv, w_mla_ukv, lam_q1, lam_k1, lam_q2, lam_k2, g_diff, w_out):
    w_t = jnp.swapaxes(w_in, 1, 2)
    assert w_t.shape[1] == R_DIFF[0][1]
    zg = jnp.zeros((DEPTH, GLA_LR, 128), F32)
    w_gate = jnp.concatenate([jnp.zeros((DEPTH, DR_B, 256), F32),
                              jnp.concatenate([w_gla_af, zg], axis=-1),
                              jnp.concatenate([zg, w_gla_ab], axis=-1),
                              jnp.zeros((DEPTH, 128 - DR_B - 2 * GLA_LR, 256), F32)], axis=1).astype(BF16)
    uq = w_mla_uq.reshape(DEPTH, Q_LORA, H_B, DN_B + DR_B)
    w_pe = uq[..., DN_B:].reshape(DEPTH, Q_LORA, H_B * DR_B)
    w_uq = jnp.concatenate([uq[..., :DN_B].reshape(DEPTH, Q_LORA, H_B * DN_B), w_pe], axis=-1).astype(BF16)
    ukv = w_mla_ukv.reshape(DEPTH, KV_LORA, H_B, DN_B + DV_B)
    w_ukv = jnp.concatenate([ukv[..., :DN_B].reshape(DEPTH, KV_LORA, H_B * DN_B),
                             ukv[..., DN_B:].reshape(DEPTH, KV_LORA, H_B * DV_B)], axis=-1).astype(BF16)
    row = lambda a: a.reshape(DEPTH, 1, a.shape[-1])
    return dict(
        w_t=w_t.astype(BF16), w_gate=w_gate,
        b_gate=row(jnp.concatenate([b_gla_af, b_gla_ab], axis=-1)),
        w_uq=w_uq, w_ukv=w_ukv, w_out=w_out.astype(BF16),
        g_pre=row(g_pre), g_post=row(g_post), g_mla_q=row(g_mla_q), g_mla_kv=row(g_mla_kv),
        g_gla4=row(jnp.tile(g_gla, (1, H_A))), g_diff4=row(jnp.tile(g_diff, (1, H_C))),
        lam=jnp.stack([lam_q1, lam_k1, lam_q2, lam_k2], axis=1))


def _rope_tables(n):
    t = np.arange(n)
    row = (t // GRID_W).astype(np.float32)
    col = (t % GRID_W).astype(np.float32)
    half = ROPE_DIM // 2
    inv = (1.0 / (np.float32(ROPE_THETA) ** (np.arange(0, half, 2, dtype=np.float32) / np.float32(half)))
           ).astype(np.float32)
    ar = row[:, None] * inv
    ac = col[:, None] * inv
    ang = np.concatenate([ar, ar, ac, ac], axis=-1).astype(np.float32)
    return (jnp.asarray(np.tile(np.cos(ang), (1, 8)).astype(np.float32)),
            jnp.asarray(np.tile(np.sin(ang), (1, 8)).astype(np.float32)))


def _sublayer(x2d, mod, pw, layer, *, seq, rope_tabs, ctx, caches):
    sample = ctx is not None
    pre = _pre_call(x2d, mod, pw, layer, seq=seq, rope_tabs=rope_tabs,
                    caches=None if sample else caches[:4])
    qk, v, la, bsum, ga, gb, gc, mq, mk, mvt, dqk, dvt = pre[:12]
    if sample:
        (oa,) = _gla_call(qk, v, la, bsum, ga, pw["g_gla4"], layer, seq=seq, state_in=ctx["state"])
        new_caches = None
    else:
        oa, sfin = _gla_call(qk, v, la, bsum, ga, pw["g_gla4"], layer, seq=seq,
                             state_out=caches[4] if caches else None)
        new_caches = tuple(pre[12:]) + (sfin,)
    ob, oc = _attn_call(mq, mk, mvt, gb, dqk, dvt, gc, pw["lam"], pw["g_diff4"], ctx, layer, seq=seq)
    y = _post_call(oa, ob, oc, x2d, mod, pw, layer, seq=seq, sample=sample)
    return y, new_caches


def kernel(x_prompt, x_sample, c, cache_mla_ckv, cache_mla_kpe, cache_diff_k, cache_diff_v, state_gla,
           c_ctx, w_ada, b_ada, g_pre, g_post, w_in, w_gla_af, b_gla_af, w_gla_ab, b_gla_ab, g_gla,
           g_mla_q, w_mla_uq, g_mla_kv, w_mla_ukv, lam_q1, lam_k1, lam_q2, lam_k2, g_diff, w_out):
    bp, tp, d = x_prompt.shape
    bs, ts, _ = x_sample.shape

    pw = _pack_params(g_pre, g_post, w_in, w_gla_af, b_gla_af, w_gla_ab, b_gla_ab, g_gla, g_mla_q,
                      w_mla_uq, g_mla_kv, w_mla_ukv, lam_q1, lam_k1, lam_q2, lam_k2, g_diff, w_out)
    cvecs = jnp.concatenate([c_ctx[None], c, jnp.zeros((MOD_ROWS - 1 - bs, d), F32)], axis=0)
    mod = _mod_call(cvecs, w_ada, b_ada)
    rope_tabs = _rope_tables(ts)
    ctx_k, ctx_vt = _ctxkv_call(cache_mla_ckv, jnp.swapaxes(cache_mla_kpe, -1, -2), pw["w_ukv"])
    ctx = dict(state=state_gla, mla_k=ctx_k, mla_vt=ctx_vt,
               diff_k_t=jnp.swapaxes(cache_diff_k, -1, -2), diff_v_t=jnp.swapaxes(cache_diff_v, -1, -2))

    y_p = x_prompt.reshape(bp * tp, d)
    y_s = x_sample.reshape(bs * ts, d)
    caches = ()
    for l in range(DEPTH):
        y_p, caches = _sublayer(y_p, mod, pw, l, seq=tp, rope_tabs=None, ctx=None, caches=caches)
        y_s, _ = _sublayer(y_s, mod, pw, l, seq=ts, rope_tabs=rope_tabs, ctx=ctx, caches=None)
    ckvn, kpe_t, kc_t, vc_t, new_state = caches
    return (y_p.reshape(bp, tp, d), y_s.reshape(bs, ts, d), ckvn, jnp.swapaxes(kpe_t, -1, -2),
            jnp.swapaxes(kc_t, -1, -2), jnp.swapaxes(vc_t, -1, -2), new_state)
```

```python
import functools
import math

import numpy as np
import jax
import jax.numpy as jnp
from jax import lax
from jax.experimental import pallas as pl
from jax.experimental.pallas import tpu as pltpu

F32 = jnp.float32
BF16 = jnp.bfloat16

D_MODEL = 1024
DEPTH = 2
GRID_W = 64
EPS = 1e-6
ROPE_THETA = 10000.0
ROPE_DIM = 32
H_A, DK_A, DV_A = 4, 32, 64
GLA_LR = 16
GLA_TAU = 16.0
GLA_CHUNK = 64
H_B, DN_B, DR_B, DV_B = 8, 64, 32, 64
Q_LORA, KV_LORA = 256, 128
H_C, DC = 4, 32
W_A, W_B, W_C = H_A * DV_A, H_B * DV_B, H_C * 2 * DC
ST_R, ST_C = H_A * DV_A, H_A * DK_A
LOG2E = math.log2(math.e)
SUBLANES = 8

R_QKV = ((0, 512),)
R_GG = ((544, 800),)
R_MLA = ((800, 1216), (512, 544), 64)
R_MG = ((1216, 1728),)
R_DIFF = ((1728, 2752),)
MOD_ROWS = 8

V7X_VMEM_LIMIT_BYTES = 56 * 1024 * 1024
TOKEN_BLOCK = 1024
POST_BLOCK = 1024
ATT_QBLOCK = 512
ATT_GROUP_ROWS = 1024
GLA_GROUP_ROWS = 1024
ATT_STATIC_BLOCKS = 4


def _cparams(n_axes=1):
    return pltpu.CompilerParams(dimension_semantics=("arbitrary",) * n_axes,
                                vmem_limit_bytes=V7X_VMEM_LIMIT_BYTES)


def _rms(x, g):
    return x * lax.rsqrt(jnp.mean(x * x, axis=-1, keepdims=True) + EPS) * g


def _silu(x):
    return x * jax.nn.sigmoid(x)


def _log_sigmoid(x):
    return jnp.minimum(x, 0.0) - jnp.log1p(jnp.exp(-jnp.abs(x)))


def _rope(z, cos, sin):
    w = z.shape[-1]
    lane = lax.broadcasted_iota(jnp.int32, z.shape, 1)
    rot = jnp.where((lane & 15) < 8, -pltpu.roll(z, w - 8, 1), pltpu.roll(z, 8, 1))
    return z * cos + rot * sin


def _lane_mask(width, lo, size, dtype):
    lane = lax.broadcasted_iota(jnp.int32, (1, width), 1)
    return jnp.where((lane >= lo) & (lane < lo + size), 1.0, 0.0).astype(dtype)


def _dot(a, b):
    return jnp.dot(a, b, preferred_element_type=F32)


def _dot_nt(a, b):
    return lax.dot_general(a, b, (((1,), (1,)), ((), ())), preferred_element_type=F32)


def _dot_tn(a, b):
    return lax.dot_general(a, b, (((0,), (0,)), ((), ())), preferred_element_type=F32)


def _layer_spec(shape, layer):
    nd = len(shape)
    if nd == 2:
        return pl.BlockSpec(tuple(shape), lambda *_: (0, 0))
    return pl.BlockSpec((1,) + tuple(shape[1:]), lambda *_: (layer,) + (0,) * (nd - 1))


def _layer_row(ref, layer):
    return ref[layer:layer + 1, :]


_ANY = pl.BlockSpec(memory_space=pl.ANY)


def _mod_kernel(cctx_ref, c_ref, w_ref, b_ref, o_ref, rows_sc):
    nc = c_ref.shape[0]
    rows_sc[0:1, :] = cctx_ref[...]
    rows_sc[1:1 + nc, :] = c_ref[...]
    if 1 + nc < MOD_ROWS:
        rows_sc[1 + nc:, :] = jnp.zeros((MOD_ROWS - 1 - nc, D_MODEL), F32)
    s = _silu(rows_sc[...]).astype(BF16)
    layer = pl.program_id(0)
    bias = b_ref[0:1, :]
    for r in range(1, DEPTH):
        bias = jnp.where(layer == r, b_ref[r:r + 1, :], bias)
    mod = _dot(s, w_ref[0].astype(BF16)) + bias
    for r in range(MOD_ROWS):
        o_ref[r] = mod[r:r + 1, :]


def _mod_call(c_ctx, c, w_ada, b_ada):
    nb = 1024
    assert 1 + c.shape[0] <= MOD_ROWS
    return pl.pallas_call(
        _mod_kernel,
        grid=(DEPTH, 3 * D_MODEL // nb),
        in_specs=[pl.BlockSpec((1, D_MODEL), lambda l, j: (0, 0)),
                  pl.BlockSpec(c.shape, lambda l, j: (0, 0)),
                  pl.BlockSpec((1, D_MODEL, nb), lambda l, j: (l, 0, j)),
                  pl.BlockSpec((DEPTH, nb), lambda l, j: (0, j))],
        out_specs=pl.BlockSpec((MOD_ROWS, 1, nb), lambda l, j: (l, 0, j)),
        out_shape=jax.ShapeDtypeStruct((DEPTH * MOD_ROWS, 1, 3 * D_MODEL), F32),
        scratch_shapes=[pltpu.VMEM((MOD_ROWS, D_MODEL), F32)],
        compiler_params=_cparams(2), name="adaln_mod",
    )(c_ctx, c, w_ada, b_ada)


def _ctxkv_kernel(ckv_ref, kpe_ref, w_ref, k_ref, vt_ref):
    kv = _dot(ckv_ref[0, 0].astype(BF16), w_ref[0])
    kpe4 = jnp.concatenate([kpe_ref[0, 0]] * 4, axis=0).T
    k_ref[0, 0, :, 0:512] = kv[:, 0:512].astype(BF16)
    k_ref[0, 0, :, 512:640] = kpe4.astype(BF16)
    vt_ref[0, 0] = kv[:, 512:1024].T.astype(BF16)


def _ctxkv_call(cache_ckv, cache_kpe_t, wukv):
    nb, _, tc, _ = cache_ckv.shape
    return pl.pallas_call(
        _ctxkv_kernel,
        grid=(DEPTH, nb),
        in_specs=[pl.BlockSpec((1, 1, tc, KV_LORA), lambda l, b: (b, l, 0, 0)),
                  pl.BlockSpec((1, 1, DR_B, tc), lambda l, b: (b, l, 0, 0)),
                  pl.BlockSpec((1, KV_LORA, 1024), lambda l, b: (l, 0, 0))],
        out_specs=[pl.BlockSpec((1, 1, tc, 640), lambda l, b: (l, b, 0, 0)),
                   pl.BlockSpec((1, 1, W_B, tc), lambda l, b: (l, b, 0, 0))],
        out_shape=[jax.ShapeDtypeStruct((DEPTH, nb, tc, 640), BF16),
                   jax.ShapeDtypeStruct((DEPTH, nb, W_B, tc), BF16)],
        compiler_params=_cparams(2), name="mla_ctx_kv",
    )(cache_ckv, cache_kpe_t, wukv)


def _pre_kernel(*refs, rope, ctx_out, alias_in, bpb, seq, layer):
    it = iter(refs)
    (x_ref, mod_ref, gpre_ref, w_ref, wg_ref, bg_ref, gq_ref, gkv_ref, wuq_ref,
     wukv_ref) = (next(it) for _ in range(10))
    if rope:
        cos_ref, sin_ref = next(it), next(it)
    for _ in range(alias_in):
        next(it)
    (qk_ref, v_ref, la_ref, bs_ref, ga_ref, gb_ref, gc_ref, mq_ref, mk_ref, mvt_ref, dqk_ref,
     dvt_ref) = (next(it) for _ in range(12))
    if ctx_out:
        ckvn_ref, kpe_ref, kc_ref, vc_ref = (next(it) for _ in range(4))

    d = D_MODEL
    shift = mod_ref[0, :, 0:d]
    scale = mod_ref[0, :, d:2 * d]
    h = (_rms(x_ref[...], _layer_row(gpre_ref, layer) * (1.0 + scale)) + shift).astype(BF16)
    def proj(group):
        parts = [jnp.zeros((r, d), BF16) if isinstance(r, int) else w_ref[0, r[0]:r[1], :] for r in group]
        return _dot_nt(h, parts[0] if len(parts) == 1 else jnp.concatenate(parts, axis=0))
    if rope:
        cos = cos_ref[...]
        sin = sin_ref[...]

    pg = proj(R_QKV)
    qk_ref[:, 0:128] = pg[:, 0:128] * (DK_A ** -0.5)
    qk_ref[:, 128:256] = pg[:, 128:256]
    v_ref[...] = pg[:, 256:512]
    pm = proj(R_MLA)
    tail = pm[:, 384:512]
    xg = _dot(tail.astype(BF16), wg_ref[0]) + bg_ref[0]
    la = _log_sigmoid(xg) * (1.0 / GLA_TAU)
    la_ref[...] = la
    bs_ref[:, 0:128] = _chunk_scan(la[:, 0:128], False)
    bs_ref[:, 128:256] = _chunk_scan(la[:, 128:256], True)
    ga_ref[...] = _silu(proj(R_GG)).astype(BF16)

    qall = _dot(_rms(pm[:, 0:256], _layer_row(gq_ref, layer)).astype(BF16), wuq_ref[0])
    q_pe = qall[:, 512:768]
    if rope:
        q_pe = _rope(q_pe, cos, sin)
    sb = (DN_B + DR_B) ** -0.5 * LOG2E
    mq_ref[:, 0:512] = (qall[:, 0:512] * sb).astype(BF16)
    mq_ref[:, 512:768] = (q_pe * sb).astype(BF16)
    ckvn = _rms(pm[:, 256:384], _layer_row(gkv_ref, layer))
    kvall = _dot(ckvn.astype(BF16), wukv_ref[0])
    lane = lax.broadcasted_iota(jnp.int32, tail.shape, 1)
    kpe4 = jnp.where(lane < DR_B, tail, 0.0)
    kpe4 = kpe4 + pltpu.roll(kpe4, DR_B, 1)
    kpe4 = kpe4 + pltpu.roll(kpe4, 2 * DR_B, 1)
    if rope:
        kpe4 = _rope(kpe4, cos[:, 0:128], sin[:, 0:128])
    mk_ref[:, 0:512] = kvall[:, 0:512].astype(BF16)
    mk_ref[:, 512:640] = kpe4.astype(BF16)
    mvt_ref[...] = kvall[:, 512:1024].T.astype(BF16)
    gb_ref[...] = _silu(proj(R_MG)).astype(BF16)

    pd = proj(R_DIFF)
    dq, dk, dv = pd[:, 0:256], pd[:, 256:512], pd[:, 512:768]
    if rope:
        dq = _rope(dq, cos, sin)
        dk = _rope(dk, cos, sin)
    dqk_ref[:, 0:256] = (dq * (DC ** -0.5 * LOG2E)).astype(BF16)
    dqk_ref[:, 256:512] = dk.astype(BF16)
    dv_t = dv.T
    dvt_ref[...] = dv_t.astype(BF16)
    gc_ref[...] = _silu(pd[:, 768:1024]).astype(BF16)
    if ctx_out:
        kpe_t = kpe4.T
        dk_t = dk.T
        for ref in () if alias_in else (ckvn_ref, kpe_ref, kc_ref, vc_ref):
            ref[:, 1:] = jnp.zeros(ref[:, 1:].shape, F32)
        for bb in range(bpb):
            rs = slice(bb * seq, (bb + 1) * seq)
            ckvn_ref[bb, 0] = ckvn[rs]
            kpe_ref[bb, 0] = kpe_t[0:DR_B, rs]
            kc_ref[bb, 0] = dk_t[:, rs].reshape(H_C, 2 * DC, seq)
            vc_ref[bb, 0] = dv_t[:, rs].reshape(H_C, 2 * DC, seq)


def _pre_call(x2d, mod, pw, layer, *, seq, rope_tabs, caches):
    n = x2d.shape[0]
    tm = min(TOKEN_BLOCK, n)
    bpb = max(tm // seq, 1)
    rope = rope_tabs is not None
    ctx_out = caches is not None
    steps_per_seq = max(seq // tm, 1)
    nbt = n // seq

    def mod_idx(i):
        return (layer * MOD_ROWS + ((i * tm) // seq + 1 if rope else 0), 0, 0)

    names = ["g_pre", "w_t", "w_gate", "b_gate", "g_mla_q", "g_mla_kv", "w_uq", "w_ukv"]
    in_specs = [pl.BlockSpec((tm, D_MODEL), lambda i: (i, 0)), pl.BlockSpec((1, 1, 3 * D_MODEL), mod_idx)]
    in_specs += [_layer_spec(pw[k].shape, layer) for k in names]
    args = [x2d, mod] + [pw[k] for k in names]
    if rope:
        in_specs += [pl.BlockSpec((tm, 256), lambda i: (i % steps_per_seq, 0))] * 2
        args += list(rope_tabs)
    outs = [(256, F32, False), (256, F32, False), (256, F32, False), (256, F32, False),
            (W_A, BF16, False), (W_B, BF16, False), (W_C, BF16, False),
            (768, BF16, False), (640, BF16, False), (W_B, BF16, True),
            (512, BF16, False), (W_C, BF16, True)]
    out_specs = [pl.BlockSpec((w, tm), lambda i: (0, i)) if tr else pl.BlockSpec((tm, w), lambda i: (i, 0))
                 for w, _, tr in outs]
    out_shape = [jax.ShapeDtypeStruct((w, n) if tr else (n, w), dt) for w, dt, tr in outs]
    widths = outs
    aliases = {}
    if ctx_out:
        assert caches or layer == 0
        nl = 1 if caches else DEPTH
        out_specs += [pl.BlockSpec((bpb, nl, seq, KV_LORA), lambda i: (i, layer, 0, 0)),
                      pl.BlockSpec((bpb, nl, DR_B, seq), lambda i: (i, layer, 0, 0)),
                      pl.BlockSpec((bpb, nl, H_C, 2 * DC, seq), lambda i: (i, layer, 0, 0, 0)),
                      pl.BlockSpec((bpb, nl, H_C, 2 * DC, seq), lambda i: (i, layer, 0, 0, 0))]
        out_shape += [jax.ShapeDtypeStruct((nbt, DEPTH, seq, KV_LORA), F32),
                      jax.ShapeDtypeStruct((nbt, DEPTH, DR_B, seq), F32),
                      jax.ShapeDtypeStruct((nbt, DEPTH, H_C, 2 * DC, seq), F32),
                      jax.ShapeDtypeStruct((nbt, DEPTH, H_C, 2 * DC, seq), F32)]
        for j, arr in enumerate(caches):
            aliases[len(args)] = len(widths) + j
            in_specs.append(_ANY)
            args.append(arr)
    return pl.pallas_call(
        functools.partial(_pre_kernel, rope=rope, ctx_out=ctx_out, alias_in=len(aliases), bpb=bpb, seq=seq,
                          layer=layer),
        grid=(n // tm,), in_specs=in_specs, out_specs=out_specs, out_shape=out_shape,
        input_output_aliases=aliases,
        compiler_params=_cparams(), name="pre_rope" if rope else "pre_ctx",
    )(*args)


_GLA_LEVELS = (1, 2, 4, 8, 16, 32)


def _gla_consts(rev):
    c = GLA_CHUNK
    row = lax.broadcasted_iota(jnp.int32, (c, 128), 0)
    pos = (c - 1 - row) if rev else row
    ri = lax.broadcasted_iota(jnp.int32, (c, H_A * c), 0)
    cj = lax.broadcasted_iota(jnp.int32, (c, H_A * c), 1) & (c - 1)
    pi = (c - 1 - ri) if rev else ri
    pj = (c - 1 - cj) if rev else cj
    x = pi ^ pj
    lvl = jnp.where(pi == pj, 0, -1)
    for kbit in range(6):
        lvl = jnp.where((pj < pi) & ((x >> kbit) == 1), kbit + 1, lvl)
    return pos, lvl


def _chunk_scan(x, rev):
    rows = x.shape[0]
    nt = rows // SUBLANES
    tiles_per_chunk = GLA_CHUNK // SUBLANES
    x3 = x.reshape(nt, SUBLANES, 128)
    sub = lax.broadcasted_iota(jnp.int32, x3.shape, 1)
    tile = lax.broadcasted_iota(jnp.int32, x3.shape, 0) & (tiles_per_chunk - 1)
    edge = 0 if rev else SUBLANES - 1
    s = 1
    while s < SUBLANES:
        if rev:
            x3 = x3 + jnp.where(sub < SUBLANES - s, pltpu.roll(x3, SUBLANES - s, 1), 0.0)
        else:
            x3 = x3 + jnp.where(sub >= s, pltpu.roll(x3, s, 1), 0.0)
        s *= 2
    s = 1
    while s < tiles_per_chunk:
        tot = jnp.broadcast_to(x3[:, edge:edge + 1, :], x3.shape)
        if rev:
            shifted = jnp.concatenate([tot[s:], tot[:s]], axis=0)
            x3 = x3 + jnp.where(tile < tiles_per_chunk - s, shifted, 0.0)
        else:
            shifted = jnp.concatenate([tot[nt - s:], tot[:nt - s]], axis=0)
            x3 = x3 + jnp.where(tile >= s, shifted, 0.0)
        s *= 2
    return x3.reshape(rows, 128)


def _gla_scores(q, k, la, b, pos, lvl, hm_bf, rev):
    c = GLA_CHUNK
    prv = pltpu.roll(la, c - 1 if rev else 1, 0)
    nxt = pltpu.roll(la, 1 if rev else c - 1, 0)
    s_tot = jnp.where(lvl == 0, _dot_nt(q.astype(BF16), jnp.concatenate([k.astype(BF16)] * H_A, 0) * hm_bf), 0.0)
    for kbit, m in enumerate(_GLA_LEVELS):
        up = ((pos >> kbit) & 1) == 1
        if m == 1:
            e = jnp.where(up, la, 0.0)
        elif m == 2:
            c4 = pos & 3
            e = jnp.where(c4 == 0, nxt, jnp.where(c4 == 1, 0.0, jnp.where(c4 == 2, la, la + prv)))
        else:
            nblk = c // (2 * m)
            loc = m if rev else m - 1
            b3 = b.reshape(nblk, 2 * m, 128)
            ref = jnp.broadcast_to(b3[:, loc:loc + 1, :], (nblk, 2 * m, 128)).reshape(c, 128)
            dlt = b - ref
            e = jnp.where(up, dlt, -dlt)
        xm = (jnp.where(up, q, k) * jnp.exp(e)).astype(BF16)
        sm = _dot_nt(xm, jnp.concatenate([xm] * H_A, 0) * hm_bf)
        s_tot = jnp.where(lvl == kbit + 1, sm, s_tot)
    return s_tot


def _gla_apply(s_tot, q, k, v, b, st_prev, hm_f32, vm_bf, rev):
    c = GLA_CHUNK
    vbd = jnp.concatenate([v] * H_A, 0) * vm_bf
    blast = b[0:1, :] if rev else b[c - 1:c, :]
    qbar = (q * jnp.exp(b)).astype(BF16)
    kdec = (k * jnp.exp(blast - b)).astype(BF16)
    o = _dot(s_tot.astype(BF16), vbd) + _dot_nt(qbar, st_prev.astype(BF16))
    st_new = st_prev * jnp.exp(blast) + _dot_tn(v, kdec) * hm_f32
    return o, st_new


def _gla_kernel(*refs, seq, has_s0, alias_in, spb):
    it = iter(refs)
    qk_ref, v_ref, la_ref, b_ref, gate_ref, g_ref = (next(it) for _ in range(6))
    s0_ref = next(it) if has_s0 else None
    for _ in range(alias_in):
        next(it)
    oa_ref = next(it)
    sfin_ref = None if has_s0 else next(it)
    acc_sc, st_sc, blk_sc = next(it), next(it), next(it)

    c = GLA_CHUNK
    nc = seq // c
    acc_sc[...] = jnp.zeros_like(acc_sc)
    if not has_s0 and not alias_in:
        sfin_ref[:, 1:] = jnp.zeros(sfin_ref[:, 1:].shape, F32)

    hrow = lax.broadcasted_iota(jnp.int32, (H_A * c, 128), 0) // c
    hm_f32 = jnp.where(hrow == lax.broadcasted_iota(jnp.int32, (H_A * c, 128), 1) // DK_A, 1.0, 0.0)
    hm_bf = hm_f32.astype(BF16)
    vrow = lax.broadcasted_iota(jnp.int32, (H_A * c, H_A * DV_A), 0) // c
    vm_bf = jnp.where(vrow == lax.broadcasted_iota(jnp.int32, (H_A * c, H_A * DV_A), 1) // DV_A,
                      1.0, 0.0).astype(BF16)
    consts = (_gla_consts(False), _gla_consts(True))

    def run_sequence(bb):
        base = bb * seq
        for d in range(2):
            if has_s0:
                blk_sc[...] = jnp.zeros_like(blk_sc)
                for hh in range(H_A):
                    blk_sc[DK_A * hh:DK_A * (hh + 1), DV_A * hh:DV_A * (hh + 1)] = s0_ref[bb, 0, d, hh]
                st_sc[d] = blk_sc[...].T
            else:
                st_sc[d] = jnp.zeros((ST_R, ST_C), F32)

        def chunk_rows(n, d):
            cn = (nc - 1 - n) if d else n
            start = base + cn * c
            return pl.ds(start if isinstance(start, int) else pl.multiple_of(start, c), c)

        def scores(n):
            out = []
            for d in range(2):
                rows = chunk_rows(n, d)
                pos, lvl = consts[d]
                out.append(_gla_scores(qk_ref[rows, 0:128], qk_ref[rows, 128:256],
                                       la_ref[rows, 128 * d:128 * d + 128],
                                       b_ref[rows, 128 * d:128 * d + 128], pos, lvl, hm_bf, bool(d)))
            return tuple(out)

        def apply(n, s_both):
            for d in range(2):
                rows = chunk_rows(n, d)
                o, st_new = _gla_apply(s_both[d], qk_ref[rows, 0:128], qk_ref[rows, 128:256],
                                       v_ref[rows, :].astype(BF16), b_ref[rows, 128 * d:128 * d + 128],
                                       st_sc[d], hm_f32, vm_bf, bool(d))
                acc_sc[rows, :] = acc_sc[rows, :] + o
                st_sc[d] = st_new

        def body(n, s_cur):
            s_next = scores(n + 1)
            apply(n, s_cur)
            return s_next

        trips = nc - 1
        s_last = lax.fori_loop(0, trips, body, scores(0),
                               unroll=next(u for u in (5, 3, 1) if trips % u == 0))
        apply(nc - 1, s_last)

        rs = slice(base, base + seq)
        first = lax.broadcasted_iota(jnp.int32, (seq, 128), 1) < DV_A
        for p in range(H_A // 2):
            cols = slice(128 * p, 128 * (p + 1))
            oa = acc_sc[rs, cols]
            sq = oa * oa
            s0 = jnp.sum(jnp.where(first, sq, 0.0), axis=-1, keepdims=True)
            s1 = jnp.sum(jnp.where(first, 0.0, sq), axis=-1, keepdims=True)
            inv = jnp.where(first, lax.rsqrt(s0 * (1.0 / DV_A) + EPS), lax.rsqrt(s1 * (1.0 / DV_A) + EPS))
            oa_ref[rs, cols] = (oa * inv * g_ref[0, :, cols] * gate_ref[rs, cols]).astype(BF16)
        if not has_s0:
            for d in range(2):
                blk_sc[...] = st_sc[d].T
                for hh in range(H_A):
                    sfin_ref[bb, 0, d, hh] = blk_sc[DK_A * hh:DK_A * (hh + 1), DV_A * hh:DV_A * (hh + 1)]

    for bb in range(spb):
        run_sequence(bb)


def _gla_call(qk, v, la, bsum, gate, g4, layer, *, seq, state_in=None, state_out=None):
    n = qk.shape[0]
    nb = n // seq
    has_s0 = state_in is not None
    assert has_s0 or state_out is not None or layer == 0
    spb = max(1, min(nb, GLA_GROUP_ROWS // seq))
    blk = lambda w: pl.BlockSpec((spb * seq, w), lambda i: (i, 0))
    nl = DEPTH if (not has_s0 and state_out is None) else 1
    st_spec = pl.BlockSpec((spb, nl, 2, H_A, DK_A, DV_A), lambda i: (i, layer, 0, 0, 0, 0))
    in_specs = [blk(256), blk(256), blk(256), blk(256), blk(W_A), _layer_spec(g4.shape, layer)]
    args = [qk, v, la, bsum, gate, g4]
    out_specs = [blk(W_A)]
    out_shape = [jax.ShapeDtypeStruct((n, W_A), BF16)]
    aliases = {}
    if has_s0:
        in_specs.append(st_spec)
        args.append(state_in)
    else:
        out_specs.append(st_spec)
        out_shape.append(jax.ShapeDtypeStruct((nb, DEPTH, 2, H_A, DK_A, DV_A), F32))
        if state_out is not None:
            aliases[len(args)] = 1
            in_specs.append(_ANY)
            args.append(state_out)
    return pl.pallas_call(
        functools.partial(_gla_kernel, seq=seq, has_s0=has_s0, alias_in=len(aliases), spb=spb),
        grid=(nb // spb,), in_specs=in_specs, out_specs=out_specs, out_shape=out_shape,
        input_output_aliases=aliases,
        scratch_shapes=[pltpu.VMEM((spb * seq, W_A), F32), pltpu.VMEM((2, ST_R, ST_C), F32),
                        pltpu.VMEM((ST_C, ST_R), F32)],
        compiler_params=_cparams(), name="gla_state" if has_s0 else "gla_ctx",
    )(*args)


def _softmax_t_pv(st, vt, ones_rows):
    dv, keys = vt.shape
    m = _col_reduce(st, jnp.max)
    e = jnp.exp2(st - m)
    if not ones_rows:
        l = _col_reduce(e, jnp.sum)
        return _dot(vt, e.astype(BF16)) * (1.0 / l)
    o = _dot(jnp.concatenate([vt, jnp.ones((16, keys), BF16)], axis=0), e.astype(BF16))
    return o[:dv] * (1.0 / o[dv:dv + 1])


def _pipelined_attention(score_fns, value_fns, depth):
    outs = []
    pending = [fn() for fn in score_fns[:depth]]
    for j, vfn in enumerate(value_fns):
        st = pending.pop(0)
        if j + depth < len(score_fns):
            pending.append(score_fns[j + depth]())
        outs.append(_softmax_t_pv(st, *vfn()))
    return outs


def _lookahead(n_keys):
    return 2 if n_keys >= 1024 else 3


def _col_reduce(x, op):
    rows, cols = x.shape
    part = 128 if rows % 128 == 0 and rows > 128 else rows
    if part != rows:
        x = op(x.reshape(rows // part, part, cols), axis=0)
    return op(x, axis=0, keepdims=True)


def _seqs_per_step(n, seq, ctx_len):
    if ctx_len or seq >= ATT_QBLOCK:
        return 1
    return max(1, min(n // seq, ATT_GROUP_ROWS // seq))


def _for_query_blocks(run, seq, spb):
    qb = min(ATT_QBLOCK, seq)
    nq = seq // qb
    if spb * nq <= ATT_STATIC_BLOCKS:
        run([(bb, slice(bb * seq + j * qb, bb * seq + (j + 1) * qb)) for bb in range(spb) for j in range(nq)])
    else:
        assert spb == 1

        def body(i, carry):
            run([(0, pl.ds(pl.multiple_of(i * qb, qb), qb))])
            return carry
        lax.fori_loop(0, nq, body, 0)


def _mla_part(q_ref, k_ref, vt_ref, gate_ref, ck_ref, cvt_ref, ob_ref, kk_sc, vt_sc, *, seq, ctx_len, spb):
    for bb in range(spb):
        rs = slice(bb * seq, (bb + 1) * seq)
        for p in range(H_B // 2):
            kk_sc[bb, p, 0:seq, 0:128] = k_ref[rs, 128 * p:128 * p + 128]
            kk_sc[bb, p, 0:seq, 128:256] = k_ref[rs, 512:640]
            if ctx_len:
                kk_sc[bb, p, seq:seq + ctx_len, 0:128] = ck_ref[0, 0, :, 128 * p:128 * p + 128]
                kk_sc[bb, p, seq:seq + ctx_len, 128:256] = ck_ref[0, 0, :, 512:640]
        vt_sc[bb, :, 0:seq] = vt_ref[:, rs]
        if ctx_len:
            vt_sc[bb, :, seq:seq + ctx_len] = cvt_ref[0, 0]

    def scores(bb, rows, h):
        p, hh = divmod(h, 2)
        qn = q_ref[rows, 128 * p:128 * p + 128] * _lane_mask(128, DN_B * hh, DN_B, BF16)
        qp = (q_ref[rows, 512 + 128 * (h // 4):512 + 128 * (h // 4) + 128]
              * _lane_mask(128, DR_B * (h % 4), DR_B, BF16))
        return _dot_nt(kk_sc[bb, p], jnp.concatenate([qn, qp], axis=-1))

    def items(blocks):
        its = [(bb, rows, h) for bb, rows in blocks for h in range(H_B)]

        def finish(outs):
            for j, (_, rows) in enumerate(blocks):
                ob = jnp.concatenate(outs[H_B * j:H_B * (j + 1)], axis=0).T
                ob_ref[rows, :] = (ob * gate_ref[rows, :]).astype(BF16)

        return ([functools.partial(scores, bb, rows, h) for bb, rows, h in its],
                [functools.partial(lambda bb, h: (vt_sc[bb, DV_B * h:DV_B * (h + 1), :], not ctx_len), bb, h)
                 for bb, _, h in its],
                finish)

    return items


def _diff_part(qk_ref, vt_ref, gate_ref, lam_ref, g_ref, ck_ref, cv_ref, oc_ref, k_sc, vt_sc, *,
               seq, ctx_len, lam_init, spb):
    lam = (jnp.exp(jnp.sum(lam_ref[0, 0:1, :] * lam_ref[0, 1:2, :], axis=-1, keepdims=True))
           - jnp.exp(jnp.sum(lam_ref[0, 2:3, :] * lam_ref[0, 3:4, :], axis=-1, keepdims=True)) + lam_init)
    dh = 2 * DC
    for bb in range(spb):
        rs = slice(bb * seq, (bb + 1) * seq)
        for p in range(H_C // 2):
            k_sc[bb, p, 0:seq, :] = qk_ref[rs, 256 + 128 * p:256 + 128 * p + 128]
            if ctx_len:
                pair_t = jnp.concatenate([ck_ref[0, 0, 2 * p], ck_ref[0, 0, 2 * p + 1]], axis=0)
                k_sc[bb, p, seq:seq + ctx_len, :] = pair_t.T.astype(BF16)
        vt_sc[bb, :, 0:seq] = vt_ref[:, rs]
        if ctx_len:
            for h in range(H_C):
                vt_sc[bb, dh * h:dh * (h + 1), seq:seq + ctx_len] = cv_ref[0, 0, h].astype(BF16)

    def scores(bb, rows, h, comp):
        p, hh = divmod(h, 2)
        qm = qk_ref[rows, 128 * p:128 * p + 128] * _lane_mask(128, dh * hh + DC * comp, DC, BF16)
        return _dot_nt(k_sc[bb, p], qm)

    def items(blocks):
        its = [(bb, rows, h, comp) for bb, rows in blocks for h in range(H_C) for comp in range(2)]

        def finish(o12):
            for j, (_, rows) in enumerate(blocks):
                outs = []
                for h in range(H_C):
                    o1, o2 = o12[2 * (H_C * j + h)], o12[2 * (H_C * j + h) + 1]
                    ot = o1 - lam * o2
                    outs.append(ot * lax.rsqrt(jnp.mean(ot * ot, axis=0, keepdims=True) + EPS))
                oc = jnp.concatenate(outs, axis=0).T
                oc_ref[rows, :] = (oc * g_ref[0] * (1.0 - lam_init) * gate_ref[rows, :]).astype(BF16)

        return ([functools.partial(scores, *item) for item in its],
                [functools.partial(lambda bb, h: (vt_sc[bb, dh * h:dh * (h + 1), :], True), bb, h)
                 for bb, _, h, _ in its],
                finish)

    return items


def _attn_kernel(*refs, seq, ctx_len, lam_init, spb, mla, diff):
    it = iter(refs)
    n_ctx = 2 if ctx_len else 0
    mla_in = [next(it) for _ in range(4 + n_ctx)] + [None] * (2 - n_ctx) if mla else None
    diff_in = [next(it) for _ in range(5 + n_ctx)] + [None] * (2 - n_ctx) if diff else None
    ob_ref = next(it) if mla else None
    oc_ref = next(it) if diff else None
    parts = []
    if mla:
        parts.append(_mla_part(*mla_in, ob_ref, next(it), next(it), seq=seq, ctx_len=ctx_len, spb=spb))
    if diff:
        parts.append(_diff_part(*diff_in, oc_ref, next(it), next(it), seq=seq, ctx_len=ctx_len,
                                lam_init=lam_init, spb=spb))

    def run(blocks):
        built = [part(blocks) for part in parts]
        outs = _pipelined_attention([f for b in built for f in b[0]], [f for b in built for f in b[1]],
                                    _lookahead(seq + ctx_len))
        lo = 0
        for score_fns, _, finish in built:
            finish(outs[lo:lo + len(score_fns)])
            lo += len(score_fns)

    _for_query_blocks(run, seq, spb)


def _attn_call(mq, mk, mvt, gate_b, dqk, dvt, gate_c, lamp, g4, ctx, layer, *, seq):
    n = mq.shape[0]
    ctx_len = 0 if ctx is None else ctx["mla_k"].shape[2]
    lam_init = 0.8 - 0.6 * math.exp(-0.3 * layer)
    spb = _seqs_per_step(n, seq, ctx_len)
    rows = spb * seq
    tk = seq + ctx_len
    rblk = lambda w: pl.BlockSpec((rows, w), lambda i: (i, 0))
    tblk = lambda w: pl.BlockSpec((w, rows), lambda i: (0, i))
    mla_specs, mla_args = [rblk(768), rblk(640), tblk(W_B), rblk(W_B)], [mq, mk, mvt, gate_b]
    diff_specs = [rblk(512), tblk(W_C), rblk(W_C), _layer_spec(lamp.shape, layer), _layer_spec(g4.shape, layer)]
    diff_args = [dqk, dvt, gate_c, lamp, g4]
    if ctx_len:
        mla_specs += [pl.BlockSpec((1, 1, ctx_len, 640), lambda i: (layer, i, 0, 0)),
                      pl.BlockSpec((1, 1, W_B, ctx_len), lambda i: (layer, i, 0, 0))]
        mla_args += [ctx["mla_k"], ctx["mla_vt"]]
        diff_specs += [pl.BlockSpec((1, 1, H_C, 2 * DC, ctx_len), lambda i: (i, layer, 0, 0, 0))] * 2
        diff_args += [ctx["diff_k_t"], ctx["diff_v_t"]]
    mla_out = (rblk(W_B), jax.ShapeDtypeStruct((n, W_B), BF16))
    diff_out = (rblk(W_C), jax.ShapeDtypeStruct((n, W_C), BF16))
    mla_scratch = [pltpu.VMEM((spb, H_B // 2, tk, 256), BF16), pltpu.VMEM((spb, W_B, tk), BF16)]
    diff_scratch = [pltpu.VMEM((spb, H_C // 2, tk, 128), BF16), pltpu.VMEM((spb, W_C, tk), BF16)]

    def call(mla, diff, name):
        outs = ([mla_out] if mla else []) + ([diff_out] if diff else [])
        return pl.pallas_call(
            functools.partial(_attn_kernel, seq=seq, ctx_len=ctx_len, lam_init=lam_init, spb=spb,
                              mla=mla, diff=diff),
            grid=(n // rows,),
            in_specs=(mla_specs if mla else []) + (diff_specs if diff else []),
            out_specs=[o[0] for o in outs], out_shape=[o[1] for o in outs],
            scratch_shapes=(mla_scratch if mla else []) + (diff_scratch if diff else []),
            compiler_params=_cparams(), name=name,
        )(*(mla_args if mla else []), *(diff_args if diff else []))

    if ctx_len:
        (ob,) = call(True, False, "mla_ctx")
        (oc,) = call(False, True, "diff_ctx")
        return ob, oc
    return call(True, True, "attn_self")


def _post_kernel(oa_ref, ob_ref, oc_ref, x_ref, mod_ref, w_ref, g_ref, y_ref, *, layer):
    mix = jnp.concatenate([oa_ref[...], ob_ref[...], oc_ref[...]], axis=-1)
    y_ref[...] = x_ref[...] + _rms(_dot(mix, w_ref[0]),
                                   _layer_row(g_ref, layer) * mod_ref[0, :, 2 * D_MODEL:3 * D_MODEL])


def _post_call(oa, ob, oc, x2d, mod, pw, layer, *, seq, sample):
    n = x2d.shape[0]
    tm = min(POST_BLOCK, seq if sample else n)
    blk = lambda w: pl.BlockSpec((tm, w), lambda i: (i, 0))

    def mod_idx(i):
        return (layer * MOD_ROWS + ((i * tm) // seq + 1 if sample else 0), 0, 0)

    return pl.pallas_call(
        functools.partial(_post_kernel, layer=layer),
        grid=(n // tm,),
        in_specs=[blk(W_A), blk(W_B), blk(W_C), blk(D_MODEL),
                  pl.BlockSpec((1, 1, 3 * D_MODEL), mod_idx),
                  _layer_spec(pw["w_out"].shape, layer), _layer_spec(pw["g_post"].shape, layer)],
        out_specs=blk(D_MODEL),
        out_shape=jax.ShapeDtypeStruct((n, D_MODEL), F32),
        compiler_params=_cparams(), name="post",
    )(oa, ob, oc, x2d, mod, pw["w_out"], pw["g_post"])


def _pack_params(g_pre, g_post, w_in, w_gla_af, b_gla_af, w_gla_ab, b_gla_ab, g_gla, g_mla_q, w_mla_uq,
                 g_mla_kv, w_mla_ukv, lam_q1, lam_k1, lam_q2, lam_k2, g_diff, w_out):
    w_t = jnp.swapaxes(w_in, 1, 2)
    assert w_t.shape[1] == R_DIFF[0][1]
    zg = jnp.zeros((DEPTH, GLA_LR, 128), F32)
    w_gate = jnp.concatenate([jnp.zeros((DEPTH, DR_B, 256), F32),
                              jnp.concatenate([w_gla_af, zg], axis=-1),
                              jnp.concatenate([zg, w_gla_ab], axis=-1),
                              jnp.zeros((DEPTH, 128 - DR_B - 2 * GLA_LR, 256), F32)], axis=1).astype(BF16)
    uq = w_mla_uq.reshape(DEPTH, Q_LORA, H_B, DN_B + DR_B)
    w_pe = uq[..., DN_B:].reshape(DEPTH, Q_LORA, H_B * DR_B)
    w_uq = jnp.concatenate([uq[..., :DN_B].reshape(DEPTH, Q_LORA, H_B * DN_B), w_pe], axis=-1).astype(BF16)
    ukv = w_mla_ukv.reshape(DEPTH, KV_LORA, H_B, DN_B + DV_B)
    w_ukv = jnp.concatenate([ukv[..., :DN_B].reshape(DEPTH, KV_LORA, H_B * DN_B),
                             ukv[..., DN_B:].reshape(DEPTH, KV_LORA, H_B * DV_B)], axis=-1).astype(BF16)
    row = lambda a: a.reshape(DEPTH, 1, a.shape[-1])
    return dict(
        w_t=w_t.astype(BF16), w_gate=w_gate,
        b_gate=row(jnp.concatenate([b_gla_af, b_gla_ab], axis=-1)),
        w_uq=w_uq, w_ukv=w_ukv, w_out=w_out.astype(BF16),
        g_pre=g_pre, g_post=g_post, g_mla_q=g_mla_q, g_mla_kv=g_mla_kv,
        g_gla4=row(jnp.tile(g_gla, (1, H_A))), g_diff4=row(jnp.tile(g_diff, (1, H_C))),
        lam=jnp.stack([lam_q1, lam_k1, lam_q2, lam_k2], axis=1))


def _rope_tables(n):
    t = np.arange(n)
    row = (t // GRID_W).astype(np.float32)
    col = (t % GRID_W).astype(np.float32)
    half = ROPE_DIM // 2
    inv = (1.0 / (np.float32(ROPE_THETA) ** (np.arange(0, half, 2, dtype=np.float32) / np.float32(half)))
           ).astype(np.float32)
    ar = row[:, None] * inv
    ac = col[:, None] * inv
    ang = np.concatenate([ar, ar, ac, ac], axis=-1).astype(np.float32)
    return (jnp.asarray(np.tile(np.cos(ang), (1, 8)).astype(np.float32)),
            jnp.asarray(np.tile(np.sin(ang), (1, 8)).astype(np.float32)))


def _sublayer(x2d, mod, pw, layer, *, seq, rope_tabs, ctx, caches):
    sample = ctx is not None
    pre = _pre_call(x2d, mod, pw, layer, seq=seq, rope_tabs=rope_tabs,
                    caches=None if sample else caches[:4])
    qk, v, la, bsum, ga, gb, gc, mq, mk, mvt, dqk, dvt = pre[:12]
    if sample:
        (oa,) = _gla_call(qk, v, la, bsum, ga, pw["g_gla4"], layer, seq=seq, state_in=ctx["state"])
        new_caches = None
    else:
        oa, sfin = _gla_call(qk, v, la, bsum, ga, pw["g_gla4"], layer, seq=seq,
                             state_out=caches[4] if caches else None)
        new_caches = tuple(pre[12:]) + (sfin,)
    ob, oc = _attn_call(mq, mk, mvt, gb, dqk, dvt, gc, pw["lam"], pw["g_diff4"], ctx, layer, seq=seq)
    y = _post_call(oa, ob, oc, x2d, mod, pw, layer, seq=seq, sample=sample)
    return y, new_caches


def kernel(x_prompt, x_sample, c, cache_mla_ckv, cache_mla_kpe, cache_diff_k, cache_diff_v, state_gla,
           c_ctx, w_ada, b_ada, g_pre, g_post, w_in, w_gla_af, b_gla_af, w_gla_ab, b_gla_ab, g_gla,
           g_mla_q, w_mla_uq, g_mla_kv, w_mla_ukv, lam_q1, lam_k1, lam_q2, lam_k2, g_diff, w_out):
    bp, tp, d = x_prompt.shape
    bs, ts, _ = x_sample.shape

    pw = _pack_params(g_pre, g_post, w_in, w_gla_af, b_gla_af, w_gla_ab, b_gla_ab, g_gla, g_mla_q,
                      w_mla_uq, g_mla_kv, w_mla_ukv, lam_q1, lam_k1, lam_q2, lam_k2, g_diff, w_out)
    mod = _mod_call(c_ctx.reshape(1, d), c, w_ada, b_ada)
    rope_tabs = _rope_tables(ts)
    ctx_k, ctx_vt = _ctxkv_call(cache_mla_ckv, jnp.swapaxes(cache_mla_kpe, -1, -2), pw["w_ukv"])
    ctx = dict(state=state_gla, mla_k=ctx_k, mla_vt=ctx_vt,
               diff_k_t=jnp.swapaxes(cache_diff_k, -1, -2), diff_v_t=jnp.swapaxes(cache_diff_v, -1, -2))

    y_p = x_prompt.reshape(bp * tp, d)
    y_s = x_sample.reshape(bs * ts, d)
    caches = ()
    for l in range(DEPTH):
        y_p, caches = _sublayer(y_p, mod, pw, l, seq=tp, rope_tabs=None, ctx=None, caches=caches)
        y_s, _ = _sublayer(y_s, mod, pw, l, seq=ts, rope_tabs=rope_tabs, ctx=ctx, caches=None)
    ckvn, kpe_t, kc_t, vc_t, new_state = caches
    return (y_p.reshape(bp, tp, d), y_s.reshape(bs, ts, d), ckvn, jnp.swapaxes(kpe_t, -1, -2),
            jnp.swapaxes(kc_t, -1, -2), jnp.swapaxes(vc_t, -1, -2), new_state)
```

```python
import functools
import math

import numpy as np
import jax
import jax.numpy as jnp
from jax import lax
from jax.experimental import pallas as pl
from jax.experimental.pallas import tpu as pltpu

F32 = jnp.float32
BF16 = jnp.bfloat16

D_MODEL = 1024
DEPTH = 2
GRID_W = 64
EPS = 1e-6
ROPE_THETA = 10000.0
ROPE_DIM = 32
H_A, DK_A, DV_A = 4, 32, 64
GLA_LR = 16
GLA_TAU = 16.0
GLA_CHUNK = 64
H_B, DN_B, DR_B, DV_B = 8, 64, 32, 64
Q_LORA, KV_LORA = 256, 128
H_C, DC = 4, 32
W_A, W_B, W_C = H_A * DV_A, H_B * DV_B, H_C * 2 * DC
ST_R, ST_C = H_A * DV_A, H_A * DK_A
LOG2E = math.log2(math.e)
SUBLANES = 8

R_QKV = ((0, 512),)
R_GG = ((544, 800),)
R_MLA = ((800, 1216), (512, 544), 64)
R_MG = ((1216, 1728),)
R_DIFF = ((1728, 2752),)
MOD_ROWS = 8

V7X_VMEM_LIMIT_BYTES = 56 * 1024 * 1024
TOKEN_BLOCK = 1024
POST_BLOCK = 1024
ATT_QBLOCK = 512
ATT_GROUP_ROWS = 1024
GLA_GROUP_ROWS = 1024
ATT_STATIC_BLOCKS = 4


def _cparams(n_axes=1):
    return pltpu.CompilerParams(dimension_semantics=("arbitrary",) * n_axes,
                                vmem_limit_bytes=V7X_VMEM_LIMIT_BYTES)


def _rms(x, g):
    return x * lax.rsqrt(jnp.mean(x * x, axis=-1, keepdims=True) + EPS) * g


def _silu(x):
    return x * jax.nn.sigmoid(x)


def _log_sigmoid(x):
    return jnp.minimum(x, 0.0) - jnp.log1p(jnp.exp(-jnp.abs(x)))


def _rope(z, cos, sin):
    w = z.shape[-1]
    lane = lax.broadcasted_iota(jnp.int32, z.shape, 1)
    rot = jnp.where((lane & 15) < 8, -pltpu.roll(z, w - 8, 1), pltpu.roll(z, 8, 1))
    return z * cos + rot * sin


def _lane_mask(width, lo, size, dtype):
    lane = lax.broadcasted_iota(jnp.int32, (1, width), 1)
    return jnp.where((lane >= lo) & (lane < lo + size), 1.0, 0.0).astype(dtype)


def _dot(a, b):
    return jnp.dot(a, b, preferred_element_type=F32)


def _dot_nt(a, b):
    return lax.dot_general(a, b, (((1,), (1,)), ((), ())), preferred_element_type=F32)


def _dot_tn(a, b):
    return lax.dot_general(a, b, (((0,), (0,)), ((), ())), preferred_element_type=F32)


def _layer_spec(shape, layer):
    nd = len(shape)
    if nd == 2:
        return pl.BlockSpec(tuple(shape), lambda *_: (0, 0))
    return pl.BlockSpec((1,) + tuple(shape[1:]), lambda *_: (layer,) + (0,) * (nd - 1))


def _layer_row(ref, layer):
    return ref[layer:layer + 1, :]


_ANY = pl.BlockSpec(memory_space=pl.ANY)


def _mod_kernel(cctx_ref, c_ref, w_ref, b_ref, o_ref, rows_sc):
    nc = c_ref.shape[0]
    rows_sc[0:1, :] = cctx_ref[...]
    rows_sc[1:1 + nc, :] = c_ref[...]
    if 1 + nc < MOD_ROWS:
        rows_sc[1 + nc:, :] = jnp.zeros((MOD_ROWS - 1 - nc, D_MODEL), F32)
    s = _silu(rows_sc[...]).astype(BF16)
    layer = pl.program_id(0)
    bias = b_ref[0:1, :]
    for r in range(1, DEPTH):
        bias = jnp.where(layer == r, b_ref[r:r + 1, :], bias)
    mod = _dot(s, w_ref[0].astype(BF16)) + bias
    for r in range(MOD_ROWS):
        o_ref[r] = mod[r:r + 1, :]


def _mod_call(c_ctx, c, w_ada, b_ada):
    nb = 1024
    assert 1 + c.shape[0] <= MOD_ROWS
    return pl.pallas_call(
        _mod_kernel,
        grid=(DEPTH, 3 * D_MODEL // nb),
        in_specs=[pl.BlockSpec((1, D_MODEL), lambda l, j: (0, 0)),
                  pl.BlockSpec(c.shape, lambda l, j: (0, 0)),
                  pl.BlockSpec((1, D_MODEL, nb), lambda l, j: (l, 0, j)),
                  pl.BlockSpec((DEPTH, nb), lambda l, j: (0, j))],
        out_specs=pl.BlockSpec((MOD_ROWS, 1, nb), lambda l, j: (l, 0, j)),
        out_shape=jax.ShapeDtypeStruct((DEPTH * MOD_ROWS, 1, 3 * D_MODEL), F32),
        scratch_shapes=[pltpu.VMEM((MOD_ROWS, D_MODEL), F32)],
        compiler_params=_cparams(2), name="adaln_mod",
    )(c_ctx, c, w_ada, b_ada)


def _ctxkv_kernel(ckv_ref, kpe_ref, w_ref, k_ref, vt_ref):
    kv = _dot(ckv_ref[0, 0].astype(BF16), w_ref[0])
    kpe4 = jnp.concatenate([kpe_ref[0, 0]] * 4, axis=0).T
    k_ref[0, 0, :, 0:512] = kv[:, 0:512].astype(BF16)
    k_ref[0, 0, :, 512:640] = kpe4.astype(BF16)
    vt_ref[0, 0] = kv[:, 512:1024].T.astype(BF16)


def _ctxkv_call(cache_ckv, cache_kpe_t, wukv):
    nb, _, tc, _ = cache_ckv.shape
    return pl.pallas_call(
        _ctxkv_kernel,
        grid=(DEPTH, nb),
        in_specs=[pl.BlockSpec((1, 1, tc, KV_LORA), lambda l, b: (b, l, 0, 0)),
                  pl.BlockSpec((1, 1, DR_B, tc), lambda l, b: (b, l, 0, 0)),
                  pl.BlockSpec((1, KV_LORA, 1024), lambda l, b: (l, 0, 0))],
        out_specs=[pl.BlockSpec((1, 1, tc, 640), lambda l, b: (l, b, 0, 0)),
                   pl.BlockSpec((1, 1, W_B, tc), lambda l, b: (l, b, 0, 0))],
        out_shape=[jax.ShapeDtypeStruct((DEPTH, nb, tc, 640), BF16),
                   jax.ShapeDtypeStruct((DEPTH, nb, W_B, tc), BF16)],
        compiler_params=_cparams(2), name="mla_ctx_kv",
    )(cache_ckv, cache_kpe_t, wukv)


def _pre_kernel(*refs, rope, ctx_out, alias_in, bpb, seq, layer):
    it = iter(refs)
    (x_ref, mod_ref, gpre_ref, w_ref, wg_ref, bg_ref, gq_ref, gkv_ref, wuq_ref,
     wukv_ref) = (next(it) for _ in range(10))
    if rope:
        cos_ref, sin_ref = next(it), next(it)
    for _ in range(alias_in):
        next(it)
    (qk_ref, v_ref, la_ref, bs_ref, ga_ref, gb_ref, gc_ref, mq_ref, mk_ref, mvt_ref, dqk_ref,
     dvt_ref) = (next(it) for _ in range(12))
    if ctx_out:
        ckvn_ref, kpe_ref, kc_ref, vc_ref = (next(it) for _ in range(4))

    d = D_MODEL
    shift = mod_ref[0, :, 0:d]
    scale = mod_ref[0, :, d:2 * d]
    h = (_rms(x_ref[...], _layer_row(gpre_ref, layer) * (1.0 + scale)) + shift).astype(BF16)
    def proj(group):
        parts = [jnp.zeros((r, d), BF16) if isinstance(r, int) else w_ref[0, r[0]:r[1], :] for r in group]
        return _dot_nt(h, parts[0] if len(parts) == 1 else jnp.concatenate(parts, axis=0))
    if rope:
        cos = cos_ref[...]
        sin = sin_ref[...]

    pg = proj(R_QKV)
    qk_ref[:, 0:128] = pg[:, 0:128] * (DK_A ** -0.5)
    qk_ref[:, 128:256] = pg[:, 128:256]
    v_ref[...] = pg[:, 256:512]
    pm = proj(R_MLA)
    tail = pm[:, 384:512]
    xg = _dot(tail.astype(BF16), wg_ref[0]) + bg_ref[0]
    la = _log_sigmoid(xg) * (1.0 / GLA_TAU)
    la_ref[...] = la
    bs_ref[:, 0:128] = _chunk_scan(la[:, 0:128], False)
    bs_ref[:, 128:256] = _chunk_scan(la[:, 128:256], True)
    ga_ref[...] = _silu(proj(R_GG)).astype(BF16)

    qall = _dot(_rms(pm[:, 0:256], _layer_row(gq_ref, layer)).astype(BF16), wuq_ref[0])
    q_pe = qall[:, 512:768]
    if rope:
        q_pe = _rope(q_pe, cos, sin)
    sb = (DN_B + DR_B) ** -0.5 * LOG2E
    mq_ref[:, 0:512] = (qall[:, 0:512] * sb).astype(BF16)
    mq_ref[:, 512:768] = (q_pe * sb).astype(BF16)
    ckvn = _rms(pm[:, 256:384], _layer_row(gkv_ref, layer))
    kvall = _dot(ckvn.astype(BF16), wukv_ref[0])
    lane = lax.broadcasted_iota(jnp.int32, tail.shape, 1)
    kpe4 = jnp.where(lane < DR_B, tail, 0.0)
    kpe4 = kpe4 + pltpu.roll(kpe4, DR_B, 1)
    kpe4 = kpe4 + pltpu.roll(kpe4, 2 * DR_B, 1)
    if rope:
        kpe4 = _rope(kpe4, cos[:, 0:128], sin[:, 0:128])
    mk_ref[:, 0:512] = kvall[:, 0:512].astype(BF16)
    mk_ref[:, 512:640] = kpe4.astype(BF16)
    mvt_ref[...] = kvall[:, 512:1024].T.astype(BF16)
    gb_ref[...] = _silu(proj(R_MG)).astype(BF16)

    pd = proj(R_DIFF)
    dq, dk, dv = pd[:, 0:256], pd[:, 256:512], pd[:, 512:768]
    if rope:
        dq = _rope(dq, cos, sin)
        dk = _rope(dk, cos, sin)
    dqk_ref[:, 0:256] = (dq * (DC ** -0.5 * LOG2E)).astype(BF16)
    dqk_ref[:, 256:512] = dk.astype(BF16)
    dv_t = dv.T
    dvt_ref[...] = dv_t.astype(BF16)
    gc_ref[...] = _silu(pd[:, 768:1024]).astype(BF16)
    if ctx_out:
        kpe_t = kpe4.T
        dk_t = dk.T
        for ref in () if alias_in else (ckvn_ref, kpe_ref, kc_ref, vc_ref):
            ref[:, 1:] = jnp.zeros(ref[:, 1:].shape, F32)
        for bb in range(bpb):
            rs = slice(bb * seq, (bb + 1) * seq)
            ckvn_ref[bb, 0] = ckvn[rs]
            kpe_ref[bb, 0] = kpe_t[0:DR_B, rs]
            kc_ref[bb, 0] = dk_t[:, rs].reshape(H_C, 2 * DC, seq)
            vc_ref[bb, 0] = dv_t[:, rs].reshape(H_C, 2 * DC, seq)


def _pre_call(x2d, mod, pw, layer, *, seq, rope_tabs, caches):
    n = x2d.shape[0]
    tm = min(TOKEN_BLOCK, n)
    bpb = max(tm // seq, 1)
    rope = rope_tabs is not None
    ctx_out = caches is not None
    steps_per_seq = max(seq // tm, 1)
    nbt = n // seq

    def mod_idx(i):
        return (layer * MOD_ROWS + ((i * tm) // seq + 1 if rope else 0), 0, 0)

    names = ["g_pre", "w_t", "w_gate", "b_gate", "g_mla_q", "g_mla_kv", "w_uq", "w_ukv"]
    in_specs = [pl.BlockSpec((tm, D_MODEL), lambda i: (i, 0)), pl.BlockSpec((1, 1, 3 * D_MODEL), mod_idx)]
    in_specs += [_layer_spec(pw[k].shape, layer) for k in names]
    args = [x2d, mod] + [pw[k] for k in names]
    if rope:
        in_specs += [pl.BlockSpec((tm, 256), lambda i: (i % steps_per_seq, 0))] * 2
        args += list(rope_tabs)
    outs = [(256, F32, False), (256, F32, False), (256, F32, False), (256, F32, False),
            (W_A, BF16, False), (W_B, BF16, False), (W_C, BF16, False),
            (768, BF16, False), (640, BF16, False), (W_B, BF16, True),
            (512, BF16, False), (W_C, BF16, True)]
    out_specs = [pl.BlockSpec((w, tm), lambda i: (0, i)) if tr else pl.BlockSpec((tm, w), lambda i: (i, 0))
                 for w, _, tr in outs]
    out_shape = [jax.ShapeDtypeStruct((w, n) if tr else (n, w), dt) for w, dt, tr in outs]
    widths = outs
    aliases = {}
    if ctx_out:
        assert caches or layer == 0
        nl = 1 if caches else DEPTH
        out_specs += [pl.BlockSpec((bpb, nl, seq, KV_LORA), lambda i: (i, layer, 0, 0)),
                      pl.BlockSpec((bpb, nl, DR_B, seq), lambda i: (i, layer, 0, 0)),
                      pl.BlockSpec((bpb, nl, H_C, 2 * DC, seq), lambda i: (i, layer, 0, 0, 0)),
                      pl.BlockSpec((bpb, nl, H_C, 2 * DC, seq), lambda i: (i, layer, 0, 0, 0))]
        out_shape += [jax.ShapeDtypeStruct((nbt, DEPTH, seq, KV_LORA), F32),
                      jax.ShapeDtypeStruct((nbt, DEPTH, DR_B, seq), F32),
                      jax.ShapeDtypeStruct((nbt, DEPTH, H_C, 2 * DC, seq), F32),
                      jax.ShapeDtypeStruct((nbt, DEPTH, H_C, 2 * DC, seq), F32)]
        for j, arr in enumerate(caches):
            aliases[len(args)] = len(widths) + j
            in_specs.append(_ANY)
            args.append(arr)
    return pl.pallas_call(
        functools.partial(_pre_kernel, rope=rope, ctx_out=ctx_out, alias_in=len(aliases), bpb=bpb, seq=seq,
                          layer=layer),
        grid=(n // tm,), in_specs=in_specs, out_specs=out_specs, out_shape=out_shape,
        input_output_aliases=aliases,
        compiler_params=_cparams(), name="pre_rope" if rope else "pre_ctx",
    )(*args)


_GLA_LEVELS = (1, 2, 4, 8, 16, 32)


def _gla_consts(rev):
    c = GLA_CHUNK
    row = lax.broadcasted_iota(jnp.int32, (c, 128), 0)
    pos = (c - 1 - row) if rev else row
    ri = lax.broadcasted_iota(jnp.int32, (c, H_A * c), 0)
    cj = lax.broadcasted_iota(jnp.int32, (c, H_A * c), 1) & (c - 1)
    pi = (c - 1 - ri) if rev else ri
    pj = (c - 1 - cj) if rev else cj
    x = pi ^ pj
    lvl = jnp.where(pi == pj, 0, -1)
    for kbit in range(6):
        lvl = jnp.where((pj < pi) & ((x >> kbit) == 1), kbit + 1, lvl)
    return pos, lvl


def _chunk_scan(x, rev):
    rows = x.shape[0]
    nt = rows // SUBLANES
    tiles_per_chunk = GLA_CHUNK // SUBLANES
    x3 = x.reshape(nt, SUBLANES, 128)
    sub = lax.broadcasted_iota(jnp.int32, x3.shape, 1)
    tile = lax.broadcasted_iota(jnp.int32, x3.shape, 0) & (tiles_per_chunk - 1)
    edge = 0 if rev else SUBLANES - 1
    s = 1
    while s < SUBLANES:
        if rev:
            x3 = x3 + jnp.where(sub < SUBLANES - s, pltpu.roll(x3, SUBLANES - s, 1), 0.0)
        else:
            x3 = x3 + jnp.where(sub >= s, pltpu.roll(x3, s, 1), 0.0)
        s *= 2
    s = 1
    while s < tiles_per_chunk:
        tot = jnp.broadcast_to(x3[:, edge:edge + 1, :], x3.shape)
        if rev:
            shifted = jnp.concatenate([tot[s:], tot[:s]], axis=0)
            x3 = x3 + jnp.where(tile < tiles_per_chunk - s, shifted, 0.0)
        else:
            shifted = jnp.concatenate([tot[nt - s:], tot[:nt - s]], axis=0)
            x3 = x3 + jnp.where(tile >= s, shifted, 0.0)
        s *= 2
    return x3.reshape(rows, 128)


def _gla_scores(q, k, la, b, pos, lvl, hm_bf, rev):
    c = GLA_CHUNK
    prv = pltpu.roll(la, c - 1 if rev else 1, 0)
    nxt = pltpu.roll(la, 1 if rev else c - 1, 0)
    lvl = lvl.astype(BF16)
    s_tot = jnp.where(lvl == 0, _dot_nt(q.astype(BF16), jnp.concatenate([k.astype(BF16)] * H_A, 0) * hm_bf
                                        ).astype(BF16), jnp.zeros((), BF16))
    for kbit, m in enumerate(_GLA_LEVELS):
        up = ((pos >> kbit) & 1) == 1
        if m == 1:
            e = jnp.where(up, la, 0.0)
        elif m == 2:
            c4 = pos & 3
            e = jnp.where(c4 == 0, nxt, jnp.where(c4 == 1, 0.0, jnp.where(c4 == 2, la, la + prv)))
        else:
            nblk = c // (2 * m)
            loc = m if rev else m - 1
            b3 = b.reshape(nblk, 2 * m, 128)
            ref = jnp.broadcast_to(b3[:, loc:loc + 1, :], (nblk, 2 * m, 128)).reshape(c, 128)
            dlt = b - ref
            e = jnp.where(up, dlt, -dlt)
        xm = (jnp.where(up, q, k) * jnp.exp(e)).astype(BF16)
        sm = _dot_nt(xm, jnp.concatenate([xm] * H_A, 0) * hm_bf)
        s_tot = jnp.where(lvl == kbit + 1, sm.astype(BF16), s_tot)
    return s_tot


def _gla_apply(s_tot, q, k, v, b, st_prev, hm_f32, vm_bf, rev):
    c = GLA_CHUNK
    vbd = jnp.concatenate([v] * H_A, 0) * vm_bf
    blast = b[0:1, :] if rev else b[c - 1:c, :]
    qbar = (q * jnp.exp(b)).astype(BF16)
    kdec = (k * jnp.exp(blast - b)).astype(BF16)
    o = _dot(s_tot, vbd) + _dot_nt(qbar, st_prev.astype(BF16))
    st_new = st_prev * jnp.exp(blast) + _dot_tn(v, kdec) * hm_f32
    return o, st_new


def _gla_kernel(*refs, seq, has_s0, alias_in, spb):
    it = iter(refs)
    qk_ref, v_ref, la_ref, b_ref, gate_ref, g_ref = (next(it) for _ in range(6))
    s0_ref = next(it) if has_s0 else None
    for _ in range(alias_in):
        next(it)
    oa_ref = next(it)
    sfin_ref = None if has_s0 else next(it)
    acc_sc, st_sc, blk_sc = next(it), next(it), next(it)

    c = GLA_CHUNK
    nc = seq // c
    acc_sc[...] = jnp.zeros_like(acc_sc)
    if not has_s0 and not alias_in:
        sfin_ref[:, 1:] = jnp.zeros(sfin_ref[:, 1:].shape, F32)

    hrow = lax.broadcasted_iota(jnp.int32, (H_A * c, 128), 0) // c
    hm_f32 = jnp.where(hrow == lax.broadcasted_iota(jnp.int32, (H_A * c, 128), 1) // DK_A, 1.0, 0.0)
    hm_bf = hm_f32.astype(BF16)
    vrow = lax.broadcasted_iota(jnp.int32, (H_A * c, H_A * DV_A), 0) // c
    vm_bf = jnp.where(vrow == lax.broadcasted_iota(jnp.int32, (H_A * c, H_A * DV_A), 1) // DV_A,
                      1.0, 0.0).astype(BF16)
    consts = (_gla_consts(False), _gla_consts(True))

    def run_sequence(bb):
        base = bb * seq
        for d in range(2):
            if has_s0:
                blk_sc[...] = jnp.zeros_like(blk_sc)
                for hh in range(H_A):
                    blk_sc[DK_A * hh:DK_A * (hh + 1), DV_A * hh:DV_A * (hh + 1)] = s0_ref[bb, 0, d, hh]
                st_sc[d] = blk_sc[...].T
            else:
                st_sc[d] = jnp.zeros((ST_R, ST_C), F32)

        def chunk_rows(n, d):
            cn = (nc - 1 - n) if d else n
            start = base + cn * c
            return pl.ds(start if isinstance(start, int) else pl.multiple_of(start, c), c)

        def scores(n):
            out = []
            for d in range(2):
                rows = chunk_rows(n, d)
                pos, lvl = consts[d]
                out.append(_gla_scores(qk_ref[rows, 0:128], qk_ref[rows, 128:256],
                                       la_ref[rows, 128 * d:128 * d + 128],
                                       b_ref[rows, 128 * d:128 * d + 128], pos, lvl, hm_bf, bool(d)))
            return tuple(out)

        def apply(n, s_both):
            for d in range(2):
                rows = chunk_rows(n, d)
                o, st_new = _gla_apply(s_both[d], qk_ref[rows, 0:128], qk_ref[rows, 128:256],
                                       v_ref[rows, :].astype(BF16), b_ref[rows, 128 * d:128 * d + 128],
                                       st_sc[d], hm_f32, vm_bf, bool(d))
                acc_sc[rows, :] = acc_sc[rows, :] + o
                st_sc[d] = st_new

        def body(n, s_cur):
            s_next = scores(n + 1)
            apply(n, s_cur)
            return s_next

        trips = nc - 1
        s_last = lax.fori_loop(0, trips, body, scores(0),
                               unroll=next(u for u in (5, 3, 1) if trips % u == 0))
        apply(nc - 1, s_last)

        rs = slice(base, base + seq)
        first = lax.broadcasted_iota(jnp.int32, (seq, 128), 1) < DV_A
        for p in range(H_A // 2):
            cols = slice(128 * p, 128 * (p + 1))
            oa = acc_sc[rs, cols]
            sq = oa * oa
            s0 = jnp.sum(jnp.where(first, sq, 0.0), axis=-1, keepdims=True)
            s1 = jnp.sum(jnp.where(first, 0.0, sq), axis=-1, keepdims=True)
            inv = jnp.where(first, lax.rsqrt(s0 * (1.0 / DV_A) + EPS), lax.rsqrt(s1 * (1.0 / DV_A) + EPS))
            oa_ref[rs, cols] = (oa * inv * g_ref[0, :, cols] * gate_ref[rs, cols]).astype(BF16)
        if not has_s0:
            for d in range(2):
                blk_sc[...] = st_sc[d].T
                for hh in range(H_A):
                    sfin_ref[bb, 0, d, hh] = blk_sc[DK_A * hh:DK_A * (hh + 1), DV_A * hh:DV_A * (hh + 1)]

    for bb in range(spb):
        run_sequence(bb)


def _gla_call(qk, v, la, bsum, gate, g4, layer, *, seq, state_in=None, state_out=None):
    n = qk.shape[0]
    nb = n // seq
    has_s0 = state_in is not None
    assert has_s0 or state_out is not None or layer == 0
    spb = max(1, min(nb, GLA_GROUP_ROWS // seq))
    blk = lambda w: pl.BlockSpec((spb * seq, w), lambda i: (i, 0))
    nl = DEPTH if (not has_s0 and state_out is None) else 1
    st_spec = pl.BlockSpec((spb, nl, 2, H_A, DK_A, DV_A), lambda i: (i, layer, 0, 0, 0, 0))
    in_specs = [blk(256), blk(256), blk(256), blk(256), blk(W_A), _layer_spec(g4.shape, layer)]
    args = [qk, v, la, bsum, gate, g4]
    out_specs = [blk(W_A)]
    out_shape = [jax.ShapeDtypeStruct((n, W_A), BF16)]
    aliases = {}
    if has_s0:
        in_specs.append(st_spec)
        args.append(state_in)
    else:
        out_specs.append(st_spec)
        out_shape.append(jax.ShapeDtypeStruct((nb, DEPTH, 2, H_A, DK_A, DV_A), F32))
        if state_out is not None:
            aliases[len(args)] = 1
            in_specs.append(_ANY)
            args.append(state_out)
    return pl.pallas_call(
        functools.partial(_gla_kernel, seq=seq, has_s0=has_s0, alias_in=len(aliases), spb=spb),
        grid=(nb // spb,), in_specs=in_specs, out_specs=out_specs, out_shape=out_shape,
        input_output_aliases=aliases,
        scratch_shapes=[pltpu.VMEM((spb * seq, W_A), F32), pltpu.VMEM((2, ST_R, ST_C), F32),
                        pltpu.VMEM((ST_C, ST_R), F32)],
        compiler_params=_cparams(), name="gla_state" if has_s0 else "gla_ctx",
    )(*args)


def _softmax_t_pv(st, vt, ones_rows):
    dv, keys = vt.shape
    m = _col_reduce(st, jnp.max)
    e = jnp.exp2(st - m)
    if not ones_rows:
        l = _col_reduce(e, jnp.sum)
        return _dot(vt, e.astype(BF16)) * (1.0 / l)
    o = _dot(jnp.concatenate([vt, jnp.ones((16, keys), BF16)], axis=0), e.astype(BF16))
    return o[:dv] * (1.0 / o[dv:dv + 1])


def _pipelined_attention(score_fns, value_fns, depth):
    outs = []
    pending = [fn() for fn in score_fns[:depth]]
    for j, vfn in enumerate(value_fns):
        st = pending.pop(0)
        if j + depth < len(score_fns):
            pending.append(score_fns[j + depth]())
        outs.append(_softmax_t_pv(st, *vfn()))
    return outs


def _lookahead(n_keys):
    return 2 if n_keys >= 1024 else 3


def _col_reduce(x, op):
    rows, cols = x.shape
    part = 128 if rows % 128 == 0 and rows > 128 else rows
    if part != rows:
        x = op(x.reshape(rows // part, part, cols), axis=0)
    return op(x, axis=0, keepdims=True)


def _seqs_per_step(n, seq, ctx_len):
    if ctx_len or seq >= ATT_QBLOCK:
        return 1
    return max(1, min(n // seq, ATT_GROUP_ROWS // seq))


def _for_query_blocks(run, seq, spb):
    qb = min(ATT_QBLOCK, seq)
    nq = seq // qb
    if spb * nq <= ATT_STATIC_BLOCKS:
        run([(bb, slice(bb * seq + j * qb, bb * seq + (j + 1) * qb)) for bb in range(spb) for j in range(nq)])
    else:
        assert spb == 1

        def body(i, carry):
            run([(0, pl.ds(pl.multiple_of(i * qb, qb), qb))])
            return carry
        lax.fori_loop(0, nq, body, 0)


def _mla_part(q_ref, k_ref, vt_ref, gate_ref, ck_ref, cvt_ref, ob_ref, kk_sc, vt_sc, *, seq, ctx_len, spb):
    for bb in range(spb):
        rs = slice(bb * seq, (bb + 1) * seq)
        for p in range(H_B // 2):
            kk_sc[bb, p, 0:seq, 0:128] = k_ref[rs, 128 * p:128 * p + 128]
            kk_sc[bb, p, 0:seq, 128:256] = k_ref[rs, 512:640]
            if ctx_len:
                kk_sc[bb, p, seq:seq + ctx_len, 0:128] = ck_ref[0, 0, :, 128 * p:128 * p + 128]
                kk_sc[bb, p, seq:seq + ctx_len, 128:256] = ck_ref[0, 0, :, 512:640]
        vt_sc[bb, :, 0:seq] = vt_ref[:, rs]
        if ctx_len:
            vt_sc[bb, :, seq:seq + ctx_len] = cvt_ref[0, 0]

    def scores(bb, rows, h):
        p, hh = divmod(h, 2)
        qn = q_ref[rows, 128 * p:128 * p + 128] * _lane_mask(128, DN_B * hh, DN_B, BF16)
        qp = (q_ref[rows, 512 + 128 * (h // 4):512 + 128 * (h // 4) + 128]
              * _lane_mask(128, DR_B * (h % 4), DR_B, BF16))
        return _dot_nt(kk_sc[bb, p], jnp.concatenate([qn, qp], axis=-1))

    def items(blocks):
        its = [(bb, rows, h) for bb, rows in blocks for h in range(H_B)]

        def finish(outs):
            for j, (_, rows) in enumerate(blocks):
                ob = jnp.concatenate(outs[H_B * j:H_B * (j + 1)], axis=0).T
                ob_ref[rows, :] = (ob * gate_ref[rows, :]).astype(BF16)

        return ([functools.partial(scores, bb, rows, h) for bb, rows, h in its],
                [functools.partial(lambda bb, h: (vt_sc[bb, DV_B * h:DV_B * (h + 1), :], not ctx_len), bb, h)
                 for bb, _, h in its],
                finish)

    return items


def _diff_part(qk_ref, vt_ref, gate_ref, lam_ref, g_ref, ck_ref, cv_ref, oc_ref, k_sc, vt_sc, *,
               seq, ctx_len, lam_init, spb):
    lam = (jnp.exp(jnp.sum(lam_ref[0, 0:1, :] * lam_ref[0, 1:2, :], axis=-1, keepdims=True))
           - jnp.exp(jnp.sum(lam_ref[0, 2:3, :] * lam_ref[0, 3:4, :], axis=-1, keepdims=True)) + lam_init)
    dh = 2 * DC
    for bb in range(spb):
        rs = slice(bb * seq, (bb + 1) * seq)
        for p in range(H_C // 2):
            k_sc[bb, p, 0:seq, :] = qk_ref[rs, 256 + 128 * p:256 + 128 * p + 128]
            if ctx_len:
                pair_t = jnp.concatenate([ck_ref[0, 0, 2 * p], ck_ref[0, 0, 2 * p + 1]], axis=0)
                k_sc[bb, p, seq:seq + ctx_len, :] = pair_t.T.astype(BF16)
        vt_sc[bb, :, 0:seq] = vt_ref[:, rs]
        if ctx_len:
            for h in range(H_C):
                vt_sc[bb, dh * h:dh * (h + 1), seq:seq + ctx_len] = cv_ref[0, 0, h].astype(BF16)

    def scores(bb, rows, h, comp):
        p, hh = divmod(h, 2)
        qm = qk_ref[rows, 128 * p:128 * p + 128] * _lane_mask(128, dh * hh + DC * comp, DC, BF16)
        return _dot_nt(k_sc[bb, p], qm)

    def items(blocks):
        its = [(bb, rows, h, comp) for bb, rows in blocks for h in range(H_C) for comp in range(2)]

        def finish(o12):
            for j, (_, rows) in enumerate(blocks):
                outs = []
                for h in range(H_C):
                    o1, o2 = o12[2 * (H_C * j + h)], o12[2 * (H_C * j + h) + 1]
                    ot = o1 - lam * o2
                    outs.append(ot * lax.rsqrt(jnp.mean(ot * ot, axis=0, keepdims=True) + EPS))
                oc = jnp.concatenate(outs, axis=0).T
                oc_ref[rows, :] = (oc * g_ref[0] * (1.0 - lam_init) * gate_ref[rows, :]).astype(BF16)

        return ([functools.partial(scores, *item) for item in its],
                [functools.partial(lambda bb, h: (vt_sc[bb, dh * h:dh * (h + 1), :], True), bb, h)
                 for bb, _, h, _ in its],
                finish)

    return items


def _attn_kernel(*refs, seq, ctx_len, lam_init, spb, mla, diff):
    it = iter(refs)
    n_ctx = 2 if ctx_len else 0
    mla_in = [next(it) for _ in range(4 + n_ctx)] + [None] * (2 - n_ctx) if mla else None
    diff_in = [next(it) for _ in range(5 + n_ctx)] + [None] * (2 - n_ctx) if diff else None
    ob_ref = next(it) if mla else None
    oc_ref = next(it) if diff else None
    parts = []
    if mla:
        parts.append(_mla_part(*mla_in, ob_ref, next(it), next(it), seq=seq, ctx_len=ctx_len, spb=spb))
    if diff:
        parts.append(_diff_part(*diff_in, oc_ref, next(it), next(it), seq=seq, ctx_len=ctx_len,
                                lam_init=lam_init, spb=spb))

    def run(blocks):
        built = [part(blocks) for part in parts]
        outs = _pipelined_attention([f for b in built for f in b[0]], [f for b in built for f in b[1]],
                                    _lookahead(seq + ctx_len))
        lo = 0
        for score_fns, _, finish in built:
            finish(outs[lo:lo + len(score_fns)])
            lo += len(score_fns)

    _for_query_blocks(run, seq, spb)


def _attn_call(mq, mk, mvt, gate_b, dqk, dvt, gate_c, lamp, g4, ctx, layer, *, seq):
    n = mq.shape[0]
    ctx_len = 0 if ctx is None else ctx["mla_k"].shape[2]
    lam_init = 0.8 - 0.6 * math.exp(-0.3 * layer)
    spb = _seqs_per_step(n, seq, ctx_len)
    rows = spb * seq
    tk = seq + ctx_len
    rblk = lambda w: pl.BlockSpec((rows, w), lambda i: (i, 0))
    tblk = lambda w: pl.BlockSpec((w, rows), lambda i: (0, i))
    mla_specs, mla_args = [rblk(768), rblk(640), tblk(W_B), rblk(W_B)], [mq, mk, mvt, gate_b]
    diff_specs = [rblk(512), tblk(W_C), rblk(W_C), _layer_spec(lamp.shape, layer), _layer_spec(g4.shape, layer)]
    diff_args = [dqk, dvt, gate_c, lamp, g4]
    if ctx_len:
        mla_specs += [pl.BlockSpec((1, 1, ctx_len, 640), lambda i: (layer, i, 0, 0)),
                      pl.BlockSpec((1, 1, W_B, ctx_len), lambda i: (layer, i, 0, 0))]
        mla_args += [ctx["mla_k"], ctx["mla_vt"]]
        diff_specs += [pl.BlockSpec((1, 1, H_C, 2 * DC, ctx_len), lambda i: (i, layer, 0, 0, 0))] * 2
        diff_args += [ctx["diff_k_t"], ctx["diff_v_t"]]
    mla_out = (rblk(W_B), jax.ShapeDtypeStruct((n, W_B), BF16))
    diff_out = (rblk(W_C), jax.ShapeDtypeStruct((n, W_C), BF16))
    mla_scratch = [pltpu.VMEM((spb, H_B // 2, tk, 256), BF16), pltpu.VMEM((spb, W_B, tk), BF16)]
    diff_scratch = [pltpu.VMEM((spb, H_C // 2, tk, 128), BF16), pltpu.VMEM((spb, W_C, tk), BF16)]

    def call(mla, diff, name):
        outs = ([mla_out] if mla else []) + ([diff_out] if diff else [])
        return pl.pallas_call(
            functools.partial(_attn_kernel, seq=seq, ctx_len=ctx_len, lam_init=lam_init, spb=spb,
                              mla=mla, diff=diff),
            grid=(n // rows,),
            in_specs=(mla_specs if mla else []) + (diff_specs if diff else []),
            out_specs=[o[0] for o in outs], out_shape=[o[1] for o in outs],
            scratch_shapes=(mla_scratch if mla else []) + (diff_scratch if diff else []),
            compiler_params=_cparams(), name=name,
        )(*(mla_args if mla else []), *(diff_args if diff else []))

    if ctx_len:
        (ob,) = call(True, False, "mla_ctx")
        (oc,) = call(False, True, "diff_ctx")
        return ob, oc
    return call(True, True, "attn_self")


def _post_kernel(oa_ref, ob_ref, oc_ref, x_ref, mod_ref, w_ref, g_ref, y_ref, *, layer):
    mix = jnp.concatenate([oa_ref[...], ob_ref[...], oc_ref[...]], axis=-1)
    y_ref[...] = x_ref[...] + _rms(_dot(mix, w_ref[0]),
                                   _layer_row(g_ref, layer) * mod_ref[0, :, 2 * D_MODEL:3 * D_MODEL])


def _post_call(oa, ob, oc, x2d, mod, pw, layer, *, seq, sample):
    n = x2d.shape[0]
    tm = min(POST_BLOCK, seq if sample else n)
    blk = lambda w: pl.BlockSpec((tm, w), lambda i: (i, 0))

    def mod_idx(i):
        return (layer * MOD_ROWS + ((i * tm) // seq + 1 if sample else 0), 0, 0)

    return pl.pallas_call(
        functools.partial(_post_kernel, layer=layer),
        grid=(n // tm,),
        in_specs=[blk(W_A), blk(W_B), blk(W_C), blk(D_MODEL),
                  pl.BlockSpec((1, 1, 3 * D_MODEL), mod_idx),
                  _layer_spec(pw["w_out"].shape, layer), _layer_spec(pw["g_post"].shape, layer)],
        out_specs=blk(D_MODEL),
        out_shape=jax.ShapeDtypeStruct((n, D_MODEL), F32),
        compiler_params=_cparams(), name="post",
    )(oa, ob, oc, x2d, mod, pw["w_out"], pw["g_post"])


def _pack_params(g_pre, g_post, w_in, w_gla_af, b_gla_af, w_gla_ab, b_gla_ab, g_gla, g_mla_q, w_mla_uq,
                 g_mla_kv, w_mla_ukv, lam_q1, lam_k1, lam_q2, lam_k2, g_diff, w_out):
    w_t = jnp.swapaxes(w_in, 1, 2)
    assert w_t.shape[1] == R_DIFF[0][1]
    zg = jnp.zeros((DEPTH, GLA_LR, 128), F32)
    w_gate = jnp.concatenate([jnp.zeros((DEPTH, DR_B, 256), F32),
                              jnp.concatenate([w_gla_af, zg], axis=-1),
                              jnp.concatenate([zg, w_gla_ab], axis=-1),
                              jnp.zeros((DEPTH, 128 - DR_B - 2 * GLA_LR, 256), F32)], axis=1).astype(BF16)
    uq = w_mla_uq.reshape(DEPTH, Q_LORA, H_B, DN_B + DR_B)
    w_pe = uq[..., DN_B:].reshape(DEPTH, Q_LORA, H_B * DR_B)
    w_uq = jnp.concatenate([uq[..., :DN_B].reshape(DEPTH, Q_LORA, H_B * DN_B), w_pe], axis=-1).astype(BF16)
    ukv = w_mla_ukv.reshape(DEPTH, KV_LORA, H_B, DN_B + DV_B)
    w_ukv = jnp.concatenate([ukv[..., :DN_B].reshape(DEPTH, KV_LORA, H_B * DN_B),
                             ukv[..., DN_B:].reshape(DEPTH, KV_LORA, H_B * DV_B)], axis=-1).astype(BF16)
    row = lambda a: a.reshape(DEPTH, 1, a.shape[-1])
    return dict(
        w_t=w_t.astype(BF16), w_gate=w_gate,
        b_gate=row(jnp.concatenate([b_gla_af, b_gla_ab], axis=-1)),
        w_uq=w_uq, w_ukv=w_ukv, w_out=w_out.astype(BF16),
        g_pre=g_pre, g_post=g_post, g_mla_q=g_mla_q, g_mla_kv=g_mla_kv,
        g_gla4=row(jnp.tile(g_gla, (1, H_A))), g_diff4=row(jnp.tile(g_diff, (1, H_C))),
        lam=jnp.stack([lam_q1, lam_k1, lam_q2, lam_k2], axis=1))


def _rope_tables(n):
    t = np.arange(n)
    row = (t // GRID_W).astype(np.float32)
    col = (t % GRID_W).astype(np.float32)
    half = ROPE_DIM // 2
    inv = (1.0 / (np.float32(ROPE_THETA) ** (np.arange(0, half, 2, dtype=np.float32) / np.float32(half)))
           ).astype(np.float32)
    ar = row[:, None] * inv
    ac = col[:, None] * inv
    ang = np.concatenate([ar, ar, ac, ac], axis=-1).astype(np.float32)
    return (jnp.asarray(np.tile(np.cos(ang), (1, 8)).astype(np.float32)),
            jnp.asarray(np.tile(np.sin(ang), (1, 8)).astype(np.float32)))


def _sublayer(x2d, mod, pw, layer, *, seq, rope_tabs, ctx, caches):
    sample = ctx is not None
    pre = _pre_call(x2d, mod, pw, layer, seq=seq, rope_tabs=rope_tabs,
                    caches=None if sample else caches[:4])
    qk, v, la, bsum, ga, gb, gc, mq, mk, mvt, dqk, dvt = pre[:12]
    if sample:
        (oa,) = _gla_call(qk, v, la, bsum, ga, pw["g_gla4"], layer, seq=seq, state_in=ctx["state"])
        new_caches = None
    else:
        oa, sfin = _gla_call(qk, v, la, bsum, ga, pw["g_gla4"], layer, seq=seq,
                             state_out=caches[4] if caches else None)
        new_caches = tuple(pre[12:]) + (sfin,)
    ob, oc = _attn_call(mq, mk, mvt, gb, dqk, dvt, gc, pw["lam"], pw["g_diff4"], ctx, layer, seq=seq)
    y = _post_call(oa, ob, oc, x2d, mod, pw, layer, seq=seq, sample=sample)
    return y, new_caches


def kernel(x_prompt, x_sample, c, cache_mla_ckv, cache_mla_kpe, cache_diff_k, cache_diff_v, state_gla,
           c_ctx, w_ada, b_ada, g_pre, g_post, w_in, w_gla_af, b_gla_af, w_gla_ab, b_gla_ab, g_gla,
           g_mla_q, w_mla_uq, g_mla_kv, w_mla_ukv, lam_q1, lam_k1, lam_q2, lam_k2, g_diff, w_out):
    bp, tp, d = x_prompt.shape
    bs, ts, _ = x_sample.shape

    pw = _pack_params(g_pre, g_post, w_in, w_gla_af, b_gla_af, w_gla_ab, b_gla_ab, g_gla, g_mla_q,
                      w_mla_uq, g_mla_kv, w_mla_ukv, lam_q1, lam_k1, lam_q2, lam_k2, g_diff, w_out)
    mod = _mod_call(c_ctx.reshape(1, d), c, w_ada, b_ada)
    rope_tabs = _rope_tables(ts)
    ctx_k, ctx_vt = _ctxkv_call(cache_mla_ckv, jnp.swapaxes(cache_mla_kpe, -1, -2), pw["w_ukv"])
    ctx = dict(state=state_gla, mla_k=ctx_k, mla_vt=ctx_vt,
               diff_k_t=jnp.swapaxes(cache_diff_k, -1, -2), diff_v_t=jnp.swapaxes(cache_diff_v, -1, -2))

    y_p = x_prompt.reshape(bp * tp, d)
    y_s = x_sample.reshape(bs * ts, d)
    caches = ()
    for l in range(DEPTH):
        y_p, caches = _sublayer(y_p, mod, pw, l, seq=tp, rope_tabs=None, ctx=None, caches=caches)
        y_s, _ = _sublayer(y_s, mod, pw, l, seq=ts, rope_tabs=rope_tabs, ctx=ctx, caches=None)
    ckvn, kpe_t, kc_t, vc_t, new_state = caches
    return (y_p.reshape(bp, tp, d), y_s.reshape(bs, ts, d), ckvn, jnp.swapaxes(kpe_t, -1, -2),
            jnp.swapaxes(kc_t, -1, -2), jnp.swapaxes(vc_t, -1, -2), new_state)
```

```python
import functools
import math

import numpy as np
import jax
import jax.numpy as jnp
from jax import lax
from jax.experimental import pallas as pl
from jax.experimental.pallas import tpu as pltpu

F32 = jnp.float32
BF16 = jnp.bfloat16

D_MODEL = 1024
DEPTH = 2
GRID_W = 64
EPS = 1e-6
ROPE_THETA = 10000.0
ROPE_DIM = 32
H_A, DK_A, DV_A = 4, 32, 64
GLA_LR = 16
GLA_TAU = 16.0
GLA_CHUNK = 64
H_B, DN_B, DR_B, DV_B = 8, 64, 32, 64
Q_LORA, KV_LORA = 256, 128
H_C, DC = 4, 32
W_A, W_B, W_C = H_A * DV_A, H_B * DV_B, H_C * 2 * DC
ST_R, ST_C = H_A * DV_A, H_A * DK_A
LOG2E = math.log2(math.e)
SUBLANES = 8

R_QKV = ((0, 512),)
R_GG = ((544, 800),)
R_MLA = ((800, 1216), (512, 544), 64)
R_MG = ((1216, 1728),)
R_DIFF = ((1728, 2752),)
MOD_ROWS = 8

V7X_VMEM_LIMIT_BYTES = 56 * 1024 * 1024
TOKEN_BLOCK = 1024
POST_BLOCK = 1024
ATT_QBLOCK = 512
ATT_GROUP_ROWS = 1024
GLA_GROUP_ROWS = 1024
ATT_STATIC_BLOCKS = 4


def _cparams(n_axes=1):
    return pltpu.CompilerParams(dimension_semantics=("arbitrary",) * n_axes,
                                vmem_limit_bytes=V7X_VMEM_LIMIT_BYTES)


def _rms(x, g):
    return x * lax.rsqrt(jnp.mean(x * x, axis=-1, keepdims=True) + EPS) * g


def _silu(x):
    return x * jax.nn.sigmoid(x)


def _log_sigmoid(x):
    return jnp.minimum(x, 0.0) - jnp.log1p(jnp.exp(-jnp.abs(x)))


def _rope(z, cos, sin):
    w = z.shape[-1]
    lane = lax.broadcasted_iota(jnp.int32, z.shape, 1)
    rot = jnp.where((lane & 15) < 8, -pltpu.roll(z, w - 8, 1), pltpu.roll(z, 8, 1))
    return z * cos + rot * sin


def _lane_mask(width, lo, size, dtype):
    lane = lax.broadcasted_iota(jnp.int32, (1, width), 1)
    return jnp.where((lane >= lo) & (lane < lo + size), 1.0, 0.0).astype(dtype)


def _dot(a, b):
    return jnp.dot(a, b, preferred_element_type=F32)


def _dot_nt(a, b):
    return lax.dot_general(a, b, (((1,), (1,)), ((), ())), preferred_element_type=F32)


def _dot_tn(a, b):
    return lax.dot_general(a, b, (((0,), (0,)), ((), ())), preferred_element_type=F32)


def _layer_spec(shape, layer):
    nd = len(shape)
    if nd == 2:
        return pl.BlockSpec(tuple(shape), lambda *_: (0, 0))
    return pl.BlockSpec((1,) + tuple(shape[1:]), lambda *_: (layer,) + (0,) * (nd - 1))


def _layer_row(ref, layer):
    return ref[layer:layer + 1, :]


_ANY = pl.BlockSpec(memory_space=pl.ANY)


def _mod_kernel(cctx_ref, c_ref, w_ref, b_ref, o_ref, rows_sc):
    nc = c_ref.shape[0]
    rows_sc[0:1, :] = cctx_ref[...]
    rows_sc[1:1 + nc, :] = c_ref[...]
    if 1 + nc < MOD_ROWS:
        rows_sc[1 + nc:, :] = jnp.zeros((MOD_ROWS - 1 - nc, D_MODEL), F32)
    s = _silu(rows_sc[...]).astype(BF16)
    layer = pl.program_id(0)
    bias = b_ref[0:1, :]
    for r in range(1, DEPTH):
        bias = jnp.where(layer == r, b_ref[r:r + 1, :], bias)
    mod = _dot(s, w_ref[0].astype(BF16)) + bias
    for r in range(MOD_ROWS):
        o_ref[r] = mod[r:r + 1, :]


def _mod_call(c_ctx, c, w_ada, b_ada):
    nb = 1024
    assert 1 + c.shape[0] <= MOD_ROWS
    return pl.pallas_call(
        _mod_kernel,
        grid=(DEPTH, 3 * D_MODEL // nb),
        in_specs=[pl.BlockSpec((1, D_MODEL), lambda l, j: (0, 0)),
                  pl.BlockSpec(c.shape, lambda l, j: (0, 0)),
                  pl.BlockSpec((1, D_MODEL, nb), lambda l, j: (l, 0, j)),
                  pl.BlockSpec((DEPTH, nb), lambda l, j: (0, j))],
        out_specs=pl.BlockSpec((MOD_ROWS, 1, nb), lambda l, j: (l, 0, j)),
        out_shape=jax.ShapeDtypeStruct((DEPTH * MOD_ROWS, 1, 3 * D_MODEL), F32),
        scratch_shapes=[pltpu.VMEM((MOD_ROWS, D_MODEL), F32)],
        compiler_params=_cparams(2), name="adaln_mod",
    )(c_ctx, c, w_ada, b_ada)


def _ctxkv_kernel(ckv_ref, kpe_ref, w_ref, k_ref, vt_ref):
    kv = _dot(ckv_ref[0, 0].astype(BF16), w_ref[0])
    kpe4 = jnp.concatenate([kpe_ref[0, 0]] * 4, axis=0).T
    k_ref[0, 0, :, 0:512] = kv[:, 0:512].astype(BF16)
    k_ref[0, 0, :, 512:640] = kpe4.astype(BF16)
    vt_ref[0, 0] = kv[:, 512:1024].T.astype(BF16)


def _ctxkv_call(cache_ckv, cache_kpe_t, wukv):
    nb, _, tc, _ = cache_ckv.shape
    return pl.pallas_call(
        _ctxkv_kernel,
        grid=(DEPTH, nb),
        in_specs=[pl.BlockSpec((1, 1, tc, KV_LORA), lambda l, b: (b, l, 0, 0)),
                  pl.BlockSpec((1, 1, DR_B, tc), lambda l, b: (b, l, 0, 0)),
                  pl.BlockSpec((1, KV_LORA, 1024), lambda l, b: (l, 0, 0))],
        out_specs=[pl.BlockSpec((1, 1, tc, 640), lambda l, b: (l, b, 0, 0)),
                   pl.BlockSpec((1, 1, W_B, tc), lambda l, b: (l, b, 0, 0))],
        out_shape=[jax.ShapeDtypeStruct((DEPTH, nb, tc, 640), BF16),
                   jax.ShapeDtypeStruct((DEPTH, nb, W_B, tc), BF16)],
        compiler_params=_cparams(2), name="mla_ctx_kv",
    )(cache_ckv, cache_kpe_t, wukv)


def _pre_kernel(*refs, rope, ctx_out, alias_in, bpb, seq, layer):
    it = iter(refs)
    (x_ref, mod_ref, gpre_ref, w_ref, wg_ref, bg_ref, gq_ref, gkv_ref, wuq_ref,
     wukv_ref) = (next(it) for _ in range(10))
    if rope:
        cos_ref, sin_ref = next(it), next(it)
    for _ in range(alias_in):
        next(it)
    (qk_ref, v_ref, la_ref, bs_ref, ga_ref, gb_ref, gc_ref, mq_ref, mk_ref, mvt_ref, dqk_ref,
     dvt_ref) = (next(it) for _ in range(12))
    if ctx_out:
        ckvn_ref, kpe_ref, kc_ref, vc_ref = (next(it) for _ in range(4))

    d = D_MODEL
    shift = mod_ref[0, :, 0:d]
    scale = mod_ref[0, :, d:2 * d]
    h = (_rms(x_ref[...], _layer_row(gpre_ref, layer) * (1.0 + scale)) + shift).astype(BF16)
    def proj(group):
        parts = [jnp.zeros((r, d), BF16) if isinstance(r, int) else w_ref[0, r[0]:r[1], :] for r in group]
        return _dot_nt(h, parts[0] if len(parts) == 1 else jnp.concatenate(parts, axis=0))
    if rope:
        cos = cos_ref[...]
        sin = sin_ref[...]

    pg = proj(R_QKV)
    qk_ref[:, 0:128] = pg[:, 0:128] * (DK_A ** -0.5)
    qk_ref[:, 128:256] = pg[:, 128:256]
    v_ref[...] = pg[:, 256:512]
    pm = proj(R_MLA)
    tail = pm[:, 384:512]
    xg = _dot(tail.astype(BF16), wg_ref[0]) + bg_ref[0]
    la = _log_sigmoid(xg) * (LOG2E / GLA_TAU)
    la_ref[...] = la
    bs_ref[:, 0:128] = _chunk_scan(la[:, 0:128], False)
    bs_ref[:, 128:256] = _chunk_scan(la[:, 128:256], True)
    ga_ref[...] = _silu(proj(R_GG)).astype(BF16)

    qall = _dot(_rms(pm[:, 0:256], _layer_row(gq_ref, layer)).astype(BF16), wuq_ref[0])
    q_pe = qall[:, 512:768]
    if rope:
        q_pe = _rope(q_pe, cos, sin)
    sb = (DN_B + DR_B) ** -0.5 * LOG2E
    mq_ref[:, 0:512] = (qall[:, 0:512] * sb).astype(BF16)
    mq_ref[:, 512:768] = (q_pe * sb).astype(BF16)
    ckvn = _rms(pm[:, 256:384], _layer_row(gkv_ref, layer))
    kvall = _dot(ckvn.astype(BF16), wukv_ref[0])
    lane = lax.broadcasted_iota(jnp.int32, tail.shape, 1)
    kpe4 = jnp.where(lane < DR_B, tail, 0.0)
    kpe4 = kpe4 + pltpu.roll(kpe4, DR_B, 1)
    kpe4 = kpe4 + pltpu.roll(kpe4, 2 * DR_B, 1)
    if rope:
        kpe4 = _rope(kpe4, cos[:, 0:128], sin[:, 0:128])
    mk_ref[:, 0:512] = kvall[:, 0:512].astype(BF16)
    mk_ref[:, 512:640] = kpe4.astype(BF16)
    mvt_ref[...] = kvall[:, 512:1024].T.astype(BF16)
    gb_ref[...] = _silu(proj(R_MG)).astype(BF16)

    pd = proj(R_DIFF)
    dq, dk, dv = pd[:, 0:256], pd[:, 256:512], pd[:, 512:768]
    if rope:
        dq = _rope(dq, cos, sin)
        dk = _rope(dk, cos, sin)
    dqk_ref[:, 0:256] = (dq * (DC ** -0.5 * LOG2E)).astype(BF16)
    dqk_ref[:, 256:512] = dk.astype(BF16)
    dv_t = dv.T
    dvt_ref[...] = dv_t.astype(BF16)
    gc_ref[...] = _silu(pd[:, 768:1024]).astype(BF16)
    if ctx_out:
        kpe_t = kpe4.T
        dk_t = dk.T
        for ref in () if alias_in else (ckvn_ref, kpe_ref, kc_ref, vc_ref):
            ref[:, 1:] = jnp.zeros(ref[:, 1:].shape, F32)
        for bb in range(bpb):
            rs = slice(bb * seq, (bb + 1) * seq)
            ckvn_ref[bb, 0] = ckvn[rs]
            kpe_ref[bb, 0] = kpe_t[0:DR_B, rs]
            kc_ref[bb, 0] = dk_t[:, rs].reshape(H_C, 2 * DC, seq)
            vc_ref[bb, 0] = dv_t[:, rs].reshape(H_C, 2 * DC, seq)


def _pre_call(x2d, mod, pw, layer, *, seq, rope_tabs, caches):
    n = x2d.shape[0]
    tm = min(TOKEN_BLOCK, n)
    bpb = max(tm // seq, 1)
    rope = rope_tabs is not None
    ctx_out = caches is not None
    steps_per_seq = max(seq // tm, 1)
    nbt = n // seq

    def mod_idx(i):
        return (layer * MOD_ROWS + ((i * tm) // seq + 1 if rope else 0), 0, 0)

    names = ["g_pre", "w_t", "w_gate", "b_gate", "g_mla_q", "g_mla_kv", "w_uq", "w_ukv"]
    in_specs = [pl.BlockSpec((tm, D_MODEL), lambda i: (i, 0)), pl.BlockSpec((1, 1, 3 * D_MODEL), mod_idx)]
    in_specs += [_layer_spec(pw[k].shape, layer) for k in names]
    args = [x2d, mod] + [pw[k] for k in names]
    if rope:
        in_specs += [pl.BlockSpec((tm, 256), lambda i: (i % steps_per_seq, 0))] * 2
        args += list(rope_tabs)
    outs = [(256, F32, False), (256, F32, False), (256, F32, False), (256, F32, False),
            (W_A, BF16, False), (W_B, BF16, False), (W_C, BF16, False),
            (768, BF16, False), (640, BF16, False), (W_B, BF16, True),
            (512, BF16, False), (W_C, BF16, True)]
    out_specs = [pl.BlockSpec((w, tm), lambda i: (0, i)) if tr else pl.BlockSpec((tm, w), lambda i: (i, 0))
                 for w, _, tr in outs]
    out_shape = [jax.ShapeDtypeStruct((w, n) if tr else (n, w), dt) for w, dt, tr in outs]
    widths = outs
    aliases = {}
    if ctx_out:
        assert caches or layer == 0
        nl = 1 if caches else DEPTH
        out_specs += [pl.BlockSpec((bpb, nl, seq, KV_LORA), lambda i: (i, layer, 0, 0)),
                      pl.BlockSpec((bpb, nl, DR_B, seq), lambda i: (i, layer, 0, 0)),
                      pl.BlockSpec((bpb, nl, H_C, 2 * DC, seq), lambda i: (i, layer, 0, 0, 0)),
                      pl.BlockSpec((bpb, nl, H_C, 2 * DC, seq), lambda i: (i, layer, 0, 0, 0))]
        out_shape += [jax.ShapeDtypeStruct((nbt, DEPTH, seq, KV_LORA), F32),
                      jax.ShapeDtypeStruct((nbt, DEPTH, DR_B, seq), F32),
                      jax.ShapeDtypeStruct((nbt, DEPTH, H_C, 2 * DC, seq), F32),
                      jax.ShapeDtypeStruct((nbt, DEPTH, H_C, 2 * DC, seq), F32)]
        for j, arr in enumerate(caches):
            aliases[len(args)] = len(widths) + j
            in_specs.append(_ANY)
            args.append(arr)
    return pl.pallas_call(
        functools.partial(_pre_kernel, rope=rope, ctx_out=ctx_out, alias_in=len(aliases), bpb=bpb, seq=seq,
                          layer=layer),
        grid=(n // tm,), in_specs=in_specs, out_specs=out_specs, out_shape=out_shape,
        input_output_aliases=aliases,
        compiler_params=_cparams(), name="pre_rope" if rope else "pre_ctx",
    )(*args)


_GLA_LEVELS = (1, 2, 4, 8, 16, 32)


def _gla_consts(rev):
    c = GLA_CHUNK
    row = lax.broadcasted_iota(jnp.int32, (c, 128), 0)
    pos = (c - 1 - row) if rev else row
    ri = lax.broadcasted_iota(jnp.int32, (c, H_A * c), 0)
    cj = lax.broadcasted_iota(jnp.int32, (c, H_A * c), 1) & (c - 1)
    pi = (c - 1 - ri) if rev else ri
    pj = (c - 1 - cj) if rev else cj
    x = pi ^ pj
    lvl = jnp.where(pi == pj, 0, -1)
    for kbit in range(6):
        lvl = jnp.where((pj < pi) & ((x >> kbit) == 1), kbit + 1, lvl)
    return pos, lvl


def _chunk_scan(x, rev):
    rows = x.shape[0]
    nt = rows // SUBLANES
    tiles_per_chunk = GLA_CHUNK // SUBLANES
    x3 = x.reshape(nt, SUBLANES, 128)
    sub = lax.broadcasted_iota(jnp.int32, x3.shape, 1)
    tile = lax.broadcasted_iota(jnp.int32, x3.shape, 0) & (tiles_per_chunk - 1)
    edge = 0 if rev else SUBLANES - 1
    s = 1
    while s < SUBLANES:
        if rev:
            x3 = x3 + jnp.where(sub < SUBLANES - s, pltpu.roll(x3, SUBLANES - s, 1), 0.0)
        else:
            x3 = x3 + jnp.where(sub >= s, pltpu.roll(x3, s, 1), 0.0)
        s *= 2
    s = 1
    while s < tiles_per_chunk:
        tot = jnp.broadcast_to(x3[:, edge:edge + 1, :], x3.shape)
        if rev:
            shifted = jnp.concatenate([tot[s:], tot[:s]], axis=0)
            x3 = x3 + jnp.where(tile < tiles_per_chunk - s, shifted, 0.0)
        else:
            shifted = jnp.concatenate([tot[nt - s:], tot[:nt - s]], axis=0)
            x3 = x3 + jnp.where(tile >= s, shifted, 0.0)
        s *= 2
    return x3.reshape(rows, 128)


def _gla_scores(q, k, la, b, pos, lvl, hm_bf, rev):
    c = GLA_CHUNK
    prv = pltpu.roll(la, c - 1 if rev else 1, 0)
    nxt = pltpu.roll(la, 1 if rev else c - 1, 0)
    lvl = lvl.astype(BF16)
    s_tot = jnp.where(lvl == 0, _dot_nt(q.astype(BF16), jnp.concatenate([k.astype(BF16)] * H_A, 0) * hm_bf
                                        ).astype(BF16), jnp.zeros((), BF16))
    for kbit, m in enumerate(_GLA_LEVELS):
        up = ((pos >> kbit) & 1) == 1
        if m == 1:
            e = jnp.where(up, la, 0.0)
        elif m == 2:
            c4 = pos & 3
            e = jnp.where(c4 == 0, nxt, jnp.where(c4 == 1, 0.0, jnp.where(c4 == 2, la, la + prv)))
        else:
            nblk = c // (2 * m)
            loc = m if rev else m - 1
            b3 = b.reshape(nblk, 2 * m, 128)
            ref = jnp.broadcast_to(b3[:, loc:loc + 1, :], (nblk, 2 * m, 128)).reshape(c, 128)
            dlt = b - ref
            e = jnp.where(up, dlt, -dlt)
        xm = (jnp.where(up, q, k) * jnp.exp2(e)).astype(BF16)
        sm = _dot_nt(xm, jnp.concatenate([xm] * H_A, 0) * hm_bf)
        s_tot = jnp.where(lvl == kbit + 1, sm.astype(BF16), s_tot)
    return s_tot


def _gla_apply(s_tot, q, k, v, b, st_prev, hm_f32, vm_bf, rev):
    c = GLA_CHUNK
    vbd = jnp.concatenate([v] * H_A, 0) * vm_bf
    blast = b[0:1, :] if rev else b[c - 1:c, :]
    qbar = (q * jnp.exp2(b)).astype(BF16)
    kdec = (k * jnp.exp2(blast - b)).astype(BF16)
    o = _dot(s_tot, vbd) + _dot_nt(qbar, st_prev.astype(BF16))
    st_new = st_prev * jnp.exp2(blast) + _dot_tn(v, kdec) * hm_f32
    return o, st_new


def _gla_kernel(*refs, seq, has_s0, alias_in, spb):
    it = iter(refs)
    qk_ref, v_ref, la_ref, b_ref, gate_ref, g_ref = (next(it) for _ in range(6))
    s0_ref = next(it) if has_s0 else None
    for _ in range(alias_in):
        next(it)
    oa_ref = next(it)
    sfin_ref = None if has_s0 else next(it)
    acc_sc, st_sc, blk_sc = next(it), next(it), next(it)

    c = GLA_CHUNK
    nc = seq // c
    acc_sc[...] = jnp.zeros_like(acc_sc)
    if not has_s0 and not alias_in:
        sfin_ref[:, 1:] = jnp.zeros(sfin_ref[:, 1:].shape, F32)

    hrow = lax.broadcasted_iota(jnp.int32, (H_A * c, 128), 0) // c
    hm_f32 = jnp.where(hrow == lax.broadcasted_iota(jnp.int32, (H_A * c, 128), 1) // DK_A, 1.0, 0.0)
    hm_bf = hm_f32.astype(BF16)
    vrow = lax.broadcasted_iota(jnp.int32, (H_A * c, H_A * DV_A), 0) // c
    vm_bf = jnp.where(vrow == lax.broadcasted_iota(jnp.int32, (H_A * c, H_A * DV_A), 1) // DV_A,
                      1.0, 0.0).astype(BF16)
    consts = (_gla_consts(False), _gla_consts(True))

    def run_sequence(bb):
        base = bb * seq
        for d in range(2):
            if has_s0:
                blk_sc[...] = jnp.zeros_like(blk_sc)
                for hh in range(H_A):
                    blk_sc[DK_A * hh:DK_A * (hh + 1), DV_A * hh:DV_A * (hh + 1)] = s0_ref[bb, 0, d, hh]
                st_sc[d] = blk_sc[...].T
            else:
                st_sc[d] = jnp.zeros((ST_R, ST_C), F32)

        def chunk_rows(n, d):
            cn = (nc - 1 - n) if d else n
            start = base + cn * c
            return pl.ds(start if isinstance(start, int) else pl.multiple_of(start, c), c)

        def scores(n):
            out = []
            for d in range(2):
                rows = chunk_rows(n, d)
                pos, lvl = consts[d]
                out.append(_gla_scores(qk_ref[rows, 0:128], qk_ref[rows, 128:256],
                                       la_ref[rows, 128 * d:128 * d + 128],
                                       b_ref[rows, 128 * d:128 * d + 128], pos, lvl, hm_bf, bool(d)))
            return tuple(out)

        def apply(n, s_both):
            for d in range(2):
                rows = chunk_rows(n, d)
                o, st_new = _gla_apply(s_both[d], qk_ref[rows, 0:128], qk_ref[rows, 128:256],
                                       v_ref[rows, :].astype(BF16), b_ref[rows, 128 * d:128 * d + 128],
                                       st_sc[d], hm_f32, vm_bf, bool(d))
                acc_sc[rows, :] = acc_sc[rows, :] + o
                st_sc[d] = st_new

        def body(n, s_cur):
            s_next = scores(n + 1)
            apply(n, s_cur)
            return s_next

        trips = nc - 1
        s_last = lax.fori_loop(0, trips, body, scores(0),
                               unroll=next(u for u in (5, 3, 1) if trips % u == 0))
        apply(nc - 1, s_last)

        rs = slice(base, base + seq)
        first = lax.broadcasted_iota(jnp.int32, (seq, 128), 1) < DV_A
        for p in range(H_A // 2):
            cols = slice(128 * p, 128 * (p + 1))
            oa = acc_sc[rs, cols]
            sq = oa * oa
            s0 = jnp.sum(jnp.where(first, sq, 0.0), axis=-1, keepdims=True)
            s1 = jnp.sum(jnp.where(first, 0.0, sq), axis=-1, keepdims=True)
            inv = jnp.where(first, lax.rsqrt(s0 * (1.0 / DV_A) + EPS), lax.rsqrt(s1 * (1.0 / DV_A) + EPS))
            oa_ref[rs, cols] = (oa * inv * g_ref[0, :, cols] * gate_ref[rs, cols]).astype(BF16)
        if not has_s0:
            for d in range(2):
                blk_sc[...] = st_sc[d].T
                for hh in range(H_A):
                    sfin_ref[bb, 0, d, hh] = blk_sc[DK_A * hh:DK_A * (hh + 1), DV_A * hh:DV_A * (hh + 1)]

    for bb in range(spb):
        run_sequence(bb)


def _gla_call(qk, v, la, bsum, gate, g4, layer, *, seq, state_in=None, state_out=None):
    n = qk.shape[0]
    nb = n // seq
    has_s0 = state_in is not None
    assert has_s0 or state_out is not None or layer == 0
    spb = max(1, min(nb, GLA_GROUP_ROWS // seq))
    blk = lambda w: pl.BlockSpec((spb * seq, w), lambda i: (i, 0))
    nl = DEPTH if (not has_s0 and state_out is None) else 1
    st_spec = pl.BlockSpec((spb, nl, 2, H_A, DK_A, DV_A), lambda i: (i, layer, 0, 0, 0, 0))
    in_specs = [blk(256), blk(256), blk(256), blk(256), blk(W_A), _layer_spec(g4.shape, layer)]
    args = [qk, v, la, bsum, gate, g4]
    out_specs = [blk(W_A)]
    out_shape = [jax.ShapeDtypeStruct((n, W_A), BF16)]
    aliases = {}
    if has_s0:
        in_specs.append(st_spec)
        args.append(state_in)
    else:
        out_specs.append(st_spec)
        out_shape.append(jax.ShapeDtypeStruct((nb, DEPTH, 2, H_A, DK_A, DV_A), F32))
        if state_out is not None:
            aliases[len(args)] = 1
            in_specs.append(_ANY)
            args.append(state_out)
    return pl.pallas_call(
        functools.partial(_gla_kernel, seq=seq, has_s0=has_s0, alias_in=len(aliases), spb=spb),
        grid=(nb // spb,), in_specs=in_specs, out_specs=out_specs, out_shape=out_shape,
        input_output_aliases=aliases,
        scratch_shapes=[pltpu.VMEM((spb * seq, W_A), F32), pltpu.VMEM((2, ST_R, ST_C), F32),
                        pltpu.VMEM((ST_C, ST_R), F32)],
        compiler_params=_cparams(), name="gla_state" if has_s0 else "gla_ctx",
    )(*args)


def _softmax_t_pv(st, vt, ones_rows):
    dv, keys = vt.shape
    m = _col_reduce(st, jnp.max)
    e = jnp.exp2(st - m)
    if not ones_rows:
        l = _col_reduce(e, jnp.sum)
        return _dot(vt, e.astype(BF16)) * (1.0 / l)
    o = _dot(jnp.concatenate([vt, jnp.ones((16, keys), BF16)], axis=0), e.astype(BF16))
    return o[:dv] * (1.0 / o[dv:dv + 1])


def _pipelined_attention(score_fns, value_fns, depth):
    outs = []
    pending = [fn() for fn in score_fns[:depth]]
    for j, vfn in enumerate(value_fns):
        st = pending.pop(0)
        if j + depth < len(score_fns):
            pending.append(score_fns[j + depth]())
        outs.append(_softmax_t_pv(st, *vfn()))
    return outs


def _lookahead(n_keys):
    return 2 if n_keys >= 1024 else 3


def _col_reduce(x, op):
    rows, cols = x.shape
    part = 128 if rows % 128 == 0 and rows > 128 else rows
    if part != rows:
        x = op(x.reshape(rows // part, part, cols), axis=0)
    return op(x, axis=0, keepdims=True)


def _seqs_per_step(n, seq, ctx_len):
    if ctx_len or seq >= ATT_QBLOCK:
        return 1
    return max(1, min(n // seq, ATT_GROUP_ROWS // seq))


def _for_query_blocks(run, seq, spb):
    qb = min(ATT_QBLOCK, seq)
    nq = seq // qb
    if spb * nq <= ATT_STATIC_BLOCKS:
        run([(bb, slice(bb * seq + j * qb, bb * seq + (j + 1) * qb)) for bb in range(spb) for j in range(nq)])
    else:
        assert spb == 1

        def body(i, carry):
            run([(0, pl.ds(pl.multiple_of(i * qb, qb), qb))])
            return carry
        lax.fori_loop(0, nq, body, 0)


def _mla_part(q_ref, k_ref, vt_ref, gate_ref, ck_ref, cvt_ref, ob_ref, kk_sc, vt_sc, *, seq, ctx_len, spb):
    for bb in range(spb):
        rs = slice(bb * seq, (bb + 1) * seq)
        for p in range(H_B // 2):
            kk_sc[bb, p, 0:seq, 0:128] = k_ref[rs, 128 * p:128 * p + 128]
            kk_sc[bb, p, 0:seq, 128:256] = k_ref[rs, 512:640]
            if ctx_len:
                kk_sc[bb, p, seq:seq + ctx_len, 0:128] = ck_ref[0, 0, :, 128 * p:128 * p + 128]
                kk_sc[bb, p, seq:seq + ctx_len, 128:256] = ck_ref[0, 0, :, 512:640]
        vt_sc[bb, :, 0:seq] = vt_ref[:, rs]
        if ctx_len:
            vt_sc[bb, :, seq:seq + ctx_len] = cvt_ref[0, 0]

    def scores(bb, rows, h):
        p, hh = divmod(h, 2)
        qn = q_ref[rows, 128 * p:128 * p + 128] * _lane_mask(128, DN_B * hh, DN_B, BF16)
        qp = (q_ref[rows, 512 + 128 * (h // 4):512 + 128 * (h // 4) + 128]
              * _lane_mask(128, DR_B * (h % 4), DR_B, BF16))
        return _dot_nt(kk_sc[bb, p], jnp.concatenate([qn, qp], axis=-1))

    def items(blocks):
        its = [(bb, rows, h) for bb, rows in blocks for h in range(H_B)]

        def finish(outs):
            for j, (_, rows) in enumerate(blocks):
                ob = jnp.concatenate(outs[H_B * j:H_B * (j + 1)], axis=0).T
                ob_ref[rows, :] = (ob * gate_ref[rows, :]).astype(BF16)

        return ([functools.partial(scores, bb, rows, h) for bb, rows, h in its],
                [functools.partial(lambda bb, h: (vt_sc[bb, DV_B * h:DV_B * (h + 1), :], not ctx_len), bb, h)
                 for bb, _, h in its],
                finish)

    return items


def _diff_part(qk_ref, vt_ref, gate_ref, lam_ref, g_ref, ck_ref, cv_ref, oc_ref, k_sc, vt_sc, *,
               seq, ctx_len, lam_init, spb):
    lam = (jnp.exp(jnp.sum(lam_ref[0, 0:1, :] * lam_ref[0, 1:2, :], axis=-1, keepdims=True))
           - jnp.exp(jnp.sum(lam_ref[0, 2:3, :] * lam_ref[0, 3:4, :], axis=-1, keepdims=True)) + lam_init)
    dh = 2 * DC
    for bb in range(spb):
        rs = slice(bb * seq, (bb + 1) * seq)
        for p in range(H_C // 2):
            k_sc[bb, p, 0:seq, :] = qk_ref[rs, 256 + 128 * p:256 + 128 * p + 128]
            if ctx_len:
                pair_t = jnp.concatenate([ck_ref[0, 0, 2 * p], ck_ref[0, 0, 2 * p + 1]], axis=0)
                k_sc[bb, p, seq:seq + ctx_len, :] = pair_t.T.astype(BF16)
        vt_sc[bb, :, 0:seq] = vt_ref[:, rs]
        if ctx_len:
            for h in range(H_C):
                vt_sc[bb, dh * h:dh * (h + 1), seq:seq + ctx_len] = cv_ref[0, 0, h].astype(BF16)

    def scores(bb, rows, h, comp):
        p, hh = divmod(h, 2)
        qm = qk_ref[rows, 128 * p:128 * p + 128] * _lane_mask(128, dh * hh + DC * comp, DC, BF16)
        return _dot_nt(k_sc[bb, p], qm)

    def items(blocks):
        its = [(bb, rows, h, comp) for bb, rows in blocks for h in range(H_C) for comp in range(2)]

        def finish(o12):
            for j, (_, rows) in enumerate(blocks):
                outs = []
                for h in range(H_C):
                    o1, o2 = o12[2 * (H_C * j + h)], o12[2 * (H_C * j + h) + 1]
                    ot = o1 - lam * o2
                    outs.append(ot * lax.rsqrt(jnp.mean(ot * ot, axis=0, keepdims=True) + EPS))
                oc = jnp.concatenate(outs, axis=0).T
                oc_ref[rows, :] = (oc * g_ref[0] * (1.0 - lam_init) * gate_ref[rows, :]).astype(BF16)

        return ([functools.partial(scores, *item) for item in its],
                [functools.partial(lambda bb, h: (vt_sc[bb, dh * h:dh * (h + 1), :], True), bb, h)
                 for bb, _, h, _ in its],
                finish)

    return items


def _attn_kernel(*refs, seq, ctx_len, lam_init, spb, mla, diff):
    it = iter(refs)
    n_ctx = 2 if ctx_len else 0
    mla_in = [next(it) for _ in range(4 + n_ctx)] + [None] * (2 - n_ctx) if mla else None
    diff_in = [next(it) for _ in range(5 + n_ctx)] + [None] * (2 - n_ctx) if diff else None
    ob_ref = next(it) if mla else None
    oc_ref = next(it) if diff else None
    parts = []
    if mla:
        parts.append(_mla_part(*mla_in, ob_ref, next(it), next(it), seq=seq, ctx_len=ctx_len, spb=spb))
    if diff:
        parts.append(_diff_part(*diff_in, oc_ref, next(it), next(it), seq=seq, ctx_len=ctx_len,
                                lam_init=lam_init, spb=spb))

    def run(blocks):
        built = [part(blocks) for part in parts]
        outs = _pipelined_attention([f for b in built for f in b[0]], [f for b in built for f in b[1]],
                                    _lookahead(seq + ctx_len))
        lo = 0
        for score_fns, _, finish in built:
            finish(outs[lo:lo + len(score_fns)])
            lo += len(score_fns)

    _for_query_blocks(run, seq, spb)


def _attn_call(mq, mk, mvt, gate_b, dqk, dvt, gate_c, lamp, g4, ctx, layer, *, seq):
    n = mq.shape[0]
    ctx_len = 0 if ctx is None else ctx["mla_k"].shape[2]
    lam_init = 0.8 - 0.6 * math.exp(-0.3 * layer)
    spb = _seqs_per_step(n, seq, ctx_len)
    rows = spb * seq
    tk = seq + ctx_len
    rblk = lambda w: pl.BlockSpec((rows, w), lambda i: (i, 0))
    tblk = lambda w: pl.BlockSpec((w, rows), lambda i: (0, i))
    mla_specs, mla_args = [rblk(768), rblk(640), tblk(W_B), rblk(W_B)], [mq, mk, mvt, gate_b]
    diff_specs = [rblk(512), tblk(W_C), rblk(W_C), _layer_spec(lamp.shape, layer), _layer_spec(g4.shape, layer)]
    diff_args = [dqk, dvt, gate_c, lamp, g4]
    if ctx_len:
        mla_specs += [pl.BlockSpec((1, 1, ctx_len, 640), lambda i: (layer, i, 0, 0)),
                      pl.BlockSpec((1, 1, W_B, ctx_len), lambda i: (layer, i, 0, 0))]
        mla_args += [ctx["mla_k"], ctx["mla_vt"]]
        diff_specs += [pl.BlockSpec((1, 1, H_C, 2 * DC, ctx_len), lambda i: (i, layer, 0, 0, 0))] * 2
        diff_args += [ctx["diff_k_t"], ctx["diff_v_t"]]
    mla_out = (rblk(W_B), jax.ShapeDtypeStruct((n, W_B), BF16))
    diff_out = (rblk(W_C), jax.ShapeDtypeStruct((n, W_C), BF16))
    mla_scratch = [pltpu.VMEM((spb, H_B // 2, tk, 256), BF16), pltpu.VMEM((spb, W_B, tk), BF16)]
    diff_scratch = [pltpu.VMEM((spb, H_C // 2, tk, 128), BF16), pltpu.VMEM((spb, W_C, tk), BF16)]

    def call(mla, diff, name):
        outs = ([mla_out] if mla else []) + ([diff_out] if diff else [])
        return pl.pallas_call(
            functools.partial(_attn_kernel, seq=seq, ctx_len=ctx_len, lam_init=lam_init, spb=spb,
                              mla=mla, diff=diff),
            grid=(n // rows,),
            in_specs=(mla_specs if mla else []) + (diff_specs if diff else []),
            out_specs=[o[0] for o in outs], out_shape=[o[1] for o in outs],
            scratch_shapes=(mla_scratch if mla else []) + (diff_scratch if diff else []),
            compiler_params=_cparams(), name=name,
        )(*(mla_args if mla else []), *(diff_args if diff else []))

    if ctx_len:
        (ob,) = call(True, False, "mla_ctx")
        (oc,) = call(False, True, "diff_ctx")
        return ob, oc
    return call(True, True, "attn_self")


def _post_kernel(oa_ref, ob_ref, oc_ref, x_ref, mod_ref, w_ref, g_ref, y_ref, *, layer):
    mix = jnp.concatenate([oa_ref[...], ob_ref[...], oc_ref[...]], axis=-1)
    y_ref[...] = x_ref[...] + _rms(_dot(mix, w_ref[0]),
                                   _layer_row(g_ref, layer) * mod_ref[0, :, 2 * D_MODEL:3 * D_MODEL])


def _post_call(oa, ob, oc, x2d, mod, pw, layer, *, seq, sample):
    n = x2d.shape[0]
    tm = min(POST_BLOCK, seq if sample else n)
    blk = lambda w: pl.BlockSpec((tm, w), lambda i: (i, 0))

    def mod_idx(i):
        return (layer * MOD_ROWS + ((i * tm) // seq + 1 if sample else 0), 0, 0)

    return pl.pallas_call(
        functools.partial(_post_kernel, layer=layer),
        grid=(n // tm,),
        in_specs=[blk(W_A), blk(W_B), blk(W_C), blk(D_MODEL),
                  pl.BlockSpec((1, 1, 3 * D_MODEL), mod_idx),
                  _layer_spec(pw["w_out"].shape, layer), _layer_spec(pw["g_post"].shape, layer)],
        out_specs=blk(D_MODEL),
        out_shape=jax.ShapeDtypeStruct((n, D_MODEL), F32),
        compiler_params=_cparams(), name="post",
    )(oa, ob, oc, x2d, mod, pw["w_out"], pw["g_post"])


def _pack_params(g_pre, g_post, w_in, w_gla_af, b_gla_af, w_gla_ab, b_gla_ab, g_gla, g_mla_q, w_mla_uq,
                 g_mla_kv, w_mla_ukv, lam_q1, lam_k1, lam_q2, lam_k2, g_diff, w_out):
    w_t = jnp.swapaxes(w_in, 1, 2)
    assert w_t.shape[1] == R_DIFF[0][1]
    zg = jnp.zeros((DEPTH, GLA_LR, 128), F32)
    w_gate = jnp.concatenate([jnp.zeros((DEPTH, DR_B, 256), F32),
                              jnp.concatenate([w_gla_af, zg], axis=-1),
                              jnp.concatenate([zg, w_gla_ab], axis=-1),
                              jnp.zeros((DEPTH, 128 - DR_B - 2 * GLA_LR, 256), F32)], axis=1).astype(BF16)
    uq = w_mla_uq.reshape(DEPTH, Q_LORA, H_B, DN_B + DR_B)
    w_pe = uq[..., DN_B:].reshape(DEPTH, Q_LORA, H_B * DR_B)
    w_uq = jnp.concatenate([uq[..., :DN_B].reshape(DEPTH, Q_LORA, H_B * DN_B), w_pe], axis=-1).astype(BF16)
    ukv = w_mla_ukv.reshape(DEPTH, KV_LORA, H_B, DN_B + DV_B)
    w_ukv = jnp.concatenate([ukv[..., :DN_B].reshape(DEPTH, KV_LORA, H_B * DN_B),
                             ukv[..., DN_B:].reshape(DEPTH, KV_LORA, H_B * DV_B)], axis=-1).astype(BF16)
    row = lambda a: a.reshape(DEPTH, 1, a.shape[-1])
    return dict(
        w_t=w_t.astype(BF16), w_gate=w_gate,
        b_gate=row(jnp.concatenate([b_gla_af, b_gla_ab], axis=-1)),
        w_uq=w_uq, w_ukv=w_ukv, w_out=w_out.astype(BF16),
        g_pre=g_pre, g_post=g_post, g_mla_q=g_mla_q, g_mla_kv=g_mla_kv,
        g_gla4=row(jnp.tile(g_gla, (1, H_A))), g_diff4=row(jnp.tile(g_diff, (1, H_C))),
        lam=jnp.stack([lam_q1, lam_k1, lam_q2, lam_k2], axis=1))


def _rope_tables(n):
    t = np.arange(n)
    row = (t // GRID_W).astype(np.float32)
    col = (t % GRID_W).astype(np.float32)
    half = ROPE_DIM // 2
    inv = (1.0 / (np.float32(ROPE_THETA) ** (np.arange(0, half, 2, dtype=np.float32) / np.float32(half)))
           ).astype(np.float32)
    ar = row[:, None] * inv
    ac = col[:, None] * inv
    ang = np.concatenate([ar, ar, ac, ac], axis=-1).astype(np.float32)
    return (jnp.asarray(np.tile(np.cos(ang), (1, 8)).astype(np.float32)),
            jnp.asarray(np.tile(np.sin(ang), (1, 8)).astype(np.float32)))


def _sublayer(x2d, mod, pw, layer, *, seq, rope_tabs, ctx, caches):
    sample = ctx is not None
    pre = _pre_call(x2d, mod, pw, layer, seq=seq, rope_tabs=rope_tabs,
                    caches=None if sample else caches[:4])
    qk, v, la, bsum, ga, gb, gc, mq, mk, mvt, dqk, dvt = pre[:12]
    if sample:
        (oa,) = _gla_call(qk, v, la, bsum, ga, pw["g_gla4"], layer, seq=seq, state_in=ctx["state"])
        new_caches = None
    else:
        oa, sfin = _gla_call(qk, v, la, bsum, ga, pw["g_gla4"], layer, seq=seq,
                             state_out=caches[4] if caches else None)
        new_caches = tuple(pre[12:]) + (sfin,)
    ob, oc = _attn_call(mq, mk, mvt, gb, dqk, dvt, gc, pw["lam"], pw["g_diff4"], ctx, layer, seq=seq)
    y = _post_call(oa, ob, oc, x2d, mod, pw, layer, seq=seq, sample=sample)
    return y, new_caches


def kernel(x_prompt, x_sample, c, cache_mla_ckv, cache_mla_kpe, cache_diff_k, cache_diff_v, state_gla,
           c_ctx, w_ada, b_ada, g_pre, g_post, w_in, w_gla_af, b_gla_af, w_gla_ab, b_gla_ab, g_gla,
           g_mla_q, w_mla_uq, g_mla_kv, w_mla_ukv, lam_q1, lam_k1, lam_q2, lam_k2, g_diff, w_out):
    bp, tp, d = x_prompt.shape
    bs, ts, _ = x_sample.shape

    pw = _pack_params(g_pre, g_post, w_in, w_gla_af, b_gla_af, w_gla_ab, b_gla_ab, g_gla, g_mla_q,
                      w_mla_uq, g_mla_kv, w_mla_ukv, lam_q1, lam_k1, lam_q2, lam_k2, g_diff, w_out)
    mod = _mod_call(c_ctx.reshape(1, d), c, w_ada, b_ada)
    rope_tabs = _rope_tables(ts)
    ctx_k, ctx_vt = _ctxkv_call(cache_mla_ckv, jnp.swapaxes(cache_mla_kpe, -1, -2), pw["w_ukv"])
    ctx = dict(state=state_gla, mla_k=ctx_k, mla_vt=ctx_vt,
               diff_k_t=jnp.swapaxes(cache_diff_k, -1, -2), diff_v_t=jnp.swapaxes(cache_diff_v, -1, -2))

    y_p = x_prompt.reshape(bp * tp, d)
    y_s = x_sample.reshape(bs * ts, d)
    caches = ()
    for l in range(DEPTH):
        y_p, caches = _sublayer(y_p, mod, pw, l, seq=tp, rope_tabs=None, ctx=None, caches=caches)
        y_s, _ = _sublayer(y_s, mod, pw, l, seq=ts, rope_tabs=rope_tabs, ctx=ctx, caches=None)
    ckvn, kpe_t, kc_t, vc_t, new_state = caches
    return (y_p.reshape(bp, tp, d), y_s.reshape(bs, ts, d), ckvn, jnp.swapaxes(kpe_t, -1, -2),
            jnp.swapaxes(kc_t, -1, -2), jnp.swapaxes(vc_t, -1, -2), new_state)
```

```python
import functools
import math

import numpy as np
import jax
import jax.numpy as jnp
from jax import lax
from jax.experimental import pallas as pl
from jax.experimental.pallas import tpu as pltpu

F32 = jnp.float32
BF16 = jnp.bfloat16

D_MODEL = 1024
DEPTH = 2
GRID_W = 64
EPS = 1e-6
ROPE_THETA = 10000.0
ROPE_DIM = 32
H_A, DK_A, DV_A = 4, 32, 64
GLA_LR = 16
GLA_TAU = 16.0
GLA_CHUNK = 64
H_B, DN_B, DR_B, DV_B = 8, 64, 32, 64
Q_LORA, KV_LORA = 256, 128
H_C, DC = 4, 32
W_A, W_B, W_C = H_A * DV_A, H_B * DV_B, H_C * 2 * DC
ST_R, ST_C = H_A * DV_A, H_A * DK_A
LOG2E = math.log2(math.e)
SUBLANES = 8

R_QKV = ((0, 512),)
R_GG = ((544, 800),)
R_MLA = ((800, 1216), (512, 544), 64)
R_MG = ((1216, 1728),)
R_DIFF = ((1728, 2752),)
MOD_ROWS = 8

V7X_VMEM_LIMIT_BYTES = 56 * 1024 * 1024
TOKEN_BLOCK = 1024
POST_BLOCK = 1024
ATT_QBLOCK = 512
ATT_GROUP_ROWS = 1024
GLA_GROUP_ROWS = 1024
ATT_STATIC_BLOCKS = 4


def _cparams(n_axes=1):
    return pltpu.CompilerParams(dimension_semantics=("arbitrary",) * n_axes,
                                vmem_limit_bytes=V7X_VMEM_LIMIT_BYTES)


def _rms(x, g):
    return x * lax.rsqrt(jnp.mean(x * x, axis=-1, keepdims=True) + EPS) * g


def _silu(x):
    return x * jax.nn.sigmoid(x)


def _log_sigmoid(x):
    return jnp.minimum(x, 0.0) - jnp.log1p(jnp.exp(-jnp.abs(x)))


def _rope(z, cos, sin):
    w = z.shape[-1]
    lane = lax.broadcasted_iota(jnp.int32, z.shape, 1)
    rot = jnp.where((lane & 15) < 8, -pltpu.roll(z, w - 8, 1), pltpu.roll(z, 8, 1))
    return z * cos + rot * sin


def _lane_mask(width, lo, size, dtype):
    lane = lax.broadcasted_iota(jnp.int32, (1, width), 1)
    return jnp.where((lane >= lo) & (lane < lo + size), 1.0, 0.0).astype(dtype)


def _dot(a, b):
    return jnp.dot(a, b, preferred_element_type=F32)


def _dot_nt(a, b):
    return lax.dot_general(a, b, (((1,), (1,)), ((), ())), preferred_element_type=F32)


def _dot_tn(a, b):
    return lax.dot_general(a, b, (((0,), (0,)), ((), ())), preferred_element_type=F32)


def _layer_spec(shape, layer):
    nd = len(shape)
    if nd == 2:
        return pl.BlockSpec(tuple(shape), lambda *_: (0, 0))
    return pl.BlockSpec((1,) + tuple(shape[1:]), lambda *_: (layer,) + (0,) * (nd - 1))


def _layer_row(ref, layer):
    return ref[layer:layer + 1, :]


_ANY = pl.BlockSpec(memory_space=pl.ANY)


def _mod_kernel(cctx_ref, c_ref, w_ref, b_ref, o_ref, rows_sc):
    nc = c_ref.shape[0]
    rows_sc[0:1, :] = cctx_ref[...]
    rows_sc[1:1 + nc, :] = c_ref[...]
    if 1 + nc < MOD_ROWS:
        rows_sc[1 + nc:, :] = jnp.zeros((MOD_ROWS - 1 - nc, D_MODEL), F32)
    s = _silu(rows_sc[...]).astype(BF16)
    layer = pl.program_id(0)
    bias = b_ref[0:1, :]
    for r in range(1, DEPTH):
        bias = jnp.where(layer == r, b_ref[r:r + 1, :], bias)
    mod = _dot(s, w_ref[0].astype(BF16)) + bias
    for r in range(MOD_ROWS):
        o_ref[r] = mod[r:r + 1, :]


def _mod_call(c_ctx, c, w_ada, b_ada):
    nb = 1024
    assert 1 + c.shape[0] <= MOD_ROWS
    return pl.pallas_call(
        _mod_kernel,
        grid=(DEPTH, 3 * D_MODEL // nb),
        in_specs=[pl.BlockSpec((1, D_MODEL), lambda l, j: (0, 0)),
                  pl.BlockSpec(c.shape, lambda l, j: (0, 0)),
                  pl.BlockSpec((1, D_MODEL, nb), lambda l, j: (l, 0, j)),
                  pl.BlockSpec((DEPTH, nb), lambda l, j: (0, j))],
        out_specs=pl.BlockSpec((MOD_ROWS, 1, nb), lambda l, j: (l, 0, j)),
        out_shape=jax.ShapeDtypeStruct((DEPTH * MOD_ROWS, 1, 3 * D_MODEL), F32),
        scratch_shapes=[pltpu.VMEM((MOD_ROWS, D_MODEL), F32)],
        compiler_params=_cparams(2), name="adaln_mod",
    )(c_ctx, c, w_ada, b_ada)


def _ctxkv_kernel(ckv_ref, kpe_ref, w_ref, k_ref, vt_ref):
    for b in range(ckv_ref.shape[0]):
        kv = _dot(ckv_ref[b, 0].astype(BF16), w_ref[0])
        kpe4 = jnp.concatenate([kpe_ref[b, 0]] * 4, axis=0).T
        k_ref[0, b, :, 0:512] = kv[:, 0:512].astype(BF16)
        k_ref[0, b, :, 512:640] = kpe4.astype(BF16)
        vt_ref[0, b] = kv[:, 512:1024].T.astype(BF16)


def _ctxkv_call(cache_ckv, cache_kpe_t, wukv):
    nb, _, tc, _ = cache_ckv.shape
    return pl.pallas_call(
        _ctxkv_kernel,
        grid=(DEPTH,),
        in_specs=[pl.BlockSpec((nb, 1, tc, KV_LORA), lambda l: (0, l, 0, 0)),
                  pl.BlockSpec((nb, 1, DR_B, tc), lambda l: (0, l, 0, 0)),
                  pl.BlockSpec((1, KV_LORA, 1024), lambda l: (l, 0, 0))],
        out_specs=[pl.BlockSpec((1, nb, tc, 640), lambda l: (l, 0, 0, 0)),
                   pl.BlockSpec((1, nb, W_B, tc), lambda l: (l, 0, 0, 0))],
        out_shape=[jax.ShapeDtypeStruct((DEPTH, nb, tc, 640), BF16),
                   jax.ShapeDtypeStruct((DEPTH, nb, W_B, tc), BF16)],
        compiler_params=_cparams(), name="mla_ctx_kv",
    )(cache_ckv, cache_kpe_t, wukv)


def _pre_kernel(*refs, rope, ctx_out, alias_in, bpb, seq, layer):
    it = iter(refs)
    (x_ref, mod_ref, gpre_ref, w_ref, wg_ref, bg_ref, gq_ref, gkv_ref, wuq_ref,
     wukv_ref) = (next(it) for _ in range(10))
    if rope:
        cos_ref, sin_ref = next(it), next(it)
    for _ in range(alias_in):
        next(it)
    (qk_ref, v_ref, la_ref, bs_ref, ga_ref, gb_ref, gc_ref, mq_ref, mk_ref, mvt_ref, dqk_ref,
     dvt_ref) = (next(it) for _ in range(12))
    if ctx_out:
        ckvn_ref, kpe_ref, kc_ref, vc_ref = (next(it) for _ in range(4))

    d = D_MODEL
    shift = mod_ref[0, :, 0:d]
    scale = mod_ref[0, :, d:2 * d]
    h = (_rms(x_ref[...], _layer_row(gpre_ref, layer) * (1.0 + scale)) + shift).astype(BF16)
    def proj(group):
        parts = [jnp.zeros((r, d), BF16) if isinstance(r, int) else w_ref[0, r[0]:r[1], :] for r in group]
        return _dot_nt(h, parts[0] if len(parts) == 1 else jnp.concatenate(parts, axis=0))
    if rope:
        cos = cos_ref[...]
        sin = sin_ref[...]

    pg = proj(R_QKV)
    qk_ref[:, 0:128] = pg[:, 0:128] * (DK_A ** -0.5)
    qk_ref[:, 128:256] = pg[:, 128:256]
    v_ref[...] = pg[:, 256:512]
    pm = proj(R_MLA)
    tail = pm[:, 384:512]
    xg = _dot(tail.astype(BF16), wg_ref[0]) + bg_ref[0]
    la = _log_sigmoid(xg) * (1.0 / GLA_TAU)
    la_ref[...] = la
    bs_ref[:, 0:128] = _chunk_scan(la[:, 0:128], False)
    bs_ref[:, 128:256] = _chunk_scan(la[:, 128:256], True)
    ga_ref[...] = _silu(proj(R_GG)).astype(BF16)

    qall = _dot(_rms(pm[:, 0:256], _layer_row(gq_ref, layer)).astype(BF16), wuq_ref[0])
    q_pe = qall[:, 512:768]
    if rope:
        q_pe = _rope(q_pe, cos, sin)
    sb = (DN_B + DR_B) ** -0.5 * LOG2E
    mq_ref[:, 0:512] = (qall[:, 0:512] * sb).astype(BF16)
    mq_ref[:, 512:768] = (q_pe * sb).astype(BF16)
    ckvn = _rms(pm[:, 256:384], _layer_row(gkv_ref, layer))
    kvall = _dot(ckvn.astype(BF16), wukv_ref[0])
    lane = lax.broadcasted_iota(jnp.int32, tail.shape, 1)
    kpe4 = jnp.where(lane < DR_B, tail, 0.0)
    kpe4 = kpe4 + pltpu.roll(kpe4, DR_B, 1)
    kpe4 = kpe4 + pltpu.roll(kpe4, 2 * DR_B, 1)
    if rope:
        kpe4 = _rope(kpe4, cos[:, 0:128], sin[:, 0:128])
    mk_ref[:, 0:512] = kvall[:, 0:512].astype(BF16)
    mk_ref[:, 512:640] = kpe4.astype(BF16)
    mvt_ref[...] = kvall[:, 512:1024].T.astype(BF16)
    gb_ref[...] = _silu(proj(R_MG)).astype(BF16)

    pd = proj(R_DIFF)
    dq, dk, dv = pd[:, 0:256], pd[:, 256:512], pd[:, 512:768]
    if rope:
        dq = _rope(dq, cos, sin)
        dk = _rope(dk, cos, sin)
    dqk_ref[:, 0:256] = (dq * (DC ** -0.5 * LOG2E)).astype(BF16)
    dqk_ref[:, 256:512] = dk.astype(BF16)
    dv_t = dv.T
    dvt_ref[...] = dv_t.astype(BF16)
    gc_ref[...] = _silu(pd[:, 768:1024]).astype(BF16)
    if ctx_out:
        kpe_t = kpe4.T
        dk_t = dk.T
        for ref in () if alias_in else (ckvn_ref, kpe_ref, kc_ref, vc_ref):
            ref[:, 1:] = jnp.zeros(ref[:, 1:].shape, F32)
        for bb in range(bpb):
            rs = slice(bb * seq, (bb + 1) * seq)
            ckvn_ref[bb, 0] = ckvn[rs]
            kpe_ref[bb, 0] = kpe_t[0:DR_B, rs]
            kc_ref[bb, 0] = dk_t[:, rs].reshape(H_C, 2 * DC, seq)
            vc_ref[bb, 0] = dv_t[:, rs].reshape(H_C, 2 * DC, seq)


def _pre_call(x2d, mod, pw, layer, *, seq, rope_tabs, caches):
    n = x2d.shape[0]
    tm = min(TOKEN_BLOCK, n)
    bpb = max(tm // seq, 1)
    rope = rope_tabs is not None
    ctx_out = caches is not None
    steps_per_seq = max(seq // tm, 1)
    nbt = n // seq

    def mod_idx(i):
        return (layer * MOD_ROWS + ((i * tm) // seq + 1 if rope else 0), 0, 0)

    names = ["g_pre", "w_t", "w_gate", "b_gate", "g_mla_q", "g_mla_kv", "w_uq", "w_ukv"]
    in_specs = [pl.BlockSpec((tm, D_MODEL), lambda i: (i, 0)), pl.BlockSpec((1, 1, 3 * D_MODEL), mod_idx)]
    in_specs += [_layer_spec(pw[k].shape, layer) for k in names]
    args = [x2d, mod] + [pw[k] for k in names]
    if rope:
        in_specs += [pl.BlockSpec((tm, 256), lambda i: (i % steps_per_seq, 0))] * 2
        args += list(rope_tabs)
    outs = [(256, F32, False), (256, F32, False), (256, F32, False), (256, F32, False),
            (W_A, BF16, False), (W_B, BF16, False), (W_C, BF16, False),
            (768, BF16, False), (640, BF16, False), (W_B, BF16, True),
            (512, BF16, False), (W_C, BF16, True)]
    out_specs = [pl.BlockSpec((w, tm), lambda i: (0, i)) if tr else pl.BlockSpec((tm, w), lambda i: (i, 0))
                 for w, _, tr in outs]
    out_shape = [jax.ShapeDtypeStruct((w, n) if tr else (n, w), dt) for w, dt, tr in outs]
    widths = outs
    aliases = {}
    if ctx_out:
        assert caches or layer == 0
        nl = 1 if caches else DEPTH
        out_specs += [pl.BlockSpec((bpb, nl, seq, KV_LORA), lambda i: (i, layer, 0, 0)),
                      pl.BlockSpec((bpb, nl, DR_B, seq), lambda i: (i, layer, 0, 0)),
                      pl.BlockSpec((bpb, nl, H_C, 2 * DC, seq), lambda i: (i, layer, 0, 0, 0)),
                      pl.BlockSpec((bpb, nl, H_C, 2 * DC, seq), lambda i: (i, layer, 0, 0, 0))]
        out_shape += [jax.ShapeDtypeStruct((nbt, DEPTH, seq, KV_LORA), F32),
                      jax.ShapeDtypeStruct((nbt, DEPTH, DR_B, seq), F32),
                      jax.ShapeDtypeStruct((nbt, DEPTH, H_C, 2 * DC, seq), F32),
                      jax.ShapeDtypeStruct((nbt, DEPTH, H_C, 2 * DC, seq), F32)]
        for j, arr in enumerate(caches):
            aliases[len(args)] = len(widths) + j
            in_specs.append(_ANY)
            args.append(arr)
    return pl.pallas_call(
        functools.partial(_pre_kernel, rope=rope, ctx_out=ctx_out, alias_in=len(aliases), bpb=bpb, seq=seq,
                          layer=layer),
        grid=(n // tm,), in_specs=in_specs, out_specs=out_specs, out_shape=out_shape,
        input_output_aliases=aliases,
        compiler_params=_cparams(), name="pre_rope" if rope else "pre_ctx",
    )(*args)


_GLA_LEVELS = (1, 2, 4, 8, 16, 32)


def _gla_consts(rev):
    c = GLA_CHUNK
    row = lax.broadcasted_iota(jnp.int32, (c, 128), 0)
    pos = (c - 1 - row) if rev else row
    ri = lax.broadcasted_iota(jnp.int32, (c, H_A * c), 0)
    cj = lax.broadcasted_iota(jnp.int32, (c, H_A * c), 1) & (c - 1)
    pi = (c - 1 - ri) if rev else ri
    pj = (c - 1 - cj) if rev else cj
    x = pi ^ pj
    lvl = jnp.where(pi == pj, 0, -1)
    for kbit in range(6):
        lvl = jnp.where((pj < pi) & ((x >> kbit) == 1), kbit + 1, lvl)
    return pos, lvl


def _chunk_scan(x, rev):
    rows = x.shape[0]
    nt = rows // SUBLANES
    tiles_per_chunk = GLA_CHUNK // SUBLANES
    x3 = x.reshape(nt, SUBLANES, 128)
    sub = lax.broadcasted_iota(jnp.int32, x3.shape, 1)
    tile = lax.broadcasted_iota(jnp.int32, x3.shape, 0) & (tiles_per_chunk - 1)
    edge = 0 if rev else SUBLANES - 1
    s = 1
    while s < SUBLANES:
        if rev:
            x3 = x3 + jnp.where(sub < SUBLANES - s, pltpu.roll(x3, SUBLANES - s, 1), 0.0)
        else:
            x3 = x3 + jnp.where(sub >= s, pltpu.roll(x3, s, 1), 0.0)
        s *= 2
    s = 1
    while s < tiles_per_chunk:
        tot = jnp.broadcast_to(x3[:, edge:edge + 1, :], x3.shape)
        if rev:
            shifted = jnp.concatenate([tot[s:], tot[:s]], axis=0)
            x3 = x3 + jnp.where(tile < tiles_per_chunk - s, shifted, 0.0)
        else:
            shifted = jnp.concatenate([tot[nt - s:], tot[:nt - s]], axis=0)
            x3 = x3 + jnp.where(tile >= s, shifted, 0.0)
        s *= 2
    return x3.reshape(rows, 128)


def _gla_scores(q, k, la, b, pos, lvl, hm_bf, rev):
    c = GLA_CHUNK
    prv = pltpu.roll(la, c - 1 if rev else 1, 0)
    nxt = pltpu.roll(la, 1 if rev else c - 1, 0)
    lvl = lvl.astype(BF16)
    s_tot = jnp.where(lvl == 0, _dot_nt(q.astype(BF16), jnp.concatenate([k.astype(BF16)] * H_A, 0) * hm_bf
                                        ).astype(BF16), jnp.zeros((), BF16))
    for kbit, m in enumerate(_GLA_LEVELS):
        up = ((pos >> kbit) & 1) == 1
        if m == 1:
            e = jnp.where(up, la, 0.0)
        elif m == 2:
            c4 = pos & 3
            e = jnp.where(c4 == 0, nxt, jnp.where(c4 == 1, 0.0, jnp.where(c4 == 2, la, la + prv)))
        else:
            nblk = c // (2 * m)
            loc = m if rev else m - 1
            b3 = b.reshape(nblk, 2 * m, 128)
            ref = jnp.broadcast_to(b3[:, loc:loc + 1, :], (nblk, 2 * m, 128)).reshape(c, 128)
            dlt = b - ref
            e = jnp.where(up, dlt, -dlt)
        xm = (jnp.where(up, q, k) * jnp.exp(e)).astype(BF16)
        sm = _dot_nt(xm, jnp.concatenate([xm] * H_A, 0) * hm_bf)
        s_tot = jnp.where(lvl == kbit + 1, sm.astype(BF16), s_tot)
    return s_tot


def _gla_apply(s_tot, q, k, v, b, st_prev, hm_f32, vm_bf, rev):
    c = GLA_CHUNK
    vbd = jnp.concatenate([v] * H_A, 0) * vm_bf
    blast = b[0:1, :] if rev else b[c - 1:c, :]
    qbar = (q * jnp.exp(b)).astype(BF16)
    kdec = (k * jnp.exp(blast - b)).astype(BF16)
    o = _dot(s_tot, vbd) + _dot_nt(qbar, st_prev.astype(BF16))
    st_new = st_prev * jnp.exp(blast) + _dot_tn(v, kdec) * hm_f32
    return o, st_new


def _gla_kernel(*refs, seq, has_s0, alias_in, spb):
    it = iter(refs)
    qk_ref, v_ref, la_ref, b_ref, gate_ref, g_ref = (next(it) for _ in range(6))
    s0_ref = next(it) if has_s0 else None
    for _ in range(alias_in):
        next(it)
    oa_ref = next(it)
    sfin_ref = None if has_s0 else next(it)
    acc_sc, st_sc, blk_sc = next(it), next(it), next(it)

    c = GLA_CHUNK
    nc = seq // c
    acc_sc[...] = jnp.zeros_like(acc_sc)
    if not has_s0 and not alias_in:
        sfin_ref[:, 1:] = jnp.zeros(sfin_ref[:, 1:].shape, F32)

    hrow = lax.broadcasted_iota(jnp.int32, (H_A * c, 128), 0) // c
    hm_f32 = jnp.where(hrow == lax.broadcasted_iota(jnp.int32, (H_A * c, 128), 1) // DK_A, 1.0, 0.0)
    hm_bf = hm_f32.astype(BF16)
    vrow = lax.broadcasted_iota(jnp.int32, (H_A * c, H_A * DV_A), 0) // c
    vm_bf = jnp.where(vrow == lax.broadcasted_iota(jnp.int32, (H_A * c, H_A * DV_A), 1) // DV_A,
                      1.0, 0.0).astype(BF16)
    consts = (_gla_consts(False), _gla_consts(True))

    def run_sequence(bb):
        base = bb * seq
        for d in range(2):
            if has_s0:
                blk_sc[...] = jnp.zeros_like(blk_sc)
                for hh in range(H_A):
                    blk_sc[DK_A * hh:DK_A * (hh + 1), DV_A * hh:DV_A * (hh + 1)] = s0_ref[bb, 0, d, hh]
                st_sc[d] = blk_sc[...].T
            else:
                st_sc[d] = jnp.zeros((ST_R, ST_C), F32)

        def chunk_rows(n, d):
            cn = (nc - 1 - n) if d else n
            start = base + cn * c
            return pl.ds(start if isinstance(start, int) else pl.multiple_of(start, c), c)

        def scores(n):
            out = []
            for d in range(2):
                rows = chunk_rows(n, d)
                pos, lvl = consts[d]
                out.append(_gla_scores(qk_ref[rows, 0:128], qk_ref[rows, 128:256],
                                       la_ref[rows, 128 * d:128 * d + 128],
                                       b_ref[rows, 128 * d:128 * d + 128], pos, lvl, hm_bf, bool(d)))
            return tuple(out)

        def apply(n, s_both):
            for d in range(2):
                rows = chunk_rows(n, d)
                o, st_new = _gla_apply(s_both[d], qk_ref[rows, 0:128], qk_ref[rows, 128:256],
                                       v_ref[rows, :].astype(BF16), b_ref[rows, 128 * d:128 * d + 128],
                                       st_sc[d], hm_f32, vm_bf, bool(d))
                acc_sc[rows, :] = acc_sc[rows, :] + o
                st_sc[d] = st_new

        def body(n, s_cur):
            s_next = scores(n + 1)
            apply(n, s_cur)
            return s_next

        trips = nc - 1
        s_last = lax.fori_loop(0, trips, body, scores(0),
                               unroll=next(u for u in (5, 3, 1) if trips % u == 0))
        apply(nc - 1, s_last)

        rs = slice(base, base + seq)
        first = lax.broadcasted_iota(jnp.int32, (seq, 128), 1) < DV_A
        for p in range(H_A // 2):
            cols = slice(128 * p, 128 * (p + 1))
            oa = acc_sc[rs, cols]
            sq = oa * oa
            s0 = jnp.sum(jnp.where(first, sq, 0.0), axis=-1, keepdims=True)
            s1 = jnp.sum(jnp.where(first, 0.0, sq), axis=-1, keepdims=True)
            inv = jnp.where(first, lax.rsqrt(s0 * (1.0 / DV_A) + EPS), lax.rsqrt(s1 * (1.0 / DV_A) + EPS))
            oa_ref[rs, cols] = (oa * inv * g_ref[0, :, cols] * gate_ref[rs, cols]).astype(BF16)
        if not has_s0:
            for d in range(2):
                blk_sc[...] = st_sc[d].T
                for hh in range(H_A):
                    sfin_ref[bb, 0, d, hh] = blk_sc[DK_A * hh:DK_A * (hh + 1), DV_A * hh:DV_A * (hh + 1)]

    for bb in range(spb):
        run_sequence(bb)


def _gla_call(qk, v, la, bsum, gate, g4, layer, *, seq, state_in=None, state_out=None):
    n = qk.shape[0]
    nb = n // seq
    has_s0 = state_in is not None
    assert has_s0 or state_out is not None or layer == 0
    spb = max(1, min(nb, GLA_GROUP_ROWS // seq))
    blk = lambda w: pl.BlockSpec((spb * seq, w), lambda i: (i, 0))
    nl = DEPTH if (not has_s0 and state_out is None) else 1
    st_spec = pl.BlockSpec((spb, nl, 2, H_A, DK_A, DV_A), lambda i: (i, layer, 0, 0, 0, 0))
    in_specs = [blk(256), blk(256), blk(256), blk(256), blk(W_A), _layer_spec(g4.shape, layer)]
    args = [qk, v, la, bsum, gate, g4]
    out_specs = [blk(W_A)]
    out_shape = [jax.ShapeDtypeStruct((n, W_A), BF16)]
    aliases = {}
    if has_s0:
        in_specs.append(st_spec)
        args.append(state_in)
    else:
        out_specs.append(st_spec)
        out_shape.append(jax.ShapeDtypeStruct((nb, DEPTH, 2, H_A, DK_A, DV_A), F32))
        if state_out is not None:
            aliases[len(args)] = 1
            in_specs.append(_ANY)
            args.append(state_out)
    return pl.pallas_call(
        functools.partial(_gla_kernel, seq=seq, has_s0=has_s0, alias_in=len(aliases), spb=spb),
        grid=(nb // spb,), in_specs=in_specs, out_specs=out_specs, out_shape=out_shape,
        input_output_aliases=aliases,
        scratch_shapes=[pltpu.VMEM((spb * seq, W_A), F32), pltpu.VMEM((2, ST_R, ST_C), F32),
                        pltpu.VMEM((ST_C, ST_R), F32)],
        compiler_params=_cparams(), name="gla_state" if has_s0 else "gla_ctx",
    )(*args)


def _softmax_t_pv(st, vt, ones_rows):
    dv, keys = vt.shape
    m = _col_reduce(st, jnp.max)
    e = jnp.exp2(st - m)
    if not ones_rows:
        l = _col_reduce(e, jnp.sum)
        return _dot(vt, e.astype(BF16)) * (1.0 / l)
    o = _dot(jnp.concatenate([vt, jnp.ones((16, keys), BF16)], axis=0), e.astype(BF16))
    return o[:dv] * (1.0 / o[dv:dv + 1])


def _pipelined_attention(score_fns, value_fns, depth):
    outs = []
    pending = [fn() for fn in score_fns[:depth]]
    for j, vfn in enumerate(value_fns):
        st = pending.pop(0)
        if j + depth < len(score_fns):
            pending.append(score_fns[j + depth]())
        outs.append(_softmax_t_pv(st, *vfn()))
    return outs


def _lookahead(n_keys):
    return 2 if n_keys >= 1024 else 3


def _col_reduce(x, op):
    rows, cols = x.shape
    part = 128 if rows % 128 == 0 and rows > 128 else rows
    if part != rows:
        x = op(x.reshape(rows // part, part, cols), axis=0)
    return op(x, axis=0, keepdims=True)


def _seqs_per_step(n, seq, ctx_len):
    if ctx_len or seq >= ATT_QBLOCK:
        return 1
    return max(1, min(n // seq, ATT_GROUP_ROWS // seq))


def _for_query_blocks(run, seq, spb):
    qb = min(ATT_QBLOCK, seq)
    nq = seq // qb
    if spb * nq <= ATT_STATIC_BLOCKS:
        run([(bb, slice(bb * seq + j * qb, bb * seq + (j + 1) * qb)) for bb in range(spb) for j in range(nq)])
    else:
        assert spb == 1

        def body(i, carry):
            run([(0, pl.ds(pl.multiple_of(i * qb, qb), qb))])
            return carry
        lax.fori_loop(0, nq, body, 0)


def _mla_part(q_ref, k_ref, vt_ref, gate_ref, ck_ref, cvt_ref, ob_ref, kk_sc, vt_sc, *, seq, ctx_len, spb):
    for bb in range(spb):
        rs = slice(bb * seq, (bb + 1) * seq)
        for p in range(H_B // 2):
            kk_sc[bb, p, 0:seq, 0:128] = k_ref[rs, 128 * p:128 * p + 128]
            kk_sc[bb, p, 0:seq, 128:256] = k_ref[rs, 512:640]
            if ctx_len:
                kk_sc[bb, p, seq:seq + ctx_len, 0:128] = ck_ref[0, 0, :, 128 * p:128 * p + 128]
                kk_sc[bb, p, seq:seq + ctx_len, 128:256] = ck_ref[0, 0, :, 512:640]
        vt_sc[bb, :, 0:seq] = vt_ref[:, rs]
        if ctx_len:
            vt_sc[bb, :, seq:seq + ctx_len] = cvt_ref[0, 0]

    def scores(bb, rows, h):
        p, hh = divmod(h, 2)
        qn = q_ref[rows, 128 * p:128 * p + 128] * _lane_mask(128, DN_B * hh, DN_B, BF16)
        qp = (q_ref[rows, 512 + 128 * (h // 4):512 + 128 * (h // 4) + 128]
              * _lane_mask(128, DR_B * (h % 4), DR_B, BF16))
        return _dot_nt(kk_sc[bb, p], jnp.concatenate([qn, qp], axis=-1))

    def items(blocks):
        its = [(bb, rows, h) for bb, rows in blocks for h in range(H_B)]

        def finish(outs):
            for j, (_, rows) in enumerate(blocks):
                ob = jnp.concatenate(outs[H_B * j:H_B * (j + 1)], axis=0).T
                ob_ref[rows, :] = (ob * gate_ref[rows, :]).astype(BF16)

        return ([functools.partial(scores, bb, rows, h) for bb, rows, h in its],
                [functools.partial(lambda bb, h: (vt_sc[bb, DV_B * h:DV_B * (h + 1), :], not ctx_len), bb, h)
                 for bb, _, h in its],
                finish)

    return items


def _diff_part(qk_ref, vt_ref, gate_ref, lam_ref, g_ref, ck_ref, cv_ref, oc_ref, k_sc, vt_sc, *,
               seq, ctx_len, lam_init, spb):
    lam = (jnp.exp(jnp.sum(lam_ref[0, 0:1, :] * lam_ref[0, 1:2, :], axis=-1, keepdims=True))
           - jnp.exp(jnp.sum(lam_ref[0, 2:3, :] * lam_ref[0, 3:4, :], axis=-1, keepdims=True)) + lam_init)
    dh = 2 * DC
    for bb in range(spb):
        rs = slice(bb * seq, (bb + 1) * seq)
        for p in range(H_C // 2):
            k_sc[bb, p, 0:seq, :] = qk_ref[rs, 256 + 128 * p:256 + 128 * p + 128]
            if ctx_len:
                pair_t = jnp.concatenate([ck_ref[0, 0, 2 * p], ck_ref[0, 0, 2 * p + 1]], axis=0)
                k_sc[bb, p, seq:seq + ctx_len, :] = pair_t.T.astype(BF16)
        vt_sc[bb, :, 0:seq] = vt_ref[:, rs]
        if ctx_len:
            for h in range(H_C):
                vt_sc[bb, dh * h:dh * (h + 1), seq:seq + ctx_len] = cv_ref[0, 0, h].astype(BF16)

    def scores(bb, rows, h, comp):
        p, hh = divmod(h, 2)
        qm = qk_ref[rows, 128 * p:128 * p + 128] * _lane_mask(128, dh * hh + DC * comp, DC, BF16)
        return _dot_nt(k_sc[bb, p], qm)

    def items(blocks):
        its = [(bb, rows, h, comp) for bb, rows in blocks for h in range(H_C) for comp in range(2)]

        def finish(o12):
            for j, (_, rows) in enumerate(blocks):
                outs = []
                for h in range(H_C):
                    o1, o2 = o12[2 * (H_C * j + h)], o12[2 * (H_C * j + h) + 1]
                    ot = o1 - lam * o2
                    outs.append(ot * lax.rsqrt(jnp.mean(ot * ot, axis=0, keepdims=True) + EPS))
                oc = jnp.concatenate(outs, axis=0).T
                oc_ref[rows, :] = (oc * g_ref[0] * (1.0 - lam_init) * gate_ref[rows, :]).astype(BF16)

        return ([functools.partial(scores, *item) for item in its],
                [functools.partial(lambda bb, h: (vt_sc[bb, dh * h:dh * (h + 1), :], True), bb, h)
                 for bb, _, h, _ in its],
                finish)

    return items


def _attn_kernel(*refs, seq, ctx_len, lam_init, spb, mla, diff):
    it = iter(refs)
    n_ctx = 2 if ctx_len else 0
    mla_in = [next(it) for _ in range(4 + n_ctx)] + [None] * (2 - n_ctx) if mla else None
    diff_in = [next(it) for _ in range(5 + n_ctx)] + [None] * (2 - n_ctx) if diff else None
    ob_ref = next(it) if mla else None
    oc_ref = next(it) if diff else None
    parts = []
    if mla:
        parts.append(_mla_part(*mla_in, ob_ref, next(it), next(it), seq=seq, ctx_len=ctx_len, spb=spb))
    if diff:
        parts.append(_diff_part(*diff_in, oc_ref, next(it), next(it), seq=seq, ctx_len=ctx_len,
                                lam_init=lam_init, spb=spb))

    def run(blocks):
        built = [part(blocks) for part in parts]
        outs = _pipelined_attention([f for b in built for f in b[0]], [f for b in built for f in b[1]],
                                    _lookahead(seq + ctx_len))
        lo = 0
        for score_fns, _, finish in built:
            finish(outs[lo:lo + len(score_fns)])
            lo += len(score_fns)

    _for_query_blocks(run, seq, spb)


def _attn_call(mq, mk, mvt, gate_b, dqk, dvt, gate_c, lamp, g4, ctx, layer, *, seq):
    n = mq.shape[0]
    ctx_len = 0 if ctx is None else ctx["mla_k"].shape[2]
    lam_init = 0.8 - 0.6 * math.exp(-0.3 * layer)
    spb = _seqs_per_step(n, seq, ctx_len)
    rows = spb * seq
    tk = seq + ctx_len
    rblk = lambda w: pl.BlockSpec((rows, w), lambda i: (i, 0))
    tblk = lambda w: pl.BlockSpec((w, rows), lambda i: (0, i))
    mla_specs, mla_args = [rblk(768), rblk(640), tblk(W_B), rblk(W_B)], [mq, mk, mvt, gate_b]
    diff_specs = [rblk(512), tblk(W_C), rblk(W_C), _layer_spec(lamp.shape, layer), _layer_spec(g4.shape, layer)]
    diff_args = [dqk, dvt, gate_c, lamp, g4]
    if ctx_len:
        mla_specs += [pl.BlockSpec((1, 1, ctx_len, 640), lambda i: (layer, i, 0, 0)),
                      pl.BlockSpec((1, 1, W_B, ctx_len), lambda i: (layer, i, 0, 0))]
        mla_args += [ctx["mla_k"], ctx["mla_vt"]]
        diff_specs += [pl.BlockSpec((1, 1, H_C, 2 * DC, ctx_len), lambda i: (i, layer, 0, 0, 0))] * 2
        diff_args += [ctx["diff_k_t"], ctx["diff_v_t"]]
    mla_out = (rblk(W_B), jax.ShapeDtypeStruct((n, W_B), BF16))
    diff_out = (rblk(W_C), jax.ShapeDtypeStruct((n, W_C), BF16))
    mla_scratch = [pltpu.VMEM((spb, H_B // 2, tk, 256), BF16), pltpu.VMEM((spb, W_B, tk), BF16)]
    diff_scratch = [pltpu.VMEM((spb, H_C // 2, tk, 128), BF16), pltpu.VMEM((spb, W_C, tk), BF16)]

    def call(mla, diff, name):
        outs = ([mla_out] if mla else []) + ([diff_out] if diff else [])
        return pl.pallas_call(
            functools.partial(_attn_kernel, seq=seq, ctx_len=ctx_len, lam_init=lam_init, spb=spb,
                              mla=mla, diff=diff),
            grid=(n // rows,),
            in_specs=(mla_specs if mla else []) + (diff_specs if diff else []),
            out_specs=[o[0] for o in outs], out_shape=[o[1] for o in outs],
            scratch_shapes=(mla_scratch if mla else []) + (diff_scratch if diff else []),
            compiler_params=_cparams(), name=name,
        )(*(mla_args if mla else []), *(diff_args if diff else []))

    if ctx_len:
        (ob,) = call(True, False, "mla_ctx")
        (oc,) = call(False, True, "diff_ctx")
        return ob, oc
    return call(True, True, "attn_self")


def _post_kernel(oa_ref, ob_ref, oc_ref, x_ref, mod_ref, w_ref, g_ref, y_ref, *, layer):
    mix = jnp.concatenate([oa_ref[...], ob_ref[...], oc_ref[...]], axis=-1)
    y_ref[...] = x_ref[...] + _rms(_dot(mix, w_ref[0]),
                                   _layer_row(g_ref, layer) * mod_ref[0, :, 2 * D_MODEL:3 * D_MODEL])


def _post_call(oa, ob, oc, x2d, mod, pw, layer, *, seq, sample):
    n = x2d.shape[0]
    tm = min(POST_BLOCK, seq if sample else n)
    blk = lambda w: pl.BlockSpec((tm, w), lambda i: (i, 0))

    def mod_idx(i):
        return (layer * MOD_ROWS + ((i * tm) // seq + 1 if sample else 0), 0, 0)

    return pl.pallas_call(
        functools.partial(_post_kernel, layer=layer),
        grid=(n // tm,),
        in_specs=[blk(W_A), blk(W_B), blk(W_C), blk(D_MODEL),
                  pl.BlockSpec((1, 1, 3 * D_MODEL), mod_idx),
                  _layer_spec(pw["w_out"].shape, layer), _layer_spec(pw["g_post"].shape, layer)],
        out_specs=blk(D_MODEL),
        out_shape=jax.ShapeDtypeStruct((n, D_MODEL), F32),
        compiler_params=_cparams(), name="post",
    )(oa, ob, oc, x2d, mod, pw["w_out"], pw["g_post"])


def _pack_params(g_pre, g_post, w_in, w_gla_af, b_gla_af, w_gla_ab, b_gla_ab, g_gla, g_mla_q, w_mla_uq,
                 g_mla_kv, w_mla_ukv, lam_q1, lam_k1, lam_q2, lam_k2, g_diff, w_out):
    w_t = jnp.swapaxes(w_in, 1, 2)
    assert w_t.shape[1] == R_DIFF[0][1]
    zg = jnp.zeros((DEPTH, GLA_LR, 128), F32)
    w_gate = jnp.concatenate([jnp.zeros((DEPTH, DR_B, 256), F32),
                              jnp.concatenate([w_gla_af, zg], axis=-1),
                              jnp.concatenate([zg, w_gla_ab], axis=-1),
                              jnp.zeros((DEPTH, 128 - DR_B - 2 * GLA_LR, 256), F32)], axis=1).astype(BF16)
    uq = w_mla_uq.reshape(DEPTH, Q_LORA, H_B, DN_B + DR_B)
    w_pe = uq[..., DN_B:].reshape(DEPTH, Q_LORA, H_B * DR_B)
    w_uq = jnp.concatenate([uq[..., :DN_B].reshape(DEPTH, Q_LORA, H_B * DN_B), w_pe], axis=-1).astype(BF16)
    ukv = w_mla_ukv.reshape(DEPTH, KV_LORA, H_B, DN_B + DV_B)
    w_ukv = jnp.concatenate([ukv[..., :DN_B].reshape(DEPTH, KV_LORA, H_B * DN_B),
                             ukv[..., DN_B:].reshape(DEPTH, KV_LORA, H_B * DV_B)], axis=-1).astype(BF16)
    row = lambda a: a.reshape(DEPTH, 1, a.shape[-1])
    return dict(
        w_t=w_t.astype(BF16), w_gate=w_gate,
        b_gate=row(jnp.concatenate([b_gla_af, b_gla_ab], axis=-1)),
        w_uq=w_uq, w_ukv=w_ukv, w_out=w_out.astype(BF16),
        g_pre=g_pre, g_post=g_post, g_mla_q=g_mla_q, g_mla_kv=g_mla_kv,
        g_gla4=row(jnp.tile(g_gla, (1, H_A))), g_diff4=row(jnp.tile(g_diff, (1, H_C))),
        lam=jnp.stack([lam_q1, lam_k1, lam_q2, lam_k2], axis=1))


def _rope_tables(n):
    t = np.arange(n)
    row = (t // GRID_W).astype(np.float32)
    col = (t % GRID_W).astype(np.float32)
    half = ROPE_DIM // 2
    inv = (1.0 / (np.float32(ROPE_THETA) ** (np.arange(0, half, 2, dtype=np.float32) / np.float32(half)))
           ).astype(np.float32)
    ar = row[:, None] * inv
    ac = col[:, None] * inv
    ang = np.concatenate([ar, ar, ac, ac], axis=-1).astype(np.float32)
    return (jnp.asarray(np.tile(np.cos(ang), (1, 8)).astype(np.float32)),
            jnp.asarray(np.tile(np.sin(ang), (1, 8)).astype(np.float32)))


def _sublayer(x2d, mod, pw, layer, *, seq, rope_tabs, ctx, caches):
    sample = ctx is not None
    pre = _pre_call(x2d, mod, pw, layer, seq=seq, rope_tabs=rope_tabs,
                    caches=None if sample else caches[:4])
    qk, v, la, bsum, ga, gb, gc, mq, mk, mvt, dqk, dvt = pre[:12]
    if sample:
        (oa,) = _gla_call(qk, v, la, bsum, ga, pw["g_gla4"], layer, seq=seq, state_in=ctx["state"])
        new_caches = None
    else:
        oa, sfin = _gla_call(qk, v, la, bsum, ga, pw["g_gla4"], layer, seq=seq,
                             state_out=caches[4] if caches else None)
        new_caches = tuple(pre[12:]) + (sfin,)
    ob, oc = _attn_call(mq, mk, mvt, gb, dqk, dvt, gc, pw["lam"], pw["g_diff4"], ctx, layer, seq=seq)
    y = _post_call(oa, ob, oc, x2d, mod, pw, layer, seq=seq, sample=sample)
    return y, new_caches


def kernel(x_prompt, x_sample, c, cache_mla_ckv, cache_mla_kpe, cache_diff_k, cache_diff_v, state_gla,
           c_ctx, w_ada, b_ada, g_pre, g_post, w_in, w_gla_af, b_gla_af, w_gla_ab, b_gla_ab, g_gla,
           g_mla_q, w_mla_uq, g_mla_kv, w_mla_ukv, lam_q1, lam_k1, lam_q2, lam_k2, g_diff, w_out):
    bp, tp, d = x_prompt.shape
    bs, ts, _ = x_sample.shape

    pw = _pack_params(g_pre, g_post, w_in, w_gla_af, b_gla_af, w_gla_ab, b_gla_ab, g_gla, g_mla_q,
                      w_mla_uq, g_mla_kv, w_mla_ukv, lam_q1, lam_k1, lam_q2, lam_k2, g_diff, w_out)
    mod = _mod_call(c_ctx.reshape(1, d), c, w_ada, b_ada)
    rope_tabs = _rope_tables(ts)
    ctx_k, ctx_vt = _ctxkv_call(cache_mla_ckv, jnp.swapaxes(cache_mla_kpe, -1, -2), pw["w_ukv"])
    ctx = dict(state=state_gla, mla_k=ctx_k, mla_vt=ctx_vt,
               diff_k_t=jnp.swapaxes(cache_diff_k, -1, -2), diff_v_t=jnp.swapaxes(cache_diff_v, -1, -2))

    y_p = x_prompt.reshape(bp * tp, d)
    y_s = x_sample.reshape(bs * ts, d)
    caches = ()
    for l in range(DEPTH):
        y_p, caches = _sublayer(y_p, mod, pw, l, seq=tp, rope_tabs=None, ctx=None, caches=caches)
        y_s, _ = _sublayer(y_s, mod, pw, l, seq=ts, rope_tabs=rope_tabs, ctx=ctx, caches=None)
    ckvn, kpe_t, kc_t, vc_t, new_state = caches
    return (y_p.reshape(bp, tp, d), y_s.reshape(bs, ts, d), ckvn, jnp.swapaxes(kpe_t, -1, -2),
            jnp.swapaxes(kc_t, -1, -2), jnp.swapaxes(vc_t, -1, -2), new_state)
```

```python
import functools
import math

import numpy as np
import jax
import jax.numpy as jnp
from jax import lax
from jax.experimental import pallas as pl
from jax.experimental.pallas import tpu as pltpu

F32 = jnp.float32
BF16 = jnp.bfloat16

D_MODEL = 1024
DEPTH = 2
GRID_W = 64
EPS = 1e-6
ROPE_THETA = 10000.0
ROPE_DIM = 32
H_A, DK_A, DV_A = 4, 32, 64
GLA_LR = 16
GLA_TAU = 16.0
GLA_CHUNK = 64
H_B, DN_B, DR_B, DV_B = 8, 64, 32, 64
Q_LORA, KV_LORA = 256, 128
H_C, DC = 4, 32
W_A, W_B, W_C = H_A * DV_A, H_B * DV_B, H_C * 2 * DC
ST_R, ST_C = H_A * DV_A, H_A * DK_A
LOG2E = math.log2(math.e)
SUBLANES = 8

R_QKV = ((0, 512),)
R_GG = ((544, 800),)
R_MLA = ((800, 1216), (512, 544), 64)
R_MG = ((1216, 1728),)
R_DIFF = ((1728, 2752),)
MOD_ROWS = 8

V7X_VMEM_LIMIT_BYTES = 56 * 1024 * 1024
TOKEN_BLOCK = 1024
POST_BLOCK = 1024
ATT_QBLOCK = 512
ATT_GROUP_ROWS = 1024
GLA_GROUP_ROWS = 1024
ATT_STATIC_BLOCKS = 4


def _cparams(n_axes=1):
    return pltpu.CompilerParams(dimension_semantics=("arbitrary",) * n_axes,
                                vmem_limit_bytes=V7X_VMEM_LIMIT_BYTES)


def _rms(x, g):
    return x * lax.rsqrt(jnp.mean(x * x, axis=-1, keepdims=True) + EPS) * g


def _silu(x):
    return x * jax.nn.sigmoid(x)


def _log_sigmoid(x):
    return jnp.minimum(x, 0.0) - jnp.log1p(jnp.exp(-jnp.abs(x)))


def _rope(z, cos, sin):
    w = z.shape[-1]
    lane = lax.broadcasted_iota(jnp.int32, z.shape, 1)
    rot = jnp.where((lane & 15) < 8, -pltpu.roll(z, w - 8, 1), pltpu.roll(z, 8, 1))
    return z * cos + rot * sin


def _lane_mask(width, lo, size, dtype):
    lane = lax.broadcasted_iota(jnp.int32, (1, width), 1)
    return jnp.where((lane >= lo) & (lane < lo + size), 1.0, 0.0).astype(dtype)


def _dot(a, b):
    return jnp.dot(a, b, preferred_element_type=F32)


def _dot_nt(a, b):
    return lax.dot_general(a, b, (((1,), (1,)), ((), ())), preferred_element_type=F32)


def _dot_tn(a, b):
    return lax.dot_general(a, b, (((0,), (0,)), ((), ())), preferred_element_type=F32)


def _layer_spec(shape, layer):
    nd = len(shape)
    if nd == 2:
        return pl.BlockSpec(tuple(shape), lambda *_: (0, 0))
    return pl.BlockSpec((1,) + tuple(shape[1:]), lambda *_: (layer,) + (0,) * (nd - 1))


def _layer_row(ref, layer):
    return ref[layer:layer + 1, :]


_ANY = pl.BlockSpec(memory_space=pl.ANY)


def _mod_kernel(cctx_ref, c_ref, w_ref, b_ref, o_ref, rows_sc):
    nc = c_ref.shape[0]
    rows_sc[0:1, :] = cctx_ref[...]
    rows_sc[1:1 + nc, :] = c_ref[...]
    if 1 + nc < MOD_ROWS:
        rows_sc[1 + nc:, :] = jnp.zeros((MOD_ROWS - 1 - nc, D_MODEL), F32)
    s = _silu(rows_sc[...]).astype(BF16)
    layer = pl.program_id(0)
    bias = b_ref[0:1, :]
    for r in range(1, DEPTH):
        bias = jnp.where(layer == r, b_ref[r:r + 1, :], bias)
    mod = _dot(s, w_ref[0].astype(BF16)) + bias
    for r in range(MOD_ROWS):
        o_ref[r] = mod[r:r + 1, :]


def _mod_call(c_ctx, c, w_ada, b_ada):
    nb = 3 * D_MODEL // 2
    assert 1 + c.shape[0] <= MOD_ROWS
    return pl.pallas_call(
        _mod_kernel,
        grid=(DEPTH, 3 * D_MODEL // nb),
        in_specs=[pl.BlockSpec((1, D_MODEL), lambda l, j: (0, 0)),
                  pl.BlockSpec(c.shape, lambda l, j: (0, 0)),
                  pl.BlockSpec((1, D_MODEL, nb), lambda l, j: (l, 0, j)),
                  pl.BlockSpec((DEPTH, nb), lambda l, j: (0, j))],
        out_specs=pl.BlockSpec((MOD_ROWS, 1, nb), lambda l, j: (l, 0, j)),
        out_shape=jax.ShapeDtypeStruct((DEPTH * MOD_ROWS, 1, 3 * D_MODEL), F32),
        scratch_shapes=[pltpu.VMEM((MOD_ROWS, D_MODEL), F32)],
        compiler_params=_cparams(2), name="adaln_mod",
    )(c_ctx, c, w_ada, b_ada)


def _ctxkv_kernel(ckv_ref, kpe_ref, w_ref, k_ref, vt_ref):
    for b in range(ckv_ref.shape[0]):
        kv = _dot(ckv_ref[b, 0].astype(BF16), w_ref[0])
        kpe4 = jnp.concatenate([kpe_ref[b, 0]] * 4, axis=0).T
        k_ref[0, b, :, 0:512] = kv[:, 0:512].astype(BF16)
        k_ref[0, b, :, 512:640] = kpe4.astype(BF16)
        vt_ref[0, b] = kv[:, 512:1024].T.astype(BF16)


def _ctxkv_call(cache_ckv, cache_kpe_t, wukv):
    nb, _, tc, _ = cache_ckv.shape
    return pl.pallas_call(
        _ctxkv_kernel,
        grid=(DEPTH,),
        in_specs=[pl.BlockSpec((nb, 1, tc, KV_LORA), lambda l: (0, l, 0, 0)),
                  pl.BlockSpec((nb, 1, DR_B, tc), lambda l: (0, l, 0, 0)),
                  pl.BlockSpec((1, KV_LORA, 1024), lambda l: (l, 0, 0))],
        out_specs=[pl.BlockSpec((1, nb, tc, 640), lambda l: (l, 0, 0, 0)),
                   pl.BlockSpec((1, nb, W_B, tc), lambda l: (l, 0, 0, 0))],
        out_shape=[jax.ShapeDtypeStruct((DEPTH, nb, tc, 640), BF16),
                   jax.ShapeDtypeStruct((DEPTH, nb, W_B, tc), BF16)],
        compiler_params=_cparams(), name="mla_ctx_kv",
    )(cache_ckv, cache_kpe_t, wukv)


def _pre_kernel(*refs, rope, ctx_out, alias_in, bpb, seq, layer):
    it = iter(refs)
    (x_ref, mod_ref, gpre_ref, w_ref, wg_ref, bg_ref, gq_ref, gkv_ref, wuq_ref,
     wukv_ref) = (next(it) for _ in range(10))
    if rope:
        cos_ref, sin_ref = next(it), next(it)
    for _ in range(alias_in):
        next(it)
    (qk_ref, v_ref, la_ref, bs_ref, ga_ref, gb_ref, gc_ref, mq_ref, mk_ref, mvt_ref, dqk_ref,
     dvt_ref) = (next(it) for _ in range(12))
    if ctx_out:
        ckvn_ref, kpe_ref, kc_ref, vc_ref = (next(it) for _ in range(4))

    d = D_MODEL
    shift = mod_ref[0, :, 0:d]
    scale = mod_ref[0, :, d:2 * d]
    h = (_rms(x_ref[...], _layer_row(gpre_ref, layer) * (1.0 + scale)) + shift).astype(BF16)
    def proj(group):
        parts = [jnp.zeros((r, d), BF16) if isinstance(r, int) else w_ref[0, r[0]:r[1], :] for r in group]
        return _dot_nt(h, parts[0] if len(parts) == 1 else jnp.concatenate(parts, axis=0))
    if rope:
        cos = cos_ref[...]
        sin = sin_ref[...]

    pg = proj(R_QKV)
    qk_ref[:, 0:128] = pg[:, 0:128] * (DK_A ** -0.5)
    qk_ref[:, 128:256] = pg[:, 128:256]
    v_ref[...] = pg[:, 256:512]
    pm = proj(R_MLA)
    tail = pm[:, 384:512]
    xg = _dot(tail.astype(BF16), wg_ref[0]) + bg_ref[0]
    la = _log_sigmoid(xg) * (1.0 / GLA_TAU)
    la_ref[...] = la
    bs_ref[:, 0:128] = _chunk_scan(la[:, 0:128], False)
    bs_ref[:, 128:256] = _chunk_scan(la[:, 128:256], True)
    ga_ref[...] = _silu(proj(R_GG)).astype(BF16)

    qall = _dot(_rms(pm[:, 0:256], _layer_row(gq_ref, layer)).astype(BF16), wuq_ref[0])
    q_pe = qall[:, 512:768]
    if rope:
        q_pe = _rope(q_pe, cos, sin)
    sb = (DN_B + DR_B) ** -0.5 * LOG2E
    mq_ref[:, 0:512] = (qall[:, 0:512] * sb).astype(BF16)
    mq_ref[:, 512:768] = (q_pe * sb).astype(BF16)
    ckvn = _rms(pm[:, 256:384], _layer_row(gkv_ref, layer))
    kvall = _dot(ckvn.astype(BF16), wukv_ref[0])
    lane = lax.broadcasted_iota(jnp.int32, tail.shape, 1)
    kpe4 = jnp.where(lane < DR_B, tail, 0.0)
    kpe4 = kpe4 + pltpu.roll(kpe4, DR_B, 1)
    kpe4 = kpe4 + pltpu.roll(kpe4, 2 * DR_B, 1)
    if rope:
        kpe4 = _rope(kpe4, cos[:, 0:128], sin[:, 0:128])
    mk_ref[:, 0:512] = kvall[:, 0:512].astype(BF16)
    mk_ref[:, 512:640] = kpe4.astype(BF16)
    mvt_ref[...] = kvall[:, 512:1024].T.astype(BF16)
    gb_ref[...] = _silu(proj(R_MG)).astype(BF16)

    pd = proj(R_DIFF)
    dq, dk, dv = pd[:, 0:256], pd[:, 256:512], pd[:, 512:768]
    if rope:
        dq = _rope(dq, cos, sin)
        dk = _rope(dk, cos, sin)
    dqk_ref[:, 0:256] = (dq * (DC ** -0.5 * LOG2E)).astype(BF16)
    dqk_ref[:, 256:512] = dk.astype(BF16)
    dv_t = dv.T
    dvt_ref[...] = dv_t.astype(BF16)
    gc_ref[...] = _silu(pd[:, 768:1024]).astype(BF16)
    if ctx_out:
        kpe_t = kpe4.T
        dk_t = dk.T
        for ref in () if alias_in else (ckvn_ref, kpe_ref, kc_ref, vc_ref):
            ref[:, 1:] = jnp.zeros(ref[:, 1:].shape, F32)
        for bb in range(bpb):
            rs = slice(bb * seq, (bb + 1) * seq)
            ckvn_ref[bb, 0] = ckvn[rs]
            kpe_ref[bb, 0] = kpe_t[0:DR_B, rs]
            kc_ref[bb, 0] = dk_t[:, rs].reshape(H_C, 2 * DC, seq)
            vc_ref[bb, 0] = dv_t[:, rs].reshape(H_C, 2 * DC, seq)


def _pre_call(x2d, mod, pw, layer, *, seq, rope_tabs, caches):
    n = x2d.shape[0]
    tm = min(TOKEN_BLOCK, n)
    bpb = max(tm // seq, 1)
    rope = rope_tabs is not None
    ctx_out = caches is not None
    steps_per_seq = max(seq // tm, 1)
    nbt = n // seq

    def mod_idx(i):
        return (layer * MOD_ROWS + ((i * tm) // seq + 1 if rope else 0), 0, 0)

    names = ["g_pre", "w_t", "w_gate", "b_gate", "g_mla_q", "g_mla_kv", "w_uq", "w_ukv"]
    in_specs = [pl.BlockSpec((tm, D_MODEL), lambda i: (i, 0)), pl.BlockSpec((1, 1, 3 * D_MODEL), mod_idx)]
    in_specs += [_layer_spec(pw[k].shape, layer) for k in names]
    args = [x2d, mod] + [pw[k] for k in names]
    if rope:
        in_specs += [pl.BlockSpec((tm, 256), lambda i: (i % steps_per_seq, 0))] * 2
        args += list(rope_tabs)
    outs = [(256, F32, False), (256, F32, False), (256, F32, False), (256, F32, False),
            (W_A, BF16, False), (W_B, BF16, False), (W_C, BF16, False),
            (768, BF16, False), (640, BF16, False), (W_B, BF16, True),
            (512, BF16, False), (W_C, BF16, True)]
    out_specs = [pl.BlockSpec((w, tm), lambda i: (0, i)) if tr else pl.BlockSpec((tm, w), lambda i: (i, 0))
                 for w, _, tr in outs]
    out_shape = [jax.ShapeDtypeStruct((w, n) if tr else (n, w), dt) for w, dt, tr in outs]
    widths = outs
    aliases = {}
    if ctx_out:
        assert caches or layer == 0
        nl = 1 if caches else DEPTH
        out_specs += [pl.BlockSpec((bpb, nl, seq, KV_LORA), lambda i: (i, layer, 0, 0)),
                      pl.BlockSpec((bpb, nl, DR_B, seq), lambda i: (i, layer, 0, 0)),
                      pl.BlockSpec((bpb, nl, H_C, 2 * DC, seq), lambda i: (i, layer, 0, 0, 0)),
                      pl.BlockSpec((bpb, nl, H_C, 2 * DC, seq), lambda i: (i, layer, 0, 0, 0))]
        out_shape += [jax.ShapeDtypeStruct((nbt, DEPTH, seq, KV_LORA), F32),
                      jax.ShapeDtypeStruct((nbt, DEPTH, DR_B, seq), F32),
                      jax.ShapeDtypeStruct((nbt, DEPTH, H_C, 2 * DC, seq), F32),
                      jax.ShapeDtypeStruct((nbt, DEPTH, H_C, 2 * DC, seq), F32)]
        for j, arr in enumerate(caches):
            aliases[len(args)] = len(widths) + j
            in_specs.append(_ANY)
            args.append(arr)
    return pl.pallas_call(
        functools.partial(_pre_kernel, rope=rope, ctx_out=ctx_out, alias_in=len(aliases), bpb=bpb, seq=seq,
                          layer=layer),
        grid=(n // tm,), in_specs=in_specs, out_specs=out_specs, out_shape=out_shape,
        input_output_aliases=aliases,
        compiler_params=_cparams(), name="pre_rope" if rope else "pre_ctx",
    )(*args)


_GLA_LEVELS = (1, 2, 4, 8, 16, 32)


def _gla_consts(rev):
    c = GLA_CHUNK
    row = lax.broadcasted_iota(jnp.int32, (c, 128), 0)
    pos = (c - 1 - row) if rev else row
    ri = lax.broadcasted_iota(jnp.int32, (c, H_A * c), 0)
    cj = lax.broadcasted_iota(jnp.int32, (c, H_A * c), 1) & (c - 1)
    pi = (c - 1 - ri) if rev else ri
    pj = (c - 1 - cj) if rev else cj
    x = pi ^ pj
    lvl = jnp.where(pi == pj, 0, -1)
    for kbit in range(6):
        lvl = jnp.where((pj < pi) & ((x >> kbit) == 1), kbit + 1, lvl)
    return pos, lvl


def _chunk_scan(x, rev):
    rows = x.shape[0]
    nt = rows // SUBLANES
    tiles_per_chunk = GLA_CHUNK // SUBLANES
    x3 = x.reshape(nt, SUBLANES, 128)
    sub = lax.broadcasted_iota(jnp.int32, x3.shape, 1)
    tile = lax.broadcasted_iota(jnp.int32, x3.shape, 0) & (tiles_per_chunk - 1)
    edge = 0 if rev else SUBLANES - 1
    s = 1
    while s < SUBLANES:
        if rev:
            x3 = x3 + jnp.where(sub < SUBLANES - s, pltpu.roll(x3, SUBLANES - s, 1), 0.0)
        else:
            x3 = x3 + jnp.where(sub >= s, pltpu.roll(x3, s, 1), 0.0)
        s *= 2
    s = 1
    while s < tiles_per_chunk:
        tot = jnp.broadcast_to(x3[:, edge:edge + 1, :], x3.shape)
        if rev:
            shifted = jnp.concatenate([tot[s:], tot[:s]], axis=0)
            x3 = x3 + jnp.where(tile < tiles_per_chunk - s, shifted, 0.0)
        else:
            shifted = jnp.concatenate([tot[nt - s:], tot[:nt - s]], axis=0)
            x3 = x3 + jnp.where(tile >= s, shifted, 0.0)
        s *= 2
    return x3.reshape(rows, 128)


def _gla_scores(q, k, la, b, pos, lvl, hm_bf, rev):
    c = GLA_CHUNK
    prv = pltpu.roll(la, c - 1 if rev else 1, 0)
    nxt = pltpu.roll(la, 1 if rev else c - 1, 0)
    lvl = lvl.astype(BF16)
    s_tot = jnp.where(lvl == 0, _dot_nt(q.astype(BF16), jnp.concatenate([k.astype(BF16)] * H_A, 0) * hm_bf
                                        ).astype(BF16), jnp.zeros((), BF16))
    for kbit, m in enumerate(_GLA_LEVELS):
        up = ((pos >> kbit) & 1) == 1
        if m == 1:
            e = jnp.where(up, la, 0.0)
        elif m == 2:
            c4 = pos & 3
            e = jnp.where(c4 == 0, nxt, jnp.where(c4 == 1, 0.0, jnp.where(c4 == 2, la, la + prv)))
        else:
            nblk = c // (2 * m)
            loc = m if rev else m - 1
            b3 = b.reshape(nblk, 2 * m, 128)
            ref = jnp.broadcast_to(b3[:, loc:loc + 1, :], (nblk, 2 * m, 128)).reshape(c, 128)
            dlt = b - ref
            e = jnp.where(up, dlt, -dlt)
        xm = (jnp.where(up, q, k) * jnp.exp(e)).astype(BF16)
        sm = _dot_nt(xm, jnp.concatenate([xm] * H_A, 0) * hm_bf)
        s_tot = jnp.where(lvl == kbit + 1, sm.astype(BF16), s_tot)
    return s_tot


def _gla_apply(s_tot, q, k, v, b, st_prev, hm_f32, vm_bf, rev):
    c = GLA_CHUNK
    vbd = jnp.concatenate([v] * H_A, 0) * vm_bf
    blast = b[0:1, :] if rev else b[c - 1:c, :]
    qbar = (q * jnp.exp(b)).astype(BF16)
    kdec = (k * jnp.exp(blast - b)).astype(BF16)
    o = _dot(s_tot, vbd) + _dot_nt(qbar, st_prev.astype(BF16))
    st_new = st_prev * jnp.exp(blast) + _dot_tn(v, kdec) * hm_f32
    return o, st_new


def _gla_kernel(*refs, seq, has_s0, alias_in, spb):
    it = iter(refs)
    qk_ref, v_ref, la_ref, b_ref, gate_ref, g_ref = (next(it) for _ in range(6))
    s0_ref = next(it) if has_s0 else None
    for _ in range(alias_in):
        next(it)
    oa_ref = next(it)
    sfin_ref = None if has_s0 else next(it)
    acc_sc, st_sc, blk_sc = next(it), next(it), next(it)

    c = GLA_CHUNK
    nc = seq // c
    acc_sc[...] = jnp.zeros_like(acc_sc)
    if not has_s0 and not alias_in:
        sfin_ref[:, 1:] = jnp.zeros(sfin_ref[:, 1:].shape, F32)

    hrow = lax.broadcasted_iota(jnp.int32, (H_A * c, 128), 0) // c
    hm_f32 = jnp.where(hrow == lax.broadcasted_iota(jnp.int32, (H_A * c, 128), 1) // DK_A, 1.0, 0.0)
    hm_bf = hm_f32.astype(BF16)
    vrow = lax.broadcasted_iota(jnp.int32, (H_A * c, H_A * DV_A), 0) // c
    vm_bf = jnp.where(vrow == lax.broadcasted_iota(jnp.int32, (H_A * c, H_A * DV_A), 1) // DV_A,
                      1.0, 0.0).astype(BF16)
    consts = (_gla_consts(False), _gla_consts(True))

    def run_sequence(bb):
        base = bb * seq
        for d in range(2):
            if has_s0:
                blk_sc[...] = jnp.zeros_like(blk_sc)
                for hh in range(H_A):
                    blk_sc[DK_A * hh:DK_A * (hh + 1), DV_A * hh:DV_A * (hh + 1)] = s0_ref[bb, 0, d, hh]
                st_sc[d] = blk_sc[...].T
            else:
                st_sc[d] = jnp.zeros((ST_R, ST_C), F32)

        def chunk_rows(n, d):
            cn = (nc - 1 - n) if d else n
            start = base + cn * c
            return pl.ds(start if isinstance(start, int) else pl.multiple_of(start, c), c)

        def scores(n):
            out = []
            for d in range(2):
                rows = chunk_rows(n, d)
                pos, lvl = consts[d]
                out.append(_gla_scores(qk_ref[rows, 0:128], qk_ref[rows, 128:256],
                                       la_ref[rows, 128 * d:128 * d + 128],
                                       b_ref[rows, 128 * d:128 * d + 128], pos, lvl, hm_bf, bool(d)))
            return tuple(out)

        def apply(n, s_both):
            for d in range(2):
                rows = chunk_rows(n, d)
                o, st_new = _gla_apply(s_both[d], qk_ref[rows, 0:128], qk_ref[rows, 128:256],
                                       v_ref[rows, :].astype(BF16), b_ref[rows, 128 * d:128 * d + 128],
                                       st_sc[d], hm_f32, vm_bf, bool(d))
                acc_sc[rows, :] = acc_sc[rows, :] + o
                st_sc[d] = st_new

        def body(n, s_cur):
            s_next = scores(n + 1)
            apply(n, s_cur)
            return s_next

        trips = nc - 1
        s_last = lax.fori_loop(0, trips, body, scores(0),
                               unroll=next(u for u in (5, 3, 1) if trips % u == 0))
        apply(nc - 1, s_last)

        rs = slice(base, base + seq)
        first = lax.broadcasted_iota(jnp.int32, (seq, 128), 1) < DV_A
        for p in range(H_A // 2):
            cols = slice(128 * p, 128 * (p + 1))
            oa = acc_sc[rs, cols]
            sq = oa * oa
            s0 = jnp.sum(jnp.where(first, sq, 0.0), axis=-1, keepdims=True)
            s1 = jnp.sum(jnp.where(first, 0.0, sq), axis=-1, keepdims=True)
            inv = jnp.where(first, lax.rsqrt(s0 * (1.0 / DV_A) + EPS), lax.rsqrt(s1 * (1.0 / DV_A) + EPS))
            oa_ref[rs, cols] = (oa * inv * g_ref[0, :, cols] * gate_ref[rs, cols]).astype(BF16)
        if not has_s0:
            for d in range(2):
                blk_sc[...] = st_sc[d].T
                for hh in range(H_A):
                    sfin_ref[bb, 0, d, hh] = blk_sc[DK_A * hh:DK_A * (hh + 1), DV_A * hh:DV_A * (hh + 1)]

    for bb in range(spb):
        run_sequence(bb)


def _gla_call(qk, v, la, bsum, gate, g4, layer, *, seq, state_in=None, state_out=None):
    n = qk.shape[0]
    nb = n // seq
    has_s0 = state_in is not None
    assert has_s0 or state_out is not None or layer == 0
    spb = max(1, min(nb, GLA_GROUP_ROWS // seq))
    blk = lambda w: pl.BlockSpec((spb * seq, w), lambda i: (i, 0))
    nl = DEPTH if (not has_s0 and state_out is None) else 1
    st_spec = pl.BlockSpec((spb, nl, 2, H_A, DK_A, DV_A), lambda i: (i, layer, 0, 0, 0, 0))
    in_specs = [blk(256), blk(256), blk(256), blk(256), blk(W_A), _layer_spec(g4.shape, layer)]
    args = [qk, v, la, bsum, gate, g4]
    out_specs = [blk(W_A)]
    out_shape = [jax.ShapeDtypeStruct((n, W_A), BF16)]
    aliases = {}
    if has_s0:
        in_specs.append(st_spec)
        args.append(state_in)
    else:
        out_specs.append(st_spec)
        out_shape.append(jax.ShapeDtypeStruct((nb, DEPTH, 2, H_A, DK_A, DV_A), F32))
        if state_out is not None:
            aliases[len(args)] = 1
            in_specs.append(_ANY)
            args.append(state_out)
    return pl.pallas_call(
        functools.partial(_gla_kernel, seq=seq, has_s0=has_s0, alias_in=len(aliases), spb=spb),
        grid=(nb // spb,), in_specs=in_specs, out_specs=out_specs, out_shape=out_shape,
        input_output_aliases=aliases,
        scratch_shapes=[pltpu.VMEM((spb * seq, W_A), F32), pltpu.VMEM((2, ST_R, ST_C), F32),
                        pltpu.VMEM((ST_C, ST_R), F32)],
        compiler_params=_cparams(), name="gla_state" if has_s0 else "gla_ctx",
    )(*args)


def _softmax_t_pv(st, vt, ones_rows):
    dv, keys = vt.shape
    m = _col_reduce(st, jnp.max)
    e = jnp.exp2(st - m)
    if not ones_rows:
        l = _col_reduce(e, jnp.sum)
        return _dot(vt, e.astype(BF16)) * (1.0 / l)
    o = _dot(jnp.concatenate([vt, jnp.ones((16, keys), BF16)], axis=0), e.astype(BF16))
    return o[:dv] * (1.0 / o[dv:dv + 1])


def _pipelined_attention(score_fns, value_fns, depth):
    outs = []
    pending = [fn() for fn in score_fns[:depth]]
    for j, vfn in enumerate(value_fns):
        st = pending.pop(0)
        if j + depth < len(score_fns):
            pending.append(score_fns[j + depth]())
        outs.append(_softmax_t_pv(st, *vfn()))
    return outs


def _lookahead(n_keys):
    return 2 if n_keys >= 1024 else 3


def _col_reduce(x, op):
    rows, cols = x.shape
    part = 128 if rows % 128 == 0 and rows > 128 else rows
    if part != rows:
        x = op(x.reshape(rows // part, part, cols), axis=0)
    return op(x, axis=0, keepdims=True)


def _seqs_per_step(n, seq, ctx_len):
    if ctx_len or seq >= ATT_QBLOCK:
        return 1
    return max(1, min(n // seq, ATT_GROUP_ROWS // seq))


def _for_query_blocks(run, seq, spb):
    qb = min(ATT_QBLOCK, seq)
    nq = seq // qb
    if spb * nq <= ATT_STATIC_BLOCKS:
        run([(bb, slice(bb * seq + j * qb, bb * seq + (j + 1) * qb)) for bb in range(spb) for j in range(nq)])
    else:
        assert spb == 1

        def body(i, carry):
            run([(0, pl.ds(pl.multiple_of(i * qb, qb), qb))])
            return carry
        lax.fori_loop(0, nq, body, 0)


def _mla_part(q_ref, k_ref, vt_ref, gate_ref, ck_ref, cvt_ref, ob_ref, kk_sc, vt_sc, *, seq, ctx_len, spb):
    for bb in range(spb):
        rs = slice(bb * seq, (bb + 1) * seq)
        for p in range(H_B // 2):
            kk_sc[bb, p, 0:seq, 0:128] = k_ref[rs, 128 * p:128 * p + 128]
            kk_sc[bb, p, 0:seq, 128:256] = k_ref[rs, 512:640]
            if ctx_len:
                kk_sc[bb, p, seq:seq + ctx_len, 0:128] = ck_ref[0, 0, :, 128 * p:128 * p + 128]
                kk_sc[bb, p, seq:seq + ctx_len, 128:256] = ck_ref[0, 0, :, 512:640]
        vt_sc[bb, :, 0:seq] = vt_ref[:, rs]
        if ctx_len:
            vt_sc[bb, :, seq:seq + ctx_len] = cvt_ref[0, 0]

    def scores(bb, rows, h):
        p, hh = divmod(h, 2)
        qn = q_ref[rows, 128 * p:128 * p + 128] * _lane_mask(128, DN_B * hh, DN_B, BF16)
        qp = (q_ref[rows, 512 + 128 * (h // 4):512 + 128 * (h // 4) + 128]
              * _lane_mask(128, DR_B * (h % 4), DR_B, BF16))
        return _dot_nt(kk_sc[bb, p], jnp.concatenate([qn, qp], axis=-1))

    def items(blocks):
        its = [(bb, rows, h) for bb, rows in blocks for h in range(H_B)]

        def finish(outs):
            for j, (_, rows) in enumerate(blocks):
                ob = jnp.concatenate(outs[H_B * j:H_B * (j + 1)], axis=0).T
                ob_ref[rows, :] = (ob * gate_ref[rows, :]).astype(BF16)

        return ([functools.partial(scores, bb, rows, h) for bb, rows, h in its],
                [functools.partial(lambda bb, h: (vt_sc[bb, DV_B * h:DV_B * (h + 1), :], not ctx_len), bb, h)
                 for bb, _, h in its],
                finish)

    return items


def _diff_part(qk_ref, vt_ref, gate_ref, lam_ref, g_ref, ck_ref, cv_ref, oc_ref, k_sc, vt_sc, *,
               seq, ctx_len, lam_init, spb):
    lam = (jnp.exp(jnp.sum(lam_ref[0, 0:1, :] * lam_ref[0, 1:2, :], axis=-1, keepdims=True))
           - jnp.exp(jnp.sum(lam_ref[0, 2:3, :] * lam_ref[0, 3:4, :], axis=-1, keepdims=True)) + lam_init)
    dh = 2 * DC
    for bb in range(spb):
        rs = slice(bb * seq, (bb + 1) * seq)
        for p in range(H_C // 2):
            k_sc[bb, p, 0:seq, :] = qk_ref[rs, 256 + 128 * p:256 + 128 * p + 128]
            if ctx_len:
                pair_t = jnp.concatenate([ck_ref[0, 0, 2 * p], ck_ref[0, 0, 2 * p + 1]], axis=0)
                k_sc[bb, p, seq:seq + ctx_len, :] = pair_t.T.astype(BF16)
        vt_sc[bb, :, 0:seq] = vt_ref[:, rs]
        if ctx_len:
            for h in range(H_C):
                vt_sc[bb, dh * h:dh * (h + 1), seq:seq + ctx_len] = cv_ref[0, 0, h].astype(BF16)

    def scores(bb, rows, h, comp):
        p, hh = divmod(h, 2)
        qm = qk_ref[rows, 128 * p:128 * p + 128] * _lane_mask(128, dh * hh + DC * comp, DC, BF16)
        return _dot_nt(k_sc[bb, p], qm)

    def items(blocks):
        its = [(bb, rows, h, comp) for bb, rows in blocks for h in range(H_C) for comp in range(2)]

        def finish(o12):
            for j, (_, rows) in enumerate(blocks):
                outs = []
                for h in range(H_C):
                    o1, o2 = o12[2 * (H_C * j + h)], o12[2 * (H_C * j + h) + 1]
                    ot = o1 - lam * o2
                    outs.append(ot * lax.rsqrt(jnp.mean(ot * ot, axis=0, keepdims=True) + EPS))
                oc = jnp.concatenate(outs, axis=0).T
                oc_ref[rows, :] = (oc * g_ref[0] * (1.0 - lam_init) * gate_ref[rows, :]).astype(BF16)

        return ([functools.partial(scores, *item) for item in its],
                [functools.partial(lambda bb, h: (vt_sc[bb, dh * h:dh * (h + 1), :], True), bb, h)
                 for bb, _, h, _ in its],
                finish)

    return items


def _attn_kernel(*refs, seq, ctx_len, lam_init, spb, mla, diff):
    it = iter(refs)
    n_ctx = 2 if ctx_len else 0
    mla_in = [next(it) for _ in range(4 + n_ctx)] + [None] * (2 - n_ctx) if mla else None
    diff_in = [next(it) for _ in range(5 + n_ctx)] + [None] * (2 - n_ctx) if diff else None
    ob_ref = next(it) if mla else None
    oc_ref = next(it) if diff else None
    parts = []
    if mla:
        parts.append(_mla_part(*mla_in, ob_ref, next(it), next(it), seq=seq, ctx_len=ctx_len, spb=spb))
    if diff:
        parts.append(_diff_part(*diff_in, oc_ref, next(it), next(it), seq=seq, ctx_len=ctx_len,
                                lam_init=lam_init, spb=spb))

    def run(blocks):
        built = [part(blocks) for part in parts]
        outs = _pipelined_attention([f for b in built for f in b[0]], [f for b in built for f in b[1]],
                                    _lookahead(seq + ctx_len))
        lo = 0
        for score_fns, _, finish in built:
            finish(outs[lo:lo + len(score_fns)])
            lo += len(score_fns)

    _for_query_blocks(run, seq, spb)


def _attn_call(mq, mk, mvt, gate_b, dqk, dvt, gate_c, lamp, g4, ctx, layer, *, seq):
    n = mq.shape[0]
    ctx_len = 0 if ctx is None else ctx["mla_k"].shape[2]
    lam_init = 0.8 - 0.6 * math.exp(-0.3 * layer)
    spb = _seqs_per_step(n, seq, ctx_len)
    rows = spb * seq
    tk = seq + ctx_len
    rblk = lambda w: pl.BlockSpec((rows, w), lambda i: (i, 0))
    tblk = lambda w: pl.BlockSpec((w, rows), lambda i: (0, i))
    mla_specs, mla_args = [rblk(768), rblk(640), tblk(W_B), rblk(W_B)], [mq, mk, mvt, gate_b]
    diff_specs = [rblk(512), tblk(W_C), rblk(W_C), _layer_spec(lamp.shape, layer), _layer_spec(g4.shape, layer)]
    diff_args = [dqk, dvt, gate_c, lamp, g4]
    if ctx_len:
        mla_specs += [pl.BlockSpec((1, 1, ctx_len, 640), lambda i: (layer, i, 0, 0)),
                      pl.BlockSpec((1, 1, W_B, ctx_len), lambda i: (layer, i, 0, 0))]
        mla_args += [ctx["mla_k"], ctx["mla_vt"]]
        diff_specs += [pl.BlockSpec((1, 1, H_C, 2 * DC, ctx_len), lambda i: (i, layer, 0, 0, 0))] * 2
        diff_args += [ctx["diff_k_t"], ctx["diff_v_t"]]
    mla_out = (rblk(W_B), jax.ShapeDtypeStruct((n, W_B), BF16))
    diff_out = (rblk(W_C), jax.ShapeDtypeStruct((n, W_C), BF16))
    mla_scratch = [pltpu.VMEM((spb, H_B // 2, tk, 256), BF16), pltpu.VMEM((spb, W_B, tk), BF16)]
    diff_scratch = [pltpu.VMEM((spb, H_C // 2, tk, 128), BF16), pltpu.VMEM((spb, W_C, tk), BF16)]

    def call(mla, diff, name):
        outs = ([mla_out] if mla else []) + ([diff_out] if diff else [])
        return pl.pallas_call(
            functools.partial(_attn_kernel, seq=seq, ctx_len=ctx_len, lam_init=lam_init, spb=spb,
                              mla=mla, diff=diff),
            grid=(n // rows,),
            in_specs=(mla_specs if mla else []) + (diff_specs if diff else []),
            out_specs=[o[0] for o in outs], out_shape=[o[1] for o in outs],
            scratch_shapes=(mla_scratch if mla else []) + (diff_scratch if diff else []),
            compiler_params=_cparams(), name=name,
        )(*(mla_args if mla else []), *(diff_args if diff else []))

    if ctx_len:
        (ob,) = call(True, False, "mla_ctx")
        (oc,) = call(False, True, "diff_ctx")
        return ob, oc
    return call(True, True, "attn_self")


def _post_kernel(oa_ref, ob_ref, oc_ref, x_ref, mod_ref, w_ref, g_ref, y_ref, *, layer):
    mix = jnp.concatenate([oa_ref[...], ob_ref[...], oc_ref[...]], axis=-1)
    y_ref[...] = x_ref[...] + _rms(_dot(mix, w_ref[0]),
                                   _layer_row(g_ref, layer) * mod_ref[0, :, 2 * D_MODEL:3 * D_MODEL])


def _post_call(oa, ob, oc, x2d, mod, pw, layer, *, seq, sample):
    n = x2d.shape[0]
    tm = min(POST_BLOCK, seq if sample else n)
    blk = lambda w: pl.BlockSpec((tm, w), lambda i: (i, 0))

    def mod_idx(i):
        return (layer * MOD_ROWS + ((i * tm) // seq + 1 if sample else 0), 0, 0)

    return pl.pallas_call(
        functools.partial(_post_kernel, layer=layer),
        grid=(n // tm,),
        in_specs=[blk(W_A), blk(W_B), blk(W_C), blk(D_MODEL),
                  pl.BlockSpec((1, 1, 3 * D_MODEL), mod_idx),
                  _layer_spec(pw["w_out"].shape, layer), _layer_spec(pw["g_post"].shape, layer)],
        out_specs=blk(D_MODEL),
        out_shape=jax.ShapeDtypeStruct((n, D_MODEL), F32),
        compiler_params=_cparams(), name="post",
    )(oa, ob, oc, x2d, mod, pw["w_out"], pw["g_post"])


def _pack_params(g_pre, g_post, w_in, w_gla_af, b_gla_af, w_gla_ab, b_gla_ab, g_gla, g_mla_q, w_mla_uq,
                 g_mla_kv, w_mla_ukv, lam_q1, lam_k1, lam_q2, lam_k2, g_diff, w_out):
    w_t = jnp.swapaxes(w_in, 1, 2)
    assert w_t.shape[1] == R_DIFF[0][1]
    zg = jnp.zeros((DEPTH, GLA_LR, 128), F32)
    w_gate = jnp.concatenate([jnp.zeros((DEPTH, DR_B, 256), F32),
                              jnp.concatenate([w_gla_af, zg], axis=-1),
                              jnp.concatenate([zg, w_gla_ab], axis=-1),
                              jnp.zeros((DEPTH, 128 - DR_B - 2 * GLA_LR, 256), F32)], axis=1).astype(BF16)
    uq = w_mla_uq.reshape(DEPTH, Q_LORA, H_B, DN_B + DR_B)
    w_pe = uq[..., DN_B:].reshape(DEPTH, Q_LORA, H_B * DR_B)
    w_uq = jnp.concatenate([uq[..., :DN_B].reshape(DEPTH, Q_LORA, H_B * DN_B), w_pe], axis=-1).astype(BF16)
    ukv = w_mla_ukv.reshape(DEPTH, KV_LORA, H_B, DN_B + DV_B)
    w_ukv = jnp.concatenate([ukv[..., :DN_B].reshape(DEPTH, KV_LORA, H_B * DN_B),
                             ukv[..., DN_B:].reshape(DEPTH, KV_LORA, H_B * DV_B)], axis=-1).astype(BF16)
    row = lambda a: a.reshape(DEPTH, 1, a.shape[-1])
    return dict(
        w_t=w_t.astype(BF16), w_gate=w_gate,
        b_gate=row(jnp.concatenate([b_gla_af, b_gla_ab], axis=-1)),
        w_uq=w_uq, w_ukv=w_ukv, w_out=w_out.astype(BF16),
        g_pre=g_pre, g_post=g_post, g_mla_q=g_mla_q, g_mla_kv=g_mla_kv,
        g_gla4=row(jnp.tile(g_gla, (1, H_A))), g_diff4=row(jnp.tile(g_diff, (1, H_C))),
        lam=jnp.stack([lam_q1, lam_k1, lam_q2, lam_k2], axis=1))


def _rope_tables(n):
    t = np.arange(n)
    row = (t // GRID_W).astype(np.float32)
    col = (t % GRID_W).astype(np.float32)
    half = ROPE_DIM // 2
    inv = (1.0 / (np.float32(ROPE_THETA) ** (np.arange(0, half, 2, dtype=np.float32) / np.float32(half)))
           ).astype(np.float32)
    ar = row[:, None] * inv
    ac = col[:, None] * inv
    ang = np.concatenate([ar, ar, ac, ac], axis=-1).astype(np.float32)
    return (jnp.asarray(np.tile(np.cos(ang), (1, 8)).astype(np.float32)),
            jnp.asarray(np.tile(np.sin(ang), (1, 8)).astype(np.float32)))


def _sublayer(x2d, mod, pw, layer, *, seq, rope_tabs, ctx, caches):
    sample = ctx is not None
    pre = _pre_call(x2d, mod, pw, layer, seq=seq, rope_tabs=rope_tabs,
                    caches=None if sample else caches[:4])
    qk, v, la, bsum, ga, gb, gc, mq, mk, mvt, dqk, dvt = pre[:12]
    if sample:
        (oa,) = _gla_call(qk, v, la, bsum, ga, pw["g_gla4"], layer, seq=seq, state_in=ctx["state"])
        new_caches = None
    else:
        oa, sfin = _gla_call(qk, v, la, bsum, ga, pw["g_gla4"], layer, seq=seq,
                             state_out=caches[4] if caches else None)
        new_caches = tuple(pre[12:]) + (sfin,)
    ob, oc = _attn_call(mq, mk, mvt, gb, dqk, dvt, gc, pw["lam"], pw["g_diff4"], ctx, layer, seq=seq)
    y = _post_call(oa, ob, oc, x2d, mod, pw, layer, seq=seq, sample=sample)
    return y, new_caches


def kernel(x_prompt, x_sample, c, cache_mla_ckv, cache_mla_kpe, cache_diff_k, cache_diff_v, state_gla,
           c_ctx, w_ada, b_ada, g_pre, g_post, w_in, w_gla_af, b_gla_af, w_gla_ab, b_gla_ab, g_gla,
           g_mla_q, w_mla_uq, g_mla_kv, w_mla_ukv, lam_q1, lam_k1, lam_q2, lam_k2, g_diff, w_out):
    bp, tp, d = x_prompt.shape
    bs, ts, _ = x_sample.shape

    pw = _pack_params(g_pre, g_post, w_in, w_gla_af, b_gla_af, w_gla_ab, b_gla_ab, g_gla, g_mla_q,
                      w_mla_uq, g_mla_kv, w_mla_ukv, lam_q1, lam_k1, lam_q2, lam_k2, g_diff, w_out)
    mod = _mod_call(c_ctx.reshape(1, d), c, w_ada, b_ada)
    rope_tabs = _rope_tables(ts)
    ctx_k, ctx_vt = _ctxkv_call(cache_mla_ckv, jnp.swapaxes(cache_mla_kpe, -1, -2), pw["w_ukv"])
    ctx = dict(state=state_gla, mla_k=ctx_k, mla_vt=ctx_vt,
               diff_k_t=jnp.swapaxes(cache_diff_k, -1, -2), diff_v_t=jnp.swapaxes(cache_diff_v, -1, -2))

    y_p = x_prompt.reshape(bp * tp, d)
    y_s = x_sample.reshape(bs * ts, d)
    caches = ()
    for l in range(DEPTH):
        y_p, caches = _sublayer(y_p, mod, pw, l, seq=tp, rope_tabs=None, ctx=None, caches=caches)
        y_s, _ = _sublayer(y_s, mod, pw, l, seq=ts, rope_tabs=rope_tabs, ctx=ctx, caches=None)
    ckvn, kpe_t, kc_t, vc_t, new_state = caches
    return (y_p.reshape(bp, tp, d), y_s.reshape(bs, ts, d), ckvn, jnp.swapaxes(kpe_t, -1, -2),
            jnp.swapaxes(kc_t, -1, -2), jnp.swapaxes(vc_t, -1, -2), new_state)
```

```python
import functools
import math

import numpy as np
import jax
import jax.numpy as jnp
from jax import lax
from jax.experimental import pallas as pl
from jax.experimental.pallas import tpu as pltpu

F32 = jnp.float32
BF16 = jnp.bfloat16

D_MODEL = 1024
DEPTH = 2
GRID_W = 64
EPS = 1e-6
ROPE_THETA = 10000.0
ROPE_DIM = 32
H_A, DK_A, DV_A = 4, 32, 64
GLA_LR = 16
GLA_TAU = 16.0
GLA_CHUNK = 64
H_B, DN_B, DR_B, DV_B = 8, 64, 32, 64
Q_LORA, KV_LORA = 256, 128
H_C, DC = 4, 32
W_A, W_B, W_C = H_A * DV_A, H_B * DV_B, H_C * 2 * DC
ST_R, ST_C = H_A * DV_A, H_A * DK_A
LOG2E = math.log2(math.e)
SUBLANES = 8

R_QKV = ((0, 512),)
R_GG = ((544, 800),)
R_MLA = ((800, 1216), (512, 544), 64)
R_MG = ((1216, 1728),)
R_DIFF = ((1728, 2752),)
MOD_ROWS = 8

V7X_VMEM_LIMIT_BYTES = 56 * 1024 * 1024
TOKEN_BLOCK = 1024
POST_BLOCK = 1024
ATT_QBLOCK = 512
ATT_GROUP_ROWS = 1024
GLA_GROUP_ROWS = 1024
ATT_STATIC_BLOCKS = 4


def _cparams(n_axes=1):
    return pltpu.CompilerParams(dimension_semantics=("arbitrary",) * n_axes,
                                vmem_limit_bytes=V7X_VMEM_LIMIT_BYTES)


def _rms(x, g):
    return x * lax.rsqrt(jnp.mean(x * x, axis=-1, keepdims=True) + EPS) * g


def _silu(x):
    return x * jax.nn.sigmoid(x)


def _log_sigmoid(x):
    return jnp.minimum(x, 0.0) - jnp.log1p(jnp.exp(-jnp.abs(x)))


def _rope(z, cos, sin):
    w = z.shape[-1]
    lane = lax.broadcasted_iota(jnp.int32, z.shape, 1)
    rot = jnp.where((lane & 15) < 8, -pltpu.roll(z, w - 8, 1), pltpu.roll(z, 8, 1))
    return z * cos + rot * sin


def _lane_mask(width, lo, size, dtype):
    lane = lax.broadcasted_iota(jnp.int32, (1, width), 1)
    return jnp.where((lane >= lo) & (lane < lo + size), 1.0, 0.0).astype(dtype)


def _dot(a, b):
    return jnp.dot(a, b, preferred_element_type=F32)


def _dot_nt(a, b):
    return lax.dot_general(a, b, (((1,), (1,)), ((), ())), preferred_element_type=F32)


def _dot_tn(a, b):
    return lax.dot_general(a, b, (((0,), (0,)), ((), ())), preferred_element_type=F32)


def _layer_spec(shape, layer):
    nd = len(shape)
    if nd == 2:
        return pl.BlockSpec(tuple(shape), lambda *_: (0, 0))
    return pl.BlockSpec((1,) + tuple(shape[1:]), lambda *_: (layer,) + (0,) * (nd - 1))


def _layer_row(ref, layer):
    return ref[layer:layer + 1, :]


_ANY = pl.BlockSpec(memory_space=pl.ANY)


def _mod_kernel(cctx_ref, c_ref, w_ref, b_ref, o_ref, rows_sc):
    nc = c_ref.shape[0]
    rows_sc[0:1, :] = cctx_ref[...]
    rows_sc[1:1 + nc, :] = c_ref[...]
    if 1 + nc < MOD_ROWS:
        rows_sc[1 + nc:, :] = jnp.zeros((MOD_ROWS - 1 - nc, D_MODEL), F32)
    s = _silu(rows_sc[...]).astype(BF16)
    layer = pl.program_id(0)
    bias = b_ref[0:1, :]
    for r in range(1, DEPTH):
        bias = jnp.where(layer == r, b_ref[r:r + 1, :], bias)
    mod = _dot(s, w_ref[0].astype(BF16)) + bias
    for r in range(MOD_ROWS):
        o_ref[r] = mod[r:r + 1, :]


def _mod_call(c_ctx, c, w_ada, b_ada):
    nb = 3 * D_MODEL // 2
    assert 1 + c.shape[0] <= MOD_ROWS
    return pl.pallas_call(
        _mod_kernel,
        grid=(DEPTH, 3 * D_MODEL // nb),
        in_specs=[pl.BlockSpec((1, D_MODEL), lambda l, j: (0, 0)),
                  pl.BlockSpec(c.shape, lambda l, j: (0, 0)),
                  pl.BlockSpec((1, D_MODEL, nb), lambda l, j: (l, 0, j)),
                  pl.BlockSpec((DEPTH, nb), lambda l, j: (0, j))],
        out_specs=pl.BlockSpec((MOD_ROWS, 1, nb), lambda l, j: (l, 0, j)),
        out_shape=jax.ShapeDtypeStruct((DEPTH * MOD_ROWS, 1, 3 * D_MODEL), F32),
        scratch_shapes=[pltpu.VMEM((MOD_ROWS, D_MODEL), F32)],
        compiler_params=_cparams(2), name="adaln_mod",
    )(c_ctx, c, w_ada, b_ada)


def _ctxkv_kernel(ckv_ref, kpe_ref, w_ref, k_ref, vt_ref):
    for b in range(ckv_ref.shape[0]):
        kv = _dot(ckv_ref[b, 0].astype(BF16), w_ref[0])
        kpe4 = jnp.concatenate([kpe_ref[b, 0]] * 4, axis=0).T
        k_ref[0, b, :, 0:512] = kv[:, 0:512].astype(BF16)
        k_ref[0, b, :, 512:640] = kpe4.astype(BF16)
        vt_ref[0, b] = kv[:, 512:1024].T.astype(BF16)


def _ctxkv_call(cache_ckv, cache_kpe_t, wukv):
    nb, _, tc, _ = cache_ckv.shape
    return pl.pallas_call(
        _ctxkv_kernel,
        grid=(DEPTH,),
        in_specs=[pl.BlockSpec((nb, 1, tc, KV_LORA), lambda l: (0, l, 0, 0)),
                  pl.BlockSpec((nb, 1, DR_B, tc), lambda l: (0, l, 0, 0)),
                  pl.BlockSpec((1, KV_LORA, 1024), lambda l: (l, 0, 0))],
        out_specs=[pl.BlockSpec((1, nb, tc, 640), lambda l: (l, 0, 0, 0)),
                   pl.BlockSpec((1, nb, W_B, tc), lambda l: (l, 0, 0, 0))],
        out_shape=[jax.ShapeDtypeStruct((DEPTH, nb, tc, 640), BF16),
                   jax.ShapeDtypeStruct((DEPTH, nb, W_B, tc), BF16)],
        compiler_params=_cparams(), name="mla_ctx_kv",
    )(cache_ckv, cache_kpe_t, wukv)


def _pre_kernel(*refs, rope, ctx_out, alias_in, bpb, seq, layer):
    it = iter(refs)
    (x_ref, mod_ref, gpre_ref, w_ref, wg_ref, bg_ref, gq_ref, gkv_ref, wuq_ref,
     wukv_ref) = (next(it) for _ in range(10))
    if rope:
        cos_ref, sin_ref = next(it), next(it)
    for _ in range(alias_in):
        next(it)
    (qk_ref, v_ref, la_ref, bs_ref, ga_ref, gb_ref, gc_ref, mq_ref, mk_ref, mvt_ref, dqk_ref,
     dvt_ref) = (next(it) for _ in range(12))
    if ctx_out:
        ckvn_ref, kpe_ref, kc_ref, vc_ref = (next(it) for _ in range(4))

    d = D_MODEL
    shift = mod_ref[0, :, 0:d]
    scale = mod_ref[0, :, d:2 * d]
    h = (_rms(x_ref[...], _layer_row(gpre_ref, layer) * (1.0 + scale)) + shift).astype(BF16)
    def proj(group):
        parts = [jnp.zeros((r, d), BF16) if isinstance(r, int) else w_ref[0, r[0]:r[1], :] for r in group]
        return _dot_nt(h, parts[0] if len(parts) == 1 else jnp.concatenate(parts, axis=0))
    if rope:
        cos = cos_ref[...]
        sin = sin_ref[...]

    pg = proj(R_QKV)
    qk_ref[:, 0:128] = pg[:, 0:128] * (DK_A ** -0.5)
    qk_ref[:, 128:256] = pg[:, 128:256]
    v_ref[...] = pg[:, 256:512]
    pm = proj(R_MLA)
    tail = pm[:, 384:512]
    xg = _dot(tail.astype(BF16), wg_ref[0]) + bg_ref[0]
    la = _log_sigmoid(xg) * (1.0 / GLA_TAU)
    la_ref[...] = la
    bs_ref[:, 0:128] = _chunk_scan(la[:, 0:128], False)
    bs_ref[:, 128:256] = _chunk_scan(la[:, 128:256], True)
    ga_ref[...] = _silu(proj(R_GG)).astype(BF16)

    qall = _dot(_rms(pm[:, 0:256], _layer_row(gq_ref, layer)).astype(BF16), wuq_ref[0])
    q_pe = qall[:, 512:768]
    if rope:
        q_pe = _rope(q_pe, cos, sin)
    sb = (DN_B + DR_B) ** -0.5 * LOG2E
    mq_ref[:, 0:512] = (qall[:, 0:512] * sb).astype(BF16)
    mq_ref[:, 512:768] = (q_pe * sb).astype(BF16)
    ckvn = _rms(pm[:, 256:384], _layer_row(gkv_ref, layer))
    kvall = _dot(ckvn.astype(BF16), wukv_ref[0])
    lane = lax.broadcasted_iota(jnp.int32, tail.shape, 1)
    kpe4 = jnp.where(lane < DR_B, tail, 0.0)
    kpe4 = kpe4 + pltpu.roll(kpe4, DR_B, 1)
    kpe4 = kpe4 + pltpu.roll(kpe4, 2 * DR_B, 1)
    if rope:
        kpe4 = _rope(kpe4, cos[:, 0:128], sin[:, 0:128])
    mk_ref[:, 0:512] = kvall[:, 0:512].astype(BF16)
    mk_ref[:, 512:640] = kpe4.astype(BF16)
    mvt_ref[...] = kvall[:, 512:1024].T.astype(BF16)
    gb_ref[...] = _silu(proj(R_MG)).astype(BF16)

    pd = proj(R_DIFF)
    dq, dk, dv = pd[:, 0:256], pd[:, 256:512], pd[:, 512:768]
    if rope:
        dq = _rope(dq, cos, sin)
        dk = _rope(dk, cos, sin)
    dqk_ref[:, 0:256] = (dq * (DC ** -0.5 * LOG2E)).astype(BF16)
    dqk_ref[:, 256:512] = dk.astype(BF16)
    dv_t = dv.T
    dvt_ref[...] = dv_t.astype(BF16)
    gc_ref[...] = _silu(pd[:, 768:1024]).astype(BF16)
    if ctx_out:
        kpe_t = kpe4.T
        dk_t = dk.T
        for ref in () if alias_in else (ckvn_ref, kpe_ref, kc_ref, vc_ref):
            ref[:, 1:] = jnp.zeros(ref[:, 1:].shape, F32)
        for bb in range(bpb):
            rs = slice(bb * seq, (bb + 1) * seq)
            ckvn_ref[bb, 0] = ckvn[rs]
            kpe_ref[bb, 0] = kpe_t[0:DR_B, rs]
            kc_ref[bb, 0] = dk_t[:, rs].reshape(H_C, 2 * DC, seq)
            vc_ref[bb, 0] = dv_t[:, rs].reshape(H_C, 2 * DC, seq)


def _pre_call(x2d, mod, pw, layer, *, seq, rope_tabs, caches):
    n = x2d.shape[0]
    tm = min(TOKEN_BLOCK, n)
    bpb = max(tm // seq, 1)
    rope = rope_tabs is not None
    ctx_out = caches is not None
    steps_per_seq = max(seq // tm, 1)
    nbt = n // seq

    def mod_idx(i):
        return (layer * MOD_ROWS + ((i * tm) // seq + 1 if rope else 0), 0, 0)

    names = ["g_pre", "w_t", "w_gate", "b_gate", "g_mla_q", "g_mla_kv", "w_uq", "w_ukv"]
    in_specs = [pl.BlockSpec((tm, D_MODEL), lambda i: (i, 0)), pl.BlockSpec((1, 1, 3 * D_MODEL), mod_idx)]
    in_specs += [_layer_spec(pw[k].shape, layer) for k in names]
    args = [x2d, mod] + [pw[k] for k in names]
    if rope:
        in_specs += [pl.BlockSpec((tm, 256), lambda i: (i % steps_per_seq, 0))] * 2
        args += list(rope_tabs)
    outs = [(256, F32, False), (256, F32, False), (256, F32, False), (256, F32, False),
            (W_A, BF16, False), (W_B, BF16, False), (W_C, BF16, False),
            (768, BF16, False), (640, BF16, False), (W_B, BF16, True),
            (512, BF16, False), (W_C, BF16, True)]
    out_specs = [pl.BlockSpec((w, tm), lambda i: (0, i)) if tr else pl.BlockSpec((tm, w), lambda i: (i, 0))
                 for w, _, tr in outs]
    out_shape = [jax.ShapeDtypeStruct((w, n) if tr else (n, w), dt) for w, dt, tr in outs]
    widths = outs
    aliases = {}
    if ctx_out:
        assert caches or layer == 0
        nl = 1 if caches else DEPTH
        out_specs += [pl.BlockSpec((bpb, nl, seq, KV_LORA), lambda i: (i, layer, 0, 0)),
                      pl.BlockSpec((bpb, nl, DR_B, seq), lambda i: (i, layer, 0, 0)),
                      pl.BlockSpec((bpb, nl, H_C, 2 * DC, seq), lambda i: (i, layer, 0, 0, 0)),
                      pl.BlockSpec((bpb, nl, H_C, 2 * DC, seq), lambda i: (i, layer, 0, 0, 0))]
        out_shape += [jax.ShapeDtypeStruct((nbt, DEPTH, seq, KV_LORA), F32),
                      jax.ShapeDtypeStruct((nbt, DEPTH, DR_B, seq), F32),
                      jax.ShapeDtypeStruct((nbt, DEPTH, H_C, 2 * DC, seq), F32),
                      jax.ShapeDtypeStruct((nbt, DEPTH, H_C, 2 * DC, seq), F32)]
        for j, arr in enumerate(caches):
            aliases[len(args)] = len(widths) + j
            in_specs.append(_ANY)
            args.append(arr)
    return pl.pallas_call(
        functools.partial(_pre_kernel, rope=rope, ctx_out=ctx_out, alias_in=len(aliases), bpb=bpb, seq=seq,
                          layer=layer),
        grid=(n // tm,), in_specs=in_specs, out_specs=out_specs, out_shape=out_shape,
        input_output_aliases=aliases,
        compiler_params=_cparams(), name="pre_rope" if rope else "pre_ctx",
    )(*args)


_GLA_LEVELS = (1, 2, 4, 8, 16, 32)


def _gla_consts(rev):
    c = GLA_CHUNK
    row = lax.broadcasted_iota(jnp.int32, (c, 128), 0)
    pos = (c - 1 - row) if rev else row
    ri = lax.broadcasted_iota(jnp.int32, (c, H_A * c), 0)
    cj = lax.broadcasted_iota(jnp.int32, (c, H_A * c), 1) & (c - 1)
    pi = (c - 1 - ri) if rev else ri
    pj = (c - 1 - cj) if rev else cj
    x = pi ^ pj
    lvl = jnp.where(pi == pj, 0, -1)
    for kbit in range(6):
        lvl = jnp.where((pj < pi) & ((x >> kbit) == 1), kbit + 1, lvl)
    return pos, lvl


def _chunk_scan(x, rev):
    rows = x.shape[0]
    nt = rows // SUBLANES
    tiles_per_chunk = GLA_CHUNK // SUBLANES
    x3 = x.reshape(nt, SUBLANES, 128)
    sub = lax.broadcasted_iota(jnp.int32, x3.shape, 1)
    tile = lax.broadcasted_iota(jnp.int32, x3.shape, 0) & (tiles_per_chunk - 1)
    edge = 0 if rev else SUBLANES - 1
    s = 1
    while s < SUBLANES:
        if rev:
            x3 = x3 + jnp.where(sub < SUBLANES - s, pltpu.roll(x3, SUBLANES - s, 1), 0.0)
        else:
            x3 = x3 + jnp.where(sub >= s, pltpu.roll(x3, s, 1), 0.0)
        s *= 2
    s = 1
    while s < tiles_per_chunk:
        tot = jnp.broadcast_to(x3[:, edge:edge + 1, :], x3.shape)
        if rev:
            shifted = jnp.concatenate([tot[s:], tot[:s]], axis=0)
            x3 = x3 + jnp.where(tile < tiles_per_chunk - s, shifted, 0.0)
        else:
            shifted = jnp.concatenate([tot[nt - s:], tot[:nt - s]], axis=0)
            x3 = x3 + jnp.where(tile >= s, shifted, 0.0)
        s *= 2
    return x3.reshape(rows, 128)


def _gla_scores(q, k, la, b, pos, lvl, hm_bf, rev):
    c = GLA_CHUNK
    prv = pltpu.roll(la, c - 1 if rev else 1, 0)
    nxt = pltpu.roll(la, 1 if rev else c - 1, 0)
    lvl = lvl.astype(BF16)
    s_tot = jnp.where(lvl == 0, _dot_nt(q.astype(BF16), jnp.concatenate([k.astype(BF16)] * H_A, 0) * hm_bf
                                        ).astype(BF16), jnp.zeros((), BF16))
    for kbit, m in enumerate(_GLA_LEVELS):
        up = ((pos >> kbit) & 1) == 1
        if m == 1:
            e = jnp.where(up, la, 0.0)
        elif m == 2:
            c4 = pos & 3
            e = jnp.where(c4 == 0, nxt, jnp.where(c4 == 1, 0.0, jnp.where(c4 == 2, la, la + prv)))
        else:
            nblk = c // (2 * m)
            loc = m if rev else m - 1
            b3 = b.reshape(nblk, 2 * m, 128)
            ref = jnp.broadcast_to(b3[:, loc:loc + 1, :], (nblk, 2 * m, 128)).reshape(c, 128)
            dlt = b - ref
            e = jnp.where(up, dlt, -dlt)
        xm = (jnp.where(up, q, k) * jnp.exp(e)).astype(BF16)
        sm = _dot_nt(xm, jnp.concatenate([xm] * H_A, 0) * hm_bf)
        s_tot = jnp.where(lvl == kbit + 1, sm.astype(BF16), s_tot)
    return s_tot


def _gla_apply(s_tot, q, k, v, b, st_prev, hm_f32, vm_bf, rev):
    c = GLA_CHUNK
    vbd = jnp.concatenate([v] * H_A, 0) * vm_bf
    blast = b[0:1, :] if rev else b[c - 1:c, :]
    qbar = (q * jnp.exp(b)).astype(BF16)
    kdec = (k * jnp.exp(blast - b)).astype(BF16)
    o = _dot(s_tot, vbd) + _dot_nt(qbar, st_prev.astype(BF16))
    st_new = st_prev * jnp.exp(blast) + _dot_tn(v, kdec) * hm_f32
    return o, st_new


def _gla_kernel(*refs, seq, has_s0, alias_in, spb):
    it = iter(refs)
    qk_ref, v_ref, la_ref, b_ref, gate_ref, g_ref = (next(it) for _ in range(6))
    s0_ref = next(it) if has_s0 else None
    for _ in range(alias_in):
        next(it)
    oa_ref = next(it)
    sfin_ref = None if has_s0 else next(it)
    acc_sc, st_sc, blk_sc = next(it), next(it), next(it)

    c = GLA_CHUNK
    nc = seq // c
    acc_sc[...] = jnp.zeros_like(acc_sc)
    if not has_s0 and not alias_in:
        sfin_ref[:, 1:] = jnp.zeros(sfin_ref[:, 1:].shape, F32)

    hrow = lax.broadcasted_iota(jnp.int32, (H_A * c, 128), 0) // c
    hm_f32 = jnp.where(hrow == lax.broadcasted_iota(jnp.int32, (H_A * c, 128), 1) // DK_A, 1.0, 0.0)
    hm_bf = hm_f32.astype(BF16)
    vrow = lax.broadcasted_iota(jnp.int32, (H_A * c, H_A * DV_A), 0) // c
    vm_bf = jnp.where(vrow == lax.broadcasted_iota(jnp.int32, (H_A * c, H_A * DV_A), 1) // DV_A,
                      1.0, 0.0).astype(BF16)
    consts = (_gla_consts(False), _gla_consts(True))

    def run_sequence(bb):
        base = bb * seq
        for d in range(2):
            if has_s0:
                blk_sc[...] = jnp.zeros_like(blk_sc)
                for hh in range(H_A):
                    blk_sc[DK_A * hh:DK_A * (hh + 1), DV_A * hh:DV_A * (hh + 1)] = s0_ref[bb, 0, d, hh]
                st_sc[d] = blk_sc[...].T
            else:
                st_sc[d] = jnp.zeros((ST_R, ST_C), F32)

        def chunk_rows(n, d):
            cn = (nc - 1 - n) if d else n
            start = base + cn * c
            return pl.ds(start if isinstance(start, int) else pl.multiple_of(start, c), c)

        def scores(n):
            out = []
            for d in range(2):
                rows = chunk_rows(n, d)
                pos, lvl = consts[d]
                out.append(_gla_scores(qk_ref[rows, 0:128], qk_ref[rows, 128:256],
                                       la_ref[rows, 128 * d:128 * d + 128],
                                       b_ref[rows, 128 * d:128 * d + 128], pos, lvl, hm_bf, bool(d)))
            return tuple(out)

        def apply(n, s_both):
            for d in range(2):
                rows = chunk_rows(n, d)
                o, st_new = _gla_apply(s_both[d], qk_ref[rows, 0:128], qk_ref[rows, 128:256],
                                       v_ref[rows, :].astype(BF16), b_ref[rows, 128 * d:128 * d + 128],
                                       st_sc[d], hm_f32, vm_bf, bool(d))
                acc_sc[rows, :] = acc_sc[rows, :] + o
                st_sc[d] = st_new

        def body(n, s_cur):
            s_next = scores(n + 1)
            apply(n, s_cur)
            return s_next

        trips = nc - 1
        s_last = lax.fori_loop(0, trips, body, scores(0),
                               unroll=next(u for u in (15, 5, 3, 1) if trips % u == 0))
        apply(nc - 1, s_last)

        rs = slice(base, base + seq)
        first = lax.broadcasted_iota(jnp.int32, (seq, 128), 1) < DV_A
        for p in range(H_A // 2):
            cols = slice(128 * p, 128 * (p + 1))
            oa = acc_sc[rs, cols]
            sq = oa * oa
            s0 = jnp.sum(jnp.where(first, sq, 0.0), axis=-1, keepdims=True)
            s1 = jnp.sum(jnp.where(first, 0.0, sq), axis=-1, keepdims=True)
            inv = jnp.where(first, lax.rsqrt(s0 * (1.0 / DV_A) + EPS), lax.rsqrt(s1 * (1.0 / DV_A) + EPS))
            oa_ref[rs, cols] = (oa * inv * g_ref[0, :, cols] * gate_ref[rs, cols]).astype(BF16)
        if not has_s0:
            for d in range(2):
                blk_sc[...] = st_sc[d].T
                for hh in range(H_A):
                    sfin_ref[bb, 0, d, hh] = blk_sc[DK_A * hh:DK_A * (hh + 1), DV_A * hh:DV_A * (hh + 1)]

    for bb in range(spb):
        run_sequence(bb)


def _gla_call(qk, v, la, bsum, gate, g4, layer, *, seq, state_in=None, state_out=None):
    n = qk.shape[0]
    nb = n // seq
    has_s0 = state_in is not None
    assert has_s0 or state_out is not None or layer == 0
    spb = max(1, min(nb, GLA_GROUP_ROWS // seq))
    blk = lambda w: pl.BlockSpec((spb * seq, w), lambda i: (i, 0))
    nl = DEPTH if (not has_s0 and state_out is None) else 1
    st_spec = pl.BlockSpec((spb, nl, 2, H_A, DK_A, DV_A), lambda i: (i, layer, 0, 0, 0, 0))
    in_specs = [blk(256), blk(256), blk(256), blk(256), blk(W_A), _layer_spec(g4.shape, layer)]
    args = [qk, v, la, bsum, gate, g4]
    out_specs = [blk(W_A)]
    out_shape = [jax.ShapeDtypeStruct((n, W_A), BF16)]
    aliases = {}
    if has_s0:
        in_specs.append(st_spec)
        args.append(state_in)
    else:
        out_specs.append(st_spec)
        out_shape.append(jax.ShapeDtypeStruct((nb, DEPTH, 2, H_A, DK_A, DV_A), F32))
        if state_out is not None:
            aliases[len(args)] = 1
            in_specs.append(_ANY)
            args.append(state_out)
    return pl.pallas_call(
        functools.partial(_gla_kernel, seq=seq, has_s0=has_s0, alias_in=len(aliases), spb=spb),
        grid=(nb // spb,), in_specs=in_specs, out_specs=out_specs, out_shape=out_shape,
        input_output_aliases=aliases,
        scratch_shapes=[pltpu.VMEM((spb * seq, W_A), F32), pltpu.VMEM((2, ST_R, ST_C), F32),
                        pltpu.VMEM((ST_C, ST_R), F32)],
        compiler_params=_cparams(), name="gla_state" if has_s0 else "gla_ctx",
    )(*args)


def _softmax_t_pv(st, vt, ones_rows):
    dv, keys = vt.shape
    m = _col_reduce(st, jnp.max)
    e = jnp.exp2(st - m)
    if not ones_rows:
        l = _col_reduce(e, jnp.sum)
        return _dot(vt, e.astype(BF16)) * (1.0 / l)
    o = _dot(jnp.concatenate([vt, jnp.ones((16, keys), BF16)], axis=0), e.astype(BF16))
    return o[:dv] * (1.0 / o[dv:dv + 1])


def _pipelined_attention(score_fns, value_fns, depth):
    outs = []
    pending = [fn() for fn in score_fns[:depth]]
    for j, vfn in enumerate(value_fns):
        st = pending.pop(0)
        if j + depth < len(score_fns):
            pending.append(score_fns[j + depth]())
        outs.append(_softmax_t_pv(st, *vfn()))
    return outs


def _lookahead(n_keys):
    return 2 if n_keys >= 1024 else 3


def _col_reduce(x, op):
    rows, cols = x.shape
    part = 128 if rows % 128 == 0 and rows > 128 else rows
    if part != rows:
        x = op(x.reshape(rows // part, part, cols), axis=0)
    return op(x, axis=0, keepdims=True)


def _seqs_per_step(n, seq, ctx_len):
    if ctx_len or seq >= ATT_QBLOCK:
        return 1
    return max(1, min(n // seq, ATT_GROUP_ROWS // seq))


def _for_query_blocks(run, seq, spb):
    qb = min(ATT_QBLOCK, seq)
    nq = seq // qb
    if spb * nq <= ATT_STATIC_BLOCKS:
        run([(bb, slice(bb * seq + j * qb, bb * seq + (j + 1) * qb)) for bb in range(spb) for j in range(nq)])
    else:
        assert spb == 1

        def body(i, carry):
            run([(0, pl.ds(pl.multiple_of(i * qb, qb), qb))])
            return carry
        lax.fori_loop(0, nq, body, 0)


def _mla_part(q_ref, k_ref, vt_ref, gate_ref, ck_ref, cvt_ref, ob_ref, kk_sc, vt_sc, *, seq, ctx_len, spb):
    for bb in range(spb):
        rs = slice(bb * seq, (bb + 1) * seq)
        for p in range(H_B // 2):
            kk_sc[bb, p, 0:seq, 0:128] = k_ref[rs, 128 * p:128 * p + 128]
            kk_sc[bb, p, 0:seq, 128:256] = k_ref[rs, 512:640]
            if ctx_len:
                kk_sc[bb, p, seq:seq + ctx_len, 0:128] = ck_ref[0, 0, :, 128 * p:128 * p + 128]
                kk_sc[bb, p, seq:seq + ctx_len, 128:256] = ck_ref[0, 0, :, 512:640]
        vt_sc[bb, :, 0:seq] = vt_ref[:, rs]
        if ctx_len:
            vt_sc[bb, :, seq:seq + ctx_len] = cvt_ref[0, 0]

    def scores(bb, rows, h):
        p, hh = divmod(h, 2)
        qn = q_ref[rows, 128 * p:128 * p + 128] * _lane_mask(128, DN_B * hh, DN_B, BF16)
        qp = (q_ref[rows, 512 + 128 * (h // 4):512 + 128 * (h // 4) + 128]
              * _lane_mask(128, DR_B * (h % 4), DR_B, BF16))
        return _dot_nt(kk_sc[bb, p], jnp.concatenate([qn, qp], axis=-1))

    def items(blocks):
        its = [(bb, rows, h) for bb, rows in blocks for h in range(H_B)]

        def finish(outs):
            for j, (_, rows) in enumerate(blocks):
                ob = jnp.concatenate(outs[H_B * j:H_B * (j + 1)], axis=0).T
                ob_ref[rows, :] = (ob * gate_ref[rows, :]).astype(BF16)

        return ([functools.partial(scores, bb, rows, h) for bb, rows, h in its],
                [functools.partial(lambda bb, h: (vt_sc[bb, DV_B * h:DV_B * (h + 1), :], not ctx_len), bb, h)
                 for bb, _, h in its],
                finish)

    return items


def _diff_part(qk_ref, vt_ref, gate_ref, lam_ref, g_ref, ck_ref, cv_ref, oc_ref, k_sc, vt_sc, *,
               seq, ctx_len, lam_init, spb):
    lam = (jnp.exp(jnp.sum(lam_ref[0, 0:1, :] * lam_ref[0, 1:2, :], axis=-1, keepdims=True))
           - jnp.exp(jnp.sum(lam_ref[0, 2:3, :] * lam_ref[0, 3:4, :], axis=-1, keepdims=True)) + lam_init)
    dh = 2 * DC
    for bb in range(spb):
        rs = slice(bb * seq, (bb + 1) * seq)
        for p in range(H_C // 2):
            k_sc[bb, p, 0:seq, :] = qk_ref[rs, 256 + 128 * p:256 + 128 * p + 128]
            if ctx_len:
                pair_t = jnp.concatenate([ck_ref[0, 0, 2 * p], ck_ref[0, 0, 2 * p + 1]], axis=0)
                k_sc[bb, p, seq:seq + ctx_len, :] = pair_t.T.astype(BF16)
        vt_sc[bb, :, 0:seq] = vt_ref[:, rs]
        if ctx_len:
            for h in range(H_C):
                vt_sc[bb, dh * h:dh * (h + 1), seq:seq + ctx_len] = cv_ref[0, 0, h].astype(BF16)

    def scores(bb, rows, h, comp):
        p, hh = divmod(h, 2)
        qm = qk_ref[rows, 128 * p:128 * p + 128] * _lane_mask(128, dh * hh + DC * comp, DC, BF16)
        return _dot_nt(k_sc[bb, p], qm)

    def items(blocks):
        its = [(bb, rows, h, comp) for bb, rows in blocks for h in range(H_C) for comp in range(2)]

        def finish(o12):
            for j, (_, rows) in enumerate(blocks):
                outs = []
                for h in range(H_C):
                    o1, o2 = o12[2 * (H_C * j + h)], o12[2 * (H_C * j + h) + 1]
                    ot = o1 - lam * o2
                    outs.append(ot * lax.rsqrt(jnp.mean(ot * ot, axis=0, keepdims=True) + EPS))
                oc = jnp.concatenate(outs, axis=0).T
                oc_ref[rows, :] = (oc * g_ref[0] * (1.0 - lam_init) * gate_ref[rows, :]).astype(BF16)

        return ([functools.partial(scores, *item) for item in its],
                [functools.partial(lambda bb, h: (vt_sc[bb, dh * h:dh * (h + 1), :], True), bb, h)
                 for bb, _, h, _ in its],
                finish)

    return items


def _attn_kernel(*refs, seq, ctx_len, lam_init, spb, mla, diff):
    it = iter(refs)
    n_ctx = 2 if ctx_len else 0
    mla_in = [next(it) for _ in range(4 + n_ctx)] + [None] * (2 - n_ctx) if mla else None
    diff_in = [next(it) for _ in range(5 + n_ctx)] + [None] * (2 - n_ctx) if diff else None
    ob_ref = next(it) if mla else None
    oc_ref = next(it) if diff else None
    parts = []
    if mla:
        parts.append(_mla_part(*mla_in, ob_ref, next(it), next(it), seq=seq, ctx_len=ctx_len, spb=spb))
    if diff:
        parts.append(_diff_part(*diff_in, oc_ref, next(it), next(it), seq=seq, ctx_len=ctx_len,
                                lam_init=lam_init, spb=spb))

    def run(blocks):
        built = [part(blocks) for part in parts]
        outs = _pipelined_attention([f for b in built for f in b[0]], [f for b in built for f in b[1]],
                                    _lookahead(seq + ctx_len))
        lo = 0
        for score_fns, _, finish in built:
            finish(outs[lo:lo + len(score_fns)])
            lo += len(score_fns)

    _for_query_blocks(run, seq, spb)


def _attn_call(mq, mk, mvt, gate_b, dqk, dvt, gate_c, lamp, g4, ctx, layer, *, seq):
    n = mq.shape[0]
    ctx_len = 0 if ctx is None else ctx["mla_k"].shape[2]
    lam_init = 0.8 - 0.6 * math.exp(-0.3 * layer)
    spb = _seqs_per_step(n, seq, ctx_len)
    rows = spb * seq
    tk = seq + ctx_len
    rblk = lambda w: pl.BlockSpec((rows, w), lambda i: (i, 0))
    tblk = lambda w: pl.BlockSpec((w, rows), lambda i: (0, i))
    mla_specs, mla_args = [rblk(768), rblk(640), tblk(W_B), rblk(W_B)], [mq, mk, mvt, gate_b]
    diff_specs = [rblk(512), tblk(W_C), rblk(W_C), _layer_spec(lamp.shape, layer), _layer_spec(g4.shape, layer)]
    diff_args = [dqk, dvt, gate_c, lamp, g4]
    if ctx_len:
        mla_specs += [pl.BlockSpec((1, 1, ctx_len, 640), lambda i: (layer, i, 0, 0)),
                      pl.BlockSpec((1, 1, W_B, ctx_len), lambda i: (layer, i, 0, 0))]
        mla_args += [ctx["mla_k"], ctx["mla_vt"]]
        diff_specs += [pl.BlockSpec((1, 1, H_C, 2 * DC, ctx_len), lambda i: (i, layer, 0, 0, 0))] * 2
        diff_args += [ctx["diff_k_t"], ctx["diff_v_t"]]
    mla_out = (rblk(W_B), jax.ShapeDtypeStruct((n, W_B), BF16))
    diff_out = (rblk(W_C), jax.ShapeDtypeStruct((n, W_C), BF16))
    mla_scratch = [pltpu.VMEM((spb, H_B // 2, tk, 256), BF16), pltpu.VMEM((spb, W_B, tk), BF16)]
    diff_scratch = [pltpu.VMEM((spb, H_C // 2, tk, 128), BF16), pltpu.VMEM((spb, W_C, tk), BF16)]

    def call(mla, diff, name):
        outs = ([mla_out] if mla else []) + ([diff_out] if diff else [])
        return pl.pallas_call(
            functools.partial(_attn_kernel, seq=seq, ctx_len=ctx_len, lam_init=lam_init, spb=spb,
                              mla=mla, diff=diff),
            grid=(n // rows,),
            in_specs=(mla_specs if mla else []) + (diff_specs if diff else []),
            out_specs=[o[0] for o in outs], out_shape=[o[1] for o in outs],
            scratch_shapes=(mla_scratch if mla else []) + (diff_scratch if diff else []),
            compiler_params=_cparams(), name=name,
        )(*(mla_args if mla else []), *(diff_args if diff else []))

    if ctx_len:
        (ob,) = call(True, False, "mla_ctx")
        (oc,) = call(False, True, "diff_ctx")
        return ob, oc
    return call(True, True, "attn_self")


def _post_kernel(oa_ref, ob_ref, oc_ref, x_ref, mod_ref, w_ref, g_ref, y_ref, *, layer):
    mix = jnp.concatenate([oa_ref[...], ob_ref[...], oc_ref[...]], axis=-1)
    y_ref[...] = x_ref[...] + _rms(_dot(mix, w_ref[0]),
                                   _layer_row(g_ref, layer) * mod_ref[0, :, 2 * D_MODEL:3 * D_MODEL])


def _post_call(oa, ob, oc, x2d, mod, pw, layer, *, seq, sample):
    n = x2d.shape[0]
    tm = min(POST_BLOCK, seq if sample else n)
    blk = lambda w: pl.BlockSpec((tm, w), lambda i: (i, 0))

    def mod_idx(i):
        return (layer * MOD_ROWS + ((i * tm) // seq + 1 if sample else 0), 0, 0)

    return pl.pallas_call(
        functools.partial(_post_kernel, layer=layer),
        grid=(n // tm,),
        in_specs=[blk(W_A), blk(W_B), blk(W_C), blk(D_MODEL),
                  pl.BlockSpec((1, 1, 3 * D_MODEL), mod_idx),
                  _layer_spec(pw["w_out"].shape, layer), _layer_spec(pw["g_post"].shape, layer)],
        out_specs=blk(D_MODEL),
        out_shape=jax.ShapeDtypeStruct((n, D_MODEL), F32),
        compiler_params=_cparams(), name="post",
    )(oa, ob, oc, x2d, mod, pw["w_out"], pw["g_post"])


def _pack_params(g_pre, g_post, w_in, w_gla_af, b_gla_af, w_gla_ab, b_gla_ab, g_gla, g_mla_q, w_mla_uq,
                 g_mla_kv, w_mla_ukv, lam_q1, lam_k1, lam_q2, lam_k2, g_diff, w_out):
    w_t = jnp.swapaxes(w_in, 1, 2)
    assert w_t.shape[1] == R_DIFF[0][1]
    zg = jnp.zeros((DEPTH, GLA_LR, 128), F32)
    w_gate = jnp.concatenate([jnp.zeros((DEPTH, DR_B, 256), F32),
                              jnp.concatenate([w_gla_af, zg], axis=-1),
                              jnp.concatenate([zg, w_gla_ab], axis=-1),
                              jnp.zeros((DEPTH, 128 - DR_B - 2 * GLA_LR, 256), F32)], axis=1).astype(BF16)
    uq = w_mla_uq.reshape(DEPTH, Q_LORA, H_B, DN_B + DR_B)
    w_pe = uq[..., DN_B:].reshape(DEPTH, Q_LORA, H_B * DR_B)
    w_uq = jnp.concatenate([uq[..., :DN_B].reshape(DEPTH, Q_LORA, H_B * DN_B), w_pe], axis=-1).astype(BF16)
    ukv = w_mla_ukv.reshape(DEPTH, KV_LORA, H_B, DN_B + DV_B)
    w_ukv = jnp.concatenate([ukv[..., :DN_B].reshape(DEPTH, KV_LORA, H_B * DN_B),
                             ukv[..., DN_B:].reshape(DEPTH, KV_LORA, H_B * DV_B)], axis=-1).astype(BF16)
    row = lambda a: a.reshape(DEPTH, 1, a.shape[-1])
    return dict(
        w_t=w_t.astype(BF16), w_gate=w_gate,
        b_gate=row(jnp.concatenate([b_gla_af, b_gla_ab], axis=-1)),
        w_uq=w_uq, w_ukv=w_ukv, w_out=w_out.astype(BF16),
        g_pre=g_pre, g_post=g_post, g_mla_q=g_mla_q, g_mla_kv=g_mla_kv,
        g_gla4=row(jnp.tile(g_gla, (1, H_A))), g_diff4=row(jnp.tile(g_diff, (1, H_C))),
        lam=jnp.stack([lam_q1, lam_k1, lam_q2, lam_k2], axis=1))


def _rope_tables(n):
    t = np.arange(n)
    row = (t // GRID_W).astype(np.float32)
    col = (t % GRID_W).astype(np.float32)
    half = ROPE_DIM // 2
    inv = (1.0 / (np.float32(ROPE_THETA) ** (np.arange(0, half, 2, dtype=np.float32) / np.float32(half)))
           ).astype(np.float32)
    ar = row[:, None] * inv
    ac = col[:, None] * inv
    ang = np.concatenate([ar, ar, ac, ac], axis=-1).astype(np.float32)
    return (jnp.asarray(np.tile(np.cos(ang), (1, 8)).astype(np.float32)),
            jnp.asarray(np.tile(np.sin(ang), (1, 8)).astype(np.float32)))


def _sublayer(x2d, mod, pw, layer, *, seq, rope_tabs, ctx, caches):
    sample = ctx is not None
    pre = _pre_call(x2d, mod, pw, layer, seq=seq, rope_tabs=rope_tabs,
                    caches=None if sample else caches[:4])
    qk, v, la, bsum, ga, gb, gc, mq, mk, mvt, dqk, dvt = pre[:12]
    if sample:
        (oa,) = _gla_call(qk, v, la, bsum, ga, pw["g_gla4"], layer, seq=seq, state_in=ctx["state"])
        new_caches = None
    else:
        oa, sfin = _gla_call(qk, v, la, bsum, ga, pw["g_gla4"], layer, seq=seq,
                             state_out=caches[4] if caches else None)
        new_caches = tuple(pre[12:]) + (sfin,)
    ob, oc = _attn_call(mq, mk, mvt, gb, dqk, dvt, gc, pw["lam"], pw["g_diff4"], ctx, layer, seq=seq)
    y = _post_call(oa, ob, oc, x2d, mod, pw, layer, seq=seq, sample=sample)
    return y, new_caches


def kernel(x_prompt, x_sample, c, cache_mla_ckv, cache_mla_kpe, cache_diff_k, cache_diff_v, state_gla,
           c_ctx, w_ada, b_ada, g_pre, g_post, w_in, w_gla_af, b_gla_af, w_gla_ab, b_gla_ab, g_gla,
           g_mla_q, w_mla_uq, g_mla_kv, w_mla_ukv, lam_q1, lam_k1, lam_q2, lam_k2, g_diff, w_out):
    bp, tp, d = x_prompt.shape
    bs, ts, _ = x_sample.shape

    pw = _pack_params(g_pre, g_post, w_in, w_gla_af, b_gla_af, w_gla_ab, b_gla_ab, g_gla, g_mla_q,
                      w_mla_uq, g_mla_kv, w_mla_ukv, lam_q1, lam_k1, lam_q2, lam_k2, g_diff, w_out)
    mod = _mod_call(c_ctx.reshape(1, d), c, w_ada, b_ada)
    rope_tabs = _rope_tables(ts)
    ctx_k, ctx_vt = _ctxkv_call(cache_mla_ckv, jnp.swapaxes(cache_mla_kpe, -1, -2), pw["w_ukv"])
    ctx = dict(state=state_gla, mla_k=ctx_k, mla_vt=ctx_vt,
               diff_k_t=jnp.swapaxes(cache_diff_k, -1, -2), diff_v_t=jnp.swapaxes(cache_diff_v, -1, -2))

    y_p = x_prompt.reshape(bp * tp, d)
    y_s = x_sample.reshape(bs * ts, d)
    caches = ()
    for l in range(DEPTH):
        y_p, caches = _sublayer(y_p, mod, pw, l, seq=tp, rope_tabs=None, ctx=None, caches=caches)
        y_s, _ = _sublayer(y_s, mod, pw, l, seq=ts, rope_tabs=rope_tabs, ctx=ctx, caches=None)
    ckvn, kpe_t, kc_t, vc_t, new_state = caches
    return (y_p.reshape(bp, tp, d), y_s.reshape(bs, ts, d), ckvn, jnp.swapaxes(kpe_t, -1, -2),
            jnp.swapaxes(kc_t, -1, -2), jnp.swapaxes(vc_t, -1, -2), new_state)
```

```python
import functools
import math

import numpy as np
import jax
import jax.numpy as jnp
from jax import lax
from jax.experimental import pallas as pl
from jax.experimental.pallas import tpu as pltpu

F32 = jnp.float32
BF16 = jnp.bfloat16

D_MODEL = 1024
DEPTH = 2
GRID_W = 64
EPS = 1e-6
ROPE_THETA = 10000.0
ROPE_DIM = 32
H_A, DK_A, DV_A = 4, 32, 64
GLA_LR = 16
GLA_TAU = 16.0
GLA_CHUNK = 64
H_B, DN_B, DR_B, DV_B = 8, 64, 32, 64
Q_LORA, KV_LORA = 256, 128
H_C, DC = 4, 32
W_A, W_B, W_C = H_A * DV_A, H_B * DV_B, H_C * 2 * DC
ST_R, ST_C = H_A * DV_A, H_A * DK_A
LOG2E = math.log2(math.e)
SUBLANES = 8

R_QKV = ((0, 512),)
R_GG = ((544, 800),)
R_MLA = ((800, 1216), (512, 544), 64)
R_MG = ((1216, 1728),)
R_DIFF = ((1728, 2752),)
MOD_ROWS = 8

V7X_VMEM_LIMIT_BYTES = 56 * 1024 * 1024
TOKEN_BLOCK = 1024
POST_BLOCK = 1024
ATT_QBLOCK = 512
ATT_GROUP_ROWS = 1024
GLA_GROUP_ROWS = 1024
SOFTMAX_KEY_CHUNK = 512
ATT_STATIC_BLOCKS = 4


def _cparams(n_axes=1):
    return pltpu.CompilerParams(dimension_semantics=("arbitrary",) * n_axes,
                                vmem_limit_bytes=V7X_VMEM_LIMIT_BYTES)


def _rms(x, g):
    return x * lax.rsqrt(jnp.mean(x * x, axis=-1, keepdims=True) + EPS) * g


def _silu(x):
    return x * jax.nn.sigmoid(x)


def _log_sigmoid(x):
    return jnp.minimum(x, 0.0) - jnp.log1p(jnp.exp(-jnp.abs(x)))


def _rope(z, cos, sin):
    w = z.shape[-1]
    lane = lax.broadcasted_iota(jnp.int32, z.shape, 1)
    rot = jnp.where((lane & 15) < 8, -pltpu.roll(z, w - 8, 1), pltpu.roll(z, 8, 1))
    return z * cos + rot * sin


def _lane_mask(width, lo, size, dtype):
    lane = lax.broadcasted_iota(jnp.int32, (1, width), 1)
    return jnp.where((lane >= lo) & (lane < lo + size), 1.0, 0.0).astype(dtype)


def _dot(a, b):
    return jnp.dot(a, b, preferred_element_type=F32)


def _dot_nt(a, b):
    return lax.dot_general(a, b, (((1,), (1,)), ((), ())), preferred_element_type=F32)


def _dot_tn(a, b):
    return lax.dot_general(a, b, (((0,), (0,)), ((), ())), preferred_element_type=F32)


def _layer_spec(shape, layer):
    nd = len(shape)
    if nd == 2:
        return pl.BlockSpec(tuple(shape), lambda *_: (0, 0))
    return pl.BlockSpec((1,) + tuple(shape[1:]), lambda *_: (layer,) + (0,) * (nd - 1))


def _layer_row(ref, layer):
    return ref[layer:layer + 1, :]


_ANY = pl.BlockSpec(memory_space=pl.ANY)


def _mod_kernel(cctx_ref, c_ref, w_ref, b_ref, o_ref, rows_sc):
    nc = c_ref.shape[0]
    rows_sc[0:1, :] = cctx_ref[...]
    rows_sc[1:1 + nc, :] = c_ref[...]
    if 1 + nc < MOD_ROWS:
        rows_sc[1 + nc:, :] = jnp.zeros((MOD_ROWS - 1 - nc, D_MODEL), F32)
    s = _silu(rows_sc[...]).astype(BF16)
    layer = pl.program_id(0)
    bias = b_ref[0:1, :]
    for r in range(1, DEPTH):
        bias = jnp.where(layer == r, b_ref[r:r + 1, :], bias)
    mod = _dot(s, w_ref[0].astype(BF16)) + bias
    for r in range(MOD_ROWS):
        o_ref[r] = mod[r:r + 1, :]


def _mod_call(c_ctx, c, w_ada, b_ada):
    nb = 3 * D_MODEL // 2
    assert 1 + c.shape[0] <= MOD_ROWS
    return pl.pallas_call(
        _mod_kernel,
        grid=(DEPTH, 3 * D_MODEL // nb),
        in_specs=[pl.BlockSpec((1, D_MODEL), lambda l, j: (0, 0)),
                  pl.BlockSpec(c.shape, lambda l, j: (0, 0)),
                  pl.BlockSpec((1, D_MODEL, nb), lambda l, j: (l, 0, j)),
                  pl.BlockSpec((DEPTH, nb), lambda l, j: (0, j))],
        out_specs=pl.BlockSpec((MOD_ROWS, 1, nb), lambda l, j: (l, 0, j)),
        out_shape=jax.ShapeDtypeStruct((DEPTH * MOD_ROWS, 1, 3 * D_MODEL), F32),
        scratch_shapes=[pltpu.VMEM((MOD_ROWS, D_MODEL), F32)],
        compiler_params=_cparams(2), name="adaln_mod",
    )(c_ctx, c, w_ada, b_ada)


def _ctxkv_kernel(ckv_ref, kpe_ref, w_ref, k_ref, vt_ref):
    for b in range(ckv_ref.shape[0]):
        kv = _dot(ckv_ref[b, 0].astype(BF16), w_ref[0])
        kpe4 = jnp.concatenate([kpe_ref[b, 0]] * 4, axis=0).T
        k_ref[0, b, :, 0:512] = kv[:, 0:512].astype(BF16)
        k_ref[0, b, :, 512:640] = kpe4.astype(BF16)
        vt_ref[0, b] = kv[:, 512:1024].T.astype(BF16)


def _ctxkv_call(cache_ckv, cache_kpe_t, wukv):
    nb, _, tc, _ = cache_ckv.shape
    return pl.pallas_call(
        _ctxkv_kernel,
        grid=(DEPTH,),
        in_specs=[pl.BlockSpec((nb, 1, tc, KV_LORA), lambda l: (0, l, 0, 0)),
                  pl.BlockSpec((nb, 1, DR_B, tc), lambda l: (0, l, 0, 0)),
                  pl.BlockSpec((1, KV_LORA, 1024), lambda l: (l, 0, 0))],
        out_specs=[pl.BlockSpec((1, nb, tc, 640), lambda l: (l, 0, 0, 0)),
                   pl.BlockSpec((1, nb, W_B, tc), lambda l: (l, 0, 0, 0))],
        out_shape=[jax.ShapeDtypeStruct((DEPTH, nb, tc, 640), BF16),
                   jax.ShapeDtypeStruct((DEPTH, nb, W_B, tc), BF16)],
        compiler_params=_cparams(), name="mla_ctx_kv",
    )(cache_ckv, cache_kpe_t, wukv)


def _pre_kernel(*refs, rope, ctx_out, alias_in, bpb, seq, layer):
    it = iter(refs)
    (x_ref, mod_ref, gpre_ref, w_ref, wg_ref, bg_ref, gq_ref, gkv_ref, wuq_ref,
     wukv_ref) = (next(it) for _ in range(10))
    if rope:
        cos_ref, sin_ref = next(it), next(it)
    for _ in range(alias_in):
        next(it)
    (qk_ref, v_ref, la_ref, bs_ref, ga_ref, gb_ref, gc_ref, mq_ref, mk_ref, mvt_ref, dqk_ref,
     dvt_ref) = (next(it) for _ in range(12))
    if ctx_out:
        ckvn_ref, kpe_ref, kc_ref, vc_ref = (next(it) for _ in range(4))

    d = D_MODEL
    shift = mod_ref[0, :, 0:d]
    scale = mod_ref[0, :, d:2 * d]
    h = (_rms(x_ref[...], _layer_row(gpre_ref, layer) * (1.0 + scale)) + shift).astype(BF16)
    def proj(group):
        parts = [jnp.zeros((r, d), BF16) if isinstance(r, int) else w_ref[0, r[0]:r[1], :] for r in group]
        return _dot_nt(h, parts[0] if len(parts) == 1 else jnp.concatenate(parts, axis=0))
    if rope:
        cos = cos_ref[...]
        sin = sin_ref[...]

    pg = proj(R_QKV)
    qk_ref[:, 0:128] = pg[:, 0:128] * (DK_A ** -0.5)
    qk_ref[:, 128:256] = pg[:, 128:256]
    v_ref[...] = pg[:, 256:512]
    pm = proj(R_MLA)
    tail = pm[:, 384:512]
    xg = _dot(tail.astype(BF16), wg_ref[0]) + bg_ref[0]
    la = _log_sigmoid(xg) * (1.0 / GLA_TAU)
    la_ref[...] = la
    bs_ref[:, 0:128] = _chunk_scan(la[:, 0:128], False)
    bs_ref[:, 128:256] = _chunk_scan(la[:, 128:256], True)
    ga_ref[...] = _silu(proj(R_GG)).astype(BF16)

    qall = _dot(_rms(pm[:, 0:256], _layer_row(gq_ref, layer)).astype(BF16), wuq_ref[0])
    q_pe = qall[:, 512:768]
    if rope:
        q_pe = _rope(q_pe, cos, sin)
    sb = (DN_B + DR_B) ** -0.5 * LOG2E
    mq_ref[:, 0:512] = (qall[:, 0:512] * sb).astype(BF16)
    mq_ref[:, 512:768] = (q_pe * sb).astype(BF16)
    ckvn = _rms(pm[:, 256:384], _layer_row(gkv_ref, layer))
    kvall = _dot(ckvn.astype(BF16), wukv_ref[0])
    lane = lax.broadcasted_iota(jnp.int32, tail.shape, 1)
    kpe4 = jnp.where(lane < DR_B, tail, 0.0)
    kpe4 = kpe4 + pltpu.roll(kpe4, DR_B, 1)
    kpe4 = kpe4 + pltpu.roll(kpe4, 2 * DR_B, 1)
    if rope:
        kpe4 = _rope(kpe4, cos[:, 0:128], sin[:, 0:128])
    mk_ref[:, 0:512] = kvall[:, 0:512].astype(BF16)
    mk_ref[:, 512:640] = kpe4.astype(BF16)
    mvt_ref[...] = kvall[:, 512:1024].T.astype(BF16)
    gb_ref[...] = _silu(proj(R_MG)).astype(BF16)

    pd = proj(R_DIFF)
    dq, dk, dv = pd[:, 0:256], pd[:, 256:512], pd[:, 512:768]
    if rope:
        dq = _rope(dq, cos, sin)
        dk = _rope(dk, cos, sin)
    dqk_ref[:, 0:256] = (dq * (DC ** -0.5 * LOG2E)).astype(BF16)
    dqk_ref[:, 256:512] = dk.astype(BF16)
    dv_t = dv.T
    dvt_ref[...] = dv_t.astype(BF16)
    gc_ref[...] = _silu(pd[:, 768:1024]).astype(BF16)
    if ctx_out:
        kpe_t = kpe4.T
        dk_t = dk.T
        for ref in () if alias_in else (ckvn_ref, kpe_ref, kc_ref, vc_ref):
            ref[:, 1:] = jnp.zeros(ref[:, 1:].shape, F32)
        for bb in range(bpb):
            rs = slice(bb * seq, (bb + 1) * seq)
            ckvn_ref[bb, 0] = ckvn[rs]
            kpe_ref[bb, 0] = kpe_t[0:DR_B, rs]
            kc_ref[bb, 0] = dk_t[:, rs].reshape(H_C, 2 * DC, seq)
            vc_ref[bb, 0] = dv_t[:, rs].reshape(H_C, 2 * DC, seq)


def _pre_call(x2d, mod, pw, layer, *, seq, rope_tabs, caches):
    n = x2d.shape[0]
    tm = min(TOKEN_BLOCK, n)
    bpb = max(tm // seq, 1)
    rope = rope_tabs is not None
    ctx_out = caches is not None
    steps_per_seq = max(seq // tm, 1)
    nbt = n // seq

    def mod_idx(i):
        return (layer * MOD_ROWS + ((i * tm) // seq + 1 if rope else 0), 0, 0)

    names = ["g_pre", "w_t", "w_gate", "b_gate", "g_mla_q", "g_mla_kv", "w_uq", "w_ukv"]
    in_specs = [pl.BlockSpec((tm, D_MODEL), lambda i: (i, 0)), pl.BlockSpec((1, 1, 3 * D_MODEL), mod_idx)]
    in_specs += [_layer_spec(pw[k].shape, layer) for k in names]
    args = [x2d, mod] + [pw[k] for k in names]
    if rope:
        in_specs += [pl.BlockSpec((tm, 256), lambda i: (i % steps_per_seq, 0))] * 2
        args += list(rope_tabs)
    outs = [(256, F32, False), (256, F32, False), (256, F32, False), (256, F32, False),
            (W_A, BF16, False), (W_B, BF16, False), (W_C, BF16, False),
            (768, BF16, False), (640, BF16, False), (W_B, BF16, True),
            (512, BF16, False), (W_C, BF16, True)]
    out_specs = [pl.BlockSpec((w, tm), lambda i: (0, i)) if tr else pl.BlockSpec((tm, w), lambda i: (i, 0))
                 for w, _, tr in outs]
    out_shape = [jax.ShapeDtypeStruct((w, n) if tr else (n, w), dt) for w, dt, tr in outs]
    widths = outs
    aliases = {}
    if ctx_out:
        assert caches or layer == 0
        nl = 1 if caches else DEPTH
        out_specs += [pl.BlockSpec((bpb, nl, seq, KV_LORA), lambda i: (i, layer, 0, 0)),
                      pl.BlockSpec((bpb, nl, DR_B, seq), lambda i: (i, layer, 0, 0)),
                      pl.BlockSpec((bpb, nl, H_C, 2 * DC, seq), lambda i: (i, layer, 0, 0, 0)),
                      pl.BlockSpec((bpb, nl, H_C, 2 * DC, seq), lambda i: (i, layer, 0, 0, 0))]
        out_shape += [jax.ShapeDtypeStruct((nbt, DEPTH, seq, KV_LORA), F32),
                      jax.ShapeDtypeStruct((nbt, DEPTH, DR_B, seq), F32),
                      jax.ShapeDtypeStruct((nbt, DEPTH, H_C, 2 * DC, seq), F32),
                      jax.ShapeDtypeStruct((nbt, DEPTH, H_C, 2 * DC, seq), F32)]
        for j, arr in enumerate(caches):
            aliases[len(args)] = len(widths) + j
            in_specs.append(_ANY)
            args.append(arr)
    return pl.pallas_call(
        functools.partial(_pre_kernel, rope=rope, ctx_out=ctx_out, alias_in=len(aliases), bpb=bpb, seq=seq,
                          layer=layer),
        grid=(n // tm,), in_specs=in_specs, out_specs=out_specs, out_shape=out_shape,
        input_output_aliases=aliases,
        compiler_params=_cparams(), name="pre_rope" if rope else "pre_ctx",
    )(*args)


_GLA_LEVELS = (1, 2, 4, 8, 16, 32)


def _gla_consts(rev):
    c = GLA_CHUNK
    row = lax.broadcasted_iota(jnp.int32, (c, 128), 0)
    pos = (c - 1 - row) if rev else row
    ri = lax.broadcasted_iota(jnp.int32, (c, H_A * c), 0)
    cj = lax.broadcasted_iota(jnp.int32, (c, H_A * c), 1) & (c - 1)
    pi = (c - 1 - ri) if rev else ri
    pj = (c - 1 - cj) if rev else cj
    x = pi ^ pj
    lvl = jnp.where(pi == pj, 0, -1)
    for kbit in range(6):
        lvl = jnp.where((pj < pi) & ((x >> kbit) == 1), kbit + 1, lvl)
    return pos, lvl


def _chunk_scan(x, rev):
    rows = x.shape[0]
    nt = rows // SUBLANES
    tiles_per_chunk = GLA_CHUNK // SUBLANES
    x3 = x.reshape(nt, SUBLANES, 128)
    sub = lax.broadcasted_iota(jnp.int32, x3.shape, 1)
    tile = lax.broadcasted_iota(jnp.int32, x3.shape, 0) & (tiles_per_chunk - 1)
    edge = 0 if rev else SUBLANES - 1
    s = 1
    while s < SUBLANES:
        if rev:
            x3 = x3 + jnp.where(sub < SUBLANES - s, pltpu.roll(x3, SUBLANES - s, 1), 0.0)
        else:
            x3 = x3 + jnp.where(sub >= s, pltpu.roll(x3, s, 1), 0.0)
        s *= 2
    s = 1
    while s < tiles_per_chunk:
        tot = jnp.broadcast_to(x3[:, edge:edge + 1, :], x3.shape)
        if rev:
            shifted = jnp.concatenate([tot[s:], tot[:s]], axis=0)
            x3 = x3 + jnp.where(tile < tiles_per_chunk - s, shifted, 0.0)
        else:
            shifted = jnp.concatenate([tot[nt - s:], tot[:nt - s]], axis=0)
            x3 = x3 + jnp.where(tile >= s, shifted, 0.0)
        s *= 2
    return x3.reshape(rows, 128)


def _gla_scores(q, k, la, b, pos, lvl, hm_bf, rev):
    c = GLA_CHUNK
    prv = pltpu.roll(la, c - 1 if rev else 1, 0)
    nxt = pltpu.roll(la, 1 if rev else c - 1, 0)
    lvl = lvl.astype(BF16)
    s_tot = jnp.where(lvl == 0, _dot_nt(q.astype(BF16), jnp.concatenate([k.astype(BF16)] * H_A, 0) * hm_bf
                                        ).astype(BF16), jnp.zeros((), BF16))
    for kbit, m in enumerate(_GLA_LEVELS):
        up = ((pos >> kbit) & 1) == 1
        if m == 1:
            e = jnp.where(up, la, 0.0)
        elif m == 2:
            c4 = pos & 3
            e = jnp.where(c4 == 0, nxt, jnp.where(c4 == 1, 0.0, jnp.where(c4 == 2, la, la + prv)))
        else:
            nblk = c // (2 * m)
            loc = m if rev else m - 1
            b3 = b.reshape(nblk, 2 * m, 128)
            ref = jnp.broadcast_to(b3[:, loc:loc + 1, :], (nblk, 2 * m, 128)).reshape(c, 128)
            dlt = b - ref
            e = jnp.where(up, dlt, -dlt)
        xm = (jnp.where(up, q, k) * jnp.exp(e)).astype(BF16)
        sm = _dot_nt(xm, jnp.concatenate([xm] * H_A, 0) * hm_bf)
        s_tot = jnp.where(lvl == kbit + 1, sm.astype(BF16), s_tot)
    return s_tot


def _gla_apply(s_tot, q, k, v, b, st_prev, hm_f32, vm_bf, rev):
    c = GLA_CHUNK
    vbd = jnp.concatenate([v] * H_A, 0) * vm_bf
    blast = b[0:1, :] if rev else b[c - 1:c, :]
    qbar = (q * jnp.exp(b)).astype(BF16)
    kdec = (k * jnp.exp(blast - b)).astype(BF16)
    o = _dot(s_tot, vbd) + _dot_nt(qbar, st_prev.astype(BF16))
    st_new = st_prev * jnp.exp(blast) + _dot_tn(v, kdec) * hm_f32
    return o, st_new


def _gla_kernel(*refs, seq, has_s0, alias_in, spb):
    it = iter(refs)
    qk_ref, v_ref, la_ref, b_ref, gate_ref, g_ref = (next(it) for _ in range(6))
    s0_ref = next(it) if has_s0 else None
    for _ in range(alias_in):
        next(it)
    oa_ref = next(it)
    sfin_ref = None if has_s0 else next(it)
    acc_sc, st_sc, blk_sc = next(it), next(it), next(it)

    c = GLA_CHUNK
    nc = seq // c
    acc_sc[...] = jnp.zeros_like(acc_sc)
    if not has_s0 and not alias_in:
        sfin_ref[:, 1:] = jnp.zeros(sfin_ref[:, 1:].shape, F32)

    hrow = lax.broadcasted_iota(jnp.int32, (H_A * c, 128), 0) // c
    hm_f32 = jnp.where(hrow == lax.broadcasted_iota(jnp.int32, (H_A * c, 128), 1) // DK_A, 1.0, 0.0)
    hm_bf = hm_f32.astype(BF16)
    vrow = lax.broadcasted_iota(jnp.int32, (H_A * c, H_A * DV_A), 0) // c
    vm_bf = jnp.where(vrow == lax.broadcasted_iota(jnp.int32, (H_A * c, H_A * DV_A), 1) // DV_A,
                      1.0, 0.0).astype(BF16)
    consts = (_gla_consts(False), _gla_consts(True))

    def run_sequence(bb):
        base = bb * seq
        for d in range(2):
            if has_s0:
                blk_sc[...] = jnp.zeros_like(blk_sc)
                for hh in range(H_A):
                    blk_sc[DK_A * hh:DK_A * (hh + 1), DV_A * hh:DV_A * (hh + 1)] = s0_ref[bb, 0, d, hh]
                st_sc[d] = blk_sc[...].T
            else:
                st_sc[d] = jnp.zeros((ST_R, ST_C), F32)

        def chunk_rows(n, d):
            cn = (nc - 1 - n) if d else n
            start = base + cn * c
            return pl.ds(start if isinstance(start, int) else pl.multiple_of(start, c), c)

        def scores(n):
            out = []
            for d in range(2):
                rows = chunk_rows(n, d)
                pos, lvl = consts[d]
                out.append(_gla_scores(qk_ref[rows, 0:128], qk_ref[rows, 128:256],
                                       la_ref[rows, 128 * d:128 * d + 128],
                                       b_ref[rows, 128 * d:128 * d + 128], pos, lvl, hm_bf, bool(d)))
            return tuple(out)

        def apply(n, s_both):
            for d in range(2):
                rows = chunk_rows(n, d)
                o, st_new = _gla_apply(s_both[d], qk_ref[rows, 0:128], qk_ref[rows, 128:256],
                                       v_ref[rows, :].astype(BF16), b_ref[rows, 128 * d:128 * d + 128],
                                       st_sc[d], hm_f32, vm_bf, bool(d))
                acc_sc[rows, :] = acc_sc[rows, :] + o
                st_sc[d] = st_new

        def body(n, s_cur):
            s_next = scores(n + 1)
            apply(n, s_cur)
            return s_next

        trips = nc - 1
        s_last = lax.fori_loop(0, trips, body, scores(0),
                               unroll=next(u for u in (15, 5, 3, 1) if trips % u == 0))
        apply(nc - 1, s_last)

        rs = slice(base, base + seq)
        first = lax.broadcasted_iota(jnp.int32, (seq, 128), 1) < DV_A
        for p in range(H_A // 2):
            cols = slice(128 * p, 128 * (p + 1))
            oa = acc_sc[rs, cols]
            sq = oa * oa
            s0 = jnp.sum(jnp.where(first, sq, 0.0), axis=-1, keepdims=True)
            s1 = jnp.sum(jnp.where(first, 0.0, sq), axis=-1, keepdims=True)
            inv = jnp.where(first, lax.rsqrt(s0 * (1.0 / DV_A) + EPS), lax.rsqrt(s1 * (1.0 / DV_A) + EPS))
            oa_ref[rs, cols] = (oa * inv * g_ref[0, :, cols] * gate_ref[rs, cols]).astype(BF16)
        if not has_s0:
            for d in range(2):
                blk_sc[...] = st_sc[d].T
                for hh in range(H_A):
                    sfin_ref[bb, 0, d, hh] = blk_sc[DK_A * hh:DK_A * (hh + 1), DV_A * hh:DV_A * (hh + 1)]

    for bb in range(spb):
        run_sequence(bb)


def _gla_call(qk, v, la, bsum, gate, g4, layer, *, seq, state_in=None, state_out=None):
    n = qk.shape[0]
    nb = n // seq
    has_s0 = state_in is not None
    assert has_s0 or state_out is not None or layer == 0
    spb = max(1, min(nb, GLA_GROUP_ROWS // seq))
    blk = lambda w: pl.BlockSpec((spb * seq, w), lambda i: (i, 0))
    nl = DEPTH if (not has_s0 and state_out is None) else 1
    st_spec = pl.BlockSpec((spb, nl, 2, H_A, DK_A, DV_A), lambda i: (i, layer, 0, 0, 0, 0))
    in_specs = [blk(256), blk(256), blk(256), blk(256), blk(W_A), _layer_spec(g4.shape, layer)]
    args = [qk, v, la, bsum, gate, g4]
    out_specs = [blk(W_A)]
    out_shape = [jax.ShapeDtypeStruct((n, W_A), BF16)]
    aliases = {}
    if has_s0:
        in_specs.append(st_spec)
        args.append(state_in)
    else:
        out_specs.append(st_spec)
        out_shape.append(jax.ShapeDtypeStruct((nb, DEPTH, 2, H_A, DK_A, DV_A), F32))
        if state_out is not None:
            aliases[len(args)] = 1
            in_specs.append(_ANY)
            args.append(state_out)
    return pl.pallas_call(
        functools.partial(_gla_kernel, seq=seq, has_s0=has_s0, alias_in=len(aliases), spb=spb),
        grid=(nb // spb,), in_specs=in_specs, out_specs=out_specs, out_shape=out_shape,
        input_output_aliases=aliases,
        scratch_shapes=[pltpu.VMEM((spb * seq, W_A), F32), pltpu.VMEM((2, ST_R, ST_C), F32),
                        pltpu.VMEM((ST_C, ST_R), F32)],
        compiler_params=_cparams(), name="gla_state" if has_s0 else "gla_ctx",
    )(*args)


def _softmax_t_pv(st, vt, ones_rows):
    dv, keys = vt.shape
    m = _col_reduce(st, jnp.max)
    kc = SOFTMAX_KEY_CHUNK if keys > SOFTMAX_KEY_CHUNK and keys % SOFTMAX_KEY_CHUNK == 0 else keys
    if ones_rows:
        vt = jnp.concatenate([vt, jnp.ones((16, keys), BF16)], axis=0)
    o, l = None, None
    for lo in range(0, keys, kc):
        e = jnp.exp2(st[lo:lo + kc] - m)
        part = _dot(vt[:, lo:lo + kc], e.astype(BF16))
        o = part if o is None else o + part
        if not ones_rows:
            ls = _col_reduce(e, jnp.sum)
            l = ls if l is None else l + ls
    if ones_rows:
        return o[:dv] * (1.0 / o[dv:dv + 1])
    return o * (1.0 / l)


def _pipelined_attention(score_fns, value_fns, depth):
    outs = []
    pending = [fn() for fn in score_fns[:depth]]
    for j, vfn in enumerate(value_fns):
        st = pending.pop(0)
        if j + depth < len(score_fns):
            pending.append(score_fns[j + depth]())
        outs.append(_softmax_t_pv(st, *vfn()))
    return outs


def _lookahead(n_keys):
    return 2 if n_keys >= 1024 else 3


def _col_reduce(x, op):
    rows, cols = x.shape
    part = 128 if rows % 128 == 0 and rows > 128 else rows
    if part != rows:
        x = op(x.reshape(rows // part, part, cols), axis=0)
    return op(x, axis=0, keepdims=True)


def _seqs_per_step(n, seq, ctx_len):
    if ctx_len or seq >= ATT_QBLOCK:
        return 1
    return max(1, min(n // seq, ATT_GROUP_ROWS // seq))


def _for_query_blocks(run, seq, spb):
    qb = min(ATT_QBLOCK, seq)
    nq = seq // qb
    if spb * nq <= ATT_STATIC_BLOCKS:
        run([(bb, slice(bb * seq + j * qb, bb * seq + (j + 1) * qb)) for bb in range(spb) for j in range(nq)])
    else:
        assert spb == 1

        def body(i, carry):
            run([(0, pl.ds(pl.multiple_of(i * qb, qb), qb))])
            return carry
        lax.fori_loop(0, nq, body, 0)


def _mla_part(q_ref, k_ref, vt_ref, gate_ref, ck_ref, cvt_ref, ob_ref, kk_sc, vt_sc, *, seq, ctx_len, spb):
    for bb in range(spb):
        rs = slice(bb * seq, (bb + 1) * seq)
        for p in range(H_B // 2):
            kk_sc[bb, p, 0:seq, 0:128] = k_ref[rs, 128 * p:128 * p + 128]
            kk_sc[bb, p, 0:seq, 128:256] = k_ref[rs, 512:640]
            if ctx_len:
                kk_sc[bb, p, seq:seq + ctx_len, 0:128] = ck_ref[0, 0, :, 128 * p:128 * p + 128]
                kk_sc[bb, p, seq:seq + ctx_len, 128:256] = ck_ref[0, 0, :, 512:640]
        vt_sc[bb, :, 0:seq] = vt_ref[:, rs]
        if ctx_len:
            vt_sc[bb, :, seq:seq + ctx_len] = cvt_ref[0, 0]

    def scores(bb, rows, h):
        p, hh = divmod(h, 2)
        qn = q_ref[rows, 128 * p:128 * p + 128] * _lane_mask(128, DN_B * hh, DN_B, BF16)
        qp = (q_ref[rows, 512 + 128 * (h // 4):512 + 128 * (h // 4) + 128]
              * _lane_mask(128, DR_B * (h % 4), DR_B, BF16))
        return _dot_nt(kk_sc[bb, p], jnp.concatenate([qn, qp], axis=-1))

    def items(blocks):
        its = [(bb, rows, h) for bb, rows in blocks for h in range(H_B)]

        def finish(outs):
            for j, (_, rows) in enumerate(blocks):
                ob = jnp.concatenate(outs[H_B * j:H_B * (j + 1)], axis=0).T
                ob_ref[rows, :] = (ob * gate_ref[rows, :]).astype(BF16)

        return ([functools.partial(scores, bb, rows, h) for bb, rows, h in its],
                [functools.partial(lambda bb, h: (vt_sc[bb, DV_B * h:DV_B * (h + 1), :], not ctx_len), bb, h)
                 for bb, _, h in its],
                finish)

    return items


def _diff_part(qk_ref, vt_ref, gate_ref, lam_ref, g_ref, ck_ref, cv_ref, oc_ref, k_sc, vt_sc, *,
               seq, ctx_len, lam_init, spb):
    lam = (jnp.exp(jnp.sum(lam_ref[0, 0:1, :] * lam_ref[0, 1:2, :], axis=-1, keepdims=True))
           - jnp.exp(jnp.sum(lam_ref[0, 2:3, :] * lam_ref[0, 3:4, :], axis=-1, keepdims=True)) + lam_init)
    dh = 2 * DC
    for bb in range(spb):
        rs = slice(bb * seq, (bb + 1) * seq)
        for p in range(H_C // 2):
            k_sc[bb, p, 0:seq, :] = qk_ref[rs, 256 + 128 * p:256 + 128 * p + 128]
            if ctx_len:
                pair_t = jnp.concatenate([ck_ref[0, 0, 2 * p], ck_ref[0, 0, 2 * p + 1]], axis=0)
                k_sc[bb, p, seq:seq + ctx_len, :] = pair_t.T.astype(BF16)
        vt_sc[bb, :, 0:seq] = vt_ref[:, rs]
        if ctx_len:
            for h in range(H_C):
                vt_sc[bb, dh * h:dh * (h + 1), seq:seq + ctx_len] = cv_ref[0, 0, h].astype(BF16)

    def scores(bb, rows, h, comp):
        p, hh = divmod(h, 2)
        qm = qk_ref[rows, 128 * p:128 * p + 128] * _lane_mask(128, dh * hh + DC * comp, DC, BF16)
        return _dot_nt(k_sc[bb, p], qm)

    def items(blocks):
        its = [(bb, rows, h, comp) for bb, rows in blocks for h in range(H_C) for comp in range(2)]

        def finish(o12):
            for j, (_, rows) in enumerate(blocks):
                outs = []
                for h in range(H_C):
                    o1, o2 = o12[2 * (H_C * j + h)], o12[2 * (H_C * j + h) + 1]
                    ot = o1 - lam * o2
                    outs.append(ot * lax.rsqrt(jnp.mean(ot * ot, axis=0, keepdims=True) + EPS))
                oc = jnp.concatenate(outs, axis=0).T
                oc_ref[rows, :] = (oc * g_ref[0] * (1.0 - lam_init) * gate_ref[rows, :]).astype(BF16)

        return ([functools.partial(scores, *item) for item in its],
                [functools.partial(lambda bb, h: (vt_sc[bb, dh * h:dh * (h + 1), :], True), bb, h)
                 for bb, _, h, _ in its],
                finish)

    return items


def _attn_kernel(*refs, seq, ctx_len, lam_init, spb, mla, diff):
    it = iter(refs)
    n_ctx = 2 if ctx_len else 0
    mla_in = [next(it) for _ in range(4 + n_ctx)] + [None] * (2 - n_ctx) if mla else None
    diff_in = [next(it) for _ in range(5 + n_ctx)] + [None] * (2 - n_ctx) if diff else None
    ob_ref = next(it) if mla else None
    oc_ref = next(it) if diff else None
    parts = []
    if mla:
        parts.append(_mla_part(*mla_in, ob_ref, next(it), next(it), seq=seq, ctx_len=ctx_len, spb=spb))
    if diff:
        parts.append(_diff_part(*diff_in, oc_ref, next(it), next(it), seq=seq, ctx_len=ctx_len,
                                lam_init=lam_init, spb=spb))

    def run(blocks):
        built = [part(blocks) for part in parts]
        outs = _pipelined_attention([f for b in built for f in b[0]], [f for b in built for f in b[1]],
                                    _lookahead(seq + ctx_len))
        lo = 0
        for score_fns, _, finish in built:
            finish(outs[lo:lo + len(score_fns)])
            lo += len(score_fns)

    _for_query_blocks(run, seq, spb)


def _attn_call(mq, mk, mvt, gate_b, dqk, dvt, gate_c, lamp, g4, ctx, layer, *, seq):
    n = mq.shape[0]
    ctx_len = 0 if ctx is None else ctx["mla_k"].shape[2]
    lam_init = 0.8 - 0.6 * math.exp(-0.3 * layer)
    spb = _seqs_per_step(n, seq, ctx_len)
    rows = spb * seq
    tk = seq + ctx_len
    rblk = lambda w: pl.BlockSpec((rows, w), lambda i: (i, 0))
    tblk = lambda w: pl.BlockSpec((w, rows), lambda i: (0, i))
    mla_specs, mla_args = [rblk(768), rblk(640), tblk(W_B), rblk(W_B)], [mq, mk, mvt, gate_b]
    diff_specs = [rblk(512), tblk(W_C), rblk(W_C), _layer_spec(lamp.shape, layer), _layer_spec(g4.shape, layer)]
    diff_args = [dqk, dvt, gate_c, lamp, g4]
    if ctx_len:
        mla_specs += [pl.BlockSpec((1, 1, ctx_len, 640), lambda i: (layer, i, 0, 0)),
                      pl.BlockSpec((1, 1, W_B, ctx_len), lambda i: (layer, i, 0, 0))]
        mla_args += [ctx["mla_k"], ctx["mla_vt"]]
        diff_specs += [pl.BlockSpec((1, 1, H_C, 2 * DC, ctx_len), lambda i: (i, layer, 0, 0, 0))] * 2
        diff_args += [ctx["diff_k_t"], ctx["diff_v_t"]]
    mla_out = (rblk(W_B), jax.ShapeDtypeStruct((n, W_B), BF16))
    diff_out = (rblk(W_C), jax.ShapeDtypeStruct((n, W_C), BF16))
    mla_scratch = [pltpu.VMEM((spb, H_B // 2, tk, 256), BF16), pltpu.VMEM((spb, W_B, tk), BF16)]
    diff_scratch = [pltpu.VMEM((spb, H_C // 2, tk, 128), BF16), pltpu.VMEM((spb, W_C, tk), BF16)]

    def call(mla, diff, name):
        outs = ([mla_out] if mla else []) + ([diff_out] if diff else [])
        return pl.pallas_call(
            functools.partial(_attn_kernel, seq=seq, ctx_len=ctx_len, lam_init=lam_init, spb=spb,
                              mla=mla, diff=diff),
            grid=(n // rows,),
            in_specs=(mla_specs if mla else []) + (diff_specs if diff else []),
            out_specs=[o[0] for o in outs], out_shape=[o[1] for o in outs],
            scratch_shapes=(mla_scratch if mla else []) + (diff_scratch if diff else []),
            compiler_params=_cparams(), name=name,
        )(*(mla_args if mla else []), *(diff_args if diff else []))

    if ctx_len:
        (ob,) = call(True, False, "mla_ctx")
        (oc,) = call(False, True, "diff_ctx")
        return ob, oc
    return call(True, True, "attn_self")


def _post_kernel(oa_ref, ob_ref, oc_ref, x_ref, mod_ref, w_ref, g_ref, y_ref, *, layer):
    mix = jnp.concatenate([oa_ref[...], ob_ref[...], oc_ref[...]], axis=-1)
    y_ref[...] = x_ref[...] + _rms(_dot(mix, w_ref[0]),
                                   _layer_row(g_ref, layer) * mod_ref[0, :, 2 * D_MODEL:3 * D_MODEL])


def _post_call(oa, ob, oc, x2d, mod, pw, layer, *, seq, sample):
    n = x2d.shape[0]
    tm = min(POST_BLOCK, seq if sample else n)
    blk = lambda w: pl.BlockSpec((tm, w), lambda i: (i, 0))

    def mod_idx(i):
        return (layer * MOD_ROWS + ((i * tm) // seq + 1 if sample else 0), 0, 0)

    return pl.pallas_call(
        functools.partial(_post_kernel, layer=layer),
        grid=(n // tm,),
        in_specs=[blk(W_A), blk(W_B), blk(W_C), blk(D_MODEL),
                  pl.BlockSpec((1, 1, 3 * D_MODEL), mod_idx),
                  _layer_spec(pw["w_out"].shape, layer), _layer_spec(pw["g_post"].shape, layer)],
        out_specs=blk(D_MODEL),
        out_shape=jax.ShapeDtypeStruct((n, D_MODEL), F32),
        compiler_params=_cparams(), name="post",
    )(oa, ob, oc, x2d, mod, pw["w_out"], pw["g_post"])


def _pack_params(g_pre, g_post, w_in, w_gla_af, b_gla_af, w_gla_ab, b_gla_ab, g_gla, g_mla_q, w_mla_uq,
                 g_mla_kv, w_mla_ukv, lam_q1, lam_k1, lam_q2, lam_k2, g_diff, w_out):
    w_t = jnp.swapaxes(w_in, 1, 2)
    assert w_t.shape[1] == R_DIFF[0][1]
    zg = jnp.zeros((DEPTH, GLA_LR, 128), F32)
    w_gate = jnp.concatenate([jnp.zeros((DEPTH, DR_B, 256), F32),
                              jnp.concatenate([w_gla_af, zg], axis=-1),
                              jnp.concatenate([zg, w_gla_ab], axis=-1),
                              jnp.zeros((DEPTH, 128 - DR_B - 2 * GLA_LR, 256), F32)], axis=1).astype(BF16)
    uq = w_mla_uq.reshape(DEPTH, Q_LORA, H_B, DN_B + DR_B)
    w_pe = uq[..., DN_B:].reshape(DEPTH, Q_LORA, H_B * DR_B)
    w_uq = jnp.concatenate([uq[..., :DN_B].reshape(DEPTH, Q_LORA, H_B * DN_B), w_pe], axis=-1).astype(BF16)
    ukv = w_mla_ukv.reshape(DEPTH, KV_LORA, H_B, DN_B + DV_B)
    w_ukv = jnp.concatenate([ukv[..., :DN_B].reshape(DEPTH, KV_LORA, H_B * DN_B),
                             ukv[..., DN_B:].reshape(DEPTH, KV_LORA, H_B * DV_B)], axis=-1).astype(BF16)
    row = lambda a: a.reshape(DEPTH, 1, a.shape[-1])
    return dict(
        w_t=w_t.astype(BF16), w_gate=w_gate,
        b_gate=row(jnp.concatenate([b_gla_af, b_gla_ab], axis=-1)),
        w_uq=w_uq, w_ukv=w_ukv, w_out=w_out.astype(BF16),
        g_pre=g_pre, g_post=g_post, g_mla_q=g_mla_q, g_mla_kv=g_mla_kv,
        g_gla4=row(jnp.tile(g_gla, (1, H_A))), g_diff4=row(jnp.tile(g_diff, (1, H_C))),
        lam=jnp.stack([lam_q1, lam_k1, lam_q2, lam_k2], axis=1))


def _rope_tables(n):
    t = np.arange(n)
    row = (t // GRID_W).astype(np.float32)
    col = (t % GRID_W).astype(np.float32)
    half = ROPE_DIM // 2
    inv = (1.0 / (np.float32(ROPE_THETA) ** (np.arange(0, half, 2, dtype=np.float32) / np.float32(half)))
           ).astype(np.float32)
    ar = row[:, None] * inv
    ac = col[:, None] * inv
    ang = np.concatenate([ar, ar, ac, ac], axis=-1).astype(np.float32)
    return (jnp.asarray(np.tile(np.cos(ang), (1, 8)).astype(np.float32)),
            jnp.asarray(np.tile(np.sin(ang), (1, 8)).astype(np.float32)))


def _sublayer(x2d, mod, pw, layer, *, seq, rope_tabs, ctx, caches):
    sample = ctx is not None
    pre = _pre_call(x2d, mod, pw, layer, seq=seq, rope_tabs=rope_tabs,
                    caches=None if sample else caches[:4])
    qk, v, la, bsum, ga, gb, gc, mq, mk, mvt, dqk, dvt = pre[:12]
    if sample:
        (oa,) = _gla_call(qk, v, la, bsum, ga, pw["g_gla4"], layer, seq=seq, state_in=ctx["state"])
        new_caches = None
    else:
        oa, sfin = _gla_call(qk, v, la, bsum, ga, pw["g_gla4"], layer, seq=seq,
                             state_out=caches[4] if caches else None)
        new_caches = tuple(pre[12:]) + (sfin,)
    ob, oc = _attn_call(mq, mk, mvt, gb, dqk, dvt, gc, pw["lam"], pw["g_diff4"], ctx, layer, seq=seq)
    y = _post_call(oa, ob, oc, x2d, mod, pw, layer, seq=seq, sample=sample)
    return y, new_caches


def kernel(x_prompt, x_sample, c, cache_mla_ckv, cache_mla_kpe, cache_diff_k, cache_diff_v, state_gla,
           c_ctx, w_ada, b_ada, g_pre, g_post, w_in, w_gla_af, b_gla_af, w_gla_ab, b_gla_ab, g_gla,
           g_mla_q, w_mla_uq, g_mla_kv, w_mla_ukv, lam_q1, lam_k1, lam_q2, lam_k2, g_diff, w_out):
    bp, tp, d = x_prompt.shape
    bs, ts, _ = x_sample.shape

    pw = _pack_params(g_pre, g_post, w_in, w_gla_af, b_gla_af, w_gla_ab, b_gla_ab, g_gla, g_mla_q,
                      w_mla_uq, g_mla_kv, w_mla_ukv, lam_q1, lam_k1, lam_q2, lam_k2, g_diff, w_out)
    mod = _mod_call(c_ctx.reshape(1, d), c, w_ada, b_ada)
    rope_tabs = _rope_tables(ts)
    ctx_k, ctx_vt = _ctxkv_call(cache_mla_ckv, jnp.swapaxes(cache_mla_kpe, -1, -2), pw["w_ukv"])
    ctx = dict(state=state_gla, mla_k=ctx_k, mla_vt=ctx_vt,
               diff_k_t=jnp.swapaxes(cache_diff_k, -1, -2), diff_v_t=jnp.swapaxes(cache_diff_v, -1, -2))

    y_p = x_prompt.reshape(bp * tp, d)
    y_s = x_sample.reshape(bs * ts, d)
    caches = ()
    for l in range(DEPTH):
        y_p, caches = _sublayer(y_p, mod, pw, l, seq=tp, rope_tabs=None, ctx=None, caches=caches)
        y_s, _ = _sublayer(y_s, mod, pw, l, seq=ts, rope_tabs=rope_tabs, ctx=ctx, caches=None)
    ckvn, kpe_t, kc_t, vc_t, new_state = caches
    return (y_p.reshape(bp, tp, d), y_s.reshape(bs, ts, d), ckvn, jnp.swapaxes(kpe_t, -1, -2),
            jnp.swapaxes(kc_t, -1, -2), jnp.swapaxes(vc_t, -1, -2), new_state)
```

```python
import functools
import math

import numpy as np
import jax
import jax.numpy as jnp
from jax import lax
from jax.experimental import pallas as pl
from jax.experimental.pallas import tpu as pltpu

F32 = jnp.float32
BF16 = jnp.bfloat16

D_MODEL = 1024
DEPTH = 2
GRID_W = 64
EPS = 1e-6
ROPE_THETA = 10000.0
ROPE_DIM = 32
H_A, DK_A, DV_A = 4, 32, 64
GLA_LR = 16
GLA_TAU = 16.0
GLA_CHUNK = 64
H_B, DN_B, DR_B, DV_B = 8, 64, 32, 64
Q_LORA, KV_LORA = 256, 128
H_C, DC = 4, 32
W_A, W_B, W_C = H_A * DV_A, H_B * DV_B, H_C * 2 * DC
ST_R, ST_C = H_A * DV_A, H_A * DK_A
LOG2E = math.log2(math.e)
SUBLANES = 8

R_QKV = ((0, 512),)
R_GG = ((544, 800),)
R_MLA = ((800, 1216), (512, 544), 64)
R_MG = ((1216, 1728),)
R_DIFF = ((1728, 2752),)
MOD_ROWS = 8

V7X_VMEM_LIMIT_BYTES = 56 * 1024 * 1024
TOKEN_BLOCK = 1024
POST_BLOCK = 1024
ATT_QBLOCK = 512
ATT_GROUP_ROWS = 1024
GLA_GROUP_ROWS = 1024
SOFTMAX_KEY_CHUNK = 256
ATT_STATIC_BLOCKS = 4


def _cparams(n_axes=1):
    return pltpu.CompilerParams(dimension_semantics=("arbitrary",) * n_axes,
                                vmem_limit_bytes=V7X_VMEM_LIMIT_BYTES)


def _rms(x, g):
    return x * lax.rsqrt(jnp.mean(x * x, axis=-1, keepdims=True) + EPS) * g


def _silu(x):
    return x * jax.nn.sigmoid(x)


def _log_sigmoid(x):
    return jnp.minimum(x, 0.0) - jnp.log1p(jnp.exp(-jnp.abs(x)))


def _rope(z, cos, sin):
    w = z.shape[-1]
    lane = lax.broadcasted_iota(jnp.int32, z.shape, 1)
    rot = jnp.where((lane & 15) < 8, -pltpu.roll(z, w - 8, 1), pltpu.roll(z, 8, 1))
    return z * cos + rot * sin


def _lane_mask(width, lo, size, dtype):
    lane = lax.broadcasted_iota(jnp.int32, (1, width), 1)
    return jnp.where((lane >= lo) & (lane < lo + size), 1.0, 0.0).astype(dtype)


def _dot(a, b):
    return jnp.dot(a, b, preferred_element_type=F32)


def _dot_nt(a, b):
    return lax.dot_general(a, b, (((1,), (1,)), ((), ())), preferred_element_type=F32)


def _dot_tn(a, b):
    return lax.dot_general(a, b, (((0,), (0,)), ((), ())), preferred_element_type=F32)


def _layer_spec(shape, layer):
    nd = len(shape)
    if nd == 2:
        return pl.BlockSpec(tuple(shape), lambda *_: (0, 0))
    return pl.BlockSpec((1,) + tuple(shape[1:]), lambda *_: (layer,) + (0,) * (nd - 1))


def _layer_row(ref, layer):
    return ref[layer:layer + 1, :]


_ANY = pl.BlockSpec(memory_space=pl.ANY)


def _mod_kernel(cctx_ref, c_ref, w_ref, b_ref, o_ref, rows_sc):
    nc = c_ref.shape[0]
    rows_sc[0:1, :] = cctx_ref[...]
    rows_sc[1:1 + nc, :] = c_ref[...]
    if 1 + nc < MOD_ROWS:
        rows_sc[1 + nc:, :] = jnp.zeros((MOD_ROWS - 1 - nc, D_MODEL), F32)
    s = _silu(rows_sc[...]).astype(BF16)
    layer = pl.program_id(0)
    bias = b_ref[0:1, :]
    for r in range(1, DEPTH):
        bias = jnp.where(layer == r, b_ref[r:r + 1, :], bias)
    mod = _dot(s, w_ref[0].astype(BF16)) + bias
    for r in range(MOD_ROWS):
        o_ref[r] = mod[r:r + 1, :]


def _mod_call(c_ctx, c, w_ada, b_ada):
    nb = 3 * D_MODEL // 2
    assert 1 + c.shape[0] <= MOD_ROWS
    return pl.pallas_call(
        _mod_kernel,
        grid=(DEPTH, 3 * D_MODEL // nb),
        in_specs=[pl.BlockSpec((1, D_MODEL), lambda l, j: (0, 0)),
                  pl.BlockSpec(c.shape, lambda l, j: (0, 0)),
                  pl.BlockSpec((1, D_MODEL, nb), lambda l, j: (l, 0, j)),
                  pl.BlockSpec((DEPTH, nb), lambda l, j: (0, j))],
        out_specs=pl.BlockSpec((MOD_ROWS, 1, nb), lambda l, j: (l, 0, j)),
        out_shape=jax.ShapeDtypeStruct((DEPTH * MOD_ROWS, 1, 3 * D_MODEL), F32),
        scratch_shapes=[pltpu.VMEM((MOD_ROWS, D_MODEL), F32)],
        compiler_params=_cparams(2), name="adaln_mod",
    )(c_ctx, c, w_ada, b_ada)


def _ctxkv_kernel(ckv_ref, kpe_ref, w_ref, k_ref, vt_ref):
    for b in range(ckv_ref.shape[0]):
        kv = _dot(ckv_ref[b, 0].astype(BF16), w_ref[0])
        kpe4 = jnp.concatenate([kpe_ref[b, 0]] * 4, axis=0).T
        k_ref[0, b, :, 0:512] = kv[:, 0:512].astype(BF16)
        k_ref[0, b, :, 512:640] = kpe4.astype(BF16)
        vt_ref[0, b] = kv[:, 512:1024].T.astype(BF16)


def _ctxkv_call(cache_ckv, cache_kpe_t, wukv):
    nb, _, tc, _ = cache_ckv.shape
    return pl.pallas_call(
        _ctxkv_kernel,
        grid=(DEPTH,),
        in_specs=[pl.BlockSpec((nb, 1, tc, KV_LORA), lambda l: (0, l, 0, 0)),
                  pl.BlockSpec((nb, 1, DR_B, tc), lambda l: (0, l, 0, 0)),
                  pl.BlockSpec((1, KV_LORA, 1024), lambda l: (l, 0, 0))],
        out_specs=[pl.BlockSpec((1, nb, tc, 640), lambda l: (l, 0, 0, 0)),
                   pl.BlockSpec((1, nb, W_B, tc), lambda l: (l, 0, 0, 0))],
        out_shape=[jax.ShapeDtypeStruct((DEPTH, nb, tc, 640), BF16),
                   jax.ShapeDtypeStruct((DEPTH, nb, W_B, tc), BF16)],
        compiler_params=_cparams(), name="mla_ctx_kv",
    )(cache_ckv, cache_kpe_t, wukv)


def _pre_kernel(*refs, rope, ctx_out, alias_in, bpb, seq, layer):
    it = iter(refs)
    (x_ref, mod_ref, gpre_ref, w_ref, wg_ref, bg_ref, gq_ref, gkv_ref, wuq_ref,
     wukv_ref) = (next(it) for _ in range(10))
    if rope:
        cos_ref, sin_ref = next(it), next(it)
    for _ in range(alias_in):
        next(it)
    (qk_ref, v_ref, la_ref, bs_ref, ga_ref, gb_ref, gc_ref, mq_ref, mk_ref, mvt_ref, dqk_ref,
     dvt_ref) = (next(it) for _ in range(12))
    if ctx_out:
        ckvn_ref, kpe_ref, kc_ref, vc_ref = (next(it) for _ in range(4))

    d = D_MODEL
    shift = mod_ref[0, :, 0:d]
    scale = mod_ref[0, :, d:2 * d]
    h = (_rms(x_ref[...], _layer_row(gpre_ref, layer) * (1.0 + scale)) + shift).astype(BF16)
    def proj(group):
        parts = [jnp.zeros((r, d), BF16) if isinstance(r, int) else w_ref[0, r[0]:r[1], :] for r in group]
        return _dot_nt(h, parts[0] if len(parts) == 1 else jnp.concatenate(parts, axis=0))
    if rope:
        cos = cos_ref[...]
        sin = sin_ref[...]

    pg = proj(R_QKV)
    qk_ref[:, 0:128] = pg[:, 0:128] * (DK_A ** -0.5)
    qk_ref[:, 128:256] = pg[:, 128:256]
    v_ref[...] = pg[:, 256:512]
    pm = proj(R_MLA)
    tail = pm[:, 384:512]
    xg = _dot(tail.astype(BF16), wg_ref[0]) + bg_ref[0]
    la = _log_sigmoid(xg) * (1.0 / GLA_TAU)
    la_ref[...] = la
    bs_ref[:, 0:128] = _chunk_scan(la[:, 0:128], False)
    bs_ref[:, 128:256] = _chunk_scan(la[:, 128:256], True)
    ga_ref[...] = _silu(proj(R_GG)).astype(BF16)

    qall = _dot(_rms(pm[:, 0:256], _layer_row(gq_ref, layer)).astype(BF16), wuq_ref[0])
    q_pe = qall[:, 512:768]
    if rope:
        q_pe = _rope(q_pe, cos, sin)
    sb = (DN_B + DR_B) ** -0.5 * LOG2E
    mq_ref[:, 0:512] = (qall[:, 0:512] * sb).astype(BF16)
    mq_ref[:, 512:768] = (q_pe * sb).astype(BF16)
    ckvn = _rms(pm[:, 256:384], _layer_row(gkv_ref, layer))
    kvall = _dot(ckvn.astype(BF16), wukv_ref[0])
    lane = lax.broadcasted_iota(jnp.int32, tail.shape, 1)
    kpe4 = jnp.where(lane < DR_B, tail, 0.0)
    kpe4 = kpe4 + pltpu.roll(kpe4, DR_B, 1)
    kpe4 = kpe4 + pltpu.roll(kpe4, 2 * DR_B, 1)
    if rope:
        kpe4 = _rope(kpe4, cos[:, 0:128], sin[:, 0:128])
    mk_ref[:, 0:512] = kvall[:, 0:512].astype(BF16)
    mk_ref[:, 512:640] = kpe4.astype(BF16)
    mvt_ref[...] = kvall[:, 512:1024].T.astype(BF16)
    gb_ref[...] = _silu(proj(R_MG)).astype(BF16)

    pd = proj(R_DIFF)
    dq, dk, dv = pd[:, 0:256], pd[:, 256:512], pd[:, 512:768]
    if rope:
        dq = _rope(dq, cos, sin)
        dk = _rope(dk, cos, sin)
    dqk_ref[:, 0:256] = (dq * (DC ** -0.5 * LOG2E)).astype(BF16)
    dqk_ref[:, 256:512] = dk.astype(BF16)
    dv_t = dv.T
    dvt_ref[...] = dv_t.astype(BF16)
    gc_ref[...] = _silu(pd[:, 768:1024]).astype(BF16)
    if ctx_out:
        kpe_t = kpe4.T
        dk_t = dk.T
        for ref in () if alias_in else (ckvn_ref, kpe_ref, kc_ref, vc_ref):
            ref[:, 1:] = jnp.zeros(ref[:, 1:].shape, F32)
        for bb in range(bpb):
            rs = slice(bb * seq, (bb + 1) * seq)
            ckvn_ref[bb, 0] = ckvn[rs]
            kpe_ref[bb, 0] = kpe_t[0:DR_B, rs]
            kc_ref[bb, 0] = dk_t[:, rs].reshape(H_C, 2 * DC, seq)
            vc_ref[bb, 0] = dv_t[:, rs].reshape(H_C, 2 * DC, seq)


def _pre_call(x2d, mod, pw, layer, *, seq, rope_tabs, caches):
    n = x2d.shape[0]
    tm = min(TOKEN_BLOCK, n)
    bpb = max(tm // seq, 1)
    rope = rope_tabs is not None
    ctx_out = caches is not None
    steps_per_seq = max(seq // tm, 1)
    nbt = n // seq

    def mod_idx(i):
        return (layer * MOD_ROWS + ((i * tm) // seq + 1 if rope else 0), 0, 0)

    names = ["g_pre", "w_t", "w_gate", "b_gate", "g_mla_q", "g_mla_kv", "w_uq", "w_ukv"]
    in_specs = [pl.BlockSpec((tm, D_MODEL), lambda i: (i, 0)), pl.BlockSpec((1, 1, 3 * D_MODEL), mod_idx)]
    in_specs += [_layer_spec(pw[k].shape, layer) for k in names]
    args = [x2d, mod] + [pw[k] for k in names]
    if rope:
        in_specs += [pl.BlockSpec((tm, 256), lambda i: (i % steps_per_seq, 0))] * 2
        args += list(rope_tabs)
    outs = [(256, F32, False), (256, F32, False), (256, F32, False), (256, F32, False),
            (W_A, BF16, False), (W_B, BF16, False), (W_C, BF16, False),
            (768, BF16, False), (640, BF16, False), (W_B, BF16, True),
            (512, BF16, False), (W_C, BF16, True)]
    out_specs = [pl.BlockSpec((w, tm), lambda i: (0, i)) if tr else pl.BlockSpec((tm, w), lambda i: (i, 0))
                 for w, _, tr in outs]
    out_shape = [jax.ShapeDtypeStruct((w, n) if tr else (n, w), dt) for w, dt, tr in outs]
    widths = outs
    aliases = {}
    if ctx_out:
        assert caches or layer == 0
        nl = 1 if caches else DEPTH
        out_specs += [pl.BlockSpec((bpb, nl, seq, KV_LORA), lambda i: (i, layer, 0, 0)),
                      pl.BlockSpec((bpb, nl, DR_B, seq), lambda i: (i, layer, 0, 0)),
                      pl.BlockSpec((bpb, nl, H_C, 2 * DC, seq), lambda i: (i, layer, 0, 0, 0)),
                      pl.BlockSpec((bpb, nl, H_C, 2 * DC, seq), lambda i: (i, layer, 0, 0, 0))]
        out_shape += [jax.ShapeDtypeStruct((nbt, DEPTH, seq, KV_LORA), F32),
                      jax.ShapeDtypeStruct((nbt, DEPTH, DR_B, seq), F32),
                      jax.ShapeDtypeStruct((nbt, DEPTH, H_C, 2 * DC, seq), F32),
                      jax.ShapeDtypeStruct((nbt, DEPTH, H_C, 2 * DC, seq), F32)]
        for j, arr in enumerate(caches):
            aliases[len(args)] = len(widths) + j
            in_specs.append(_ANY)
            args.append(arr)
    return pl.pallas_call(
        functools.partial(_pre_kernel, rope=rope, ctx_out=ctx_out, alias_in=len(aliases), bpb=bpb, seq=seq,
                          layer=layer),
        grid=(n // tm,), in_specs=in_specs, out_specs=out_specs, out_shape=out_shape,
        input_output_aliases=aliases,
        compiler_params=_cparams(), name="pre_rope" if rope else "pre_ctx",
    )(*args)


_GLA_LEVELS = (1, 2, 4, 8, 16, 32)


def _gla_consts(rev):
    c = GLA_CHUNK
    row = lax.broadcasted_iota(jnp.int32, (c, 128), 0)
    pos = (c - 1 - row) if rev else row
    ri = lax.broadcasted_iota(jnp.int32, (c, H_A * c), 0)
    cj = lax.broadcasted_iota(jnp.int32, (c, H_A * c), 1) & (c - 1)
    pi = (c - 1 - ri) if rev else ri
    pj = (c - 1 - cj) if rev else cj
    x = pi ^ pj
    lvl = jnp.where(pi == pj, 0, -1)
    for kbit in range(6):
        lvl = jnp.where((pj < pi) & ((x >> kbit) == 1), kbit + 1, lvl)
    return pos, lvl


def _chunk_scan(x, rev):
    rows = x.shape[0]
    nt = rows // SUBLANES
    tiles_per_chunk = GLA_CHUNK // SUBLANES
    x3 = x.reshape(nt, SUBLANES, 128)
    sub = lax.broadcasted_iota(jnp.int32, x3.shape, 1)
    tile = lax.broadcasted_iota(jnp.int32, x3.shape, 0) & (tiles_per_chunk - 1)
    edge = 0 if rev else SUBLANES - 1
    s = 1
    while s < SUBLANES:
        if rev:
            x3 = x3 + jnp.where(sub < SUBLANES - s, pltpu.roll(x3, SUBLANES - s, 1), 0.0)
        else:
            x3 = x3 + jnp.where(sub >= s, pltpu.roll(x3, s, 1), 0.0)
        s *= 2
    s = 1
    while s < tiles_per_chunk:
        tot = jnp.broadcast_to(x3[:, edge:edge + 1, :], x3.shape)
        if rev:
            shifted = jnp.concatenate([tot[s:], tot[:s]], axis=0)
            x3 = x3 + jnp.where(tile < tiles_per_chunk - s, shifted, 0.0)
        else:
            shifted = jnp.concatenate([tot[nt - s:], tot[:nt - s]], axis=0)
            x3 = x3 + jnp.where(tile >= s, shifted, 0.0)
        s *= 2
    return x3.reshape(rows, 128)


def _gla_scores(q, k, la, b, pos, lvl, hm_bf, rev):
    c = GLA_CHUNK
    prv = pltpu.roll(la, c - 1 if rev else 1, 0)
    nxt = pltpu.roll(la, 1 if rev else c - 1, 0)
    lvl = lvl.astype(BF16)
    s_tot = jnp.where(lvl == 0, _dot_nt(q.astype(BF16), jnp.concatenate([k.astype(BF16)] * H_A, 0) * hm_bf
                                        ).astype(BF16), jnp.zeros((), BF16))
    for kbit, m in enumerate(_GLA_LEVELS):
        up = ((pos >> kbit) & 1) == 1
        if m == 1:
            e = jnp.where(up, la, 0.0)
        elif m == 2:
            c4 = pos & 3
            e = jnp.where(c4 == 0, nxt, jnp.where(c4 == 1, 0.0, jnp.where(c4 == 2, la, la + prv)))
        else:
            nblk = c // (2 * m)
            loc = m if rev else m - 1
            b3 = b.reshape(nblk, 2 * m, 128)
            ref = jnp.broadcast_to(b3[:, loc:loc + 1, :], (nblk, 2 * m, 128)).reshape(c, 128)
            dlt = b - ref
            e = jnp.where(up, dlt, -dlt)
        xm = (jnp.where(up, q, k) * jnp.exp(e)).astype(BF16)
        sm = _dot_nt(xm, jnp.concatenate([xm] * H_A, 0) * hm_bf)
        s_tot = jnp.where(lvl == kbit + 1, sm.astype(BF16), s_tot)
    return s_tot


def _gla_apply(s_tot, q, k, v, b, st_prev, hm_f32, vm_bf, rev):
    c = GLA_CHUNK
    vbd = jnp.concatenate([v] * H_A, 0) * vm_bf
    blast = b[0:1, :] if rev else b[c - 1:c, :]
    qbar = (q * jnp.exp(b)).astype(BF16)
    kdec = (k * jnp.exp(blast - b)).astype(BF16)
    o = _dot(s_tot, vbd) + _dot_nt(qbar, st_prev.astype(BF16))
    st_new = st_prev * jnp.exp(blast) + _dot_tn(v, kdec) * hm_f32
    return o, st_new


def _gla_kernel(*refs, seq, has_s0, alias_in, spb):
    it = iter(refs)
    qk_ref, v_ref, la_ref, b_ref, gate_ref, g_ref = (next(it) for _ in range(6))
    s0_ref = next(it) if has_s0 else None
    for _ in range(alias_in):
        next(it)
    oa_ref = next(it)
    sfin_ref = None if has_s0 else next(it)
    acc_sc, st_sc, blk_sc = next(it), next(it), next(it)

    c = GLA_CHUNK
    nc = seq // c
    acc_sc[...] = jnp.zeros_like(acc_sc)
    if not has_s0 and not alias_in:
        sfin_ref[:, 1:] = jnp.zeros(sfin_ref[:, 1:].shape, F32)

    hrow = lax.broadcasted_iota(jnp.int32, (H_A * c, 128), 0) // c
    hm_f32 = jnp.where(hrow == lax.broadcasted_iota(jnp.int32, (H_A * c, 128), 1) // DK_A, 1.0, 0.0)
    hm_bf = hm_f32.astype(BF16)
    vrow = lax.broadcasted_iota(jnp.int32, (H_A * c, H_A * DV_A), 0) // c
    vm_bf = jnp.where(vrow == lax.broadcasted_iota(jnp.int32, (H_A * c, H_A * DV_A), 1) // DV_A,
                      1.0, 0.0).astype(BF16)
    consts = (_gla_consts(False), _gla_consts(True))

    def run_sequence(bb):
        base = bb * seq
        for d in range(2):
            if has_s0:
                blk_sc[...] = jnp.zeros_like(blk_sc)
                for hh in range(H_A):
                    blk_sc[DK_A * hh:DK_A * (hh + 1), DV_A * hh:DV_A * (hh + 1)] = s0_ref[bb, 0, d, hh]
                st_sc[d] = blk_sc[...].T
            else:
                st_sc[d] = jnp.zeros((ST_R, ST_C), F32)

        def chunk_rows(n, d):
            cn = (nc - 1 - n) if d else n
            start = base + cn * c
            return pl.ds(start if isinstance(start, int) else pl.multiple_of(start, c), c)

        def scores(n):
            out = []
            for d in range(2):
                rows = chunk_rows(n, d)
                pos, lvl = consts[d]
                out.append(_gla_scores(qk_ref[rows, 0:128], qk_ref[rows, 128:256],
                                       la_ref[rows, 128 * d:128 * d + 128],
                                       b_ref[rows, 128 * d:128 * d + 128], pos, lvl, hm_bf, bool(d)))
            return tuple(out)

        def apply(n, s_both):
            for d in range(2):
                rows = chunk_rows(n, d)
                o, st_new = _gla_apply(s_both[d], qk_ref[rows, 0:128], qk_ref[rows, 128:256],
                                       v_ref[rows, :].astype(BF16), b_ref[rows, 128 * d:128 * d + 128],
                                       st_sc[d], hm_f32, vm_bf, bool(d))
                acc_sc[rows, :] = acc_sc[rows, :] + o
                st_sc[d] = st_new

        def body(n, s_cur):
            s_next = scores(n + 1)
            apply(n, s_cur)
            return s_next

        trips = nc - 1
        s_last = lax.fori_loop(0, trips, body, scores(0),
                               unroll=next(u for u in (15, 5, 3, 1) if trips % u == 0))
        apply(nc - 1, s_last)

        rs = slice(base, base + seq)
        first = lax.broadcasted_iota(jnp.int32, (seq, 128), 1) < DV_A
        for p in range(H_A // 2):
            cols = slice(128 * p, 128 * (p + 1))
            oa = acc_sc[rs, cols]
            sq = oa * oa
            s0 = jnp.sum(jnp.where(first, sq, 0.0), axis=-1, keepdims=True)
            s1 = jnp.sum(jnp.where(first, 0.0, sq), axis=-1, keepdims=True)
            inv = jnp.where(first, lax.rsqrt(s0 * (1.0 / DV_A) + EPS), lax.rsqrt(s1 * (1.0 / DV_A) + EPS))
            oa_ref[rs, cols] = (oa * inv * g_ref[0, :, cols] * gate_ref[rs, cols]).astype(BF16)
        if not has_s0:
            for d in range(2):
                blk_sc[...] = st_sc[d].T
                for hh in range(H_A):
                    sfin_ref[bb, 0, d, hh] = blk_sc[DK_A * hh:DK_A * (hh + 1), DV_A * hh:DV_A * (hh + 1)]

    for bb in range(spb):
        run_sequence(bb)


def _gla_call(qk, v, la, bsum, gate, g4, layer, *, seq, state_in=None, state_out=None):
    n = qk.shape[0]
    nb = n // seq
    has_s0 = state_in is not None
    assert has_s0 or state_out is not None or layer == 0
    spb = max(1, min(nb, GLA_GROUP_ROWS // seq))
    blk = lambda w: pl.BlockSpec((spb * seq, w), lambda i: (i, 0))
    nl = DEPTH if (not has_s0 and state_out is None) else 1
    st_spec = pl.BlockSpec((spb, nl, 2, H_A, DK_A, DV_A), lambda i: (i, layer, 0, 0, 0, 0))
    in_specs = [blk(256), blk(256), blk(256), blk(256), blk(W_A), _layer_spec(g4.shape, layer)]
    args = [qk, v, la, bsum, gate, g4]
    out_specs = [blk(W_A)]
    out_shape = [jax.ShapeDtypeStruct((n, W_A), BF16)]
    aliases = {}
    if has_s0:
        in_specs.append(st_spec)
        args.append(state_in)
    else:
        out_specs.append(st_spec)
        out_shape.append(jax.ShapeDtypeStruct((nb, DEPTH, 2, H_A, DK_A, DV_A), F32))
        if state_out is not None:
            aliases[len(args)] = 1
            in_specs.append(_ANY)
            args.append(state_out)
    return pl.pallas_call(
        functools.partial(_gla_kernel, seq=seq, has_s0=has_s0, alias_in=len(aliases), spb=spb),
        grid=(nb // spb,), in_specs=in_specs, out_specs=out_specs, out_shape=out_shape,
        input_output_aliases=aliases,
        scratch_shapes=[pltpu.VMEM((spb * seq, W_A), F32), pltpu.VMEM((2, ST_R, ST_C), F32),
                        pltpu.VMEM((ST_C, ST_R), F32)],
        compiler_params=_cparams(), name="gla_state" if has_s0 else "gla_ctx",
    )(*args)


def _softmax_t_pv(st, vt, ones_rows):
    dv, keys = vt.shape
    m = _col_reduce(st, jnp.max)
    kc = SOFTMAX_KEY_CHUNK if keys > SOFTMAX_KEY_CHUNK and keys % SOFTMAX_KEY_CHUNK == 0 else keys
    if ones_rows:
        vt = jnp.concatenate([vt, jnp.ones((16, keys), BF16)], axis=0)
    o, l = None, None
    for lo in range(0, keys, kc):
        e = jnp.exp2(st[lo:lo + kc] - m)
        part = _dot(vt[:, lo:lo + kc], e.astype(BF16))
        o = part if o is None else o + part
        if not ones_rows:
            ls = _col_reduce(e, jnp.sum)
            l = ls if l is None else l + ls
    if ones_rows:
        return o[:dv] * (1.0 / o[dv:dv + 1])
    return o * (1.0 / l)


def _pipelined_attention(score_fns, value_fns, depth):
    outs = []
    pending = [fn() for fn in score_fns[:depth]]
    for j, vfn in enumerate(value_fns):
        st = pending.pop(0)
        if j + depth < len(score_fns):
            pending.append(score_fns[j + depth]())
        outs.append(_softmax_t_pv(st, *vfn()))
    return outs


def _lookahead(n_keys):
    return 2 if n_keys >= 1024 else 3


def _col_reduce(x, op):
    rows, cols = x.shape
    part = 128 if rows % 128 == 0 and rows > 128 else rows
    if part != rows:
        x = op(x.reshape(rows // part, part, cols), axis=0)
    return op(x, axis=0, keepdims=True)


def _seqs_per_step(n, seq, ctx_len):
    if ctx_len or seq >= ATT_QBLOCK:
        return 1
    return max(1, min(n // seq, ATT_GROUP_ROWS // seq))


def _for_query_blocks(run, seq, spb):
    qb = min(ATT_QBLOCK, seq)
    nq = seq // qb
    if spb * nq <= ATT_STATIC_BLOCKS:
        run([(bb, slice(bb * seq + j * qb, bb * seq + (j + 1) * qb)) for bb in range(spb) for j in range(nq)])
    else:
        assert spb == 1

        def body(i, carry):
            run([(0, pl.ds(pl.multiple_of(i * qb, qb), qb))])
            return carry
        lax.fori_loop(0, nq, body, 0)


def _mla_part(q_ref, k_ref, vt_ref, gate_ref, ck_ref, cvt_ref, ob_ref, kk_sc, vt_sc, *, seq, ctx_len, spb):
    for bb in range(spb):
        rs = slice(bb * seq, (bb + 1) * seq)
        for p in range(H_B // 2):
            kk_sc[bb, p, 0:seq, 0:128] = k_ref[rs, 128 * p:128 * p + 128]
            kk_sc[bb, p, 0:seq, 128:256] = k_ref[rs, 512:640]
            if ctx_len:
                kk_sc[bb, p, seq:seq + ctx_len, 0:128] = ck_ref[0, 0, :, 128 * p:128 * p + 128]
                kk_sc[bb, p, seq:seq + ctx_len, 128:256] = ck_ref[0, 0, :, 512:640]
        vt_sc[bb, :, 0:seq] = vt_ref[:, rs]
        if ctx_len:
            vt_sc[bb, :, seq:seq + ctx_len] = cvt_ref[0, 0]

    def scores(bb, rows, h):
        p, hh = divmod(h, 2)
        qn = q_ref[rows, 128 * p:128 * p + 128] * _lane_mask(128, DN_B * hh, DN_B, BF16)
        qp = (q_ref[rows, 512 + 128 * (h // 4):512 + 128 * (h // 4) + 128]
              * _lane_mask(128, DR_B * (h % 4), DR_B, BF16))
        return _dot_nt(kk_sc[bb, p], jnp.concatenate([qn, qp], axis=-1))

    def items(blocks):
        its = [(bb, rows, h) for bb, rows in blocks for h in range(H_B)]

        def finish(outs):
            for j, (_, rows) in enumerate(blocks):
                ob = jnp.concatenate(outs[H_B * j:H_B * (j + 1)], axis=0).T
                ob_ref[rows, :] = (ob * gate_ref[rows, :]).astype(BF16)

        return ([functools.partial(scores, bb, rows, h) for bb, rows, h in its],
                [functools.partial(lambda bb, h: (vt_sc[bb, DV_B * h:DV_B * (h + 1), :], not ctx_len), bb, h)
                 for bb, _, h in its],
                finish)

    return items


def _diff_part(qk_ref, vt_ref, gate_ref, lam_ref, g_ref, ck_ref, cv_ref, oc_ref, k_sc, vt_sc, *,
               seq, ctx_len, lam_init, spb):
    lam = (jnp.exp(jnp.sum(lam_ref[0, 0:1, :] * lam_ref[0, 1:2, :], axis=-1, keepdims=True))
           - jnp.exp(jnp.sum(lam_ref[0, 2:3, :] * lam_ref[0, 3:4, :], axis=-1, keepdims=True)) + lam_init)
    dh = 2 * DC
    for bb in range(spb):
        rs = slice(bb * seq, (bb + 1) * seq)
        for p in range(H_C // 2):
            k_sc[bb, p, 0:seq, :] = qk_ref[rs, 256 + 128 * p:256 + 128 * p + 128]
            if ctx_len:
                pair_t = jnp.concatenate([ck_ref[0, 0, 2 * p], ck_ref[0, 0, 2 * p + 1]], axis=0)
                k_sc[bb, p, seq:seq + ctx_len, :] = pair_t.T.astype(BF16)
        vt_sc[bb, :, 0:seq] = vt_ref[:, rs]
        if ctx_len:
            for h in range(H_C):
                vt_sc[bb, dh * h:dh * (h + 1), seq:seq + ctx_len] = cv_ref[0, 0, h].astype(BF16)

    def scores(bb, rows, h, comp):
        p, hh = divmod(h, 2)
        qm = qk_ref[rows, 128 * p:128 * p + 128] * _lane_mask(128, dh * hh + DC * comp, DC, BF16)
        return _dot_nt(k_sc[bb, p], qm)

    def items(blocks):
        its = [(bb, rows, h, comp) for bb, rows in blocks for h in range(H_C) for comp in range(2)]

        def finish(o12):
            for j, (_, rows) in enumerate(blocks):
                outs = []
                for h in range(H_C):
                    o1, o2 = o12[2 * (H_C * j + h)], o12[2 * (H_C * j + h) + 1]
                    ot = o1 - lam * o2
                    outs.append(ot * lax.rsqrt(jnp.mean(ot * ot, axis=0, keepdims=True) + EPS))
                oc = jnp.concatenate(outs, axis=0).T
                oc_ref[rows, :] = (oc * g_ref[0] * (1.0 - lam_init) * gate_ref[rows, :]).astype(BF16)

        return ([functools.partial(scores, *item) for item in its],
                [functools.partial(lambda bb, h: (vt_sc[bb, dh * h:dh * (h + 1), :], True), bb, h)
                 for bb, _, h, _ in its],
                finish)

    return items


def _attn_kernel(*refs, seq, ctx_len, lam_init, spb, mla, diff):
    it = iter(refs)
    n_ctx = 2 if ctx_len else 0
    mla_in = [next(it) for _ in range(4 + n_ctx)] + [None] * (2 - n_ctx) if mla else None
    diff_in = [next(it) for _ in range(5 + n_ctx)] + [None] * (2 - n_ctx) if diff else None
    ob_ref = next(it) if mla else None
    oc_ref = next(it) if diff else None
    parts = []
    if mla:
        parts.append(_mla_part(*mla_in, ob_ref, next(it), next(it), seq=seq, ctx_len=ctx_len, spb=spb))
    if diff:
        parts.append(_diff_part(*diff_in, oc_ref, next(it), next(it), seq=seq, ctx_len=ctx_len,
                                lam_init=lam_init, spb=spb))

    def run(blocks):
        built = [part(blocks) for part in parts]
        outs = _pipelined_attention([f for b in built for f in b[0]], [f for b in built for f in b[1]],
                                    _lookahead(seq + ctx_len))
        lo = 0
        for score_fns, _, finish in built:
            finish(outs[lo:lo + len(score_fns)])
            lo += len(score_fns)

    _for_query_blocks(run, seq, spb)


def _attn_call(mq, mk, mvt, gate_b, dqk, dvt, gate_c, lamp, g4, ctx, layer, *, seq):
    n = mq.shape[0]
    ctx_len = 0 if ctx is None else ctx["mla_k"].shape[2]
    lam_init = 0.8 - 0.6 * math.exp(-0.3 * layer)
    spb = _seqs_per_step(n, seq, ctx_len)
    rows = spb * seq
    tk = seq + ctx_len
    rblk = lambda w: pl.BlockSpec((rows, w), lambda i: (i, 0))
    tblk = lambda w: pl.BlockSpec((w, rows), lambda i: (0, i))
    mla_specs, mla_args = [rblk(768), rblk(640), tblk(W_B), rblk(W_B)], [mq, mk, mvt, gate_b]
    diff_specs = [rblk(512), tblk(W_C), rblk(W_C), _layer_spec(lamp.shape, layer), _layer_spec(g4.shape, layer)]
    diff_args = [dqk, dvt, gate_c, lamp, g4]
    if ctx_len:
        mla_specs += [pl.BlockSpec((1, 1, ctx_len, 640), lambda i: (layer, i, 0, 0)),
                      pl.BlockSpec((1, 1, W_B, ctx_len), lambda i: (layer, i, 0, 0))]
        mla_args += [ctx["mla_k"], ctx["mla_vt"]]
        diff_specs += [pl.BlockSpec((1, 1, H_C, 2 * DC, ctx_len), lambda i: (i, layer, 0, 0, 0))] * 2
        diff_args += [ctx["diff_k_t"], ctx["diff_v_t"]]
    mla_out = (rblk(W_B), jax.ShapeDtypeStruct((n, W_B), BF16))
    diff_out = (rblk(W_C), jax.ShapeDtypeStruct((n, W_C), BF16))
    mla_scratch = [pltpu.VMEM((spb, H_B // 2, tk, 256), BF16), pltpu.VMEM((spb, W_B, tk), BF16)]
    diff_scratch = [pltpu.VMEM((spb, H_C // 2, tk, 128), BF16), pltpu.VMEM((spb, W_C, tk), BF16)]

    def call(mla, diff, name):
        outs = ([mla_out] if mla else []) + ([diff_out] if diff else [])
        return pl.pallas_call(
            functools.partial(_attn_kernel, seq=seq, ctx_len=ctx_len, lam_init=lam_init, spb=spb,
                              mla=mla, diff=diff),
            grid=(n // rows,),
            in_specs=(mla_specs if mla else []) + (diff_specs if diff else []),
            out_specs=[o[0] for o in outs], out_shape=[o[1] for o in outs],
            scratch_shapes=(mla_scratch if mla else []) + (diff_scratch if diff else []),
            compiler_params=_cparams(), name=name,
        )(*(mla_args if mla else []), *(diff_args if diff else []))

    if ctx_len:
        (ob,) = call(True, False, "mla_ctx")
        (oc,) = call(False, True, "diff_ctx")
        return ob, oc
    return call(True, True, "attn_self")


def _post_kernel(oa_ref, ob_ref, oc_ref, x_ref, mod_ref, w_ref, g_ref, y_ref, *, layer):
    mix = jnp.concatenate([oa_ref[...], ob_ref[...], oc_ref[...]], axis=-1)
    y_ref[...] = x_ref[...] + _rms(_dot(mix, w_ref[0]),
                                   _layer_row(g_ref, layer) * mod_ref[0, :, 2 * D_MODEL:3 * D_MODEL])


def _post_call(oa, ob, oc, x2d, mod, pw, layer, *, seq, sample):
    n = x2d.shape[0]
    tm = min(POST_BLOCK, seq if sample else n)
    blk = lambda w: pl.BlockSpec((tm, w), lambda i: (i, 0))

    def mod_idx(i):
        return (layer * MOD_ROWS + ((i * tm) // seq + 1 if sample else 0), 0, 0)

    return pl.pallas_call(
        functools.partial(_post_kernel, layer=layer),
        grid=(n // tm,),
        in_specs=[blk(W_A), blk(W_B), blk(W_C), blk(D_MODEL),
                  pl.BlockSpec((1, 1, 3 * D_MODEL), mod_idx),
                  _layer_spec(pw["w_out"].shape, layer), _layer_spec(pw["g_post"].shape, layer)],
        out_specs=blk(D_MODEL),
        out_shape=jax.ShapeDtypeStruct((n, D_MODEL), F32),
        compiler_params=_cparams(), name="post",
    )(oa, ob, oc, x2d, mod, pw["w_out"], pw["g_post"])


def _pack_params(g_pre, g_post, w_in, w_gla_af, b_gla_af, w_gla_ab, b_gla_ab, g_gla, g_mla_q, w_mla_uq,
                 g_mla_kv, w_mla_ukv, lam_q1, lam_k1, lam_q2, lam_k2, g_diff, w_out):
    w_t = jnp.swapaxes(w_in, 1, 2)
    assert w_t.shape[1] == R_DIFF[0][1]
    zg = jnp.zeros((DEPTH, GLA_LR, 128), F32)
    w_gate = jnp.concatenate([jnp.zeros((DEPTH, DR_B, 256), F32),
                              jnp.concatenate([w_gla_af, zg], axis=-1),
                              jnp.concatenate([zg, w_gla_ab], axis=-1),
                              jnp.zeros((DEPTH, 128 - DR_B - 2 * GLA_LR, 256), F32)], axis=1).astype(BF16)
    uq = w_mla_uq.reshape(DEPTH, Q_LORA, H_B, DN_B + DR_B)
    w_pe = uq[..., DN_B:].reshape(DEPTH, Q_LORA, H_B * DR_B)
    w_uq = jnp.concatenate([uq[..., :DN_B].reshape(DEPTH, Q_LORA, H_B * DN_B), w_pe], axis=-1).astype(BF16)
    ukv = w_mla_ukv.reshape(DEPTH, KV_LORA, H_B, DN_B + DV_B)
    w_ukv = jnp.concatenate([ukv[..., :DN_B].reshape(DEPTH, KV_LORA, H_B * DN_B),
                             ukv[..., DN_B:].reshape(DEPTH, KV_LORA, H_B * DV_B)], axis=-1).astype(BF16)
    row = lambda a: a.reshape(DEPTH, 1, a.shape[-1])
    return dict(
        w_t=w_t.astype(BF16), w_gate=w_gate,
        b_gate=row(jnp.concatenate([b_gla_af, b_gla_ab], axis=-1)),
        w_uq=w_uq, w_ukv=w_ukv, w_out=w_out.astype(BF16),
        g_pre=g_pre, g_post=g_post, g_mla_q=g_mla_q, g_mla_kv=g_mla_kv,
        g_gla4=row(jnp.tile(g_gla, (1, H_A))), g_diff4=row(jnp.tile(g_diff, (1, H_C))),
        lam=jnp.stack([lam_q1, lam_k1, lam_q2, lam_k2], axis=1))


def _rope_tables(n):
    t = np.arange(n)
    row = (t // GRID_W).astype(np.float32)
    col = (t % GRID_W).astype(np.float32)
    half = ROPE_DIM // 2
    inv = (1.0 / (np.float32(ROPE_THETA) ** (np.arange(0, half, 2, dtype=np.float32) / np.float32(half)))
           ).astype(np.float32)
    ar = row[:, None] * inv
    ac = col[:, None] * inv
    ang = np.concatenate([ar, ar, ac, ac], axis=-1).astype(np.float32)
    return (jnp.asarray(np.tile(np.cos(ang), (1, 8)).astype(np.float32)),
            jnp.asarray(np.tile(np.sin(ang), (1, 8)).astype(np.float32)))


def _sublayer(x2d, mod, pw, layer, *, seq, rope_tabs, ctx, caches):
    sample = ctx is not None
    pre = _pre_call(x2d, mod, pw, layer, seq=seq, rope_tabs=rope_tabs,
                    caches=None if sample else caches[:4])
    qk, v, la, bsum, ga, gb, gc, mq, mk, mvt, dqk, dvt = pre[:12]
    if sample:
        (oa,) = _gla_call(qk, v, la, bsum, ga, pw["g_gla4"], layer, seq=seq, state_in=ctx["state"])
        new_caches = None
    else:
        oa, sfin = _gla_call(qk, v, la, bsum, ga, pw["g_gla4"], layer, seq=seq,
                             state_out=caches[4] if caches else None)
        new_caches = tuple(pre[12:]) + (sfin,)
    ob, oc = _attn_call(mq, mk, mvt, gb, dqk, dvt, gc, pw["lam"], pw["g_diff4"], ctx, layer, seq=seq)
    y = _post_call(oa, ob, oc, x2d, mod, pw, layer, seq=seq, sample=sample)
    return y, new_caches


def kernel(x_prompt, x_sample, c, cache_mla_ckv, cache_mla_kpe, cache_diff_k, cache_diff_v, state_gla,
           c_ctx, w_ada, b_ada, g_pre, g_post, w_in, w_gla_af, b_gla_af, w_gla_ab, b_gla_ab, g_gla,
           g_mla_q, w_mla_uq, g_mla_kv, w_mla_ukv, lam_q1, lam_k1, lam_q2, lam_k2, g_diff, w_out):
    bp, tp, d = x_prompt.shape
    bs, ts, _ = x_sample.shape

    pw = _pack_params(g_pre, g_post, w_in, w_gla_af, b_gla_af, w_gla_ab, b_gla_ab, g_gla, g_mla_q,
                      w_mla_uq, g_mla_kv, w_mla_ukv, lam_q1, lam_k1, lam_q2, lam_k2, g_diff, w_out)
    mod = _mod_call(c_ctx.reshape(1, d), c, w_ada, b_ada)
    rope_tabs = _rope_tables(ts)
    ctx_k, ctx_vt = _ctxkv_call(cache_mla_ckv, jnp.swapaxes(cache_mla_kpe, -1, -2), pw["w_ukv"])
    ctx = dict(state=state_gla, mla_k=ctx_k, mla_vt=ctx_vt,
               diff_k_t=jnp.swapaxes(cache_diff_k, -1, -2), diff_v_t=jnp.swapaxes(cache_diff_v, -1, -2))

    y_p = x_prompt.reshape(bp * tp, d)
    y_s = x_sample.reshape(bs * ts, d)
    caches = ()
    for l in range(DEPTH):
        y_p, caches = _sublayer(y_p, mod, pw, l, seq=tp, rope_tabs=None, ctx=None, caches=caches)
        y_s, _ = _sublayer(y_s, mod, pw, l, seq=ts, rope_tabs=rope_tabs, ctx=ctx, caches=None)
    ckvn, kpe_t, kc_t, vc_t, new_state = caches
    return (y_p.reshape(bp, tp, d), y_s.reshape(bs, ts, d), ckvn, jnp.swapaxes(kpe_t, -1, -2),
            jnp.swapaxes(kc_t, -1, -2), jnp.swapaxes(vc_t, -1, -2), new_state)
```
